```python
import jax, jax.numpy as jnp
from jax import lax
import numpy as np

D_MODEL = 1024
BATCH = 8
SEQ = 8192
DEPTH = 1

HEAD_DIM = 64
SWA_Q_HEADS = 8
SWA_KV_HEADS = 2
SWA_GROUP = SWA_Q_HEADS // SWA_KV_HEADS
SWA_WIDTH = SWA_Q_HEADS * HEAD_DIM
SWA_KV_WIDTH = SWA_KV_HEADS * HEAD_DIM
FOX_HEADS = 8
FOX_WIDTH = FOX_HEADS * HEAD_DIM
WINDOW = 128
BLOCK = 128
ROPE_THETA = 10000.0
NORM_EPS = 1e-6
FORGET_BIAS_INIT = 2.0

IN_SPLITS = (SWA_WIDTH, SWA_KV_WIDTH, SWA_KV_WIDTH, SWA_WIDTH,
             FOX_WIDTH, FOX_WIDTH, FOX_WIDTH, FOX_HEADS, FOX_WIDTH,
             D_MODEL, D_MODEL)
IN_WIDTH = sum(IN_SPLITS)

kernel_name = "hybrid_swa_sink_fox_gated_block"


def rms_norm(x, g):
    xf = x.astype(jnp.float32)
    y = xf * lax.rsqrt(jnp.mean(xf * xf, axis=-1, keepdims=True) + NORM_EPS)
    return (y * g.astype(jnp.float32)).astype(x.dtype)


def rope_tables(positions):
    inv_freq = ROPE_THETA ** (-jnp.arange(0, HEAD_DIM, 2, dtype=jnp.float32) / HEAD_DIM)
    ang = positions.astype(jnp.float32)[..., None] * inv_freq
    ang = jnp.concatenate([ang, ang], axis=-1)[:, :, None, :]
    return jnp.cos(ang), jnp.sin(ang)


def apply_rope(t, cos, sin):
    half = HEAD_DIM // 2
    tf = t.astype(jnp.float32)
    rot = jnp.concatenate([-tf[..., half:], tf[..., :half]], axis=-1)
    return (tf * cos + rot * sin).astype(t.dtype)


def split_cols(p, sizes):
    offs = np.cumsum(sizes)[:-1].tolist()
    return jnp.split(p, offs, axis=-1)


def sliding_window_gqa_sinks(q, k, v, sinks):
    b, s = q.shape[:2]
    nb = s // BLOCK
    q = q.reshape(b, nb, BLOCK, SWA_KV_HEADS, SWA_GROUP, HEAD_DIM)
    k = k.reshape(b, nb, BLOCK, SWA_KV_HEADS, HEAD_DIM)
    v = v.reshape(b, nb, BLOCK, SWA_KV_HEADS, HEAD_DIM)
    pad_k = jnp.zeros_like(k[:, :1])
    pad_v = jnp.zeros_like(v[:, :1])
    k_band = jnp.concatenate([jnp.concatenate([pad_k, k[:, :-1]], axis=1), k], axis=2)
    v_band = jnp.concatenate([jnp.concatenate([pad_v, v[:, :-1]], axis=1), v], axis=2)
    logits = jnp.einsum('bnqkgd,bnskd->bnkgqs', q, k_band).astype(jnp.float32) * (HEAD_DIM ** -0.5)
    blk = jnp.arange(nb)[:, None, None]
    qpos = blk * BLOCK + jnp.arange(BLOCK)[None, :, None]
    kpos = (blk - 1) * BLOCK + jnp.arange(2 * BLOCK)[None, None, :]
    rel = qpos - kpos
    mask = (rel >= 0) & (rel < WINDOW) & (kpos >= 0)
    logits = jnp.where(mask[None, :, None, None], logits, -jnp.inf)
    sink = sinks.astype(jnp.float32).reshape(SWA_KV_HEADS, SWA_GROUP)[None, None, :, :, None, None]
    m = jnp.maximum(jnp.max(logits, axis=-1, keepdims=True), sink)
    p = jnp.exp(logits - m)
    denom = jnp.sum(p, axis=-1, keepdims=True) + jnp.exp(sink - m)
    out = jnp.einsum('bnkgqs,bnskd->bnqkgd', (p / denom).astype(v.dtype), v_band)
    return out.reshape(b, s, SWA_WIDTH)


def forgetting_attention(q, k, v, cum):
    b, s = q.shape[:2]
    nb = s // BLOCK
    q_blocks = q.reshape(b, nb, BLOCK, FOX_HEADS, HEAD_DIM).transpose(1, 0, 2, 3, 4)
    cq_blocks = cum.reshape(b, nb, BLOCK, FOX_HEADS).transpose(1, 0, 3, 2)
    ck = cum.transpose(0, 2, 1)[:, :, None, :]
    kpos = jnp.arange(s)
    scale = HEAD_DIM ** -0.5

    def one_block(args):
        qb, cqb, n = args
        logits = jnp.einsum('bqhd,bshd->bhqs', qb, k).astype(jnp.float32) * scale
        logits = logits + cqb[..., None] - ck
        qpos = n * BLOCK + jnp.arange(BLOCK)
        mask = kpos[None, :] <= qpos[:, None]
        logits = jnp.where(mask, logits, -jnp.inf)
        p = jax.nn.softmax(logits, axis=-1)
        return jnp.einsum('bhqs,bshd->bqhd', p.astype(v.dtype), v)

    out = lax.map(one_block, (q_blocks, cq_blocks, jnp.arange(nb)))
    return out.transpose(1, 0, 2, 3, 4).reshape(b, s, FOX_WIDTH)


def _fwd_setup_inputs(seed: int = 0) -> dict:
    key = jax.random.key(seed)
    ks = jax.random.split(key, 14)
    f32 = jnp.float32
    x = jax.random.normal(ks[0], (BATCH, SEQ, D_MODEL), f32)
    c = jax.random.normal(ks[1], (BATCH, D_MODEL), f32)
    positions = jnp.broadcast_to(jnp.arange(SEQ, dtype=jnp.int32), (BATCH, SEQ))
    w_ada = jax.random.normal(ks[2], (DEPTH, D_MODEL, 3 * D_MODEL), f32) * (0.5 * D_MODEL ** -0.5)
    b_ada = jax.random.normal(ks[3], (DEPTH, 3 * D_MODEL), f32) * 0.02
    g_norm = 1.0 + 0.05 * jax.random.normal(ks[4], (DEPTH, D_MODEL), f32)
    w_in = jax.random.normal(ks[5], (DEPTH, D_MODEL, IN_WIDTH), f32) * D_MODEL ** -0.5
    b_f = FORGET_BIAS_INIT + 0.1 * jax.random.normal(ks[6], (DEPTH, FOX_HEADS), f32)
    sinks = jax.random.normal(ks[7], (DEPTH, SWA_Q_HEADS), f32) * 0.5
    w_o_swa = jax.random.normal(ks[8], (DEPTH, SWA_WIDTH, D_MODEL), f32) * SWA_WIDTH ** -0.5
    w_o_fox = jax.random.normal(ks[9], (DEPTH, FOX_WIDTH, D_MODEL), f32) * FOX_WIDTH ** -0.5
    w_out = jax.random.normal(ks[10], (DEPTH, D_MODEL, D_MODEL), f32) * D_MODEL ** -0.5
    g_final = 1.0 + 0.05 * jax.random.normal(ks[11], (D_MODEL,), f32)
    return {"x": x, "c": c, "positions": positions, "w_ada": w_ada, "b_ada": b_ada,
            "g_norm": g_norm, "w_in": w_in, "b_f": b_f, "sinks": sinks,
            "w_o_swa": w_o_swa, "w_o_fox": w_o_fox, "w_out": w_out, "g_final": g_final}


def _fwd_reference(x, c, positions, w_ada, b_ada, g_norm, w_in, b_f, sinks, w_o_swa, w_o_fox, w_out, g_final):
    b, s, _ = x.shape
    cos, sin = rope_tables(positions)
    for l in range(DEPTH):
        ada = c @ w_ada[l] + b_ada[l]
        shift, scale, gate = jnp.split(ada, 3, axis=-1)
        h = rms_norm(x, g_norm[l]) * (1.0 + scale[:, None, :]) + shift[:, None, :]
        proj = h @ w_in[l]
        qa, ka, va, za, qb, kb, vb, fb, zb, ga, gb = split_cols(proj, IN_SPLITS)
        qa = apply_rope(qa.reshape(b, s, SWA_Q_HEADS, HEAD_DIM), cos, sin)
        ka = apply_rope(ka.reshape(b, s, SWA_KV_HEADS, HEAD_DIM), cos, sin)
        va = va.reshape(b, s, SWA_KV_HEADS, HEAD_DIM)
        att_a = sliding_window_gqa_sinks(qa, ka, va, sinks[l])
        y_a = (att_a * jax.nn.silu(za)) @ w_o_swa[l]
        log_f = jax.nn.log_sigmoid(fb.astype(jnp.float32) + b_f[l].astype(jnp.float32))
        cum = jnp.cumsum(log_f, axis=1)
        att_b = forgetting_attention(qb.reshape(b, s, FOX_HEADS, HEAD_DIM),
                                     kb.reshape(b, s, FOX_HEADS, HEAD_DIM),
                                     vb.reshape(b, s, FOX_HEADS, HEAD_DIM), cum)
        y_b = (att_b * jax.nn.silu(zb)) @ w_o_fox[l]
        merged = jax.nn.sigmoid(ga) * y_a + jax.nn.sigmoid(gb) * y_b
        x = x + gate[:, None, :] * (merged @ w_out[l])
    return rms_norm(x, g_final)


import jax as _jax
import jax.numpy as _jnp

TWIN_FORMAT = 'train_step'
FWD_PARAMS = ['x', 'c', 'positions', 'w_ada', 'b_ada', 'g_norm', 'w_in', 'b_f', 'sinks', 'w_o_swa', 'w_o_fox', 'w_out', 'g_final']
TWIN_WEIGHTS = ['w_ada', 'b_ada', 'g_norm', 'w_in', 'b_f', 'sinks', 'w_o_swa', 'w_o_fox', 'w_out', 'g_final']
TWIN_DIFF_INPUT = 'x'
TWIN_INPUTS = ['x', 'c', 'positions', 'w_ada', 'b_ada', 'g_norm', 'w_in', 'b_f', 'sinks', 'w_o_swa', 'w_o_fox', 'w_out', 'g_final', 'loss_target', 'm_w_ada', 'm_b_ada', 'm_g_norm', 'm_w_in', 'm_b_f', 'm_sinks', 'm_w_o_swa', 'm_w_o_fox', 'm_w_out', 'm_g_final', 'v_w_ada', 'v_b_ada', 'v_g_norm', 'v_w_in', 'v_b_f', 'v_sinks', 'v_w_o_swa', 'v_w_o_fox', 'v_w_out', 'v_g_final']
TWIN_OUTPUTS = ['loss', 'grad_x', 'grad_w_ada', 'grad_b_ada', 'grad_g_norm', 'grad_w_in', 'grad_b_f', 'grad_sinks', 'grad_w_o_swa', 'grad_w_o_fox', 'grad_w_out', 'grad_g_final', 'delta_w_ada', 'delta_b_ada', 'delta_g_norm', 'delta_w_in', 'delta_b_f', 'delta_sinks', 'delta_w_o_swa', 'delta_w_o_fox', 'delta_w_out', 'delta_g_final', 'new_m_w_ada', 'new_m_b_ada', 'new_m_g_norm', 'new_m_w_in', 'new_m_b_f', 'new_m_sinks', 'new_m_w_o_swa', 'new_m_w_o_fox', 'new_m_w_out', 'new_m_g_final', 'new_v_w_ada', 'new_v_b_ada', 'new_v_g_norm', 'new_v_w_in', 'new_v_b_f', 'new_v_sinks', 'new_v_w_o_swa', 'new_v_w_o_fox', 'new_v_w_out', 'new_v_g_final']
TWIN_LEAF_KINDS = {'loss': 'loss', 'grad_x': 'grad_x', 'grad_w_ada': 'grad_w', 'grad_b_ada': 'grad_w', 'grad_g_norm': 'grad_w', 'grad_w_in': 'grad_w', 'grad_b_f': 'grad_w', 'grad_sinks': 'grad_w', 'grad_w_o_swa': 'grad_w', 'grad_w_o_fox': 'grad_w', 'grad_w_out': 'grad_w', 'grad_g_final': 'grad_w', 'delta_w_ada': 'delta_w', 'delta_b_ada': 'delta_w', 'delta_g_norm': 'delta_w', 'delta_w_in': 'delta_w', 'delta_b_f': 'delta_w', 'delta_sinks': 'delta_w', 'delta_w_o_swa': 'delta_w', 'delta_w_o_fox': 'delta_w', 'delta_w_out': 'delta_w', 'delta_g_final': 'delta_w', 'new_m_w_ada': 'new_m', 'new_m_b_ada': 'new_m', 'new_m_g_norm': 'new_m', 'new_m_w_in': 'new_m', 'new_m_b_f': 'new_m', 'new_m_sinks': 'new_m', 'new_m_w_o_swa': 'new_m', 'new_m_w_o_fox': 'new_m', 'new_m_w_out': 'new_m', 'new_m_g_final': 'new_m', 'new_v_w_ada': 'new_v', 'new_v_b_ada': 'new_v', 'new_v_g_norm': 'new_v', 'new_v_w_in': 'new_v', 'new_v_b_f': 'new_v', 'new_v_sinks': 'new_v', 'new_v_w_o_swa': 'new_v', 'new_v_w_o_fox': 'new_v', 'new_v_w_out': 'new_v', 'new_v_g_final': 'new_v'}


def _forward(args):
    return _fwd_reference(*[args[k] for k in FWD_PARAMS])


def _output_shape():
    def fwd():
        inp = _fwd_setup_inputs(0)
        return _fwd_reference(*[inp[k] for k in FWD_PARAMS])
    out = _jax.eval_shape(fwd)
    return out.shape, out.dtype

N_MICROBATCH = 1
ADAM_LR = 0.001
ADAM_B1 = 0.9
ADAM_B2 = 0.999
ADAM_EPS = 1e-08
ADAM_WD = 0.01
ADAM_STEP = 10
PER_EXAMPLE_BATCH_AXIS = {'x': 0, 'c': 0, 'positions': 0, 'loss_target': 0}
SHARED_INPUTS = []
_WEIGHT_DTYPES = {'w_ada': _jnp.float32, 'b_ada': _jnp.float32, 'g_norm': _jnp.float32, 'w_in': _jnp.float32, 'b_f': _jnp.float32, 'sinks': _jnp.float32, 'w_o_swa': _jnp.float32, 'w_o_fox': _jnp.float32, 'w_out': _jnp.float32, 'g_final': _jnp.float32}
MOMENT_SCALE = {'w_ada': 9.968676e-02, 'b_ada': 9.779603e-02, 'g_norm': 7.773808e-02, 'w_in': 3.931028e-02, 'b_f': 1.210596e-01, 'sinks': 1.584981e-02, 'w_o_swa': 3.152468e-02, 'w_o_fox': 4.002276e-02, 'w_out': 5.094022e-02, 'g_final': 6.414996e+01}


def _to_microbatches(a, axis):
    t = _jnp.moveaxis(a, axis, 0)
    t = t.reshape((N_MICROBATCH, t.shape[0] // N_MICROBATCH) + t.shape[1:])
    return _jnp.moveaxis(t, 1, axis + 1)


def setup_inputs(seed: int = 0) -> dict:
    inp = _fwd_setup_inputs(seed)
    key = _jax.random.fold_in(_jax.random.key(seed), 7919)
    shape, _ = _output_shape()
    out = dict(inp)
    out["loss_target"] = _jax.random.normal(_jax.random.fold_in(key, 0), shape, _jnp.float32)
    for i, name in enumerate(TWIN_WEIGHTS):
        w = inp[name].astype(_jnp.float32)
        if MOMENT_SCALE is None:
            s = _jnp.sqrt(_jnp.mean(_jnp.square(w)) + 1e-30)
        else:
            s = MOMENT_SCALE[name]
        km, kv = _jax.random.split(_jax.random.fold_in(key, i + 1))
        out[name] = w
        out["m_" + name] = s * _jax.random.normal(km, w.shape, _jnp.float32)
        out["v_" + name] = (s * s) * _jax.random.uniform(kv, w.shape, _jnp.float32, 0.5, 1.5)
    if N_MICROBATCH > 1:
        for name, axis in PER_EXAMPLE_BATCH_AXIS.items():
            out[name] = _to_microbatches(out[name], axis)
    return {'x': out['x'], 'c': out['c'], 'positions': out['positions'], 'w_ada': out['w_ada'], 'b_ada': out['b_ada'], 'g_norm': out['g_norm'], 'w_in': out['w_in'], 'b_f': out['b_f'], 'sinks': out['sinks'], 'w_o_swa': out['w_o_swa'], 'w_o_fox': out['w_o_fox'], 'w_out': out['w_out'], 'g_final': out['g_final'], 'loss_target': out['loss_target'], 'm_w_ada': out['m_w_ada'], 'm_b_ada': out['m_b_ada'], 'm_g_norm': out['m_g_norm'], 'm_w_in': out['m_w_in'], 'm_b_f': out['m_b_f'], 'm_sinks': out['m_sinks'], 'm_w_o_swa': out['m_w_o_swa'], 'm_w_o_fox': out['m_w_o_fox'], 'm_w_out': out['m_w_out'], 'm_g_final': out['m_g_final'], 'v_w_ada': out['v_w_ada'], 'v_b_ada': out['v_b_ada'], 'v_g_norm': out['v_g_norm'], 'v_w_in': out['v_w_in'], 'v_b_f': out['v_b_f'], 'v_sinks': out['v_sinks'], 'v_w_o_swa': out['v_w_o_swa'], 'v_w_o_fox': out['v_w_o_fox'], 'v_w_out': out['v_w_out'], 'v_g_final': out['v_g_final']}


def _loss(weights, diff, rest, loss_target):
    with _jax.named_scope("forward"):
        args = {**rest, TWIN_DIFF_INPUT: diff, **{k: w.astype(_WEIGHT_DTYPES[k]) for k, w in weights.items()}}
        y = _forward(args)
    with _jax.named_scope("loss_head"):
        err = _jnp.square(y.astype(_jnp.float32) - loss_target)
        return 0.5 * _jnp.sum(_jnp.mean(err, axis=-1)) if err.ndim else 0.5 * err


def _adamw(w, g, m, v):
    m = ADAM_B1 * m + (1.0 - ADAM_B1) * g
    v = ADAM_B2 * v + (1.0 - ADAM_B2) * _jnp.square(g)
    m_hat = m / (1.0 - ADAM_B1 ** ADAM_STEP)
    v_hat = v / (1.0 - ADAM_B2 ** ADAM_STEP)
    delta = -ADAM_LR * (m_hat / (_jnp.sqrt(v_hat) + ADAM_EPS) + ADAM_WD * w)
    return delta, m, v


def reference(x, c, positions, w_ada, b_ada, g_norm, w_in, b_f, sinks, w_o_swa, w_o_fox, w_out, g_final, loss_target, m_w_ada, m_b_ada, m_g_norm, m_w_in, m_b_f, m_sinks, m_w_o_swa, m_w_o_fox, m_w_out, m_g_final, v_w_ada, v_b_ada, v_g_norm, v_w_in, v_b_f, v_sinks, v_w_o_swa, v_w_o_fox, v_w_out, v_g_final):
    given = dict(x=x, c=c, positions=positions, w_ada=w_ada, b_ada=b_ada, g_norm=g_norm, w_in=w_in, b_f=b_f, sinks=sinks, w_o_swa=w_o_swa, w_o_fox=w_o_fox, w_out=w_out, g_final=g_final, loss_target=loss_target, m_w_ada=m_w_ada, m_b_ada=m_b_ada, m_g_norm=m_g_norm, m_w_in=m_w_in, m_b_f=m_b_f, m_sinks=m_sinks, m_w_o_swa=m_w_o_swa, m_w_o_fox=m_w_o_fox, m_w_out=m_w_out, m_g_final=m_g_final, v_w_ada=v_w_ada, v_b_ada=v_b_ada, v_g_norm=v_g_norm, v_w_in=v_w_in, v_b_f=v_b_f, v_sinks=v_sinks, v_w_o_swa=v_w_o_swa, v_w_o_fox=v_w_o_fox, v_w_out=v_w_out, v_g_final=v_g_final)
    weights = {n: given[n] for n in TWIN_WEIGHTS}
    shared = {n: given[n] for n in SHARED_INPUTS}
    per_example = {n: given[n] for n in ['x', 'c', 'positions']}
    grad_fn = _jax.value_and_grad(_loss, argnums=(0, 1))

    def one_microbatch(ex, loss_target):
        ex = dict(ex)
        diff = ex.pop(TWIN_DIFF_INPUT)
        return grad_fn(weights, diff, {**shared, **ex}, loss_target)

    if N_MICROBATCH == 1:
        loss, (grad_w, grad_x) = one_microbatch(per_example, given["loss_target"])
    else:
        def body(carry, xs):
            loss_sum, grad_sum = carry
            l_k, (gw_k, gx_k) = one_microbatch(xs[0], xs[1])
            with _jax.named_scope("update"):
                return (loss_sum + l_k, _jax.tree.map(_jnp.add, grad_sum, gw_k)), gx_k

        init = (_jnp.zeros((), _jnp.float32), _jax.tree.map(_jnp.zeros_like, weights))
        (loss, grad_w), grad_x = _jax.lax.scan(body, init, (per_example, given["loss_target"]))
    with _jax.named_scope("update"):
        delta_w, new_m, new_v = {}, {}, {}
        for n in TWIN_WEIGHTS:
            delta_w[n], new_m[n], new_v[n] = _adamw(weights[n], grad_w[n], given["m_" + n], given["v_" + n])
    return (loss, grad_x, *[grad_w[n] for n in TWIN_WEIGHTS], *[delta_w[n] for n in TWIN_WEIGHTS],
            *[new_m[n] for n in TWIN_WEIGHTS], *[new_v[n] for n in TWIN_WEIGHTS])
```

```python
import functools

import numpy as np
import jax
import jax.numpy as jnp
from jax import lax
from jax.experimental import pallas as pl
from jax.experimental.pallas import tpu as pltpu

F32 = jnp.float32
BF16 = jnp.bfloat16
MESH = pl.DeviceIdType.MESH

D_MODEL = 1024
HEAD_DIM = 64
N_HEADS = 8
WINDOW = 128
NORM_EPS = 1e-6
SCALE = HEAD_DIM ** -0.5
NEG = -1e30
LANES = 128
SUBLANES = 8
VMEM_LIMIT = 60 * 1024 * 1024

W_A, W_B, W_F, W_G = 768, 1536, 128, 3072
OFF_A, OFF_B, OFF_F, OFF_G = 0, 768, 2304, 2432
W_INT = W_A + W_B + W_F + W_G
R_ZA, R_QB, R_FB, R_ZB, R_END = 768, 1280, 2816, 2824, 5384

ADAM_LR, ADAM_B1, ADAM_B2, ADAM_EPS, ADAM_WD, ADAM_STEP = 0.001, 0.9, 0.999, 1e-08, 0.01, 10

NT = (((1,), (1,)), ((), ()))
TN = (((0,), (0,)), ((), ()))


def _dot(a, b, dims=None):
    if dims is None:
        return jnp.dot(a, b, preferred_element_type=F32)
    return lax.dot_general(a, b, dims, preferred_element_type=F32)


def _split3(v):
    hi = v.astype(BF16)
    r1 = v - hi.astype(F32)
    mid = r1.astype(BF16)
    lo = (r1 - mid.astype(F32)).astype(BF16)
    return hi, mid, lo


def _sigmoid(v):
    return 1.0 / (1.0 + jnp.exp(-v))


def _params(sem=None, vmem=None):
    return pltpu.CompilerParams(dimension_semantics=sem, vmem_limit_bytes=vmem)


def _const_spec(shape):
    nd = len(shape)
    return pl.BlockSpec(shape, lambda *_: (0,) * nd, pipeline_mode=pl.Buffered(1))


def _flip(v, f):
    return 1 - v if f else v


def _allgather_small(v, name):
    r, n = v.shape

    def body(v_ref, out_ref, send_sems, recv_sems):
        x, y, c = lax.axis_index("x"), lax.axis_index("y"), lax.axis_index("c")
        me = 4 * x + 2 * y + c
        out_ref[me] = v_ref[...]
        peers = []
        for k in range(1, 8):
            peers.append((_flip(x, k & 4), _flip(y, k & 2), _flip(c, k & 1)))
        sends = []
        for k, peer in enumerate(peers):
            cp = pltpu.make_async_remote_copy(
                src_ref=v_ref, dst_ref=out_ref.at[me], send_sem=send_sems.at[k], recv_sem=recv_sems.at[k],
                device_id=peer, device_id_type=MESH)
            cp.start()
            sends.append(cp)
        for k, peer in enumerate(peers):
            src = 4 * peer[0] + 2 * peer[1] + peer[2]
            pltpu.make_async_remote_copy(
                src_ref=v_ref, dst_ref=out_ref.at[src], send_sem=send_sems.at[k], recv_sem=recv_sems.at[k],
                device_id=peer, device_id_type=MESH).wait_recv()
        for cp in sends:
            cp.wait_send()

    return pl.pallas_call(
        body, name=name,
        out_shape=jax.ShapeDtypeStruct((8, r, n), F32),
        in_specs=[pl.BlockSpec(memory_space=pltpu.VMEM)],
        out_specs=pl.BlockSpec(memory_space=pltpu.VMEM),
        scratch_shapes=[pltpu.SemaphoreType.DMA((7,)), pltpu.SemaphoreType.DMA((7,))],
    )(v)


_CHIP_FLIPS = ((1, 0), (0, 1), (1, 1))


def _allgather_weights(shards, name):
    n = len(shards)

    def body(*refs):
        ins, outs = refs[:n], refs[n:2 * n]
        send_sems, recv_sems = refs[2 * n], refs[2 * n + 1]
        x, y, c = lax.axis_index("x"), lax.axis_index("y"), lax.axis_index("c")
        k_me = 2 * x + y
        sibling = (x, y, 1 - c)
        chips = [(_flip(x, fx), _flip(y, fy)) for fx, fy in _CHIP_FLIPS]

        def piece(i, chip_k, half):
            hr = ins[i].shape[0] // 2
            return outs[i].at[chip_k, pl.ds(half * hr, hr), :]

        def copy(i, slot, chip_k, half, to):
            return pltpu.make_async_remote_copy(
                src_ref=piece(i, chip_k, half), dst_ref=piece(i, chip_k, half),
                send_sem=send_sems.at[6 * i + slot], recv_sem=recv_sems.at[6 * i + slot],
                device_id=to, device_id_type=MESH)

        for i in range(n):
            outs[i][k_me] = ins[i][...].astype(BF16)
        started = []
        for i in range(n):
            for j, chip in enumerate(chips):
                cp = copy(i, j, k_me, c, (chip[0], chip[1], c))
                cp.start()
                started.append(cp)
        for j, chip in enumerate(chips):
            chip_k = 2 * chip[0] + chip[1]
            for i in range(n):
                copy(i, j, chip_k, c, (chip[0], chip[1], c)).wait_recv()
                cp = copy(i, 3 + j, chip_k, c, sibling)
                cp.start()
                started.append(cp)
        for j, chip in enumerate(chips):
            chip_k = 2 * chip[0] + chip[1]
            for i in range(n):
                copy(i, 3 + j, chip_k, 1 - c, sibling).wait_recv()
        for cp in started:
            cp.wait_send()

    return pl.pallas_call(
        body, name=name,
        out_shape=[jax.ShapeDtypeStruct((4,) + s.shape, BF16) for s in shards],
        in_specs=[pl.BlockSpec(memory_space=pltpu.VMEM)] * n,
        out_specs=[pl.BlockSpec(memory_space=pltpu.VMEM)] * n,
        scratch_shapes=[pltpu.SemaphoreType.DMA((6 * n,)), pltpu.SemaphoreType.DMA((6 * n,))],
        compiler_params=_params(vmem=VMEM_LIMIT),
    )(*shards)


def _reduce_scatter(pieces, name):
    n = len(pieces)

    def body(*refs):
        ins, outs = refs[:n], refs[n:2 * n]
        own, got = refs[2 * n:3 * n], refs[3 * n:4 * n]
        sendb, recvb = refs[4 * n:5 * n], refs[5 * n:6 * n]
        send_sems, recv_sems, local_sems = refs[6 * n:6 * n + 3]
        x, y, c = lax.axis_index("x"), lax.axis_index("y"), lax.axis_index("c")
        k_me = 2 * x + y
        sibling = (x, y, 1 - c)
        chips = [(_flip(x, fx), _flip(y, fy)) for fx, fy in _CHIP_FLIPS]
        hrs = [p.shape[1] // 2 for p in pieces]

        def remote(i, slot, src, dst, to):
            return pltpu.make_async_remote_copy(
                src_ref=src, dst_ref=dst, send_sem=send_sems.at[5 * i + slot], recv_sem=recv_sems.at[5 * i + slot],
                device_id=to, device_id_type=MESH)

        started = []
        loads = []
        for i in range(n):
            ld = pltpu.make_async_copy(ins[i].at[:, pl.ds(c * hrs[i], hrs[i]), :], own[i], local_sems.at[i])
            ld.start()
            loads.append(ld)
            cp = remote(i, 0, ins[i].at[:, pl.ds((1 - c) * hrs[i], hrs[i]), :], got[i], sibling)
            cp.start()
            started.append(cp)
        for i in range(n):
            loads[i].wait()
            remote(i, 0, ins[i].at[:, pl.ds(c * hrs[i], hrs[i]), :], got[i], sibling).wait_recv()
            for j, chip in enumerate(chips):
                chip_k = 2 * chip[0] + chip[1]
                sendb[i][j] = (own[i][chip_k] + got[i][chip_k]).astype(BF16)
                cp = remote(i, 1 + j, sendb[i].at[j], recvb[i].at[j], (chip[0], chip[1], c))
                cp.start()
                started.append(cp)
        for i in range(n):
            acc = own[i][k_me] + got[i][k_me]
            for j, chip in enumerate(chips):
                remote(i, 1 + j, sendb[i].at[j], recvb[i].at[j], (chip[0], chip[1], c)).wait_recv()
                acc = acc + recvb[i][j].astype(F32)
            mine = outs[i].at[pl.ds(c * hrs[i], hrs[i]), :]
            outs[i][pl.ds(pl.multiple_of(c * hrs[i], SUBLANES), hrs[i]), :] = acc
            cp = remote(i, 4, mine, mine, sibling)
            cp.start()
            started.append(cp)
        for i in range(n):
            theirs = outs[i].at[pl.ds((1 - c) * hrs[i], hrs[i]), :]
            remote(i, 4, theirs, theirs, sibling).wait_recv()
        for cp in started:
            cp.wait_send()

    scratch = []
    scratch += [pltpu.VMEM((4, p.shape[1] // 2, p.shape[2]), F32) for p in pieces]
    scratch += [pltpu.VMEM((4, p.shape[1] // 2, p.shape[2]), F32) for p in pieces]
    scratch += [pltpu.VMEM((3, p.shape[1] // 2, p.shape[2]), BF16) for p in pieces]
    scratch += [pltpu.VMEM((3, p.shape[1] // 2, p.shape[2]), BF16) for p in pieces]
    scratch += [pltpu.SemaphoreType.DMA((5 * n,)), pltpu.SemaphoreType.DMA((5 * n,)), pltpu.SemaphoreType.DMA((n,))]
    return pl.pallas_call(
        body, name=name,
        out_shape=[jax.ShapeDtypeStruct(p.shape[1:], F32) for p in pieces],
        in_specs=[pl.BlockSpec(memory_space=pl.ANY)] * n,
        out_specs=[pl.BlockSpec(memory_space=pltpu.VMEM)] * n,
        scratch_shapes=scratch,
        compiler_params=_params(vmem=VMEM_LIMIT),
    )(*pieces)


def _ada_part(c_all, w_shard, b_shard):
    def body(c_ref, w_ref, b_ref, o_ref):
        o_ref[...] = _dot(c_ref[...].astype(BF16), w_ref[...].astype(BF16)) + b_ref[...]

    return pl.pallas_call(
        body, name="ada_part",
        out_shape=jax.ShapeDtypeStruct((c_all.shape[0], w_shard.shape[1]), F32),
    )(c_all, w_shard, b_shard)


def _rope_fwd(t, cos, sin_lo, sin_hi):
    return t * cos + pltpu.roll(t, 96, 1) * sin_lo + pltpu.roll(t, 32, 1) * sin_hi


def _norm_proj(x, gmod, shift, w_int, cos, sin_lo, sin_hi, tm=512):
    s = x.shape[0]

    def body(x_ref, g_ref, sh_ref, w_ref, cos_ref, sl_ref, sh2_ref, a_ref, b_ref, f_ref, gg_ref, h_ref):
        xv = x_ref[...]
        r = lax.rsqrt(jnp.mean(xv * xv, axis=-1, keepdims=True) + NORM_EPS)
        hb = ((xv * r) * g_ref[...] + sh_ref[...]).astype(BF16)
        h_ref[...] = hb
        pa = _dot(hb, w_ref[:, OFF_A:OFF_A + W_A])
        cosv, sl, sh2 = cos_ref[...], sl_ref[...], sh2_ref[...]
        for j in range(5):
            t = pa[:, LANES * j:LANES * (j + 1)]
            a_ref[:, LANES * j:LANES * (j + 1)] = _rope_fwd(t, cosv, sl, sh2).astype(BF16)
        a_ref[:, 640:768] = pa[:, 640:768].astype(BF16)
        b_ref[...] = _dot(hb, w_ref[:, OFF_B:OFF_B + W_B]).astype(BF16)
        f_ref[...] = _dot(hb, w_ref[:, OFF_F:OFF_F + W_F])
        gg_ref[...] = _dot(hb, w_ref[:, OFF_G:OFF_G + W_G]).astype(BF16)

    row = lambda w: pl.BlockSpec((tm, w), lambda i: (i, 0))
    return pl.pallas_call(
        body, name="norm_proj", grid=(s // tm,),
        out_shape=[jax.ShapeDtypeStruct((s, W_A), BF16), jax.ShapeDtypeStruct((s, W_B), BF16),
                   jax.ShapeDtypeStruct((s, W_F), F32), jax.ShapeDtypeStruct((s, W_G), BF16),
                   jax.ShapeDtypeStruct((s, D_MODEL), BF16)],
        in_specs=[row(D_MODEL), _const_spec((1, D_MODEL)), _const_spec((1, D_MODEL)), _const_spec((D_MODEL, W_INT)),
                  row(LANES), row(LANES), row(LANES)],
        out_specs=[row(W_A), row(W_B), row(W_F), row(W_G), row(D_MODEL)],
        compiler_params=_params(("parallel",), VMEM_LIMIT),
    )(x, gmod, shift, w_int, cos, sin_lo, sin_hi)


def _log_sigmoid(u):
    return jnp.minimum(u, 0.0) - jnp.log(1.0 + jnp.exp(-jnp.abs(u)))


def _fox_cumsum(f, bf_pad, tb=256):
    s = f.shape[0]

    def body(f_ref, b_ref, cum_ref, carry):
        @pl.when(pl.program_id(0) == 0)
        def _():
            carry[...] = jnp.zeros_like(carry)

        lane = lax.broadcasted_iota(jnp.int32, (tb, LANES), 1)
        logf = jnp.where(lane < N_HEADS, _log_sigmoid(f_ref[...] + b_ref[...]), 0.0)
        hi, mid, lo = _split3(logf)
        rows = lax.broadcasted_iota(jnp.int32, (tb, tb), 0)
        cols = lax.broadcasted_iota(jnp.int32, (tb, tb), 1)
        tril = (cols <= rows).astype(BF16)
        cum = _dot(tril, hi) + _dot(tril, mid) + _dot(tril, lo) + carry[0:1, :]
        cum_ref[...] = cum
        carry[...] = jnp.broadcast_to(cum[tb - 1:tb, :], carry.shape)

    return pl.pallas_call(
        body, name="fox_cumsum", grid=(s // tb,),
        out_shape=jax.ShapeDtypeStruct((s, LANES), F32),
        in_specs=[pl.BlockSpec((tb, LANES), lambda i: (i, 0)), _const_spec((1, LANES))],
        out_specs=pl.BlockSpec((tb, LANES), lambda i: (i, 0)),
        scratch_shapes=[pltpu.VMEM((SUBLANES, LANES), F32)],
        compiler_params=_params(("arbitrary",)),
    )(f, bf_pad)


def _fox_prep(b, cum, tm=512):
    s = b.shape[0]

    def body(b_ref, cum_ref, q_ref, k_ref):
        lane = lax.broadcasted_iota(jnp.int32, (tm, LANES), 1)
        cumv = cum_ref[...]
        for h in range(N_HEADS):
            p, odd = h // 2, h % 2
            ch = jnp.broadcast_to(cumv[:, h:h + 1], (tm, LANES))
            hi, mid, lo = (t.astype(F32) for t in _split3(ch))
            qp = b_ref[:, LANES * p:LANES * (p + 1)].astype(F32)
            kp = b_ref[:, 512 + LANES * p:512 + LANES * (p + 1)].astype(F32)
            if odd:
                qp, kp = pltpu.roll(qp, 64, 1), pltpu.roll(kp, 64, 1)
            qa = jnp.where(lane < 64, qp * SCALE,
                           jnp.where(lane == 64, hi, jnp.where(lane == 65, mid, jnp.where(lane == 66, lo,
                           jnp.where(lane < 70, 1.0, 0.0)))))
            ka = jnp.where(lane < 64, kp,
                           jnp.where(lane < 67, 1.0, jnp.where(lane == 67, -hi, jnp.where(lane == 68, -mid,
                           jnp.where(lane == 69, -lo, 0.0)))))
            q_ref[:, LANES * h:LANES * (h + 1)] = qa.astype(BF16)
            k_ref[:, LANES * h:LANES * (h + 1)] = ka.astype(BF16)

    return pl.pallas_call(
        body, name="fox_prep", grid=(s // tm,),
        out_shape=[jax.ShapeDtypeStruct((s, 1024), BF16), jax.ShapeDtypeStruct((s, 1024), BF16)],
        in_specs=[pl.BlockSpec((tm, 1024), lambda i: (i, 0)), pl.BlockSpec((tm, LANES), lambda i: (i, 0))],
        out_specs=[pl.BlockSpec((tm, 1024), lambda i: (i, 0)), pl.BlockSpec((tm, 1024), lambda i: (i, 0))],
        compiler_params=_params(("parallel",)),
    )(b, cum)


def _fox_fwd(qa, ka, b, t=512):
    s = qa.shape[0]
    nt = s // t

    def body(q_ref, k_ref, v_ref, o_ref, lse_ref):
        i = pl.program_id(1)
        lane = lax.broadcasted_iota(jnp.int32, (t, LANES), 1)
        rows = lax.broadcasted_iota(jnp.int32, (t, t), 0)
        cols = lax.broadcasted_iota(jnp.int32, (t, t), 1)

        def tile(j, carry, diagonal):
            m0, l0, m1, l1, acc = carry
            off = pl.multiple_of(j * t, t)
            kt = k_ref[pl.ds(off, t), :]
            vt = v_ref[pl.ds(off, t), :]
            new = []
            for hh, (m, l) in enumerate(((m0, l0), (m1, l1))):
                sc = _dot(q_ref[:, LANES * hh:LANES * (hh + 1)], kt[:, LANES * hh:LANES * (hh + 1)], NT)
                if diagonal:
                    sc = jnp.where(cols <= rows, sc, NEG)
                m_new = jnp.maximum(m, jnp.max(sc, axis=-1, keepdims=True))
                alpha = jnp.exp(m - m_new)
                p = jnp.exp(sc - m_new)
                l_new = alpha * l + jnp.sum(p, axis=-1, keepdims=True)
                new.append((m_new, l_new, alpha, _dot(p.astype(BF16), vt)))
            acc = jnp.where(lane < 64, new[0][2] * acc + new[0][3], new[1][2] * acc + new[1][3])
            return new[0][0], new[0][1], new[1][0], new[1][1], acc

        col0 = jnp.full((t, 1), NEG, F32)
        zero = jnp.zeros((t, 1), F32)
        init = (col0, zero, col0, zero, jnp.zeros((t, LANES), F32))
        carry = lax.fori_loop(0, i, lambda j, cr: tile(j, cr, False), init)
        m0, l0, m1, l1, acc = tile(i, carry, True)
        o_ref[...] = (acc * jnp.where(lane < 64, 1.0 / l0, 1.0 / l1)).astype(BF16)
        sub = lax.broadcasted_iota(jnp.int32, (SUBLANES, t), 0)
        lse0 = jnp.broadcast_to(m0 + jnp.log(l0), (t, LANES)).T[0:SUBLANES, :]
        lse1 = jnp.broadcast_to(m1 + jnp.log(l1), (t, LANES)).T[0:SUBLANES, :]
        lse_ref[0] = jnp.where(sub == 0, lse0, jnp.where(sub == 1, lse1, 0.0))

    return pl.pallas_call(
        body, name="fox_fwd", grid=(4, nt),
        out_shape=[jax.ShapeDtypeStruct((s, 512), BF16), jax.ShapeDtypeStruct((4, SUBLANES, s), F32)],
        in_specs=[pl.BlockSpec((t, 2 * LANES), lambda p, i: (i, p)),
                  pl.BlockSpec((s, 2 * LANES), lambda p, i: (0, p)),
                  pl.BlockSpec((s, LANES), lambda p, i: (0, 8 + p))],
        out_specs=[pl.BlockSpec((t, LANES), lambda p, i: (i, p)),
                   pl.BlockSpec((1, SUBLANES, t), lambda p, i: (p, 0, i))],
        compiler_params=_params(("parallel", "arbitrary"), VMEM_LIMIT),
    )(qa, ka, b)


def _dup_halves(blk, lane):
    f = blk.astype(F32)
    r = pltpu.roll(f, 64, 1)
    return jnp.where(lane < 64, f, r).astype(BF16), jnp.where(lane >= 64, f, r).astype(BF16)


def _swa_scores(qm, k_cur, k_prev, has_prev, rows, cols):
    s_c = _dot(qm, k_cur, NT) * SCALE
    s_p = _dot(qm, k_prev, NT) * SCALE
    s_c = jnp.where(cols <= rows, s_c, NEG)
    s_p = jnp.where((cols > rows) & has_prev, s_p, NEG)
    return s_c, s_p


def _swa_fwd(a, sinks):
    s = a.shape[0]
    nb = s // WINDOW

    def body(sink_ref, a_ref, ap_ref, o_ref, l_ref):
        i = pl.program_id(0)
        lane = lax.broadcasted_iota(jnp.int32, (WINDOW, LANES), 1)
        rows = lax.broadcasted_iota(jnp.int32, (WINDOW, WINDOW), 0)
        cols = lax.broadcasted_iota(jnp.int32, (WINDOW, WINDOW), 1)
        has_prev = i > 0
        k_cur = _dup_halves(a_ref[:, 512:640], lane)
        v_cur = _dup_halves(a_ref[:, 640:768], lane)
        k_prev = _dup_halves(ap_ref[:, 512:640], lane)
        v_prev = _dup_halves(ap_ref[:, 640:768], lane)
        l_all = jnp.zeros((WINDOW, LANES), F32)
        for pb in range(4):
            g = pb // 2
            qblk = a_ref[:, LANES * pb:LANES * (pb + 1)]
            outs = []
            for hh in range(2):
                h = 2 * pb + hh
                qm = jnp.where((lane < 64) if hh == 0 else (lane >= 64), qblk, jnp.zeros_like(qblk))
                s_c, s_p = _swa_scores(qm, k_cur[g], k_prev[g], has_prev, rows, cols)
                sink = sink_ref[h]
                m = jnp.maximum(jnp.maximum(jnp.max(s_c, axis=-1, keepdims=True),
                                            jnp.max(s_p, axis=-1, keepdims=True)), sink)
                p_c = jnp.exp(s_c - m)
                p_p = jnp.exp(s_p - m)
                den = jnp.sum(p_c, axis=-1, keepdims=True) + jnp.sum(p_p, axis=-1, keepdims=True) + jnp.exp(sink - m)
                inv = 1.0 / den
                outs.append(_dot((p_c * inv).astype(BF16), v_cur[g]) + _dot((p_p * inv).astype(BF16), v_prev[g]))
                l_all = jnp.where(lane == h, m + jnp.log(den), l_all)
            o_ref[:, LANES * pb:LANES * (pb + 1)] = jnp.where(lane < 64, outs[0], outs[1]).astype(BF16)
        l_ref[...] = l_all

    return pl.pallas_call(
        body, name="swa_fwd", grid=(nb,),
        out_shape=[jax.ShapeDtypeStruct((s, 512), BF16), jax.ShapeDtypeStruct((s, LANES), F32)],
        in_specs=[pl.BlockSpec(memory_space=pltpu.SMEM),
                  pl.BlockSpec((WINDOW, W_A), lambda i: (i, 0)),
                  pl.BlockSpec((WINDOW, W_A), lambda i: (jnp.maximum(i - 1, 0), 0))],
        out_specs=[pl.BlockSpec((WINDOW, 512), lambda i: (i, 0)), pl.BlockSpec((WINDOW, LANES), lambda i: (i, 0))],
        compiler_params=_params(("parallel",)),
    )(sinks, a, a)


def _mid(att_a, att_b, g, x, target, gate, g_final, wo_a, wo_b, w_out, tm=256):
    s = x.shape[0]
    nt = s // tm

    def body(aa_ref, ab_ref, g_ref, x_ref, t_ref, gate_ref, gf_ref, woa_ref, wob_ref, wout_ref,
             dx_ref, daa_ref, dab_ref, dg_ref, delta_ref, dwoa_ref, dwob_ref, dwout_ref, vec_ref,
             acc_gf, acc_gate, acc_loss):
        step = pl.program_id(0)

        @pl.when(step == 0)
        def _():
            dwoa_ref[...] = jnp.zeros_like(dwoa_ref)
            dwob_ref[...] = jnp.zeros_like(dwob_ref)
            dwout_ref[...] = jnp.zeros_like(dwout_ref)
            acc_gf[...] = jnp.zeros_like(acc_gf)
            acc_gate[...] = jnp.zeros_like(acc_gate)
            acc_loss[...] = jnp.zeros_like(acc_loss)

        def fold(v):
            return jnp.sum(v.reshape(tm // SUBLANES, SUBLANES, D_MODEL), axis=0)

        gate = gate_ref[...]
        gfin = gf_ref[...]
        branches = []
        for att_ref, z_off, wo_ref in ((aa_ref, 0, woa_ref), (ab_ref, 512, wob_ref)):
            att = att_ref[...].astype(F32)
            z = g_ref[:, z_off:z_off + 512].astype(F32)
            sz = _sigmoid(z)
            silu = z * sz
            u = (att * silu).astype(BF16)
            branches.append((att, z, sz, silu, u, _dot(u, wo_ref[...])))
        ga = g_ref[:, 1024:2048].astype(F32)
        gb = g_ref[:, 2048:3072].astype(F32)
        sga, sgb = _sigmoid(ga), _sigmoid(gb)
        y_a, y_b = branches[0][5], branches[1][5]
        mb = (sga * y_a + sgb * y_b).astype(BF16)
        o = _dot(mb, wout_ref[...])
        x2 = x_ref[...] + gate * o
        r2 = lax.rsqrt(jnp.mean(x2 * x2, axis=-1, keepdims=True) + NORM_EPS)
        xn2 = x2 * r2
        err = xn2 * gfin - t_ref[...]
        acc_loss[...] += fold(err * err)
        dy = err * (1.0 / D_MODEL)
        acc_gf[...] += fold(dy * xn2)
        dxn = dy * gfin
        dx2 = r2 * (dxn - xn2 * jnp.mean(dxn * xn2, axis=-1, keepdims=True))
        dx_ref[...] = dx2
        acc_gate[...] += fold(dx2 * o)
        d_o = (dx2 * gate).astype(BF16)
        dwout_ref[...] += _dot(mb, d_o, TN)
        dm = _dot(d_o, wout_ref[...], NT)
        dg_ref[:, 1024:2048] = (dm * y_a * sga * (1.0 - sga)).astype(BF16)
        dg_ref[:, 2048:3072] = (dm * y_b * sgb * (1.0 - sgb)).astype(BF16)
        for (att, z, sz, silu, u, _), sg, wo_ref, dwo_ref, datt_ref, z_off in (
                (branches[0], sga, woa_ref, dwoa_ref, daa_ref, 0), (branches[1], sgb, wob_ref, dwob_ref, dab_ref, 512)):
            dyb = (dm * sg).astype(BF16)
            dwo_ref[...] += _dot(u, dyb, TN)
            du = _dot(dyb, wo_ref[...], NT)
            datt = du * silu
            datt_ref[...] = datt.astype(BF16)
            dg_ref[:, z_off:z_off + 512] = (du * att * (sz * (1.0 + z * (1.0 - sz)))).astype(BF16)
            if z_off == 512:
                prod = datt * att
                hi = prod.astype(BF16)
                lo = (prod - hi.astype(F32)).astype(BF16)
                er = lax.broadcasted_iota(jnp.int32, (512, LANES), 0)
                ec = lax.broadcasted_iota(jnp.int32, (512, LANES), 1)
                e = (er // HEAD_DIM == ec).astype(BF16)
                delta = _dot(hi, e) + _dot(lo, e)
                delta_ref[...] = delta.T[0:SUBLANES, :]

        @pl.when(step == nt - 1)
        def _():
            sub = lax.broadcasted_iota(jnp.int32, (SUBLANES, D_MODEL), 0)
            dgf = jnp.sum(acc_gf[...], axis=0, keepdims=True)
            dgate = jnp.sum(acc_gate[...], axis=0, keepdims=True)
            loss = 0.5 * jnp.sum(acc_loss[...]) * (1.0 / D_MODEL)
            vec_ref[...] = jnp.where(sub == 0, dgf, jnp.where(sub == 1, dgate, jnp.where(sub == 2, loss, 0.0)))

    row = lambda w: pl.BlockSpec((tm, w), lambda i: (i, 0))
    return pl.pallas_call(
        body, name="mid", grid=(nt,),
        out_shape=[jax.ShapeDtypeStruct((s, D_MODEL), F32), jax.ShapeDtypeStruct((s, 512), BF16),
                   jax.ShapeDtypeStruct((s, 512), BF16), jax.ShapeDtypeStruct((s, W_G), BF16),
                   jax.ShapeDtypeStruct((SUBLANES, s), F32),
                   jax.ShapeDtypeStruct((512, D_MODEL), F32), jax.ShapeDtypeStruct((512, D_MODEL), F32),
                   jax.ShapeDtypeStruct((D_MODEL, D_MODEL), F32), jax.ShapeDtypeStruct((SUBLANES, D_MODEL), F32)],
        in_specs=[row(512), row(512), row(W_G), row(D_MODEL), row(D_MODEL),
                  _const_spec((1, D_MODEL)), _const_spec((1, D_MODEL)),
                  _const_spec((512, D_MODEL)), _const_spec((512, D_MODEL)), _const_spec((D_MODEL, D_MODEL))],
        out_specs=[row(D_MODEL), row(512), row(512), row(W_G),
                   pl.BlockSpec((SUBLANES, tm), lambda i: (0, i)),
                   pl.BlockSpec((512, D_MODEL), lambda i: (0, 0)), pl.BlockSpec((512, D_MODEL), lambda i: (0, 0)),
                   pl.BlockSpec((D_MODEL, D_MODEL), lambda i: (0, 0)), pl.BlockSpec((SUBLANES, D_MODEL), lambda i: (0, 0))],
        scratch_shapes=[pltpu.VMEM((SUBLANES, D_MODEL), F32)] * 3,
        compiler_params=_params(("arbitrary",), VMEM_LIMIT),
    )(att_a, att_b, g, x, target, gate, g_final, wo_a, wo_b, w_out)


def _rope_bwd(dt, cos, sin, lane):
    u = dt * sin
    lo = (lane % HEAD_DIM) < (HEAD_DIM // 2)
    return dt * cos + jnp.where(lo, pltpu.roll(u, 96, 1), -pltpu.roll(u, 32, 1))


def _swa_bwd(a, datt, l_all, sinks, cos, sin):
    s = a.shape[0]
    nb = s // WINDOW

    def body(sink_ref, a_ref, ap_ref, do_ref, l_ref, cos_ref, sin_ref, da_ref, ds_ref, halo):
        step = pl.program_id(0)
        blk = nb - 1 - step

        @pl.when(step == 0)
        def _():
            halo[...] = jnp.zeros_like(halo)
            ds_ref[...] = jnp.zeros_like(ds_ref)

        lane = lax.broadcasted_iota(jnp.int32, (WINDOW, LANES), 1)
        rows = lax.broadcasted_iota(jnp.int32, (WINDOW, WINDOW), 0)
        cols = lax.broadcasted_iota(jnp.int32, (WINDOW, WINDOW), 1)
        sub8 = lax.broadcasted_iota(jnp.int32, (SUBLANES, LANES), 0)
        lane8 = lax.broadcasted_iota(jnp.int32, (SUBLANES, LANES), 1)
        has_prev = blk > 0
        k_cur = _dup_halves(a_ref[:, 512:640], lane)
        v_cur = _dup_halves(a_ref[:, 640:768], lane)
        k_prev = _dup_halves(ap_ref[:, 512:640], lane)
        v_prev = _dup_halves(ap_ref[:, 640:768], lane)
        lv = l_ref[...]
        cosv, sinv = cos_ref[...], sin_ref[...]
        zero = jnp.zeros((WINDOW, LANES), F32)
        dk_c, dk_p, dv_c, dv_p = [zero, zero], [zero, zero], [zero, zero], [zero, zero]
        dsink = jnp.zeros((SUBLANES, LANES), F32)
        for pb in range(4):
            g = pb // 2
            qblk = a_ref[:, LANES * pb:LANES * (pb + 1)]
            doblk = do_ref[:, LANES * pb:LANES * (pb + 1)]
            dqs = []
            for hh in range(2):
                h = 2 * pb + hh
                sel = (lane < 64) if hh == 0 else (lane >= 64)
                qm = jnp.where(sel, qblk, jnp.zeros_like(qblk))
                dom = jnp.where(sel, doblk, jnp.zeros_like(doblk))
                s_c, s_p = _swa_scores(qm, k_cur[g], k_prev[g], has_prev, rows, cols)
                lh = lv[:, h:h + 1]
                p_c = jnp.exp(s_c - lh)
                p_p = jnp.exp(s_p - lh)
                dp_c = _dot(dom, v_cur[g], NT)
                dp_p = _dot(dom, v_prev[g], NT)
                delta = jnp.sum(p_c * dp_c + p_p * dp_p, axis=-1, keepdims=True)
                dsink = dsink + jnp.where((sub8 == 0) & (lane8 == h), -jnp.sum(jnp.exp(sink_ref[h] - lh) * delta), 0.0)
                ds_c = (p_c * (dp_c - delta)).astype(BF16)
                ds_p = (p_p * (dp_p - delta)).astype(BF16)
                dqs.append(_dot(ds_c, k_cur[g]) + _dot(ds_p, k_prev[g]))
                dk_c[g] = dk_c[g] + _dot(ds_c, qm, TN)
                dk_p[g] = dk_p[g] + _dot(ds_p, qm, TN)
                dv_c[g] = dv_c[g] + _dot(p_c.astype(BF16), dom, TN)
                dv_p[g] = dv_p[g] + _dot(p_p.astype(BF16), dom, TN)
            dq = jnp.where(lane < 64, dqs[0], dqs[1]) * SCALE
            da_ref[:, LANES * pb:LANES * (pb + 1)] = _rope_bwd(dq, cosv, sinv, lane).astype(BF16)

        def join(pair):
            return jnp.where(lane < 64, pair[0] + pltpu.roll(pair[0], 64, 1), pair[1] + pltpu.roll(pair[1], 64, 1))

        dk = join(dk_c) * SCALE + halo[:, 0:LANES]
        dv = join(dv_c) + halo[:, LANES:2 * LANES]
        da_ref[:, 512:640] = _rope_bwd(dk, cosv, sinv, lane).astype(BF16)
        da_ref[:, 640:768] = dv.astype(BF16)
        halo[:, 0:LANES] = join(dk_p) * SCALE
        halo[:, LANES:2 * LANES] = join(dv_p)
        ds_ref[...] += dsink

    rev = lambda w: pl.BlockSpec((WINDOW, w), lambda i: (nb - 1 - i, 0))
    return pl.pallas_call(
        body, name="swa_bwd", grid=(nb,),
        out_shape=[jax.ShapeDtypeStruct((s, W_A), BF16), jax.ShapeDtypeStruct((SUBLANES, LANES), F32)],
        in_specs=[pl.BlockSpec(memory_space=pltpu.SMEM), rev(W_A),
                  pl.BlockSpec((WINDOW, W_A), lambda i: (jnp.maximum(nb - 2 - i, 0), 0)),
                  rev(512), rev(LANES), rev(LANES), rev(LANES)],
        out_specs=[rev(W_A), pl.BlockSpec((SUBLANES, LANES), lambda i: (0, 0))],
        scratch_shapes=[pltpu.VMEM((WINDOW, 2 * LANES), F32)],
        compiler_params=_params(("arbitrary",)),
    )(sinks, a, a, datt, l_all, cos, sin)


def _fox_bwd(qa, ka, b, do, lse, delta, t=512):
    s = qa.shape[0]
    nt = s // t

    def body(q_ref, do_ref, lse_ref, dl_ref, k_ref, v_ref, dq_ref, dk_ref, dv_ref, dc_ref, dr_ref, dq_acc):
        p = pl.program_id(0)
        j = pl.program_id(1)

        @pl.when(j == 0)
        def _():
            dq_acc[...] = jnp.zeros_like(dq_acc)

        lane = lax.broadcasted_iota(jnp.int32, (t, LANES), 1)
        rows = lax.broadcasted_iota(jnp.int32, (t, t), 0)
        cols = lax.broadcasted_iota(jnp.int32, (t, t), 1)
        kt = k_ref[...]
        vt = v_ref[...]

        def tile(i, carry, diagonal):
            dk0, dk1, dv = carry
            off = pl.multiple_of(i * t, t)
            qt = q_ref[pl.ds(off, t), :]
            dot_ = do_ref[pl.ds(off, t), :]
            lse_t = lse_ref[0, :, pl.ds(off, t)]
            dl_t = dl_ref[0, :, pl.ds(off, t)]
            dks = []
            for hh in range(2):
                q = qt[:, LANES * hh:LANES * (hh + 1)]
                k = kt[:, LANES * hh:LANES * (hh + 1)]
                st = _dot(k, q, NT)
                if diagonal:
                    st = jnp.where(cols >= rows, st, NEG)
                pt = jnp.exp(st - lse_t[hh:hh + 1, :])
                dom = jnp.where((lane < 64) if hh == 0 else (lane >= 64), dot_, jnp.zeros_like(dot_))
                dv = dv + _dot(pt.astype(BF16), dom)
                dpt = _dot(vt, dom, NT)
                dst = (pt * (dpt - dl_t[hh:hh + 1, :])).astype(BF16)
                dks.append(_dot(dst, q))
                dq_acc[hh, pl.ds(off, t), :] += _dot(dst, k, TN)
            return dk0 + dks[0], dk1 + dks[1], dv

        zero = jnp.zeros((t, LANES), F32)
        carry = tile(j, (zero, zero, zero), True)
        dk0, dk1, dv = lax.fori_loop(j + 1, nt, lambda i, cr: tile(i, cr, False), carry)
        dk_ref[...] = jnp.where(lane < 64, dk0, pltpu.roll(dk1, 64, 1)).astype(BF16)
        dv_ref[...] = dv.astype(BF16)
        c0 = jnp.broadcast_to(dk0[:, 67:68], (t, LANES))
        c1 = jnp.broadcast_to(dk1[:, 67:68], (t, LANES))
        dc_ref[0] = jnp.where(lane == 2 * p, -c0, jnp.where(lane == 2 * p + 1, -c1, 0.0))

        @pl.when(j == nt - 1)
        def _():
            lane_s = lax.broadcasted_iota(jnp.int32, (s, LANES), 1)
            a0, a1 = dq_acc[0], dq_acc[1]
            dq_ref[...] = (jnp.where(lane_s < 64, a0, pltpu.roll(a1, 64, 1)) * SCALE).astype(BF16)
            r0 = jnp.broadcast_to(a0[:, 64:65], (s, LANES))
            r1 = jnp.broadcast_to(a1[:, 64:65], (s, LANES))
            dr_ref[0] = jnp.where(lane_s == 2 * p, r0, jnp.where(lane_s == 2 * p + 1, r1, 0.0))

    return pl.pallas_call(
        body, name="fox_bwd", grid=(4, nt),
        out_shape=[jax.ShapeDtypeStruct((s, 512), BF16), jax.ShapeDtypeStruct((s, 512), BF16),
                   jax.ShapeDtypeStruct((s, 512), BF16), jax.ShapeDtypeStruct((4, s, LANES), F32),
                   jax.ShapeDtypeStruct((4, s, LANES), F32)],
        in_specs=[pl.BlockSpec((s, 2 * LANES), lambda p, j: (0, p)),
                  pl.BlockSpec((s, LANES), lambda p, j: (0, p)),
                  pl.BlockSpec((1, SUBLANES, s), lambda p, j: (p, 0, 0)),
                  pl.BlockSpec((1, SUBLANES, s), lambda p, j: (p, 0, 0)),
                  pl.BlockSpec((t, 2 * LANES), lambda p, j: (j, p)),
                  pl.BlockSpec((t, LANES), lambda p, j: (j, 8 + p))],
        out_specs=[pl.BlockSpec((s, LANES), lambda p, j: (0, p)),
                   pl.BlockSpec((t, LANES), lambda p, j: (j, p)),
                   pl.BlockSpec((t, LANES), lambda p, j: (j, p)),
                   pl.BlockSpec((1, t, LANES), lambda p, j: (p, j, 0)),
                   pl.BlockSpec((1, s, LANES), lambda p, j: (p, 0, 0))],
        scratch_shapes=[pltpu.VMEM((2, s, LANES), F32)],
        compiler_params=_params(("parallel", "arbitrary"), VMEM_LIMIT),
    )(qa, do, lse, delta, ka, b)


def _fox_cumsum_bwd(dcum8, f, bf_pad, tb=256):
    s = f.shape[0]
    nb = s // tb

    def body(dc_ref, f_ref, b_ref, df_ref, db_ref, carry):
        step = pl.program_id(0)

        @pl.when(step == 0)
        def _():
            carry[...] = jnp.zeros_like(carry)
            db_ref[...] = jnp.zeros_like(db_ref)

        lane = lax.broadcasted_iota(jnp.int32, (tb, LANES), 1)
        dc = dc_ref[0]
        for k in range(1, 8):
            dc = dc + dc_ref[k]
        hi, mid, lo = _split3(dc)
        rows = lax.broadcasted_iota(jnp.int32, (tb, tb), 0)
        cols = lax.broadcasted_iota(jnp.int32, (tb, tb), 1)
        triu = (cols >= rows).astype(BF16)
        dlogf = _dot(triu, hi) + _dot(triu, mid) + _dot(triu, lo) + carry[0:1, :]
        carry[...] = jnp.broadcast_to(dlogf[0:1, :], carry.shape)
        u = f_ref[...] + b_ref[...]
        dfb = jnp.where(lane < N_HEADS, dlogf * _sigmoid(-u), 0.0)
        df_ref[...] = dfb.astype(BF16)
        sub = lax.broadcasted_iota(jnp.int32, (SUBLANES, LANES), 0)
        db_ref[...] += jnp.where(sub == 0, jnp.sum(dfb, axis=0, keepdims=True), 0.0)

    return pl.pallas_call(
        body, name="fox_cumsum_bwd", grid=(nb,),
        out_shape=[jax.ShapeDtypeStruct((s, LANES), BF16), jax.ShapeDtypeStruct((SUBLANES, LANES), F32)],
        in_specs=[pl.BlockSpec((8, tb, LANES), lambda i: (0, nb - 1 - i, 0)),
                  pl.BlockSpec((tb, LANES), lambda i: (nb - 1 - i, 0)), _const_spec((1, LANES))],
        out_specs=[pl.BlockSpec((tb, LANES), lambda i: (nb - 1 - i, 0)),
                   pl.BlockSpec((SUBLANES, LANES), lambda i: (0, 0))],
        scratch_shapes=[pltpu.VMEM((SUBLANES, LANES), F32)],
        compiler_params=_params(("arbitrary",)),
    )(dcum8, f, bf_pad)


def _dh_norm_bwd(d_a, d_b, d_f, d_g, w_t, x, dx2, gnorm, scale1, tm=256):
    s = x.shape[0]
    nt = s // tm

    def body(da_ref, db_ref, df_ref, dg_ref, w_ref, x_ref, dx2_ref, g_ref, sc_ref, gx_ref, vec_ref, a_sh, a_sc, a_g):
        step = pl.program_id(0)

        @pl.when(step == 0)
        def _():
            a_sh[...] = jnp.zeros_like(a_sh)
            a_sc[...] = jnp.zeros_like(a_sc)
            a_g[...] = jnp.zeros_like(a_g)

        def fold(v):
            return jnp.sum(v.reshape(tm // SUBLANES, SUBLANES, D_MODEL), axis=0)

        dh = (_dot(da_ref[...], w_ref[OFF_A:OFF_A + W_A, :]) + _dot(db_ref[...], w_ref[OFF_B:OFF_B + W_B, :])
              + _dot(df_ref[...], w_ref[OFF_F:OFF_F + W_F, :]) + _dot(dg_ref[...], w_ref[OFF_G:OFF_G + W_G, :]))
        xv = x_ref[...]
        r = lax.rsqrt(jnp.mean(xv * xv, axis=-1, keepdims=True) + NORM_EPS)
        xn = xv * r
        gn = g_ref[...]
        a_sh[...] += fold(dh)
        a_sc[...] += fold(dh * (xn * gn))
        dn1 = dh * sc_ref[...]
        a_g[...] += fold(dn1 * xn)
        dxn = dn1 * gn
        gx_ref[...] = dx2_ref[...] + r * (dxn - xn * jnp.mean(dxn * xn, axis=-1, keepdims=True))

        @pl.when(step == nt - 1)
        def _():
            sub = lax.broadcasted_iota(jnp.int32, (SUBLANES, D_MODEL), 0)
            v_sh = jnp.sum(a_sh[...], axis=0, keepdims=True)
            v_sc = jnp.sum(a_sc[...], axis=0, keepdims=True)
            v_g = jnp.sum(a_g[...], axis=0, keepdims=True)
            vec_ref[...] = jnp.where(sub == 0, v_sh, jnp.where(sub == 1, v_sc, jnp.where(sub == 2, v_g, 0.0)))

    row = lambda w: pl.BlockSpec((tm, w), lambda i: (i, 0))
    return pl.pallas_call(
        body, name="dh_norm_bwd", grid=(nt,),
        out_shape=[jax.ShapeDtypeStruct((s, D_MODEL), F32), jax.ShapeDtypeStruct((SUBLANES, D_MODEL), F32)],
        in_specs=[row(W_A), row(W_B), row(W_F), row(W_G), _const_spec((W_INT, D_MODEL)),
                  row(D_MODEL), row(D_MODEL), _const_spec((1, D_MODEL)), _const_spec((1, D_MODEL))],
        out_specs=[row(D_MODEL), pl.BlockSpec((SUBLANES, D_MODEL), lambda i: (0, 0))],
        scratch_shapes=[pltpu.VMEM((SUBLANES, D_MODEL), F32)] * 3,
        compiler_params=_params(("arbitrary",), VMEM_LIMIT),
    )(d_a, d_b, d_f, d_g, w_t, x, dx2, gnorm, scale1)


def _dw_in(h, d, name, tn, ts=512):
    s, n = d.shape
    ns = s // ts

    def body(h_ref, d_ref, o_ref, acc):
        k = pl.program_id(1)

        @pl.when(k == 0)
        def _():
            acc[...] = jnp.zeros_like(acc)

        acc[...] += _dot(h_ref[...], d_ref[...], TN)

        @pl.when(k == ns - 1)
        def _():
            o_ref[...] = acc[...]

    return pl.pallas_call(
        body, name=name, grid=(n // tn, ns),
        out_shape=jax.ShapeDtypeStruct((D_MODEL, n), F32),
        in_specs=[pl.BlockSpec((ts, D_MODEL), lambda jn, k: (k, 0)), pl.BlockSpec((ts, tn), lambda jn, k: (k, jn))],
        out_specs=pl.BlockSpec((D_MODEL, tn), lambda jn, k: (0, jn)),
        scratch_shapes=[pltpu.VMEM((D_MODEL, tn), F32)],
        compiler_params=_params(("parallel", "arbitrary"), VMEM_LIMIT),
    )(h, d)


def _small_grads(packs, c_t, dada_shard):
    def body(p_ref, ct_ref, da_ref, sum_ref, gw_ref):
        acc = p_ref[0]
        for dev in range(1, 8):
            acc = acc + p_ref[dev]
        sum_ref[...] = acc
        gw_ref[...] = jnp.dot(ct_ref[...], da_ref[...], preferred_element_type=F32, precision=lax.Precision.HIGHEST)

    return pl.pallas_call(
        body, name="small_grads",
        out_shape=[jax.ShapeDtypeStruct(packs.shape[1:], F32),
                   jax.ShapeDtypeStruct((c_t.shape[0], dada_shard.shape[1]), F32)],
    )(packs, c_t, dada_shard)


def _adamw(w, g, m, v, name):
    r, c = w.shape
    tr = 128 if r % 128 == 0 else r
    c1 = 1.0 / (1.0 - ADAM_B1 ** ADAM_STEP)
    c2 = 1.0 / (1.0 - ADAM_B2 ** ADAM_STEP)

    def body(w_ref, g_ref, m_ref, v_ref, d_ref, mo_ref, vo_ref):
        gv = g_ref[...]
        mn = ADAM_B1 * m_ref[...] + (1.0 - ADAM_B1) * gv
        vn = ADAM_B2 * v_ref[...] + (1.0 - ADAM_B2) * (gv * gv)
        mo_ref[...] = mn
        vo_ref[...] = vn
        d_ref[...] = -ADAM_LR * ((mn * c1) / (jnp.sqrt(vn * c2) + ADAM_EPS) + ADAM_WD * w_ref[...])

    spec = pl.BlockSpec((tr, c), lambda i: (i, 0))
    return pl.pallas_call(
        body, name=name, grid=(r // tr,),
        out_shape=[jax.ShapeDtypeStruct((r, c), F32)] * 3,
        in_specs=[spec] * 4, out_specs=[spec] * 3,
        compiler_params=_params(("parallel",)),
    )(w, g, m, v)


def _rope_tables(positions):
    inv_freq = 10000.0 ** (-jnp.arange(0, HEAD_DIM, 2, dtype=F32) / HEAD_DIM)
    ang = positions.astype(F32)[:, None] * inv_freq
    ang = jnp.concatenate([ang, ang, ang, ang], axis=-1)
    cos, sin = jnp.cos(ang), jnp.sin(ang)
    lo = (jnp.arange(LANES) % HEAD_DIM) < (HEAD_DIM // 2)
    return cos, sin, jnp.where(lo, -sin, 0.0), jnp.where(lo, 0.0, sin)


def _pad_rows(v, rows=SUBLANES):
    return jnp.pad(v, ((0, rows - v.shape[0]), (0, 0)))


def kernel(x, c, positions, w_ada, b_ada, g_norm, w_in, b_f, sinks, w_o_swa, w_o_fox, w_out, g_final, loss_target, m_w_ada, m_b_ada, m_g_norm, m_w_in, m_b_f, m_sinks, m_w_o_swa, m_w_o_fox, m_w_out, m_g_final, v_w_ada, v_b_ada, v_g_norm, v_w_in, v_b_f, v_sinks, v_w_o_swa, v_w_o_fox, v_w_out, v_g_final):
    ix, iy, ic = lax.axis_index("x"), lax.axis_index("y"), lax.axis_index("c")
    chip = 2 * ix + iy
    dev = 2 * chip + ic
    xs, tgt = x[0], loss_target[0]
    s = xs.shape[0]

    c_all = _allgather_small(_pad_rows(c), "gather_c")[:, 0, :]
    b_ada_shard = lax.dynamic_slice(b_ada, (0, chip * 768), (1, 768))
    ada_all = _allgather_small(_ada_part(c_all, w_ada[0], b_ada_shard), "gather_ada")
    ada = lax.dynamic_index_in_dim(ada_all[::2], dev, axis=1, keepdims=False).reshape(1, 3 * D_MODEL)
    shift, scale, gate = ada[:, :D_MODEL], ada[:, D_MODEL:2 * D_MODEL], ada[:, 2 * D_MODEL:]
    scale1 = 1.0 + scale

    g_in, g_oa, g_ob, g_out = _allgather_weights([w_in[0], w_o_swa[0], w_o_fox[0], w_out[0]], "gather_weights")
    w_ref_order = jnp.transpose(g_in, (1, 0, 2)).reshape(D_MODEL, R_END)
    w_int = jnp.concatenate([
        w_ref_order[:, :R_ZA], w_ref_order[:, R_QB:R_FB], w_ref_order[:, R_FB:R_ZB],
        jnp.zeros((D_MODEL, W_F - N_HEADS), BF16), w_ref_order[:, R_ZA:R_QB], w_ref_order[:, R_ZB:]], axis=1)
    w_int_t = w_int.T
    wo_a = jnp.transpose(g_oa, (1, 0, 2)).reshape(512, D_MODEL)
    wo_b = jnp.transpose(g_ob, (1, 0, 2)).reshape(512, D_MODEL)
    w_o = g_out.reshape(D_MODEL, D_MODEL)

    cos, sin, sin_lo, sin_hi = _rope_tables(positions[0])
    bf_pad = jnp.pad(b_f, ((0, 0), (0, LANES - N_HEADS)))
    sink_vec = sinks[0]

    a, b, f, g, h = _norm_proj(xs, g_norm * scale1, shift, w_int, cos, sin_lo, sin_hi)
    att_a, l_swa = _swa_fwd(a, sink_vec)
    cum = _fox_cumsum(f, bf_pad)
    qa, ka = _fox_prep(b, cum)
    att_b, lse = _fox_fwd(qa, ka, b)

    dx2, datt_a, datt_b, d_g, delta8, dwo_a, dwo_b, dw_out, vec_mid = _mid(
        att_a, att_b, g, xs, tgt, gate, g_final.reshape(1, D_MODEL), wo_a, wo_b, w_o)
    delta = jnp.pad(delta8.reshape(4, 2, s), ((0, 0), (0, SUBLANES - 2), (0, 0)))
    d_a, dsink = _swa_bwd(a, datt_a, l_swa, sink_vec, cos, sin)
    dq, dk, dv, dcum_k, dcum_q = _fox_bwd(qa, ka, b, datt_b, lse, delta)
    d_f, dbf = _fox_cumsum_bwd(jnp.concatenate([dcum_k, dcum_q], axis=0), f, bf_pad)
    d_b = jnp.concatenate([dq, dk, dv], axis=1)
    grad_x, vec_dh = _dh_norm_bwd(d_a, d_b, d_f, d_g, w_int_t, xs, dx2, g_norm, scale1)
    dw_a = _dw_in(h, d_a, "dw_in_a", 768)
    dw_b = _dw_in(h, d_b, "dw_in_b", 768)
    dw_f = _dw_in(h, d_f, "dw_in_f", 128)
    dw_g = _dw_in(h, d_g, "dw_in_g", 768)
    dw_in = jnp.concatenate([dw_a, dw_g[:, :512], dw_b, dw_f[:, :N_HEADS], dw_g[:, 512:]], axis=1)

    tail = jnp.pad(jnp.concatenate([dbf[0:1, :N_HEADS], dsink[0:1, :N_HEADS]], axis=1), ((0, 0), (0, D_MODEL - 2 * N_HEADS)))
    pack = jnp.concatenate([c, vec_dh[0:2], vec_mid[1:2], vec_dh[2:3], vec_mid[0:1], tail, jnp.zeros((1, D_MODEL), F32)], axis=0)
    packs = _allgather_small(pack, "gather_small")
    dada_all = packs[:, 1:4, :].reshape(8, 3 * D_MODEL)
    dada_shard = lax.dynamic_slice(dada_all, (0, chip * 768), (8, 768))
    sums, g_w_ada = _small_grads(packs, packs[:, 0, :].T, dada_shard)
    g_b_ada = sums[1:4].reshape(1, 3 * D_MODEL)
    g_g_norm = sums[4:5]
    g_g_final = sums[5]
    g_b_f = sums[6:7, :N_HEADS]
    g_sinks = sums[6:7, N_HEADS:2 * N_HEADS]
    loss = lax.psum(vec_mid[2, 0], ("x", "y", "c"))

    def slots(w, axis):
        if axis == 1:
            return jnp.transpose(w.reshape(w.shape[0], 4, w.shape[1] // 4), (1, 0, 2))
        return w.reshape(4, w.shape[0] // 4, w.shape[1])

    (g_w_in,) = _reduce_scatter([slots(dw_in, 1)], "reduce_w_in")
    g_wo_a, g_wo_b, g_w_out = _reduce_scatter([slots(dwo_a, 1), slots(dwo_b, 1), slots(dw_out, 0)], "reduce_w_small")

    grads = {
        "w_ada": g_w_ada, "b_ada": g_b_ada, "g_norm": g_g_norm, "w_in": g_w_in, "b_f": g_b_f, "sinks": g_sinks,
        "w_o_swa": g_wo_a, "w_o_fox": g_wo_b, "w_out": g_w_out, "g_final": g_g_final,
    }
    params = {
        "w_ada": (w_ada, m_w_ada, v_w_ada), "b_ada": (b_ada, m_b_ada, v_b_ada), "g_norm": (g_norm, m_g_norm, v_g_norm),
        "w_in": (w_in, m_w_in, v_w_in), "b_f": (b_f, m_b_f, v_b_f), "sinks": (sinks, m_sinks, v_sinks),
        "w_o_swa": (w_o_swa, m_w_o_swa, v_w_o_swa), "w_o_fox": (w_o_fox, m_w_o_fox, v_w_o_fox),
        "w_out": (w_out, m_w_out, v_w_out), "g_final": (g_final, m_g_final, v_g_final),
    }
    names = list(grads)
    out_g, out_d, out_m, out_v = [], [], [], []
    for nm in names:
        w, m, v = params[nm]
        shape2 = (w.shape[-2], w.shape[-1]) if w.ndim >= 2 else (1, w.shape[0])
        d_, m_, v_ = _adamw(w.reshape(shape2), grads[nm].reshape(shape2), m.reshape(shape2), v.reshape(shape2), "adamw_" + nm)
        out_g.append(grads[nm].reshape(w.shape))
        out_d.append(d_.reshape(w.shape))
        out_m.append(m_.reshape(w.shape))
        out_v.append(v_.reshape(w.shape))
    return (loss, grad_x[None], *out_g, *out_d, *out_m, *out_v)
```

```python
import functools

import numpy as np
import jax
import jax.numpy as jnp
from jax import lax
from jax.experimental import pallas as pl
from jax.experimental.pallas import tpu as pltpu

F32 = jnp.float32
BF16 = jnp.bfloat16
MESH = pl.DeviceIdType.MESH

D_MODEL = 1024
HEAD_DIM = 64
N_HEADS = 8
WINDOW = 128
NORM_EPS = 1e-6
SCALE = HEAD_DIM ** -0.5
NEG = -1e30
LANES = 128
SUBLANES = 8
VMEM_LIMIT = 60 * 1024 * 1024
FOX_TILE = 256

W_A, W_B, W_F, W_G = 768, 1536, 128, 3072
OFF_A, OFF_B, OFF_F, OFF_G = 0, 768, 2304, 2432
W_INT = W_A + W_B + W_F + W_G
R_ZA, R_QB, R_FB, R_ZB, R_END = 768, 1280, 2816, 2824, 5384

ADAM_LR, ADAM_B1, ADAM_B2, ADAM_EPS, ADAM_WD, ADAM_STEP = 0.001, 0.9, 0.999, 1e-08, 0.01, 10

NT = (((1,), (1,)), ((), ()))
TN = (((0,), (0,)), ((), ()))


def _dot(a, b, dims=None):
    if dims is None:
        return jnp.dot(a, b, preferred_element_type=F32)
    return lax.dot_general(a, b, dims, preferred_element_type=F32)


def _split3(v):
    hi = v.astype(BF16)
    r1 = v - hi.astype(F32)
    mid = r1.astype(BF16)
    lo = (r1 - mid.astype(F32)).astype(BF16)
    return hi, mid, lo


def _sigmoid(v):
    return 1.0 / (1.0 + jnp.exp(-v))


def _params(sem=None, vmem=None):
    return pltpu.CompilerParams(dimension_semantics=sem, vmem_limit_bytes=vmem)


def _const_spec(shape):
    nd = len(shape)
    return pl.BlockSpec(shape, lambda *_: (0,) * nd, pipeline_mode=pl.Buffered(1))


def _flip(v, f):
    return 1 - v if f else v


def _allgather_small(v, name):
    r, n = v.shape

    def body(v_ref, out_ref, send_sems, recv_sems):
        x, y, c = lax.axis_index("x"), lax.axis_index("y"), lax.axis_index("c")
        me = 4 * x + 2 * y + c
        out_ref[me] = v_ref[...]
        peers = []
        for k in range(1, 8):
            peers.append((_flip(x, k & 4), _flip(y, k & 2), _flip(c, k & 1)))
        sends = []
        for k, peer in enumerate(peers):
            cp = pltpu.make_async_remote_copy(
                src_ref=v_ref, dst_ref=out_ref.at[me], send_sem=send_sems.at[k], recv_sem=recv_sems.at[k],
                device_id=peer, device_id_type=MESH)
            cp.start()
            sends.append(cp)
        for k, peer in enumerate(peers):
            src = 4 * peer[0] + 2 * peer[1] + peer[2]
            pltpu.make_async_remote_copy(
                src_ref=v_ref, dst_ref=out_ref.at[src], send_sem=send_sems.at[k], recv_sem=recv_sems.at[k],
                device_id=peer, device_id_type=MESH).wait_recv()
        for cp in sends:
            cp.wait_send()

    return pl.pallas_call(
        body, name=name,
        out_shape=jax.ShapeDtypeStruct((8, r, n), F32),
        in_specs=[pl.BlockSpec(memory_space=pltpu.VMEM)],
        out_specs=pl.BlockSpec(memory_space=pltpu.VMEM),
        scratch_shapes=[pltpu.SemaphoreType.DMA((7,)), pltpu.SemaphoreType.DMA((7,))],
    )(v)


_CHIP_FLIPS = ((1, 0), (0, 1), (1, 1))


def _allgather_weights(shards, name):
    n = len(shards)

    def body(*refs):
        ins, outs = refs[:n], refs[n:2 * n]
        send_sems, recv_sems = refs[2 * n], refs[2 * n + 1]
        x, y, c = lax.axis_index("x"), lax.axis_index("y"), lax.axis_index("c")
        k_me = 2 * x + y
        sibling = (x, y, 1 - c)
        chips = [(_flip(x, fx), _flip(y, fy)) for fx, fy in _CHIP_FLIPS]

        def piece(i, chip_k, half):
            hr = ins[i].shape[0] // 2
            return outs[i].at[chip_k, pl.ds(half * hr, hr), :]

        def copy(i, slot, chip_k, half, to):
            return pltpu.make_async_remote_copy(
                src_ref=piece(i, chip_k, half), dst_ref=piece(i, chip_k, half),
                send_sem=send_sems.at[6 * i + slot], recv_sem=recv_sems.at[6 * i + slot],
                device_id=to, device_id_type=MESH)

        for i in range(n):
            outs[i][k_me] = ins[i][...].astype(BF16)
        started = []
        for i in range(n):
            for j, chip in enumerate(chips):
                cp = copy(i, j, k_me, c, (chip[0], chip[1], c))
                cp.start()
                started.append(cp)
        for j, chip in enumerate(chips):
            chip_k = 2 * chip[0] + chip[1]
            for i in range(n):
                copy(i, j, chip_k, c, (chip[0], chip[1], c)).wait_recv()
                cp = copy(i, 3 + j, chip_k, c, sibling)
                cp.start()
                started.append(cp)
        for j, chip in enumerate(chips):
            chip_k = 2 * chip[0] + chip[1]
            for i in range(n):
                copy(i, 3 + j, chip_k, 1 - c, sibling).wait_recv()
        for cp in started:
            cp.wait_send()

    return pl.pallas_call(
        body, name=name,
        out_shape=[jax.ShapeDtypeStruct((4,) + s.shape, BF16) for s in shards],
        in_specs=[pl.BlockSpec(memory_space=pltpu.VMEM)] * n,
        out_specs=[pl.BlockSpec(memory_space=pltpu.VMEM)] * n,
        scratch_shapes=[pltpu.SemaphoreType.DMA((6 * n,)), pltpu.SemaphoreType.DMA((6 * n,))],
        compiler_params=_params(vmem=VMEM_LIMIT),
    )(*shards)


def _reduce_scatter(pieces, name):
    n = len(pieces)

    def body(*refs):
        ins, outs = refs[:n], refs[n:2 * n]
        own, got = refs[2 * n:3 * n], refs[3 * n:4 * n]
        sendb, recvb = refs[4 * n:5 * n], refs[5 * n:6 * n]
        send_sems, recv_sems, local_sems = refs[6 * n:6 * n + 3]
        x, y, c = lax.axis_index("x"), lax.axis_index("y"), lax.axis_index("c")
        k_me = 2 * x + y
        sibling = (x, y, 1 - c)
        chips = [(_flip(x, fx), _flip(y, fy)) for fx, fy in _CHIP_FLIPS]
        hrs = [p.shape[1] // 2 for p in pieces]

        def remote(i, slot, src, dst, to):
            return pltpu.make_async_remote_copy(
                src_ref=src, dst_ref=dst, send_sem=send_sems.at[5 * i + slot], recv_sem=recv_sems.at[5 * i + slot],
                device_id=to, device_id_type=MESH)

        started = []
        loads = []
        for i in range(n):
            ld = pltpu.make_async_copy(ins[i].at[:, pl.ds(c * hrs[i], hrs[i]), :], own[i], local_sems.at[i])
            ld.start()
            loads.append(ld)
            cp = remote(i, 0, ins[i].at[:, pl.ds((1 - c) * hrs[i], hrs[i]), :], got[i], sibling)
            cp.start()
            started.append(cp)
        for i in range(n):
            loads[i].wait()
            remote(i, 0, ins[i].at[:, pl.ds(c * hrs[i], hrs[i]), :], got[i], sibling).wait_recv()
            for j, chip in enumerate(chips):
                chip_k = 2 * chip[0] + chip[1]
                sendb[i][j] = (own[i][chip_k] + got[i][chip_k]).astype(BF16)
                cp = remote(i, 1 + j, sendb[i].at[j], recvb[i].at[j], (chip[0], chip[1], c))
                cp.start()
                started.append(cp)
        for i in range(n):
            acc = own[i][k_me] + got[i][k_me]
            for j, chip in enumerate(chips):
                remote(i, 1 + j, sendb[i].at[j], recvb[i].at[j], (chip[0], chip[1], c)).wait_recv()
                acc = acc + recvb[i][j].astype(F32)
            mine = outs[i].at[pl.ds(c * hrs[i], hrs[i]), :]
            outs[i][pl.ds(pl.multiple_of(c * hrs[i], SUBLANES), hrs[i]), :] = acc
            cp = remote(i, 4, mine, mine, sibling)
            cp.start()
            started.append(cp)
        for i in range(n):
            theirs = outs[i].at[pl.ds((1 - c) * hrs[i], hrs[i]), :]
            remote(i, 4, theirs, theirs, sibling).wait_recv()
        for cp in started:
            cp.wait_send()

    scratch = []
    scratch += [pltpu.VMEM((4, p.shape[1] // 2, p.shape[2]), F32) for p in pieces]
    scratch += [pltpu.VMEM((4, p.shape[1] // 2, p.shape[2]), F32) for p in pieces]
    scratch += [pltpu.VMEM((3, p.shape[1] // 2, p.shape[2]), BF16) for p in pieces]
    scratch += [pltpu.VMEM((3, p.shape[1] // 2, p.shape[2]), BF16) for p in pieces]
    scratch += [pltpu.SemaphoreType.DMA((5 * n,)), pltpu.SemaphoreType.DMA((5 * n,)), pltpu.SemaphoreType.DMA((n,))]
    return pl.pallas_call(
        body, name=name,
        out_shape=[jax.ShapeDtypeStruct(p.shape[1:], F32) for p in pieces],
        in_specs=[pl.BlockSpec(memory_space=pl.ANY)] * n,
        out_specs=[pl.BlockSpec(memory_space=pltpu.VMEM)] * n,
        scratch_shapes=scratch,
        compiler_params=_params(vmem=VMEM_LIMIT),
    )(*pieces)


def _ada_part(c_all, w_shard, b_shard):
    def body(c_ref, w_ref, b_ref, o_ref):
        o_ref[...] = _dot(c_ref[...].astype(BF16), w_ref[...].astype(BF16)) + b_ref[...]

    return pl.pallas_call(
        body, name="ada_part",
        out_shape=jax.ShapeDtypeStruct((c_all.shape[0], w_shard.shape[1]), F32),
    )(c_all, w_shard, b_shard)


def _rope_fwd(t, cos, sin_lo, sin_hi):
    return t * cos + pltpu.roll(t, 96, 1) * sin_lo + pltpu.roll(t, 32, 1) * sin_hi


def _norm_proj(x, gmod, shift, w_int, cos, sin_lo, sin_hi, tm=512):
    s = x.shape[0]

    def body(x_ref, g_ref, sh_ref, w_ref, cos_ref, sl_ref, sh2_ref, a_ref, b_ref, f_ref, gg_ref, h_ref):
        xv = x_ref[...]
        r = lax.rsqrt(jnp.mean(xv * xv, axis=-1, keepdims=True) + NORM_EPS)
        hb = ((xv * r) * g_ref[...] + sh_ref[...]).astype(BF16)
        h_ref[...] = hb
        pa = _dot(hb, w_ref[:, OFF_A:OFF_A + W_A])
        cosv, sl, sh2 = cos_ref[...], sl_ref[...], sh2_ref[...]
        for j in range(5):
            t = pa[:, LANES * j:LANES * (j + 1)]
            a_ref[:, LANES * j:LANES * (j + 1)] = _rope_fwd(t, cosv, sl, sh2).astype(BF16)
        a_ref[:, 640:768] = pa[:, 640:768].astype(BF16)
        b_ref[...] = _dot(hb, w_ref[:, OFF_B:OFF_B + W_B]).astype(BF16)
        f_ref[...] = _dot(hb, w_ref[:, OFF_F:OFF_F + W_F])
        gg_ref[...] = _dot(hb, w_ref[:, OFF_G:OFF_G + W_G]).astype(BF16)

    row = lambda w: pl.BlockSpec((tm, w), lambda i: (i, 0))
    return pl.pallas_call(
        body, name="norm_proj", grid=(s // tm,),
        out_shape=[jax.ShapeDtypeStruct((s, W_A), BF16), jax.ShapeDtypeStruct((s, W_B), BF16),
                   jax.ShapeDtypeStruct((s, W_F), F32), jax.ShapeDtypeStruct((s, W_G), BF16),
                   jax.ShapeDtypeStruct((s, D_MODEL), BF16)],
        in_specs=[row(D_MODEL), _const_spec((1, D_MODEL)), _const_spec((1, D_MODEL)), _const_spec((D_MODEL, W_INT)),
                  row(LANES), row(LANES), row(LANES)],
        out_specs=[row(W_A), row(W_B), row(W_F), row(W_G), row(D_MODEL)],
        compiler_params=_params(("parallel",), VMEM_LIMIT),
    )(x, gmod, shift, w_int, cos, sin_lo, sin_hi)


def _log_sigmoid(u):
    return jnp.minimum(u, 0.0) - jnp.log(1.0 + jnp.exp(-jnp.abs(u)))


def _fox_cumsum(f, bf_pad, tb=256):
    s = f.shape[0]

    def body(f_ref, b_ref, cum_ref, carry):
        @pl.when(pl.program_id(0) == 0)
        def _():
            carry[...] = jnp.zeros_like(carry)

        lane = lax.broadcasted_iota(jnp.int32, (tb, LANES), 1)
        logf = jnp.where(lane < N_HEADS, _log_sigmoid(f_ref[...] + b_ref[...]), 0.0)
        hi, mid, lo = _split3(logf)
        rows = lax.broadcasted_iota(jnp.int32, (tb, tb), 0)
        cols = lax.broadcasted_iota(jnp.int32, (tb, tb), 1)
        tril = (cols <= rows).astype(BF16)
        cum = _dot(tril, hi) + _dot(tril, mid) + _dot(tril, lo) + carry[0:1, :]
        cum_ref[...] = cum
        carry[...] = jnp.broadcast_to(cum[tb - 1:tb, :], carry.shape)

    return pl.pallas_call(
        body, name="fox_cumsum", grid=(s // tb,),
        out_shape=jax.ShapeDtypeStruct((s, LANES), F32),
        in_specs=[pl.BlockSpec((tb, LANES), lambda i: (i, 0)), _const_spec((1, LANES))],
        out_specs=pl.BlockSpec((tb, LANES), lambda i: (i, 0)),
        scratch_shapes=[pltpu.VMEM((SUBLANES, LANES), F32)],
        compiler_params=_params(("arbitrary",)),
    )(f, bf_pad)


def _fox_prep(b, cum, t):
    s = b.shape[0]

    def body(b_ref, cum_ref, q_ref, k_ref, v_ref, st_ref):
        lane = lax.broadcasted_iota(jnp.int32, (t, LANES), 1)
        sub8 = lax.broadcasted_iota(jnp.int32, (SUBLANES, LANES), 0)
        lane8 = lax.broadcasted_iota(jnp.int32, (SUBLANES, LANES), 1)
        cumv = cum_ref[...]
        stats = jnp.zeros((SUBLANES, LANES), F32)
        for h in range(N_HEADS):
            p, odd = h // 2, h % 2
            ch = jnp.broadcast_to(cumv[:, h:h + 1], (t, LANES))
            hi, mid, lo = (x.astype(F32) for x in _split3(ch))
            qp = b_ref[:, LANES * p:LANES * (p + 1)].astype(F32)
            kp = b_ref[:, 512 + LANES * p:512 + LANES * (p + 1)].astype(F32)
            vp = b_ref[:, 1024 + LANES * p:1024 + LANES * (p + 1)].astype(F32)
            if odd:
                qp, kp, vp = pltpu.roll(qp, 64, 1), pltpu.roll(kp, 64, 1), pltpu.roll(vp, 64, 1)
            qs = jnp.where(lane < 64, qp * SCALE, 0.0)
            ks = jnp.where(lane < 64, kp, 0.0)
            qa = jnp.where(lane < 64, qs,
                           jnp.where(lane == 64, hi, jnp.where(lane == 65, mid, jnp.where(lane == 66, lo,
                           jnp.where(lane < 70, 1.0, 0.0)))))
            ka = jnp.where(lane < 64, ks,
                           jnp.where(lane < 67, 1.0, jnp.where(lane == 67, -hi, jnp.where(lane == 68, -mid,
                           jnp.where(lane == 69, -lo, 0.0)))))
            q_ref[:, LANES * h:LANES * (h + 1)] = qa.astype(BF16)
            k_ref[:, LANES * h:LANES * (h + 1)] = ka.astype(BF16)
            v_ref[:, LANES * h:LANES * (h + 1)] = jnp.where(lane < 64, vp, jnp.where(lane == 64, 1.0, 0.0)).astype(BF16)
            qn = jnp.sqrt(jnp.max(jnp.sum(qs * qs, axis=-1, keepdims=True)))
            kn = jnp.sqrt(jnp.max(jnp.sum(ks * ks, axis=-1, keepdims=True)))
            csum = hi + mid + lo
            row = jnp.where(lane8 == 0, qn, jnp.where(lane8 == 1, kn, jnp.where(
                lane8 == 2, csum[0:1, :], jnp.where(lane8 == 3, csum[t - 1:t, :], 0.0))))
            stats = jnp.where(sub8 == h, row, stats)
        st_ref[0] = stats

    wide = pl.BlockSpec((t, 1024), lambda i: (i, 0))
    return pl.pallas_call(
        body, name="fox_prep", grid=(s // t,),
        out_shape=[jax.ShapeDtypeStruct((s, 1024), BF16)] * 3 + [jax.ShapeDtypeStruct((s // t, SUBLANES, LANES), F32)],
        in_specs=[pl.BlockSpec((t, W_B), lambda i: (i, 0)), pl.BlockSpec((t, LANES), lambda i: (i, 0))],
        out_specs=[wide, wide, wide, pl.BlockSpec((1, SUBLANES, LANES), lambda i: (i, 0, 0))],
        compiler_params=_params(("parallel",)),
    )(b, cum)


PRUNE_MARGIN = 100.0


def _fox_tile_ranges(stats):
    nt = stats.shape[0]
    ball = 2.0 * 1.01 * jnp.max(stats[:, :, 0], axis=0) * jnp.max(stats[:, :, 1], axis=0)
    d = ball[None, None, :] + stats[:, None, :, 2] - stats[None, :, :, 3]
    idx = jnp.arange(nt)
    skip = (d <= -PRUNE_MARGIN) & (idx[None, :, None] < idx[:, None, None])
    skip = skip.reshape(nt, nt, 4, 2).all(axis=-1)
    first_key = jnp.sum(skip, axis=1).astype(F32)
    needed = (~skip) & (idx[None, :, None] <= idx[:, None, None])
    n_query = jnp.sum(needed, axis=0).astype(F32)
    table = jnp.zeros((4, SUBLANES, LANES), F32)
    table = table.at[:, 0, :nt].set(first_key.T)
    table = table.at[:, 1, :nt].set(n_query.T)
    return table


def _lane_scalar(block, row, lane_idx):
    sub8 = lax.broadcasted_iota(jnp.int32, (SUBLANES, LANES), 0)
    lane8 = lax.broadcasted_iota(jnp.int32, (SUBLANES, LANES), 1)
    return jnp.sum(jnp.where((sub8 == row) & (lane8 == lane_idx), block, 0.0)).astype(jnp.int32)


def _fox_fwd(qa, ka, va, ranges, t):
    s = qa.shape[0]
    nt = s // t
    nc = t // LANES

    def body(rg_ref, q_ref, k_ref, v_ref, o_ref, lse_ref):
        i = pl.program_id(1)
        lane = lax.broadcasted_iota(jnp.int32, (t, LANES), 1)
        rows = lax.broadcasted_iota(jnp.int32, (t, t), 0)
        cols = lax.broadcasted_iota(jnp.int32, (t, t), 1)
        first = jnp.clip(_lane_scalar(rg_ref[0], 0, i), 0, i)

        def tile(j, carry, diagonal):
            off = pl.multiple_of(j * t, t)
            kt = k_ref[pl.ds(off, t), :]
            vt = v_ref[pl.ds(off, t), :]
            out = []
            for hh in range(2):
                m, acc = carry[2 * hh], carry[2 * hh + 1]
                sc = _dot(q_ref[:, LANES * hh:LANES * (hh + 1)], kt[:, LANES * hh:LANES * (hh + 1)], NT)
                if diagonal:
                    sc = jnp.where(cols <= rows, sc, NEG)
                part = sc[:, 0:LANES]
                for cch in range(1, nc):
                    part = jnp.maximum(part, sc[:, LANES * cch:LANES * (cch + 1)])
                m_new = jnp.maximum(m, jnp.max(part, axis=-1, keepdims=True))
                alpha = jnp.exp(m - m_new)
                p = jnp.exp(sc - m_new).astype(BF16)
                out += [m_new, alpha * acc + _dot(p, vt[:, LANES * hh:LANES * (hh + 1)])]
            return tuple(out)

        col0 = jnp.full((t, 1), NEG, F32)
        zero = jnp.zeros((t, LANES), F32)
        carry = lax.fori_loop(first, i, lambda j, cr: tile(j, cr, False), (col0, zero, col0, zero))
        m0, acc0, m1, acc1 = tile(i, carry, True)
        l0, l1 = acc0[:, 64:65], acc1[:, 64:65]
        o_ref[...] = jnp.where(lane < 64, acc0 * (1.0 / l0), pltpu.roll(acc1 * (1.0 / l1), 64, 1)).astype(BF16)
        sub = lax.broadcasted_iota(jnp.int32, (SUBLANES, t), 0)
        lse0 = jnp.broadcast_to(m0 + jnp.log(l0), (t, LANES)).T[0:SUBLANES, :]
        lse1 = jnp.broadcast_to(m1 + jnp.log(l1), (t, LANES)).T[0:SUBLANES, :]
        lse_ref[0] = jnp.where(sub == 0, lse0, jnp.where(sub == 1, lse1, 0.0))

    pair = pl.BlockSpec((s, 2 * LANES), lambda p, i: (0, p))
    return pl.pallas_call(
        body, name="fox_fwd", grid=(4, nt),
        out_shape=[jax.ShapeDtypeStruct((s, 512), BF16), jax.ShapeDtypeStruct((4, SUBLANES, s), F32)],
        in_specs=[pl.BlockSpec((1, SUBLANES, LANES), lambda p, i: (p, 0, 0)),
                  pl.BlockSpec((t, 2 * LANES), lambda p, i: (i, p)), pair, pair],
        out_specs=[pl.BlockSpec((t, LANES), lambda p, i: (i, p)),
                   pl.BlockSpec((1, SUBLANES, t), lambda p, i: (p, 0, i))],
        compiler_params=_params(("parallel", "arbitrary"), VMEM_LIMIT),
    )(ranges, qa, ka, va)


def _dup_halves(blk, lane):
    f = blk.astype(F32)
    r = pltpu.roll(f, 64, 1)
    return jnp.where(lane < 64, f, r).astype(BF16), jnp.where(lane >= 64, f, r).astype(BF16)


GROUP = 4
GROUP_ROWS = GROUP * WINDOW


def _stack_heads(ref, g, lane):
    parts = []
    for pb in (2 * g, 2 * g + 1):
        blk = ref[:, LANES * pb:LANES * (pb + 1)]
        zero = jnp.zeros_like(blk)
        parts += [jnp.where(lane < 64, blk, zero), jnp.where(lane >= 64, blk, zero)]
    return jnp.concatenate(parts, axis=0)


def _swa_band(a_ref, ap_ref, g, lane):
    k = jnp.concatenate([_dup_halves(ap_ref[:, 512:640], lane)[g], _dup_halves(a_ref[:, 512:640], lane)[g]], axis=0)
    v = jnp.concatenate([_dup_halves(ap_ref[:, 640:768], lane)[g], _dup_halves(a_ref[:, 640:768], lane)[g]], axis=0)
    return k, v


def _swa_logits(q, k, has_prev):
    sc = _dot(q, k, NT) * SCALE
    rr = lax.broadcasted_iota(jnp.int32, sc.shape, 0) % WINDOW
    cc = lax.broadcasted_iota(jnp.int32, sc.shape, 1)
    valid = (cc > rr) & (cc <= rr + WINDOW) & (has_prev | (cc >= WINDOW))
    return jnp.where(valid, sc, NEG)


def _per_head_column(values):
    return jnp.concatenate([jnp.broadcast_to(v, (WINDOW, 1)) for v in values], axis=0)


def _swa_fwd(a, sinks):
    s = a.shape[0]
    nb = s // WINDOW

    def body(sink_ref, a_ref, ap_ref, o_ref, l_ref):
        has_prev = pl.program_id(0) > 0
        lane = lax.broadcasted_iota(jnp.int32, (WINDOW, LANES), 1)
        l_all = jnp.zeros((WINDOW, LANES), F32)
        for g in range(2):
            k, v = _swa_band(a_ref, ap_ref, g, lane)
            sc = _swa_logits(_stack_heads(a_ref, g, lane), k, has_prev)
            sink = _per_head_column([sink_ref[GROUP * g + hh] for hh in range(GROUP)])
            m = jnp.maximum(jnp.max(sc, axis=-1, keepdims=True), sink)
            p = jnp.exp(sc - m)
            den = jnp.sum(p, axis=-1, keepdims=True) + jnp.exp(sink - m)
            out = _dot((p * (1.0 / den)).astype(BF16), v)
            lcol = m + jnp.log(den)
            for pb in range(2):
                r0 = 2 * pb * WINDOW
                o_ref[:, LANES * (2 * g + pb):LANES * (2 * g + pb + 1)] = jnp.where(
                    lane < 64, out[r0:r0 + WINDOW], out[r0 + WINDOW:r0 + 2 * WINDOW]).astype(BF16)
            for hh in range(GROUP):
                l_all = jnp.where(lane == GROUP * g + hh, lcol[WINDOW * hh:WINDOW * (hh + 1)], l_all)
        l_ref[...] = l_all

    return pl.pallas_call(
        body, name="swa_fwd", grid=(nb,),
        out_shape=[jax.ShapeDtypeStruct((s, 512), BF16), jax.ShapeDtypeStruct((s, LANES), F32)],
        in_specs=[pl.BlockSpec(memory_space=pltpu.SMEM),
                  pl.BlockSpec((WINDOW, W_A), lambda i: (i, 0)),
                  pl.BlockSpec((WINDOW, W_A), lambda i: (jnp.maximum(i - 1, 0), 0))],
        out_specs=[pl.BlockSpec((WINDOW, 512), lambda i: (i, 0)), pl.BlockSpec((WINDOW, LANES), lambda i: (i, 0))],
        compiler_params=_params(("parallel",)),
    )(sinks, a, a)


def _mid(att_a, att_b, g, x, target, gate, g_final, wo_a, wo_b, w_out, tm=256):
    s = x.shape[0]
    nt = s // tm

    def body(aa_ref, ab_ref, g_ref, x_ref, t_ref, gate_ref, gf_ref, woa_ref, wob_ref, wout_ref,
             dx_ref, daa_ref, dab_ref, dg_ref, delta_ref, dwoa_ref, dwob_ref, dwout_ref, vec_ref,
             acc_gf, acc_gate, acc_loss):
        step = pl.program_id(0)

        @pl.when(step == 0)
        def _():
            dwoa_ref[...] = jnp.zeros_like(dwoa_ref)
            dwob_ref[...] = jnp.zeros_like(dwob_ref)
            dwout_ref[...] = jnp.zeros_like(dwout_ref)
            acc_gf[...] = jnp.zeros_like(acc_gf)
            acc_gate[...] = jnp.zeros_like(acc_gate)
            acc_loss[...] = jnp.zeros_like(acc_loss)

        def fold(v):
            return jnp.sum(v.reshape(tm // SUBLANES, SUBLANES, D_MODEL), axis=0)

        gate = gate_ref[...]
        gfin = gf_ref[...]
        branches = []
        for att_ref, z_off, wo_ref in ((aa_ref, 0, woa_ref), (ab_ref, 512, wob_ref)):
            att = att_ref[...].astype(F32)
            z = g_ref[:, z_off:z_off + 512].astype(F32)
            sz = _sigmoid(z)
            silu = z * sz
            u = (att * silu).astype(BF16)
            branches.append((att, z, sz, silu, u, _dot(u, wo_ref[...])))
        ga = g_ref[:, 1024:2048].astype(F32)
        gb = g_ref[:, 2048:3072].astype(F32)
        sga, sgb = _sigmoid(ga), _sigmoid(gb)
        y_a, y_b = branches[0][5], branches[1][5]
        mb = (sga * y_a + sgb * y_b).astype(BF16)
        o = _dot(mb, wout_ref[...])
        x2 = x_ref[...] + gate * o
        r2 = lax.rsqrt(jnp.mean(x2 * x2, axis=-1, keepdims=True) + NORM_EPS)
        xn2 = x2 * r2
        err = xn2 * gfin - t_ref[...]
        acc_loss[...] += fold(err * err)
        dy = err * (1.0 / D_MODEL)
        acc_gf[...] += fold(dy * xn2)
        dxn = dy * gfin
        dx2 = r2 * (dxn - xn2 * jnp.mean(dxn * xn2, axis=-1, keepdims=True))
        dx_ref[...] = dx2
        acc_gate[...] += fold(dx2 * o)
        d_o = (dx2 * gate).astype(BF16)
        dwout_ref[...] += _dot(mb, d_o, TN)
        dm = _dot(d_o, wout_ref[...], NT)
        dg_ref[:, 1024:2048] = (dm * y_a * sga * (1.0 - sga)).astype(BF16)
        dg_ref[:, 2048:3072] = (dm * y_b * sgb * (1.0 - sgb)).astype(BF16)
        for (att, z, sz, silu, u, _), sg, wo_ref, dwo_ref, datt_ref, z_off in (
                (branches[0], sga, woa_ref, dwoa_ref, daa_ref, 0), (branches[1], sgb, wob_ref, dwob_ref, dab_ref, 512)):
            dyb = (dm * sg).astype(BF16)
            dwo_ref[...] += _dot(u, dyb, TN)
            du = _dot(dyb, wo_ref[...], NT)
            datt = du * silu
            datt_ref[...] = datt.astype(BF16)
            dg_ref[:, z_off:z_off + 512] = (du * att * (sz * (1.0 + z * (1.0 - sz)))).astype(BF16)
            if z_off == 512:
                prod = datt * att
                hi = prod.astype(BF16)
                lo = (prod - hi.astype(F32)).astype(BF16)
                er = lax.broadcasted_iota(jnp.int32, (512, LANES), 0)
                ec = lax.broadcasted_iota(jnp.int32, (512, LANES), 1)
                e = (er // HEAD_DIM == ec).astype(BF16)
                delta = _dot(hi, e) + _dot(lo, e)
                delta_ref[...] = delta.T[0:SUBLANES, :]

        @pl.when(step == nt - 1)
        def _():
            sub = lax.broadcasted_iota(jnp.int32, (SUBLANES, D_MODEL), 0)
            dgf = jnp.sum(acc_gf[...], axis=0, keepdims=True)
            dgate = jnp.sum(acc_gate[...], axis=0, keepdims=True)
            loss = 0.5 * jnp.sum(acc_loss[...]) * (1.0 / D_MODEL)
            vec_ref[...] = jnp.where(sub == 0, dgf, jnp.where(sub == 1, dgate, jnp.where(sub == 2, loss, 0.0)))

    row = lambda w: pl.BlockSpec((tm, w), lambda i: (i, 0))
    return pl.pallas_call(
        body, name="mid", grid=(nt,),
        out_shape=[jax.ShapeDtypeStruct((s, D_MODEL), F32), jax.ShapeDtypeStruct((s, 512), BF16),
                   jax.ShapeDtypeStruct((s, 512), BF16), jax.ShapeDtypeStruct((s, W_G), BF16),
                   jax.ShapeDtypeStruct((SUBLANES, s), F32),
                   jax.ShapeDtypeStruct((512, D_MODEL), F32), jax.ShapeDtypeStruct((512, D_MODEL), F32),
                   jax.ShapeDtypeStruct((D_MODEL, D_MODEL), F32), jax.ShapeDtypeStruct((SUBLANES, D_MODEL), F32)],
        in_specs=[row(512), row(512), row(W_G), row(D_MODEL), row(D_MODEL),
                  _const_spec((1, D_MODEL)), _const_spec((1, D_MODEL)),
                  _const_spec((512, D_MODEL)), _const_spec((512, D_MODEL)), _const_spec((D_MODEL, D_MODEL))],
        out_specs=[row(D_MODEL), row(512), row(512), row(W_G),
                   pl.BlockSpec((SUBLANES, tm), lambda i: (0, i)),
                   pl.BlockSpec((512, D_MODEL), lambda i: (0, 0)), pl.BlockSpec((512, D_MODEL), lambda i: (0, 0)),
                   pl.BlockSpec((D_MODEL, D_MODEL), lambda i: (0, 0)), pl.BlockSpec((SUBLANES, D_MODEL), lambda i: (0, 0))],
        scratch_shapes=[pltpu.VMEM((SUBLANES, D_MODEL), F32)] * 3,
        compiler_params=_params(("arbitrary",), VMEM_LIMIT),
    )(att_a, att_b, g, x, target, gate, g_final, wo_a, wo_b, w_out)


def _rope_bwd(dt, cos, sin, lane):
    u = dt * sin
    lo = (lane % HEAD_DIM) < (HEAD_DIM // 2)
    return dt * cos + jnp.where(lo, pltpu.roll(u, 96, 1), -pltpu.roll(u, 32, 1))


def _swa_bwd(a, datt, l_all, sinks, cos, sin):
    s = a.shape[0]
    nb = s // WINDOW

    def body(sink_ref, a_ref, ap_ref, do_ref, l_ref, cos_ref, sin_ref, da_ref, ds_ref, halo):
        step = pl.program_id(0)
        blk = nb - 1 - step

        @pl.when(step == 0)
        def _():
            halo[...] = jnp.zeros_like(halo)
            ds_ref[...] = jnp.zeros_like(ds_ref)

        lane = lax.broadcasted_iota(jnp.int32, (WINDOW, LANES), 1)
        sub8 = lax.broadcasted_iota(jnp.int32, (SUBLANES, LANES), 0)
        lane8 = lax.broadcasted_iota(jnp.int32, (SUBLANES, LANES), 1)
        has_prev = blk > 0
        lv = l_ref[...]
        cosv, sinv = cos_ref[...], sin_ref[...]
        dsink = jnp.zeros((SUBLANES, LANES), F32)
        dkb, dvb = [], []
        for g in range(2):
            k, v = _swa_band(a_ref, ap_ref, g, lane)
            q = _stack_heads(a_ref, g, lane)
            dom = _stack_heads(do_ref, g, lane)
            sink = _per_head_column([sink_ref[GROUP * g + hh] for hh in range(GROUP)])
            lcol = _per_head_column([lv[:, GROUP * g + hh:GROUP * g + hh + 1] for hh in range(GROUP)])
            p = jnp.exp(_swa_logits(q, k, has_prev) - lcol)
            dp = _dot(dom, v, NT)
            delta = jnp.sum(p * dp, axis=-1, keepdims=True)
            sink_term = jnp.exp(sink - lcol) * delta
            for hh in range(GROUP):
                tot = jnp.sum(sink_term[WINDOW * hh:WINDOW * (hh + 1)])
                dsink = dsink + jnp.where((sub8 == 0) & (lane8 == GROUP * g + hh), -tot, 0.0)
            ds = (p * (dp - delta)).astype(BF16)
            dq = _dot(ds, k) * SCALE
            for pb in range(2):
                r0 = 2 * pb * WINDOW
                dq_pair = jnp.where(lane < 64, dq[r0:r0 + WINDOW], dq[r0 + WINDOW:r0 + 2 * WINDOW])
                da_ref[:, LANES * (2 * g + pb):LANES * (2 * g + pb + 1)] = _rope_bwd(dq_pair, cosv, sinv, lane).astype(BF16)
            dkb.append(_dot(ds, q, TN) * SCALE)
            dvb.append(_dot(p.astype(BF16), dom, TN))

        def join(pair, r0):
            x0, x1 = pair[0][r0:r0 + WINDOW], pair[1][r0:r0 + WINDOW]
            return jnp.where(lane < 64, x0 + pltpu.roll(x0, 64, 1), x1 + pltpu.roll(x1, 64, 1))

        dk = join(dkb, WINDOW) + halo[:, 0:LANES]
        dv = join(dvb, WINDOW) + halo[:, LANES:2 * LANES]
        da_ref[:, 512:640] = _rope_bwd(dk, cosv, sinv, lane).astype(BF16)
        da_ref[:, 640:768] = dv.astype(BF16)
        halo[:, 0:LANES] = join(dkb, 0)
        halo[:, LANES:2 * LANES] = join(dvb, 0)
        ds_ref[...] += dsink

    rev = lambda w: pl.BlockSpec((WINDOW, w), lambda i: (nb - 1 - i, 0))
    return pl.pallas_call(
        body, name="swa_bwd", grid=(nb,),
        out_shape=[jax.ShapeDtypeStruct((s, W_A), BF16), jax.ShapeDtypeStruct((SUBLANES, LANES), F32)],
        in_specs=[pl.BlockSpec(memory_space=pltpu.SMEM), rev(W_A),
                  pl.BlockSpec((WINDOW, W_A), lambda i: (jnp.maximum(nb - 2 - i, 0), 0)),
                  rev(512), rev(LANES), rev(LANES), rev(LANES)],
        out_specs=[rev(W_A), pl.BlockSpec((SUBLANES, LANES), lambda i: (0, 0))],
        scratch_shapes=[pltpu.VMEM((WINDOW, 2 * LANES), F32)],
        compiler_params=_params(("arbitrary",)),
    )(sinks, a, a, datt, l_all, cos, sin)


def _fox_bwd(qa, ka, b, do, lse, delta, ranges, t):
    s = qa.shape[0]
    nt = s // t

    def body(rg_ref, q_ref, do_ref, lse_ref, dl_ref, k_ref, v_ref, dq_ref, dk_ref, dv_ref, dc_ref, dr_ref, dq_acc):
        p = pl.program_id(0)
        j = pl.program_id(1)
        n_query = jnp.clip(_lane_scalar(rg_ref[0], 1, j), 1, nt - j)

        @pl.when(j == 0)
        def _():
            dq_acc[...] = jnp.zeros_like(dq_acc)

        lane = lax.broadcasted_iota(jnp.int32, (t, LANES), 1)
        rows = lax.broadcasted_iota(jnp.int32, (t, t), 0)
        cols = lax.broadcasted_iota(jnp.int32, (t, t), 1)
        kt = k_ref[...]
        vt = v_ref[...]

        def tile(i, carry, diagonal):
            dk0, dk1, dv = carry
            off = pl.multiple_of(i * t, t)
            qt = q_ref[pl.ds(off, t), :]
            dot_ = do_ref[pl.ds(off, t), :]
            lse_t = lse_ref[0, :, pl.ds(off, t)]
            dl_t = dl_ref[0, :, pl.ds(off, t)]
            dks = []
            for hh in range(2):
                q = qt[:, LANES * hh:LANES * (hh + 1)]
                k = kt[:, LANES * hh:LANES * (hh + 1)]
                st = _dot(k, q, NT)
                if diagonal:
                    st = jnp.where(cols >= rows, st, NEG)
                pt = jnp.exp(st - lse_t[hh:hh + 1, :])
                dom = jnp.where((lane < 64) if hh == 0 else (lane >= 64), dot_, jnp.zeros_like(dot_))
                dv = dv + _dot(pt.astype(BF16), dom)
                dpt = _dot(vt, dom, NT)
                dst = (pt * (dpt - dl_t[hh:hh + 1, :])).astype(BF16)
                dks.append(_dot(dst, q))
                dq_acc[hh, pl.ds(off, t), :] += _dot(dst, k, TN)
            return dk0 + dks[0], dk1 + dks[1], dv

        zero = jnp.zeros((t, LANES), F32)
        carry = tile(j, (zero, zero, zero), True)
        dk0, dk1, dv = lax.fori_loop(j + 1, j + n_query, lambda i, cr: tile(i, cr, False), carry)
        dk_ref[...] = jnp.where(lane < 64, dk0, pltpu.roll(dk1, 64, 1)).astype(BF16)
        dv_ref[...] = dv.astype(BF16)
        c0 = jnp.broadcast_to(dk0[:, 67:68], (t, LANES))
        c1 = jnp.broadcast_to(dk1[:, 67:68], (t, LANES))
        dc_ref[0] = jnp.where(lane == 2 * p, -c0, jnp.where(lane == 2 * p + 1, -c1, 0.0))

        @pl.when(j == nt - 1)
        def _():
            lane_s = lax.broadcasted_iota(jnp.int32, (s, LANES), 1)
            a0, a1 = dq_acc[0], dq_acc[1]
            dq_ref[...] = (jnp.where(lane_s < 64, a0, pltpu.roll(a1, 64, 1)) * SCALE).astype(BF16)
            r0 = jnp.broadcast_to(a0[:, 64:65], (s, LANES))
            r1 = jnp.broadcast_to(a1[:, 64:65], (s, LANES))
            dr_ref[0] = jnp.where(lane_s == 2 * p, r0, jnp.where(lane_s == 2 * p + 1, r1, 0.0))

    return pl.pallas_call(
        body, name="fox_bwd", grid=(4, nt),
        out_shape=[jax.ShapeDtypeStruct((s, 512), BF16), jax.ShapeDtypeStruct((s, 512), BF16),
                   jax.ShapeDtypeStruct((s, 512), BF16), jax.ShapeDtypeStruct((4, s, LANES), F32),
                   jax.ShapeDtypeStruct((4, s, LANES), F32)],
        in_specs=[pl.BlockSpec((1, SUBLANES, LANES), lambda p, j: (p, 0, 0)),
                  pl.BlockSpec((s, 2 * LANES), lambda p, j: (0, p)),
                  pl.BlockSpec((s, LANES), lambda p, j: (0, p)),
                  pl.BlockSpec((1, SUBLANES, s), lambda p, j: (p, 0, 0)),
                  pl.BlockSpec((1, SUBLANES, s), lambda p, j: (p, 0, 0)),
                  pl.BlockSpec((t, 2 * LANES), lambda p, j: (j, p)),
                  pl.BlockSpec((t, LANES), lambda p, j: (j, 8 + p))],
        out_specs=[pl.BlockSpec((s, LANES), lambda p, j: (0, p)),
                   pl.BlockSpec((t, LANES), lambda p, j: (j, p)),
                   pl.BlockSpec((t, LANES), lambda p, j: (j, p)),
                   pl.BlockSpec((1, t, LANES), lambda p, j: (p, j, 0)),
                   pl.BlockSpec((1, s, LANES), lambda p, j: (p, 0, 0))],
        scratch_shapes=[pltpu.VMEM((2, s, LANES), F32)],
        compiler_params=_params(("parallel", "arbitrary"), VMEM_LIMIT),
    )(ranges, qa, do, lse, delta, ka, b)


def _fox_cumsum_bwd(dcum_k, dcum_q, f, bf_pad, tb=256):
    s = f.shape[0]
    nb = s // tb

    def body(dc_ref, dr_ref, f_ref, b_ref, df_ref, db_ref, carry):
        step = pl.program_id(0)

        @pl.when(step == 0)
        def _():
            carry[...] = jnp.zeros_like(carry)
            db_ref[...] = jnp.zeros_like(db_ref)

        lane = lax.broadcasted_iota(jnp.int32, (tb, LANES), 1)
        dc = dc_ref[0] + dr_ref[0]
        for k in range(1, 4):
            dc = dc + (dc_ref[k] + dr_ref[k])
        hi, mid, lo = _split3(dc)
        rows = lax.broadcasted_iota(jnp.int32, (tb, tb), 0)
        cols = lax.broadcasted_iota(jnp.int32, (tb, tb), 1)
        triu = (cols >= rows).astype(BF16)
        dlogf = _dot(triu, hi) + _dot(triu, mid) + _dot(triu, lo) + carry[0:1, :]
        carry[...] = jnp.broadcast_to(dlogf[0:1, :], carry.shape)
        u = f_ref[...] + b_ref[...]
        dfb = jnp.where(lane < N_HEADS, dlogf * _sigmoid(-u), 0.0)
        df_ref[...] = dfb.astype(BF16)
        sub = lax.broadcasted_iota(jnp.int32, (SUBLANES, LANES), 0)
        db_ref[...] += jnp.where(sub == 0, jnp.sum(dfb, axis=0, keepdims=True), 0.0)

    return pl.pallas_call(
        body, name="fox_cumsum_bwd", grid=(nb,),
        out_shape=[jax.ShapeDtypeStruct((s, LANES), BF16), jax.ShapeDtypeStruct((SUBLANES, LANES), F32)],
        in_specs=[pl.BlockSpec((4, tb, LANES), lambda i: (0, nb - 1 - i, 0)),
                  pl.BlockSpec((4, tb, LANES), lambda i: (0, nb - 1 - i, 0)),
                  pl.BlockSpec((tb, LANES), lambda i: (nb - 1 - i, 0)), _const_spec((1, LANES))],
        out_specs=[pl.BlockSpec((tb, LANES), lambda i: (nb - 1 - i, 0)),
                   pl.BlockSpec((SUBLANES, LANES), lambda i: (0, 0))],
        scratch_shapes=[pltpu.VMEM((SUBLANES, LANES), F32)],
        compiler_params=_params(("arbitrary",)),
    )(dcum_k, dcum_q, f, bf_pad)


def _dh_norm_bwd(d_a, d_q, d_k, d_v, d_f, d_g, w_t, x, dx2, gnorm, scale1, tm=256):
    s = x.shape[0]
    nt = s // tm

    def body(da_ref, dq_ref, dk_ref, dv_ref, df_ref, dg_ref, w_ref, x_ref, dx2_ref, g_ref, sc_ref, gx_ref, vec_ref,
             a_sh, a_sc, a_g):
        step = pl.program_id(0)

        @pl.when(step == 0)
        def _():
            a_sh[...] = jnp.zeros_like(a_sh)
            a_sc[...] = jnp.zeros_like(a_sc)
            a_g[...] = jnp.zeros_like(a_g)

        def fold(v):
            return jnp.sum(v.reshape(tm // SUBLANES, SUBLANES, D_MODEL), axis=0)

        dh = (_dot(da_ref[...], w_ref[OFF_A:OFF_A + W_A, :]) + _dot(dq_ref[...], w_ref[OFF_B:OFF_B + 512, :])
              + _dot(dk_ref[...], w_ref[OFF_B + 512:OFF_B + 1024, :]) + _dot(dv_ref[...], w_ref[OFF_B + 1024:OFF_B + W_B, :])
              + _dot(df_ref[...], w_ref[OFF_F:OFF_F + W_F, :]) + _dot(dg_ref[...], w_ref[OFF_G:OFF_G + W_G, :]))
        xv = x_ref[...]
        r = lax.rsqrt(jnp.mean(xv * xv, axis=-1, keepdims=True) + NORM_EPS)
        xn = xv * r
        gn = g_ref[...]
        a_sh[...] += fold(dh)
        a_sc[...] += fold(dh * (xn * gn))
        dn1 = dh * sc_ref[...]
        a_g[...] += fold(dn1 * xn)
        dxn = dn1 * gn
        gx_ref[...] = dx2_ref[...] + r * (dxn - xn * jnp.mean(dxn * xn, axis=-1, keepdims=True))

        @pl.when(step == nt - 1)
        def _():
            sub = lax.broadcasted_iota(jnp.int32, (SUBLANES, D_MODEL), 0)
            v_sh = jnp.sum(a_sh[...], axis=0, keepdims=True)
            v_sc = jnp.sum(a_sc[...], axis=0, keepdims=True)
            v_g = jnp.sum(a_g[...], axis=0, keepdims=True)
            vec_ref[...] = jnp.where(sub == 0, v_sh, jnp.where(sub == 1, v_sc, jnp.where(sub == 2, v_g, 0.0)))

    row = lambda w: pl.BlockSpec((tm, w), lambda i: (i, 0))
    return pl.pallas_call(
        body, name="dh_norm_bwd", grid=(nt,),
        out_shape=[jax.ShapeDtypeStruct((s, D_MODEL), F32), jax.ShapeDtypeStruct((SUBLANES, D_MODEL), F32)],
        in_specs=[row(W_A), row(512), row(512), row(512), row(W_F), row(W_G), _const_spec((W_INT, D_MODEL)),
                  row(D_MODEL), row(D_MODEL), _const_spec((1, D_MODEL)), _const_spec((1, D_MODEL))],
        out_specs=[row(D_MODEL), pl.BlockSpec((SUBLANES, D_MODEL), lambda i: (0, 0))],
        scratch_shapes=[pltpu.VMEM((SUBLANES, D_MODEL), F32)] * 3,
        compiler_params=_params(("arbitrary",), VMEM_LIMIT),
    )(d_a, d_q, d_k, d_v, d_f, d_g, w_t, x, dx2, gnorm, scale1)


def _dw_in(h, d, name, tn, ts=512):
    s, n = d.shape
    ns = s // ts

    def body(h_ref, d_ref, o_ref, acc):
        k = pl.program_id(1)

        @pl.when(k == 0)
        def _():
            acc[...] = jnp.zeros_like(acc)

        acc[...] += _dot(h_ref[...], d_ref[...], TN)

        @pl.when(k == ns - 1)
        def _():
            o_ref[...] = acc[...]

    return pl.pallas_call(
        body, name=name, grid=(n // tn, ns),
        out_shape=jax.ShapeDtypeStruct((D_MODEL, n), F32),
        in_specs=[pl.BlockSpec((ts, D_MODEL), lambda jn, k: (k, 0)), pl.BlockSpec((ts, tn), lambda jn, k: (k, jn))],
        out_specs=pl.BlockSpec((D_MODEL, tn), lambda jn, k: (0, jn)),
        scratch_shapes=[pltpu.VMEM((D_MODEL, tn), F32)],
        compiler_params=_params(("parallel", "arbitrary"), VMEM_LIMIT),
    )(h, d)


def _small_grads(packs, c_t, dada_shard):
    def body(p_ref, ct_ref, da_ref, sum_ref, gw_ref):
        acc = p_ref[0]
        for dev in range(1, 8):
            acc = acc + p_ref[dev]
        sum_ref[...] = acc
        gw_ref[...] = jnp.dot(ct_ref[...], da_ref[...], preferred_element_type=F32, precision=lax.Precision.HIGHEST)

    return pl.pallas_call(
        body, name="small_grads",
        out_shape=[jax.ShapeDtypeStruct(packs.shape[1:], F32),
                   jax.ShapeDtypeStruct((c_t.shape[0], dada_shard.shape[1]), F32)],
    )(packs, c_t, dada_shard)


def _adamw(w, g, m, v, name):
    r, c = w.shape
    tr = 128 if r % 128 == 0 else r
    c1 = 1.0 / (1.0 - ADAM_B1 ** ADAM_STEP)
    c2 = 1.0 / (1.0 - ADAM_B2 ** ADAM_STEP)

    def body(w_ref, g_ref, m_ref, v_ref, d_ref, mo_ref, vo_ref):
        gv = g_ref[...]
        mn = ADAM_B1 * m_ref[...] + (1.0 - ADAM_B1) * gv
        vn = ADAM_B2 * v_ref[...] + (1.0 - ADAM_B2) * (gv * gv)
        mo_ref[...] = mn
        vo_ref[...] = vn
        d_ref[...] = -ADAM_LR * ((mn * c1) / (jnp.sqrt(vn * c2) + ADAM_EPS) + ADAM_WD * w_ref[...])

    spec = pl.BlockSpec((tr, c), lambda i: (i, 0))
    return pl.pallas_call(
        body, name=name, grid=(r // tr,),
        out_shape=[jax.ShapeDtypeStruct((r, c), F32)] * 3,
        in_specs=[spec] * 4, out_specs=[spec] * 3,
        compiler_params=_params(("parallel",)),
    )(w, g, m, v)


def _rope_tables(positions):
    inv_freq = 10000.0 ** (-jnp.arange(0, HEAD_DIM, 2, dtype=F32) / HEAD_DIM)
    ang = positions.astype(F32)[:, None] * inv_freq
    cos, sin = jnp.tile(jnp.cos(ang), (1, 4)), jnp.tile(jnp.sin(ang), (1, 4))
    lo = (jnp.arange(LANES) % HEAD_DIM) < (HEAD_DIM // 2)
    return cos, sin, jnp.where(lo, -sin, 0.0), jnp.where(lo, 0.0, sin)


def _pad_rows(v, rows=SUBLANES):
    return jnp.pad(v, ((0, rows - v.shape[0]), (0, 0)))


def kernel(x, c, positions, w_ada, b_ada, g_norm, w_in, b_f, sinks, w_o_swa, w_o_fox, w_out, g_final, loss_target, m_w_ada, m_b_ada, m_g_norm, m_w_in, m_b_f, m_sinks, m_w_o_swa, m_w_o_fox, m_w_out, m_g_final, v_w_ada, v_b_ada, v_g_norm, v_w_in, v_b_f, v_sinks, v_w_o_swa, v_w_o_fox, v_w_out, v_g_final):
    ix, iy, ic = lax.axis_index("x"), lax.axis_index("y"), lax.axis_index("c")
    chip = 2 * ix + iy
    dev = 2 * chip + ic
    xs, tgt = x[0], loss_target[0]
    s = xs.shape[0]

    c_all = _allgather_small(_pad_rows(c), "gather_c")[:, 0, :]
    b_ada_shard = lax.dynamic_slice(b_ada, (0, chip * 768), (1, 768))
    ada_all = _allgather_small(_ada_part(c_all, w_ada[0], b_ada_shard), "gather_ada")
    ada = lax.dynamic_index_in_dim(ada_all[::2], dev, axis=1, keepdims=False).reshape(1, 3 * D_MODEL)
    shift, scale, gate = ada[:, :D_MODEL], ada[:, D_MODEL:2 * D_MODEL], ada[:, 2 * D_MODEL:]
    scale1 = 1.0 + scale

    g_in, g_oa, g_ob, g_out = _allgather_weights([w_in[0], w_o_swa[0], w_o_fox[0], w_out[0]], "gather_weights")
    w_ref_order = jnp.transpose(g_in, (1, 0, 2)).reshape(D_MODEL, R_END)
    w_int = jnp.concatenate([
        w_ref_order[:, :R_ZA], w_ref_order[:, R_QB:R_FB], w_ref_order[:, R_FB:R_ZB],
        jnp.zeros((D_MODEL, W_F - N_HEADS), BF16), w_ref_order[:, R_ZA:R_QB], w_ref_order[:, R_ZB:]], axis=1)
    w_int_t = w_int.T
    wo_a = jnp.transpose(g_oa, (1, 0, 2)).reshape(512, D_MODEL)
    wo_b = jnp.transpose(g_ob, (1, 0, 2)).reshape(512, D_MODEL)
    w_o = g_out.reshape(D_MODEL, D_MODEL)

    cos, sin, sin_lo, sin_hi = _rope_tables(positions[0])
    bf_pad = jnp.pad(b_f, ((0, 0), (0, LANES - N_HEADS)))
    sink_vec = sinks[0]

    a, b, f, g, h = _norm_proj(xs, g_norm * scale1, shift, w_int, cos, sin_lo, sin_hi)
    att_a, l_swa = _swa_fwd(a, sink_vec)
    cum = _fox_cumsum(f, bf_pad)
    qa, ka, va, stats = _fox_prep(b, cum, FOX_TILE)
    ranges = _fox_tile_ranges(stats)
    att_b, lse = _fox_fwd(qa, ka, va, ranges, FOX_TILE)

    dx2, datt_a, datt_b, d_g, delta8, dwo_a, dwo_b, dw_out, vec_mid = _mid(
        att_a, att_b, g, xs, tgt, gate, g_final.reshape(1, D_MODEL), wo_a, wo_b, w_o)
    delta = jnp.pad(delta8.reshape(4, 2, s), ((0, 0), (0, SUBLANES - 2), (0, 0)))
    d_a, dsink = _swa_bwd(a, datt_a, l_swa, sink_vec, cos, sin)
    dq, dk, dv, dcum_k, dcum_q = _fox_bwd(qa, ka, b, datt_b, lse, delta, ranges, FOX_TILE)
    d_f, dbf = _fox_cumsum_bwd(dcum_k, dcum_q, f, bf_pad)
    grad_x, vec_dh = _dh_norm_bwd(d_a, dq, dk, dv, d_f, d_g, w_int_t, xs, dx2, g_norm, scale1)
    dw_a = _dw_in(h, d_a, "dw_in_a", 768)
    dw_q = _dw_in(h, dq, "dw_in_q", 512)
    dw_k = _dw_in(h, dk, "dw_in_k", 512)
    dw_v = _dw_in(h, dv, "dw_in_v", 512)
    dw_f = _dw_in(h, d_f, "dw_in_f", 128)
    dw_g = _dw_in(h, d_g, "dw_in_g", 768)
    dw_in = jnp.concatenate([dw_a, dw_g[:, :512], dw_q, dw_k, dw_v, dw_f[:, :N_HEADS], dw_g[:, 512:]], axis=1)

    tail = jnp.pad(jnp.concatenate([dbf[0:1, :N_HEADS], dsink[0:1, :N_HEADS]], axis=1), ((0, 0), (0, D_MODEL - 2 * N_HEADS)))
    pack = jnp.concatenate([c, vec_dh[0:2], vec_mid[1:2], vec_dh[2:3], vec_mid[0:1], tail, vec_mid[2:3]], axis=0)
    packs = _allgather_small(pack, "gather_small")
    dada_all = packs[:, 1:4, :].reshape(8, 3 * D_MODEL)
    dada_shard = lax.dynamic_slice(dada_all, (0, chip * 768), (8, 768))
    sums, g_w_ada = _small_grads(packs, packs[:, 0, :].T, dada_shard)
    g_b_ada = sums[1:4].reshape(1, 3 * D_MODEL)
    g_g_norm = sums[4:5]
    g_g_final = sums[5]
    g_b_f = sums[6:7, :N_HEADS]
    g_sinks = sums[6:7, N_HEADS:2 * N_HEADS]
    loss = sums[7, 0]

    def slots(w, axis):
        if axis == 1:
            return jnp.transpose(w.reshape(w.shape[0], 4, w.shape[1] // 4), (1, 0, 2))
        return w.reshape(4, w.shape[0] // 4, w.shape[1])

    (g_w_in,) = _reduce_scatter([slots(dw_in, 1)], "reduce_w_in")
    g_wo_a, g_wo_b, g_w_out = _reduce_scatter([slots(dwo_a, 1), slots(dwo_b, 1), slots(dw_out, 0)], "reduce_w_small")

    grads = {
        "w_ada": g_w_ada, "b_ada": g_b_ada, "g_norm": g_g_norm, "w_in": g_w_in, "b_f": g_b_f, "sinks": g_sinks,
        "w_o_swa": g_wo_a, "w_o_fox": g_wo_b, "w_out": g_w_out, "g_final": g_g_final,
    }
    params = {
        "w_ada": (w_ada, m_w_ada, v_w_ada), "b_ada": (b_ada, m_b_ada, v_b_ada), "g_norm": (g_norm, m_g_norm, v_g_norm),
        "w_in": (w_in, m_w_in, v_w_in), "b_f": (b_f, m_b_f, v_b_f), "sinks": (sinks, m_sinks, v_sinks),
        "w_o_swa": (w_o_swa, m_w_o_swa, v_w_o_swa), "w_o_fox": (w_o_fox, m_w_o_fox, v_w_o_fox),
        "w_out": (w_out, m_w_out, v_w_out), "g_final": (g_final, m_g_final, v_g_final),
    }
    names = list(grads)
    out_g, out_d, out_m, out_v = [], [], [], []
    for nm in names:
        w, m, v = params[nm]
        shape2 = (w.shape[-2], w.shape[-1]) if w.ndim >= 2 else (1, w.shape[0])
        d_, m_, v_ = _adamw(w.reshape(shape2), grads[nm].reshape(shape2), m.reshape(shape2), v.reshape(shape2), "adamw_" + nm)
        out_g.append(grads[nm].reshape(w.shape))
        out_d.append(d_.reshape(w.shape))
        out_m.append(m_.reshape(w.shape))
        out_v.append(v_.reshape(w.shape))
    return (loss, grad_x[None], *out_g, *out_d, *out_m, *out_v)
```

```python
import functools

import numpy as np
import jax
import jax.numpy as jnp
from jax import lax
from jax.experimental import pallas as pl
from jax.experimental.pallas import tpu as pltpu

F32 = jnp.float32
BF16 = jnp.bfloat16
MESH = pl.DeviceIdType.MESH

D_MODEL = 1024
HEAD_DIM = 64
N_HEADS = 8
WINDOW = 128
NORM_EPS = 1e-6
SCALE = HEAD_DIM ** -0.5
NEG = -1e30
LANES = 128
SUBLANES = 8
VMEM_LIMIT = 60 * 1024 * 1024
FOX_TILE = 512

W_A, W_B, W_F, W_G = 768, 1536, 128, 3072
OFF_A, OFF_B, OFF_F, OFF_G = 0, 768, 2304, 2432
W_INT = W_A + W_B + W_F + W_G
R_ZA, R_QB, R_FB, R_ZB, R_END = 768, 1280, 2816, 2824, 5384

ADAM_LR, ADAM_B1, ADAM_B2, ADAM_EPS, ADAM_WD, ADAM_STEP = 0.001, 0.9, 0.999, 1e-08, 0.01, 10

NT = (((1,), (1,)), ((), ()))
TN = (((0,), (0,)), ((), ()))


def _dot(a, b, dims=None):
    if dims is None:
        return jnp.dot(a, b, preferred_element_type=F32)
    return lax.dot_general(a, b, dims, preferred_element_type=F32)


def _split3(v):
    hi = v.astype(BF16)
    r1 = v - hi.astype(F32)
    mid = r1.astype(BF16)
    lo = (r1 - mid.astype(F32)).astype(BF16)
    return hi, mid, lo


def _sigmoid(v):
    return 1.0 / (1.0 + jnp.exp(-v))


def _params(sem=None, vmem=None):
    return pltpu.CompilerParams(dimension_semantics=sem, vmem_limit_bytes=vmem)


def _const_spec(shape):
    nd = len(shape)
    return pl.BlockSpec(shape, lambda *_: (0,) * nd, pipeline_mode=pl.Buffered(1))


def _flip(v, f):
    return 1 - v if f else v


def _allgather_small(v, name):
    r, n = v.shape

    def body(v_ref, out_ref, send_sems, recv_sems):
        x, y, c = lax.axis_index("x"), lax.axis_index("y"), lax.axis_index("c")
        me = 4 * x + 2 * y + c
        out_ref[me] = v_ref[...]
        peers = []
        for k in range(1, 8):
            peers.append((_flip(x, k & 4), _flip(y, k & 2), _flip(c, k & 1)))
        sends = []
        for k, peer in enumerate(peers):
            cp = pltpu.make_async_remote_copy(
                src_ref=v_ref, dst_ref=out_ref.at[me], send_sem=send_sems.at[k], recv_sem=recv_sems.at[k],
                device_id=peer, device_id_type=MESH)
            cp.start()
            sends.append(cp)
        for k, peer in enumerate(peers):
            src = 4 * peer[0] + 2 * peer[1] + peer[2]
            pltpu.make_async_remote_copy(
                src_ref=v_ref, dst_ref=out_ref.at[src], send_sem=send_sems.at[k], recv_sem=recv_sems.at[k],
                device_id=peer, device_id_type=MESH).wait_recv()
        for cp in sends:
            cp.wait_send()

    return pl.pallas_call(
        body, name=name,
        out_shape=jax.ShapeDtypeStruct((8, r, n), F32),
        in_specs=[pl.BlockSpec(memory_space=pltpu.VMEM)],
        out_specs=pl.BlockSpec(memory_space=pltpu.VMEM),
        scratch_shapes=[pltpu.SemaphoreType.DMA((7,)), pltpu.SemaphoreType.DMA((7,))],
    )(v)


_CHIP_FLIPS = ((1, 0), (0, 1), (1, 1))


def _allgather_weights(shards, name):
    n = len(shards)

    def body(*refs):
        ins, outs = refs[:n], refs[n:2 * n]
        send_sems, recv_sems = refs[2 * n], refs[2 * n + 1]
        x, y, c = lax.axis_index("x"), lax.axis_index("y"), lax.axis_index("c")
        k_me = 2 * x + y
        sibling = (x, y, 1 - c)
        chips = [(_flip(x, fx), _flip(y, fy)) for fx, fy in _CHIP_FLIPS]

        def piece(i, chip_k, half):
            hr = ins[i].shape[0] // 2
            return outs[i].at[chip_k, pl.ds(half * hr, hr), :]

        def copy(i, slot, chip_k, half, to):
            return pltpu.make_async_remote_copy(
                src_ref=piece(i, chip_k, half), dst_ref=piece(i, chip_k, half),
                send_sem=send_sems.at[6 * i + slot], recv_sem=recv_sems.at[6 * i + slot],
                device_id=to, device_id_type=MESH)

        for i in range(n):
            outs[i][k_me] = ins[i][...].astype(BF16)
        started = []
        for i in range(n):
            for j, chip in enumerate(chips):
                cp = copy(i, j, k_me, c, (chip[0], chip[1], c))
                cp.start()
                started.append(cp)
        for j, chip in enumerate(chips):
            chip_k = 2 * chip[0] + chip[1]
            for i in range(n):
                copy(i, j, chip_k, c, (chip[0], chip[1], c)).wait_recv()
                cp = copy(i, 3 + j, chip_k, c, sibling)
                cp.start()
                started.append(cp)
        for j, chip in enumerate(chips):
            chip_k = 2 * chip[0] + chip[1]
            for i in range(n):
                copy(i, 3 + j, chip_k, 1 - c, sibling).wait_recv()
        for cp in started:
            cp.wait_send()

    return pl.pallas_call(
        body, name=name,
        out_shape=[jax.ShapeDtypeStruct((4,) + s.shape, BF16) for s in shards],
        in_specs=[pl.BlockSpec(memory_space=pltpu.VMEM)] * n,
        out_specs=[pl.BlockSpec(memory_space=pltpu.VMEM)] * n,
        scratch_shapes=[pltpu.SemaphoreType.DMA((6 * n,)), pltpu.SemaphoreType.DMA((6 * n,))],
        compiler_params=_params(vmem=VMEM_LIMIT),
    )(*shards)


def _reduce_scatter(pieces, name):
    n = len(pieces)

    def body(*refs):
        ins, outs = refs[:n], refs[n:2 * n]
        own, got = refs[2 * n:3 * n], refs[3 * n:4 * n]
        sendb, recvb = refs[4 * n:5 * n], refs[5 * n:6 * n]
        send_sems, recv_sems, local_sems = refs[6 * n:6 * n + 3]
        x, y, c = lax.axis_index("x"), lax.axis_index("y"), lax.axis_index("c")
        k_me = 2 * x + y
        sibling = (x, y, 1 - c)
        chips = [(_flip(x, fx), _flip(y, fy)) for fx, fy in _CHIP_FLIPS]
        hrs = [p.shape[1] // 2 for p in pieces]

        def remote(i, slot, src, dst, to):
            return pltpu.make_async_remote_copy(
                src_ref=src, dst_ref=dst, send_sem=send_sems.at[5 * i + slot], recv_sem=recv_sems.at[5 * i + slot],
                device_id=to, device_id_type=MESH)

        started = []
        loads = []
        for i in range(n):
            ld = pltpu.make_async_copy(ins[i].at[:, pl.ds(c * hrs[i], hrs[i]), :], own[i], local_sems.at[i])
            ld.start()
            loads.append(ld)
            cp = remote(i, 0, ins[i].at[:, pl.ds((1 - c) * hrs[i], hrs[i]), :], got[i], sibling)
            cp.start()
            started.append(cp)
        for i in range(n):
            loads[i].wait()
            remote(i, 0, ins[i].at[:, pl.ds(c * hrs[i], hrs[i]), :], got[i], sibling).wait_recv()
            for j, chip in enumerate(chips):
                chip_k = 2 * chip[0] + chip[1]
                sendb[i][j] = (own[i][chip_k] + got[i][chip_k]).astype(BF16)
                cp = remote(i, 1 + j, sendb[i].at[j], recvb[i].at[j], (chip[0], chip[1], c))
                cp.start()
                started.append(cp)
        for i in range(n):
            acc = own[i][k_me] + got[i][k_me]
            for j, chip in enumerate(chips):
                remote(i, 1 + j, sendb[i].at[j], recvb[i].at[j], (chip[0], chip[1], c)).wait_recv()
                acc = acc + recvb[i][j].astype(F32)
            mine = outs[i].at[pl.ds(c * hrs[i], hrs[i]), :]
            outs[i][pl.ds(pl.multiple_of(c * hrs[i], SUBLANES), hrs[i]), :] = acc
            cp = remote(i, 4, mine, mine, sibling)
            cp.start()
            started.append(cp)
        for i in range(n):
            theirs = outs[i].at[pl.ds((1 - c) * hrs[i], hrs[i]), :]
            remote(i, 4, theirs, theirs, sibling).wait_recv()
        for cp in started:
            cp.wait_send()

    scratch = []
    scratch += [pltpu.VMEM((4, p.shape[1] // 2, p.shape[2]), F32) for p in pieces]
    scratch += [pltpu.VMEM((4, p.shape[1] // 2, p.shape[2]), F32) for p in pieces]
    scratch += [pltpu.VMEM((3, p.shape[1] // 2, p.shape[2]), BF16) for p in pieces]
    scratch += [pltpu.VMEM((3, p.shape[1] // 2, p.shape[2]), BF16) for p in pieces]
    scratch += [pltpu.SemaphoreType.DMA((5 * n,)), pltpu.SemaphoreType.DMA((5 * n,)), pltpu.SemaphoreType.DMA((n,))]
    return pl.pallas_call(
        body, name=name,
        out_shape=[jax.ShapeDtypeStruct(p.shape[1:], F32) for p in pieces],
        in_specs=[pl.BlockSpec(memory_space=pl.ANY)] * n,
        out_specs=[pl.BlockSpec(memory_space=pltpu.VMEM)] * n,
        scratch_shapes=scratch,
        compiler_params=_params(vmem=VMEM_LIMIT),
    )(*pieces)


def _ada_part(c_all, w_shard, b_shard):
    def body(c_ref, w_ref, b_ref, o_ref):
        o_ref[...] = _dot(c_ref[...].astype(BF16), w_ref[...].astype(BF16)) + b_ref[...]

    return pl.pallas_call(
        body, name="ada_part",
        out_shape=jax.ShapeDtypeStruct((c_all.shape[0], w_shard.shape[1]), F32),
    )(c_all, w_shard, b_shard)


def _rope_fwd(t, cos, sin_lo, sin_hi):
    return t * cos + pltpu.roll(t, 96, 1) * sin_lo + pltpu.roll(t, 32, 1) * sin_hi


def _norm_proj(x, gmod, shift, w_int, cos, sin_lo, sin_hi, tm=512):
    s = x.shape[0]

    def body(x_ref, g_ref, sh_ref, w_ref, cos_ref, sl_ref, sh2_ref, a_ref, b_ref, f_ref, gg_ref, ht_ref):
        xv = x_ref[...]
        r = lax.rsqrt(jnp.mean(xv * xv, axis=-1, keepdims=True) + NORM_EPS)
        hf = (xv * r) * g_ref[...] + sh_ref[...]
        hb = hf.astype(BF16)
        ht_ref[...] = hf.T.astype(BF16)
        pa = _dot(hb, w_ref[:, OFF_A:OFF_A + W_A])
        cosv, sl, sh2 = cos_ref[...], sl_ref[...], sh2_ref[...]
        for j in range(5):
            t = pa[:, LANES * j:LANES * (j + 1)]
            a_ref[:, LANES * j:LANES * (j + 1)] = _rope_fwd(t, cosv, sl, sh2).astype(BF16)
        a_ref[:, 640:768] = pa[:, 640:768].astype(BF16)
        b_ref[...] = _dot(hb, w_ref[:, OFF_B:OFF_B + W_B]).astype(BF16)
        f_ref[...] = _dot(hb, w_ref[:, OFF_F:OFF_F + W_F])
        gg_ref[...] = _dot(hb, w_ref[:, OFF_G:OFF_G + W_G]).astype(BF16)

    row = lambda w: pl.BlockSpec((tm, w), lambda i: (i, 0))
    return pl.pallas_call(
        body, name="norm_proj", grid=(s // tm,),
        out_shape=[jax.ShapeDtypeStruct((s, W_A), BF16), jax.ShapeDtypeStruct((s, W_B), BF16),
                   jax.ShapeDtypeStruct((s, W_F), F32), jax.ShapeDtypeStruct((s, W_G), BF16),
                   jax.ShapeDtypeStruct((D_MODEL, s), BF16)],
        in_specs=[row(D_MODEL), _const_spec((1, D_MODEL)), _const_spec((1, D_MODEL)), _const_spec((D_MODEL, W_INT)),
                  row(LANES), row(LANES), row(LANES)],
        out_specs=[row(W_A), row(W_B), row(W_F), row(W_G), pl.BlockSpec((D_MODEL, tm), lambda i: (0, i))],
        compiler_params=_params(("parallel",), VMEM_LIMIT),
    )(x, gmod, shift, w_int, cos, sin_lo, sin_hi)


def _log_sigmoid(u):
    return jnp.minimum(u, 0.0) - jnp.log(1.0 + jnp.exp(-jnp.abs(u)))


def _fox_cumsum(f, bf_pad, tb=256):
    s = f.shape[0]

    def body(f_ref, b_ref, cum_ref, carry):
        @pl.when(pl.program_id(0) == 0)
        def _():
            carry[...] = jnp.zeros_like(carry)

        lane = lax.broadcasted_iota(jnp.int32, (tb, LANES), 1)
        logf = jnp.where(lane < N_HEADS, _log_sigmoid(f_ref[...] + b_ref[...]), 0.0)
        hi, mid, lo = _split3(logf)
        rows = lax.broadcasted_iota(jnp.int32, (tb, tb), 0)
        cols = lax.broadcasted_iota(jnp.int32, (tb, tb), 1)
        tril = (cols <= rows).astype(BF16)
        cum = _dot(tril, hi) + _dot(tril, mid) + _dot(tril, lo) + carry[0:1, :]
        cum_ref[...] = cum
        carry[...] = jnp.broadcast_to(cum[tb - 1:tb, :], carry.shape)

    return pl.pallas_call(
        body, name="fox_cumsum", grid=(s // tb,),
        out_shape=jax.ShapeDtypeStruct((s, LANES), F32),
        in_specs=[pl.BlockSpec((tb, LANES), lambda i: (i, 0)), _const_spec((1, LANES))],
        out_specs=pl.BlockSpec((tb, LANES), lambda i: (i, 0)),
        scratch_shapes=[pltpu.VMEM((SUBLANES, LANES), F32)],
        compiler_params=_params(("arbitrary",)),
    )(f, bf_pad)


def _fox_prep(b, cum, t):
    s = b.shape[0]

    def body(b_ref, cum_ref, q_ref, k_ref, v_ref, st_ref):
        lane = lax.broadcasted_iota(jnp.int32, (t, LANES), 1)
        sub8 = lax.broadcasted_iota(jnp.int32, (SUBLANES, LANES), 0)
        lane8 = lax.broadcasted_iota(jnp.int32, (SUBLANES, LANES), 1)
        cumv = cum_ref[...]
        stats = jnp.zeros((SUBLANES, LANES), F32)
        for h in range(N_HEADS):
            p, odd = h // 2, h % 2
            ch = jnp.broadcast_to(cumv[:, h:h + 1], (t, LANES))
            hi, mid, lo = (x.astype(F32) for x in _split3(ch))
            qp = b_ref[:, LANES * p:LANES * (p + 1)].astype(F32)
            kp = b_ref[:, 512 + LANES * p:512 + LANES * (p + 1)].astype(F32)
            vp = b_ref[:, 1024 + LANES * p:1024 + LANES * (p + 1)].astype(F32)
            if odd:
                qp, kp, vp = pltpu.roll(qp, 64, 1), pltpu.roll(kp, 64, 1), pltpu.roll(vp, 64, 1)
            qs = jnp.where(lane < 64, qp * SCALE, 0.0)
            ks = jnp.where(lane < 64, kp, 0.0)
            qa = jnp.where(lane < 64, qs,
                           jnp.where(lane == 64, hi, jnp.where(lane == 65, mid, jnp.where(lane == 66, lo,
                           jnp.where(lane < 70, 1.0, 0.0)))))
            ka = jnp.where(lane < 64, ks,
                           jnp.where(lane < 67, 1.0, jnp.where(lane == 67, -hi, jnp.where(lane == 68, -mid,
                           jnp.where(lane == 69, -lo, 0.0)))))
            q_ref[:, LANES * h:LANES * (h + 1)] = qa.astype(BF16)
            k_ref[:, LANES * h:LANES * (h + 1)] = ka.astype(BF16)
            v_ref[:, LANES * h:LANES * (h + 1)] = jnp.where(lane < 64, vp, jnp.where(lane == 64, 1.0, 0.0)).astype(BF16)
            qn = jnp.sqrt(jnp.max(jnp.sum(qs * qs, axis=-1, keepdims=True)))
            kn = jnp.sqrt(jnp.max(jnp.sum(ks * ks, axis=-1, keepdims=True)))
            csum = hi + mid + lo
            row = jnp.where(lane8 == 0, qn, jnp.where(lane8 == 1, kn, jnp.where(
                lane8 == 2, csum[0:1, :], jnp.where(lane8 == 3, csum[t - 1:t, :], 0.0))))
            stats = jnp.where(sub8 == h, row, stats)
        st_ref[0] = stats

    wide = pl.BlockSpec((t, 1024), lambda i: (i, 0))
    return pl.pallas_call(
        body, name="fox_prep", grid=(s // t,),
        out_shape=[jax.ShapeDtypeStruct((s, 1024), BF16)] * 3 + [jax.ShapeDtypeStruct((s // t, SUBLANES, LANES), F32)],
        in_specs=[pl.BlockSpec((t, W_B), lambda i: (i, 0)), pl.BlockSpec((t, LANES), lambda i: (i, 0))],
        out_specs=[wide, wide, wide, pl.BlockSpec((1, SUBLANES, LANES), lambda i: (i, 0, 0))],
        compiler_params=_params(("parallel",)),
    )(b, cum)


PRUNE_MARGIN = 90.0


def _fox_tile_ranges(stats):
    nt = stats.shape[0]
    ball = 2.0 * 1.01 * jnp.max(stats[:, :, 0], axis=0) * jnp.max(stats[:, :, 1], axis=0)
    d = ball[None, None, :] + stats[:, None, :, 2] - stats[None, :, :, 3]
    idx = jnp.arange(nt)
    skip = (d <= -PRUNE_MARGIN) & (idx[None, :, None] < idx[:, None, None])
    skip = skip.reshape(nt, nt, 4, 2).all(axis=-1)
    first_key = jnp.sum(skip, axis=1).astype(F32)
    needed = (~skip) & (idx[None, :, None] <= idx[:, None, None])
    n_query = jnp.sum(needed, axis=0).astype(F32)
    table = jnp.zeros((4, SUBLANES, LANES), F32)
    table = table.at[:, 0, :nt].set(first_key.T)
    table = table.at[:, 1, :nt].set(n_query.T)
    return table


def _lane_scalar(block, row, lane_idx):
    sub8 = lax.broadcasted_iota(jnp.int32, (SUBLANES, LANES), 0)
    lane8 = lax.broadcasted_iota(jnp.int32, (SUBLANES, LANES), 1)
    return jnp.sum(jnp.where((sub8 == row) & (lane8 == lane_idx), block, 0.0)).astype(jnp.int32)


def _fox_fwd(qa, ka, va, ranges, t):
    s = qa.shape[0]
    nt = s // t
    nc = t // LANES

    def body(rg_ref, q_ref, k_ref, v_ref, o_ref, lse_ref):
        i = pl.program_id(1)
        lane = lax.broadcasted_iota(jnp.int32, (t, LANES), 1)
        rows = lax.broadcasted_iota(jnp.int32, (t, t), 0)
        cols = lax.broadcasted_iota(jnp.int32, (t, t), 1)
        first = jnp.clip(_lane_scalar(rg_ref[0], 0, i), 0, i)

        def tile(j, carry, diagonal):
            off = pl.multiple_of(j * t, t)
            kt = k_ref[pl.ds(off, t), :]
            vt = v_ref[pl.ds(off, t), :]
            out = []
            for hh in range(2):
                m, acc = carry[2 * hh], carry[2 * hh + 1]
                sc = _dot(q_ref[:, LANES * hh:LANES * (hh + 1)], kt[:, LANES * hh:LANES * (hh + 1)], NT)
                if diagonal:
                    sc = jnp.where(cols <= rows, sc, NEG)
                part = sc[:, 0:LANES]
                for cch in range(1, nc):
                    part = jnp.maximum(part, sc[:, LANES * cch:LANES * (cch + 1)])
                m_new = jnp.maximum(m, jnp.max(part, axis=-1, keepdims=True))
                alpha = jnp.exp(m - m_new)
                p = jnp.exp(sc - m_new).astype(BF16)
                out += [m_new, alpha * acc + _dot(p, vt[:, LANES * hh:LANES * (hh + 1)])]
            return tuple(out)

        col0 = jnp.full((t, 1), NEG, F32)
        zero = jnp.zeros((t, LANES), F32)
        carry = lax.fori_loop(first, i, lambda j, cr: tile(j, cr, False), (col0, zero, col0, zero))
        m0, acc0, m1, acc1 = tile(i, carry, True)
        l0, l1 = acc0[:, 64:65], acc1[:, 64:65]
        o_ref[...] = jnp.where(lane < 64, acc0 * (1.0 / l0), pltpu.roll(acc1 * (1.0 / l1), 64, 1)).astype(BF16)
        sub = lax.broadcasted_iota(jnp.int32, (SUBLANES, t), 0)
        lse0 = jnp.broadcast_to(m0 + jnp.log(l0), (t, LANES)).T[0:SUBLANES, :]
        lse1 = jnp.broadcast_to(m1 + jnp.log(l1), (t, LANES)).T[0:SUBLANES, :]
        lse_ref[0] = jnp.where(sub == 0, lse0, jnp.where(sub == 1, lse1, 0.0))

    pair = pl.BlockSpec((s, 2 * LANES), lambda p, i: (0, p))
    return pl.pallas_call(
        body, name="fox_fwd", grid=(4, nt),
        out_shape=[jax.ShapeDtypeStruct((s, 512), BF16), jax.ShapeDtypeStruct((4, SUBLANES, s), F32)],
        in_specs=[pl.BlockSpec((1, SUBLANES, LANES), lambda p, i: (p, 0, 0)),
                  pl.BlockSpec((t, 2 * LANES), lambda p, i: (i, p)), pair, pair],
        out_specs=[pl.BlockSpec((t, LANES), lambda p, i: (i, p)),
                   pl.BlockSpec((1, SUBLANES, t), lambda p, i: (p, 0, i))],
        compiler_params=_params(("parallel", "arbitrary"), VMEM_LIMIT),
    )(ranges, qa, ka, va)


def _dup_halves(blk, lane):
    f = blk.astype(F32)
    r = pltpu.roll(f, 64, 1)
    return jnp.where(lane < 64, f, r).astype(BF16), jnp.where(lane >= 64, f, r).astype(BF16)


GROUP = 4
GROUP_ROWS = GROUP * WINDOW


def _stack_heads(ref, g, lane):
    parts = []
    for pb in (2 * g, 2 * g + 1):
        blk = ref[:, LANES * pb:LANES * (pb + 1)]
        zero = jnp.zeros_like(blk)
        parts += [jnp.where(lane < 64, blk, zero), jnp.where(lane >= 64, blk, zero)]
    return jnp.concatenate(parts, axis=0)


def _swa_band(a_ref, ap_ref, g, lane):
    k = jnp.concatenate([_dup_halves(ap_ref[:, 512:640], lane)[g], _dup_halves(a_ref[:, 512:640], lane)[g]], axis=0)
    v = jnp.concatenate([_dup_halves(ap_ref[:, 640:768], lane)[g], _dup_halves(a_ref[:, 640:768], lane)[g]], axis=0)
    return k, v


def _swa_logits(q, k, has_prev):
    sc = _dot(q, k, NT) * SCALE
    rr = lax.broadcasted_iota(jnp.int32, sc.shape, 0) % WINDOW
    cc = lax.broadcasted_iota(jnp.int32, sc.shape, 1)
    valid = (cc > rr) & (cc <= rr + WINDOW) & (has_prev | (cc >= WINDOW))
    return jnp.where(valid, sc, NEG)


def _per_head_column(values):
    return jnp.concatenate([jnp.broadcast_to(v, (WINDOW, 1)) for v in values], axis=0)


SWA_BLOCKS = 4
SWA_ROWS = SWA_BLOCKS * WINDOW


def _swa_blocks(a_ref, ap_ref):
    return [ap_ref] + [a_ref.at[pl.ds(WINDOW * jb, WINDOW), :] for jb in range(SWA_BLOCKS)]


def _swa_fwd(a, sinks):
    s = a.shape[0]

    def body(sink_ref, a_ref, ap_ref, o_ref, l_ref):
        lane = lax.broadcasted_iota(jnp.int32, (WINDOW, LANES), 1)
        blocks = _swa_blocks(a_ref, ap_ref)
        for jb in range(SWA_BLOCKS):
            has_prev = (pl.program_id(0) > 0) if jb == 0 else True
            l_all = jnp.zeros((WINDOW, LANES), F32)
            rows = slice(WINDOW * jb, WINDOW * (jb + 1))
            for g in range(2):
                k, v = _swa_band(blocks[jb + 1], blocks[jb], g, lane)
                sc = _swa_logits(_stack_heads(blocks[jb + 1], g, lane), k, has_prev)
                sink = _per_head_column([sink_ref[GROUP * g + hh] for hh in range(GROUP)])
                m = jnp.maximum(jnp.max(sc, axis=-1, keepdims=True), sink)
                p = jnp.exp(sc - m)
                den = jnp.sum(p, axis=-1, keepdims=True) + jnp.exp(sink - m)
                out = _dot((p * (1.0 / den)).astype(BF16), v)
                lcol = m + jnp.log(den)
                for pb in range(2):
                    r0 = 2 * pb * WINDOW
                    o_ref[rows, LANES * (2 * g + pb):LANES * (2 * g + pb + 1)] = jnp.where(
                        lane < 64, out[r0:r0 + WINDOW], out[r0 + WINDOW:r0 + 2 * WINDOW]).astype(BF16)
                for hh in range(GROUP):
                    l_all = jnp.where(lane == GROUP * g + hh, lcol[WINDOW * hh:WINDOW * (hh + 1)], l_all)
            l_ref[rows, :] = l_all

    return pl.pallas_call(
        body, name="swa_fwd", grid=(s // SWA_ROWS,),
        out_shape=[jax.ShapeDtypeStruct((s, 512), BF16), jax.ShapeDtypeStruct((s, LANES), F32)],
        in_specs=[pl.BlockSpec(memory_space=pltpu.SMEM),
                  pl.BlockSpec((SWA_ROWS, W_A), lambda i: (i, 0)),
                  pl.BlockSpec((WINDOW, W_A), lambda i: (jnp.maximum(SWA_BLOCKS * i - 1, 0), 0))],
        out_specs=[pl.BlockSpec((SWA_ROWS, 512), lambda i: (i, 0)), pl.BlockSpec((SWA_ROWS, LANES), lambda i: (i, 0))],
        compiler_params=_params(("parallel",)),
    )(sinks, a, a)


def _mid(att_a, att_b, g, x, target, gate, g_final, wo_a, wo_b, w_out, tm=256):
    s = x.shape[0]
    nt = s // tm

    def body(aa_ref, ab_ref, g_ref, x_ref, t_ref, gate_ref, gf_ref, woa_ref, wob_ref, wout_ref,
             dx_ref, daa_ref, dab_ref, dg_ref, delta_ref, dwoa_ref, dwob_ref, dwout_ref, vec_ref,
             acc_gf, acc_gate, acc_loss):
        step = pl.program_id(0)

        @pl.when(step == 0)
        def _():
            dwoa_ref[...] = jnp.zeros_like(dwoa_ref)
            dwob_ref[...] = jnp.zeros_like(dwob_ref)
            dwout_ref[...] = jnp.zeros_like(dwout_ref)
            acc_gf[...] = jnp.zeros_like(acc_gf)
            acc_gate[...] = jnp.zeros_like(acc_gate)
            acc_loss[...] = jnp.zeros_like(acc_loss)

        def fold(v):
            return jnp.sum(v.reshape(tm // SUBLANES, SUBLANES, D_MODEL), axis=0)

        gate = gate_ref[...]
        gfin = gf_ref[...]
        branches = []
        for att_ref, z_off, wo_ref in ((aa_ref, 0, woa_ref), (ab_ref, 512, wob_ref)):
            att = att_ref[...].astype(F32)
            z = g_ref[:, z_off:z_off + 512].astype(F32)
            sz = _sigmoid(z)
            silu = z * sz
            u = (att * silu).astype(BF16)
            branches.append((att, z, sz, silu, u, _dot(u, wo_ref[...])))
        ga = g_ref[:, 1024:2048].astype(F32)
        gb = g_ref[:, 2048:3072].astype(F32)
        sga, sgb = _sigmoid(ga), _sigmoid(gb)
        y_a, y_b = branches[0][5], branches[1][5]
        mb = (sga * y_a + sgb * y_b).astype(BF16)
        o = _dot(mb, wout_ref[...])
        x2 = x_ref[...] + gate * o
        r2 = lax.rsqrt(jnp.mean(x2 * x2, axis=-1, keepdims=True) + NORM_EPS)
        xn2 = x2 * r2
        err = xn2 * gfin - t_ref[...]
        acc_loss[...] += fold(err * err)
        dy = err * (1.0 / D_MODEL)
        acc_gf[...] += fold(dy * xn2)
        dxn = dy * gfin
        dx2 = r2 * (dxn - xn2 * jnp.mean(dxn * xn2, axis=-1, keepdims=True))
        dx_ref[...] = dx2
        acc_gate[...] += fold(dx2 * o)
        d_o = (dx2 * gate).astype(BF16)
        dwout_ref[...] += _dot(mb, d_o, TN)
        dm = _dot(d_o, wout_ref[...], NT)
        dg_ref[:, 1024:2048] = (dm * y_a * sga * (1.0 - sga)).astype(BF16)
        dg_ref[:, 2048:3072] = (dm * y_b * sgb * (1.0 - sgb)).astype(BF16)
        for (att, z, sz, silu, u, _), sg, wo_ref, dwo_ref, datt_ref, z_off in (
                (branches[0], sga, woa_ref, dwoa_ref, daa_ref, 0), (branches[1], sgb, wob_ref, dwob_ref, dab_ref, 512)):
            dyb = (dm * sg).astype(BF16)
            dwo_ref[...] += _dot(u, dyb, TN)
            du = _dot(dyb, wo_ref[...], NT)
            datt = du * silu
            datt_ref[...] = datt.astype(BF16)
            dg_ref[:, z_off:z_off + 512] = (du * att * (sz * (1.0 + z * (1.0 - sz)))).astype(BF16)
            if z_off == 512:
                prod = datt * att
                hi = prod.astype(BF16)
                lo = (prod - hi.astype(F32)).astype(BF16)
                er = lax.broadcasted_iota(jnp.int32, (512, LANES), 0)
                ec = lax.broadcasted_iota(jnp.int32, (512, LANES), 1)
                e = (er // HEAD_DIM == ec).astype(BF16)
                delta = _dot(hi, e) + _dot(lo, e)
                delta_ref[...] = delta.T[0:SUBLANES, :]

        @pl.when(step == nt - 1)
        def _():
            sub = lax.broadcasted_iota(jnp.int32, (SUBLANES, D_MODEL), 0)
            dgf = jnp.sum(acc_gf[...], axis=0, keepdims=True)
            dgate = jnp.sum(acc_gate[...], axis=0, keepdims=True)
            loss = 0.5 * jnp.sum(acc_loss[...]) * (1.0 / D_MODEL)
            vec_ref[...] = jnp.where(sub == 0, dgf, jnp.where(sub == 1, dgate, jnp.where(sub == 2, loss, 0.0)))

    row = lambda w: pl.BlockSpec((tm, w), lambda i: (i, 0))
    return pl.pallas_call(
        body, name="mid", grid=(nt,),
        out_shape=[jax.ShapeDtypeStruct((s, D_MODEL), F32), jax.ShapeDtypeStruct((s, 512), BF16),
                   jax.ShapeDtypeStruct((s, 512), BF16), jax.ShapeDtypeStruct((s, W_G), BF16),
                   jax.ShapeDtypeStruct((SUBLANES, s), F32),
                   jax.ShapeDtypeStruct((512, D_MODEL), F32), jax.ShapeDtypeStruct((512, D_MODEL), F32),
                   jax.ShapeDtypeStruct((D_MODEL, D_MODEL), F32), jax.ShapeDtypeStruct((SUBLANES, D_MODEL), F32)],
        in_specs=[row(512), row(512), row(W_G), row(D_MODEL), row(D_MODEL),
                  _const_spec((1, D_MODEL)), _const_spec((1, D_MODEL)),
                  _const_spec((512, D_MODEL)), _const_spec((512, D_MODEL)), _const_spec((D_MODEL, D_MODEL))],
        out_specs=[row(D_MODEL), row(512), row(512), row(W_G),
                   pl.BlockSpec((SUBLANES, tm), lambda i: (0, i)),
                   pl.BlockSpec((512, D_MODEL), lambda i: (0, 0)), pl.BlockSpec((512, D_MODEL), lambda i: (0, 0)),
                   pl.BlockSpec((D_MODEL, D_MODEL), lambda i: (0, 0)), pl.BlockSpec((SUBLANES, D_MODEL), lambda i: (0, 0))],
        scratch_shapes=[pltpu.VMEM((SUBLANES, D_MODEL), F32)] * 3,
        compiler_params=_params(("arbitrary",), VMEM_LIMIT),
    )(att_a, att_b, g, x, target, gate, g_final, wo_a, wo_b, w_out)


def _rope_bwd(dt, cos, sin, lane):
    u = dt * sin
    lo = (lane % HEAD_DIM) < (HEAD_DIM // 2)
    return dt * cos + jnp.where(lo, pltpu.roll(u, 96, 1), -pltpu.roll(u, 32, 1))


def _swa_bwd(a, datt, l_all, sinks, cos, sin):
    s = a.shape[0]
    nt = s // SWA_ROWS

    def body(sink_ref, a_ref, ap_ref, do_ref, l_ref, cos_ref, sin_ref, da_ref, ds_ref, halo):
        step = pl.program_id(0)
        tile = nt - 1 - step

        @pl.when(step == 0)
        def _():
            halo[...] = jnp.zeros_like(halo)
            ds_ref[...] = jnp.zeros_like(ds_ref)

        lane = lax.broadcasted_iota(jnp.int32, (WINDOW, LANES), 1)
        sub8 = lax.broadcasted_iota(jnp.int32, (SUBLANES, LANES), 0)
        lane8 = lax.broadcasted_iota(jnp.int32, (SUBLANES, LANES), 1)
        blocks = _swa_blocks(a_ref, ap_ref)
        dsink = jnp.zeros((SUBLANES, LANES), F32)

        def join(pair, r0):
            x0, x1 = pair[0][r0:r0 + WINDOW], pair[1][r0:r0 + WINDOW]
            return jnp.where(lane < 64, x0 + pltpu.roll(x0, 64, 1), x1 + pltpu.roll(x1, 64, 1))

        carry_k, carry_v = halo[:, 0:LANES], halo[:, LANES:2 * LANES]
        for jb in reversed(range(SWA_BLOCKS)):
            has_prev = (tile > 0) if jb == 0 else True
            rows = slice(WINDOW * jb, WINDOW * (jb + 1))
            lv = l_ref[rows, :]
            cosv, sinv = cos_ref[rows, :], sin_ref[rows, :]
            do_blk = do_ref.at[pl.ds(WINDOW * jb, WINDOW), :]
            dkb, dvb = [], []
            for g in range(2):
                k, v = _swa_band(blocks[jb + 1], blocks[jb], g, lane)
                q = _stack_heads(blocks[jb + 1], g, lane)
                dom = _stack_heads(do_blk, g, lane)
                sink = _per_head_column([sink_ref[GROUP * g + hh] for hh in range(GROUP)])
                lcol = _per_head_column([lv[:, GROUP * g + hh:GROUP * g + hh + 1] for hh in range(GROUP)])
                p = jnp.exp(_swa_logits(q, k, has_prev) - lcol)
                dp = _dot(dom, v, NT)
                delta = jnp.sum(p * dp, axis=-1, keepdims=True)
                sink_term = jnp.exp(sink - lcol) * delta
                for hh in range(GROUP):
                    tot = jnp.sum(sink_term[WINDOW * hh:WINDOW * (hh + 1)])
                    dsink = dsink + jnp.where((sub8 == 0) & (lane8 == GROUP * g + hh), -tot, 0.0)
                ds = (p * (dp - delta)).astype(BF16)
                dq = _dot(ds, k) * SCALE
                for pb in range(2):
                    r0 = 2 * pb * WINDOW
                    dq_pair = jnp.where(lane < 64, dq[r0:r0 + WINDOW], dq[r0 + WINDOW:r0 + 2 * WINDOW])
                    da_ref[rows, LANES * (2 * g + pb):LANES * (2 * g + pb + 1)] = _rope_bwd(
                        dq_pair, cosv, sinv, lane).astype(BF16)
                dkb.append(_dot(ds, q, TN) * SCALE)
                dvb.append(_dot(p.astype(BF16), dom, TN))
            da_ref[rows, 512:640] = _rope_bwd(join(dkb, WINDOW) + carry_k, cosv, sinv, lane).astype(BF16)
            da_ref[rows, 640:768] = (join(dvb, WINDOW) + carry_v).astype(BF16)
            carry_k, carry_v = join(dkb, 0), join(dvb, 0)
        halo[:, 0:LANES] = carry_k
        halo[:, LANES:2 * LANES] = carry_v
        ds_ref[...] += dsink

    rev = lambda w: pl.BlockSpec((SWA_ROWS, w), lambda i: (nt - 1 - i, 0))
    return pl.pallas_call(
        body, name="swa_bwd", grid=(nt,),
        out_shape=[jax.ShapeDtypeStruct((s, W_A), BF16), jax.ShapeDtypeStruct((SUBLANES, LANES), F32)],
        in_specs=[pl.BlockSpec(memory_space=pltpu.SMEM), rev(W_A),
                  pl.BlockSpec((WINDOW, W_A), lambda i: (jnp.maximum(SWA_BLOCKS * (nt - 1 - i) - 1, 0), 0)),
                  rev(512), rev(LANES), rev(LANES), rev(LANES)],
        out_specs=[rev(W_A), pl.BlockSpec((SUBLANES, LANES), lambda i: (0, 0))],
        scratch_shapes=[pltpu.VMEM((WINDOW, 2 * LANES), F32)],
        compiler_params=_params(("arbitrary",)),
    )(sinks, a, a, datt, l_all, cos, sin)


def _fox_bwd(qa, ka, b, do, lse, delta, ranges, t):
    s = qa.shape[0]
    nt = s // t

    def body(rg_ref, q_ref, do_ref, lse_ref, dl_ref, k_ref, v_ref, dq_ref, dk_ref, dv_ref, dc_ref, dr_ref, dq_acc):
        p = pl.program_id(0)
        j = pl.program_id(1)
        n_query = jnp.clip(_lane_scalar(rg_ref[0], 1, j), 1, nt - j)

        @pl.when(j == 0)
        def _():
            dq_acc[...] = jnp.zeros_like(dq_acc)

        lane = lax.broadcasted_iota(jnp.int32, (t, LANES), 1)
        rows = lax.broadcasted_iota(jnp.int32, (t, t), 0)
        cols = lax.broadcasted_iota(jnp.int32, (t, t), 1)
        kt = k_ref[...]
        vt = v_ref[...]

        def tile(i, carry, diagonal):
            dk0, dk1, dv = carry
            off = pl.multiple_of(i * t, t)
            qt = q_ref[pl.ds(off, t), :]
            dot_ = do_ref[pl.ds(off, t), :]
            lse_t = lse_ref[0, :, pl.ds(off, t)]
            dl_t = dl_ref[0, :, pl.ds(off, t)]
            dks = []
            for hh in range(2):
                q = qt[:, LANES * hh:LANES * (hh + 1)]
                k = kt[:, LANES * hh:LANES * (hh + 1)]
                st = _dot(k, q, NT)
                if diagonal:
                    st = jnp.where(cols >= rows, st, NEG)
                pt = jnp.exp(st - lse_t[hh:hh + 1, :])
                dom = jnp.where((lane < 64) if hh == 0 else (lane >= 64), dot_, jnp.zeros_like(dot_))
                dv = dv + _dot(pt.astype(BF16), dom)
                dpt = _dot(vt, dom, NT)
                dst = (pt * (dpt - dl_t[hh:hh + 1, :])).astype(BF16)
                dks.append(_dot(dst, q))
                dq_acc[hh, pl.ds(off, t), :] += _dot(dst, k, TN)
            return dk0 + dks[0], dk1 + dks[1], dv

        zero = jnp.zeros((t, LANES), F32)
        carry = tile(j, (zero, zero, zero), True)
        dk0, dk1, dv = lax.fori_loop(j + 1, j + n_query, lambda i, cr: tile(i, cr, False), carry)
        dk_ref[...] = jnp.where(lane < 64, dk0, pltpu.roll(dk1, 64, 1)).astype(BF16)
        dv_ref[...] = dv.astype(BF16)
        c0 = jnp.broadcast_to(dk0[:, 67:68], (t, LANES))
        c1 = jnp.broadcast_to(dk1[:, 67:68], (t, LANES))
        dc_ref[0] = jnp.where(lane == 2 * p, -c0, jnp.where(lane == 2 * p + 1, -c1, 0.0))

        @pl.when(j == nt - 1)
        def _():
            lane_s = lax.broadcasted_iota(jnp.int32, (s, LANES), 1)
            a0, a1 = dq_acc[0], dq_acc[1]
            dq_ref[...] = (jnp.where(lane_s < 64, a0, pltpu.roll(a1, 64, 1)) * SCALE).astype(BF16)
            r0 = jnp.broadcast_to(a0[:, 64:65], (s, LANES))
            r1 = jnp.broadcast_to(a1[:, 64:65], (s, LANES))
            dr_ref[0] = jnp.where(lane_s == 2 * p, r0, jnp.where(lane_s == 2 * p + 1, r1, 0.0))

    return pl.pallas_call(
        body, name="fox_bwd", grid=(4, nt),
        out_shape=[jax.ShapeDtypeStruct((s, 512), BF16), jax.ShapeDtypeStruct((s, 512), BF16),
                   jax.ShapeDtypeStruct((s, 512), BF16), jax.ShapeDtypeStruct((4, s, LANES), F32),
                   jax.ShapeDtypeStruct((4, s, LANES), F32)],
        in_specs=[pl.BlockSpec((1, SUBLANES, LANES), lambda p, j: (p, 0, 0)),
                  pl.BlockSpec((s, 2 * LANES), lambda p, j: (0, p)),
                  pl.BlockSpec((s, LANES), lambda p, j: (0, p)),
                  pl.BlockSpec((1, SUBLANES, s), lambda p, j: (p, 0, 0)),
                  pl.BlockSpec((1, SUBLANES, s), lambda p, j: (p, 0, 0)),
                  pl.BlockSpec((t, 2 * LANES), lambda p, j: (j, p)),
                  pl.BlockSpec((t, LANES), lambda p, j: (j, 8 + p))],
        out_specs=[pl.BlockSpec((s, LANES), lambda p, j: (0, p)),
                   pl.BlockSpec((t, LANES), lambda p, j: (j, p)),
                   pl.BlockSpec((t, LANES), lambda p, j: (j, p)),
                   pl.BlockSpec((1, t, LANES), lambda p, j: (p, j, 0)),
                   pl.BlockSpec((1, s, LANES), lambda p, j: (p, 0, 0))],
        scratch_shapes=[pltpu.VMEM((2, s, LANES), F32)],
        compiler_params=_params(("parallel", "arbitrary"), VMEM_LIMIT),
    )(ranges, qa, do, lse, delta, ka, b)


def _fox_cumsum_bwd(dcum_k, dcum_q, f, bf_pad, tb=256):
    s = f.shape[0]
    nb = s // tb

    def body(dc_ref, dr_ref, f_ref, b_ref, df_ref, db_ref, carry):
        step = pl.program_id(0)

        @pl.when(step == 0)
        def _():
            carry[...] = jnp.zeros_like(carry)
            db_ref[...] = jnp.zeros_like(db_ref)

        lane = lax.broadcasted_iota(jnp.int32, (tb, LANES), 1)
        dc = dc_ref[0] + dr_ref[0]
        for k in range(1, 4):
            dc = dc + (dc_ref[k] + dr_ref[k])
        hi, mid, lo = _split3(dc)
        rows = lax.broadcasted_iota(jnp.int32, (tb, tb), 0)
        cols = lax.broadcasted_iota(jnp.int32, (tb, tb), 1)
        triu = (cols >= rows).astype(BF16)
        dlogf = _dot(triu, hi) + _dot(triu, mid) + _dot(triu, lo) + carry[0:1, :]
        carry[...] = jnp.broadcast_to(dlogf[0:1, :], carry.shape)
        u = f_ref[...] + b_ref[...]
        dfb = jnp.where(lane < N_HEADS, dlogf * _sigmoid(-u), 0.0)
        df_ref[...] = dfb.astype(BF16)
        sub = lax.broadcasted_iota(jnp.int32, (SUBLANES, LANES), 0)
        db_ref[...] += jnp.where(sub == 0, jnp.sum(dfb, axis=0, keepdims=True), 0.0)

    return pl.pallas_call(
        body, name="fox_cumsum_bwd", grid=(nb,),
        out_shape=[jax.ShapeDtypeStruct((s, LANES), BF16), jax.ShapeDtypeStruct((SUBLANES, LANES), F32)],
        in_specs=[pl.BlockSpec((4, tb, LANES), lambda i: (0, nb - 1 - i, 0)),
                  pl.BlockSpec((4, tb, LANES), lambda i: (0, nb - 1 - i, 0)),
                  pl.BlockSpec((tb, LANES), lambda i: (nb - 1 - i, 0)), _const_spec((1, LANES))],
        out_specs=[pl.BlockSpec((tb, LANES), lambda i: (nb - 1 - i, 0)),
                   pl.BlockSpec((SUBLANES, LANES), lambda i: (0, 0))],
        scratch_shapes=[pltpu.VMEM((SUBLANES, LANES), F32)],
        compiler_params=_params(("arbitrary",)),
    )(dcum_k, dcum_q, f, bf_pad)


def _dh_norm_bwd(d_a, d_q, d_k, d_v, d_f, d_g, w_t, x, dx2, gnorm, scale1, tm=512):
    s = x.shape[0]
    nt = s // tm

    def body(da_ref, dq_ref, dk_ref, dv_ref, df_ref, dg_ref, w_ref, x_ref, dx2_ref, g_ref, sc_ref, gx_ref, vec_ref,
             a_sh, a_sc, a_g):
        step = pl.program_id(0)

        @pl.when(step == 0)
        def _():
            a_sh[...] = jnp.zeros_like(a_sh)
            a_sc[...] = jnp.zeros_like(a_sc)
            a_g[...] = jnp.zeros_like(a_g)

        def fold(v):
            return jnp.sum(v.reshape(tm // SUBLANES, SUBLANES, D_MODEL), axis=0)

        dh = (_dot(da_ref[...], w_ref[OFF_A:OFF_A + W_A, :]) + _dot(dq_ref[...], w_ref[OFF_B:OFF_B + 512, :])
              + _dot(dk_ref[...], w_ref[OFF_B + 512:OFF_B + 1024, :]) + _dot(dv_ref[...], w_ref[OFF_B + 1024:OFF_B + W_B, :])
              + _dot(df_ref[...], w_ref[OFF_F:OFF_F + W_F, :]) + _dot(dg_ref[...], w_ref[OFF_G:OFF_G + W_G, :]))
        xv = x_ref[...]
        r = lax.rsqrt(jnp.mean(xv * xv, axis=-1, keepdims=True) + NORM_EPS)
        xn = xv * r
        gn = g_ref[...]
        a_sh[...] += fold(dh)
        a_sc[...] += fold(dh * (xn * gn))
        dn1 = dh * sc_ref[...]
        a_g[...] += fold(dn1 * xn)
        dxn = dn1 * gn
        gx_ref[...] = dx2_ref[...] + r * (dxn - xn * jnp.mean(dxn * xn, axis=-1, keepdims=True))

        @pl.when(step == nt - 1)
        def _():
            sub = lax.broadcasted_iota(jnp.int32, (SUBLANES, D_MODEL), 0)
            v_sh = jnp.sum(a_sh[...], axis=0, keepdims=True)
            v_sc = jnp.sum(a_sc[...], axis=0, keepdims=True)
            v_g = jnp.sum(a_g[...], axis=0, keepdims=True)
            vec_ref[...] = jnp.where(sub == 0, v_sh, jnp.where(sub == 1, v_sc, jnp.where(sub == 2, v_g, 0.0)))

    row = lambda w: pl.BlockSpec((tm, w), lambda i: (i, 0))
    return pl.pallas_call(
        body, name="dh_norm_bwd", grid=(nt,),
        out_shape=[jax.ShapeDtypeStruct((s, D_MODEL), F32), jax.ShapeDtypeStruct((SUBLANES, D_MODEL), F32)],
        in_specs=[row(W_A), row(512), row(512), row(512), row(W_F), row(W_G), _const_spec((W_INT, D_MODEL)),
                  row(D_MODEL), row(D_MODEL), _const_spec((1, D_MODEL)), _const_spec((1, D_MODEL))],
        out_specs=[row(D_MODEL), pl.BlockSpec((SUBLANES, D_MODEL), lambda i: (0, 0))],
        scratch_shapes=[pltpu.VMEM((SUBLANES, D_MODEL), F32)] * 3,
        compiler_params=_params(("arbitrary",), VMEM_LIMIT),
    )(d_a, d_q, d_k, d_v, d_f, d_g, w_t, x, dx2, gnorm, scale1)


def _dw_in(h_t, d, name, tn, ts=1024):
    s, n = d.shape
    ns = s // ts

    def body(h_ref, d_ref, o_ref, acc):
        k = pl.program_id(1)

        @pl.when(k == 0)
        def _():
            acc[...] = jnp.zeros_like(acc)

        acc[...] += _dot(h_ref[...], d_ref[...])

        @pl.when(k == ns - 1)
        def _():
            o_ref[...] = acc[...]

    return pl.pallas_call(
        body, name=name, grid=(n // tn, ns),
        out_shape=jax.ShapeDtypeStruct((D_MODEL, n), F32),
        in_specs=[pl.BlockSpec((D_MODEL, ts), lambda jn, k: (0, k)), pl.BlockSpec((ts, tn), lambda jn, k: (k, jn))],
        out_specs=pl.BlockSpec((D_MODEL, tn), lambda jn, k: (0, jn)),
        scratch_shapes=[pltpu.VMEM((D_MODEL, tn), F32)],
        compiler_params=_params(("parallel", "arbitrary"), VMEM_LIMIT),
    )(h_t, d)


def _small_grads(packs, c_t, dada_shard):
    def body(p_ref, ct_ref, da_ref, sum_ref, gw_ref):
        acc = p_ref[0]
        for dev in range(1, 8):
            acc = acc + p_ref[dev]
        sum_ref[...] = acc
        gw_ref[...] = jnp.dot(ct_ref[...], da_ref[...], preferred_element_type=F32, precision=lax.Precision.HIGHEST)

    return pl.pallas_call(
        body, name="small_grads",
        out_shape=[jax.ShapeDtypeStruct(packs.shape[1:], F32),
                   jax.ShapeDtypeStruct((c_t.shape[0], dada_shard.shape[1]), F32)],
    )(packs, c_t, dada_shard)


def _adamw(w, g, m, v, name):
    r, c = w.shape
    tr = 128 if r % 128 == 0 else r
    c1 = 1.0 / (1.0 - ADAM_B1 ** ADAM_STEP)
    c2 = 1.0 / (1.0 - ADAM_B2 ** ADAM_STEP)

    def body(w_ref, g_ref, m_ref, v_ref, d_ref, mo_ref, vo_ref):
        gv = g_ref[...]
        mn = ADAM_B1 * m_ref[...] + (1.0 - ADAM_B1) * gv
        vn = ADAM_B2 * v_ref[...] + (1.0 - ADAM_B2) * (gv * gv)
        mo_ref[...] = mn
        vo_ref[...] = vn
        d_ref[...] = -ADAM_LR * ((mn * c1) / (jnp.sqrt(vn * c2) + ADAM_EPS) + ADAM_WD * w_ref[...])

    spec = pl.BlockSpec((tr, c), lambda i: (i, 0))
    return pl.pallas_call(
        body, name=name, grid=(r // tr,),
        out_shape=[jax.ShapeDtypeStruct((r, c), F32)] * 3,
        in_specs=[spec] * 4, out_specs=[spec] * 3,
        compiler_params=_params(("parallel",)),
    )(w, g, m, v)


def _rope_tables(positions):
    inv_freq = 10000.0 ** (-jnp.arange(0, HEAD_DIM, 2, dtype=F32) / HEAD_DIM)
    ang = positions.astype(F32)[:, None] * inv_freq
    cos, sin = jnp.tile(jnp.cos(ang), (1, 4)), jnp.tile(jnp.sin(ang), (1, 4))
    lo = (jnp.arange(LANES) % HEAD_DIM) < (HEAD_DIM // 2)
    return cos, sin, jnp.where(lo, -sin, 0.0), jnp.where(lo, 0.0, sin)


def _pad_rows(v, rows=SUBLANES):
    return jnp.pad(v, ((0, rows - v.shape[0]), (0, 0)))


def kernel(x, c, positions, w_ada, b_ada, g_norm, w_in, b_f, sinks, w_o_swa, w_o_fox, w_out, g_final, loss_target, m_w_ada, m_b_ada, m_g_norm, m_w_in, m_b_f, m_sinks, m_w_o_swa, m_w_o_fox, m_w_out, m_g_final, v_w_ada, v_b_ada, v_g_norm, v_w_in, v_b_f, v_sinks, v_w_o_swa, v_w_o_fox, v_w_out, v_g_final):
    ix, iy, ic = lax.axis_index("x"), lax.axis_index("y"), lax.axis_index("c")
    chip = 2 * ix + iy
    dev = 2 * chip + ic
    xs, tgt = x[0], loss_target[0]
    s = xs.shape[0]

    c_all = _allgather_small(_pad_rows(c), "gather_c")[:, 0, :]
    b_ada_shard = lax.dynamic_slice(b_ada, (0, chip * 768), (1, 768))
    ada_all = _allgather_small(_ada_part(c_all, w_ada[0], b_ada_shard), "gather_ada")
    ada = lax.dynamic_index_in_dim(ada_all[::2], dev, axis=1, keepdims=False).reshape(1, 3 * D_MODEL)
    shift, scale, gate = ada[:, :D_MODEL], ada[:, D_MODEL:2 * D_MODEL], ada[:, 2 * D_MODEL:]
    scale1 = 1.0 + scale

    g_in, g_oa, g_ob, g_out = _allgather_weights([w_in[0], w_o_swa[0], w_o_fox[0], w_out[0]], "gather_weights")
    w_ref_order = jnp.transpose(g_in, (1, 0, 2)).reshape(D_MODEL, R_END)
    w_int = jnp.concatenate([
        w_ref_order[:, :R_ZA], w_ref_order[:, R_QB:R_FB], w_ref_order[:, R_FB:R_ZB],
        jnp.zeros((D_MODEL, W_F - N_HEADS), BF16), w_ref_order[:, R_ZA:R_QB], w_ref_order[:, R_ZB:]], axis=1)
    w_int_t = w_int.T
    wo_a = jnp.transpose(g_oa, (1, 0, 2)).reshape(512, D_MODEL)
    wo_b = jnp.transpose(g_ob, (1, 0, 2)).reshape(512, D_MODEL)
    w_o = g_out.reshape(D_MODEL, D_MODEL)

    cos, sin, sin_lo, sin_hi = _rope_tables(positions[0])
    bf_pad = jnp.pad(b_f, ((0, 0), (0, LANES - N_HEADS)))
    sink_vec = sinks[0]

    a, b, f, g, h_t = _norm_proj(xs, g_norm * scale1, shift, w_int, cos, sin_lo, sin_hi)
    att_a, l_swa = _swa_fwd(a, sink_vec)
    cum = _fox_cumsum(f, bf_pad)
    qa, ka, va, stats = _fox_prep(b, cum, FOX_TILE)
    ranges = _fox_tile_ranges(stats)
    att_b, lse = _fox_fwd(qa, ka, va, ranges, FOX_TILE)

    dx2, datt_a, datt_b, d_g, delta8, dwo_a, dwo_b, dw_out, vec_mid = _mid(
        att_a, att_b, g, xs, tgt, gate, g_final.reshape(1, D_MODEL), wo_a, wo_b, w_o)
    delta = jnp.pad(delta8.reshape(4, 2, s), ((0, 0), (0, SUBLANES - 2), (0, 0)))
    d_a, dsink = _swa_bwd(a, datt_a, l_swa, sink_vec, cos, sin)
    dq, dk, dv, dcum_k, dcum_q = _fox_bwd(qa, ka, b, datt_b, lse, delta, ranges, FOX_TILE)
    d_f, dbf = _fox_cumsum_bwd(dcum_k, dcum_q, f, bf_pad)
    grad_x, vec_dh = _dh_norm_bwd(d_a, dq, dk, dv, d_f, d_g, w_int_t, xs, dx2, g_norm, scale1)
    dw_a = _dw_in(h_t, d_a, "dw_in_a", 768)
    dw_q = _dw_in(h_t, dq, "dw_in_q", 512)
    dw_k = _dw_in(h_t, dk, "dw_in_k", 512)
    dw_v = _dw_in(h_t, dv, "dw_in_v", 512)
    dw_f = _dw_in(h_t, d_f, "dw_in_f", 128)
    dw_g = _dw_in(h_t, d_g, "dw_in_g", 1024)
    dw_in = jnp.concatenate([dw_a, dw_g[:, :512], dw_q, dw_k, dw_v, dw_f[:, :N_HEADS], dw_g[:, 512:]], axis=1)

    tail = jnp.pad(jnp.concatenate([dbf[0:1, :N_HEADS], dsink[0:1, :N_HEADS]], axis=1), ((0, 0), (0, D_MODEL - 2 * N_HEADS)))
    pack = jnp.concatenate([c, vec_dh[0:2], vec_mid[1:2], vec_dh[2:3], vec_mid[0:1], tail, vec_mid[2:3]], axis=0)
    packs = _allgather_small(pack, "gather_small")
    dada_all = packs[:, 1:4, :].reshape(8, 3 * D_MODEL)
    dada_shard = lax.dynamic_slice(dada_all, (0, chip * 768), (8, 768))
    sums, g_w_ada = _small_grads(packs, packs[:, 0, :].T, dada_shard)
    g_b_ada = sums[1:4].reshape(1, 3 * D_MODEL)
    g_g_norm = sums[4:5]
    g_g_final = sums[5]
    g_b_f = sums[6:7, :N_HEADS]
    g_sinks = sums[6:7, N_HEADS:2 * N_HEADS]
    loss = sums[7, 0]

    def slots(w, axis):
        if axis == 1:
            return jnp.transpose(w.reshape(w.shape[0], 4, w.shape[1] // 4), (1, 0, 2))
        return w.reshape(4, w.shape[0] // 4, w.shape[1])

    (g_w_in,) = _reduce_scatter([slots(dw_in, 1)], "reduce_w_in")
    g_wo_a, g_wo_b, g_w_out = _reduce_scatter([slots(dwo_a, 1), slots(dwo_b, 1), slots(dw_out, 0)], "reduce_w_small")

    grads = {
        "w_ada": g_w_ada, "b_ada": g_b_ada, "g_norm": g_g_norm, "w_in": g_w_in, "b_f": g_b_f, "sinks": g_sinks,
        "w_o_swa": g_wo_a, "w_o_fox": g_wo_b, "w_out": g_w_out, "g_final": g_g_final,
    }
    params = {
        "w_ada": (w_ada, m_w_ada, v_w_ada), "b_ada": (b_ada, m_b_ada, v_b_ada), "g_norm": (g_norm, m_g_norm, v_g_norm),
        "w_in": (w_in, m_w_in, v_w_in), "b_f": (b_f, m_b_f, v_b_f), "sinks": (sinks, m_sinks, v_sinks),
        "w_o_swa": (w_o_swa, m_w_o_swa, v_w_o_swa), "w_o_fox": (w_o_fox, m_w_o_fox, v_w_o_fox),
        "w_out": (w_out, m_w_out, v_w_out), "g_final": (g_final, m_g_final, v_g_final),
    }
    names = list(grads)
    out_g, out_d, out_m, out_v = [], [], [], []
    for nm in names:
        w, m, v = params[nm]
        shape2 = (w.shape[-2], w.shape[-1]) if w.ndim >= 2 else (1, w.shape[0])
        d_, m_, v_ = _adamw(w.reshape(shape2), grads[nm].reshape(shape2), m.reshape(shape2), v.reshape(shape2), "adamw_" + nm)
        out_g.append(grads[nm].reshape(w.shape))
        out_d.append(d_.reshape(w.shape))
        out_m.append(m_.reshape(w.shape))
        out_v.append(v_.reshape(w.shape))
    return (loss, grad_x[None], *out_g, *out_d, *out_m, *out_v)
```

```python
import functools

import numpy as np
import jax
import jax.numpy as jnp
from jax import lax
from jax.experimental import pallas as pl
from jax.experimental.pallas import tpu as pltpu

F32 = jnp.float32
BF16 = jnp.bfloat16
MESH = pl.DeviceIdType.MESH

D_MODEL = 1024
HEAD_DIM = 64
N_HEADS = 8
WINDOW = 128
NORM_EPS = 1e-6
SCALE = HEAD_DIM ** -0.5
NEG = -1e30
LANES = 128
SUBLANES = 8
VMEM_LIMIT = 60 * 1024 * 1024
FOX_TILE = 512

W_A, W_B, W_F, W_G = 768, 1536, 128, 3072
OFF_A, OFF_B, OFF_F, OFF_G = 0, 768, 2304, 2432
W_INT = W_A + W_B + W_F + W_G
R_ZA, R_QB, R_FB, R_ZB, R_END = 768, 1280, 2816, 2824, 5384

ADAM_LR, ADAM_B1, ADAM_B2, ADAM_EPS, ADAM_WD, ADAM_STEP = 0.001, 0.9, 0.999, 1e-08, 0.01, 10

NT = (((1,), (1,)), ((), ()))
TN = (((0,), (0,)), ((), ()))


def _dot(a, b, dims=None):
    if dims is None:
        return jnp.dot(a, b, preferred_element_type=F32)
    return lax.dot_general(a, b, dims, preferred_element_type=F32)


def _split3(v):
    hi = v.astype(BF16)
    r1 = v - hi.astype(F32)
    mid = r1.astype(BF16)
    lo = (r1 - mid.astype(F32)).astype(BF16)
    return hi, mid, lo


def _sigmoid(v):
    return 1.0 / (1.0 + jnp.exp(-v))


def _params(sem=None, vmem=None):
    return pltpu.CompilerParams(dimension_semantics=sem, vmem_limit_bytes=vmem)


def _const_spec(shape):
    nd = len(shape)
    return pl.BlockSpec(shape, lambda *_: (0,) * nd, pipeline_mode=pl.Buffered(1))


def _flip(v, f):
    return 1 - v if f else v


def _allgather_small(v, name):
    r, n = v.shape

    def body(v_ref, out_ref, send_sems, recv_sems):
        x, y, c = lax.axis_index("x"), lax.axis_index("y"), lax.axis_index("c")
        me = 4 * x + 2 * y + c
        out_ref[me] = v_ref[...]
        peers = []
        for k in range(1, 8):
            peers.append((_flip(x, k & 4), _flip(y, k & 2), _flip(c, k & 1)))
        sends = []
        for k, peer in enumerate(peers):
            cp = pltpu.make_async_remote_copy(
                src_ref=v_ref, dst_ref=out_ref.at[me], send_sem=send_sems.at[k], recv_sem=recv_sems.at[k],
                device_id=peer, device_id_type=MESH)
            cp.start()
            sends.append(cp)
        for k, peer in enumerate(peers):
            src = 4 * peer[0] + 2 * peer[1] + peer[2]
            pltpu.make_async_remote_copy(
                src_ref=v_ref, dst_ref=out_ref.at[src], send_sem=send_sems.at[k], recv_sem=recv_sems.at[k],
                device_id=peer, device_id_type=MESH).wait_recv()
        for cp in sends:
            cp.wait_send()

    return pl.pallas_call(
        body, name=name,
        out_shape=jax.ShapeDtypeStruct((8, r, n), F32),
        in_specs=[pl.BlockSpec(memory_space=pltpu.VMEM)],
        out_specs=pl.BlockSpec(memory_space=pltpu.VMEM),
        scratch_shapes=[pltpu.SemaphoreType.DMA((7,)), pltpu.SemaphoreType.DMA((7,))],
    )(v)


_CHIP_FLIPS = ((1, 0), (0, 1), (1, 1))


def _allgather_weights(shards, name):
    n = len(shards)

    def body(*refs):
        ins, outs = refs[:n], refs[n:2 * n]
        send_sems, recv_sems = refs[2 * n], refs[2 * n + 1]
        x, y, c = lax.axis_index("x"), lax.axis_index("y"), lax.axis_index("c")
        k_me = 2 * x + y
        sibling = (x, y, 1 - c)
        chips = [(_flip(x, fx), _flip(y, fy)) for fx, fy in _CHIP_FLIPS]

        def piece(i, chip_k, half):
            hr = ins[i].shape[0] // 2
            return outs[i].at[chip_k, pl.ds(half * hr, hr), :]

        def copy(i, slot, chip_k, half, to):
            return pltpu.make_async_remote_copy(
                src_ref=piece(i, chip_k, half), dst_ref=piece(i, chip_k, half),
                send_sem=send_sems.at[6 * i + slot], recv_sem=recv_sems.at[6 * i + slot],
                device_id=to, device_id_type=MESH)

        for i in range(n):
            outs[i][k_me] = ins[i][...].astype(BF16)
        started = []
        for i in range(n):
            for j, chip in enumerate(chips):
                cp = copy(i, j, k_me, c, (chip[0], chip[1], c))
                cp.start()
                started.append(cp)
        for j, chip in enumerate(chips):
            chip_k = 2 * chip[0] + chip[1]
            for i in range(n):
                copy(i, j, chip_k, c, (chip[0], chip[1], c)).wait_recv()
                cp = copy(i, 3 + j, chip_k, c, sibling)
                cp.start()
                started.append(cp)
        for j, chip in enumerate(chips):
            chip_k = 2 * chip[0] + chip[1]
            for i in range(n):
                copy(i, 3 + j, chip_k, 1 - c, sibling).wait_recv()
        for cp in started:
            cp.wait_send()

    return pl.pallas_call(
        body, name=name,
        out_shape=[jax.ShapeDtypeStruct((4,) + s.shape, BF16) for s in shards],
        in_specs=[pl.BlockSpec(memory_space=pltpu.VMEM)] * n,
        out_specs=[pl.BlockSpec(memory_space=pltpu.VMEM)] * n,
        scratch_shapes=[pltpu.SemaphoreType.DMA((6 * n,)), pltpu.SemaphoreType.DMA((6 * n,))],
        compiler_params=_params(vmem=VMEM_LIMIT),
    )(*shards)


def _reduce_scatter(pieces, name):
    n = len(pieces)

    def body(*refs):
        ins, outs = refs[:n], refs[n:2 * n]
        own, got = refs[2 * n:3 * n], refs[3 * n:4 * n]
        sendb, recvb = refs[4 * n:5 * n], refs[5 * n:6 * n]
        send_sems, recv_sems, local_sems = refs[6 * n:6 * n + 3]
        x, y, c = lax.axis_index("x"), lax.axis_index("y"), lax.axis_index("c")
        k_me = 2 * x + y
        sibling = (x, y, 1 - c)
        chips = [(_flip(x, fx), _flip(y, fy)) for fx, fy in _CHIP_FLIPS]
        hrs = [p.shape[1] // 2 for p in pieces]

        def remote(i, slot, src, dst, to):
            return pltpu.make_async_remote_copy(
                src_ref=src, dst_ref=dst, send_sem=send_sems.at[5 * i + slot], recv_sem=recv_sems.at[5 * i + slot],
                device_id=to, device_id_type=MESH)

        started = []
        loads = []
        for i in range(n):
            ld = pltpu.make_async_copy(ins[i].at[:, pl.ds(c * hrs[i], hrs[i]), :], own[i], local_sems.at[i])
            ld.start()
            loads.append(ld)
            cp = remote(i, 0, ins[i].at[:, pl.ds((1 - c) * hrs[i], hrs[i]), :], got[i], sibling)
            cp.start()
            started.append(cp)
        for i in range(n):
            loads[i].wait()
            remote(i, 0, ins[i].at[:, pl.ds(c * hrs[i], hrs[i]), :], got[i], sibling).wait_recv()
            for j, chip in enumerate(chips):
                chip_k = 2 * chip[0] + chip[1]
                sendb[i][j] = (own[i][chip_k] + got[i][chip_k]).astype(BF16)
                cp = remote(i, 1 + j, sendb[i].at[j], recvb[i].at[j], (chip[0], chip[1], c))
                cp.start()
                started.append(cp)
        for i in range(n):
            acc = own[i][k_me] + got[i][k_me]
            for j, chip in enumerate(chips):
                remote(i, 1 + j, sendb[i].at[j], recvb[i].at[j], (chip[0], chip[1], c)).wait_recv()
                acc = acc + recvb[i][j].astype(F32)
            mine = outs[i].at[pl.ds(c * hrs[i], hrs[i]), :]
            outs[i][pl.ds(pl.multiple_of(c * hrs[i], SUBLANES), hrs[i]), :] = acc
            cp = remote(i, 4, mine, mine, sibling)
            cp.start()
            started.append(cp)
        for i in range(n):
            theirs = outs[i].at[pl.ds((1 - c) * hrs[i], hrs[i]), :]
            remote(i, 4, theirs, theirs, sibling).wait_recv()
        for cp in started:
            cp.wait_send()

    scratch = []
    scratch += [pltpu.VMEM((4, p.shape[1] // 2, p.shape[2]), F32) for p in pieces]
    scratch += [pltpu.VMEM((4, p.shape[1] // 2, p.shape[2]), F32) for p in pieces]
    scratch += [pltpu.VMEM((3, p.shape[1] // 2, p.shape[2]), BF16) for p in pieces]
    scratch += [pltpu.VMEM((3, p.shape[1] // 2, p.shape[2]), BF16) for p in pieces]
    scratch += [pltpu.SemaphoreType.DMA((5 * n,)), pltpu.SemaphoreType.DMA((5 * n,)), pltpu.SemaphoreType.DMA((n,))]
    return pl.pallas_call(
        body, name=name,
        out_shape=[jax.ShapeDtypeStruct(p.shape[1:], F32) for p in pieces],
        in_specs=[pl.BlockSpec(memory_space=pl.ANY)] * n,
        out_specs=[pl.BlockSpec(memory_space=pltpu.VMEM)] * n,
        scratch_shapes=scratch,
        compiler_params=_params(vmem=VMEM_LIMIT),
    )(*pieces)


def _ada_part(c_all, w_shard, b_shard):
    def body(c_ref, w_ref, b_ref, o_ref):
        o_ref[...] = _dot(c_ref[...].astype(BF16), w_ref[...].astype(BF16)) + b_ref[...]

    return pl.pallas_call(
        body, name="ada_part",
        out_shape=jax.ShapeDtypeStruct((c_all.shape[0], w_shard.shape[1]), F32),
    )(c_all, w_shard, b_shard)


def _rope_fwd(t, cos, sin_lo, sin_hi):
    return t * cos + pltpu.roll(t, 96, 1) * sin_lo + pltpu.roll(t, 32, 1) * sin_hi


def _norm_proj(x, gmod, shift, w_int, cos, sin_lo, sin_hi, tm=512):
    s = x.shape[0]

    def body(x_ref, g_ref, sh_ref, w_ref, cos_ref, sl_ref, sh2_ref, a_ref, b_ref, f_ref, gg_ref, ht_ref):
        xv = x_ref[...]
        r = lax.rsqrt(jnp.mean(xv * xv, axis=-1, keepdims=True) + NORM_EPS)
        hf = (xv * r) * g_ref[...] + sh_ref[...]
        hb = hf.astype(BF16)
        ht_ref[...] = hf.T.astype(BF16)
        pa = _dot(hb, w_ref[:, OFF_A:OFF_A + W_A])
        cosv, sl, sh2 = cos_ref[...], sl_ref[...], sh2_ref[...]
        for j in range(5):
            t = pa[:, LANES * j:LANES * (j + 1)]
            a_ref[:, LANES * j:LANES * (j + 1)] = _rope_fwd(t, cosv, sl, sh2).astype(BF16)
        a_ref[:, 640:768] = pa[:, 640:768].astype(BF16)
        b_ref[...] = _dot(hb, w_ref[:, OFF_B:OFF_B + W_B]).astype(BF16)
        f_ref[...] = _dot(hb, w_ref[:, OFF_F:OFF_F + W_F])
        gg_ref[...] = _dot(hb, w_ref[:, OFF_G:OFF_G + W_G]).astype(BF16)

    row = lambda w: pl.BlockSpec((tm, w), lambda i: (i, 0))
    return pl.pallas_call(
        body, name="norm_proj", grid=(s // tm,),
        out_shape=[jax.ShapeDtypeStruct((s, W_A), BF16), jax.ShapeDtypeStruct((s, W_B), BF16),
                   jax.ShapeDtypeStruct((s, W_F), F32), jax.ShapeDtypeStruct((s, W_G), BF16),
                   jax.ShapeDtypeStruct((D_MODEL, s), BF16)],
        in_specs=[row(D_MODEL), _const_spec((1, D_MODEL)), _const_spec((1, D_MODEL)), _const_spec((D_MODEL, W_INT)),
                  row(LANES), row(LANES), row(LANES)],
        out_specs=[row(W_A), row(W_B), row(W_F), row(W_G), pl.BlockSpec((D_MODEL, tm), lambda i: (0, i))],
        compiler_params=_params(("parallel",), VMEM_LIMIT),
    )(x, gmod, shift, w_int, cos, sin_lo, sin_hi)


def _log_sigmoid(u):
    return jnp.minimum(u, 0.0) - jnp.log(1.0 + jnp.exp(-jnp.abs(u)))


def _fox_cumsum(f, bf_pad, tb=256):
    s = f.shape[0]

    def body(f_ref, b_ref, cum_ref, carry):
        @pl.when(pl.program_id(0) == 0)
        def _():
            carry[...] = jnp.zeros_like(carry)

        lane = lax.broadcasted_iota(jnp.int32, (tb, LANES), 1)
        logf = jnp.where(lane < N_HEADS, _log_sigmoid(f_ref[...] + b_ref[...]), 0.0)
        hi, mid, lo = _split3(logf)
        rows = lax.broadcasted_iota(jnp.int32, (tb, tb), 0)
        cols = lax.broadcasted_iota(jnp.int32, (tb, tb), 1)
        tril = (cols <= rows).astype(BF16)
        cum = _dot(tril, hi) + _dot(tril, mid) + _dot(tril, lo) + carry[0:1, :]
        cum_ref[...] = cum
        carry[...] = jnp.broadcast_to(cum[tb - 1:tb, :], carry.shape)

    return pl.pallas_call(
        body, name="fox_cumsum", grid=(s // tb,),
        out_shape=jax.ShapeDtypeStruct((s, LANES), F32),
        in_specs=[pl.BlockSpec((tb, LANES), lambda i: (i, 0)), _const_spec((1, LANES))],
        out_specs=pl.BlockSpec((tb, LANES), lambda i: (i, 0)),
        scratch_shapes=[pltpu.VMEM((SUBLANES, LANES), F32)],
        compiler_params=_params(("arbitrary",)),
    )(f, bf_pad)


def _fox_prep(b, cum, t):
    s = b.shape[0]

    def body(b_ref, cum_ref, q_ref, k_ref, v_ref, st_ref):
        lane = lax.broadcasted_iota(jnp.int32, (t, LANES), 1)
        sub8 = lax.broadcasted_iota(jnp.int32, (SUBLANES, LANES), 0)
        lane8 = lax.broadcasted_iota(jnp.int32, (SUBLANES, LANES), 1)
        cumv = cum_ref[...]
        stats = jnp.zeros((SUBLANES, LANES), F32)
        for h in range(N_HEADS):
            p, odd = h // 2, h % 2
            ch = jnp.broadcast_to(cumv[:, h:h + 1], (t, LANES))
            hi, mid, lo = (x.astype(F32) for x in _split3(ch))
            qp = b_ref[:, LANES * p:LANES * (p + 1)].astype(F32)
            kp = b_ref[:, 512 + LANES * p:512 + LANES * (p + 1)].astype(F32)
            vp = b_ref[:, 1024 + LANES * p:1024 + LANES * (p + 1)].astype(F32)
            if odd:
                qp, kp, vp = pltpu.roll(qp, 64, 1), pltpu.roll(kp, 64, 1), pltpu.roll(vp, 64, 1)
            qs = jnp.where(lane < 64, qp * SCALE, 0.0)
            ks = jnp.where(lane < 64, kp, 0.0)
            qa = jnp.where(lane < 64, qs,
                           jnp.where(lane == 64, hi, jnp.where(lane == 65, mid, jnp.where(lane == 66, lo,
                           jnp.where(lane < 70, 1.0, 0.0)))))
            ka = jnp.where(lane < 64, ks,
                           jnp.where(lane < 67, 1.0, jnp.where(lane == 67, -hi, jnp.where(lane == 68, -mid,
                           jnp.where(lane == 69, -lo, 0.0)))))
            q_ref[:, LANES * h:LANES * (h + 1)] = qa.astype(BF16)
            k_ref[:, LANES * h:LANES * (h + 1)] = ka.astype(BF16)
            v_ref[:, LANES * h:LANES * (h + 1)] = jnp.where(lane < 64, vp, jnp.where(lane == 64, 1.0, 0.0)).astype(BF16)
            qn = jnp.sqrt(jnp.max(jnp.sum(qs * qs, axis=-1, keepdims=True)))
            kn = jnp.sqrt(jnp.max(jnp.sum(ks * ks, axis=-1, keepdims=True)))
            csum = hi + mid + lo
            row = jnp.where(lane8 == 0, qn, jnp.where(lane8 == 1, kn, jnp.where(
                lane8 == 2, csum[0:1, :], jnp.where(lane8 == 3, csum[t - 1:t, :], 0.0))))
            stats = jnp.where(sub8 == h, row, stats)
        st_ref[0] = stats

    wide = pl.BlockSpec((t, 1024), lambda i: (i, 0))
    return pl.pallas_call(
        body, name="fox_prep", grid=(s // t,),
        out_shape=[jax.ShapeDtypeStruct((s, 1024), BF16)] * 3 + [jax.ShapeDtypeStruct((s // t, SUBLANES, LANES), F32)],
        in_specs=[pl.BlockSpec((t, W_B), lambda i: (i, 0)), pl.BlockSpec((t, LANES), lambda i: (i, 0))],
        out_specs=[wide, wide, wide, pl.BlockSpec((1, SUBLANES, LANES), lambda i: (i, 0, 0))],
        compiler_params=_params(("parallel",)),
    )(b, cum)


PRUNE_MARGIN = 90.0


def _fox_tile_ranges(stats):
    nt = stats.shape[0]
    ball = 2.0 * 1.01 * jnp.max(stats[:, :, 0], axis=0) * jnp.max(stats[:, :, 1], axis=0)
    d = ball[None, None, :] + stats[:, None, :, 2] - stats[None, :, :, 3]
    idx = jnp.arange(nt)
    skip = (d <= -PRUNE_MARGIN) & (idx[None, :, None] < idx[:, None, None])
    skip = skip.reshape(nt, nt, 4, 2).all(axis=-1)
    first_key = jnp.sum(skip, axis=1).astype(F32)
    needed = (~skip) & (idx[None, :, None] <= idx[:, None, None])
    n_query = jnp.sum(needed, axis=0).astype(F32)
    table = jnp.zeros((4, SUBLANES, LANES), F32)
    table = table.at[:, 0, :nt].set(first_key.T)
    table = table.at[:, 1, :nt].set(n_query.T)
    return table


def _lane_scalar(block, row, lane_idx):
    sub8 = lax.broadcasted_iota(jnp.int32, (SUBLANES, LANES), 0)
    lane8 = lax.broadcasted_iota(jnp.int32, (SUBLANES, LANES), 1)
    return jnp.sum(jnp.where((sub8 == row) & (lane8 == lane_idx), block, 0.0)).astype(jnp.int32)


def _fox_fwd(qa, ka, va, ranges, t):
    s = qa.shape[0]
    nt = s // t
    nc = t // LANES

    def body(rg_ref, q_ref, k_ref, v_ref, o_ref, lse_ref):
        i = pl.program_id(1)
        lane = lax.broadcasted_iota(jnp.int32, (t, LANES), 1)
        rows = lax.broadcasted_iota(jnp.int32, (t, t), 0)
        cols = lax.broadcasted_iota(jnp.int32, (t, t), 1)
        first = jnp.clip(_lane_scalar(rg_ref[0], 0, i), 0, i)

        def tile(j, carry, diagonal):
            off = pl.multiple_of(j * t, t)
            kt = k_ref[pl.ds(off, t), :]
            vt = v_ref[pl.ds(off, t), :]
            heads = range(2)
            scs = [_dot(q_ref[:, LANES * hh:LANES * (hh + 1)], kt[:, LANES * hh:LANES * (hh + 1)], NT) for hh in heads]
            if diagonal:
                scs = [jnp.where(cols <= rows, sc, NEG) for sc in scs]
            m_new = []
            for hh in heads:
                part = scs[hh][:, 0:LANES]
                for cch in range(1, nc):
                    part = jnp.maximum(part, scs[hh][:, LANES * cch:LANES * (cch + 1)])
                m_new.append(jnp.maximum(carry[2 * hh], jnp.max(part, axis=-1, keepdims=True)))
            alphas = [jnp.exp(carry[2 * hh] - m_new[hh]) for hh in heads]
            ps = [jnp.exp(scs[hh] - m_new[hh]).astype(BF16) for hh in heads]
            pvs = [_dot(ps[hh], vt[:, LANES * hh:LANES * (hh + 1)]) for hh in heads]
            return (m_new[0], alphas[0] * carry[1] + pvs[0], m_new[1], alphas[1] * carry[3] + pvs[1])

        col0 = jnp.full((t, 1), NEG, F32)
        zero = jnp.zeros((t, LANES), F32)
        carry = lax.fori_loop(first, i, lambda j, cr: tile(j, cr, False), (col0, zero, col0, zero))
        m0, acc0, m1, acc1 = tile(i, carry, True)
        l0, l1 = acc0[:, 64:65], acc1[:, 64:65]
        o_ref[...] = jnp.where(lane < 64, acc0 * (1.0 / l0), pltpu.roll(acc1 * (1.0 / l1), 64, 1)).astype(BF16)
        sub = lax.broadcasted_iota(jnp.int32, (SUBLANES, t), 0)
        lse0 = jnp.broadcast_to(m0 + jnp.log(l0), (t, LANES)).T[0:SUBLANES, :]
        lse1 = jnp.broadcast_to(m1 + jnp.log(l1), (t, LANES)).T[0:SUBLANES, :]
        lse_ref[0] = jnp.where(sub == 0, lse0, jnp.where(sub == 1, lse1, 0.0))

    pair = pl.BlockSpec((s, 2 * LANES), lambda p, i: (0, p))
    return pl.pallas_call(
        body, name="fox_fwd", grid=(4, nt),
        out_shape=[jax.ShapeDtypeStruct((s, 512), BF16), jax.ShapeDtypeStruct((4, SUBLANES, s), F32)],
        in_specs=[pl.BlockSpec((1, SUBLANES, LANES), lambda p, i: (p, 0, 0)),
                  pl.BlockSpec((t, 2 * LANES), lambda p, i: (i, p)), pair, pair],
        out_specs=[pl.BlockSpec((t, LANES), lambda p, i: (i, p)),
                   pl.BlockSpec((1, SUBLANES, t), lambda p, i: (p, 0, i))],
        compiler_params=_params(("parallel", "arbitrary"), VMEM_LIMIT),
    )(ranges, qa, ka, va)


def _dup_halves(blk, lane):
    f = blk.astype(F32)
    r = pltpu.roll(f, 64, 1)
    return jnp.where(lane < 64, f, r).astype(BF16), jnp.where(lane >= 64, f, r).astype(BF16)


GROUP = 4
GROUP_ROWS = GROUP * WINDOW


def _stack_heads(ref, g, lane):
    parts = []
    for pb in (2 * g, 2 * g + 1):
        blk = ref[:, LANES * pb:LANES * (pb + 1)]
        zero = jnp.zeros_like(blk)
        parts += [jnp.where(lane < 64, blk, zero), jnp.where(lane >= 64, blk, zero)]
    return jnp.concatenate(parts, axis=0)


def _swa_band(a_ref, ap_ref, g, lane):
    k = jnp.concatenate([_dup_halves(ap_ref[:, 512:640], lane)[g], _dup_halves(a_ref[:, 512:640], lane)[g]], axis=0)
    v = jnp.concatenate([_dup_halves(ap_ref[:, 640:768], lane)[g], _dup_halves(a_ref[:, 640:768], lane)[g]], axis=0)
    return k, v


def _swa_logits(q, k, has_prev):
    sc = _dot(q, k, NT) * SCALE
    rr = lax.broadcasted_iota(jnp.int32, sc.shape, 0) % WINDOW
    cc = lax.broadcasted_iota(jnp.int32, sc.shape, 1)
    valid = (cc > rr) & (cc <= rr + WINDOW) & (has_prev | (cc >= WINDOW))
    return jnp.where(valid, sc, NEG)


def _per_head_column(values):
    return jnp.concatenate([jnp.broadcast_to(v, (WINDOW, 1)) for v in values], axis=0)


SWA_BLOCKS = 4
SWA_ROWS = SWA_BLOCKS * WINDOW


def _swa_blocks(a_ref, ap_ref):
    return [ap_ref] + [a_ref.at[pl.ds(WINDOW * jb, WINDOW), :] for jb in range(SWA_BLOCKS)]


def _swa_fwd(a, sinks):
    s = a.shape[0]

    def body(sink_ref, a_ref, ap_ref, o_ref, l_ref):
        lane = lax.broadcasted_iota(jnp.int32, (WINDOW, LANES), 1)
        blocks = _swa_blocks(a_ref, ap_ref)
        units = [(jb, g) for jb in range(SWA_BLOCKS) for g in range(2)]
        sinks_col = [_per_head_column([sink_ref[GROUP * g + hh] for hh in range(GROUP)]) for g in range(2)]
        bands = [_swa_band(blocks[jb + 1], blocks[jb], g, lane) for jb, g in units]
        scs = [_swa_logits(_stack_heads(blocks[jb + 1], g, lane), bands[u][0],
                           (pl.program_id(0) > 0) if jb == 0 else True) for u, (jb, g) in enumerate(units)]
        ms = [jnp.maximum(jnp.max(scs[u], axis=-1, keepdims=True), sinks_col[g]) for u, (jb, g) in enumerate(units)]
        ps = [jnp.exp(scs[u] - ms[u]) for u in range(len(units))]
        dens = [jnp.sum(ps[u], axis=-1, keepdims=True) + jnp.exp(sinks_col[g] - ms[u]) for u, (jb, g) in enumerate(units)]
        outs = [_dot((ps[u] * (1.0 / dens[u])).astype(BF16), bands[u][1]) for u in range(len(units))]
        for jb in range(SWA_BLOCKS):
            rows = slice(WINDOW * jb, WINDOW * (jb + 1))
            l_all = jnp.zeros((WINDOW, LANES), F32)
            for g in range(2):
                u = 2 * jb + g
                lcol = ms[u] + jnp.log(dens[u])
                for pb in range(2):
                    r0 = 2 * pb * WINDOW
                    o_ref[rows, LANES * (2 * g + pb):LANES * (2 * g + pb + 1)] = jnp.where(
                        lane < 64, outs[u][r0:r0 + WINDOW], outs[u][r0 + WINDOW:r0 + 2 * WINDOW]).astype(BF16)
                for hh in range(GROUP):
                    l_all = jnp.where(lane == GROUP * g + hh, lcol[WINDOW * hh:WINDOW * (hh + 1)], l_all)
            l_ref[rows, :] = l_all

    return pl.pallas_call(
        body, name="swa_fwd", grid=(s // SWA_ROWS,),
        out_shape=[jax.ShapeDtypeStruct((s, 512), BF16), jax.ShapeDtypeStruct((s, LANES), F32)],
        in_specs=[pl.BlockSpec(memory_space=pltpu.SMEM),
                  pl.BlockSpec((SWA_ROWS, W_A), lambda i: (i, 0)),
                  pl.BlockSpec((WINDOW, W_A), lambda i: (jnp.maximum(SWA_BLOCKS * i - 1, 0), 0))],
        out_specs=[pl.BlockSpec((SWA_ROWS, 512), lambda i: (i, 0)), pl.BlockSpec((SWA_ROWS, LANES), lambda i: (i, 0))],
        compiler_params=_params(("parallel",)),
    )(sinks, a, a)


def _mid(att_a, att_b, g, x, target, gate, g_final, wo_a, wo_b, w_out, tm=256):
    s = x.shape[0]
    nt = s // tm

    def body(aa_ref, ab_ref, g_ref, x_ref, t_ref, gate_ref, gf_ref, woa_ref, wob_ref, wout_ref,
             dx_ref, daa_ref, dab_ref, dg_ref, delta_ref, dwoa_ref, dwob_ref, dwout_ref, vec_ref,
             acc_gf, acc_gate, acc_loss):
        step = pl.program_id(0)

        @pl.when(step == 0)
        def _():
            dwoa_ref[...] = jnp.zeros_like(dwoa_ref)
            dwob_ref[...] = jnp.zeros_like(dwob_ref)
            dwout_ref[...] = jnp.zeros_like(dwout_ref)
            acc_gf[...] = jnp.zeros_like(acc_gf)
            acc_gate[...] = jnp.zeros_like(acc_gate)
            acc_loss[...] = jnp.zeros_like(acc_loss)

        def fold(v):
            return jnp.sum(v.reshape(tm // SUBLANES, SUBLANES, D_MODEL), axis=0)

        gate = gate_ref[...]
        gfin = gf_ref[...]
        branches = []
        for att_ref, z_off, wo_ref in ((aa_ref, 0, woa_ref), (ab_ref, 512, wob_ref)):
            att = att_ref[...].astype(F32)
            z = g_ref[:, z_off:z_off + 512].astype(F32)
            sz = _sigmoid(z)
            silu = z * sz
            u = (att * silu).astype(BF16)
            branches.append((att, z, sz, silu, u, _dot(u, wo_ref[...])))
        ga = g_ref[:, 1024:2048].astype(F32)
        gb = g_ref[:, 2048:3072].astype(F32)
        sga, sgb = _sigmoid(ga), _sigmoid(gb)
        y_a, y_b = branches[0][5], branches[1][5]
        mb = (sga * y_a + sgb * y_b).astype(BF16)
        o = _dot(mb, wout_ref[...])
        x2 = x_ref[...] + gate * o
        r2 = lax.rsqrt(jnp.mean(x2 * x2, axis=-1, keepdims=True) + NORM_EPS)
        xn2 = x2 * r2
        err = xn2 * gfin - t_ref[...]
        acc_loss[...] += fold(err * err)
        dy = err * (1.0 / D_MODEL)
        acc_gf[...] += fold(dy * xn2)
        dxn = dy * gfin
        dx2 = r2 * (dxn - xn2 * jnp.mean(dxn * xn2, axis=-1, keepdims=True))
        dx_ref[...] = dx2
        acc_gate[...] += fold(dx2 * o)
        d_o = (dx2 * gate).astype(BF16)
        dwout_ref[...] += _dot(mb, d_o, TN)
        dm = _dot(d_o, wout_ref[...], NT)
        dg_ref[:, 1024:2048] = (dm * y_a * sga * (1.0 - sga)).astype(BF16)
        dg_ref[:, 2048:3072] = (dm * y_b * sgb * (1.0 - sgb)).astype(BF16)
        for (att, z, sz, silu, u, _), sg, wo_ref, dwo_ref, datt_ref, z_off in (
                (branches[0], sga, woa_ref, dwoa_ref, daa_ref, 0), (branches[1], sgb, wob_ref, dwob_ref, dab_ref, 512)):
            dyb = (dm * sg).astype(BF16)
            dwo_ref[...] += _dot(u, dyb, TN)
            du = _dot(dyb, wo_ref[...], NT)
            datt = du * silu
            datt_ref[...] = datt.astype(BF16)
            dg_ref[:, z_off:z_off + 512] = (du * att * (sz * (1.0 + z * (1.0 - sz)))).astype(BF16)
            if z_off == 512:
                prod = datt * att
                hi = prod.astype(BF16)
                lo = (prod - hi.astype(F32)).astype(BF16)
                er = lax.broadcasted_iota(jnp.int32, (512, LANES), 0)
                ec = lax.broadcasted_iota(jnp.int32, (512, LANES), 1)
                e = (er // HEAD_DIM == ec).astype(BF16)
                delta = _dot(hi, e) + _dot(lo, e)
                delta_ref[...] = delta.T[0:SUBLANES, :]

        @pl.when(step == nt - 1)
        def _():
            sub = lax.broadcasted_iota(jnp.int32, (SUBLANES, D_MODEL), 0)
            dgf = jnp.sum(acc_gf[...], axis=0, keepdims=True)
            dgate = jnp.sum(acc_gate[...], axis=0, keepdims=True)
            loss = 0.5 * jnp.sum(acc_loss[...]) * (1.0 / D_MODEL)
            vec_ref[...] = jnp.where(sub == 0, dgf, jnp.where(sub == 1, dgate, jnp.where(sub == 2, loss, 0.0)))

    row = lambda w: pl.BlockSpec((tm, w), lambda i: (i, 0))
    return pl.pallas_call(
        body, name="mid", grid=(nt,),
        out_shape=[jax.ShapeDtypeStruct((s, D_MODEL), F32), jax.ShapeDtypeStruct((s, 512), BF16),
                   jax.ShapeDtypeStruct((s, 512), BF16), jax.ShapeDtypeStruct((s, W_G), BF16),
                   jax.ShapeDtypeStruct((SUBLANES, s), F32),
                   jax.ShapeDtypeStruct((512, D_MODEL), F32), jax.ShapeDtypeStruct((512, D_MODEL), F32),
                   jax.ShapeDtypeStruct((D_MODEL, D_MODEL), F32), jax.ShapeDtypeStruct((SUBLANES, D_MODEL), F32)],
        in_specs=[row(512), row(512), row(W_G), row(D_MODEL), row(D_MODEL),
                  _const_spec((1, D_MODEL)), _const_spec((1, D_MODEL)),
                  _const_spec((512, D_MODEL)), _const_spec((512, D_MODEL)), _const_spec((D_MODEL, D_MODEL))],
        out_specs=[row(D_MODEL), row(512), row(512), row(W_G),
                   pl.BlockSpec((SUBLANES, tm), lambda i: (0, i)),
                   pl.BlockSpec((512, D_MODEL), lambda i: (0, 0)), pl.BlockSpec((512, D_MODEL), lambda i: (0, 0)),
                   pl.BlockSpec((D_MODEL, D_MODEL), lambda i: (0, 0)), pl.BlockSpec((SUBLANES, D_MODEL), lambda i: (0, 0))],
        scratch_shapes=[pltpu.VMEM((SUBLANES, D_MODEL), F32)] * 3,
        compiler_params=_params(("arbitrary",), VMEM_LIMIT),
    )(att_a, att_b, g, x, target, gate, g_final, wo_a, wo_b, w_out)


def _rope_bwd(dt, cos, sin, lane):
    u = dt * sin
    lo = (lane % HEAD_DIM) < (HEAD_DIM // 2)
    return dt * cos + jnp.where(lo, pltpu.roll(u, 96, 1), -pltpu.roll(u, 32, 1))


def _swa_bwd(a, datt, l_all, sinks, cos, sin):
    s = a.shape[0]
    nt = s // SWA_ROWS

    def body(sink_ref, a_ref, ap_ref, do_ref, l_ref, cos_ref, sin_ref, da_ref, ds_ref, halo):
        step = pl.program_id(0)
        tile = nt - 1 - step

        @pl.when(step == 0)
        def _():
            halo[...] = jnp.zeros_like(halo)
            ds_ref[...] = jnp.zeros_like(ds_ref)

        lane = lax.broadcasted_iota(jnp.int32, (WINDOW, LANES), 1)
        sub8 = lax.broadcasted_iota(jnp.int32, (SUBLANES, LANES), 0)
        lane8 = lax.broadcasted_iota(jnp.int32, (SUBLANES, LANES), 1)
        blocks = _swa_blocks(a_ref, ap_ref)
        dsink = jnp.zeros((SUBLANES, LANES), F32)

        def join(pair, r0):
            x0, x1 = pair[0][r0:r0 + WINDOW], pair[1][r0:r0 + WINDOW]
            return jnp.where(lane < 64, x0 + pltpu.roll(x0, 64, 1), x1 + pltpu.roll(x1, 64, 1))

        units = [(jb, g) for jb in range(SWA_BLOCKS) for g in range(2)]
        n_u = len(units)
        sinks_col = [_per_head_column([sink_ref[GROUP * g + hh] for hh in range(GROUP)]) for g in range(2)]
        bands = [_swa_band(blocks[jb + 1], blocks[jb], g, lane) for jb, g in units]
        qs = [_stack_heads(blocks[jb + 1], g, lane) for jb, g in units]
        doms = [_stack_heads(do_ref.at[pl.ds(WINDOW * jb, WINDOW), :], g, lane) for jb, g in units]
        lcols = []
        for jb, g in units:
            lv = l_ref[WINDOW * jb:WINDOW * (jb + 1), :]
            lcols.append(_per_head_column([lv[:, GROUP * g + hh:GROUP * g + hh + 1] for hh in range(GROUP)]))
        ps = [jnp.exp(_swa_logits(qs[u], bands[u][0], (tile > 0) if jb == 0 else True) - lcols[u])
              for u, (jb, g) in enumerate(units)]
        dps = [_dot(doms[u], bands[u][1], NT) for u in range(n_u)]
        deltas = [jnp.sum(ps[u] * dps[u], axis=-1, keepdims=True) for u in range(n_u)]
        for u, (jb, g) in enumerate(units):
            sink_term = jnp.exp(sinks_col[g] - lcols[u]) * deltas[u]
            for hh in range(GROUP):
                tot = jnp.sum(sink_term[WINDOW * hh:WINDOW * (hh + 1)])
                dsink = dsink + jnp.where((sub8 == 0) & (lane8 == GROUP * g + hh), -tot, 0.0)
        dss = [(ps[u] * (dps[u] - deltas[u])).astype(BF16) for u in range(n_u)]
        dqs = [_dot(dss[u], bands[u][0]) * SCALE for u in range(n_u)]
        dks = [_dot(dss[u], qs[u], TN) * SCALE for u in range(n_u)]
        dvs = [_dot(ps[u].astype(BF16), doms[u], TN) for u in range(n_u)]

        carry_k, carry_v = halo[:, 0:LANES], halo[:, LANES:2 * LANES]
        for jb in reversed(range(SWA_BLOCKS)):
            rows = slice(WINDOW * jb, WINDOW * (jb + 1))
            cosv, sinv = cos_ref[rows, :], sin_ref[rows, :]
            for g in range(2):
                dq = dqs[2 * jb + g]
                for pb in range(2):
                    r0 = 2 * pb * WINDOW
                    dq_pair = jnp.where(lane < 64, dq[r0:r0 + WINDOW], dq[r0 + WINDOW:r0 + 2 * WINDOW])
                    da_ref[rows, LANES * (2 * g + pb):LANES * (2 * g + pb + 1)] = _rope_bwd(
                        dq_pair, cosv, sinv, lane).astype(BF16)
            dkb, dvb = dks[2 * jb:2 * jb + 2], dvs[2 * jb:2 * jb + 2]
            da_ref[rows, 512:640] = _rope_bwd(join(dkb, WINDOW) + carry_k, cosv, sinv, lane).astype(BF16)
            da_ref[rows, 640:768] = (join(dvb, WINDOW) + carry_v).astype(BF16)
            carry_k, carry_v = join(dkb, 0), join(dvb, 0)
        halo[:, 0:LANES] = carry_k
        halo[:, LANES:2 * LANES] = carry_v
        ds_ref[...] += dsink

    rev = lambda w: pl.BlockSpec((SWA_ROWS, w), lambda i: (nt - 1 - i, 0))
    return pl.pallas_call(
        body, name="swa_bwd", grid=(nt,),
        out_shape=[jax.ShapeDtypeStruct((s, W_A), BF16), jax.ShapeDtypeStruct((SUBLANES, LANES), F32)],
        in_specs=[pl.BlockSpec(memory_space=pltpu.SMEM), rev(W_A),
                  pl.BlockSpec((WINDOW, W_A), lambda i: (jnp.maximum(SWA_BLOCKS * (nt - 1 - i) - 1, 0), 0)),
                  rev(512), rev(LANES), rev(LANES), rev(LANES)],
        out_specs=[rev(W_A), pl.BlockSpec((SUBLANES, LANES), lambda i: (0, 0))],
        scratch_shapes=[pltpu.VMEM((WINDOW, 2 * LANES), F32)],
        compiler_params=_params(("arbitrary",)),
    )(sinks, a, a, datt, l_all, cos, sin)


def _fox_bwd(qa, ka, b, do, lse, delta, ranges, t):
    s = qa.shape[0]
    nt = s // t

    def body(rg_ref, q_ref, do_ref, lse_ref, dl_ref, k_ref, v_ref, dq_ref, dk_ref, dv_ref, dc_ref, dr_ref, dq_acc):
        p = pl.program_id(0)
        j = pl.program_id(1)
        n_query = jnp.clip(_lane_scalar(rg_ref[0], 1, j), 1, nt - j)

        @pl.when(j == 0)
        def _():
            dq_acc[...] = jnp.zeros_like(dq_acc)

        lane = lax.broadcasted_iota(jnp.int32, (t, LANES), 1)
        rows = lax.broadcasted_iota(jnp.int32, (t, t), 0)
        cols = lax.broadcasted_iota(jnp.int32, (t, t), 1)
        kt = k_ref[...]
        vt = v_ref[...]

        def tile(i, carry, diagonal):
            dk0, dk1, dv = carry
            off = pl.multiple_of(i * t, t)
            qt = q_ref[pl.ds(off, t), :]
            dot_ = do_ref[pl.ds(off, t), :]
            lse_t = lse_ref[0, :, pl.ds(off, t)]
            dl_t = dl_ref[0, :, pl.ds(off, t)]
            heads = range(2)
            qs = [qt[:, LANES * hh:LANES * (hh + 1)] for hh in heads]
            ks = [kt[:, LANES * hh:LANES * (hh + 1)] for hh in heads]
            doms = [jnp.where((lane < 64) if hh == 0 else (lane >= 64), dot_, jnp.zeros_like(dot_)) for hh in heads]
            sts = [_dot(ks[hh], qs[hh], NT) for hh in heads]
            dpts = [_dot(vt, doms[hh], NT) for hh in heads]
            if diagonal:
                sts = [jnp.where(cols >= rows, st, NEG) for st in sts]
            pts = [jnp.exp(sts[hh] - lse_t[hh:hh + 1, :]) for hh in heads]
            dsts = [(pts[hh] * (dpts[hh] - dl_t[hh:hh + 1, :])).astype(BF16) for hh in heads]
            dv = dv + _dot(pts[0].astype(BF16), doms[0]) + _dot(pts[1].astype(BF16), doms[1])
            dks = [_dot(dsts[hh], qs[hh]) for hh in heads]
            for hh in heads:
                dq_acc[hh, pl.ds(off, t), :] += _dot(dsts[hh], ks[hh], TN)
            return dk0 + dks[0], dk1 + dks[1], dv

        zero = jnp.zeros((t, LANES), F32)
        carry = tile(j, (zero, zero, zero), True)
        dk0, dk1, dv = lax.fori_loop(j + 1, j + n_query, lambda i, cr: tile(i, cr, False), carry)
        dk_ref[...] = jnp.where(lane < 64, dk0, pltpu.roll(dk1, 64, 1)).astype(BF16)
        dv_ref[...] = dv.astype(BF16)
        c0 = jnp.broadcast_to(dk0[:, 67:68], (t, LANES))
        c1 = jnp.broadcast_to(dk1[:, 67:68], (t, LANES))
        dc_ref[0] = jnp.where(lane == 2 * p, -c0, jnp.where(lane == 2 * p + 1, -c1, 0.0))

        @pl.when(j == nt - 1)
        def _():
            lane_s = lax.broadcasted_iota(jnp.int32, (s, LANES), 1)
            a0, a1 = dq_acc[0], dq_acc[1]
            dq_ref[...] = (jnp.where(lane_s < 64, a0, pltpu.roll(a1, 64, 1)) * SCALE).astype(BF16)
            r0 = jnp.broadcast_to(a0[:, 64:65], (s, LANES))
            r1 = jnp.broadcast_to(a1[:, 64:65], (s, LANES))
            dr_ref[0] = jnp.where(lane_s == 2 * p, r0, jnp.where(lane_s == 2 * p + 1, r1, 0.0))

    return pl.pallas_call(
        body, name="fox_bwd", grid=(4, nt),
        out_shape=[jax.ShapeDtypeStruct((s, 512), BF16), jax.ShapeDtypeStruct((s, 512), BF16),
                   jax.ShapeDtypeStruct((s, 512), BF16), jax.ShapeDtypeStruct((4, s, LANES), F32),
                   jax.ShapeDtypeStruct((4, s, LANES), F32)],
        in_specs=[pl.BlockSpec((1, SUBLANES, LANES), lambda p, j: (p, 0, 0)),
                  pl.BlockSpec((s, 2 * LANES), lambda p, j: (0, p)),
                  pl.BlockSpec((s, LANES), lambda p, j: (0, p)),
                  pl.BlockSpec((1, SUBLANES, s), lambda p, j: (p, 0, 0)),
                  pl.BlockSpec((1, SUBLANES, s), lambda p, j: (p, 0, 0)),
                  pl.BlockSpec((t, 2 * LANES), lambda p, j: (j, p)),
                  pl.BlockSpec((t, LANES), lambda p, j: (j, 8 + p))],
        out_specs=[pl.BlockSpec((s, LANES), lambda p, j: (0, p)),
                   pl.BlockSpec((t, LANES), lambda p, j: (j, p)),
                   pl.BlockSpec((t, LANES), lambda p, j: (j, p)),
                   pl.BlockSpec((1, t, LANES), lambda p, j: (p, j, 0)),
                   pl.BlockSpec((1, s, LANES), lambda p, j: (p, 0, 0))],
        scratch_shapes=[pltpu.VMEM((2, s, LANES), F32)],
        compiler_params=_params(("parallel", "arbitrary"), VMEM_LIMIT),
    )(ranges, qa, do, lse, delta, ka, b)


def _fox_cumsum_bwd(dcum_k, dcum_q, f, bf_pad, tb=256):
    s = f.shape[0]
    nb = s // tb

    def body(dc_ref, dr_ref, f_ref, b_ref, df_ref, db_ref, carry):
        step = pl.program_id(0)

        @pl.when(step == 0)
        def _():
            carry[...] = jnp.zeros_like(carry)
            db_ref[...] = jnp.zeros_like(db_ref)

        lane = lax.broadcasted_iota(jnp.int32, (tb, LANES), 1)
        dc = dc_ref[0] + dr_ref[0]
        for k in range(1, 4):
            dc = dc + (dc_ref[k] + dr_ref[k])
        hi, mid, lo = _split3(dc)
        rows = lax.broadcasted_iota(jnp.int32, (tb, tb), 0)
        cols = lax.broadcasted_iota(jnp.int32, (tb, tb), 1)
        triu = (cols >= rows).astype(BF16)
        dlogf = _dot(triu, hi) + _dot(triu, mid) + _dot(triu, lo) + carry[0:1, :]
        carry[...] = jnp.broadcast_to(dlogf[0:1, :], carry.shape)
        u = f_ref[...] + b_ref[...]
        dfb = jnp.where(lane < N_HEADS, dlogf * _sigmoid(-u), 0.0)
        df_ref[...] = dfb.astype(BF16)
        sub = lax.broadcasted_iota(jnp.int32, (SUBLANES, LANES), 0)
        db_ref[...] += jnp.where(sub == 0, jnp.sum(dfb, axis=0, keepdims=True), 0.0)

    return pl.pallas_call(
        body, name="fox_cumsum_bwd", grid=(nb,),
        out_shape=[jax.ShapeDtypeStruct((s, LANES), BF16), jax.ShapeDtypeStruct((SUBLANES, LANES), F32)],
        in_specs=[pl.BlockSpec((4, tb, LANES), lambda i: (0, nb - 1 - i, 0)),
                  pl.BlockSpec((4, tb, LANES), lambda i: (0, nb - 1 - i, 0)),
                  pl.BlockSpec((tb, LANES), lambda i: (nb - 1 - i, 0)), _const_spec((1, LANES))],
        out_specs=[pl.BlockSpec((tb, LANES), lambda i: (nb - 1 - i, 0)),
                   pl.BlockSpec((SUBLANES, LANES), lambda i: (0, 0))],
        scratch_shapes=[pltpu.VMEM((SUBLANES, LANES), F32)],
        compiler_params=_params(("arbitrary",)),
    )(dcum_k, dcum_q, f, bf_pad)


def _dh_norm_bwd(d_a, d_q, d_k, d_v, d_f, d_g, w_t, x, dx2, gnorm, scale1, tm=512):
    s = x.shape[0]
    nt = s // tm

    def body(da_ref, dq_ref, dk_ref, dv_ref, df_ref, dg_ref, w_ref, x_ref, dx2_ref, g_ref, sc_ref, gx_ref, vec_ref,
             a_sh, a_sc, a_g):
        step = pl.program_id(0)

        @pl.when(step == 0)
        def _():
            a_sh[...] = jnp.zeros_like(a_sh)
            a_sc[...] = jnp.zeros_like(a_sc)
            a_g[...] = jnp.zeros_like(a_g)

        def fold(v):
            return jnp.sum(v.reshape(tm // SUBLANES, SUBLANES, D_MODEL), axis=0)

        d_all = jnp.concatenate([da_ref[...], dq_ref[...], dk_ref[...], dv_ref[...], df_ref[...], dg_ref[...]], axis=1)
        dh = _dot(d_all, w_ref[...])
        xv = x_ref[...]
        r = lax.rsqrt(jnp.mean(xv * xv, axis=-1, keepdims=True) + NORM_EPS)
        xn = xv * r
        gn = g_ref[...]
        a_sh[...] += fold(dh)
        a_sc[...] += fold(dh * (xn * gn))
        dn1 = dh * sc_ref[...]
        a_g[...] += fold(dn1 * xn)
        dxn = dn1 * gn
        gx_ref[...] = dx2_ref[...] + r * (dxn - xn * jnp.mean(dxn * xn, axis=-1, keepdims=True))

        @pl.when(step == nt - 1)
        def _():
            sub = lax.broadcasted_iota(jnp.int32, (SUBLANES, D_MODEL), 0)
            v_sh = jnp.sum(a_sh[...], axis=0, keepdims=True)
            v_sc = jnp.sum(a_sc[...], axis=0, keepdims=True)
            v_g = jnp.sum(a_g[...], axis=0, keepdims=True)
            vec_ref[...] = jnp.where(sub == 0, v_sh, jnp.where(sub == 1, v_sc, jnp.where(sub == 2, v_g, 0.0)))

    row = lambda w: pl.BlockSpec((tm, w), lambda i: (i, 0))
    return pl.pallas_call(
        body, name="dh_norm_bwd", grid=(nt,),
        out_shape=[jax.ShapeDtypeStruct((s, D_MODEL), F32), jax.ShapeDtypeStruct((SUBLANES, D_MODEL), F32)],
        in_specs=[row(W_A), row(512), row(512), row(512), row(W_F), row(W_G), _const_spec((W_INT, D_MODEL)),
                  row(D_MODEL), row(D_MODEL), _const_spec((1, D_MODEL)), _const_spec((1, D_MODEL))],
        out_specs=[row(D_MODEL), pl.BlockSpec((SUBLANES, D_MODEL), lambda i: (0, 0))],
        scratch_shapes=[pltpu.VMEM((SUBLANES, D_MODEL), F32)] * 3,
        compiler_params=_params(("arbitrary",), VMEM_LIMIT),
    )(d_a, d_q, d_k, d_v, d_f, d_g, w_t, x, dx2, gnorm, scale1)


def _dw_in(h_t, d, name, tn, ts=1024):
    s, n = d.shape
    ns = s // ts

    def body(h_ref, d_ref, o_ref, acc):
        k = pl.program_id(1)

        @pl.when(k == 0)
        def _():
            acc[...] = jnp.zeros_like(acc)

        acc[...] += _dot(h_ref[...], d_ref[...])

        @pl.when(k == ns - 1)
        def _():
            o_ref[...] = acc[...]

    return pl.pallas_call(
        body, name=name, grid=(n // tn, ns),
        out_shape=jax.ShapeDtypeStruct((D_MODEL, n), F32),
        in_specs=[pl.BlockSpec((D_MODEL, ts), lambda jn, k: (0, k)), pl.BlockSpec((ts, tn), lambda jn, k: (k, jn))],
        out_specs=pl.BlockSpec((D_MODEL, tn), lambda jn, k: (0, jn)),
        scratch_shapes=[pltpu.VMEM((D_MODEL, tn), F32)],
        compiler_params=_params(("parallel", "arbitrary"), VMEM_LIMIT),
    )(h_t, d)


def _small_grads(packs, c_t, dada_shard):
    def body(p_ref, ct_ref, da_ref, sum_ref, gw_ref):
        acc = p_ref[0]
        for dev in range(1, 8):
            acc = acc + p_ref[dev]
        sum_ref[...] = acc
        gw_ref[...] = jnp.dot(ct_ref[...], da_ref[...], preferred_element_type=F32, precision=lax.Precision.HIGHEST)

    return pl.pallas_call(
        body, name="small_grads",
        out_shape=[jax.ShapeDtypeStruct(packs.shape[1:], F32),
                   jax.ShapeDtypeStruct((c_t.shape[0], dada_shard.shape[1]), F32)],
    )(packs, c_t, dada_shard)


def _adamw_body(w_ref, g_ref, m_ref, v_ref, d_ref, mo_ref, vo_ref):
    c1 = 1.0 / (1.0 - ADAM_B1 ** ADAM_STEP)
    c2 = 1.0 / (1.0 - ADAM_B2 ** ADAM_STEP)
    gv = g_ref[...]
    mn = ADAM_B1 * m_ref[...] + (1.0 - ADAM_B1) * gv
    vn = ADAM_B2 * v_ref[...] + (1.0 - ADAM_B2) * (gv * gv)
    mo_ref[...] = mn
    vo_ref[...] = vn
    d_ref[...] = -ADAM_LR * ((mn * c1) / (jnp.sqrt(vn * c2) + ADAM_EPS) + ADAM_WD * w_ref[...])


def _adamw3(w, g, m, v, name, tb=128):
    spec = pl.BlockSpec((tb, SUBLANES, LANES), lambda i: (i, 0, 0))
    return pl.pallas_call(
        functools.partial(_adamw_body), name=name, grid=(pl.cdiv(w.shape[0], tb),),
        out_shape=[jax.ShapeDtypeStruct(w.shape, F32)] * 3,
        in_specs=[spec] * 4, out_specs=[spec] * 3,
        compiler_params=_params(("parallel",)),
    )(w, g, m, v)


def _adamw(w, g, m, v, name):
    r, c = w.shape
    tr = 128 if r % 128 == 0 else r
    body = functools.partial(_adamw_body)
    spec = pl.BlockSpec((tr, c), lambda i: (i, 0))
    return pl.pallas_call(
        body, name=name, grid=(r // tr,),
        out_shape=[jax.ShapeDtypeStruct((r, c), F32)] * 3,
        in_specs=[spec] * 4, out_specs=[spec] * 3,
        compiler_params=_params(("parallel",)),
    )(w, g, m, v)


def _rope_tables(positions):
    inv_freq = 10000.0 ** (-jnp.arange(0, HEAD_DIM, 2, dtype=F32) / HEAD_DIM)
    ang = positions.astype(F32)[:, None] * inv_freq
    cos, sin = jnp.tile(jnp.cos(ang), (1, 4)), jnp.tile(jnp.sin(ang), (1, 4))
    lo = (jnp.arange(LANES) % HEAD_DIM) < (HEAD_DIM // 2)
    return cos, sin, jnp.where(lo, -sin, 0.0), jnp.where(lo, 0.0, sin)


def _pad_rows(v, rows=SUBLANES):
    return jnp.pad(v, ((0, rows - v.shape[0]), (0, 0)))


def kernel(x, c, positions, w_ada, b_ada, g_norm, w_in, b_f, sinks, w_o_swa, w_o_fox, w_out, g_final, loss_target, m_w_ada, m_b_ada, m_g_norm, m_w_in, m_b_f, m_sinks, m_w_o_swa, m_w_o_fox, m_w_out, m_g_final, v_w_ada, v_b_ada, v_g_norm, v_w_in, v_b_f, v_sinks, v_w_o_swa, v_w_o_fox, v_w_out, v_g_final):
    ix, iy, ic = lax.axis_index("x"), lax.axis_index("y"), lax.axis_index("c")
    chip = 2 * ix + iy
    dev = 2 * chip + ic
    xs, tgt = x[0], loss_target[0]
    s = xs.shape[0]

    c_all = _allgather_small(_pad_rows(c), "gather_c")[:, 0, :]
    b_ada_shard = lax.dynamic_slice(b_ada, (0, chip * 768), (1, 768))
    ada_all = _allgather_small(_ada_part(c_all, w_ada[0], b_ada_shard), "gather_ada")
    ada = lax.dynamic_index_in_dim(ada_all[::2], dev, axis=1, keepdims=False).reshape(1, 3 * D_MODEL)
    shift, scale, gate = ada[:, :D_MODEL], ada[:, D_MODEL:2 * D_MODEL], ada[:, 2 * D_MODEL:]
    scale1 = 1.0 + scale

    g_in, g_oa, g_ob, g_out = _allgather_weights([w_in[0], w_o_swa[0], w_o_fox[0], w_out[0]], "gather_weights")
    w_ref_order = jnp.transpose(g_in, (1, 0, 2)).reshape(D_MODEL, R_END)
    w_int = jnp.concatenate([
        w_ref_order[:, :R_ZA], w_ref_order[:, R_QB:R_FB], w_ref_order[:, R_FB:R_ZB],
        jnp.zeros((D_MODEL, W_F - N_HEADS), BF16), w_ref_order[:, R_ZA:R_QB], w_ref_order[:, R_ZB:]], axis=1)
    w_int_t = w_int.T
    wo_a = jnp.transpose(g_oa, (1, 0, 2)).reshape(512, D_MODEL)
    wo_b = jnp.transpose(g_ob, (1, 0, 2)).reshape(512, D_MODEL)
    w_o = g_out.reshape(D_MODEL, D_MODEL)

    cos, sin, sin_lo, sin_hi = _rope_tables(positions[0])
    bf_pad = jnp.pad(b_f, ((0, 0), (0, LANES - N_HEADS)))
    sink_vec = sinks[0]

    a, b, f, g, h_t = _norm_proj(xs, g_norm * scale1, shift, w_int, cos, sin_lo, sin_hi)
    att_a, l_swa = _swa_fwd(a, sink_vec)
    cum = _fox_cumsum(f, bf_pad)
    qa, ka, va, stats = _fox_prep(b, cum, FOX_TILE)
    ranges = _fox_tile_ranges(stats)
    att_b, lse = _fox_fwd(qa, ka, va, ranges, FOX_TILE)

    dx2, datt_a, datt_b, d_g, delta8, dwo_a, dwo_b, dw_out, vec_mid = _mid(
        att_a, att_b, g, xs, tgt, gate, g_final.reshape(1, D_MODEL), wo_a, wo_b, w_o)
    delta = jnp.pad(delta8.reshape(4, 2, s), ((0, 0), (0, SUBLANES - 2), (0, 0)))
    d_a, dsink = _swa_bwd(a, datt_a, l_swa, sink_vec, cos, sin)
    dq, dk, dv, dcum_k, dcum_q = _fox_bwd(qa, ka, b, datt_b, lse, delta, ranges, FOX_TILE)
    d_f, dbf = _fox_cumsum_bwd(dcum_k, dcum_q, f, bf_pad)
    grad_x, vec_dh = _dh_norm_bwd(d_a, dq, dk, dv, d_f, d_g, w_int_t, xs, dx2, g_norm, scale1)
    dw_a = _dw_in(h_t, d_a, "dw_in_a", 768)
    dw_q = _dw_in(h_t, dq, "dw_in_q", 512)
    dw_k = _dw_in(h_t, dk, "dw_in_k", 512)
    dw_v = _dw_in(h_t, dv, "dw_in_v", 512)
    dw_f = _dw_in(h_t, d_f, "dw_in_f", 128)
    dw_g = _dw_in(h_t, d_g, "dw_in_g", 1024)
    dw_in = jnp.concatenate([dw_a, dw_g[:, :512], dw_q, dw_k, dw_v, dw_f[:, :N_HEADS], dw_g[:, 512:]], axis=1)

    tail = jnp.pad(jnp.concatenate([dbf[0:1, :N_HEADS], dsink[0:1, :N_HEADS]], axis=1), ((0, 0), (0, D_MODEL - 2 * N_HEADS)))
    pack = jnp.concatenate([c, vec_dh[0:2], vec_mid[1:2], vec_dh[2:3], vec_mid[0:1], tail, vec_mid[2:3]], axis=0)
    packs = _allgather_small(pack, "gather_small")
    dada_all = packs[:, 1:4, :].reshape(8, 3 * D_MODEL)
    dada_shard = lax.dynamic_slice(dada_all, (0, chip * 768), (8, 768))
    sums, g_w_ada = _small_grads(packs, packs[:, 0, :].T, dada_shard)
    g_b_ada = sums[1:4].reshape(1, 3 * D_MODEL)
    g_g_norm = sums[4:5]
    g_g_final = sums[5]
    g_b_f = sums[6:7, :N_HEADS]
    g_sinks = sums[6:7, N_HEADS:2 * N_HEADS]
    loss = sums[7, 0]

    def slots(w, axis):
        if axis == 1:
            return jnp.transpose(w.reshape(w.shape[0], 4, w.shape[1] // 4), (1, 0, 2))
        return w.reshape(4, w.shape[0] // 4, w.shape[1])

    (g_w_in,) = _reduce_scatter([slots(dw_in, 1)], "reduce_w_in")
    g_wo_a, g_wo_b, g_w_out = _reduce_scatter([slots(dwo_a, 1), slots(dwo_b, 1), slots(dw_out, 0)], "reduce_w_small")

    grads = {
        "w_ada": g_w_ada, "b_ada": g_b_ada, "g_norm": g_g_norm, "w_in": g_w_in, "b_f": g_b_f, "sinks": g_sinks,
        "w_o_swa": g_wo_a, "w_o_fox": g_wo_b, "w_out": g_w_out, "g_final": g_g_final,
    }
    params = {
        "w_ada": (w_ada, m_w_ada, v_w_ada), "b_ada": (b_ada, m_b_ada, v_b_ada), "g_norm": (g_norm, m_g_norm, v_g_norm),
        "w_in": (w_in, m_w_in, v_w_in), "b_f": (b_f, m_b_f, v_b_f), "sinks": (sinks, m_sinks, v_sinks),
        "w_o_swa": (w_o_swa, m_w_o_swa, v_w_o_swa), "w_o_fox": (w_o_fox, m_w_o_fox, v_w_o_fox),
        "w_out": (w_out, m_w_out, v_w_out), "g_final": (g_final, m_g_final, v_g_final),
    }
    n_col = w_in.shape[2]

    def as_stored(t):
        return jnp.transpose(t, (2, 0, 1)).reshape(n_col, SUBLANES, LANES)

    def from_stored(t):
        return jnp.transpose(t, (1, 2, 0)).reshape(1, D_MODEL, n_col)

    names = list(grads)
    out_g, out_d, out_m, out_v = [], [], [], []
    for nm in names:
        w, m, v = params[nm]
        if nm == "w_in":
            g_st = as_stored(grads[nm][None])
            d_, m_, v_ = _adamw3(as_stored(w), g_st, as_stored(m), as_stored(v), "adamw_" + nm)
            res = [from_stored(t) for t in (g_st, d_, m_, v_)]
        else:
            shape2 = (w.shape[-2], w.shape[-1]) if w.ndim >= 2 else (1, w.shape[0])
            d_, m_, v_ = _adamw(w.reshape(shape2), grads[nm].reshape(shape2), m.reshape(shape2), v.reshape(shape2), "adamw_" + nm)
            res = [t.reshape(w.shape) for t in (grads[nm], d_, m_, v_)]
        out_g.append(res[0])
        out_d.append(res[1])
        out_m.append(res[2])
        out_v.append(res[3])
    return (loss, grad_x[None], *out_g, *out_d, *out_m, *out_v)
```

```python
import functools

import numpy as np
import jax
import jax.numpy as jnp
from jax import lax
from jax.experimental import pallas as pl
from jax.experimental.pallas import tpu as pltpu

F32 = jnp.float32
BF16 = jnp.bfloat16
MESH = pl.DeviceIdType.MESH

D_MODEL = 1024
HEAD_DIM = 64
N_HEADS = 8
WINDOW = 128
NORM_EPS = 1e-6
SCALE = HEAD_DIM ** -0.5
NEG = -1e30
LANES = 128
SUBLANES = 8
VMEM_LIMIT = 60 * 1024 * 1024
FOX_TILE = 512

W_A, W_B, W_F, W_G = 768, 1536, 128, 3072
OFF_A, OFF_B, OFF_F, OFF_G = 0, 768, 2304, 2432
W_INT = W_A + W_B + W_F + W_G
R_ZA, R_QB, R_FB, R_ZB, R_END = 768, 1280, 2816, 2824, 5384

ADAM_LR, ADAM_B1, ADAM_B2, ADAM_EPS, ADAM_WD, ADAM_STEP = 0.001, 0.9, 0.999, 1e-08, 0.01, 10

NT = (((1,), (1,)), ((), ()))
TN = (((0,), (0,)), ((), ()))


def _dot(a, b, dims=None):
    if dims is None:
        return jnp.dot(a, b, preferred_element_type=F32)
    return lax.dot_general(a, b, dims, preferred_element_type=F32)


def _split3(v):
    hi = v.astype(BF16)
    r1 = v - hi.astype(F32)
    mid = r1.astype(BF16)
    lo = (r1 - mid.astype(F32)).astype(BF16)
    return hi, mid, lo


def _sigmoid(v):
    return 1.0 / (1.0 + jnp.exp(-v))


def _params(sem=None, vmem=None):
    return pltpu.CompilerParams(dimension_semantics=sem, vmem_limit_bytes=vmem)


def _const_spec(shape):
    nd = len(shape)
    return pl.BlockSpec(shape, lambda *_: (0,) * nd, pipeline_mode=pl.Buffered(1))


def _flip(v, f):
    return 1 - v if f else v


def _allgather_small(v, name):
    r, n = v.shape

    def body(v_ref, out_ref, send_sems, recv_sems):
        x, y, c = lax.axis_index("x"), lax.axis_index("y"), lax.axis_index("c")
        me = 4 * x + 2 * y + c
        out_ref[me] = v_ref[...]
        peers = []
        for k in range(1, 8):
            peers.append((_flip(x, k & 4), _flip(y, k & 2), _flip(c, k & 1)))
        sends = []
        for k, peer in enumerate(peers):
            cp = pltpu.make_async_remote_copy(
                src_ref=v_ref, dst_ref=out_ref.at[me], send_sem=send_sems.at[k], recv_sem=recv_sems.at[k],
                device_id=peer, device_id_type=MESH)
            cp.start()
            sends.append(cp)
        for k, peer in enumerate(peers):
            src = 4 * peer[0] + 2 * peer[1] + peer[2]
            pltpu.make_async_remote_copy(
                src_ref=v_ref, dst_ref=out_ref.at[src], send_sem=send_sems.at[k], recv_sem=recv_sems.at[k],
                device_id=peer, device_id_type=MESH).wait_recv()
        for cp in sends:
            cp.wait_send()

    return pl.pallas_call(
        body, name=name,
        out_shape=jax.ShapeDtypeStruct((8, r, n), F32),
        in_specs=[pl.BlockSpec(memory_space=pltpu.VMEM)],
        out_specs=pl.BlockSpec(memory_space=pltpu.VMEM),
        scratch_shapes=[pltpu.SemaphoreType.DMA((7,)), pltpu.SemaphoreType.DMA((7,))],
    )(v)


_CHIP_FLIPS = ((1, 0), (0, 1), (1, 1))


def _gather_inputs(c_pad, w_ada, b_ada_shard, shards, name):
    n = len(shards)
    n_col = w_ada.shape[1]

    def body(*refs):
        c_ref, wa_ref, ba_ref = refs[:3]
        ins = refs[3:3 + n]
        ada_ref = refs[3 + n]
        outs = refs[4 + n:4 + 2 * n]
        call_ref, send_sems, recv_sems = refs[4 + 2 * n:7 + 2 * n]
        x, y, c = lax.axis_index("x"), lax.axis_index("y"), lax.axis_index("c")
        k_me = 2 * x + y
        me = 2 * k_me + c
        sibling = (x, y, 1 - c)
        chips = [(_flip(x, fx), _flip(y, fy)) for fx, fy in _CHIP_FLIPS]

        def piece(i, chip_k, half):
            hr = ins[i].shape[0] // 2
            return outs[i].at[chip_k, pl.ds(half * hr, hr), :]

        def copy(i, slot, chip_k, half, to):
            return pltpu.make_async_remote_copy(
                src_ref=piece(i, chip_k, half), dst_ref=piece(i, chip_k, half),
                send_sem=send_sems.at[6 * i + slot], recv_sem=recv_sems.at[6 * i + slot],
                device_id=to, device_id_type=MESH)

        def small(ref, slot, sem, to):
            return pltpu.make_async_remote_copy(
                src_ref=ref.at[slot], dst_ref=ref.at[slot], send_sem=send_sems.at[6 * n + sem],
                recv_sem=recv_sems.at[6 * n + sem], device_id=to, device_id_type=MESH)

        for i in range(n):
            outs[i][k_me] = ins[i][...].astype(BF16)
        started = []
        for i in range(n):
            for j, chip in enumerate(chips):
                cp = copy(i, j, k_me, c, (chip[0], chip[1], c))
                cp.start()
                started.append(cp)

        call_ref[me] = c_ref[...]
        peers = [(_flip(x, k & 4), _flip(y, k & 2), _flip(c, k & 1)) for k in range(1, 8)]
        for k, peer in enumerate(peers):
            cp = small(call_ref, me, k, peer)
            cp.start()
            started.append(cp)
        for k, peer in enumerate(peers):
            small(call_ref, 4 * peer[0] + 2 * peer[1] + peer[2], k, peer).wait_recv()
        c_all = call_ref[:, 0, :].astype(BF16)
        ada_ref[k_me] = _dot(c_all, wa_ref[...].astype(BF16)) + ba_ref[...]
        for j, chip in enumerate(chips):
            cp = small(ada_ref, k_me, 7 + j, (chip[0], chip[1], c))
            cp.start()
            started.append(cp)

        for j, chip in enumerate(chips):
            chip_k = 2 * chip[0] + chip[1]
            for i in range(n):
                copy(i, j, chip_k, c, (chip[0], chip[1], c)).wait_recv()
                cp = copy(i, 3 + j, chip_k, c, sibling)
                cp.start()
                started.append(cp)
        for j, chip in enumerate(chips):
            chip_k = 2 * chip[0] + chip[1]
            small(ada_ref, chip_k, 7 + j, (chip[0], chip[1], c)).wait_recv()
            for i in range(n):
                copy(i, 3 + j, chip_k, 1 - c, sibling).wait_recv()
        for cp in started:
            cp.wait_send()

    vmem = pl.BlockSpec(memory_space=pltpu.VMEM)
    return pl.pallas_call(
        body, name=name,
        out_shape=[jax.ShapeDtypeStruct((4, 8, n_col), F32)] + [jax.ShapeDtypeStruct((4,) + s.shape, BF16) for s in shards],
        in_specs=[vmem] * (3 + n),
        out_specs=[vmem] * (1 + n),
        scratch_shapes=[pltpu.VMEM((8,) + c_pad.shape, F32),
                        pltpu.SemaphoreType.DMA((6 * n + 10,)), pltpu.SemaphoreType.DMA((6 * n + 10,))],
        compiler_params=_params(vmem=VMEM_LIMIT),
    )(c_pad, w_ada, b_ada_shard, *shards)


def _reduce_scatter(pieces, name):
    n = len(pieces)

    def body(*refs):
        ins, outs = refs[:n], refs[n:2 * n]
        own, got = refs[2 * n:3 * n], refs[3 * n:4 * n]
        sendb, recvb = refs[4 * n:5 * n], refs[5 * n:6 * n]
        send_sems, recv_sems, local_sems = refs[6 * n:6 * n + 3]
        x, y, c = lax.axis_index("x"), lax.axis_index("y"), lax.axis_index("c")
        k_me = 2 * x + y
        sibling = (x, y, 1 - c)
        chips = [(_flip(x, fx), _flip(y, fy)) for fx, fy in _CHIP_FLIPS]
        hrs = [p.shape[1] // 2 for p in pieces]

        def remote(i, slot, src, dst, to):
            return pltpu.make_async_remote_copy(
                src_ref=src, dst_ref=dst, send_sem=send_sems.at[5 * i + slot], recv_sem=recv_sems.at[5 * i + slot],
                device_id=to, device_id_type=MESH)

        started = []
        loads = []
        for i in range(n):
            ld = pltpu.make_async_copy(ins[i].at[:, pl.ds(c * hrs[i], hrs[i]), :], own[i], local_sems.at[i])
            ld.start()
            loads.append(ld)
            cp = remote(i, 0, ins[i].at[:, pl.ds((1 - c) * hrs[i], hrs[i]), :], got[i], sibling)
            cp.start()
            started.append(cp)
        for i in range(n):
            loads[i].wait()
            remote(i, 0, ins[i].at[:, pl.ds(c * hrs[i], hrs[i]), :], got[i], sibling).wait_recv()
            for j, chip in enumerate(chips):
                chip_k = 2 * chip[0] + chip[1]
                sendb[i][j] = (own[i][chip_k] + got[i][chip_k]).astype(BF16)
                cp = remote(i, 1 + j, sendb[i].at[j], recvb[i].at[j], (chip[0], chip[1], c))
                cp.start()
                started.append(cp)
        for i in range(n):
            acc = own[i][k_me] + got[i][k_me]
            for j, chip in enumerate(chips):
                remote(i, 1 + j, sendb[i].at[j], recvb[i].at[j], (chip[0], chip[1], c)).wait_recv()
                acc = acc + recvb[i][j].astype(F32)
            mine = outs[i].at[pl.ds(c * hrs[i], hrs[i]), :]
            outs[i][pl.ds(pl.multiple_of(c * hrs[i], SUBLANES), hrs[i]), :] = acc
            cp = remote(i, 4, mine, mine, sibling)
            cp.start()
            started.append(cp)
        for i in range(n):
            theirs = outs[i].at[pl.ds((1 - c) * hrs[i], hrs[i]), :]
            remote(i, 4, theirs, theirs, sibling).wait_recv()
        for cp in started:
            cp.wait_send()

    scratch = []
    scratch += [pltpu.VMEM((4, p.shape[1] // 2, p.shape[2]), F32) for p in pieces]
    scratch += [pltpu.VMEM((4, p.shape[1] // 2, p.shape[2]), F32) for p in pieces]
    scratch += [pltpu.VMEM((3, p.shape[1] // 2, p.shape[2]), BF16) for p in pieces]
    scratch += [pltpu.VMEM((3, p.shape[1] // 2, p.shape[2]), BF16) for p in pieces]
    scratch += [pltpu.SemaphoreType.DMA((5 * n,)), pltpu.SemaphoreType.DMA((5 * n,)), pltpu.SemaphoreType.DMA((n,))]
    return pl.pallas_call(
        body, name=name,
        out_shape=[jax.ShapeDtypeStruct(p.shape[1:], F32) for p in pieces],
        in_specs=[pl.BlockSpec(memory_space=pl.ANY)] * n,
        out_specs=[pl.BlockSpec(memory_space=pltpu.VMEM)] * n,
        scratch_shapes=scratch,
        compiler_params=_params(vmem=VMEM_LIMIT),
    )(*pieces)


def _rope_fwd(t, cos, sin_lo, sin_hi):
    return t * cos + pltpu.roll(t, 96, 1) * sin_lo + pltpu.roll(t, 32, 1) * sin_hi


def _norm_proj(x, gmod, shift, w_int, cos, sin_lo, sin_hi, tm=512):
    s = x.shape[0]

    def body(x_ref, g_ref, sh_ref, w_ref, cos_ref, sl_ref, sh2_ref, a_ref, b_ref, f_ref, gg_ref, ht_ref):
        xv = x_ref[...]
        r = lax.rsqrt(jnp.mean(xv * xv, axis=-1, keepdims=True) + NORM_EPS)
        hf = (xv * r) * g_ref[...] + sh_ref[...]
        hb = hf.astype(BF16)
        ht_ref[...] = hf.T.astype(BF16)
        pa = _dot(hb, w_ref[:, OFF_A:OFF_A + W_A])
        cosv, sl, sh2 = cos_ref[...], sl_ref[...], sh2_ref[...]
        for j in range(5):
            t = pa[:, LANES * j:LANES * (j + 1)]
            a_ref[:, LANES * j:LANES * (j + 1)] = _rope_fwd(t, cosv, sl, sh2).astype(BF16)
        a_ref[:, 640:768] = pa[:, 640:768].astype(BF16)
        b_ref[...] = _dot(hb, w_ref[:, OFF_B:OFF_B + W_B]).astype(BF16)
        f_ref[...] = _dot(hb, w_ref[:, OFF_F:OFF_F + W_F])
        gg_ref[...] = _dot(hb, w_ref[:, OFF_G:OFF_G + W_G]).astype(BF16)

    row = lambda w: pl.BlockSpec((tm, w), lambda i: (i, 0))
    return pl.pallas_call(
        body, name="norm_proj", grid=(s // tm,),
        out_shape=[jax.ShapeDtypeStruct((s, W_A), BF16), jax.ShapeDtypeStruct((s, W_B), BF16),
                   jax.ShapeDtypeStruct((s, W_F), F32), jax.ShapeDtypeStruct((s, W_G), BF16),
                   jax.ShapeDtypeStruct((D_MODEL, s), BF16)],
        in_specs=[row(D_MODEL), _const_spec((1, D_MODEL)), _const_spec((1, D_MODEL)), _const_spec((D_MODEL, W_INT)),
                  row(LANES), row(LANES), row(LANES)],
        out_specs=[row(W_A), row(W_B), row(W_F), row(W_G), pl.BlockSpec((D_MODEL, tm), lambda i: (0, i))],
        compiler_params=_params(("parallel",), VMEM_LIMIT),
    )(x, gmod, shift, w_int, cos, sin_lo, sin_hi)


def _log_sigmoid(u):
    return jnp.minimum(u, 0.0) - jnp.log(1.0 + jnp.exp(-jnp.abs(u)))


def _fox_cumsum(f, bf_pad, tb=512):
    s = f.shape[0]

    def body(f_ref, b_ref, cum_ref, carry):
        @pl.when(pl.program_id(0) == 0)
        def _():
            carry[...] = jnp.zeros_like(carry)

        lane = lax.broadcasted_iota(jnp.int32, (tb, LANES), 1)
        logf = jnp.where(lane < N_HEADS, _log_sigmoid(f_ref[...] + b_ref[...]), 0.0)
        hi, mid, lo = _split3(logf)
        rows = lax.broadcasted_iota(jnp.int32, (tb, tb), 0)
        cols = lax.broadcasted_iota(jnp.int32, (tb, tb), 1)
        tril = (cols <= rows).astype(BF16)
        cum = _dot(tril, hi) + _dot(tril, mid) + _dot(tril, lo) + carry[0:1, :]
        cum_ref[...] = cum
        carry[...] = jnp.broadcast_to(cum[tb - 1:tb, :], carry.shape)

    return pl.pallas_call(
        body, name="fox_cumsum", grid=(s // tb,),
        out_shape=jax.ShapeDtypeStruct((s, LANES), F32),
        in_specs=[pl.BlockSpec((tb, LANES), lambda i: (i, 0)), _const_spec((1, LANES))],
        out_specs=pl.BlockSpec((tb, LANES), lambda i: (i, 0)),
        scratch_shapes=[pltpu.VMEM((SUBLANES, LANES), F32)],
        compiler_params=_params(("arbitrary",)),
    )(f, bf_pad)


def _fox_prep(b, cum, t):
    s = b.shape[0]

    def body(b_ref, cum_ref, q_ref, k_ref, v_ref, st_ref):
        lane = lax.broadcasted_iota(jnp.int32, (t, LANES), 1)
        sub8 = lax.broadcasted_iota(jnp.int32, (SUBLANES, LANES), 0)
        lane8 = lax.broadcasted_iota(jnp.int32, (SUBLANES, LANES), 1)
        cumv = cum_ref[...]
        stats = jnp.zeros((SUBLANES, LANES), F32)
        for h in range(N_HEADS):
            p, odd = h // 2, h % 2
            ch = jnp.broadcast_to(cumv[:, h:h + 1], (t, LANES))
            hi, mid, lo = (x.astype(F32) for x in _split3(ch))
            qp = b_ref[:, LANES * p:LANES * (p + 1)].astype(F32)
            kp = b_ref[:, 512 + LANES * p:512 + LANES * (p + 1)].astype(F32)
            vp = b_ref[:, 1024 + LANES * p:1024 + LANES * (p + 1)].astype(F32)
            if odd:
                qp, kp, vp = pltpu.roll(qp, 64, 1), pltpu.roll(kp, 64, 1), pltpu.roll(vp, 64, 1)
            qs = jnp.where(lane < 64, qp * SCALE, 0.0)
            ks = jnp.where(lane < 64, kp, 0.0)
            qa = jnp.where(lane < 64, qs,
                           jnp.where(lane == 64, hi, jnp.where(lane == 65, mid, jnp.where(lane == 66, lo,
                           jnp.where(lane < 70, 1.0, 0.0)))))
            ka = jnp.where(lane < 64, ks,
                           jnp.where(lane < 67, 1.0, jnp.where(lane == 67, -hi, jnp.where(lane == 68, -mid,
                           jnp.where(lane == 69, -lo, 0.0)))))
            q_ref[:, LANES * h:LANES * (h + 1)] = qa.astype(BF16)
            k_ref[:, LANES * h:LANES * (h + 1)] = ka.astype(BF16)
            v_ref[:, LANES * h:LANES * (h + 1)] = jnp.where(lane < 64, vp, jnp.where(lane == 64, 1.0, 0.0)).astype(BF16)
            qn = jnp.sqrt(jnp.max(jnp.sum(qs * qs, axis=-1, keepdims=True)))
            kn = jnp.sqrt(jnp.max(jnp.sum(ks * ks, axis=-1, keepdims=True)))
            dmin = jnp.min(jnp.sum(qs * ks, axis=-1, keepdims=True))
            csum = hi + mid + lo
            row = jnp.where(lane8 == 0, qn, jnp.where(lane8 == 1, kn, jnp.where(
                lane8 == 2, csum[0:1, :], jnp.where(lane8 == 3, csum[t - 1:t, :], jnp.where(lane8 == 4, dmin, 0.0)))))
            stats = jnp.where(sub8 == h, row, stats)
        st_ref[0] = stats

    wide = pl.BlockSpec((t, 1024), lambda i: (i, 0))
    return pl.pallas_call(
        body, name="fox_prep", grid=(s // t,),
        out_shape=[jax.ShapeDtypeStruct((s, 1024), BF16)] * 3 + [jax.ShapeDtypeStruct((s // t, SUBLANES, LANES), F32)],
        in_specs=[pl.BlockSpec((t, W_B), lambda i: (i, 0)), pl.BlockSpec((t, LANES), lambda i: (i, 0))],
        out_specs=[wide, wide, wide, pl.BlockSpec((1, SUBLANES, LANES), lambda i: (i, 0, 0))],
        compiler_params=_params(("parallel",)),
    )(b, cum)


PRUNE_MARGIN = 90.0


def _fox_tile_ranges(stats):
    nt = stats.shape[0]
    qk_max = 1.01 * jnp.max(stats[:, :, 0], axis=0) * jnp.max(stats[:, :, 1], axis=0)
    ball = qk_max - jnp.minimum(jnp.min(stats[:, :, 4], axis=0), 0.0) + 0.05
    d = ball[None, None, :] + stats[:, None, :, 2] - stats[None, :, :, 3]
    idx = jnp.arange(nt)
    skip = (d <= -PRUNE_MARGIN) & (idx[None, :, None] < idx[:, None, None])
    skip = skip.reshape(nt, nt, 4, 2).all(axis=-1)
    first_key = jnp.sum(skip, axis=1).astype(F32)
    needed = (~skip) & (idx[None, :, None] <= idx[:, None, None])
    n_query = jnp.sum(needed, axis=0).astype(F32)
    table = jnp.zeros((4, SUBLANES, LANES), F32)
    table = table.at[:, 0, :nt].set(first_key.T)
    table = table.at[:, 1, :nt].set(n_query.T)
    return table


def _lane_scalar(block, row, lane_idx):
    sub8 = lax.broadcasted_iota(jnp.int32, (SUBLANES, LANES), 0)
    lane8 = lax.broadcasted_iota(jnp.int32, (SUBLANES, LANES), 1)
    return jnp.sum(jnp.where((sub8 == row) & (lane8 == lane_idx), block, 0.0)).astype(jnp.int32)


def _fox_fwd(qa, ka, va, ranges, t):
    s = qa.shape[0]
    nt = s // t
    nc = t // LANES

    def body(rg_ref, q_ref, k_ref, v_ref, o_ref, lse_ref):
        i = pl.program_id(1)
        lane = lax.broadcasted_iota(jnp.int32, (t, LANES), 1)
        rows = lax.broadcasted_iota(jnp.int32, (t, t), 0)
        cols = lax.broadcasted_iota(jnp.int32, (t, t), 1)
        first = jnp.clip(_lane_scalar(rg_ref[0], 0, i), 0, i)

        def tile(j, carry, diagonal):
            off = pl.multiple_of(j * t, t)
            kt = k_ref[pl.ds(off, t), :]
            vt = v_ref[pl.ds(off, t), :]
            heads = range(2)
            scs = [_dot(q_ref[:, LANES * hh:LANES * (hh + 1)], kt[:, LANES * hh:LANES * (hh + 1)], NT) for hh in heads]
            if diagonal:
                scs = [jnp.where(cols <= rows, sc, NEG) for sc in scs]
            m_new = []
            for hh in heads:
                part = scs[hh][:, 0:LANES]
                for cch in range(1, nc):
                    part = jnp.maximum(part, scs[hh][:, LANES * cch:LANES * (cch + 1)])
                m_new.append(jnp.maximum(carry[2 * hh], jnp.max(part, axis=-1, keepdims=True)))
            alphas = [jnp.exp(carry[2 * hh] - m_new[hh]) for hh in heads]
            ps = [jnp.exp(scs[hh] - m_new[hh]).astype(BF16) for hh in heads]
            pvs = [_dot(ps[hh], vt[:, LANES * hh:LANES * (hh + 1)]) for hh in heads]
            return (m_new[0], alphas[0] * carry[1] + pvs[0], m_new[1], alphas[1] * carry[3] + pvs[1])

        col0 = jnp.full((t, 1), NEG, F32)
        zero = jnp.zeros((t, LANES), F32)
        carry = lax.fori_loop(first, i, lambda j, cr: tile(j, cr, False), (col0, zero, col0, zero))
        m0, acc0, m1, acc1 = tile(i, carry, True)
        l0, l1 = acc0[:, 64:65], acc1[:, 64:65]
        o_ref[...] = jnp.where(lane < 64, acc0 * (1.0 / l0), pltpu.roll(acc1 * (1.0 / l1), 64, 1)).astype(BF16)
        sub = lax.broadcasted_iota(jnp.int32, (SUBLANES, t), 0)
        lse0 = jnp.broadcast_to(m0 + jnp.log(l0), (t, LANES)).T[0:SUBLANES, :]
        lse1 = jnp.broadcast_to(m1 + jnp.log(l1), (t, LANES)).T[0:SUBLANES, :]
        lse_ref[0] = jnp.where(sub == 0, lse0, jnp.where(sub == 1, lse1, 0.0))

    pair = pl.BlockSpec((s, 2 * LANES), lambda p, i: (0, p))
    return pl.pallas_call(
        body, name="fox_fwd", grid=(4, nt),
        out_shape=[jax.ShapeDtypeStruct((s, 512), BF16), jax.ShapeDtypeStruct((4, SUBLANES, s), F32)],
        in_specs=[pl.BlockSpec((1, SUBLANES, LANES), lambda p, i: (p, 0, 0)),
                  pl.BlockSpec((t, 2 * LANES), lambda p, i: (i, p)), pair, pair],
        out_specs=[pl.BlockSpec((t, LANES), lambda p, i: (i, p)),
                   pl.BlockSpec((1, SUBLANES, t), lambda p, i: (p, 0, i))],
        compiler_params=_params(("parallel", "arbitrary"), VMEM_LIMIT),
    )(ranges, qa, ka, va)


def _dup_halves(blk, lane):
    f = blk.astype(F32)
    r = pltpu.roll(f, 64, 1)
    return jnp.where(lane < 64, f, r).astype(BF16), jnp.where(lane >= 64, f, r).astype(BF16)


GROUP = 4
GROUP_ROWS = GROUP * WINDOW


def _stack_heads(ref, g, lane):
    parts = []
    for pb in (2 * g, 2 * g + 1):
        blk = ref[:, LANES * pb:LANES * (pb + 1)]
        zero = jnp.zeros_like(blk)
        parts += [jnp.where(lane < 64, blk, zero), jnp.where(lane >= 64, blk, zero)]
    return jnp.concatenate(parts, axis=0)


def _swa_band(a_ref, ap_ref, g, lane):
    k = jnp.concatenate([_dup_halves(ap_ref[:, 512:640], lane)[g], _dup_halves(a_ref[:, 512:640], lane)[g]], axis=0)
    v = jnp.concatenate([_dup_halves(ap_ref[:, 640:768], lane)[g], _dup_halves(a_ref[:, 640:768], lane)[g]], axis=0)
    return k, v


def _swa_logits(q, k, has_prev):
    sc = _dot(q, k, NT) * SCALE
    rr = lax.broadcasted_iota(jnp.int32, sc.shape, 0) % WINDOW
    cc = lax.broadcasted_iota(jnp.int32, sc.shape, 1)
    valid = (cc > rr) & (cc <= rr + WINDOW) & (has_prev | (cc >= WINDOW))
    return jnp.where(valid, sc, NEG)


def _per_head_column(values):
    return jnp.concatenate([jnp.broadcast_to(v, (WINDOW, 1)) for v in values], axis=0)


SWA_BLOCKS = 4
SWA_ROWS = SWA_BLOCKS * WINDOW


def _swa_blocks(a_ref, ap_ref):
    return [ap_ref] + [a_ref.at[pl.ds(WINDOW * jb, WINDOW), :] for jb in range(SWA_BLOCKS)]


def _swa_fwd(a, sinks):
    s = a.shape[0]

    def body(sink_ref, a_ref, ap_ref, o_ref, l_ref):
        lane = lax.broadcasted_iota(jnp.int32, (WINDOW, LANES), 1)
        blocks = _swa_blocks(a_ref, ap_ref)
        units = [(jb, g) for jb in range(SWA_BLOCKS) for g in range(2)]
        sinks_col = [_per_head_column([sink_ref[GROUP * g + hh] for hh in range(GROUP)]) for g in range(2)]
        bands = [_swa_band(blocks[jb + 1], blocks[jb], g, lane) for jb, g in units]
        scs = [_swa_logits(_stack_heads(blocks[jb + 1], g, lane), bands[u][0],
                           (pl.program_id(0) > 0) if jb == 0 else True) for u, (jb, g) in enumerate(units)]
        ms = [jnp.maximum(jnp.max(scs[u], axis=-1, keepdims=True), sinks_col[g]) for u, (jb, g) in enumerate(units)]
        ps = [jnp.exp(scs[u] - ms[u]) for u in range(len(units))]
        dens = [jnp.sum(ps[u], axis=-1, keepdims=True) + jnp.exp(sinks_col[g] - ms[u]) for u, (jb, g) in enumerate(units)]
        outs = [_dot((ps[u] * (1.0 / dens[u])).astype(BF16), bands[u][1]) for u in range(len(units))]
        for jb in range(SWA_BLOCKS):
            rows = slice(WINDOW * jb, WINDOW * (jb + 1))
            l_all = jnp.zeros((WINDOW, LANES), F32)
            for g in range(2):
                u = 2 * jb + g
                lcol = ms[u] + jnp.log(dens[u])
                for pb in range(2):
                    r0 = 2 * pb * WINDOW
                    o_ref[rows, LANES * (2 * g + pb):LANES * (2 * g + pb + 1)] = jnp.where(
                        lane < 64, outs[u][r0:r0 + WINDOW], outs[u][r0 + WINDOW:r0 + 2 * WINDOW]).astype(BF16)
                for hh in range(GROUP):
                    l_all = jnp.where(lane == GROUP * g + hh, lcol[WINDOW * hh:WINDOW * (hh + 1)], l_all)
            l_ref[rows, :] = l_all

    return pl.pallas_call(
        body, name="swa_fwd", grid=(s // SWA_ROWS,),
        out_shape=[jax.ShapeDtypeStruct((s, 512), BF16), jax.ShapeDtypeStruct((s, LANES), F32)],
        in_specs=[pl.BlockSpec(memory_space=pltpu.SMEM),
                  pl.BlockSpec((SWA_ROWS, W_A), lambda i: (i, 0)),
                  pl.BlockSpec((WINDOW, W_A), lambda i: (jnp.maximum(SWA_BLOCKS * i - 1, 0), 0))],
        out_specs=[pl.BlockSpec((SWA_ROWS, 512), lambda i: (i, 0)), pl.BlockSpec((SWA_ROWS, LANES), lambda i: (i, 0))],
        compiler_params=_params(("parallel",)),
    )(sinks, a, a)


def _mid(att_a, att_b, g, x, target, gate, g_final, wo_a, wo_b, w_out, tm=256):
    s = x.shape[0]
    nt = s // tm

    def body(aa_ref, ab_ref, g_ref, x_ref, t_ref, gate_ref, gf_ref, woa_ref, wob_ref, wout_ref,
             dx_ref, daa_ref, dab_ref, dg_ref, delta_ref, dwoa_ref, dwob_ref, dwout_ref, vec_ref,
             acc_gf, acc_gate, acc_loss):
        step = pl.program_id(0)

        @pl.when(step == 0)
        def _():
            dwoa_ref[...] = jnp.zeros_like(dwoa_ref)
            dwob_ref[...] = jnp.zeros_like(dwob_ref)
            dwout_ref[...] = jnp.zeros_like(dwout_ref)
            acc_gf[...] = jnp.zeros_like(acc_gf)
            acc_gate[...] = jnp.zeros_like(acc_gate)
            acc_loss[...] = jnp.zeros_like(acc_loss)

        def fold(v):
            return jnp.sum(v.reshape(tm // SUBLANES, SUBLANES, D_MODEL), axis=0)

        gate = gate_ref[...]
        gfin = gf_ref[...]
        branches = []
        for att_ref, z_off, wo_ref in ((aa_ref, 0, woa_ref), (ab_ref, 512, wob_ref)):
            att = att_ref[...].astype(F32)
            z = g_ref[:, z_off:z_off + 512].astype(F32)
            sz = _sigmoid(z)
            silu = z * sz
            u = (att * silu).astype(BF16)
            branches.append((att, z, sz, silu, u, _dot(u, wo_ref[...])))
        ga = g_ref[:, 1024:2048].astype(F32)
        gb = g_ref[:, 2048:3072].astype(F32)
        sga, sgb = _sigmoid(ga), _sigmoid(gb)
        y_a, y_b = branches[0][5], branches[1][5]
        mb = (sga * y_a + sgb * y_b).astype(BF16)
        o = _dot(mb, wout_ref[...])
        x2 = x_ref[...] + gate * o
        r2 = lax.rsqrt(jnp.mean(x2 * x2, axis=-1, keepdims=True) + NORM_EPS)
        xn2 = x2 * r2
        err = xn2 * gfin - t_ref[...]
        acc_loss[...] += fold(err * err)
        dy = err * (1.0 / D_MODEL)
        acc_gf[...] += fold(dy * xn2)
        dxn = dy * gfin
        dx2 = r2 * (dxn - xn2 * jnp.mean(dxn * xn2, axis=-1, keepdims=True))
        dx_ref[...] = dx2
        acc_gate[...] += fold(dx2 * o)
        d_o = (dx2 * gate).astype(BF16)
        dwout_ref[...] += _dot(mb, d_o, TN)
        dm = _dot(d_o, wout_ref[...], NT)
        dg_ref[:, 1024:2048] = (dm * y_a * sga * (1.0 - sga)).astype(BF16)
        dg_ref[:, 2048:3072] = (dm * y_b * sgb * (1.0 - sgb)).astype(BF16)
        for (att, z, sz, silu, u, _), sg, wo_ref, dwo_ref, datt_ref, z_off in (
                (branches[0], sga, woa_ref, dwoa_ref, daa_ref, 0), (branches[1], sgb, wob_ref, dwob_ref, dab_ref, 512)):
            dyb = (dm * sg).astype(BF16)
            dwo_ref[...] += _dot(u, dyb, TN)
            du = _dot(dyb, wo_ref[...], NT)
            datt = du * silu
            datt_ref[...] = datt.astype(BF16)
            dg_ref[:, z_off:z_off + 512] = (du * att * (sz * (1.0 + z * (1.0 - sz)))).astype(BF16)
            if z_off == 512:
                prod = datt * att
                hi = prod.astype(BF16)
                lo = (prod - hi.astype(F32)).astype(BF16)
                er = lax.broadcasted_iota(jnp.int32, (512, LANES), 0)
                ec = lax.broadcasted_iota(jnp.int32, (512, LANES), 1)
                e = (er // HEAD_DIM == ec).astype(BF16)
                delta = _dot(hi, e) + _dot(lo, e)
                delta_ref[...] = delta.T[0:SUBLANES, :]

        @pl.when(step == nt - 1)
        def _():
            sub = lax.broadcasted_iota(jnp.int32, (SUBLANES, D_MODEL), 0)
            dgf = jnp.sum(acc_gf[...], axis=0, keepdims=True)
            dgate = jnp.sum(acc_gate[...], axis=0, keepdims=True)
            loss = 0.5 * jnp.sum(acc_loss[...]) * (1.0 / D_MODEL)
            vec_ref[...] = jnp.where(sub == 0, dgf, jnp.where(sub == 1, dgate, jnp.where(sub == 2, loss, 0.0)))

    row = lambda w: pl.BlockSpec((tm, w), lambda i: (i, 0))
    return pl.pallas_call(
        body, name="mid", grid=(nt,),
        out_shape=[jax.ShapeDtypeStruct((s, D_MODEL), F32), jax.ShapeDtypeStruct((s, 512), BF16),
                   jax.ShapeDtypeStruct((s, 512), BF16), jax.ShapeDtypeStruct((s, W_G), BF16),
                   jax.ShapeDtypeStruct((SUBLANES, s), F32),
                   jax.ShapeDtypeStruct((512, D_MODEL), F32), jax.ShapeDtypeStruct((512, D_MODEL), F32),
                   jax.ShapeDtypeStruct((D_MODEL, D_MODEL), F32), jax.ShapeDtypeStruct((SUBLANES, D_MODEL), F32)],
        in_specs=[row(512), row(512), row(W_G), row(D_MODEL), row(D_MODEL),
                  _const_spec((1, D_MODEL)), _const_spec((1, D_MODEL)),
                  _const_spec((512, D_MODEL)), _const_spec((512, D_MODEL)), _const_spec((D_MODEL, D_MODEL))],
        out_specs=[row(D_MODEL), row(512), row(512), row(W_G),
                   pl.BlockSpec((SUBLANES, tm), lambda i: (0, i)),
                   pl.BlockSpec((512, D_MODEL), lambda i: (0, 0)), pl.BlockSpec((512, D_MODEL), lambda i: (0, 0)),
                   pl.BlockSpec((D_MODEL, D_MODEL), lambda i: (0, 0)), pl.BlockSpec((SUBLANES, D_MODEL), lambda i: (0, 0))],
        scratch_shapes=[pltpu.VMEM((SUBLANES, D_MODEL), F32)] * 3,
        compiler_params=_params(("arbitrary",), VMEM_LIMIT),
    )(att_a, att_b, g, x, target, gate, g_final, wo_a, wo_b, w_out)


def _rope_bwd(dt, cos, sin, lane):
    u = dt * sin
    lo = (lane % HEAD_DIM) < (HEAD_DIM // 2)
    return dt * cos + jnp.where(lo, pltpu.roll(u, 96, 1), -pltpu.roll(u, 32, 1))


def _swa_bwd(a, datt, l_all, sinks, cos, sin):
    s = a.shape[0]
    nt = s // SWA_ROWS

    def body(sink_ref, a_ref, ap_ref, do_ref, l_ref, cos_ref, sin_ref, da_ref, ds_ref, halo):
        step = pl.program_id(0)
        tile = nt - 1 - step

        @pl.when(step == 0)
        def _():
            halo[...] = jnp.zeros_like(halo)
            ds_ref[...] = jnp.zeros_like(ds_ref)

        lane = lax.broadcasted_iota(jnp.int32, (WINDOW, LANES), 1)
        sub8 = lax.broadcasted_iota(jnp.int32, (SUBLANES, LANES), 0)
        lane8 = lax.broadcasted_iota(jnp.int32, (SUBLANES, LANES), 1)
        blocks = _swa_blocks(a_ref, ap_ref)
        dsink = jnp.zeros((SUBLANES, LANES), F32)

        def join(pair, r0):
            x0, x1 = pair[0][r0:r0 + WINDOW], pair[1][r0:r0 + WINDOW]
            return jnp.where(lane < 64, x0 + pltpu.roll(x0, 64, 1), x1 + pltpu.roll(x1, 64, 1))

        units = [(jb, g) for jb in range(SWA_BLOCKS) for g in range(2)]
        n_u = len(units)
        sinks_col = [_per_head_column([sink_ref[GROUP * g + hh] for hh in range(GROUP)]) for g in range(2)]
        bands = [_swa_band(blocks[jb + 1], blocks[jb], g, lane) for jb, g in units]
        qs = [_stack_heads(blocks[jb + 1], g, lane) for jb, g in units]
        doms = [_stack_heads(do_ref.at[pl.ds(WINDOW * jb, WINDOW), :], g, lane) for jb, g in units]
        lcols = []
        for jb, g in units:
            lv = l_ref[WINDOW * jb:WINDOW * (jb + 1), :]
            lcols.append(_per_head_column([lv[:, GROUP * g + hh:GROUP * g + hh + 1] for hh in range(GROUP)]))
        ps = [jnp.exp(_swa_logits(qs[u], bands[u][0], (tile > 0) if jb == 0 else True) - lcols[u])
              for u, (jb, g) in enumerate(units)]
        dps = [_dot(doms[u], bands[u][1], NT) for u in range(n_u)]
        deltas = [jnp.sum(ps[u] * dps[u], axis=-1, keepdims=True) for u in range(n_u)]
        for u, (jb, g) in enumerate(units):
            sink_term = jnp.exp(sinks_col[g] - lcols[u]) * deltas[u]
            for hh in range(GROUP):
                tot = jnp.sum(sink_term[WINDOW * hh:WINDOW * (hh + 1)])
                dsink = dsink + jnp.where((sub8 == 0) & (lane8 == GROUP * g + hh), -tot, 0.0)
        dss = [(ps[u] * (dps[u] - deltas[u])).astype(BF16) for u in range(n_u)]
        dqs = [_dot(dss[u], bands[u][0]) * SCALE for u in range(n_u)]
        dks = [_dot(dss[u], qs[u], TN) * SCALE for u in range(n_u)]
        dvs = [_dot(ps[u].astype(BF16), doms[u], TN) for u in range(n_u)]

        carry_k, carry_v = halo[:, 0:LANES], halo[:, LANES:2 * LANES]
        for jb in reversed(range(SWA_BLOCKS)):
            rows = slice(WINDOW * jb, WINDOW * (jb + 1))
            cosv, sinv = cos_ref[rows, :], sin_ref[rows, :]
            for g in range(2):
                dq = dqs[2 * jb + g]
                for pb in range(2):
                    r0 = 2 * pb * WINDOW
                    dq_pair = jnp.where(lane < 64, dq[r0:r0 + WINDOW], dq[r0 + WINDOW:r0 + 2 * WINDOW])
                    da_ref[rows, LANES * (2 * g + pb):LANES * (2 * g + pb + 1)] = _rope_bwd(
                        dq_pair, cosv, sinv, lane).astype(BF16)
            dkb, dvb = dks[2 * jb:2 * jb + 2], dvs[2 * jb:2 * jb + 2]
            da_ref[rows, 512:640] = _rope_bwd(join(dkb, WINDOW) + carry_k, cosv, sinv, lane).astype(BF16)
            da_ref[rows, 640:768] = (join(dvb, WINDOW) + carry_v).astype(BF16)
            carry_k, carry_v = join(dkb, 0), join(dvb, 0)
        halo[:, 0:LANES] = carry_k
        halo[:, LANES:2 * LANES] = carry_v
        ds_ref[...] += dsink

    rev = lambda w: pl.BlockSpec((SWA_ROWS, w), lambda i: (nt - 1 - i, 0))
    return pl.pallas_call(
        body, name="swa_bwd", grid=(nt,),
        out_shape=[jax.ShapeDtypeStruct((s, W_A), BF16), jax.ShapeDtypeStruct((SUBLANES, LANES), F32)],
        in_specs=[pl.BlockSpec(memory_space=pltpu.SMEM), rev(W_A),
                  pl.BlockSpec((WINDOW, W_A), lambda i: (jnp.maximum(SWA_BLOCKS * (nt - 1 - i) - 1, 0), 0)),
                  rev(512), rev(LANES), rev(LANES), rev(LANES)],
        out_specs=[rev(W_A), pl.BlockSpec((SUBLANES, LANES), lambda i: (0, 0))],
        scratch_shapes=[pltpu.VMEM((WINDOW, 2 * LANES), F32)],
        compiler_params=_params(("arbitrary",)),
    )(sinks, a, a, datt, l_all, cos, sin)


def _fox_bwd(qa, ka, b, do, lse, delta, ranges, t):
    s = qa.shape[0]
    nt = s // t

    def body(rg_ref, q_ref, do_ref, lse_ref, dl_ref, k_ref, v_ref, dq_ref, dk_ref, dv_ref, dc_ref, dr_ref, dq_acc):
        p = pl.program_id(0)
        j = pl.program_id(1)
        n_query = jnp.clip(_lane_scalar(rg_ref[0], 1, j), 1, nt - j)

        @pl.when(j == 0)
        def _():
            dq_acc[...] = jnp.zeros_like(dq_acc)

        lane = lax.broadcasted_iota(jnp.int32, (t, LANES), 1)
        rows = lax.broadcasted_iota(jnp.int32, (t, t), 0)
        cols = lax.broadcasted_iota(jnp.int32, (t, t), 1)
        kt = k_ref[...]
        vt = v_ref[...]

        def tile(i, carry, diagonal):
            dk0, dk1, dv = carry
            off = pl.multiple_of(i * t, t)
            qt = q_ref[pl.ds(off, t), :]
            dot_ = do_ref[pl.ds(off, t), :]
            lse_t = lse_ref[0, :, pl.ds(off, t)]
            dl_t = dl_ref[0, :, pl.ds(off, t)]
            heads = range(2)
            qs = [qt[:, LANES * hh:LANES * (hh + 1)] for hh in heads]
            ks = [kt[:, LANES * hh:LANES * (hh + 1)] for hh in heads]
            doms = [jnp.where((lane < 64) if hh == 0 else (lane >= 64), dot_, jnp.zeros_like(dot_)) for hh in heads]
            sts = [_dot(ks[hh], qs[hh], NT) for hh in heads]
            dpts = [_dot(vt, doms[hh], NT) for hh in heads]
            if diagonal:
                sts = [jnp.where(cols >= rows, st, NEG) for st in sts]
            pts = [jnp.exp(sts[hh] - lse_t[hh:hh + 1, :]) for hh in heads]
            dsts = [(pts[hh] * (dpts[hh] - dl_t[hh:hh + 1, :])).astype(BF16) for hh in heads]
            dv = dv + _dot(pts[0].astype(BF16), doms[0]) + _dot(pts[1].astype(BF16), doms[1])
            dks = [_dot(dsts[hh], qs[hh]) for hh in heads]
            for hh in heads:
                dq_acc[hh, pl.ds(off, t), :] += _dot(dsts[hh], ks[hh], TN)
            return dk0 + dks[0], dk1 + dks[1], dv

        zero = jnp.zeros((t, LANES), F32)
        carry = tile(j, (zero, zero, zero), True)
        dk0, dk1, dv = lax.fori_loop(j + 1, j + n_query, lambda i, cr: tile(i, cr, False), carry)
        dk_ref[...] = jnp.where(lane < 64, dk0, pltpu.roll(dk1, 64, 1)).astype(BF16)
        dv_ref[...] = dv.astype(BF16)
        c0 = jnp.broadcast_to(dk0[:, 67:68], (t, LANES))
        c1 = jnp.broadcast_to(dk1[:, 67:68], (t, LANES))
        dc_ref[0] = jnp.where(lane == 2 * p, -c0, jnp.where(lane == 2 * p + 1, -c1, 0.0))

        @pl.when(j == nt - 1)
        def _():
            lane_s = lax.broadcasted_iota(jnp.int32, (s, LANES), 1)
            a0, a1 = dq_acc[0], dq_acc[1]
            dq_ref[...] = (jnp.where(lane_s < 64, a0, pltpu.roll(a1, 64, 1)) * SCALE).astype(BF16)
            r0 = jnp.broadcast_to(a0[:, 64:65], (s, LANES))
            r1 = jnp.broadcast_to(a1[:, 64:65], (s, LANES))
            dr_ref[0] = jnp.where(lane_s == 2 * p, r0, jnp.where(lane_s == 2 * p + 1, r1, 0.0))

    return pl.pallas_call(
        body, name="fox_bwd", grid=(4, nt),
        out_shape=[jax.ShapeDtypeStruct((s, 512), BF16), jax.ShapeDtypeStruct((s, 512), BF16),
                   jax.ShapeDtypeStruct((s, 512), BF16), jax.ShapeDtypeStruct((4, s, LANES), F32),
                   jax.ShapeDtypeStruct((4, s, LANES), F32)],
        in_specs=[pl.BlockSpec((1, SUBLANES, LANES), lambda p, j: (p, 0, 0)),
                  pl.BlockSpec((s, 2 * LANES), lambda p, j: (0, p)),
                  pl.BlockSpec((s, LANES), lambda p, j: (0, p)),
                  pl.BlockSpec((1, SUBLANES, s), lambda p, j: (p, 0, 0)),
                  pl.BlockSpec((1, SUBLANES, s), lambda p, j: (p, 0, 0)),
                  pl.BlockSpec((t, 2 * LANES), lambda p, j: (j, p)),
                  pl.BlockSpec((t, LANES), lambda p, j: (j, 8 + p))],
        out_specs=[pl.BlockSpec((s, LANES), lambda p, j: (0, p)),
                   pl.BlockSpec((t, LANES), lambda p, j: (j, p)),
                   pl.BlockSpec((t, LANES), lambda p, j: (j, p)),
                   pl.BlockSpec((1, t, LANES), lambda p, j: (p, j, 0)),
                   pl.BlockSpec((1, s, LANES), lambda p, j: (p, 0, 0))],
        scratch_shapes=[pltpu.VMEM((2, s, LANES), F32)],
        compiler_params=_params(("parallel", "arbitrary"), VMEM_LIMIT),
    )(ranges, qa, do, lse, delta, ka, b)


def _fox_cumsum_bwd(dcum_k, dcum_q, f, bf_pad, tb=512):
    s = f.shape[0]
    nb = s // tb

    def body(dc_ref, dr_ref, f_ref, b_ref, df_ref, db_ref, carry):
        step = pl.program_id(0)

        @pl.when(step == 0)
        def _():
            carry[...] = jnp.zeros_like(carry)
            db_ref[...] = jnp.zeros_like(db_ref)

        lane = lax.broadcasted_iota(jnp.int32, (tb, LANES), 1)
        dc = dc_ref[0] + dr_ref[0]
        for k in range(1, 4):
            dc = dc + (dc_ref[k] + dr_ref[k])
        hi, mid, lo = _split3(dc)
        rows = lax.broadcasted_iota(jnp.int32, (tb, tb), 0)
        cols = lax.broadcasted_iota(jnp.int32, (tb, tb), 1)
        triu = (cols >= rows).astype(BF16)
        dlogf = _dot(triu, hi) + _dot(triu, mid) + _dot(triu, lo) + carry[0:1, :]
        carry[...] = jnp.broadcast_to(dlogf[0:1, :], carry.shape)
        u = f_ref[...] + b_ref[...]
        dfb = jnp.where(lane < N_HEADS, dlogf * _sigmoid(-u), 0.0)
        df_ref[...] = dfb.astype(BF16)
        sub = lax.broadcasted_iota(jnp.int32, (SUBLANES, LANES), 0)
        db_ref[...] += jnp.where(sub == 0, jnp.sum(dfb, axis=0, keepdims=True), 0.0)

    return pl.pallas_call(
        body, name="fox_cumsum_bwd", grid=(nb,),
        out_shape=[jax.ShapeDtypeStruct((s, LANES), BF16), jax.ShapeDtypeStruct((SUBLANES, LANES), F32)],
        in_specs=[pl.BlockSpec((4, tb, LANES), lambda i: (0, nb - 1 - i, 0)),
                  pl.BlockSpec((4, tb, LANES), lambda i: (0, nb - 1 - i, 0)),
                  pl.BlockSpec((tb, LANES), lambda i: (nb - 1 - i, 0)), _const_spec((1, LANES))],
        out_specs=[pl.BlockSpec((tb, LANES), lambda i: (nb - 1 - i, 0)),
                   pl.BlockSpec((SUBLANES, LANES), lambda i: (0, 0))],
        scratch_shapes=[pltpu.VMEM((SUBLANES, LANES), F32)],
        compiler_params=_params(("arbitrary",)),
    )(dcum_k, dcum_q, f, bf_pad)


def _dh_norm_bwd(d_a, d_q, d_k, d_v, d_f, d_g, w_t, x, dx2, gnorm, scale1, tm=512):
    s = x.shape[0]
    nt = s // tm

    def body(da_ref, dq_ref, dk_ref, dv_ref, df_ref, dg_ref, w_ref, x_ref, dx2_ref, g_ref, sc_ref, gx_ref, vec_ref,
             a_sh, a_sc, a_g):
        step = pl.program_id(0)

        @pl.when(step == 0)
        def _():
            a_sh[...] = jnp.zeros_like(a_sh)
            a_sc[...] = jnp.zeros_like(a_sc)
            a_g[...] = jnp.zeros_like(a_g)

        def fold(v):
            return jnp.sum(v.reshape(tm // SUBLANES, SUBLANES, D_MODEL), axis=0)

        d_all = jnp.concatenate([da_ref[...], dq_ref[...], dk_ref[...], dv_ref[...], df_ref[...], dg_ref[...]], axis=1)
        dh = _dot(d_all, w_ref[...])
        xv = x_ref[...]
        r = lax.rsqrt(jnp.mean(xv * xv, axis=-1, keepdims=True) + NORM_EPS)
        xn = xv * r
        gn = g_ref[...]
        a_sh[...] += fold(dh)
        a_sc[...] += fold(dh * (xn * gn))
        dn1 = dh * sc_ref[...]
        a_g[...] += fold(dn1 * xn)
        dxn = dn1 * gn
        gx_ref[...] = dx2_ref[...] + r * (dxn - xn * jnp.mean(dxn * xn, axis=-1, keepdims=True))

        @pl.when(step == nt - 1)
        def _():
            sub = lax.broadcasted_iota(jnp.int32, (SUBLANES, D_MODEL), 0)
            v_sh = jnp.sum(a_sh[...], axis=0, keepdims=True)
            v_sc = jnp.sum(a_sc[...], axis=0, keepdims=True)
            v_g = jnp.sum(a_g[...], axis=0, keepdims=True)
            vec_ref[...] = jnp.where(sub == 0, v_sh, jnp.where(sub == 1, v_sc, jnp.where(sub == 2, v_g, 0.0)))

    row = lambda w: pl.BlockSpec((tm, w), lambda i: (i, 0))
    return pl.pallas_call(
        body, name="dh_norm_bwd", grid=(nt,),
        out_shape=[jax.ShapeDtypeStruct((s, D_MODEL), F32), jax.ShapeDtypeStruct((SUBLANES, D_MODEL), F32)],
        in_specs=[row(W_A), row(512), row(512), row(512), row(W_F), row(W_G), _const_spec((W_INT, D_MODEL)),
                  row(D_MODEL), row(D_MODEL), _const_spec((1, D_MODEL)), _const_spec((1, D_MODEL))],
        out_specs=[row(D_MODEL), pl.BlockSpec((SUBLANES, D_MODEL), lambda i: (0, 0))],
        scratch_shapes=[pltpu.VMEM((SUBLANES, D_MODEL), F32)] * 3,
        compiler_params=_params(("arbitrary",), VMEM_LIMIT),
    )(d_a, d_q, d_k, d_v, d_f, d_g, w_t, x, dx2, gnorm, scale1)


def _dw_in(h_t, d, name, tn, ts=1024):
    s, n = d.shape
    ns = s // ts

    def body(h_ref, d_ref, o_ref, acc):
        k = pl.program_id(1)

        @pl.when(k == 0)
        def _():
            acc[...] = jnp.zeros_like(acc)

        acc[...] += _dot(h_ref[...], d_ref[...])

        @pl.when(k == ns - 1)
        def _():
            o_ref[...] = acc[...]

    return pl.pallas_call(
        body, name=name, grid=(n // tn, ns),
        out_shape=jax.ShapeDtypeStruct((D_MODEL, n), F32),
        in_specs=[pl.BlockSpec((D_MODEL, ts), lambda jn, k: (0, k)), pl.BlockSpec((ts, tn), lambda jn, k: (k, jn))],
        out_specs=pl.BlockSpec((D_MODEL, tn), lambda jn, k: (0, jn)),
        scratch_shapes=[pltpu.VMEM((D_MODEL, tn), F32)],
        compiler_params=_params(("parallel", "arbitrary"), VMEM_LIMIT),
    )(h_t, d)


def _small_grads(packs, c_t, dada_shard):
    def body(p_ref, ct_ref, da_ref, sum_ref, gw_ref):
        acc = p_ref[0]
        for dev in range(1, 8):
            acc = acc + p_ref[dev]
        sum_ref[...] = acc
        gw_ref[...] = jnp.dot(ct_ref[...], da_ref[...], preferred_element_type=F32, precision=lax.Precision.HIGHEST)

    return pl.pallas_call(
        body, name="small_grads",
        out_shape=[jax.ShapeDtypeStruct(packs.shape[1:], F32),
                   jax.ShapeDtypeStruct((c_t.shape[0], dada_shard.shape[1]), F32)],
    )(packs, c_t, dada_shard)


def _adamw_body(w_ref, g_ref, m_ref, v_ref, d_ref, mo_ref, vo_ref):
    c1 = 1.0 / (1.0 - ADAM_B1 ** ADAM_STEP)
    c2 = 1.0 / (1.0 - ADAM_B2 ** ADAM_STEP)
    gv = g_ref[...]
    mn = ADAM_B1 * m_ref[...] + (1.0 - ADAM_B1) * gv
    vn = ADAM_B2 * v_ref[...] + (1.0 - ADAM_B2) * (gv * gv)
    mo_ref[...] = mn
    vo_ref[...] = vn
    d_ref[...] = -ADAM_LR * ((mn * c1) / (jnp.sqrt(vn * c2) + ADAM_EPS) + ADAM_WD * w_ref[...])


def _adamw3(w, g, m, v, name, tb=128):
    spec = pl.BlockSpec((tb, SUBLANES, LANES), lambda i: (i, 0, 0))
    return pl.pallas_call(
        functools.partial(_adamw_body), name=name, grid=(pl.cdiv(w.shape[0], tb),),
        out_shape=[jax.ShapeDtypeStruct(w.shape, F32)] * 3,
        in_specs=[spec] * 4, out_specs=[spec] * 3,
        compiler_params=_params(("parallel",)),
    )(w, g, m, v)


def _adamw(w, g, m, v, name):
    r, c = w.shape
    tr = 128 if r % 128 == 0 else r
    body = functools.partial(_adamw_body)
    spec = pl.BlockSpec((tr, c), lambda i: (i, 0))
    return pl.pallas_call(
        body, name=name, grid=(r // tr,),
        out_shape=[jax.ShapeDtypeStruct((r, c), F32)] * 3,
        in_specs=[spec] * 4, out_specs=[spec] * 3,
        compiler_params=_params(("parallel",)),
    )(w, g, m, v)


def _rope_tables(positions):
    inv_freq = 10000.0 ** (-jnp.arange(0, HEAD_DIM, 2, dtype=F32) / HEAD_DIM)
    ang = positions.astype(F32)[:, None] * inv_freq
    cos, sin = jnp.tile(jnp.cos(ang), (1, 4)), jnp.tile(jnp.sin(ang), (1, 4))
    lo = (jnp.arange(LANES) % HEAD_DIM) < (HEAD_DIM // 2)
    return cos, sin, jnp.where(lo, -sin, 0.0), jnp.where(lo, 0.0, sin)


def _pad_rows(v, rows=SUBLANES):
    return jnp.pad(v, ((0, rows - v.shape[0]), (0, 0)))


def kernel(x, c, positions, w_ada, b_ada, g_norm, w_in, b_f, sinks, w_o_swa, w_o_fox, w_out, g_final, loss_target, m_w_ada, m_b_ada, m_g_norm, m_w_in, m_b_f, m_sinks, m_w_o_swa, m_w_o_fox, m_w_out, m_g_final, v_w_ada, v_b_ada, v_g_norm, v_w_in, v_b_f, v_sinks, v_w_o_swa, v_w_o_fox, v_w_out, v_g_final):
    ix, iy, ic = lax.axis_index("x"), lax.axis_index("y"), lax.axis_index("c")
    chip = 2 * ix + iy
    dev = 2 * chip + ic
    xs, tgt = x[0], loss_target[0]
    s = xs.shape[0]

    b_ada_shard = lax.dynamic_slice(b_ada, (0, chip * 768), (1, 768))
    ada_parts, g_in, g_oa, g_ob, g_out = _gather_inputs(
        _pad_rows(c), w_ada[0], b_ada_shard, [w_in[0], w_o_swa[0], w_o_fox[0], w_out[0]], "gather_inputs")
    ada = lax.dynamic_index_in_dim(ada_parts, dev, axis=1, keepdims=False).reshape(1, 3 * D_MODEL)
    shift, scale, gate = ada[:, :D_MODEL], ada[:, D_MODEL:2 * D_MODEL], ada[:, 2 * D_MODEL:]
    scale1 = 1.0 + scale

    w_ref_order = jnp.transpose(g_in, (1, 0, 2)).reshape(D_MODEL, R_END)
    w_int = jnp.concatenate([
        w_ref_order[:, :R_ZA], w_ref_order[:, R_QB:R_FB], w_ref_order[:, R_FB:R_ZB],
        jnp.zeros((D_MODEL, W_F - N_HEADS), BF16), w_ref_order[:, R_ZA:R_QB], w_ref_order[:, R_ZB:]], axis=1)
    w_int_t = w_int.T
    wo_a = jnp.transpose(g_oa, (1, 0, 2)).reshape(512, D_MODEL)
    wo_b = jnp.transpose(g_ob, (1, 0, 2)).reshape(512, D_MODEL)
    w_o = g_out.reshape(D_MODEL, D_MODEL)

    cos, sin, sin_lo, sin_hi = _rope_tables(positions[0])
    bf_pad = jnp.pad(b_f, ((0, 0), (0, LANES - N_HEADS)))
    sink_vec = sinks[0]

    a, b, f, g, h_t = _norm_proj(xs, g_norm * scale1, shift, w_int, cos, sin_lo, sin_hi)
    att_a, l_swa = _swa_fwd(a, sink_vec)
    cum = _fox_cumsum(f, bf_pad)
    qa, ka, va, stats = _fox_prep(b, cum, FOX_TILE)
    ranges = _fox_tile_ranges(stats)
    att_b, lse = _fox_fwd(qa, ka, va, ranges, FOX_TILE)

    dx2, datt_a, datt_b, d_g, delta8, dwo_a, dwo_b, dw_out, vec_mid = _mid(
        att_a, att_b, g, xs, tgt, gate, g_final.reshape(1, D_MODEL), wo_a, wo_b, w_o)
    delta = jnp.pad(delta8.reshape(4, 2, s), ((0, 0), (0, SUBLANES - 2), (0, 0)))
    d_a, dsink = _swa_bwd(a, datt_a, l_swa, sink_vec, cos, sin)
    dq, dk, dv, dcum_k, dcum_q = _fox_bwd(qa, ka, b, datt_b, lse, delta, ranges, FOX_TILE)
    d_f, dbf = _fox_cumsum_bwd(dcum_k, dcum_q, f, bf_pad)
    grad_x, vec_dh = _dh_norm_bwd(d_a, dq, dk, dv, d_f, d_g, w_int_t, xs, dx2, g_norm, scale1)
    dw_a = _dw_in(h_t, d_a, "dw_in_a", 768)
    dw_q = _dw_in(h_t, dq, "dw_in_q", 512)
    dw_k = _dw_in(h_t, dk, "dw_in_k", 512)
    dw_v = _dw_in(h_t, dv, "dw_in_v", 512)
    dw_f = _dw_in(h_t, d_f, "dw_in_f", 128)
    dw_g = _dw_in(h_t, d_g, "dw_in_g", 1024)
    dw_in = jnp.concatenate([dw_a, dw_g[:, :512], dw_q, dw_k, dw_v, dw_f[:, :N_HEADS], dw_g[:, 512:]], axis=1)

    tail = jnp.pad(jnp.concatenate([dbf[0:1, :N_HEADS], dsink[0:1, :N_HEADS]], axis=1), ((0, 0), (0, D_MODEL - 2 * N_HEADS)))
    pack = jnp.concatenate([c, vec_dh[0:2], vec_mid[1:2], vec_dh[2:3], vec_mid[0:1], tail, vec_mid[2:3]], axis=0)
    packs = _allgather_small(pack, "gather_small")
    dada_all = packs[:, 1:4, :].reshape(8, 3 * D_MODEL)
    dada_shard = lax.dynamic_slice(dada_all, (0, chip * 768), (8, 768))
    sums, g_w_ada = _small_grads(packs, packs[:, 0, :].T, dada_shard)
    g_b_ada = sums[1:4].reshape(1, 3 * D_MODEL)
    g_g_norm = sums[4:5]
    g_g_final = sums[5]
    g_b_f = sums[6:7, :N_HEADS]
    g_sinks = sums[6:7, N_HEADS:2 * N_HEADS]
    loss = sums[7, 0]

    def slots(w, axis):
        if axis == 1:
            return jnp.transpose(w.reshape(w.shape[0], 4, w.shape[1] // 4), (1, 0, 2))
        return w.reshape(4, w.shape[0] // 4, w.shape[1])

    (g_w_in,) = _reduce_scatter([slots(dw_in, 1)], "reduce_w_in")
    g_wo_a, g_wo_b, g_w_out = _reduce_scatter([slots(dwo_a, 1), slots(dwo_b, 1), slots(dw_out, 0)], "reduce_w_small")

    grads = {
        "w_ada": g_w_ada, "b_ada": g_b_ada, "g_norm": g_g_norm, "w_in": g_w_in, "b_f": g_b_f, "sinks": g_sinks,
        "w_o_swa": g_wo_a, "w_o_fox": g_wo_b, "w_out": g_w_out, "g_final": g_g_final,
    }
    params = {
        "w_ada": (w_ada, m_w_ada, v_w_ada), "b_ada": (b_ada, m_b_ada, v_b_ada), "g_norm": (g_norm, m_g_norm, v_g_norm),
        "w_in": (w_in, m_w_in, v_w_in), "b_f": (b_f, m_b_f, v_b_f), "sinks": (sinks, m_sinks, v_sinks),
        "w_o_swa": (w_o_swa, m_w_o_swa, v_w_o_swa), "w_o_fox": (w_o_fox, m_w_o_fox, v_w_o_fox),
        "w_out": (w_out, m_w_out, v_w_out), "g_final": (g_final, m_g_final, v_g_final),
    }
    n_col = w_in.shape[2]

    def as_stored(t):
        return jnp.transpose(t, (2, 0, 1)).reshape(n_col, SUBLANES, LANES)

    def from_stored(t):
        return jnp.transpose(t, (1, 2, 0)).reshape(1, D_MODEL, n_col)

    names = list(grads)
    out_g, out_d, out_m, out_v = [], [], [], []
    for nm in names:
        w, m, v = params[nm]
        if nm == "w_in":
            g_st = as_stored(grads[nm][None])
            d_, m_, v_ = _adamw3(as_stored(w), g_st, as_stored(m), as_stored(v), "adamw_" + nm)
            res = [from_stored(t) for t in (g_st, d_, m_, v_)]
        else:
            shape2 = (w.shape[-2], w.shape[-1]) if w.ndim >= 2 else (1, w.shape[0])
            d_, m_, v_ = _adamw(w.reshape(shape2), grads[nm].reshape(shape2), m.reshape(shape2), v.reshape(shape2), "adamw_" + nm)
            res = [t.reshape(w.shape) for t in (grads[nm], d_, m_, v_)]
        out_g.append(res[0])
        out_d.append(res[1])
        out_m.append(res[2])
        out_v.append(res[3])
    return (loss, grad_x[None], *out_g, *out_d, *out_m, *out_v)
```

```python
import functools

import numpy as np
import jax
import jax.numpy as jnp
from jax import lax
from jax.experimental import pallas as pl
from jax.experimental.pallas import tpu as pltpu

F32 = jnp.float32
BF16 = jnp.bfloat16
MESH = pl.DeviceIdType.MESH

D_MODEL = 1024
HEAD_DIM = 64
N_HEADS = 8
WINDOW = 128
NORM_EPS = 1e-6
SCALE = HEAD_DIM ** -0.5
NEG = -1e30
LANES = 128
SUBLANES = 8
VMEM_LIMIT = 60 * 1024 * 1024
FOX_TILE = 512

W_A, W_B, W_F, W_G = 768, 1536, 128, 3072
OFF_A, OFF_B, OFF_F, OFF_G = 0, 768, 2304, 2432
W_INT = W_A + W_B + W_F + W_G
R_ZA, R_QB, R_FB, R_ZB, R_END = 768, 1280, 2816, 2824, 5384

ADAM_LR, ADAM_B1, ADAM_B2, ADAM_EPS, ADAM_WD, ADAM_STEP = 0.001, 0.9, 0.999, 1e-08, 0.01, 10

NT = (((1,), (1,)), ((), ()))
TN = (((0,), (0,)), ((), ()))


def _dot(a, b, dims=None):
    if dims is None:
        return jnp.dot(a, b, preferred_element_type=F32)
    return lax.dot_general(a, b, dims, preferred_element_type=F32)


def _split3(v):
    hi = v.astype(BF16)
    r1 = v - hi.astype(F32)
    mid = r1.astype(BF16)
    lo = (r1 - mid.astype(F32)).astype(BF16)
    return hi, mid, lo


def _sigmoid(v):
    return 1.0 / (1.0 + jnp.exp(-v))


def _params(sem=None, vmem=None):
    return pltpu.CompilerParams(dimension_semantics=sem, vmem_limit_bytes=vmem)


def _const_spec(shape):
    nd = len(shape)
    return pl.BlockSpec(shape, lambda *_: (0,) * nd, pipeline_mode=pl.Buffered(1))


def _flip(v, f):
    return 1 - v if f else v


_CHIP_FLIPS = ((1, 0), (0, 1), (1, 1))


def _gather_inputs(c_pad, w_ada, b_ada_shard, shards, name):
    n = len(shards)
    n_col = w_ada.shape[1]

    def body(*refs):
        c_ref, wa_ref, ba_ref = refs[:3]
        ins = refs[3:3 + n]
        ada_ref = refs[3 + n]
        outs = refs[4 + n:4 + 2 * n]
        call_ref, send_sems, recv_sems = refs[4 + 2 * n:7 + 2 * n]
        x, y, c = lax.axis_index("x"), lax.axis_index("y"), lax.axis_index("c")
        k_me = 2 * x + y
        me = 2 * k_me + c
        sibling = (x, y, 1 - c)
        chips = [(_flip(x, fx), _flip(y, fy)) for fx, fy in _CHIP_FLIPS]

        def piece(i, chip_k, half):
            hr = ins[i].shape[0] // 2
            return outs[i].at[chip_k, pl.ds(half * hr, hr), :]

        def copy(i, slot, chip_k, half, to):
            return pltpu.make_async_remote_copy(
                src_ref=piece(i, chip_k, half), dst_ref=piece(i, chip_k, half),
                send_sem=send_sems.at[6 * i + slot], recv_sem=recv_sems.at[6 * i + slot],
                device_id=to, device_id_type=MESH)

        def small(ref, slot, sem, to):
            return pltpu.make_async_remote_copy(
                src_ref=ref.at[slot], dst_ref=ref.at[slot], send_sem=send_sems.at[6 * n + sem],
                recv_sem=recv_sems.at[6 * n + sem], device_id=to, device_id_type=MESH)

        for i in range(n):
            outs[i][k_me] = ins[i][...].astype(BF16)
        started = []
        for i in range(n):
            for j, chip in enumerate(chips):
                cp = copy(i, j, k_me, c, (chip[0], chip[1], c))
                cp.start()
                started.append(cp)

        call_ref[me] = c_ref[...]
        peers = [(_flip(x, k & 4), _flip(y, k & 2), _flip(c, k & 1)) for k in range(1, 8)]
        for k, peer in enumerate(peers):
            cp = small(call_ref, me, k, peer)
            cp.start()
            started.append(cp)
        for k, peer in enumerate(peers):
            small(call_ref, 4 * peer[0] + 2 * peer[1] + peer[2], k, peer).wait_recv()
        c_all = call_ref[:, 0, :].astype(BF16)
        ada_ref[k_me] = _dot(c_all, wa_ref[...].astype(BF16)) + ba_ref[...]
        for j, chip in enumerate(chips):
            cp = small(ada_ref, k_me, 7 + j, (chip[0], chip[1], c))
            cp.start()
            started.append(cp)

        for j, chip in enumerate(chips):
            chip_k = 2 * chip[0] + chip[1]
            for i in range(n):
                copy(i, j, chip_k, c, (chip[0], chip[1], c)).wait_recv()
                cp = copy(i, 3 + j, chip_k, c, sibling)
                cp.start()
                started.append(cp)
        for j, chip in enumerate(chips):
            chip_k = 2 * chip[0] + chip[1]
            small(ada_ref, chip_k, 7 + j, (chip[0], chip[1], c)).wait_recv()
            for i in range(n):
                copy(i, 3 + j, chip_k, 1 - c, sibling).wait_recv()
        for cp in started:
            cp.wait_send()

    vmem = pl.BlockSpec(memory_space=pltpu.VMEM)
    return pl.pallas_call(
        body, name=name,
        out_shape=[jax.ShapeDtypeStruct((4, 8, n_col), F32)] + [jax.ShapeDtypeStruct((4,) + s.shape, BF16) for s in shards],
        in_specs=[vmem] * (3 + n),
        out_specs=[vmem] * (1 + n),
        scratch_shapes=[pltpu.VMEM((8,) + c_pad.shape, F32),
                        pltpu.SemaphoreType.DMA((6 * n + 10,)), pltpu.SemaphoreType.DMA((6 * n + 10,))],
        compiler_params=_params(vmem=VMEM_LIMIT),
    )(c_pad, w_ada, b_ada_shard, *shards)


def _reduce_scatter(pieces, pack, name):
    n = len(pieces)

    def body(*refs):
        pack_ref, ins = refs[0], refs[1:1 + n]
        packs_ref, outs = refs[1 + n], refs[2 + n:2 + 2 * n]
        rest = refs[2 + 2 * n:]
        own, got = rest[:n], rest[n:2 * n]
        sendb, recvb = rest[2 * n:3 * n], rest[3 * n:4 * n]
        send_sems, recv_sems, local_sems = rest[4 * n:4 * n + 3]
        x, y, c = lax.axis_index("x"), lax.axis_index("y"), lax.axis_index("c")
        k_me = 2 * x + y
        me = 2 * k_me + c
        sibling = (x, y, 1 - c)
        chips = [(_flip(x, fx), _flip(y, fy)) for fx, fy in _CHIP_FLIPS]
        hrs = [p.shape[1] // 2 for p in pieces]

        def remote(i, slot, src, dst, to):
            return pltpu.make_async_remote_copy(
                src_ref=src, dst_ref=dst, send_sem=send_sems.at[5 * i + slot], recv_sem=recv_sems.at[5 * i + slot],
                device_id=to, device_id_type=MESH)

        started = []
        packs_ref[me] = pack_ref[...]
        peers = [(_flip(x, k & 4), _flip(y, k & 2), _flip(c, k & 1)) for k in range(1, 8)]
        for k, peer in enumerate(peers):
            cp = pltpu.make_async_remote_copy(
                src_ref=pack_ref, dst_ref=packs_ref.at[me], send_sem=send_sems.at[5 * n + k],
                recv_sem=recv_sems.at[5 * n + k], device_id=peer, device_id_type=MESH)
            cp.start()
            started.append(cp)
        loads = []
        for i in range(n):
            ld = pltpu.make_async_copy(ins[i].at[:, pl.ds(c * hrs[i], hrs[i]), :], own[i], local_sems.at[i])
            ld.start()
            loads.append(ld)
            cp = remote(i, 0, ins[i].at[:, pl.ds((1 - c) * hrs[i], hrs[i]), :], got[i], sibling)
            cp.start()
            started.append(cp)
        for i in range(n):
            loads[i].wait()
            remote(i, 0, ins[i].at[:, pl.ds(c * hrs[i], hrs[i]), :], got[i], sibling).wait_recv()
            for j, chip in enumerate(chips):
                chip_k = 2 * chip[0] + chip[1]
                sendb[i][j] = (own[i][chip_k] + got[i][chip_k]).astype(BF16)
                cp = remote(i, 1 + j, sendb[i].at[j], recvb[i].at[j], (chip[0], chip[1], c))
                cp.start()
                started.append(cp)
        for i in range(n):
            acc = own[i][k_me] + got[i][k_me]
            for j, chip in enumerate(chips):
                remote(i, 1 + j, sendb[i].at[j], recvb[i].at[j], (chip[0], chip[1], c)).wait_recv()
                acc = acc + recvb[i][j].astype(F32)
            mine = outs[i].at[pl.ds(c * hrs[i], hrs[i]), :]
            outs[i][pl.ds(pl.multiple_of(c * hrs[i], SUBLANES), hrs[i]), :] = acc
            cp = remote(i, 4, mine, mine, sibling)
            cp.start()
            started.append(cp)
        for i in range(n):
            theirs = outs[i].at[pl.ds((1 - c) * hrs[i], hrs[i]), :]
            remote(i, 4, theirs, theirs, sibling).wait_recv()
        for k, peer in enumerate(peers):
            pltpu.make_async_remote_copy(
                src_ref=pack_ref, dst_ref=packs_ref.at[4 * peer[0] + 2 * peer[1] + peer[2]],
                send_sem=send_sems.at[5 * n + k], recv_sem=recv_sems.at[5 * n + k],
                device_id=peer, device_id_type=MESH).wait_recv()
        for cp in started:
            cp.wait_send()

    vmem = pl.BlockSpec(memory_space=pltpu.VMEM)
    scratch = []
    scratch += [pltpu.VMEM((4, p.shape[1] // 2, p.shape[2]), F32) for p in pieces]
    scratch += [pltpu.VMEM((4, p.shape[1] // 2, p.shape[2]), F32) for p in pieces]
    scratch += [pltpu.VMEM((3, p.shape[1] // 2, p.shape[2]), BF16) for p in pieces]
    scratch += [pltpu.VMEM((3, p.shape[1] // 2, p.shape[2]), BF16) for p in pieces]
    scratch += [pltpu.SemaphoreType.DMA((5 * n + 7,)), pltpu.SemaphoreType.DMA((5 * n + 7,)), pltpu.SemaphoreType.DMA((n,))]
    return pl.pallas_call(
        body, name=name,
        out_shape=[jax.ShapeDtypeStruct((8,) + pack.shape, F32)] + [jax.ShapeDtypeStruct(p.shape[1:], F32) for p in pieces],
        in_specs=[vmem] + [pl.BlockSpec(memory_space=pl.ANY)] * n,
        out_specs=[vmem] * (1 + n),
        scratch_shapes=scratch,
        compiler_params=_params(vmem=VMEM_LIMIT),
    )(pack, *pieces)


def _rope_fwd(t, cos, sin, lane):
    lo = (lane % HEAD_DIM) < (HEAD_DIM // 2)
    return t * cos + jnp.where(lo, -pltpu.roll(t, 96, 1), pltpu.roll(t, 32, 1)) * sin


def _norm_proj(x, gmod, shift, w_int, pos, freq, tm=512):
    s = x.shape[0]

    def body(x_ref, g_ref, sh_ref, w_ref, pos_ref, fr_ref, a_ref, b_ref, f_ref, gg_ref, ht_ref, cos_ref, sin_ref):
        xv = x_ref[...]
        r = lax.rsqrt(jnp.mean(xv * xv, axis=-1, keepdims=True) + NORM_EPS)
        hf = (xv * r) * g_ref[...] + sh_ref[...]
        hb = hf.astype(BF16)
        ht_ref[...] = hf.T.astype(BF16)
        pa = _dot(hb, w_ref[:, OFF_A:OFF_A + W_A])
        ang = pos_ref[...] * fr_ref[...]
        cosv, sinv = jnp.cos(ang), jnp.sin(ang)
        cos_ref[...] = cosv
        sin_ref[...] = sinv
        lane = lax.broadcasted_iota(jnp.int32, (tm, LANES), 1)
        for j in range(5):
            t = pa[:, LANES * j:LANES * (j + 1)]
            a_ref[:, LANES * j:LANES * (j + 1)] = _rope_fwd(t, cosv, sinv, lane).astype(BF16)
        a_ref[:, 640:768] = pa[:, 640:768].astype(BF16)
        b_ref[...] = _dot(hb, w_ref[:, OFF_B:OFF_B + W_B]).astype(BF16)
        f_ref[...] = _dot(hb, w_ref[:, OFF_F:OFF_F + W_F])
        gg_ref[...] = _dot(hb, w_ref[:, OFF_G:OFF_G + W_G]).astype(BF16)

    row = lambda w: pl.BlockSpec((tm, w), lambda i: (i, 0))
    return pl.pallas_call(
        body, name="norm_proj", grid=(s // tm,),
        out_shape=[jax.ShapeDtypeStruct((s, W_A), BF16), jax.ShapeDtypeStruct((s, W_B), BF16),
                   jax.ShapeDtypeStruct((s, W_F), F32), jax.ShapeDtypeStruct((s, W_G), BF16),
                   jax.ShapeDtypeStruct((D_MODEL, s), BF16),
                   jax.ShapeDtypeStruct((s, LANES), F32), jax.ShapeDtypeStruct((s, LANES), F32)],
        in_specs=[row(D_MODEL), _const_spec((1, D_MODEL)), _const_spec((1, D_MODEL)), _const_spec((D_MODEL, W_INT)),
                  row(LANES), _const_spec((1, LANES))],
        out_specs=[row(W_A), row(W_B), row(W_F), row(W_G), pl.BlockSpec((D_MODEL, tm), lambda i: (0, i)),
                   row(LANES), row(LANES)],
        compiler_params=_params(("parallel",), VMEM_LIMIT),
    )(x, gmod, shift, w_int, pos, freq)


def _log_sigmoid(u):
    return jnp.minimum(u, 0.0) - jnp.log(1.0 + jnp.exp(-jnp.abs(u)))


def _fox_cumsum(f, bf_pad, tb=512):
    s = f.shape[0]

    def body(f_ref, b_ref, cum_ref, carry):
        @pl.when(pl.program_id(0) == 0)
        def _():
            carry[...] = jnp.zeros_like(carry)

        lane = lax.broadcasted_iota(jnp.int32, (tb, LANES), 1)
        logf = jnp.where(lane < N_HEADS, _log_sigmoid(f_ref[...] + b_ref[...]), 0.0)
        hi, mid, lo = _split3(logf)
        rows = lax.broadcasted_iota(jnp.int32, (tb, tb), 0)
        cols = lax.broadcasted_iota(jnp.int32, (tb, tb), 1)
        tril = (cols <= rows).astype(BF16)
        cum = _dot(tril, hi) + _dot(tril, mid) + _dot(tril, lo) + carry[0:1, :]
        cum_ref[...] = cum
        carry[...] = jnp.broadcast_to(cum[tb - 1:tb, :], carry.shape)

    return pl.pallas_call(
        body, name="fox_cumsum", grid=(s // tb,),
        out_shape=jax.ShapeDtypeStruct((s, LANES), F32),
        in_specs=[pl.BlockSpec((tb, LANES), lambda i: (i, 0)), _const_spec((1, LANES))],
        out_specs=pl.BlockSpec((tb, LANES), lambda i: (i, 0)),
        scratch_shapes=[pltpu.VMEM((SUBLANES, LANES), F32)],
        compiler_params=_params(("arbitrary",)),
    )(f, bf_pad)


def _fox_prep(b, cum, t):
    s = b.shape[0]

    def body(b_ref, cum_ref, q_ref, k_ref, v_ref, st_ref):
        lane = lax.broadcasted_iota(jnp.int32, (t, LANES), 1)
        sub8 = lax.broadcasted_iota(jnp.int32, (SUBLANES, LANES), 0)
        lane8 = lax.broadcasted_iota(jnp.int32, (SUBLANES, LANES), 1)
        cumv = cum_ref[...]
        stats = jnp.zeros((SUBLANES, LANES), F32)
        for h in range(N_HEADS):
            p, odd = h // 2, h % 2
            ch = jnp.broadcast_to(cumv[:, h:h + 1], (t, LANES))
            hi, mid, lo = (x.astype(F32) for x in _split3(ch))
            qp = b_ref[:, LANES * p:LANES * (p + 1)].astype(F32)
            kp = b_ref[:, 512 + LANES * p:512 + LANES * (p + 1)].astype(F32)
            vp = b_ref[:, 1024 + LANES * p:1024 + LANES * (p + 1)].astype(F32)
            if odd:
                qp, kp, vp = pltpu.roll(qp, 64, 1), pltpu.roll(kp, 64, 1), pltpu.roll(vp, 64, 1)
            qs = jnp.where(lane < 64, qp * SCALE, 0.0)
            ks = jnp.where(lane < 64, kp, 0.0)
            qa = jnp.where(lane < 64, qs,
                           jnp.where(lane == 64, hi, jnp.where(lane == 65, mid, jnp.where(lane == 66, lo,
                           jnp.where(lane < 70, 1.0, 0.0)))))
            ka = jnp.where(lane < 64, ks,
                           jnp.where(lane < 67, 1.0, jnp.where(lane == 67, -hi, jnp.where(lane == 68, -mid,
                           jnp.where(lane == 69, -lo, 0.0)))))
            q_ref[:, LANES * h:LANES * (h + 1)] = qa.astype(BF16)
            k_ref[:, LANES * h:LANES * (h + 1)] = ka.astype(BF16)
            v_ref[:, LANES * h:LANES * (h + 1)] = jnp.where(lane < 64, vp, jnp.where(lane == 64, 1.0, 0.0)).astype(BF16)
            qn = jnp.sqrt(jnp.max(jnp.sum(qs * qs, axis=-1, keepdims=True)))
            kn = jnp.sqrt(jnp.max(jnp.sum(ks * ks, axis=-1, keepdims=True)))
            dmin = jnp.min(jnp.sum(qs * ks, axis=-1, keepdims=True))
            csum = hi + mid + lo
            row = jnp.where(lane8 == 0, qn, jnp.where(lane8 == 1, kn, jnp.where(
                lane8 == 2, csum[0:1, :], jnp.where(lane8 == 3, csum[t - 1:t, :], jnp.where(lane8 == 4, dmin, 0.0)))))
            stats = jnp.where(sub8 == h, row, stats)
        st_ref[0] = stats

    wide = pl.BlockSpec((t, 1024), lambda i: (i, 0))
    return pl.pallas_call(
        body, name="fox_prep", grid=(s // t,),
        out_shape=[jax.ShapeDtypeStruct((s, 1024), BF16)] * 3 + [jax.ShapeDtypeStruct((s // t, SUBLANES, LANES), F32)],
        in_specs=[pl.BlockSpec((t, W_B), lambda i: (i, 0)), pl.BlockSpec((t, LANES), lambda i: (i, 0))],
        out_specs=[wide, wide, wide, pl.BlockSpec((1, SUBLANES, LANES), lambda i: (i, 0, 0))],
        compiler_params=_params(("parallel",)),
    )(b, cum)


PRUNE_MARGIN = 90.0


def _fox_tile_ranges(stats):
    nt = stats.shape[0]
    qk_max = 1.01 * jnp.max(stats[:, :, 0], axis=0) * jnp.max(stats[:, :, 1], axis=0)
    ball = qk_max - jnp.minimum(jnp.min(stats[:, :, 4], axis=0), 0.0) + 0.05
    d = ball[None, None, :] + stats[:, None, :, 2] - stats[None, :, :, 3]
    idx = jnp.arange(nt)
    skip = (d <= -PRUNE_MARGIN) & (idx[None, :, None] < idx[:, None, None])
    skip = skip.reshape(nt, nt, 4, 2).all(axis=-1)
    first_key = jnp.sum(skip, axis=1).astype(F32)
    needed = (~skip) & (idx[None, :, None] <= idx[:, None, None])
    n_query = jnp.sum(needed, axis=0).astype(F32)
    table = jnp.zeros((4, SUBLANES, LANES), F32)
    table = table.at[:, 0, :nt].set(first_key.T)
    table = table.at[:, 1, :nt].set(n_query.T)
    return table


def _lane_scalar(block, row, lane_idx):
    sub8 = lax.broadcasted_iota(jnp.int32, (SUBLANES, LANES), 0)
    lane8 = lax.broadcasted_iota(jnp.int32, (SUBLANES, LANES), 1)
    return jnp.sum(jnp.where((sub8 == row) & (lane8 == lane_idx), block, 0.0)).astype(jnp.int32)


def _fox_fwd(qa, ka, va, ranges, t):
    s = qa.shape[0]
    nt = s // t
    nc = t // LANES

    def body(rg_ref, q_ref, k_ref, v_ref, o_ref, lse_ref):
        i = pl.program_id(1)
        lane = lax.broadcasted_iota(jnp.int32, (t, LANES), 1)
        rows = lax.broadcasted_iota(jnp.int32, (t, t), 0)
        cols = lax.broadcasted_iota(jnp.int32, (t, t), 1)
        first = jnp.clip(_lane_scalar(rg_ref[0], 0, i), 0, i)

        heads = range(2)

        def scores(j):
            kt = k_ref[pl.ds(pl.multiple_of(j * t, t), t), :]
            return tuple(_dot(q_ref[:, LANES * hh:LANES * (hh + 1)], kt[:, LANES * hh:LANES * (hh + 1)], NT) for hh in heads)

        def update(j, scs, carry):
            vt = v_ref[pl.ds(pl.multiple_of(j * t, t), t), :]
            m_new = []
            for hh in heads:
                part = scs[hh][:, 0:LANES]
                for cch in range(1, nc):
                    part = jnp.maximum(part, scs[hh][:, LANES * cch:LANES * (cch + 1)])
                m_new.append(jnp.maximum(carry[2 * hh], jnp.max(part, axis=-1, keepdims=True)))
            alphas = [jnp.exp(carry[2 * hh] - m_new[hh]) for hh in heads]
            ps = [jnp.exp(scs[hh] - m_new[hh]).astype(BF16) for hh in heads]
            pvs = [_dot(ps[hh], vt[:, LANES * hh:LANES * (hh + 1)]) for hh in heads]
            return (m_new[0], alphas[0] * carry[1] + pvs[0], m_new[1], alphas[1] * carry[3] + pvs[1])

        col0 = jnp.full((t, 1), NEG, F32)
        zero = jnp.zeros((t, LANES), F32)
        carry = lax.fori_loop(first, i, lambda j, cr: update(j, scores(j), cr), (col0, zero, col0, zero))
        m0, acc0, m1, acc1 = update(i, tuple(jnp.where(cols <= rows, sc, NEG) for sc in scores(i)), carry)
        l0, l1 = acc0[:, 64:65], acc1[:, 64:65]
        o_ref[...] = jnp.where(lane < 64, acc0 * (1.0 / l0), pltpu.roll(acc1 * (1.0 / l1), 64, 1)).astype(BF16)
        sub = lax.broadcasted_iota(jnp.int32, (SUBLANES, t), 0)
        lse0 = jnp.broadcast_to(m0 + jnp.log(l0), (t, LANES)).T[0:SUBLANES, :]
        lse1 = jnp.broadcast_to(m1 + jnp.log(l1), (t, LANES)).T[0:SUBLANES, :]
        lse_ref[0] = jnp.where(sub == 0, lse0, jnp.where(sub == 1, lse1, 0.0))

    pair = pl.BlockSpec((s, 2 * LANES), lambda p, i: (0, p))
    return pl.pallas_call(
        body, name="fox_fwd", grid=(4, nt),
        out_shape=[jax.ShapeDtypeStruct((s, 512), BF16), jax.ShapeDtypeStruct((4, SUBLANES, s), F32)],
        in_specs=[pl.BlockSpec((1, SUBLANES, LANES), lambda p, i: (p, 0, 0)),
                  pl.BlockSpec((t, 2 * LANES), lambda p, i: (i, p)), pair, pair],
        out_specs=[pl.BlockSpec((t, LANES), lambda p, i: (i, p)),
                   pl.BlockSpec((1, SUBLANES, t), lambda p, i: (p, 0, i))],
        compiler_params=_params(("parallel", "arbitrary"), VMEM_LIMIT),
    )(ranges, qa, ka, va)


def _dup_halves(blk, lane):
    f = blk.astype(F32)
    r = pltpu.roll(f, 64, 1)
    return jnp.where(lane < 64, f, r).astype(BF16), jnp.where(lane >= 64, f, r).astype(BF16)


GROUP = 4
GROUP_ROWS = GROUP * WINDOW


def _stack_heads(ref, g, lane):
    parts = []
    for pb in (2 * g, 2 * g + 1):
        blk = ref[:, LANES * pb:LANES * (pb + 1)]
        zero = jnp.zeros_like(blk)
        parts += [jnp.where(lane < 64, blk, zero), jnp.where(lane >= 64, blk, zero)]
    return jnp.concatenate(parts, axis=0)


def _swa_band(a_ref, ap_ref, g, lane):
    k = jnp.concatenate([_dup_halves(ap_ref[:, 512:640], lane)[g], _dup_halves(a_ref[:, 512:640], lane)[g]], axis=0)
    v = jnp.concatenate([_dup_halves(ap_ref[:, 640:768], lane)[g], _dup_halves(a_ref[:, 640:768], lane)[g]], axis=0)
    return k, v


def _swa_logits(q, k, has_prev):
    sc = _dot(q, k, NT) * SCALE
    rr = lax.broadcasted_iota(jnp.int32, sc.shape, 0) % WINDOW
    cc = lax.broadcasted_iota(jnp.int32, sc.shape, 1)
    valid = (cc > rr) & (cc <= rr + WINDOW) & (has_prev | (cc >= WINDOW))
    return jnp.where(valid, sc, NEG)


def _per_head_column(values):
    return jnp.concatenate([jnp.broadcast_to(v, (WINDOW, 1)) for v in values], axis=0)


SWA_BLOCKS = 4
SWA_ROWS = SWA_BLOCKS * WINDOW


def _swa_blocks(a_ref, ap_ref):
    return [ap_ref] + [a_ref.at[pl.ds(WINDOW * jb, WINDOW), :] for jb in range(SWA_BLOCKS)]


def _swa_fwd(a, sinks):
    s = a.shape[0]

    def body(sink_ref, a_ref, ap_ref, o_ref, l_ref):
        lane = lax.broadcasted_iota(jnp.int32, (WINDOW, LANES), 1)
        blocks = _swa_blocks(a_ref, ap_ref)
        units = [(jb, g) for jb in range(SWA_BLOCKS) for g in range(2)]
        sinks_col = [_per_head_column([sink_ref[GROUP * g + hh] for hh in range(GROUP)]) for g in range(2)]
        bands = [_swa_band(blocks[jb + 1], blocks[jb], g, lane) for jb, g in units]
        scs = [_swa_logits(_stack_heads(blocks[jb + 1], g, lane), bands[u][0],
                           (pl.program_id(0) > 0) if jb == 0 else True) for u, (jb, g) in enumerate(units)]
        ms = [jnp.maximum(jnp.max(scs[u], axis=-1, keepdims=True), sinks_col[g]) for u, (jb, g) in enumerate(units)]
        ps = [jnp.exp(scs[u] - ms[u]) for u in range(len(units))]
        dens = [jnp.sum(ps[u], axis=-1, keepdims=True) + jnp.exp(sinks_col[g] - ms[u]) for u, (jb, g) in enumerate(units)]
        outs = [_dot((ps[u] * (1.0 / dens[u])).astype(BF16), bands[u][1]) for u in range(len(units))]
        for jb in range(SWA_BLOCKS):
            rows = slice(WINDOW * jb, WINDOW * (jb + 1))
            l_all = jnp.zeros((WINDOW, LANES), F32)
            for g in range(2):
                u = 2 * jb + g
                lcol = ms[u] + jnp.log(dens[u])
                for pb in range(2):
                    r0 = 2 * pb * WINDOW
                    o_ref[rows, LANES * (2 * g + pb):LANES * (2 * g + pb + 1)] = jnp.where(
                        lane < 64, outs[u][r0:r0 + WINDOW], outs[u][r0 + WINDOW:r0 + 2 * WINDOW]).astype(BF16)
                for hh in range(GROUP):
                    l_all = jnp.where(lane == GROUP * g + hh, lcol[WINDOW * hh:WINDOW * (hh + 1)], l_all)
            l_ref[rows, :] = l_all

    return pl.pallas_call(
        body, name="swa_fwd", grid=(s // SWA_ROWS,),
        out_shape=[jax.ShapeDtypeStruct((s, 512), BF16), jax.ShapeDtypeStruct((s, LANES), F32)],
        in_specs=[pl.BlockSpec(memory_space=pltpu.SMEM),
                  pl.BlockSpec((SWA_ROWS, W_A), lambda i: (i, 0)),
                  pl.BlockSpec((WINDOW, W_A), lambda i: (jnp.maximum(SWA_BLOCKS * i - 1, 0), 0))],
        out_specs=[pl.BlockSpec((SWA_ROWS, 512), lambda i: (i, 0)), pl.BlockSpec((SWA_ROWS, LANES), lambda i: (i, 0))],
        compiler_params=_params(("parallel",)),
    )(sinks, a, a)


def _mid(att_a, att_b, g, x, target, gate, g_final, wo_a, wo_b, w_out, tm=256):
    s = x.shape[0]
    nt = s // tm

    def body(aa_ref, ab_ref, g_ref, x_ref, t_ref, gate_ref, gf_ref, woa_ref, wob_ref, wout_ref,
             dx_ref, daa_ref, dab_ref, dg_ref, delta_ref, dwoa_ref, dwob_ref, dwout_ref, vec_ref,
             acc_gf, acc_gate, acc_loss):
        step = pl.program_id(0)

        @pl.when(step == 0)
        def _():
            dwoa_ref[...] = jnp.zeros_like(dwoa_ref)
            dwob_ref[...] = jnp.zeros_like(dwob_ref)
            dwout_ref[...] = jnp.zeros_like(dwout_ref)
            acc_gf[...] = jnp.zeros_like(acc_gf)
            acc_gate[...] = jnp.zeros_like(acc_gate)
            acc_loss[...] = jnp.zeros_like(acc_loss)

        def fold(v):
            return jnp.sum(v.reshape(tm // SUBLANES, SUBLANES, D_MODEL), axis=0)

        gate = gate_ref[...]
        gfin = gf_ref[...]
        branches = []
        for att_ref, z_off, wo_ref in ((aa_ref, 0, woa_ref), (ab_ref, 512, wob_ref)):
            att = att_ref[...].astype(F32)
            z = g_ref[:, z_off:z_off + 512].astype(F32)
            sz = _sigmoid(z)
            silu = z * sz
            u = (att * silu).astype(BF16)
            branches.append((att, z, sz, silu, u, _dot(u, wo_ref[...])))
        ga = g_ref[:, 1024:2048].astype(F32)
        gb = g_ref[:, 2048:3072].astype(F32)
        sga, sgb = _sigmoid(ga), _sigmoid(gb)
        y_a, y_b = branches[0][5], branches[1][5]
        mb = (sga * y_a + sgb * y_b).astype(BF16)
        o = _dot(mb, wout_ref[...])
        x2 = x_ref[...] + gate * o
        r2 = lax.rsqrt(jnp.mean(x2 * x2, axis=-1, keepdims=True) + NORM_EPS)
        xn2 = x2 * r2
        err = xn2 * gfin - t_ref[...]
        acc_loss[...] += fold(err * err)
        dy = err * (1.0 / D_MODEL)
        acc_gf[...] += fold(dy * xn2)
        dxn = dy * gfin
        dx2 = r2 * (dxn - xn2 * jnp.mean(dxn * xn2, axis=-1, keepdims=True))
        dx_ref[...] = dx2
        acc_gate[...] += fold(dx2 * o)
        d_o = (dx2 * gate).astype(BF16)
        dwout_ref[...] += _dot(mb, d_o, TN)
        dm = _dot(d_o, wout_ref[...], NT)
        dg_ref[:, 1024:2048] = (dm * y_a * sga * (1.0 - sga)).astype(BF16)
        dg_ref[:, 2048:3072] = (dm * y_b * sgb * (1.0 - sgb)).astype(BF16)
        for (att, z, sz, silu, u, _), sg, wo_ref, dwo_ref, datt_ref, z_off in (
                (branches[0], sga, woa_ref, dwoa_ref, daa_ref, 0), (branches[1], sgb, wob_ref, dwob_ref, dab_ref, 512)):
            dyb = (dm * sg).astype(BF16)
            dwo_ref[...] += _dot(u, dyb, TN)
            du = _dot(dyb, wo_ref[...], NT)
            datt = du * silu
            datt_ref[...] = datt.astype(BF16)
            dg_ref[:, z_off:z_off + 512] = (du * att * (sz * (1.0 + z * (1.0 - sz)))).astype(BF16)
            if z_off == 512:
                prod = datt * att
                hi = prod.astype(BF16)
                lo = (prod - hi.astype(F32)).astype(BF16)
                er = lax.broadcasted_iota(jnp.int32, (512, LANES), 0)
                ec = lax.broadcasted_iota(jnp.int32, (512, LANES), 1)
                e = (er // HEAD_DIM == ec).astype(BF16)
                delta = _dot(hi, e) + _dot(lo, e)
                delta_ref[...] = delta.T[0:SUBLANES, :]

        @pl.when(step == nt - 1)
        def _():
            sub = lax.broadcasted_iota(jnp.int32, (SUBLANES, D_MODEL), 0)
            dgf = jnp.sum(acc_gf[...], axis=0, keepdims=True)
            dgate = jnp.sum(acc_gate[...], axis=0, keepdims=True)
            loss = 0.5 * jnp.sum(acc_loss[...]) * (1.0 / D_MODEL)
            vec_ref[...] = jnp.where(sub == 0, dgf, jnp.where(sub == 1, dgate, jnp.where(sub == 2, loss, 0.0)))

    row = lambda w: pl.BlockSpec((tm, w), lambda i: (i, 0))
    return pl.pallas_call(
        body, name="mid", grid=(nt,),
        out_shape=[jax.ShapeDtypeStruct((s, D_MODEL), F32), jax.ShapeDtypeStruct((s, 512), BF16),
                   jax.ShapeDtypeStruct((s, 512), BF16), jax.ShapeDtypeStruct((s, W_G), BF16),
                   jax.ShapeDtypeStruct((SUBLANES, s), F32),
                   jax.ShapeDtypeStruct((512, D_MODEL), F32), jax.ShapeDtypeStruct((512, D_MODEL), F32),
                   jax.ShapeDtypeStruct((D_MODEL, D_MODEL), F32), jax.ShapeDtypeStruct((SUBLANES, D_MODEL), F32)],
        in_specs=[row(512), row(512), row(W_G), row(D_MODEL), row(D_MODEL),
                  _const_spec((1, D_MODEL)), _const_spec((1, D_MODEL)),
                  _const_spec((512, D_MODEL)), _const_spec((512, D_MODEL)), _const_spec((D_MODEL, D_MODEL))],
        out_specs=[row(D_MODEL), row(512), row(512), row(W_G),
                   pl.BlockSpec((SUBLANES, tm), lambda i: (0, i)),
                   pl.BlockSpec((512, D_MODEL), lambda i: (0, 0)), pl.BlockSpec((512, D_MODEL), lambda i: (0, 0)),
                   pl.BlockSpec((D_MODEL, D_MODEL), lambda i: (0, 0)), pl.BlockSpec((SUBLANES, D_MODEL), lambda i: (0, 0))],
        scratch_shapes=[pltpu.VMEM((SUBLANES, D_MODEL), F32)] * 3,
        compiler_params=_params(("arbitrary",), VMEM_LIMIT),
    )(att_a, att_b, g, x, target, gate, g_final, wo_a, wo_b, w_out)


def _rope_bwd(dt, cos, sin, lane):
    u = dt * sin
    lo = (lane % HEAD_DIM) < (HEAD_DIM // 2)
    return dt * cos + jnp.where(lo, pltpu.roll(u, 96, 1), -pltpu.roll(u, 32, 1))


def _swa_bwd(a, datt, l_all, sinks, cos, sin):
    s = a.shape[0]
    nt = s // SWA_ROWS

    def body(sink_ref, a_ref, ap_ref, do_ref, l_ref, cos_ref, sin_ref, da_ref, ds_ref, halo):
        step = pl.program_id(0)
        tile = nt - 1 - step

        @pl.when(step == 0)
        def _():
            halo[...] = jnp.zeros_like(halo)
            ds_ref[...] = jnp.zeros_like(ds_ref)

        lane = lax.broadcasted_iota(jnp.int32, (WINDOW, LANES), 1)
        sub8 = lax.broadcasted_iota(jnp.int32, (SUBLANES, LANES), 0)
        lane8 = lax.broadcasted_iota(jnp.int32, (SUBLANES, LANES), 1)
        blocks = _swa_blocks(a_ref, ap_ref)
        dsink = jnp.zeros((SUBLANES, LANES), F32)

        def join(pair, r0):
            x0, x1 = pair[0][r0:r0 + WINDOW], pair[1][r0:r0 + WINDOW]
            return jnp.where(lane < 64, x0 + pltpu.roll(x0, 64, 1), x1 + pltpu.roll(x1, 64, 1))

        units = [(jb, g) for jb in range(SWA_BLOCKS) for g in range(2)]
        n_u = len(units)
        sinks_col = [_per_head_column([sink_ref[GROUP * g + hh] for hh in range(GROUP)]) for g in range(2)]
        bands = [_swa_band(blocks[jb + 1], blocks[jb], g, lane) for jb, g in units]
        qs = [_stack_heads(blocks[jb + 1], g, lane) for jb, g in units]
        doms = [_stack_heads(do_ref.at[pl.ds(WINDOW * jb, WINDOW), :], g, lane) for jb, g in units]
        lcols = []
        for jb, g in units:
            lv = l_ref[WINDOW * jb:WINDOW * (jb + 1), :]
            lcols.append(_per_head_column([lv[:, GROUP * g + hh:GROUP * g + hh + 1] for hh in range(GROUP)]))
        ps = [jnp.exp(_swa_logits(qs[u], bands[u][0], (tile > 0) if jb == 0 else True) - lcols[u])
              for u, (jb, g) in enumerate(units)]
        dps = [_dot(doms[u], bands[u][1], NT) for u in range(n_u)]
        deltas = [jnp.sum(ps[u] * dps[u], axis=-1, keepdims=True) for u in range(n_u)]
        for u, (jb, g) in enumerate(units):
            sink_term = jnp.exp(sinks_col[g] - lcols[u]) * deltas[u]
            for hh in range(GROUP):
                tot = jnp.sum(sink_term[WINDOW * hh:WINDOW * (hh + 1)])
                dsink = dsink + jnp.where((sub8 == 0) & (lane8 == GROUP * g + hh), -tot, 0.0)
        dss = [(ps[u] * (dps[u] - deltas[u])).astype(BF16) for u in range(n_u)]
        dqs = [_dot(dss[u], bands[u][0]) * SCALE for u in range(n_u)]
        dks = [_dot(dss[u], qs[u], TN) * SCALE for u in range(n_u)]
        dvs = [_dot(ps[u].astype(BF16), doms[u], TN) for u in range(n_u)]

        carry_k, carry_v = halo[:, 0:LANES], halo[:, LANES:2 * LANES]
        for jb in reversed(range(SWA_BLOCKS)):
            rows = slice(WINDOW * jb, WINDOW * (jb + 1))
            cosv, sinv = cos_ref[rows, :], sin_ref[rows, :]
            for g in range(2):
                dq = dqs[2 * jb + g]
                for pb in range(2):
                    r0 = 2 * pb * WINDOW
                    dq_pair = jnp.where(lane < 64, dq[r0:r0 + WINDOW], dq[r0 + WINDOW:r0 + 2 * WINDOW])
                    da_ref[rows, LANES * (2 * g + pb):LANES * (2 * g + pb + 1)] = _rope_bwd(
                        dq_pair, cosv, sinv, lane).astype(BF16)
            dkb, dvb = dks[2 * jb:2 * jb + 2], dvs[2 * jb:2 * jb + 2]
            da_ref[rows, 512:640] = _rope_bwd(join(dkb, WINDOW) + carry_k, cosv, sinv, lane).astype(BF16)
            da_ref[rows, 640:768] = (join(dvb, WINDOW) + carry_v).astype(BF16)
            carry_k, carry_v = join(dkb, 0), join(dvb, 0)
        halo[:, 0:LANES] = carry_k
        halo[:, LANES:2 * LANES] = carry_v
        ds_ref[...] += dsink

    rev = lambda w: pl.BlockSpec((SWA_ROWS, w), lambda i: (nt - 1 - i, 0))
    return pl.pallas_call(
        body, name="swa_bwd", grid=(nt,),
        out_shape=[jax.ShapeDtypeStruct((s, W_A), BF16), jax.ShapeDtypeStruct((SUBLANES, LANES), F32)],
        in_specs=[pl.BlockSpec(memory_space=pltpu.SMEM), rev(W_A),
                  pl.BlockSpec((WINDOW, W_A), lambda i: (jnp.maximum(SWA_BLOCKS * (nt - 1 - i) - 1, 0), 0)),
                  rev(512), rev(LANES), rev(LANES), rev(LANES)],
        out_specs=[rev(W_A), pl.BlockSpec((SUBLANES, LANES), lambda i: (0, 0))],
        scratch_shapes=[pltpu.VMEM((WINDOW, 2 * LANES), F32)],
        compiler_params=_params(("arbitrary",)),
    )(sinks, a, a, datt, l_all, cos, sin)


def _fox_bwd(qa, ka, b, do, lse, delta, ranges, t):
    s = qa.shape[0]
    nt = s // t

    def body(rg_ref, q_ref, do_ref, lse_ref, dl_ref, k_ref, v_ref, dq_ref, dk_ref, dv_ref, dc_ref, dr_ref, dq_acc):
        p = pl.program_id(0)
        j = pl.program_id(1)
        n_query = jnp.clip(_lane_scalar(rg_ref[0], 1, j), 1, nt - j)

        @pl.when(j == 0)
        def _():
            dq_acc[...] = jnp.zeros_like(dq_acc)

        lane = lax.broadcasted_iota(jnp.int32, (t, LANES), 1)
        rows = lax.broadcasted_iota(jnp.int32, (t, t), 0)
        cols = lax.broadcasted_iota(jnp.int32, (t, t), 1)
        kt = k_ref[...]
        vt = v_ref[...]

        def tile(i, carry, diagonal):
            dk0, dk1, dv = carry
            off = pl.multiple_of(i * t, t)
            qt = q_ref[pl.ds(off, t), :]
            dot_ = do_ref[pl.ds(off, t), :]
            lse_t = lse_ref[0, :, pl.ds(off, t)]
            dl_t = dl_ref[0, :, pl.ds(off, t)]
            heads = range(2)
            qs = [qt[:, LANES * hh:LANES * (hh + 1)] for hh in heads]
            ks = [kt[:, LANES * hh:LANES * (hh + 1)] for hh in heads]
            doms = [jnp.where((lane < 64) if hh == 0 else (lane >= 64), dot_, jnp.zeros_like(dot_)) for hh in heads]
            sts = [_dot(ks[hh], qs[hh], NT) for hh in heads]
            dpts = [_dot(vt, doms[hh], NT) for hh in heads]
            if diagonal:
                sts = [jnp.where(cols >= rows, st, NEG) for st in sts]
            pts = [jnp.exp(sts[hh] - lse_t[hh:hh + 1, :]) for hh in heads]
            dsts = [(pts[hh] * (dpts[hh] - dl_t[hh:hh + 1, :])).astype(BF16) for hh in heads]
            dv = dv + _dot(pts[0].astype(BF16), doms[0]) + _dot(pts[1].astype(BF16), doms[1])
            dks = [_dot(dsts[hh], qs[hh]) for hh in heads]
            for hh in heads:
                dq_acc[hh, pl.ds(off, t), :] += _dot(dsts[hh], ks[hh], TN)
            return dk0 + dks[0], dk1 + dks[1], dv

        zero = jnp.zeros((t, LANES), F32)
        carry = tile(j, (zero, zero, zero), True)
        dk0, dk1, dv = lax.fori_loop(j + 1, j + n_query, lambda i, cr: tile(i, cr, False), carry)
        dk_ref[...] = jnp.where(lane < 64, dk0, pltpu.roll(dk1, 64, 1)).astype(BF16)
        dv_ref[...] = dv.astype(BF16)
        c0 = jnp.broadcast_to(dk0[:, 67:68], (t, LANES))
        c1 = jnp.broadcast_to(dk1[:, 67:68], (t, LANES))
        dc_ref[0] = jnp.where(lane == 2 * p, -c0, jnp.where(lane == 2 * p + 1, -c1, 0.0))

        @pl.when(j == nt - 1)
        def _():
            lane_s = lax.broadcasted_iota(jnp.int32, (s, LANES), 1)
            a0, a1 = dq_acc[0], dq_acc[1]
            dq_ref[...] = (jnp.where(lane_s < 64, a0, pltpu.roll(a1, 64, 1)) * SCALE).astype(BF16)
            r0 = jnp.broadcast_to(a0[:, 64:65], (s, LANES))
            r1 = jnp.broadcast_to(a1[:, 64:65], (s, LANES))
            dr_ref[0] = jnp.where(lane_s == 2 * p, r0, jnp.where(lane_s == 2 * p + 1, r1, 0.0))

    return pl.pallas_call(
        body, name="fox_bwd", grid=(4, nt),
        out_shape=[jax.ShapeDtypeStruct((s, 512), BF16), jax.ShapeDtypeStruct((s, 512), BF16),
                   jax.ShapeDtypeStruct((s, 512), BF16), jax.ShapeDtypeStruct((4, s, LANES), F32),
                   jax.ShapeDtypeStruct((4, s, LANES), F32)],
        in_specs=[pl.BlockSpec((1, SUBLANES, LANES), lambda p, j: (p, 0, 0)),
                  pl.BlockSpec((s, 2 * LANES), lambda p, j: (0, p)),
                  pl.BlockSpec((s, LANES), lambda p, j: (0, p)),
                  pl.BlockSpec((1, SUBLANES, s), lambda p, j: (p, 0, 0)),
                  pl.BlockSpec((1, SUBLANES, s), lambda p, j: (p, 0, 0)),
                  pl.BlockSpec((t, 2 * LANES), lambda p, j: (j, p)),
                  pl.BlockSpec((t, LANES), lambda p, j: (j, 8 + p))],
        out_specs=[pl.BlockSpec((s, LANES), lambda p, j: (0, p)),
                   pl.BlockSpec((t, LANES), lambda p, j: (j, p)),
                   pl.BlockSpec((t, LANES), lambda p, j: (j, p)),
                   pl.BlockSpec((1, t, LANES), lambda p, j: (p, j, 0)),
                   pl.BlockSpec((1, s, LANES), lambda p, j: (p, 0, 0))],
        scratch_shapes=[pltpu.VMEM((2, s, LANES), F32)],
        compiler_params=_params(("parallel", "arbitrary"), VMEM_LIMIT),
    )(ranges, qa, do, lse, delta, ka, b)


def _fox_cumsum_bwd(dcum_k, dcum_q, f, bf_pad, tb=512):
    s = f.shape[0]
    nb = s // tb

    def body(dc_ref, dr_ref, f_ref, b_ref, df_ref, db_ref, carry):
        step = pl.program_id(0)

        @pl.when(step == 0)
        def _():
            carry[...] = jnp.zeros_like(carry)
            db_ref[...] = jnp.zeros_like(db_ref)

        lane = lax.broadcasted_iota(jnp.int32, (tb, LANES), 1)
        dc = dc_ref[0] + dr_ref[0]
        for k in range(1, 4):
            dc = dc + (dc_ref[k] + dr_ref[k])
        hi, mid, lo = _split3(dc)
        rows = lax.broadcasted_iota(jnp.int32, (tb, tb), 0)
        cols = lax.broadcasted_iota(jnp.int32, (tb, tb), 1)
        triu = (cols >= rows).astype(BF16)
        dlogf = _dot(triu, hi) + _dot(triu, mid) + _dot(triu, lo) + carry[0:1, :]
        carry[...] = jnp.broadcast_to(dlogf[0:1, :], carry.shape)
        u = f_ref[...] + b_ref[...]
        dfb = jnp.where(lane < N_HEADS, dlogf * _sigmoid(-u), 0.0)
        df_ref[...] = dfb.astype(BF16)
        sub = lax.broadcasted_iota(jnp.int32, (SUBLANES, LANES), 0)
        db_ref[...] += jnp.where(sub == 0, jnp.sum(dfb, axis=0, keepdims=True), 0.0)

    return pl.pallas_call(
        body, name="fox_cumsum_bwd", grid=(nb,),
        out_shape=[jax.ShapeDtypeStruct((s, LANES), BF16), jax.ShapeDtypeStruct((SUBLANES, LANES), F32)],
        in_specs=[pl.BlockSpec((4, tb, LANES), lambda i: (0, nb - 1 - i, 0)),
                  pl.BlockSpec((4, tb, LANES), lambda i: (0, nb - 1 - i, 0)),
                  pl.BlockSpec((tb, LANES), lambda i: (nb - 1 - i, 0)), _const_spec((1, LANES))],
        out_specs=[pl.BlockSpec((tb, LANES), lambda i: (nb - 1 - i, 0)),
                   pl.BlockSpec((SUBLANES, LANES), lambda i: (0, 0))],
        scratch_shapes=[pltpu.VMEM((SUBLANES, LANES), F32)],
        compiler_params=_params(("arbitrary",)),
    )(dcum_k, dcum_q, f, bf_pad)


def _dh_norm_bwd(d_a, d_q, d_k, d_v, d_f, d_g, w_t, x, dx2, gnorm, scale1, tm=512):
    s = x.shape[0]
    nt = s // tm

    def body(da_ref, dq_ref, dk_ref, dv_ref, df_ref, dg_ref, w_ref, x_ref, dx2_ref, g_ref, sc_ref, gx_ref, vec_ref,
             a_sh, a_sc, a_g):
        step = pl.program_id(0)

        @pl.when(step == 0)
        def _():
            a_sh[...] = jnp.zeros_like(a_sh)
            a_sc[...] = jnp.zeros_like(a_sc)
            a_g[...] = jnp.zeros_like(a_g)

        def fold(v):
            return jnp.sum(v.reshape(tm // SUBLANES, SUBLANES, D_MODEL), axis=0)

        d_all = jnp.concatenate([da_ref[...], dq_ref[...], dk_ref[...], dv_ref[...], df_ref[...], dg_ref[...]], axis=1)
        dh = _dot(d_all, w_ref[...])
        xv = x_ref[...]
        r = lax.rsqrt(jnp.mean(xv * xv, axis=-1, keepdims=True) + NORM_EPS)
        xn = xv * r
        gn = g_ref[...]
        a_sh[...] += fold(dh)
        a_sc[...] += fold(dh * (xn * gn))
        dn1 = dh * sc_ref[...]
        a_g[...] += fold(dn1 * xn)
        dxn = dn1 * gn
        gx_ref[...] = dx2_ref[...] + r * (dxn - xn * jnp.mean(dxn * xn, axis=-1, keepdims=True))

        @pl.when(step == nt - 1)
        def _():
            sub = lax.broadcasted_iota(jnp.int32, (SUBLANES, D_MODEL), 0)
            v_sh = jnp.sum(a_sh[...], axis=0, keepdims=True)
            v_sc = jnp.sum(a_sc[...], axis=0, keepdims=True)
            v_g = jnp.sum(a_g[...], axis=0, keepdims=True)
            vec_ref[...] = jnp.where(sub == 0, v_sh, jnp.where(sub == 1, v_sc, jnp.where(sub == 2, v_g, 0.0)))

    row = lambda w: pl.BlockSpec((tm, w), lambda i: (i, 0))
    return pl.pallas_call(
        body, name="dh_norm_bwd", grid=(nt,),
        out_shape=[jax.ShapeDtypeStruct((s, D_MODEL), F32), jax.ShapeDtypeStruct((SUBLANES, D_MODEL), F32)],
        in_specs=[row(W_A), row(512), row(512), row(512), row(W_F), row(W_G), _const_spec((W_INT, D_MODEL)),
                  row(D_MODEL), row(D_MODEL), _const_spec((1, D_MODEL)), _const_spec((1, D_MODEL))],
        out_specs=[row(D_MODEL), pl.BlockSpec((SUBLANES, D_MODEL), lambda i: (0, 0))],
        scratch_shapes=[pltpu.VMEM((SUBLANES, D_MODEL), F32)] * 3,
        compiler_params=_params(("arbitrary",), VMEM_LIMIT),
    )(d_a, d_q, d_k, d_v, d_f, d_g, w_t, x, dx2, gnorm, scale1)


def _dw_in(h_t, d, name, tn, ts=1024):
    s, n = d.shape
    ns = s // ts

    def body(h_ref, d_ref, o_ref, acc):
        k = pl.program_id(1)

        @pl.when(k == 0)
        def _():
            acc[...] = jnp.zeros_like(acc)

        acc[...] += _dot(h_ref[...], d_ref[...])

        @pl.when(k == ns - 1)
        def _():
            o_ref[...] = acc[...]

    return pl.pallas_call(
        body, name=name, grid=(n // tn, ns),
        out_shape=jax.ShapeDtypeStruct((D_MODEL, n), F32),
        in_specs=[pl.BlockSpec((D_MODEL, ts), lambda jn, k: (0, k)), pl.BlockSpec((ts, tn), lambda jn, k: (k, jn))],
        out_specs=pl.BlockSpec((D_MODEL, tn), lambda jn, k: (0, jn)),
        scratch_shapes=[pltpu.VMEM((D_MODEL, tn), F32)],
        compiler_params=_params(("parallel", "arbitrary"), VMEM_LIMIT),
    )(h_t, d)


def _small_grads(packs, c_t, dada_shard):
    def body(p_ref, ct_ref, da_ref, sum_ref, gw_ref):
        acc = p_ref[0]
        for dev in range(1, 8):
            acc = acc + p_ref[dev]
        sum_ref[...] = acc
        gw_ref[...] = jnp.dot(ct_ref[...], da_ref[...], preferred_element_type=F32, precision=lax.Precision.HIGHEST)

    return pl.pallas_call(
        body, name="small_grads",
        out_shape=[jax.ShapeDtypeStruct(packs.shape[1:], F32),
                   jax.ShapeDtypeStruct((c_t.shape[0], dada_shard.shape[1]), F32)],
    )(packs, c_t, dada_shard)


def _adamw_body(w_ref, g_ref, m_ref, v_ref, d_ref, mo_ref, vo_ref):
    c1 = 1.0 / (1.0 - ADAM_B1 ** ADAM_STEP)
    c2 = 1.0 / (1.0 - ADAM_B2 ** ADAM_STEP)
    gv = g_ref[...]
    mn = ADAM_B1 * m_ref[...] + (1.0 - ADAM_B1) * gv
    vn = ADAM_B2 * v_ref[...] + (1.0 - ADAM_B2) * (gv * gv)
    mo_ref[...] = mn
    vo_ref[...] = vn
    d_ref[...] = -ADAM_LR * ((mn * c1) / (jnp.sqrt(vn * c2) + ADAM_EPS) + ADAM_WD * w_ref[...])


def _adamw3(w, g, m, v, name, tb=128):
    spec = pl.BlockSpec((tb, SUBLANES, LANES), lambda i: (i, 0, 0))
    return pl.pallas_call(
        functools.partial(_adamw_body), name=name, grid=(pl.cdiv(w.shape[0], tb),),
        out_shape=[jax.ShapeDtypeStruct(w.shape, F32)] * 3,
        in_specs=[spec] * 4, out_specs=[spec] * 3,
        compiler_params=_params(("parallel",)),
    )(w, g, m, v)


def _adamw(w, g, m, v, name):
    r, c = w.shape
    tr = 128 if r % 128 == 0 else r
    body = functools.partial(_adamw_body)
    spec = pl.BlockSpec((tr, c), lambda i: (i, 0))
    return pl.pallas_call(
        body, name=name, grid=(r // tr,),
        out_shape=[jax.ShapeDtypeStruct((r, c), F32)] * 3,
        in_specs=[spec] * 4, out_specs=[spec] * 3,
        compiler_params=_params(("parallel",)),
    )(w, g, m, v)


def _rope_inputs(positions):
    inv_freq = 10000.0 ** (-jnp.arange(0, HEAD_DIM, 2, dtype=F32) / HEAD_DIM)
    pos = jnp.broadcast_to(positions.astype(F32)[:, None], (positions.shape[0], LANES))
    return pos, jnp.tile(inv_freq, 4)[None, :]


def _pad_rows(v, rows=SUBLANES):
    return jnp.pad(v, ((0, rows - v.shape[0]), (0, 0)))


def kernel(x, c, positions, w_ada, b_ada, g_norm, w_in, b_f, sinks, w_o_swa, w_o_fox, w_out, g_final, loss_target, m_w_ada, m_b_ada, m_g_norm, m_w_in, m_b_f, m_sinks, m_w_o_swa, m_w_o_fox, m_w_out, m_g_final, v_w_ada, v_b_ada, v_g_norm, v_w_in, v_b_f, v_sinks, v_w_o_swa, v_w_o_fox, v_w_out, v_g_final):
    ix, iy, ic = lax.axis_index("x"), lax.axis_index("y"), lax.axis_index("c")
    chip = 2 * ix + iy
    dev = 2 * chip + ic
    xs, tgt = x[0], loss_target[0]
    s = xs.shape[0]

    b_ada_shard = lax.dynamic_slice(b_ada, (0, chip * 768), (1, 768))
    ada_parts, g_in, g_oa, g_ob, g_out = _gather_inputs(
        _pad_rows(c), w_ada[0], b_ada_shard, [w_in[0], w_o_swa[0], w_o_fox[0], w_out[0]], "gather_inputs")
    ada = lax.dynamic_index_in_dim(ada_parts, dev, axis=1, keepdims=False).reshape(1, 3 * D_MODEL)
    shift, scale, gate = ada[:, :D_MODEL], ada[:, D_MODEL:2 * D_MODEL], ada[:, 2 * D_MODEL:]
    scale1 = 1.0 + scale

    w_ref_order = jnp.transpose(g_in, (1, 0, 2)).reshape(D_MODEL, R_END)
    w_int = jnp.concatenate([
        w_ref_order[:, :R_ZA], w_ref_order[:, R_QB:R_FB], w_ref_order[:, R_FB:R_ZB],
        jnp.zeros((D_MODEL, W_F - N_HEADS), BF16), w_ref_order[:, R_ZA:R_QB], w_ref_order[:, R_ZB:]], axis=1)
    w_int_t = w_int.T
    wo_a = jnp.transpose(g_oa, (1, 0, 2)).reshape(512, D_MODEL)
    wo_b = jnp.transpose(g_ob, (1, 0, 2)).reshape(512, D_MODEL)
    w_o = g_out.reshape(D_MODEL, D_MODEL)

    pos, freq = _rope_inputs(positions[0])
    bf_pad = jnp.pad(b_f, ((0, 0), (0, LANES - N_HEADS)))
    sink_vec = sinks[0]

    a, b, f, g, h_t, cos, sin = _norm_proj(xs, g_norm * scale1, shift, w_int, pos, freq)
    att_a, l_swa = _swa_fwd(a, sink_vec)
    cum = _fox_cumsum(f, bf_pad)
    qa, ka, va, stats = _fox_prep(b, cum, FOX_TILE)
    ranges = _fox_tile_ranges(stats)
    att_b, lse = _fox_fwd(qa, ka, va, ranges, FOX_TILE)

    dx2, datt_a, datt_b, d_g, delta8, dwo_a, dwo_b, dw_out, vec_mid = _mid(
        att_a, att_b, g, xs, tgt, gate, g_final.reshape(1, D_MODEL), wo_a, wo_b, w_o)
    delta = jnp.pad(delta8.reshape(4, 2, s), ((0, 0), (0, SUBLANES - 2), (0, 0)))
    d_a, dsink = _swa_bwd(a, datt_a, l_swa, sink_vec, cos, sin)
    dq, dk, dv, dcum_k, dcum_q = _fox_bwd(qa, ka, b, datt_b, lse, delta, ranges, FOX_TILE)
    d_f, dbf = _fox_cumsum_bwd(dcum_k, dcum_q, f, bf_pad)
    grad_x, vec_dh = _dh_norm_bwd(d_a, dq, dk, dv, d_f, d_g, w_int_t, xs, dx2, g_norm, scale1)
    dw_a = _dw_in(h_t, d_a, "dw_in_a", 768)
    dw_q = _dw_in(h_t, dq, "dw_in_q", 512)
    dw_k = _dw_in(h_t, dk, "dw_in_k", 512)
    dw_v = _dw_in(h_t, dv, "dw_in_v", 512)
    dw_f = _dw_in(h_t, d_f, "dw_in_f", 128)
    dw_g = _dw_in(h_t, d_g, "dw_in_g", 1024)
    dw_in = jnp.concatenate([dw_a, dw_g[:, :512], dw_q, dw_k, dw_v, dw_f[:, :N_HEADS], dw_g[:, 512:]], axis=1)

    tail = jnp.pad(jnp.concatenate([dbf[0:1, :N_HEADS], dsink[0:1, :N_HEADS]], axis=1), ((0, 0), (0, D_MODEL - 2 * N_HEADS)))
    pack = jnp.concatenate([c, vec_dh[0:2], vec_mid[1:2], vec_dh[2:3], vec_mid[0:1], tail, vec_mid[2:3]], axis=0)

    def slots(w, axis):
        if axis == 1:
            return jnp.transpose(w.reshape(w.shape[0], 4, w.shape[1] // 4), (1, 0, 2))
        return w.reshape(4, w.shape[0] // 4, w.shape[1])

    packs, g_w_in, g_wo_a, g_wo_b, g_w_out = _reduce_scatter(
        [slots(dw_in, 1), slots(dwo_a, 1), slots(dwo_b, 1), slots(dw_out, 0)], pack, "reduce_grads")
    dada_all = packs[:, 1:4, :].reshape(8, 3 * D_MODEL)
    dada_shard = lax.dynamic_slice(dada_all, (0, chip * 768), (8, 768))
    sums, g_w_ada = _small_grads(packs, packs[:, 0, :].T, dada_shard)
    g_b_ada = sums[1:4].reshape(1, 3 * D_MODEL)
    g_g_norm = sums[4:5]
    g_g_final = sums[5]
    g_b_f = sums[6:7, :N_HEADS]
    g_sinks = sums[6:7, N_HEADS:2 * N_HEADS]
    loss = sums[7, 0]

    grads = {
        "w_ada": g_w_ada, "b_ada": g_b_ada, "g_norm": g_g_norm, "w_in": g_w_in, "b_f": g_b_f, "sinks": g_sinks,
        "w_o_swa": g_wo_a, "w_o_fox": g_wo_b, "w_out": g_w_out, "g_final": g_g_final,
    }
    params = {
        "w_ada": (w_ada, m_w_ada, v_w_ada), "b_ada": (b_ada, m_b_ada, v_b_ada), "g_norm": (g_norm, m_g_norm, v_g_norm),
        "w_in": (w_in, m_w_in, v_w_in), "b_f": (b_f, m_b_f, v_b_f), "sinks": (sinks, m_sinks, v_sinks),
        "w_o_swa": (w_o_swa, m_w_o_swa, v_w_o_swa), "w_o_fox": (w_o_fox, m_w_o_fox, v_w_o_fox),
        "w_out": (w_out, m_w_out, v_w_out), "g_final": (g_final, m_g_final, v_g_final),
    }
    n_col = w_in.shape[2]

    def as_stored(t):
        return jnp.transpose(t, (2, 0, 1)).reshape(n_col, SUBLANES, LANES)

    def from_stored(t):
        return jnp.transpose(t, (1, 2, 0)).reshape(1, D_MODEL, n_col)

    names = list(grads)
    out_g, out_d, out_m, out_v = [], [], [], []
    for nm in names:
        w, m, v = params[nm]
        if nm == "w_in":
            g_st = as_stored(grads[nm][None])
            d_, m_, v_ = _adamw3(as_stored(w), g_st, as_stored(m), as_stored(v), "adamw_" + nm)
            res = [from_stored(t) for t in (g_st, d_, m_, v_)]
        else:
            shape2 = (w.shape[-2], w.shape[-1]) if w.ndim >= 2 else (1, w.shape[0])
            d_, m_, v_ = _adamw(w.reshape(shape2), grads[nm].reshape(shape2), m.reshape(shape2), v.reshape(shape2), "adamw_" + nm)
            res = [t.reshape(w.shape) for t in (grads[nm], d_, m_, v_)]
        out_g.append(res[0])
        out_d.append(res[1])
        out_m.append(res[2])
        out_v.append(res[3])
    return (loss, grad_x[None], *out_g, *out_d, *out_m, *out_v)
```

```python
import functools

import numpy as np
import jax
import jax.numpy as jnp
from jax import lax
from jax.experimental import pallas as pl
from jax.experimental.pallas import tpu as pltpu

F32 = jnp.float32
BF16 = jnp.bfloat16
MESH = pl.DeviceIdType.MESH

D_MODEL = 1024
HEAD_DIM = 64
N_HEADS = 8
WINDOW = 128
NORM_EPS = 1e-6
SCALE = HEAD_DIM ** -0.5
NEG = -1e30
LANES = 128
SUBLANES = 8
VMEM_LIMIT = 60 * 1024 * 1024
FOX_TILE = 512

W_A, W_B, W_F, W_G = 768, 1536, 128, 3072
OFF_A, OFF_B, OFF_F, OFF_G = 0, 768, 2304, 2432
W_INT = W_A + W_B + W_F + W_G
R_ZA, R_QB, R_FB, R_ZB, R_END = 768, 1280, 2816, 2824, 5384

ADAM_LR, ADAM_B1, ADAM_B2, ADAM_EPS, ADAM_WD, ADAM_STEP = 0.001, 0.9, 0.999, 1e-08, 0.01, 10

NT = (((1,), (1,)), ((), ()))
TN = (((0,), (0,)), ((), ()))


def _dot(a, b, dims=None):
    if dims is None:
        return jnp.dot(a, b, preferred_element_type=F32)
    return lax.dot_general(a, b, dims, preferred_element_type=F32)


def _split3(v):
    hi = v.astype(BF16)
    r1 = v - hi.astype(F32)
    mid = r1.astype(BF16)
    lo = (r1 - mid.astype(F32)).astype(BF16)
    return hi, mid, lo


def _sigmoid(v):
    return 1.0 / (1.0 + jnp.exp(-v))


def _params(sem=None, vmem=None):
    return pltpu.CompilerParams(dimension_semantics=sem, vmem_limit_bytes=vmem)


def _const_spec(shape):
    nd = len(shape)
    return pl.BlockSpec(shape, lambda *_: (0,) * nd, pipeline_mode=pl.Buffered(1))


def _flip(v, f):
    return 1 - v if f else v


_CHIP_FLIPS = ((1, 0), (0, 1), (1, 1))


def _gather_inputs(c_pad, w_ada, b_ada_shard, shards, name):
    n = len(shards)
    n_col = w_ada.shape[1]

    def body(*refs):
        c_ref, wa_ref, ba_ref = refs[:3]
        ins = refs[3:3 + n]
        ada_ref = refs[3 + n]
        outs = refs[4 + n:4 + 2 * n]
        call_ref, send_sems, recv_sems = refs[4 + 2 * n:7 + 2 * n]
        x, y, c = lax.axis_index("x"), lax.axis_index("y"), lax.axis_index("c")
        k_me = 2 * x + y
        me = 2 * k_me + c
        sibling = (x, y, 1 - c)
        chips = [(_flip(x, fx), _flip(y, fy)) for fx, fy in _CHIP_FLIPS]

        def piece(i, chip_k, half):
            hr = ins[i].shape[0] // 2
            return outs[i].at[chip_k, pl.ds(half * hr, hr), :]

        def copy(i, slot, chip_k, half, to):
            return pltpu.make_async_remote_copy(
                src_ref=piece(i, chip_k, half), dst_ref=piece(i, chip_k, half),
                send_sem=send_sems.at[6 * i + slot], recv_sem=recv_sems.at[6 * i + slot],
                device_id=to, device_id_type=MESH)

        def small(ref, slot, sem, to):
            return pltpu.make_async_remote_copy(
                src_ref=ref.at[slot], dst_ref=ref.at[slot], send_sem=send_sems.at[6 * n + sem],
                recv_sem=recv_sems.at[6 * n + sem], device_id=to, device_id_type=MESH)

        for i in range(n):
            outs[i][k_me] = ins[i][...].astype(BF16)
        started = []
        for i in range(n):
            for j, chip in enumerate(chips):
                cp = copy(i, j, k_me, c, (chip[0], chip[1], c))
                cp.start()
                started.append(cp)

        call_ref[me] = c_ref[...]
        peers = [(_flip(x, k & 4), _flip(y, k & 2), _flip(c, k & 1)) for k in range(1, 8)]
        for k, peer in enumerate(peers):
            cp = small(call_ref, me, k, peer)
            cp.start()
            started.append(cp)
        for k, peer in enumerate(peers):
            small(call_ref, 4 * peer[0] + 2 * peer[1] + peer[2], k, peer).wait_recv()
        c_all = call_ref[:, 0, :].astype(BF16)
        ada_ref[k_me] = _dot(c_all, wa_ref[...].astype(BF16)) + ba_ref[...]
        for j, chip in enumerate(chips):
            cp = small(ada_ref, k_me, 7 + j, (chip[0], chip[1], c))
            cp.start()
            started.append(cp)

        for j, chip in enumerate(chips):
            chip_k = 2 * chip[0] + chip[1]
            for i in range(n):
                copy(i, j, chip_k, c, (chip[0], chip[1], c)).wait_recv()
                cp = copy(i, 3 + j, chip_k, c, sibling)
                cp.start()
                started.append(cp)
        for j, chip in enumerate(chips):
            chip_k = 2 * chip[0] + chip[1]
            small(ada_ref, chip_k, 7 + j, (chip[0], chip[1], c)).wait_recv()
            for i in range(n):
                copy(i, 3 + j, chip_k, 1 - c, sibling).wait_recv()
        for cp in started:
            cp.wait_send()

    vmem = pl.BlockSpec(memory_space=pltpu.VMEM)
    return pl.pallas_call(
        body, name=name,
        out_shape=[jax.ShapeDtypeStruct((4, 8, n_col), F32)] + [jax.ShapeDtypeStruct((4,) + s.shape, BF16) for s in shards],
        in_specs=[vmem] * (3 + n),
        out_specs=[vmem] * (1 + n),
        scratch_shapes=[pltpu.VMEM((8,) + c_pad.shape, F32),
                        pltpu.SemaphoreType.DMA((6 * n + 10,)), pltpu.SemaphoreType.DMA((6 * n + 10,))],
        compiler_params=_params(vmem=VMEM_LIMIT),
    )(c_pad, w_ada, b_ada_shard, *shards)


def _reduce_scatter(pieces, pack, name):
    n = len(pieces)

    def body(*refs):
        pack_ref, ins = refs[0], refs[1:1 + n]
        packs_ref, outs = refs[1 + n], refs[2 + n:2 + 2 * n]
        rest = refs[2 + 2 * n:]
        own, got = rest[:n], rest[n:2 * n]
        sendb, recvb = rest[2 * n:3 * n], rest[3 * n:4 * n]
        send_sems, recv_sems, local_sems = rest[4 * n:4 * n + 3]
        x, y, c = lax.axis_index("x"), lax.axis_index("y"), lax.axis_index("c")
        k_me = 2 * x + y
        me = 2 * k_me + c
        sibling = (x, y, 1 - c)
        chips = [(_flip(x, fx), _flip(y, fy)) for fx, fy in _CHIP_FLIPS]
        hrs = [p.shape[1] // 2 for p in pieces]

        def remote(i, slot, src, dst, to):
            return pltpu.make_async_remote_copy(
                src_ref=src, dst_ref=dst, send_sem=send_sems.at[5 * i + slot], recv_sem=recv_sems.at[5 * i + slot],
                device_id=to, device_id_type=MESH)

        started = []
        packs_ref[me] = pack_ref[...]
        peers = [(_flip(x, k & 4), _flip(y, k & 2), _flip(c, k & 1)) for k in range(1, 8)]
        for k, peer in enumerate(peers):
            cp = pltpu.make_async_remote_copy(
                src_ref=pack_ref, dst_ref=packs_ref.at[me], send_sem=send_sems.at[5 * n + k],
                recv_sem=recv_sems.at[5 * n + k], device_id=peer, device_id_type=MESH)
            cp.start()
            started.append(cp)
        loads = []
        for i in range(n):
            ld = pltpu.make_async_copy(ins[i].at[:, pl.ds(c * hrs[i], hrs[i]), :], own[i], local_sems.at[i])
            ld.start()
            loads.append(ld)
            cp = remote(i, 0, ins[i].at[:, pl.ds((1 - c) * hrs[i], hrs[i]), :], got[i], sibling)
            cp.start()
            started.append(cp)
        for i in range(n):
            loads[i].wait()
            remote(i, 0, ins[i].at[:, pl.ds(c * hrs[i], hrs[i]), :], got[i], sibling).wait_recv()
            for j, chip in enumerate(chips):
                chip_k = 2 * chip[0] + chip[1]
                sendb[i][j] = (own[i][chip_k] + got[i][chip_k]).astype(BF16)
                cp = remote(i, 1 + j, sendb[i].at[j], recvb[i].at[j], (chip[0], chip[1], c))
                cp.start()
                started.append(cp)
        for i in range(n):
            acc = own[i][k_me] + got[i][k_me]
            for j, chip in enumerate(chips):
                remote(i, 1 + j, sendb[i].at[j], recvb[i].at[j], (chip[0], chip[1], c)).wait_recv()
                acc = acc + recvb[i][j].astype(F32)
            mine = outs[i].at[pl.ds(c * hrs[i], hrs[i]), :]
            outs[i][pl.ds(pl.multiple_of(c * hrs[i], SUBLANES), hrs[i]), :] = acc
            cp = remote(i, 4, mine, mine, sibling)
            cp.start()
            started.append(cp)
        for i in range(n):
            theirs = outs[i].at[pl.ds((1 - c) * hrs[i], hrs[i]), :]
            remote(i, 4, theirs, theirs, sibling).wait_recv()
        for k, peer in enumerate(peers):
            pltpu.make_async_remote_copy(
                src_ref=pack_ref, dst_ref=packs_ref.at[4 * peer[0] + 2 * peer[1] + peer[2]],
                send_sem=send_sems.at[5 * n + k], recv_sem=recv_sems.at[5 * n + k],
                device_id=peer, device_id_type=MESH).wait_recv()
        for cp in started:
            cp.wait_send()

    vmem = pl.BlockSpec(memory_space=pltpu.VMEM)
    scratch = []
    scratch += [pltpu.VMEM((4, p.shape[1] // 2, p.shape[2]), F32) for p in pieces]
    scratch += [pltpu.VMEM((4, p.shape[1] // 2, p.shape[2]), F32) for p in pieces]
    scratch += [pltpu.VMEM((3, p.shape[1] // 2, p.shape[2]), BF16) for p in pieces]
    scratch += [pltpu.VMEM((3, p.shape[1] // 2, p.shape[2]), BF16) for p in pieces]
    scratch += [pltpu.SemaphoreType.DMA((5 * n + 7,)), pltpu.SemaphoreType.DMA((5 * n + 7,)), pltpu.SemaphoreType.DMA((n,))]
    return pl.pallas_call(
        body, name=name,
        out_shape=[jax.ShapeDtypeStruct((8,) + pack.shape, F32)] + [jax.ShapeDtypeStruct(p.shape[1:], F32) for p in pieces],
        in_specs=[vmem] + [pl.BlockSpec(memory_space=pl.ANY)] * n,
        out_specs=[vmem] * (1 + n),
        scratch_shapes=scratch,
        compiler_params=_params(vmem=VMEM_LIMIT),
    )(pack, *pieces)


def _rope_fwd(t, cos, sin, lane):
    lo = (lane % HEAD_DIM) < (HEAD_DIM // 2)
    return t * cos + jnp.where(lo, -pltpu.roll(t, 96, 1), pltpu.roll(t, 32, 1)) * sin


def _norm_proj(x, gmod, shift, w_int, pos, freq, tm=512):
    s = x.shape[0]

    def body(x_ref, g_ref, sh_ref, w_ref, pos_ref, fr_ref, a_ref, b_ref, f_ref, gg_ref, ht_ref, cos_ref, sin_ref):
        xv = x_ref[...]
        r = lax.rsqrt(jnp.mean(xv * xv, axis=-1, keepdims=True) + NORM_EPS)
        hf = (xv * r) * g_ref[...] + sh_ref[...]
        hb = hf.astype(BF16)
        ht_ref[...] = hf.T.astype(BF16)
        pa = _dot(hb, w_ref[:, OFF_A:OFF_A + W_A])
        ang = pos_ref[...] * fr_ref[...]
        cosv, sinv = jnp.cos(ang), jnp.sin(ang)
        cos_ref[...] = cosv
        sin_ref[...] = sinv
        lane = lax.broadcasted_iota(jnp.int32, (tm, LANES), 1)
        for j in range(5):
            t = pa[:, LANES * j:LANES * (j + 1)]
            a_ref[:, LANES * j:LANES * (j + 1)] = _rope_fwd(t, cosv, sinv, lane).astype(BF16)
        a_ref[:, 640:768] = pa[:, 640:768].astype(BF16)
        b_ref[...] = _dot(hb, w_ref[:, OFF_B:OFF_B + W_B]).astype(BF16)
        f_ref[...] = _dot(hb, w_ref[:, OFF_F:OFF_F + W_F])
        gg_ref[...] = _dot(hb, w_ref[:, OFF_G:OFF_G + W_G]).astype(BF16)

    row = lambda w: pl.BlockSpec((tm, w), lambda i: (i, 0))
    return pl.pallas_call(
        body, name="norm_proj", grid=(s // tm,),
        out_shape=[jax.ShapeDtypeStruct((s, W_A), BF16), jax.ShapeDtypeStruct((s, W_B), BF16),
                   jax.ShapeDtypeStruct((s, W_F), F32), jax.ShapeDtypeStruct((s, W_G), BF16),
                   jax.ShapeDtypeStruct((D_MODEL, s), BF16),
                   jax.ShapeDtypeStruct((s, LANES), F32), jax.ShapeDtypeStruct((s, LANES), F32)],
        in_specs=[row(D_MODEL), _const_spec((1, D_MODEL)), _const_spec((1, D_MODEL)), _const_spec((D_MODEL, W_INT)),
                  row(LANES), _const_spec((1, LANES))],
        out_specs=[row(W_A), row(W_B), row(W_F), row(W_G), pl.BlockSpec((D_MODEL, tm), lambda i: (0, i)),
                   row(LANES), row(LANES)],
        compiler_params=_params(("parallel",), VMEM_LIMIT),
    )(x, gmod, shift, w_int, pos, freq)


def _log_sigmoid(u):
    return jnp.minimum(u, 0.0) - jnp.log(1.0 + jnp.exp(-jnp.abs(u)))


def _fox_cumsum(f, bf_pad, tb=512):
    s = f.shape[0]

    def body(f_ref, b_ref, cum_ref, carry):
        @pl.when(pl.program_id(0) == 0)
        def _():
            carry[...] = jnp.zeros_like(carry)

        lane = lax.broadcasted_iota(jnp.int32, (tb, LANES), 1)
        logf = jnp.where(lane < N_HEADS, _log_sigmoid(f_ref[...] + b_ref[...]), 0.0)
        hi, mid, lo = _split3(logf)
        rows = lax.broadcasted_iota(jnp.int32, (tb, tb), 0)
        cols = lax.broadcasted_iota(jnp.int32, (tb, tb), 1)
        tril = (cols <= rows).astype(BF16)
        cum = _dot(tril, hi) + _dot(tril, mid) + _dot(tril, lo) + carry[0:1, :]
        cum_ref[...] = cum
        carry[...] = jnp.broadcast_to(cum[tb - 1:tb, :], carry.shape)

    return pl.pallas_call(
        body, name="fox_cumsum", grid=(s // tb,),
        out_shape=jax.ShapeDtypeStruct((s, LANES), F32),
        in_specs=[pl.BlockSpec((tb, LANES), lambda i: (i, 0)), _const_spec((1, LANES))],
        out_specs=pl.BlockSpec((tb, LANES), lambda i: (i, 0)),
        scratch_shapes=[pltpu.VMEM((SUBLANES, LANES), F32)],
        compiler_params=_params(("arbitrary",)),
    )(f, bf_pad)


def _fox_prep(b, cum, t):
    s = b.shape[0]

    def body(b_ref, cum_ref, q_ref, k_ref, v_ref, st_ref):
        lane = lax.broadcasted_iota(jnp.int32, (t, LANES), 1)
        sub8 = lax.broadcasted_iota(jnp.int32, (SUBLANES, LANES), 0)
        lane8 = lax.broadcasted_iota(jnp.int32, (SUBLANES, LANES), 1)
        cumv = cum_ref[...]
        stats = jnp.zeros((SUBLANES, LANES), F32)
        for h in range(N_HEADS):
            p, odd = h // 2, h % 2
            ch = jnp.broadcast_to(cumv[:, h:h + 1], (t, LANES))
            hi, mid, lo = (x.astype(F32) for x in _split3(ch))
            qp = b_ref[:, LANES * p:LANES * (p + 1)].astype(F32)
            kp = b_ref[:, 512 + LANES * p:512 + LANES * (p + 1)].astype(F32)
            vp = b_ref[:, 1024 + LANES * p:1024 + LANES * (p + 1)].astype(F32)
            if odd:
                qp, kp, vp = pltpu.roll(qp, 64, 1), pltpu.roll(kp, 64, 1), pltpu.roll(vp, 64, 1)
            qs = jnp.where(lane < 64, qp * SCALE, 0.0)
            ks = jnp.where(lane < 64, kp, 0.0)
            qa = jnp.where(lane < 64, qs,
                           jnp.where(lane == 64, hi, jnp.where(lane == 65, mid, jnp.where(lane == 66, lo,
                           jnp.where(lane < 70, 1.0, 0.0)))))
            ka = jnp.where(lane < 64, ks,
                           jnp.where(lane < 67, 1.0, jnp.where(lane == 67, -hi, jnp.where(lane == 68, -mid,
                           jnp.where(lane == 69, -lo, 0.0)))))
            q_ref[:, LANES * h:LANES * (h + 1)] = qa.astype(BF16)
            k_ref[:, LANES * h:LANES * (h + 1)] = ka.astype(BF16)
            v_ref[:, LANES * h:LANES * (h + 1)] = jnp.where(lane < 64, vp, jnp.where(lane == 64, 1.0, 0.0)).astype(BF16)
            qn = jnp.sqrt(jnp.max(jnp.sum(qs * qs, axis=-1, keepdims=True)))
            kn = jnp.sqrt(jnp.max(jnp.sum(ks * ks, axis=-1, keepdims=True)))
            dmin = jnp.min(jnp.sum(qs * ks, axis=-1, keepdims=True))
            csum = hi + mid + lo
            row = jnp.where(lane8 == 0, qn, jnp.where(lane8 == 1, kn, jnp.where(
                lane8 == 2, csum[0:1, :], jnp.where(lane8 == 3, csum[t - 1:t, :], jnp.where(lane8 == 4, dmin, 0.0)))))
            stats = jnp.where(sub8 == h, row, stats)
        st_ref[0] = stats

    wide = pl.BlockSpec((t, 1024), lambda i: (i, 0))
    return pl.pallas_call(
        body, name="fox_prep", grid=(s // t,),
        out_shape=[jax.ShapeDtypeStruct((s, 1024), BF16)] * 3 + [jax.ShapeDtypeStruct((s // t, SUBLANES, LANES), F32)],
        in_specs=[pl.BlockSpec((t, W_B), lambda i: (i, 0)), pl.BlockSpec((t, LANES), lambda i: (i, 0))],
        out_specs=[wide, wide, wide, pl.BlockSpec((1, SUBLANES, LANES), lambda i: (i, 0, 0))],
        compiler_params=_params(("parallel",)),
    )(b, cum)


PRUNE_MARGIN = 90.0


def _fox_tile_ranges(stats):
    nt = stats.shape[0]
    qk_max = 1.01 * jnp.max(stats[:, :, 0], axis=0) * jnp.max(stats[:, :, 1], axis=0)
    ball = qk_max - jnp.minimum(jnp.min(stats[:, :, 4], axis=0), 0.0) + 0.05
    d = ball[None, None, :] + stats[:, None, :, 2] - stats[None, :, :, 3]
    idx = jnp.arange(nt)
    skip = (d <= -PRUNE_MARGIN) & (idx[None, :, None] < idx[:, None, None])
    skip = skip.reshape(nt, nt, 4, 2).all(axis=-1)
    first_key = jnp.sum(skip, axis=1).astype(F32)
    needed = (~skip) & (idx[None, :, None] <= idx[:, None, None])
    n_query = jnp.sum(needed, axis=0).astype(F32)
    table = jnp.zeros((4, SUBLANES, LANES), F32)
    table = table.at[:, 0, :nt].set(first_key.T)
    table = table.at[:, 1, :nt].set(n_query.T)
    return table


def _lane_scalar(block, row, lane_idx):
    sub8 = lax.broadcasted_iota(jnp.int32, (SUBLANES, LANES), 0)
    lane8 = lax.broadcasted_iota(jnp.int32, (SUBLANES, LANES), 1)
    return jnp.sum(jnp.where((sub8 == row) & (lane8 == lane_idx), block, 0.0)).astype(jnp.int32)


def _fox_fwd(qa, ka, va, ranges, t):
    s = qa.shape[0]
    nt = s // t
    nc = t // LANES

    def body(rg_ref, q_ref, k_ref, v_ref, o_ref, lse_ref):
        i = pl.program_id(1)
        lane = lax.broadcasted_iota(jnp.int32, (t, LANES), 1)
        rows = lax.broadcasted_iota(jnp.int32, (t, t), 0)
        cols = lax.broadcasted_iota(jnp.int32, (t, t), 1)
        first = jnp.clip(_lane_scalar(rg_ref[0], 0, i), 0, i)

        heads = range(2)

        def update(js, carry, diagonal=False):
            offs = [pl.multiple_of(j * t, t) for j in js]
            kts = [k_ref[pl.ds(off, t), :] for off in offs]
            vts = [v_ref[pl.ds(off, t), :] for off in offs]
            scs = [[_dot(q_ref[:, LANES * hh:LANES * (hh + 1)], kt[:, LANES * hh:LANES * (hh + 1)], NT) for kt in kts]
                   for hh in heads]
            if diagonal:
                scs = [[jnp.where(cols <= rows, sc, NEG) for sc in per_head] for per_head in scs]
            m_new = []
            for hh in heads:
                part = None
                for sc in scs[hh]:
                    for cch in range(nc):
                        chunk = sc[:, LANES * cch:LANES * (cch + 1)]
                        part = chunk if part is None else jnp.maximum(part, chunk)
                m_new.append(jnp.maximum(carry[2 * hh], jnp.max(part, axis=-1, keepdims=True)))
            alphas = [jnp.exp(carry[2 * hh] - m_new[hh]) for hh in heads]
            ps = [[jnp.exp(sc - m_new[hh]).astype(BF16) for sc in scs[hh]] for hh in heads]
            out = []
            for hh in heads:
                pv = None
                for p, vt in zip(ps[hh], vts):
                    term = _dot(p, vt[:, LANES * hh:LANES * (hh + 1)])
                    pv = term if pv is None else pv + term
                out += [m_new[hh], alphas[hh] * carry[2 * hh + 1] + pv]
            return tuple(out)

        col0 = jnp.full((t, 1), NEG, F32)
        zero = jnp.zeros((t, LANES), F32)
        n_off = i - first
        carry = lax.fori_loop(0, n_off // 2, lambda u, cr: update([first + 2 * u, first + 2 * u + 1], cr),
                              (col0, zero, col0, zero))
        carry = lax.fori_loop(0, n_off % 2, lambda u, cr: update([i - 1], cr), carry)
        m0, acc0, m1, acc1 = update([i], carry, diagonal=True)
        l0, l1 = acc0[:, 64:65], acc1[:, 64:65]
        o_ref[...] = jnp.where(lane < 64, acc0 * (1.0 / l0), pltpu.roll(acc1 * (1.0 / l1), 64, 1)).astype(BF16)
        sub = lax.broadcasted_iota(jnp.int32, (SUBLANES, t), 0)
        lse0 = jnp.broadcast_to(m0 + jnp.log(l0), (t, LANES)).T[0:SUBLANES, :]
        lse1 = jnp.broadcast_to(m1 + jnp.log(l1), (t, LANES)).T[0:SUBLANES, :]
        lse_ref[0] = jnp.where(sub == 0, lse0, jnp.where(sub == 1, lse1, 0.0))

    pair = pl.BlockSpec((s, 2 * LANES), lambda p, i: (0, p))
    return pl.pallas_call(
        body, name="fox_fwd", grid=(4, nt),
        out_shape=[jax.ShapeDtypeStruct((s, 512), BF16), jax.ShapeDtypeStruct((4, SUBLANES, s), F32)],
        in_specs=[pl.BlockSpec((1, SUBLANES, LANES), lambda p, i: (p, 0, 0)),
                  pl.BlockSpec((t, 2 * LANES), lambda p, i: (i, p)), pair, pair],
        out_specs=[pl.BlockSpec((t, LANES), lambda p, i: (i, p)),
                   pl.BlockSpec((1, SUBLANES, t), lambda p, i: (p, 0, i))],
        compiler_params=_params(("parallel", "arbitrary"), VMEM_LIMIT),
    )(ranges, qa, ka, va)


def _dup_halves(blk, lane):
    f = blk.astype(F32)
    r = pltpu.roll(f, 64, 1)
    return jnp.where(lane < 64, f, r).astype(BF16), jnp.where(lane >= 64, f, r).astype(BF16)


GROUP = 4
GROUP_ROWS = GROUP * WINDOW


def _stack_heads(ref, g, lane):
    parts = []
    for pb in (2 * g, 2 * g + 1):
        blk = ref[:, LANES * pb:LANES * (pb + 1)]
        zero = jnp.zeros_like(blk)
        parts += [jnp.where(lane < 64, blk, zero), jnp.where(lane >= 64, blk, zero)]
    return jnp.concatenate(parts, axis=0)


def _swa_band(a_ref, ap_ref, g, lane):
    k = jnp.concatenate([_dup_halves(ap_ref[:, 512:640], lane)[g], _dup_halves(a_ref[:, 512:640], lane)[g]], axis=0)
    v = jnp.concatenate([_dup_halves(ap_ref[:, 640:768], lane)[g], _dup_halves(a_ref[:, 640:768], lane)[g]], axis=0)
    return k, v


def _swa_logits(q, k, has_prev):
    sc = _dot(q, k, NT) * SCALE
    rr = lax.broadcasted_iota(jnp.int32, sc.shape, 0) % WINDOW
    cc = lax.broadcasted_iota(jnp.int32, sc.shape, 1)
    valid = (cc > rr) & (cc <= rr + WINDOW) & (has_prev | (cc >= WINDOW))
    return jnp.where(valid, sc, NEG)


def _per_head_column(values):
    return jnp.concatenate([jnp.broadcast_to(v, (WINDOW, 1)) for v in values], axis=0)


SWA_BLOCKS = 4
SWA_ROWS = SWA_BLOCKS * WINDOW


def _swa_blocks(a_ref, ap_ref):
    return [ap_ref] + [a_ref.at[pl.ds(WINDOW * jb, WINDOW), :] for jb in range(SWA_BLOCKS)]


def _swa_fwd(a, sinks):
    s = a.shape[0]

    def body(sink_ref, a_ref, ap_ref, o_ref, l_ref):
        lane = lax.broadcasted_iota(jnp.int32, (WINDOW, LANES), 1)
        blocks = _swa_blocks(a_ref, ap_ref)
        units = [(jb, g) for jb in range(SWA_BLOCKS) for g in range(2)]
        sinks_col = [_per_head_column([sink_ref[GROUP * g + hh] for hh in range(GROUP)]) for g in range(2)]
        bands = [_swa_band(blocks[jb + 1], blocks[jb], g, lane) for jb, g in units]
        scs = [_swa_logits(_stack_heads(blocks[jb + 1], g, lane), bands[u][0],
                           (pl.program_id(0) > 0) if jb == 0 else True) for u, (jb, g) in enumerate(units)]
        ms = [jnp.maximum(jnp.max(scs[u], axis=-1, keepdims=True), sinks_col[g]) for u, (jb, g) in enumerate(units)]
        ps = [jnp.exp(scs[u] - ms[u]) for u in range(len(units))]
        dens = [jnp.sum(ps[u], axis=-1, keepdims=True) + jnp.exp(sinks_col[g] - ms[u]) for u, (jb, g) in enumerate(units)]
        outs = [_dot((ps[u] * (1.0 / dens[u])).astype(BF16), bands[u][1]) for u in range(len(units))]
        for jb in range(SWA_BLOCKS):
            rows = slice(WINDOW * jb, WINDOW * (jb + 1))
            l_all = jnp.zeros((WINDOW, LANES), F32)
            for g in range(2):
                u = 2 * jb + g
                lcol = ms[u] + jnp.log(dens[u])
                for pb in range(2):
                    r0 = 2 * pb * WINDOW
                    o_ref[rows, LANES * (2 * g + pb):LANES * (2 * g + pb + 1)] = jnp.where(
                        lane < 64, outs[u][r0:r0 + WINDOW], outs[u][r0 + WINDOW:r0 + 2 * WINDOW]).astype(BF16)
                for hh in range(GROUP):
                    l_all = jnp.where(lane == GROUP * g + hh, lcol[WINDOW * hh:WINDOW * (hh + 1)], l_all)
            l_ref[rows, :] = l_all

    return pl.pallas_call(
        body, name="swa_fwd", grid=(s // SWA_ROWS,),
        out_shape=[jax.ShapeDtypeStruct((s, 512), BF16), jax.ShapeDtypeStruct((s, LANES), F32)],
        in_specs=[pl.BlockSpec(memory_space=pltpu.SMEM),
                  pl.BlockSpec((SWA_ROWS, W_A), lambda i: (i, 0)),
                  pl.BlockSpec((WINDOW, W_A), lambda i: (jnp.maximum(SWA_BLOCKS * i - 1, 0), 0))],
        out_specs=[pl.BlockSpec((SWA_ROWS, 512), lambda i: (i, 0)), pl.BlockSpec((SWA_ROWS, LANES), lambda i: (i, 0))],
        compiler_params=_params(("parallel",)),
    )(sinks, a, a)


def _mid(att_a, att_b, g, x, target, gate, g_final, wo_a, wo_b, w_out, tm=256):
    s = x.shape[0]
    nt = s // tm

    def body(aa_ref, ab_ref, g_ref, x_ref, t_ref, gate_ref, gf_ref, woa_ref, wob_ref, wout_ref,
             dx_ref, daa_ref, dab_ref, dg_ref, delta_ref, dwoa_ref, dwob_ref, dwout_ref, vec_ref,
             acc_gf, acc_gate, acc_loss):
        step = pl.program_id(0)

        @pl.when(step == 0)
        def _():
            dwoa_ref[...] = jnp.zeros_like(dwoa_ref)
            dwob_ref[...] = jnp.zeros_like(dwob_ref)
            dwout_ref[...] = jnp.zeros_like(dwout_ref)
            acc_gf[...] = jnp.zeros_like(acc_gf)
            acc_gate[...] = jnp.zeros_like(acc_gate)
            acc_loss[...] = jnp.zeros_like(acc_loss)

        def fold(v):
            return jnp.sum(v.reshape(tm // SUBLANES, SUBLANES, D_MODEL), axis=0)

        gate = gate_ref[...]
        gfin = gf_ref[...]
        branches = []
        for att_ref, z_off, wo_ref in ((aa_ref, 0, woa_ref), (ab_ref, 512, wob_ref)):
            att = att_ref[...].astype(F32)
            z = g_ref[:, z_off:z_off + 512].astype(F32)
            sz = _sigmoid(z)
            silu = z * sz
            u = (att * silu).astype(BF16)
            branches.append((att, z, sz, silu, u, _dot(u, wo_ref[...])))
        ga = g_ref[:, 1024:2048].astype(F32)
        gb = g_ref[:, 2048:3072].astype(F32)
        sga, sgb = _sigmoid(ga), _sigmoid(gb)
        y_a, y_b = branches[0][5], branches[1][5]
        mb = (sga * y_a + sgb * y_b).astype(BF16)
        o = _dot(mb, wout_ref[...])
        x2 = x_ref[...] + gate * o
        r2 = lax.rsqrt(jnp.mean(x2 * x2, axis=-1, keepdims=True) + NORM_EPS)
        xn2 = x2 * r2
        err = xn2 * gfin - t_ref[...]
        acc_loss[...] += fold(err * err)
        dy = err * (1.0 / D_MODEL)
        acc_gf[...] += fold(dy * xn2)
        dxn = dy * gfin
        dx2 = r2 * (dxn - xn2 * jnp.mean(dxn * xn2, axis=-1, keepdims=True))
        dx_ref[...] = dx2
        acc_gate[...] += fold(dx2 * o)
        d_o = (dx2 * gate).astype(BF16)
        dwout_ref[...] += _dot(mb, d_o, TN)
        dm = _dot(d_o, wout_ref[...], NT)
        dg_ref[:, 1024:2048] = (dm * y_a * sga * (1.0 - sga)).astype(BF16)
        dg_ref[:, 2048:3072] = (dm * y_b * sgb * (1.0 - sgb)).astype(BF16)
        for (att, z, sz, silu, u, _), sg, wo_ref, dwo_ref, datt_ref, z_off in (
                (branches[0], sga, woa_ref, dwoa_ref, daa_ref, 0), (branches[1], sgb, wob_ref, dwob_ref, dab_ref, 512)):
            dyb = (dm * sg).astype(BF16)
            dwo_ref[...] += _dot(u, dyb, TN)
            du = _dot(dyb, wo_ref[...], NT)
            datt = du * silu
            datt_ref[...] = datt.astype(BF16)
            dg_ref[:, z_off:z_off + 512] = (du * att * (sz * (1.0 + z * (1.0 - sz)))).astype(BF16)
            if z_off == 512:
                prod = datt * att
                hi = prod.astype(BF16)
                lo = (prod - hi.astype(F32)).astype(BF16)
                er = lax.broadcasted_iota(jnp.int32, (512, LANES), 0)
                ec = lax.broadcasted_iota(jnp.int32, (512, LANES), 1)
                e = (er // HEAD_DIM == ec).astype(BF16)
                delta = _dot(hi, e) + _dot(lo, e)
                delta_ref[...] = delta.T[0:SUBLANES, :]

        @pl.when(step == nt - 1)
        def _():
            sub = lax.broadcasted_iota(jnp.int32, (SUBLANES, D_MODEL), 0)
            dgf = jnp.sum(acc_gf[...], axis=0, keepdims=True)
            dgate = jnp.sum(acc_gate[...], axis=0, keepdims=True)
            loss = 0.5 * jnp.sum(acc_loss[...]) * (1.0 / D_MODEL)
            vec_ref[...] = jnp.where(sub == 0, dgf, jnp.where(sub == 1, dgate, jnp.where(sub == 2, loss, 0.0)))

    row = lambda w: pl.BlockSpec((tm, w), lambda i: (i, 0))
    return pl.pallas_call(
        body, name="mid", grid=(nt,),
        out_shape=[jax.ShapeDtypeStruct((s, D_MODEL), F32), jax.ShapeDtypeStruct((s, 512), BF16),
                   jax.ShapeDtypeStruct((s, 512), BF16), jax.ShapeDtypeStruct((s, W_G), BF16),
                   jax.ShapeDtypeStruct((SUBLANES, s), F32),
                   jax.ShapeDtypeStruct((512, D_MODEL), F32), jax.ShapeDtypeStruct((512, D_MODEL), F32),
                   jax.ShapeDtypeStruct((D_MODEL, D_MODEL), F32), jax.ShapeDtypeStruct((SUBLANES, D_MODEL), F32)],
        in_specs=[row(512), row(512), row(W_G), row(D_MODEL), row(D_MODEL),
                  _const_spec((1, D_MODEL)), _const_spec((1, D_MODEL)),
                  _const_spec((512, D_MODEL)), _const_spec((512, D_MODEL)), _const_spec((D_MODEL, D_MODEL))],
        out_specs=[row(D_MODEL), row(512), row(512), row(W_G),
                   pl.BlockSpec((SUBLANES, tm), lambda i: (0, i)),
                   pl.BlockSpec((512, D_MODEL), lambda i: (0, 0)), pl.BlockSpec((512, D_MODEL), lambda i: (0, 0)),
                   pl.BlockSpec((D_MODEL, D_MODEL), lambda i: (0, 0)), pl.BlockSpec((SUBLANES, D_MODEL), lambda i: (0, 0))],
        scratch_shapes=[pltpu.VMEM((SUBLANES, D_MODEL), F32)] * 3,
        compiler_params=_params(("arbitrary",), VMEM_LIMIT),
    )(att_a, att_b, g, x, target, gate, g_final, wo_a, wo_b, w_out)


def _rope_bwd(dt, cos, sin, lane):
    u = dt * sin
    lo = (lane % HEAD_DIM) < (HEAD_DIM // 2)
    return dt * cos + jnp.where(lo, pltpu.roll(u, 96, 1), -pltpu.roll(u, 32, 1))


def _swa_bwd(a, datt, l_all, sinks, cos, sin):
    s = a.shape[0]
    nt = s // SWA_ROWS

    def body(sink_ref, a_ref, ap_ref, do_ref, l_ref, cos_ref, sin_ref, da_ref, ds_ref, halo):
        step = pl.program_id(0)
        tile = nt - 1 - step

        @pl.when(step == 0)
        def _():
            halo[...] = jnp.zeros_like(halo)
            ds_ref[...] = jnp.zeros_like(ds_ref)

        lane = lax.broadcasted_iota(jnp.int32, (WINDOW, LANES), 1)
        sub8 = lax.broadcasted_iota(jnp.int32, (SUBLANES, LANES), 0)
        lane8 = lax.broadcasted_iota(jnp.int32, (SUBLANES, LANES), 1)
        blocks = _swa_blocks(a_ref, ap_ref)
        dsink = jnp.zeros((SUBLANES, LANES), F32)

        def join(pair, r0):
            x0, x1 = pair[0][r0:r0 + WINDOW], pair[1][r0:r0 + WINDOW]
            return jnp.where(lane < 64, x0 + pltpu.roll(x0, 64, 1), x1 + pltpu.roll(x1, 64, 1))

        units = [(jb, g) for jb in range(SWA_BLOCKS) for g in range(2)]
        n_u = len(units)
        sinks_col = [_per_head_column([sink_ref[GROUP * g + hh] for hh in range(GROUP)]) for g in range(2)]
        bands = [_swa_band(blocks[jb + 1], blocks[jb], g, lane) for jb, g in units]
        qs = [_stack_heads(blocks[jb + 1], g, lane) for jb, g in units]
        doms = [_stack_heads(do_ref.at[pl.ds(WINDOW * jb, WINDOW), :], g, lane) for jb, g in units]
        lcols = []
        for jb, g in units:
            lv = l_ref[WINDOW * jb:WINDOW * (jb + 1), :]
            lcols.append(_per_head_column([lv[:, GROUP * g + hh:GROUP * g + hh + 1] for hh in range(GROUP)]))
        ps = [jnp.exp(_swa_logits(qs[u], bands[u][0], (tile > 0) if jb == 0 else True) - lcols[u])
              for u, (jb, g) in enumerate(units)]
        dps = [_dot(doms[u], bands[u][1], NT) for u in range(n_u)]
        deltas = [jnp.sum(ps[u] * dps[u], axis=-1, keepdims=True) for u in range(n_u)]
        for u, (jb, g) in enumerate(units):
            sink_term = jnp.exp(sinks_col[g] - lcols[u]) * deltas[u]
            for hh in range(GROUP):
                tot = jnp.sum(sink_term[WINDOW * hh:WINDOW * (hh + 1)])
                dsink = dsink + jnp.where((sub8 == 0) & (lane8 == GROUP * g + hh), -tot, 0.0)
        dss = [(ps[u] * (dps[u] - deltas[u])).astype(BF16) for u in range(n_u)]
        dqs = [_dot(dss[u], bands[u][0]) * SCALE for u in range(n_u)]
        dks = [_dot(dss[u], qs[u], TN) * SCALE for u in range(n_u)]
        dvs = [_dot(ps[u].astype(BF16), doms[u], TN) for u in range(n_u)]

        carry_k, carry_v = halo[:, 0:LANES], halo[:, LANES:2 * LANES]
        for jb in reversed(range(SWA_BLOCKS)):
            rows = slice(WINDOW * jb, WINDOW * (jb + 1))
            cosv, sinv = cos_ref[rows, :], sin_ref[rows, :]
            for g in range(2):
                dq = dqs[2 * jb + g]
                for pb in range(2):
                    r0 = 2 * pb * WINDOW
                    dq_pair = jnp.where(lane < 64, dq[r0:r0 + WINDOW], dq[r0 + WINDOW:r0 + 2 * WINDOW])
                    da_ref[rows, LANES * (2 * g + pb):LANES * (2 * g + pb + 1)] = _rope_bwd(
                        dq_pair, cosv, sinv, lane).astype(BF16)
            dkb, dvb = dks[2 * jb:2 * jb + 2], dvs[2 * jb:2 * jb + 2]
            da_ref[rows, 512:640] = _rope_bwd(join(dkb, WINDOW) + carry_k, cosv, sinv, lane).astype(BF16)
            da_ref[rows, 640:768] = (join(dvb, WINDOW) + carry_v).astype(BF16)
            carry_k, carry_v = join(dkb, 0), join(dvb, 0)
        halo[:, 0:LANES] = carry_k
        halo[:, LANES:2 * LANES] = carry_v
        ds_ref[...] += dsink

    rev = lambda w: pl.BlockSpec((SWA_ROWS, w), lambda i: (nt - 1 - i, 0))
    return pl.pallas_call(
        body, name="swa_bwd", grid=(nt,),
        out_shape=[jax.ShapeDtypeStruct((s, W_A), BF16), jax.ShapeDtypeStruct((SUBLANES, LANES), F32)],
        in_specs=[pl.BlockSpec(memory_space=pltpu.SMEM), rev(W_A),
                  pl.BlockSpec((WINDOW, W_A), lambda i: (jnp.maximum(SWA_BLOCKS * (nt - 1 - i) - 1, 0), 0)),
                  rev(512), rev(LANES), rev(LANES), rev(LANES)],
        out_specs=[rev(W_A), pl.BlockSpec((SUBLANES, LANES), lambda i: (0, 0))],
        scratch_shapes=[pltpu.VMEM((WINDOW, 2 * LANES), F32)],
        compiler_params=_params(("arbitrary",)),
    )(sinks, a, a, datt, l_all, cos, sin)


def _fox_bwd(qa, ka, b, do, lse, delta, ranges, t):
    s = qa.shape[0]
    nt = s // t

    def body(rg_ref, q_ref, do_ref, lse_ref, dl_ref, k_ref, v_ref, dq_ref, dk_ref, dv_ref, dc_ref, dr_ref, dq_acc):
        p = pl.program_id(0)
        j = pl.program_id(1)
        n_query = jnp.clip(_lane_scalar(rg_ref[0], 1, j), 1, nt - j)

        @pl.when(j == 0)
        def _():
            dq_acc[...] = jnp.zeros_like(dq_acc)

        lane = lax.broadcasted_iota(jnp.int32, (t, LANES), 1)
        rows = lax.broadcasted_iota(jnp.int32, (t, t), 0)
        cols = lax.broadcasted_iota(jnp.int32, (t, t), 1)
        kt = k_ref[...]
        vt = v_ref[...]

        ks = [kt[:, LANES * hh:LANES * (hh + 1)] for hh in range(2)]

        def tile(qis, carry, diagonal=False):
            dk0, dk1, dv = carry
            offs = [pl.multiple_of(i * t, t) for i in qis]
            units = [(u, hh) for u in range(len(qis)) for hh in range(2)]
            qts = [q_ref[pl.ds(off, t), :] for off in offs]
            dos = [do_ref[pl.ds(off, t), :] for off in offs]
            lses = [lse_ref[0, :, pl.ds(off, t)] for off in offs]
            dls = [dl_ref[0, :, pl.ds(off, t)] for off in offs]
            qs = [qts[u][:, LANES * hh:LANES * (hh + 1)] for u, hh in units]
            doms = [jnp.where((lane < 64) if hh == 0 else (lane >= 64), dos[u], jnp.zeros_like(dos[u])) for u, hh in units]
            sts = [_dot(ks[hh], qs[n], NT) for n, (u, hh) in enumerate(units)]
            dpts = [_dot(vt, doms[n], NT) for n in range(len(units))]
            if diagonal:
                sts = [jnp.where(cols >= rows, st, NEG) for st in sts]
            pts = [jnp.exp(sts[n] - lses[u][hh:hh + 1, :]) for n, (u, hh) in enumerate(units)]
            dsts = [(pts[n] * (dpts[n] - dls[u][hh:hh + 1, :])).astype(BF16) for n, (u, hh) in enumerate(units)]
            for n, (u, hh) in enumerate(units):
                dv = dv + _dot(pts[n].astype(BF16), doms[n])
                term = _dot(dsts[n], qs[n])
                dk0, dk1 = (dk0 + term, dk1) if hh == 0 else (dk0, dk1 + term)
                dq_acc[hh, pl.ds(offs[u], t), :] += _dot(dsts[n], ks[hh], TN)
            return dk0, dk1, dv

        zero = jnp.zeros((t, LANES), F32)
        carry = tile([j], (zero, zero, zero), diagonal=True)
        n_rest = n_query - 1
        carry = lax.fori_loop(0, n_rest // 2, lambda u, cr: tile([j + 1 + 2 * u, j + 2 + 2 * u], cr), carry)
        dk0, dk1, dv = lax.fori_loop(0, n_rest % 2, lambda u, cr: tile([j + n_rest], cr), carry)
        dk_ref[...] = jnp.where(lane < 64, dk0, pltpu.roll(dk1, 64, 1)).astype(BF16)
        dv_ref[...] = dv.astype(BF16)
        c0 = jnp.broadcast_to(dk0[:, 67:68], (t, LANES))
        c1 = jnp.broadcast_to(dk1[:, 67:68], (t, LANES))
        dc_ref[0] = jnp.where(lane == 2 * p, -c0, jnp.where(lane == 2 * p + 1, -c1, 0.0))

        @pl.when(j == nt - 1)
        def _():
            lane_s = lax.broadcasted_iota(jnp.int32, (s, LANES), 1)
            a0, a1 = dq_acc[0], dq_acc[1]
            dq_ref[...] = (jnp.where(lane_s < 64, a0, pltpu.roll(a1, 64, 1)) * SCALE).astype(BF16)
            r0 = jnp.broadcast_to(a0[:, 64:65], (s, LANES))
            r1 = jnp.broadcast_to(a1[:, 64:65], (s, LANES))
            dr_ref[0] = jnp.where(lane_s == 2 * p, r0, jnp.where(lane_s == 2 * p + 1, r1, 0.0))

    return pl.pallas_call(
        body, name="fox_bwd", grid=(4, nt),
        out_shape=[jax.ShapeDtypeStruct((s, 512), BF16), jax.ShapeDtypeStruct((s, 512), BF16),
                   jax.ShapeDtypeStruct((s, 512), BF16), jax.ShapeDtypeStruct((4, s, LANES), F32),
                   jax.ShapeDtypeStruct((4, s, LANES), F32)],
        in_specs=[pl.BlockSpec((1, SUBLANES, LANES), lambda p, j: (p, 0, 0)),
                  pl.BlockSpec((s, 2 * LANES), lambda p, j: (0, p)),
                  pl.BlockSpec((s, LANES), lambda p, j: (0, p)),
                  pl.BlockSpec((1, SUBLANES, s), lambda p, j: (p, 0, 0)),
                  pl.BlockSpec((1, SUBLANES, s), lambda p, j: (p, 0, 0)),
                  pl.BlockSpec((t, 2 * LANES), lambda p, j: (j, p)),
                  pl.BlockSpec((t, LANES), lambda p, j: (j, 8 + p))],
        out_specs=[pl.BlockSpec((s, LANES), lambda p, j: (0, p)),
                   pl.BlockSpec((t, LANES), lambda p, j: (j, p)),
                   pl.BlockSpec((t, LANES), lambda p, j: (j, p)),
                   pl.BlockSpec((1, t, LANES), lambda p, j: (p, j, 0)),
                   pl.BlockSpec((1, s, LANES), lambda p, j: (p, 0, 0))],
        scratch_shapes=[pltpu.VMEM((2, s, LANES), F32)],
        compiler_params=_params(("parallel", "arbitrary"), VMEM_LIMIT),
    )(ranges, qa, do, lse, delta, ka, b)


def _fox_cumsum_bwd(dcum_k, dcum_q, f, bf_pad, tb=512):
    s = f.shape[0]
    nb = s // tb

    def body(dc_ref, dr_ref, f_ref, b_ref, df_ref, db_ref, carry):
        step = pl.program_id(0)

        @pl.when(step == 0)
        def _():
            carry[...] = jnp.zeros_like(carry)
            db_ref[...] = jnp.zeros_like(db_ref)

        lane = lax.broadcasted_iota(jnp.int32, (tb, LANES), 1)
        dc = dc_ref[0] + dr_ref[0]
        for k in range(1, 4):
            dc = dc + (dc_ref[k] + dr_ref[k])
        hi, mid, lo = _split3(dc)
        rows = lax.broadcasted_iota(jnp.int32, (tb, tb), 0)
        cols = lax.broadcasted_iota(jnp.int32, (tb, tb), 1)
        triu = (cols >= rows).astype(BF16)
        dlogf = _dot(triu, hi) + _dot(triu, mid) + _dot(triu, lo) + carry[0:1, :]
        carry[...] = jnp.broadcast_to(dlogf[0:1, :], carry.shape)
        u = f_ref[...] + b_ref[...]
        dfb = jnp.where(lane < N_HEADS, dlogf * _sigmoid(-u), 0.0)
        df_ref[...] = dfb.astype(BF16)
        sub = lax.broadcasted_iota(jnp.int32, (SUBLANES, LANES), 0)
        db_ref[...] += jnp.where(sub == 0, jnp.sum(dfb, axis=0, keepdims=True), 0.0)

    return pl.pallas_call(
        body, name="fox_cumsum_bwd", grid=(nb,),
        out_shape=[jax.ShapeDtypeStruct((s, LANES), BF16), jax.ShapeDtypeStruct((SUBLANES, LANES), F32)],
        in_specs=[pl.BlockSpec((4, tb, LANES), lambda i: (0, nb - 1 - i, 0)),
                  pl.BlockSpec((4, tb, LANES), lambda i: (0, nb - 1 - i, 0)),
                  pl.BlockSpec((tb, LANES), lambda i: (nb - 1 - i, 0)), _const_spec((1, LANES))],
        out_specs=[pl.BlockSpec((tb, LANES), lambda i: (nb - 1 - i, 0)),
                   pl.BlockSpec((SUBLANES, LANES), lambda i: (0, 0))],
        scratch_shapes=[pltpu.VMEM((SUBLANES, LANES), F32)],
        compiler_params=_params(("arbitrary",)),
    )(dcum_k, dcum_q, f, bf_pad)


def _dh_norm_bwd(d_a, d_q, d_k, d_v, d_f, d_g, w_t, x, dx2, gnorm, scale1, tm=512):
    s = x.shape[0]
    nt = s // tm

    def body(da_ref, dq_ref, dk_ref, dv_ref, df_ref, dg_ref, w_ref, x_ref, dx2_ref, g_ref, sc_ref, gx_ref, vec_ref,
             a_sh, a_sc, a_g):
        step = pl.program_id(0)

        @pl.when(step == 0)
        def _():
            a_sh[...] = jnp.zeros_like(a_sh)
            a_sc[...] = jnp.zeros_like(a_sc)
            a_g[...] = jnp.zeros_like(a_g)

        def fold(v):
            return jnp.sum(v.reshape(tm // SUBLANES, SUBLANES, D_MODEL), axis=0)

        d_all = jnp.concatenate([da_ref[...], dq_ref[...], dk_ref[...], dv_ref[...], df_ref[...], dg_ref[...]], axis=1)
        dh = _dot(d_all, w_ref[...])
        xv = x_ref[...]
        r = lax.rsqrt(jnp.mean(xv * xv, axis=-1, keepdims=True) + NORM_EPS)
        xn = xv * r
        gn = g_ref[...]
        a_sh[...] += fold(dh)
        a_sc[...] += fold(dh * (xn * gn))
        dn1 = dh * sc_ref[...]
        a_g[...] += fold(dn1 * xn)
        dxn = dn1 * gn
        gx_ref[...] = dx2_ref[...] + r * (dxn - xn * jnp.mean(dxn * xn, axis=-1, keepdims=True))

        @pl.when(step == nt - 1)
        def _():
            sub = lax.broadcasted_iota(jnp.int32, (SUBLANES, D_MODEL), 0)
            v_sh = jnp.sum(a_sh[...], axis=0, keepdims=True)
            v_sc = jnp.sum(a_sc[...], axis=0, keepdims=True)
            v_g = jnp.sum(a_g[...], axis=0, keepdims=True)
            vec_ref[...] = jnp.where(sub == 0, v_sh, jnp.where(sub == 1, v_sc, jnp.where(sub == 2, v_g, 0.0)))

    row = lambda w: pl.BlockSpec((tm, w), lambda i: (i, 0))
    return pl.pallas_call(
        body, name="dh_norm_bwd", grid=(nt,),
        out_shape=[jax.ShapeDtypeStruct((s, D_MODEL), F32), jax.ShapeDtypeStruct((SUBLANES, D_MODEL), F32)],
        in_specs=[row(W_A), row(512), row(512), row(512), row(W_F), row(W_G), _const_spec((W_INT, D_MODEL)),
                  row(D_MODEL), row(D_MODEL), _const_spec((1, D_MODEL)), _const_spec((1, D_MODEL))],
        out_specs=[row(D_MODEL), pl.BlockSpec((SUBLANES, D_MODEL), lambda i: (0, 0))],
        scratch_shapes=[pltpu.VMEM((SUBLANES, D_MODEL), F32)] * 3,
        compiler_params=_params(("arbitrary",), VMEM_LIMIT),
    )(d_a, d_q, d_k, d_v, d_f, d_g, w_t, x, dx2, gnorm, scale1)


def _dw_in(h_t, d, name, tn, ts=1024):
    s, n = d.shape
    ns = s // ts

    def body(h_ref, d_ref, o_ref, acc):
        k = pl.program_id(1)

        @pl.when(k == 0)
        def _():
            acc[...] = jnp.zeros_like(acc)

        acc[...] += _dot(h_ref[...], d_ref[...])

        @pl.when(k == ns - 1)
        def _():
            o_ref[...] = acc[...]

    return pl.pallas_call(
        body, name=name, grid=(n // tn, ns),
        out_shape=jax.ShapeDtypeStruct((D_MODEL, n), F32),
        in_specs=[pl.BlockSpec((D_MODEL, ts), lambda jn, k: (0, k)), pl.BlockSpec((ts, tn), lambda jn, k: (k, jn))],
        out_specs=pl.BlockSpec((D_MODEL, tn), lambda jn, k: (0, jn)),
        scratch_shapes=[pltpu.VMEM((D_MODEL, tn), F32)],
        compiler_params=_params(("parallel", "arbitrary"), VMEM_LIMIT),
    )(h_t, d)


def _small_grads(packs, c_t, dada_shard):
    def body(p_ref, ct_ref, da_ref, sum_ref, gw_ref):
        acc = p_ref[0]
        for dev in range(1, 8):
            acc = acc + p_ref[dev]
        sum_ref[...] = acc
        gw_ref[...] = jnp.dot(ct_ref[...], da_ref[...], preferred_element_type=F32, precision=lax.Precision.HIGHEST)

    return pl.pallas_call(
        body, name="small_grads",
        out_shape=[jax.ShapeDtypeStruct(packs.shape[1:], F32),
                   jax.ShapeDtypeStruct((c_t.shape[0], dada_shard.shape[1]), F32)],
    )(packs, c_t, dada_shard)


def _adamw_body(w_ref, g_ref, m_ref, v_ref, d_ref, mo_ref, vo_ref):
    c1 = 1.0 / (1.0 - ADAM_B1 ** ADAM_STEP)
    c2 = 1.0 / (1.0 - ADAM_B2 ** ADAM_STEP)
    gv = g_ref[...]
    mn = ADAM_B1 * m_ref[...] + (1.0 - ADAM_B1) * gv
    vn = ADAM_B2 * v_ref[...] + (1.0 - ADAM_B2) * (gv * gv)
    mo_ref[...] = mn
    vo_ref[...] = vn
    d_ref[...] = -ADAM_LR * ((mn * c1) / (jnp.sqrt(vn * c2) + ADAM_EPS) + ADAM_WD * w_ref[...])


def _adamw3(w, g, m, v, name, tb=128):
    spec = pl.BlockSpec((tb, SUBLANES, LANES), lambda i: (i, 0, 0))
    return pl.pallas_call(
        functools.partial(_adamw_body), name=name, grid=(pl.cdiv(w.shape[0], tb),),
        out_shape=[jax.ShapeDtypeStruct(w.shape, F32)] * 3,
        in_specs=[spec] * 4, out_specs=[spec] * 3,
        compiler_params=_params(("parallel",)),
    )(w, g, m, v)


def _adamw(w, g, m, v, name):
    r, c = w.shape
    tr = 128 if r % 128 == 0 else r
    body = functools.partial(_adamw_body)
    spec = pl.BlockSpec((tr, c), lambda i: (i, 0))
    return pl.pallas_call(
        body, name=name, grid=(r // tr,),
        out_shape=[jax.ShapeDtypeStruct((r, c), F32)] * 3,
        in_specs=[spec] * 4, out_specs=[spec] * 3,
        compiler_params=_params(("parallel",)),
    )(w, g, m, v)


def _rope_inputs(positions):
    inv_freq = 10000.0 ** (-jnp.arange(0, HEAD_DIM, 2, dtype=F32) / HEAD_DIM)
    pos = jnp.broadcast_to(positions.astype(F32)[:, None], (positions.shape[0], LANES))
    return pos, jnp.tile(inv_freq, 4)[None, :]


def _pad_rows(v, rows=SUBLANES):
    return jnp.pad(v, ((0, rows - v.shape[0]), (0, 0)))


def kernel(x, c, positions, w_ada, b_ada, g_norm, w_in, b_f, sinks, w_o_swa, w_o_fox, w_out, g_final, loss_target, m_w_ada, m_b_ada, m_g_norm, m_w_in, m_b_f, m_sinks, m_w_o_swa, m_w_o_fox, m_w_out, m_g_final, v_w_ada, v_b_ada, v_g_norm, v_w_in, v_b_f, v_sinks, v_w_o_swa, v_w_o_fox, v_w_out, v_g_final):
    ix, iy, ic = lax.axis_index("x"), lax.axis_index("y"), lax.axis_index("c")
    chip = 2 * ix + iy
    dev = 2 * chip + ic
    xs, tgt = x[0], loss_target[0]
    s = xs.shape[0]

    b_ada_shard = lax.dynamic_slice(b_ada, (0, chip * 768), (1, 768))
    ada_parts, g_in, g_oa, g_ob, g_out = _gather_inputs(
        _pad_rows(c), w_ada[0], b_ada_shard, [w_in[0], w_o_swa[0], w_o_fox[0], w_out[0]], "gather_inputs")
    ada = lax.dynamic_index_in_dim(ada_parts, dev, axis=1, keepdims=False).reshape(1, 3 * D_MODEL)
    shift, scale, gate = ada[:, :D_MODEL], ada[:, D_MODEL:2 * D_MODEL], ada[:, 2 * D_MODEL:]
    scale1 = 1.0 + scale

    w_ref_order = jnp.transpose(g_in, (1, 0, 2)).reshape(D_MODEL, R_END)
    w_int = jnp.concatenate([
        w_ref_order[:, :R_ZA], w_ref_order[:, R_QB:R_FB], w_ref_order[:, R_FB:R_ZB],
        jnp.zeros((D_MODEL, W_F - N_HEADS), BF16), w_ref_order[:, R_ZA:R_QB], w_ref_order[:, R_ZB:]], axis=1)
    w_int_t = w_int.T
    wo_a = jnp.transpose(g_oa, (1, 0, 2)).reshape(512, D_MODEL)
    wo_b = jnp.transpose(g_ob, (1, 0, 2)).reshape(512, D_MODEL)
    w_o = g_out.reshape(D_MODEL, D_MODEL)

    pos, freq = _rope_inputs(positions[0])
    bf_pad = jnp.pad(b_f, ((0, 0), (0, LANES - N_HEADS)))
    sink_vec = sinks[0]

    a, b, f, g, h_t, cos, sin = _norm_proj(xs, g_norm * scale1, shift, w_int, pos, freq)
    att_a, l_swa = _swa_fwd(a, sink_vec)
    cum = _fox_cumsum(f, bf_pad)
    qa, ka, va, stats = _fox_prep(b, cum, FOX_TILE)
    ranges = _fox_tile_ranges(stats)
    att_b, lse = _fox_fwd(qa, ka, va, ranges, FOX_TILE)

    dx2, datt_a, datt_b, d_g, delta8, dwo_a, dwo_b, dw_out, vec_mid = _mid(
        att_a, att_b, g, xs, tgt, gate, g_final.reshape(1, D_MODEL), wo_a, wo_b, w_o)
    delta = jnp.pad(delta8.reshape(4, 2, s), ((0, 0), (0, SUBLANES - 2), (0, 0)))
    d_a, dsink = _swa_bwd(a, datt_a, l_swa, sink_vec, cos, sin)
    dq, dk, dv, dcum_k, dcum_q = _fox_bwd(qa, ka, b, datt_b, lse, delta, ranges, FOX_TILE)
    d_f, dbf = _fox_cumsum_bwd(dcum_k, dcum_q, f, bf_pad)
    grad_x, vec_dh = _dh_norm_bwd(d_a, dq, dk, dv, d_f, d_g, w_int_t, xs, dx2, g_norm, scale1)
    dw_a = _dw_in(h_t, d_a, "dw_in_a", 768)
    dw_q = _dw_in(h_t, dq, "dw_in_q", 512)
    dw_k = _dw_in(h_t, dk, "dw_in_k", 512)
    dw_v = _dw_in(h_t, dv, "dw_in_v", 512)
    dw_f = _dw_in(h_t, d_f, "dw_in_f", 128)
    dw_g = _dw_in(h_t, d_g, "dw_in_g", 1024)
    dw_in = jnp.concatenate([dw_a, dw_g[:, :512], dw_q, dw_k, dw_v, dw_f[:, :N_HEADS], dw_g[:, 512:]], axis=1)

    tail = jnp.pad(jnp.concatenate([dbf[0:1, :N_HEADS], dsink[0:1, :N_HEADS]], axis=1), ((0, 0), (0, D_MODEL - 2 * N_HEADS)))
    pack = jnp.concatenate([c, vec_dh[0:2], vec_mid[1:2], vec_dh[2:3], vec_mid[0:1], tail, vec_mid[2:3]], axis=0)

    def slots(w, axis):
        if axis == 1:
            return jnp.transpose(w.reshape(w.shape[0], 4, w.shape[1] // 4), (1, 0, 2))
        return w.reshape(4, w.shape[0] // 4, w.shape[1])

    packs, g_wo_a, g_wo_b, g_w_out, g_w_in = _reduce_scatter(
        [slots(dwo_a, 1), slots(dwo_b, 1), slots(dw_out, 0), slots(dw_in, 1)], pack, "reduce_grads")
    dada_all = packs[:, 1:4, :].reshape(8, 3 * D_MODEL)
    dada_shard = lax.dynamic_slice(dada_all, (0, chip * 768), (8, 768))
    sums, g_w_ada = _small_grads(packs, packs[:, 0, :].T, dada_shard)
    g_b_ada = sums[1:4].reshape(1, 3 * D_MODEL)
    g_g_norm = sums[4:5]
    g_g_final = sums[5]
    g_b_f = sums[6:7, :N_HEADS]
    g_sinks = sums[6:7, N_HEADS:2 * N_HEADS]
    loss = sums[7, 0]

    grads = {
        "w_ada": g_w_ada, "b_ada": g_b_ada, "g_norm": g_g_norm, "w_in": g_w_in, "b_f": g_b_f, "sinks": g_sinks,
        "w_o_swa": g_wo_a, "w_o_fox": g_wo_b, "w_out": g_w_out, "g_final": g_g_final,
    }
    params = {
        "w_ada": (w_ada, m_w_ada, v_w_ada), "b_ada": (b_ada, m_b_ada, v_b_ada), "g_norm": (g_norm, m_g_norm, v_g_norm),
        "w_in": (w_in, m_w_in, v_w_in), "b_f": (b_f, m_b_f, v_b_f), "sinks": (sinks, m_sinks, v_sinks),
        "w_o_swa": (w_o_swa, m_w_o_swa, v_w_o_swa), "w_o_fox": (w_o_fox, m_w_o_fox, v_w_o_fox),
        "w_out": (w_out, m_w_out, v_w_out), "g_final": (g_final, m_g_final, v_g_final),
    }
    n_col = w_in.shape[2]

    def as_stored(t):
        return jnp.transpose(t, (2, 0, 1)).reshape(n_col, SUBLANES, LANES)

    def from_stored(t):
        return jnp.transpose(t, (1, 2, 0)).reshape(1, D_MODEL, n_col)

    names = list(grads)
    out_g, out_d, out_m, out_v = [], [], [], []
    for nm in names:
        w, m, v = params[nm]
        if nm == "w_in":
            g_st = as_stored(grads[nm][None])
            d_, m_, v_ = _adamw3(as_stored(w), g_st, as_stored(m), as_stored(v), "adamw_" + nm)
            res = [from_stored(t) for t in (g_st, d_, m_, v_)]
        else:
            shape2 = (w.shape[-2], w.shape[-1]) if w.ndim >= 2 else (1, w.shape[0])
            d_, m_, v_ = _adamw(w.reshape(shape2), grads[nm].reshape(shape2), m.reshape(shape2), v.reshape(shape2), "adamw_" + nm)
            res = [t.reshape(w.shape) for t in (grads[nm], d_, m_, v_)]
        out_g.append(res[0])
        out_d.append(res[1])
        out_m.append(res[2])
        out_v.append(res[3])
    return (loss, grad_x[None], *out_g, *out_d, *out_m, *out_v)
```

```python
import functools

import numpy as np
import jax
import jax.numpy as jnp
from jax import lax
from jax.experimental import pallas as pl
from jax.experimental.pallas import tpu as pltpu

F32 = jnp.float32
BF16 = jnp.bfloat16
MESH = pl.DeviceIdType.MESH

D_MODEL = 1024
HEAD_DIM = 64
N_HEADS = 8
WINDOW = 128
NORM_EPS = 1e-6
SCALE = HEAD_DIM ** -0.5
NEG = -1e30
LANES = 128
SUBLANES = 8
VMEM_LIMIT = 60 * 1024 * 1024
FOX_TILE = 512

W_A, W_B, W_F, W_G = 768, 1536, 128, 3072
OFF_A, OFF_B, OFF_F, OFF_G = 0, 768, 2304, 2432
W_INT = W_A + W_B + W_F + W_G
R_ZA, R_QB, R_FB, R_ZB, R_END = 768, 1280, 2816, 2824, 5384

ADAM_LR, ADAM_B1, ADAM_B2, ADAM_EPS, ADAM_WD, ADAM_STEP = 0.001, 0.9, 0.999, 1e-08, 0.01, 10

NT = (((1,), (1,)), ((), ()))
TN = (((0,), (0,)), ((), ()))


def _dot(a, b, dims=None):
    if dims is None:
        return jnp.dot(a, b, preferred_element_type=F32)
    return lax.dot_general(a, b, dims, preferred_element_type=F32)


def _split3(v):
    hi = v.astype(BF16)
    r1 = v - hi.astype(F32)
    mid = r1.astype(BF16)
    lo = (r1 - mid.astype(F32)).astype(BF16)
    return hi, mid, lo


def _sigmoid(v):
    return 1.0 / (1.0 + jnp.exp(-v))


def _params(sem=None, vmem=None):
    return pltpu.CompilerParams(dimension_semantics=sem, vmem_limit_bytes=vmem)


def _const_spec(shape):
    nd = len(shape)
    return pl.BlockSpec(shape, lambda *_: (0,) * nd, pipeline_mode=pl.Buffered(1))


def _flip(v, f):
    return 1 - v if f else v


_CHIP_FLIPS = ((1, 0), (0, 1), (1, 1))


def _gather_inputs(c_pad, w_ada, b_ada_shard, shards, name):
    n = len(shards)
    n_col = w_ada.shape[1]

    def body(*refs):
        c_ref, wa_ref, ba_ref = refs[:3]
        ins = refs[3:3 + n]
        ada_ref = refs[3 + n]
        outs = refs[4 + n:4 + 2 * n]
        call_ref, send_sems, recv_sems = refs[4 + 2 * n:7 + 2 * n]
        x, y, c = lax.axis_index("x"), lax.axis_index("y"), lax.axis_index("c")
        k_me = 2 * x + y
        me = 2 * k_me + c
        sibling = (x, y, 1 - c)
        chips = [(_flip(x, fx), _flip(y, fy)) for fx, fy in _CHIP_FLIPS]

        def piece(i, chip_k, half):
            hr = ins[i].shape[0] // 2
            return outs[i].at[chip_k, pl.ds(half * hr, hr), :]

        def copy(i, slot, chip_k, half, to):
            return pltpu.make_async_remote_copy(
                src_ref=piece(i, chip_k, half), dst_ref=piece(i, chip_k, half),
                send_sem=send_sems.at[6 * i + slot], recv_sem=recv_sems.at[6 * i + slot],
                device_id=to, device_id_type=MESH)

        def small(ref, slot, sem, to):
            return pltpu.make_async_remote_copy(
                src_ref=ref.at[slot], dst_ref=ref.at[slot], send_sem=send_sems.at[6 * n + sem],
                recv_sem=recv_sems.at[6 * n + sem], device_id=to, device_id_type=MESH)

        for i in range(n):
            outs[i][k_me] = ins[i][...].astype(BF16)
        started = []
        for i in range(n):
            for j, chip in enumerate(chips):
                cp = copy(i, j, k_me, c, (chip[0], chip[1], c))
                cp.start()
                started.append(cp)

        call_ref[me] = c_ref[...]
        peers = [(_flip(x, k & 4), _flip(y, k & 2), _flip(c, k & 1)) for k in range(1, 8)]
        for k, peer in enumerate(peers):
            cp = small(call_ref, me, k, peer)
            cp.start()
            started.append(cp)
        for k, peer in enumerate(peers):
            small(call_ref, 4 * peer[0] + 2 * peer[1] + peer[2], k, peer).wait_recv()
        c_all = call_ref[:, 0, :].astype(BF16)
        ada_ref[k_me] = _dot(c_all, wa_ref[...].astype(BF16)) + ba_ref[...]
        for j, chip in enumerate(chips):
            cp = small(ada_ref, k_me, 7 + j, (chip[0], chip[1], c))
            cp.start()
            started.append(cp)

        for j, chip in enumerate(chips):
            chip_k = 2 * chip[0] + chip[1]
            for i in range(n):
                copy(i, j, chip_k, c, (chip[0], chip[1], c)).wait_recv()
                cp = copy(i, 3 + j, chip_k, c, sibling)
                cp.start()
                started.append(cp)
        for j, chip in enumerate(chips):
            chip_k = 2 * chip[0] + chip[1]
            small(ada_ref, chip_k, 7 + j, (chip[0], chip[1], c)).wait_recv()
            for i in range(n):
                copy(i, 3 + j, chip_k, 1 - c, sibling).wait_recv()
        for cp in started:
            cp.wait_send()

    vmem = pl.BlockSpec(memory_space=pltpu.VMEM)
    return pl.pallas_call(
        body, name=name,
        out_shape=[jax.ShapeDtypeStruct((4, 8, n_col), F32)] + [jax.ShapeDtypeStruct((4,) + s.shape, BF16) for s in shards],
        in_specs=[vmem] * (3 + n),
        out_specs=[vmem] * (1 + n),
        scratch_shapes=[pltpu.VMEM((8,) + c_pad.shape, F32),
                        pltpu.SemaphoreType.DMA((6 * n + 10,)), pltpu.SemaphoreType.DMA((6 * n + 10,))],
        compiler_params=_params(vmem=VMEM_LIMIT),
    )(c_pad, w_ada, b_ada_shard, *shards)


def _reduce_scatter(pieces, pack, name):
    n = len(pieces)

    def body(*refs):
        pack_ref, ins = refs[0], refs[1:1 + n]
        packs_ref, outs = refs[1 + n], refs[2 + n:2 + 2 * n]
        rest = refs[2 + 2 * n:]
        own, got = rest[:n], rest[n:2 * n]
        sendb, recvb = rest[2 * n:3 * n], rest[3 * n:4 * n]
        send_sems, recv_sems, local_sems = rest[4 * n:4 * n + 3]
        x, y, c = lax.axis_index("x"), lax.axis_index("y"), lax.axis_index("c")
        k_me = 2 * x + y
        me = 2 * k_me + c
        sibling = (x, y, 1 - c)
        chips = [(_flip(x, fx), _flip(y, fy)) for fx, fy in _CHIP_FLIPS]
        hrs = [p.shape[1] // 2 for p in pieces]

        def remote(i, slot, src, dst, to):
            return pltpu.make_async_remote_copy(
                src_ref=src, dst_ref=dst, send_sem=send_sems.at[5 * i + slot], recv_sem=recv_sems.at[5 * i + slot],
                device_id=to, device_id_type=MESH)

        started = []
        packs_ref[me] = pack_ref[...]
        peers = [(_flip(x, k & 4), _flip(y, k & 2), _flip(c, k & 1)) for k in range(1, 8)]
        for k, peer in enumerate(peers):
            cp = pltpu.make_async_remote_copy(
                src_ref=pack_ref, dst_ref=packs_ref.at[me], send_sem=send_sems.at[5 * n + k],
                recv_sem=recv_sems.at[5 * n + k], device_id=peer, device_id_type=MESH)
            cp.start()
            started.append(cp)
        loads = []
        for i in range(n):
            ld = pltpu.make_async_copy(ins[i].at[:, pl.ds(c * hrs[i], hrs[i]), :], own[i], local_sems.at[i])
            ld.start()
            loads.append(ld)
            cp = remote(i, 0, ins[i].at[:, pl.ds((1 - c) * hrs[i], hrs[i]), :], got[i], sibling)
            cp.start()
            started.append(cp)
        for i in range(n):
            loads[i].wait()
            remote(i, 0, ins[i].at[:, pl.ds(c * hrs[i], hrs[i]), :], got[i], sibling).wait_recv()
            for j, chip in enumerate(chips):
                chip_k = 2 * chip[0] + chip[1]
                sendb[i][j] = (own[i][chip_k] + got[i][chip_k]).astype(BF16)
                cp = remote(i, 1 + j, sendb[i].at[j], recvb[i].at[j], (chip[0], chip[1], c))
                cp.start()
                started.append(cp)
        for i in range(n):
            acc = own[i][k_me] + got[i][k_me]
            for j, chip in enumerate(chips):
                remote(i, 1 + j, sendb[i].at[j], recvb[i].at[j], (chip[0], chip[1], c)).wait_recv()
                acc = acc + recvb[i][j].astype(F32)
            mine = outs[i].at[pl.ds(c * hrs[i], hrs[i]), :]
            outs[i][pl.ds(pl.multiple_of(c * hrs[i], SUBLANES), hrs[i]), :] = acc
            cp = remote(i, 4, mine, mine, sibling)
            cp.start()
            started.append(cp)
        for i in range(n):
            theirs = outs[i].at[pl.ds((1 - c) * hrs[i], hrs[i]), :]
            remote(i, 4, theirs, theirs, sibling).wait_recv()
        for k, peer in enumerate(peers):
            pltpu.make_async_remote_copy(
                src_ref=pack_ref, dst_ref=packs_ref.at[4 * peer[0] + 2 * peer[1] + peer[2]],
                send_sem=send_sems.at[5 * n + k], recv_sem=recv_sems.at[5 * n + k],
                device_id=peer, device_id_type=MESH).wait_recv()
        for cp in started:
            cp.wait_send()

    vmem = pl.BlockSpec(memory_space=pltpu.VMEM)
    scratch = []
    scratch += [pltpu.VMEM((4, p.shape[1] // 2, p.shape[2]), F32) for p in pieces]
    scratch += [pltpu.VMEM((4, p.shape[1] // 2, p.shape[2]), F32) for p in pieces]
    scratch += [pltpu.VMEM((3, p.shape[1] // 2, p.shape[2]), BF16) for p in pieces]
    scratch += [pltpu.VMEM((3, p.shape[1] // 2, p.shape[2]), BF16) for p in pieces]
    scratch += [pltpu.SemaphoreType.DMA((5 * n + 7,)), pltpu.SemaphoreType.DMA((5 * n + 7,)), pltpu.SemaphoreType.DMA((n,))]
    return pl.pallas_call(
        body, name=name,
        out_shape=[jax.ShapeDtypeStruct((8,) + pack.shape, F32)] + [jax.ShapeDtypeStruct(p.shape[1:], F32) for p in pieces],
        in_specs=[vmem] + [pl.BlockSpec(memory_space=pl.ANY)] * n,
        out_specs=[vmem] * (1 + n),
        scratch_shapes=scratch,
        compiler_params=_params(vmem=VMEM_LIMIT),
    )(pack, *pieces)


def _rope_fwd(t, cos, sin, lane):
    lo = (lane % HEAD_DIM) < (HEAD_DIM // 2)
    return t * cos + jnp.where(lo, -pltpu.roll(t, 96, 1), pltpu.roll(t, 32, 1)) * sin


def _norm_proj(x, gmod, shift, w_int, pos, freq, tm=512):
    s = x.shape[0]

    def body(x_ref, g_ref, sh_ref, w_ref, pos_ref, fr_ref, a_ref, b_ref, f_ref, gg_ref, ht_ref, cos_ref, sin_ref):
        xv = x_ref[...]
        r = lax.rsqrt(jnp.mean(xv * xv, axis=-1, keepdims=True) + NORM_EPS)
        hf = (xv * r) * g_ref[...] + sh_ref[...]
        hb = hf.astype(BF16)
        ht_ref[...] = hf.T.astype(BF16)
        pa = _dot(hb, w_ref[:, OFF_A:OFF_A + W_A])
        ang = pos_ref[...] * fr_ref[...]
        cosv, sinv = jnp.cos(ang), jnp.sin(ang)
        cos_ref[...] = cosv
        sin_ref[...] = sinv
        lane = lax.broadcasted_iota(jnp.int32, (tm, LANES), 1)
        for j in range(5):
            t = pa[:, LANES * j:LANES * (j + 1)]
            a_ref[:, LANES * j:LANES * (j + 1)] = _rope_fwd(t, cosv, sinv, lane).astype(BF16)
        a_ref[:, 640:768] = pa[:, 640:768].astype(BF16)
        b_ref[...] = _dot(hb, w_ref[:, OFF_B:OFF_B + W_B]).astype(BF16)
        f_ref[...] = _dot(hb, w_ref[:, OFF_F:OFF_F + W_F])
        gg_ref[...] = _dot(hb, w_ref[:, OFF_G:OFF_G + W_G]).astype(BF16)

    row = lambda w: pl.BlockSpec((tm, w), lambda i: (i, 0))
    return pl.pallas_call(
        body, name="norm_proj", grid=(s // tm,),
        out_shape=[jax.ShapeDtypeStruct((s, W_A), BF16), jax.ShapeDtypeStruct((s, W_B), BF16),
                   jax.ShapeDtypeStruct((s, W_F), F32), jax.ShapeDtypeStruct((s, W_G), BF16),
                   jax.ShapeDtypeStruct((D_MODEL, s), BF16),
                   jax.ShapeDtypeStruct((s, LANES), F32), jax.ShapeDtypeStruct((s, LANES), F32)],
        in_specs=[row(D_MODEL), _const_spec((1, D_MODEL)), _const_spec((1, D_MODEL)), _const_spec((D_MODEL, W_INT)),
                  row(LANES), _const_spec((1, LANES))],
        out_specs=[row(W_A), row(W_B), row(W_F), row(W_G), pl.BlockSpec((D_MODEL, tm), lambda i: (0, i)),
                   row(LANES), row(LANES)],
        compiler_params=_params(("parallel",), VMEM_LIMIT),
    )(x, gmod, shift, w_int, pos, freq)


def _log_sigmoid(u):
    return jnp.minimum(u, 0.0) - jnp.log(1.0 + jnp.exp(-jnp.abs(u)))


def _fox_cumsum(f, bf_pad, tb=512):
    s = f.shape[0]

    def body(f_ref, b_ref, cum_ref, carry):
        @pl.when(pl.program_id(0) == 0)
        def _():
            carry[...] = jnp.zeros_like(carry)

        lane = lax.broadcasted_iota(jnp.int32, (tb, LANES), 1)
        logf = jnp.where(lane < N_HEADS, _log_sigmoid(f_ref[...] + b_ref[...]), 0.0)
        hi, mid, lo = _split3(logf)
        rows = lax.broadcasted_iota(jnp.int32, (tb, tb), 0)
        cols = lax.broadcasted_iota(jnp.int32, (tb, tb), 1)
        tril = (cols <= rows).astype(BF16)
        cum = _dot(tril, hi) + _dot(tril, mid) + _dot(tril, lo) + carry[0:1, :]
        cum_ref[...] = cum
        carry[...] = jnp.broadcast_to(cum[tb - 1:tb, :], carry.shape)

    return pl.pallas_call(
        body, name="fox_cumsum", grid=(s // tb,),
        out_shape=jax.ShapeDtypeStruct((s, LANES), F32),
        in_specs=[pl.BlockSpec((tb, LANES), lambda i: (i, 0)), _const_spec((1, LANES))],
        out_specs=pl.BlockSpec((tb, LANES), lambda i: (i, 0)),
        scratch_shapes=[pltpu.VMEM((SUBLANES, LANES), F32)],
        compiler_params=_params(("arbitrary",)),
    )(f, bf_pad)


def _aug_lane(h):
    return 64 if h % 2 == 0 else 0


def _fox_prep(b, cum, t):
    s = b.shape[0]

    def body(b_ref, cum_ref, q_ref, k_ref, v_ref, st_ref):
        lane = lax.broadcasted_iota(jnp.int32, (t, LANES), 1)
        lane_b = lane.astype(BF16)
        sub8 = lax.broadcasted_iota(jnp.int32, (SUBLANES, LANES), 0)
        lane8 = lax.broadcasted_iota(jnp.int32, (SUBLANES, LANES), 1)
        cumv = cum_ref[...]
        one = jnp.ones((t, LANES), BF16)
        zero = jnp.zeros((t, LANES), BF16)
        stats = jnp.zeros((SUBLANES, LANES), F32)
        for p in range(4):
            qblk = b_ref[:, LANES * p:LANES * (p + 1)] * SCALE
            kblk = b_ref[:, 512 + LANES * p:512 + LANES * (p + 1)]
            vblk = b_ref[:, 1024 + LANES * p:1024 + LANES * (p + 1)]
            qf, kf = qblk.astype(F32), kblk.astype(F32)
            q2, k2, qk = qf * qf, kf * kf, qf * kf
            for odd in range(2):
                h = 2 * p + odd
                a0 = _aug_lane(h)
                data_b = (lane_b < 64) if odd == 0 else (lane_b >= 64)
                data = (lane < 64) if odd == 0 else (lane >= 64)
                hi, mid, lo = _split3(jnp.broadcast_to(cumv[:, h:h + 1], (t, LANES)))
                ones3_q = (lane_b >= a0 + 3) & (lane_b < a0 + 6)
                ones3_k = (lane_b >= a0) & (lane_b < a0 + 3)
                aug_q = jnp.where(lane_b == a0, hi, jnp.where(lane_b == a0 + 1, mid, jnp.where(
                    lane_b == a0 + 2, lo, jnp.where(ones3_q, one, zero))))
                aug_k = jnp.where(ones3_k, one, jnp.where(lane_b == a0 + 3, -hi, jnp.where(
                    lane_b == a0 + 4, -mid, jnp.where(lane_b == a0 + 5, -lo, zero))))
                q_ref[:, LANES * h:LANES * (h + 1)] = jnp.where(data_b, qblk, aug_q)
                k_ref[:, LANES * h:LANES * (h + 1)] = jnp.where(data_b, kblk, aug_k)
                v_ref[:, LANES * h:LANES * (h + 1)] = jnp.where(data_b, vblk, jnp.where(lane_b == a0, one, zero))
                qn = jnp.sqrt(jnp.max(jnp.sum(jnp.where(data, q2, 0.0), axis=-1, keepdims=True)))
                kn = jnp.sqrt(jnp.max(jnp.sum(jnp.where(data, k2, 0.0), axis=-1, keepdims=True)))
                dmin = jnp.min(jnp.sum(jnp.where(data, qk, 0.0), axis=-1, keepdims=True))
                c_first, c_last = cumv[0:1, h:h + 1], cumv[t - 1:t, h:h + 1]
                row =jnp.where(lane8 == 0, qn, jnp.where(lane8 == 1, kn, jnp.where(
                    lane8 == 2, c_first, jnp.where(lane8 == 3, c_last, jnp.where(lane8 == 4, dmin, 0.0)))))
                stats = jnp.where(sub8 == h, row, stats)
        st_ref[0] = stats

    wide = pl.BlockSpec((t, 1024), lambda i: (i, 0))
    return pl.pallas_call(
        body, name="fox_prep", grid=(s // t,),
        out_shape=[jax.ShapeDtypeStruct((s, 1024), BF16)] * 3 + [jax.ShapeDtypeStruct((s // t, SUBLANES, LANES), F32)],
        in_specs=[pl.BlockSpec((t, W_B), lambda i: (i, 0)), pl.BlockSpec((t, LANES), lambda i: (i, 0))],
        out_specs=[wide, wide, wide, pl.BlockSpec((1, SUBLANES, LANES), lambda i: (i, 0, 0))],
        compiler_params=_params(("parallel",)),
    )(b, cum)


PRUNE_MARGIN = 90.0


def _fox_tile_ranges(stats):
    nt = stats.shape[0]
    qk_max = 1.01 * jnp.max(stats[:, :, 0], axis=0) * jnp.max(stats[:, :, 1], axis=0)
    ball = qk_max - jnp.minimum(jnp.min(stats[:, :, 4], axis=0), 0.0) + 0.05
    d = ball[None, None, :] + stats[:, None, :, 2] - stats[None, :, :, 3]
    idx = jnp.arange(nt)
    skip = (d <= -PRUNE_MARGIN) & (idx[None, :, None] < idx[:, None, None])
    skip = skip.reshape(nt, nt, 4, 2).all(axis=-1)
    first_key = jnp.sum(skip, axis=1).astype(F32)
    needed = (~skip) & (idx[None, :, None] <= idx[:, None, None])
    n_query = jnp.sum(needed, axis=0).astype(F32)
    table = jnp.zeros((4, SUBLANES, LANES), F32)
    table = table.at[:, 0, :nt].set(first_key.T)
    table = table.at[:, 1, :nt].set(n_query.T)
    return table


def _lane_scalar(block, row, lane_idx):
    sub8 = lax.broadcasted_iota(jnp.int32, (SUBLANES, LANES), 0)
    lane8 = lax.broadcasted_iota(jnp.int32, (SUBLANES, LANES), 1)
    return jnp.sum(jnp.where((sub8 == row) & (lane8 == lane_idx), block, 0.0)).astype(jnp.int32)


def _fox_fwd(qa, ka, va, ranges, t):
    s = qa.shape[0]
    nt = s // t
    nc = t // LANES

    def body(rg_ref, q_ref, k_ref, v_ref, o_ref, lse_ref):
        i = pl.program_id(1)
        lane = lax.broadcasted_iota(jnp.int32, (t, LANES), 1)
        rows = lax.broadcasted_iota(jnp.int32, (t, t), 0)
        cols = lax.broadcasted_iota(jnp.int32, (t, t), 1)
        first = jnp.clip(_lane_scalar(rg_ref[0], 0, i), 0, i)

        heads = range(2)

        def update(js, carry, diagonal=False):
            offs = [pl.multiple_of(j * t, t) for j in js]
            kts = [k_ref[pl.ds(off, t), :] for off in offs]
            vts = [v_ref[pl.ds(off, t), :] for off in offs]
            scs = [[_dot(q_ref[:, LANES * hh:LANES * (hh + 1)], kt[:, LANES * hh:LANES * (hh + 1)], NT) for kt in kts]
                   for hh in heads]
            if diagonal:
                scs = [[jnp.where(cols <= rows, sc, NEG) for sc in per_head] for per_head in scs]
            m_new = []
            for hh in heads:
                part = None
                for sc in scs[hh]:
                    for cch in range(nc):
                        chunk = sc[:, LANES * cch:LANES * (cch + 1)]
                        part = chunk if part is None else jnp.maximum(part, chunk)
                m_new.append(jnp.maximum(carry[2 * hh], jnp.max(part, axis=-1, keepdims=True)))
            alphas = [jnp.exp(carry[2 * hh] - m_new[hh]) for hh in heads]
            ps = [[jnp.exp(sc - m_new[hh]).astype(BF16) for sc in scs[hh]] for hh in heads]
            out = []
            for hh in heads:
                pv = None
                for p, vt in zip(ps[hh], vts):
                    term = _dot(p, vt[:, LANES * hh:LANES * (hh + 1)])
                    pv = term if pv is None else pv + term
                out += [m_new[hh], alphas[hh] * carry[2 * hh + 1] + pv]
            return tuple(out)

        col0 = jnp.full((t, 1), NEG, F32)
        zero = jnp.zeros((t, LANES), F32)
        n_off = i - first
        carry = lax.fori_loop(0, n_off // 2, lambda u, cr: update([first + 2 * u, first + 2 * u + 1], cr),
                              (col0, zero, col0, zero))
        carry = lax.fori_loop(0, n_off % 2, lambda u, cr: update([i - 1], cr), carry)
        m0, acc0, m1, acc1 = update([i], carry, diagonal=True)
        l0, l1 = acc0[:, _aug_lane(0):_aug_lane(0) + 1], acc1[:, _aug_lane(1):_aug_lane(1) + 1]
        o_ref[...] = jnp.where(lane < 64, acc0 * (1.0 / l0), acc1 * (1.0 / l1)).astype(BF16)
        sub = lax.broadcasted_iota(jnp.int32, (SUBLANES, t), 0)
        lse0 = jnp.broadcast_to(m0 + jnp.log(l0), (t, LANES)).T[0:SUBLANES, :]
        lse1 = jnp.broadcast_to(m1 + jnp.log(l1), (t, LANES)).T[0:SUBLANES, :]
        lse_ref[0] = jnp.where(sub == 0, lse0, jnp.where(sub == 1, lse1, 0.0))

    pair = pl.BlockSpec((s, 2 * LANES), lambda p, i: (0, p))
    return pl.pallas_call(
        body, name="fox_fwd", grid=(4, nt),
        out_shape=[jax.ShapeDtypeStruct((s, 512), BF16), jax.ShapeDtypeStruct((4, SUBLANES, s), F32)],
        in_specs=[pl.BlockSpec((1, SUBLANES, LANES), lambda p, i: (p, 0, 0)),
                  pl.BlockSpec((t, 2 * LANES), lambda p, i: (i, p)), pair, pair],
        out_specs=[pl.BlockSpec((t, LANES), lambda p, i: (i, p)),
                   pl.BlockSpec((1, SUBLANES, t), lambda p, i: (p, 0, i))],
        compiler_params=_params(("parallel", "arbitrary"), VMEM_LIMIT),
    )(ranges, qa, ka, va)


def _dup_halves(blk, lane):
    f = blk.astype(F32)
    r = pltpu.roll(f, 64, 1)
    return jnp.where(lane < 64, f, r).astype(BF16), jnp.where(lane >= 64, f, r).astype(BF16)


GROUP = 4
GROUP_ROWS = GROUP * WINDOW


def _stack_heads(ref, g, lane):
    parts = []
    for pb in (2 * g, 2 * g + 1):
        blk = ref[:, LANES * pb:LANES * (pb + 1)]
        zero = jnp.zeros_like(blk)
        parts += [jnp.where(lane < 64, blk, zero), jnp.where(lane >= 64, blk, zero)]
    return jnp.concatenate(parts, axis=0)


def _swa_band(a_ref, ap_ref, g, lane):
    k = jnp.concatenate([_dup_halves(ap_ref[:, 512:640], lane)[g], _dup_halves(a_ref[:, 512:640], lane)[g]], axis=0)
    v = jnp.concatenate([_dup_halves(ap_ref[:, 640:768], lane)[g], _dup_halves(a_ref[:, 640:768], lane)[g]], axis=0)
    return k, v


def _swa_logits(q, k, has_prev):
    sc = _dot(q, k, NT) * SCALE
    rr = lax.broadcasted_iota(jnp.int32, sc.shape, 0) % WINDOW
    cc = lax.broadcasted_iota(jnp.int32, sc.shape, 1)
    valid = (cc > rr) & (cc <= rr + WINDOW) & (has_prev | (cc >= WINDOW))
    return jnp.where(valid, sc, NEG)


def _per_head_column(values):
    return jnp.concatenate([jnp.broadcast_to(v, (WINDOW, 1)) for v in values], axis=0)


SWA_BLOCKS = 4
SWA_ROWS = SWA_BLOCKS * WINDOW


def _swa_blocks(a_ref, ap_ref):
    return [ap_ref] + [a_ref.at[pl.ds(WINDOW * jb, WINDOW), :] for jb in range(SWA_BLOCKS)]


def _swa_fwd(a, sinks):
    s = a.shape[0]

    def body(sink_ref, a_ref, ap_ref, o_ref, l_ref):
        lane = lax.broadcasted_iota(jnp.int32, (WINDOW, LANES), 1)
        blocks = _swa_blocks(a_ref, ap_ref)
        units = [(jb, g) for jb in range(SWA_BLOCKS) for g in range(2)]
        sinks_col = [_per_head_column([sink_ref[GROUP * g + hh] for hh in range(GROUP)]) for g in range(2)]
        bands = [_swa_band(blocks[jb + 1], blocks[jb], g, lane) for jb, g in units]
        scs = [_swa_logits(_stack_heads(blocks[jb + 1], g, lane), bands[u][0],
                           (pl.program_id(0) > 0) if jb == 0 else True) for u, (jb, g) in enumerate(units)]
        ms = [jnp.maximum(jnp.max(scs[u], axis=-1, keepdims=True), sinks_col[g]) for u, (jb, g) in enumerate(units)]
        ps = [jnp.exp(scs[u] - ms[u]) for u in range(len(units))]
        dens = [jnp.sum(ps[u], axis=-1, keepdims=True) + jnp.exp(sinks_col[g] - ms[u]) for u, (jb, g) in enumerate(units)]
        outs = [_dot((ps[u] * (1.0 / dens[u])).astype(BF16), bands[u][1]) for u in range(len(units))]
        for jb in range(SWA_BLOCKS):
            rows = slice(WINDOW * jb, WINDOW * (jb + 1))
            l_all = jnp.zeros((WINDOW, LANES), F32)
            for g in range(2):
                u = 2 * jb + g
                lcol = ms[u] + jnp.log(dens[u])
                for pb in range(2):
                    r0 = 2 * pb * WINDOW
                    o_ref[rows, LANES * (2 * g + pb):LANES * (2 * g + pb + 1)] = jnp.where(
                        lane < 64, outs[u][r0:r0 + WINDOW], outs[u][r0 + WINDOW:r0 + 2 * WINDOW]).astype(BF16)
                for hh in range(GROUP):
                    l_all = jnp.where(lane == GROUP * g + hh, lcol[WINDOW * hh:WINDOW * (hh + 1)], l_all)
            l_ref[rows, :] = l_all

    return pl.pallas_call(
        body, name="swa_fwd", grid=(s // SWA_ROWS,),
        out_shape=[jax.ShapeDtypeStruct((s, 512), BF16), jax.ShapeDtypeStruct((s, LANES), F32)],
        in_specs=[pl.BlockSpec(memory_space=pltpu.SMEM),
                  pl.BlockSpec((SWA_ROWS, W_A), lambda i: (i, 0)),
                  pl.BlockSpec((WINDOW, W_A), lambda i: (jnp.maximum(SWA_BLOCKS * i - 1, 0), 0))],
        out_specs=[pl.BlockSpec((SWA_ROWS, 512), lambda i: (i, 0)), pl.BlockSpec((SWA_ROWS, LANES), lambda i: (i, 0))],
        compiler_params=_params(("parallel",)),
    )(sinks, a, a)


def _mid(att_a, att_b, g, x, target, gate, g_final, wo_a, wo_b, w_out, tm=256):
    s = x.shape[0]
    nt = s // tm

    def body(aa_ref, ab_ref, g_ref, x_ref, t_ref, gate_ref, gf_ref, woa_ref, wob_ref, wout_ref,
             dx_ref, daa_ref, dab_ref, dg_ref, delta_ref, dwoa_ref, dwob_ref, dwout_ref, vec_ref,
             acc_gf, acc_gate, acc_loss):
        step = pl.program_id(0)

        @pl.when(step == 0)
        def _():
            dwoa_ref[...] = jnp.zeros_like(dwoa_ref)
            dwob_ref[...] = jnp.zeros_like(dwob_ref)
            dwout_ref[...] = jnp.zeros_like(dwout_ref)
            acc_gf[...] = jnp.zeros_like(acc_gf)
            acc_gate[...] = jnp.zeros_like(acc_gate)
            acc_loss[...] = jnp.zeros_like(acc_loss)

        def fold(v):
            return jnp.sum(v.reshape(tm // SUBLANES, SUBLANES, D_MODEL), axis=0)

        gate = gate_ref[...]
        gfin = gf_ref[...]
        branches = []
        for att_ref, z_off, wo_ref in ((aa_ref, 0, woa_ref), (ab_ref, 512, wob_ref)):
            att = att_ref[...].astype(F32)
            z = g_ref[:, z_off:z_off + 512].astype(F32)
            sz = _sigmoid(z)
            silu = z * sz
            u = (att * silu).astype(BF16)
            branches.append((att, z, sz, silu, u, _dot(u, wo_ref[...])))
        ga = g_ref[:, 1024:2048].astype(F32)
        gb = g_ref[:, 2048:3072].astype(F32)
        sga, sgb = _sigmoid(ga), _sigmoid(gb)
        y_a, y_b = branches[0][5], branches[1][5]
        mb = (sga * y_a + sgb * y_b).astype(BF16)
        o = _dot(mb, wout_ref[...])
        x2 = x_ref[...] + gate * o
        r2 = lax.rsqrt(jnp.mean(x2 * x2, axis=-1, keepdims=True) + NORM_EPS)
        xn2 = x2 * r2
        err = xn2 * gfin - t_ref[...]
        acc_loss[...] += fold(err * err)
        dy = err * (1.0 / D_MODEL)
        acc_gf[...] += fold(dy * xn2)
        dxn = dy * gfin
        dx2 = r2 * (dxn - xn2 * jnp.mean(dxn * xn2, axis=-1, keepdims=True))
        dx_ref[...] = dx2
        acc_gate[...] += fold(dx2 * o)
        d_o = (dx2 * gate).astype(BF16)
        dwout_ref[...] += _dot(mb, d_o, TN)
        dm = _dot(d_o, wout_ref[...], NT)
        dg_ref[:, 1024:2048] = (dm * y_a * sga * (1.0 - sga)).astype(BF16)
        dg_ref[:, 2048:3072] = (dm * y_b * sgb * (1.0 - sgb)).astype(BF16)
        for (att, z, sz, silu, u, _), sg, wo_ref, dwo_ref, datt_ref, z_off in (
                (branches[0], sga, woa_ref, dwoa_ref, daa_ref, 0), (branches[1], sgb, wob_ref, dwob_ref, dab_ref, 512)):
            dyb = (dm * sg).astype(BF16)
            dwo_ref[...] += _dot(u, dyb, TN)
            du = _dot(dyb, wo_ref[...], NT)
            datt = du * silu
            datt_ref[...] = datt.astype(BF16)
            dg_ref[:, z_off:z_off + 512] = (du * att * (sz * (1.0 + z * (1.0 - sz)))).astype(BF16)
            if z_off == 512:
                prod = datt * att
                hi = prod.astype(BF16)
                lo = (prod - hi.astype(F32)).astype(BF16)
                er = lax.broadcasted_iota(jnp.int32, (512, LANES), 0)
                ec = lax.broadcasted_iota(jnp.int32, (512, LANES), 1)
                e = (er // HEAD_DIM == ec).astype(BF16)
                delta = _dot(hi, e) + _dot(lo, e)
                delta_ref[...] = delta.T[0:SUBLANES, :]

        @pl.when(step == nt - 1)
        def _():
            sub = lax.broadcasted_iota(jnp.int32, (SUBLANES, D_MODEL), 0)
            dgf = jnp.sum(acc_gf[...], axis=0, keepdims=True)
            dgate = jnp.sum(acc_gate[...], axis=0, keepdims=True)
            loss = 0.5 * jnp.sum(acc_loss[...]) * (1.0 / D_MODEL)
            vec_ref[...] = jnp.where(sub == 0, dgf, jnp.where(sub == 1, dgate, jnp.where(sub == 2, loss, 0.0)))

    row = lambda w: pl.BlockSpec((tm, w), lambda i: (i, 0))
    return pl.pallas_call(
        body, name="mid", grid=(nt,),
        out_shape=[jax.ShapeDtypeStruct((s, D_MODEL), F32), jax.ShapeDtypeStruct((s, 512), BF16),
                   jax.ShapeDtypeStruct((s, 512), BF16), jax.ShapeDtypeStruct((s, W_G), BF16),
                   jax.ShapeDtypeStruct((SUBLANES, s), F32),
                   jax.ShapeDtypeStruct((512, D_MODEL), F32), jax.ShapeDtypeStruct((512, D_MODEL), F32),
                   jax.ShapeDtypeStruct((D_MODEL, D_MODEL), F32), jax.ShapeDtypeStruct((SUBLANES, D_MODEL), F32)],
        in_specs=[row(512), row(512), row(W_G), row(D_MODEL), row(D_MODEL),
                  _const_spec((1, D_MODEL)), _const_spec((1, D_MODEL)),
                  _const_spec((512, D_MODEL)), _const_spec((512, D_MODEL)), _const_spec((D_MODEL, D_MODEL))],
        out_specs=[row(D_MODEL), row(512), row(512), row(W_G),
                   pl.BlockSpec((SUBLANES, tm), lambda i: (0, i)),
                   pl.BlockSpec((512, D_MODEL), lambda i: (0, 0)), pl.BlockSpec((512, D_MODEL), lambda i: (0, 0)),
                   pl.BlockSpec((D_MODEL, D_MODEL), lambda i: (0, 0)), pl.BlockSpec((SUBLANES, D_MODEL), lambda i: (0, 0))],
        scratch_shapes=[pltpu.VMEM((SUBLANES, D_MODEL), F32)] * 3,
        compiler_params=_params(("arbitrary",), VMEM_LIMIT),
    )(att_a, att_b, g, x, target, gate, g_final, wo_a, wo_b, w_out)


def _rope_bwd(dt, cos, sin, lane):
    u = dt * sin
    lo = (lane % HEAD_DIM) < (HEAD_DIM // 2)
    return dt * cos + jnp.where(lo, pltpu.roll(u, 96, 1), -pltpu.roll(u, 32, 1))


def _swa_bwd(a, datt, l_all, sinks, cos, sin):
    s = a.shape[0]
    nt = s // SWA_ROWS

    def body(sink_ref, a_ref, ap_ref, do_ref, l_ref, cos_ref, sin_ref, da_ref, ds_ref, halo):
        step = pl.program_id(0)
        tile = nt - 1 - step

        @pl.when(step == 0)
        def _():
            halo[...] = jnp.zeros_like(halo)
            ds_ref[...] = jnp.zeros_like(ds_ref)

        lane = lax.broadcasted_iota(jnp.int32, (WINDOW, LANES), 1)
        sub8 = lax.broadcasted_iota(jnp.int32, (SUBLANES, LANES), 0)
        lane8 = lax.broadcasted_iota(jnp.int32, (SUBLANES, LANES), 1)
        blocks = _swa_blocks(a_ref, ap_ref)
        dsink = jnp.zeros((SUBLANES, LANES), F32)

        def join(pair, r0):
            x0, x1 = pair[0][r0:r0 + WINDOW], pair[1][r0:r0 + WINDOW]
            return jnp.where(lane < 64, x0 + pltpu.roll(x0, 64, 1), x1 + pltpu.roll(x1, 64, 1))

        units = [(jb, g) for jb in range(SWA_BLOCKS) for g in range(2)]
        n_u = len(units)
        sinks_col = [_per_head_column([sink_ref[GROUP * g + hh] for hh in range(GROUP)]) for g in range(2)]
        bands = [_swa_band(blocks[jb + 1], blocks[jb], g, lane) for jb, g in units]
        qs = [_stack_heads(blocks[jb + 1], g, lane) for jb, g in units]
        doms = [_stack_heads(do_ref.at[pl.ds(WINDOW * jb, WINDOW), :], g, lane) for jb, g in units]
        lcols = []
        for jb, g in units:
            lv = l_ref[WINDOW * jb:WINDOW * (jb + 1), :]
            lcols.append(_per_head_column([lv[:, GROUP * g + hh:GROUP * g + hh + 1] for hh in range(GROUP)]))
        ps = [jnp.exp(_swa_logits(qs[u], bands[u][0], (tile > 0) if jb == 0 else True) - lcols[u])
              for u, (jb, g) in enumerate(units)]
        dps = [_dot(doms[u], bands[u][1], NT) for u in range(n_u)]
        deltas = [jnp.sum(ps[u] * dps[u], axis=-1, keepdims=True) for u in range(n_u)]
        for u, (jb, g) in enumerate(units):
            sink_term = jnp.exp(sinks_col[g] - lcols[u]) * deltas[u]
            for hh in range(GROUP):
                tot = jnp.sum(sink_term[WINDOW * hh:WINDOW * (hh + 1)])
                dsink = dsink + jnp.where((sub8 == 0) & (lane8 == GROUP * g + hh), -tot, 0.0)
        dss = [(ps[u] * (dps[u] - deltas[u])).astype(BF16) for u in range(n_u)]
        dqs = [_dot(dss[u], bands[u][0]) * SCALE for u in range(n_u)]
        dks = [_dot(dss[u], qs[u], TN) * SCALE for u in range(n_u)]
        dvs = [_dot(ps[u].astype(BF16), doms[u], TN) for u in range(n_u)]

        carry_k, carry_v = halo[:, 0:LANES], halo[:, LANES:2 * LANES]
        for jb in reversed(range(SWA_BLOCKS)):
            rows = slice(WINDOW * jb, WINDOW * (jb + 1))
            cosv, sinv = cos_ref[rows, :], sin_ref[rows, :]
            for g in range(2):
                dq = dqs[2 * jb + g]
                for pb in range(2):
                    r0 = 2 * pb * WINDOW
                    dq_pair = jnp.where(lane < 64, dq[r0:r0 + WINDOW], dq[r0 + WINDOW:r0 + 2 * WINDOW])
                    da_ref[rows, LANES * (2 * g + pb):LANES * (2 * g + pb + 1)] = _rope_bwd(
                        dq_pair, cosv, sinv, lane).astype(BF16)
            dkb, dvb = dks[2 * jb:2 * jb + 2], dvs[2 * jb:2 * jb + 2]
            da_ref[rows, 512:640] = _rope_bwd(join(dkb, WINDOW) + carry_k, cosv, sinv, lane).astype(BF16)
            da_ref[rows, 640:768] = (join(dvb, WINDOW) + carry_v).astype(BF16)
            carry_k, carry_v = join(dkb, 0), join(dvb, 0)
        halo[:, 0:LANES] = carry_k
        halo[:, LANES:2 * LANES] = carry_v
        ds_ref[...] += dsink

    rev = lambda w: pl.BlockSpec((SWA_ROWS, w), lambda i: (nt - 1 - i, 0))
    return pl.pallas_call(
        body, name="swa_bwd", grid=(nt,),
        out_shape=[jax.ShapeDtypeStruct((s, W_A), BF16), jax.ShapeDtypeStruct((SUBLANES, LANES), F32)],
        in_specs=[pl.BlockSpec(memory_space=pltpu.SMEM), rev(W_A),
                  pl.BlockSpec((WINDOW, W_A), lambda i: (jnp.maximum(SWA_BLOCKS * (nt - 1 - i) - 1, 0), 0)),
                  rev(512), rev(LANES), rev(LANES), rev(LANES)],
        out_specs=[rev(W_A), pl.BlockSpec((SUBLANES, LANES), lambda i: (0, 0))],
        scratch_shapes=[pltpu.VMEM((WINDOW, 2 * LANES), F32)],
        compiler_params=_params(("arbitrary",)),
    )(sinks, a, a, datt, l_all, cos, sin)


def _fox_bwd(qa, ka, b, do, lse, delta, ranges, t):
    s = qa.shape[0]
    nt = s // t

    def body(rg_ref, q_ref, do_ref, lse_ref, dl_ref, k_ref, v_ref, dq_ref, dk_ref, dv_ref, dc_ref, dr_ref, dq_acc):
        p = pl.program_id(0)
        j = pl.program_id(1)
        n_query = jnp.clip(_lane_scalar(rg_ref[0], 1, j), 1, nt - j)

        @pl.when(j == 0)
        def _():
            dq_acc[...] = jnp.zeros_like(dq_acc)

        lane = lax.broadcasted_iota(jnp.int32, (t, LANES), 1)
        rows = lax.broadcasted_iota(jnp.int32, (t, t), 0)
        cols = lax.broadcasted_iota(jnp.int32, (t, t), 1)
        kt = k_ref[...]
        vt = v_ref[...]

        ks = [kt[:, LANES * hh:LANES * (hh + 1)] for hh in range(2)]

        def tile(qis, carry, diagonal=False):
            dk0, dk1, dv = carry
            offs = [pl.multiple_of(i * t, t) for i in qis]
            units = [(u, hh) for u in range(len(qis)) for hh in range(2)]
            qts = [q_ref[pl.ds(off, t), :] for off in offs]
            dos = [do_ref[pl.ds(off, t), :] for off in offs]
            lses = [lse_ref[0, :, pl.ds(off, t)] for off in offs]
            dls = [dl_ref[0, :, pl.ds(off, t)] for off in offs]
            qs = [qts[u][:, LANES * hh:LANES * (hh + 1)] for u, hh in units]
            doms = [jnp.where((lane < 64) if hh == 0 else (lane >= 64), dos[u], jnp.zeros_like(dos[u])) for u, hh in units]
            sts = [_dot(ks[hh], qs[n], NT) for n, (u, hh) in enumerate(units)]
            dpts = [_dot(vt, doms[n], NT) for n in range(len(units))]
            if diagonal:
                sts = [jnp.where(cols >= rows, st, NEG) for st in sts]
            pts = [jnp.exp(sts[n] - lses[u][hh:hh + 1, :]) for n, (u, hh) in enumerate(units)]
            dsts = [(pts[n] * (dpts[n] - dls[u][hh:hh + 1, :])).astype(BF16) for n, (u, hh) in enumerate(units)]
            for n, (u, hh) in enumerate(units):
                dv = dv + _dot(pts[n].astype(BF16), doms[n])
                term = _dot(dsts[n], qs[n])
                dk0, dk1 = (dk0 + term, dk1) if hh == 0 else (dk0, dk1 + term)
                dq_acc[hh, pl.ds(offs[u], t), :] += _dot(dsts[n], ks[hh], TN)
            return dk0, dk1, dv

        zero = jnp.zeros((t, LANES), F32)
        carry = tile([j], (zero, zero, zero), diagonal=True)
        n_rest = n_query - 1
        carry = lax.fori_loop(0, n_rest // 2, lambda u, cr: tile([j + 1 + 2 * u, j + 2 + 2 * u], cr), carry)
        dk0, dk1, dv = lax.fori_loop(0, n_rest % 2, lambda u, cr: tile([j + n_rest], cr), carry)
        e0, e1 = _aug_lane(0), _aug_lane(1)
        dk_ref[...] = jnp.where(lane < 64, dk0, dk1).astype(BF16)
        dv_ref[...] = dv.astype(BF16)
        c0 = jnp.broadcast_to(dk0[:, e0 + 3:e0 + 4], (t, LANES))
        c1 = jnp.broadcast_to(dk1[:, e1 + 3:e1 + 4], (t, LANES))
        dc_ref[0] = jnp.where(lane == 2 * p, -c0, jnp.where(lane == 2 * p + 1, -c1, 0.0))

        @pl.when(j == nt - 1)
        def _():
            lane_s = lax.broadcasted_iota(jnp.int32, (s, LANES), 1)
            a0, a1 = dq_acc[0], dq_acc[1]
            dq_ref[...] = (jnp.where(lane_s < 64, a0, a1) * SCALE).astype(BF16)
            r0 = jnp.broadcast_to(a0[:, e0:e0 + 1], (s, LANES))
            r1 = jnp.broadcast_to(a1[:, e1:e1 + 1], (s, LANES))
            dr_ref[0] = jnp.where(lane_s == 2 * p, r0, jnp.where(lane_s == 2 * p + 1, r1, 0.0))

    return pl.pallas_call(
        body, name="fox_bwd", grid=(4, nt),
        out_shape=[jax.ShapeDtypeStruct((s, 512), BF16), jax.ShapeDtypeStruct((s, 512), BF16),
                   jax.ShapeDtypeStruct((s, 512), BF16), jax.ShapeDtypeStruct((4, s, LANES), F32),
                   jax.ShapeDtypeStruct((4, s, LANES), F32)],
        in_specs=[pl.BlockSpec((1, SUBLANES, LANES), lambda p, j: (p, 0, 0)),
                  pl.BlockSpec((s, 2 * LANES), lambda p, j: (0, p)),
                  pl.BlockSpec((s, LANES), lambda p, j: (0, p)),
                  pl.BlockSpec((1, SUBLANES, s), lambda p, j: (p, 0, 0)),
                  pl.BlockSpec((1, SUBLANES, s), lambda p, j: (p, 0, 0)),
                  pl.BlockSpec((t, 2 * LANES), lambda p, j: (j, p)),
                  pl.BlockSpec((t, LANES), lambda p, j: (j, 8 + p))],
        out_specs=[pl.BlockSpec((s, LANES), lambda p, j: (0, p)),
                   pl.BlockSpec((t, LANES), lambda p, j: (j, p)),
                   pl.BlockSpec((t, LANES), lambda p, j: (j, p)),
                   pl.BlockSpec((1, t, LANES), lambda p, j: (p, j, 0)),
                   pl.BlockSpec((1, s, LANES), lambda p, j: (p, 0, 0))],
        scratch_shapes=[pltpu.VMEM((2, s, LANES), F32)],
        compiler_params=_params(("parallel", "arbitrary"), VMEM_LIMIT),
    )(ranges, qa, do, lse, delta, ka, b)


def _fox_cumsum_bwd(dcum_k, dcum_q, f, bf_pad, tb=512):
    s = f.shape[0]
    nb = s // tb

    def body(dc_ref, dr_ref, f_ref, b_ref, df_ref, db_ref, carry):
        step = pl.program_id(0)

        @pl.when(step == 0)
        def _():
            carry[...] = jnp.zeros_like(carry)
            db_ref[...] = jnp.zeros_like(db_ref)

        lane = lax.broadcasted_iota(jnp.int32, (tb, LANES), 1)
        dc = dc_ref[0] + dr_ref[0]
        for k in range(1, 4):
            dc = dc + (dc_ref[k] + dr_ref[k])
        hi, mid, lo = _split3(dc)
        rows = lax.broadcasted_iota(jnp.int32, (tb, tb), 0)
        cols = lax.broadcasted_iota(jnp.int32, (tb, tb), 1)
        triu = (cols >= rows).astype(BF16)
        dlogf = _dot(triu, hi) + _dot(triu, mid) + _dot(triu, lo) + carry[0:1, :]
        carry[...] = jnp.broadcast_to(dlogf[0:1, :], carry.shape)
        u = f_ref[...] + b_ref[...]
        dfb = jnp.where(lane < N_HEADS, dlogf * _sigmoid(-u), 0.0)
        df_ref[...] = dfb.astype(BF16)
        sub = lax.broadcasted_iota(jnp.int32, (SUBLANES, LANES), 0)
        db_ref[...] += jnp.where(sub == 0, jnp.sum(dfb, axis=0, keepdims=True), 0.0)

    return pl.pallas_call(
        body, name="fox_cumsum_bwd", grid=(nb,),
        out_shape=[jax.ShapeDtypeStruct((s, LANES), BF16), jax.ShapeDtypeStruct((SUBLANES, LANES), F32)],
        in_specs=[pl.BlockSpec((4, tb, LANES), lambda i: (0, nb - 1 - i, 0)),
                  pl.BlockSpec((4, tb, LANES), lambda i: (0, nb - 1 - i, 0)),
                  pl.BlockSpec((tb, LANES), lambda i: (nb - 1 - i, 0)), _const_spec((1, LANES))],
        out_specs=[pl.BlockSpec((tb, LANES), lambda i: (nb - 1 - i, 0)),
                   pl.BlockSpec((SUBLANES, LANES), lambda i: (0, 0))],
        scratch_shapes=[pltpu.VMEM((SUBLANES, LANES), F32)],
        compiler_params=_params(("arbitrary",)),
    )(dcum_k, dcum_q, f, bf_pad)


def _dh_norm_bwd(d_a, d_q, d_k, d_v, d_f, d_g, w_t, x, dx2, gnorm, scale1, tm=512):
    s = x.shape[0]
    nt = s // tm

    def body(da_ref, dq_ref, dk_ref, dv_ref, df_ref, dg_ref, w_ref, x_ref, dx2_ref, g_ref, sc_ref, gx_ref, vec_ref,
             a_sh, a_sc, a_g):
        step = pl.program_id(0)

        @pl.when(step == 0)
        def _():
            a_sh[...] = jnp.zeros_like(a_sh)
            a_sc[...] = jnp.zeros_like(a_sc)
            a_g[...] = jnp.zeros_like(a_g)

        def fold(v):
            return jnp.sum(v.reshape(tm // SUBLANES, SUBLANES, D_MODEL), axis=0)

        d_all = jnp.concatenate([da_ref[...], dq_ref[...], dk_ref[...], dv_ref[...], df_ref[...], dg_ref[...]], axis=1)
        dh = _dot(d_all, w_ref[...])
        xv = x_ref[...]
        r = lax.rsqrt(jnp.mean(xv * xv, axis=-1, keepdims=True) + NORM_EPS)
        xn = xv * r
        gn = g_ref[...]
        a_sh[...] += fold(dh)
        a_sc[...] += fold(dh * (xn * gn))
        dn1 = dh * sc_ref[...]
        a_g[...] += fold(dn1 * xn)
        dxn = dn1 * gn
        gx_ref[...] = dx2_ref[...] + r * (dxn - xn * jnp.mean(dxn * xn, axis=-1, keepdims=True))

        @pl.when(step == nt - 1)
        def _():
            sub = lax.broadcasted_iota(jnp.int32, (SUBLANES, D_MODEL), 0)
            v_sh = jnp.sum(a_sh[...], axis=0, keepdims=True)
            v_sc = jnp.sum(a_sc[...], axis=0, keepdims=True)
            v_g = jnp.sum(a_g[...], axis=0, keepdims=True)
            vec_ref[...] = jnp.where(sub == 0, v_sh, jnp.where(sub == 1, v_sc, jnp.where(sub == 2, v_g, 0.0)))

    row = lambda w: pl.BlockSpec((tm, w), lambda i: (i, 0))
    return pl.pallas_call(
        body, name="dh_norm_bwd", grid=(nt,),
        out_shape=[jax.ShapeDtypeStruct((s, D_MODEL), F32), jax.ShapeDtypeStruct((SUBLANES, D_MODEL), F32)],
        in_specs=[row(W_A), row(512), row(512), row(512), row(W_F), row(W_G), _const_spec((W_INT, D_MODEL)),
                  row(D_MODEL), row(D_MODEL), _const_spec((1, D_MODEL)), _const_spec((1, D_MODEL))],
        out_specs=[row(D_MODEL), pl.BlockSpec((SUBLANES, D_MODEL), lambda i: (0, 0))],
        scratch_shapes=[pltpu.VMEM((SUBLANES, D_MODEL), F32)] * 3,
        compiler_params=_params(("arbitrary",), VMEM_LIMIT),
    )(d_a, d_q, d_k, d_v, d_f, d_g, w_t, x, dx2, gnorm, scale1)


def _dw_in(h_t, ds, ts=1024, tc=512):
    n = len(ds)
    s = h_t.shape[1]
    ns = s // ts

    def body(*refs):
        h_ref, d_refs, o_refs = refs[0], refs[1:1 + n], refs[1 + n:1 + 2 * n]
        accs, sem = refs[1 + 2 * n:1 + 3 * n], refs[1 + 3 * n]
        k = pl.program_id(0)

        @pl.when(k == 0)
        def _():
            for acc in accs:
                acc[...] = jnp.zeros_like(acc)

        hv = h_ref[...]
        for acc, d_ref in zip(accs, d_refs):
            width = acc.shape[1]
            for c0 in range(0, width, tc):
                c1 = min(c0 + tc, width)
                acc[:, c0:c1] += _dot(hv, d_ref[:, c0:c1])

        @pl.when(k == ns - 1)
        def _():
            copies = [pltpu.make_async_copy(acc, o_ref, sem.at[g]) for g, (acc, o_ref) in enumerate(zip(accs, o_refs))]
            for cp in copies:
                cp.start()
            for cp in copies:
                cp.wait()

    return pl.pallas_call(
        body, name="dw_in", grid=(ns,),
        out_shape=[jax.ShapeDtypeStruct((D_MODEL, d.shape[1]), F32) for d in ds],
        in_specs=[pl.BlockSpec((D_MODEL, ts), lambda k: (0, k))] + [pl.BlockSpec((ts, d.shape[1]), lambda k: (k, 0)) for d in ds],
        out_specs=[pl.BlockSpec(memory_space=pl.ANY)] * n,
        scratch_shapes=[pltpu.VMEM((D_MODEL, d.shape[1]), F32) for d in ds] + [pltpu.SemaphoreType.DMA((n,))],
        compiler_params=_params(("arbitrary",), VMEM_LIMIT),
    )(h_t, *ds)


def _small_grads(packs, c_t, dada_shard):
    def body(p_ref, ct_ref, da_ref, sum_ref, gw_ref):
        acc = p_ref[0]
        for dev in range(1, 8):
            acc = acc + p_ref[dev]
        sum_ref[...] = acc
        gw_ref[...] = jnp.dot(ct_ref[...], da_ref[...], preferred_element_type=F32, precision=lax.Precision.HIGHEST)

    return pl.pallas_call(
        body, name="small_grads",
        out_shape=[jax.ShapeDtypeStruct(packs.shape[1:], F32),
                   jax.ShapeDtypeStruct((c_t.shape[0], dada_shard.shape[1]), F32)],
    )(packs, c_t, dada_shard)


def _adamw_body(w_ref, g_ref, m_ref, v_ref, d_ref, mo_ref, vo_ref):
    c1 = 1.0 / (1.0 - ADAM_B1 ** ADAM_STEP)
    c2 = 1.0 / (1.0 - ADAM_B2 ** ADAM_STEP)
    gv = g_ref[...]
    mn = ADAM_B1 * m_ref[...] + (1.0 - ADAM_B1) * gv
    vn = ADAM_B2 * v_ref[...] + (1.0 - ADAM_B2) * (gv * gv)
    mo_ref[...] = mn
    vo_ref[...] = vn
    d_ref[...] = -ADAM_LR * ((mn * c1) / (jnp.sqrt(vn * c2) + ADAM_EPS) + ADAM_WD * w_ref[...])


def _adamw3(w, g, m, v, name, tb=128):
    spec = pl.BlockSpec((tb, SUBLANES, LANES), lambda i: (i, 0, 0))
    return pl.pallas_call(
        functools.partial(_adamw_body), name=name, grid=(pl.cdiv(w.shape[0], tb),),
        out_shape=[jax.ShapeDtypeStruct(w.shape, F32)] * 3,
        in_specs=[spec] * 4, out_specs=[spec] * 3,
        compiler_params=_params(("parallel",)),
    )(w, g, m, v)


def _adamw(w, g, m, v, name):
    r, c = w.shape
    tr = 128 if r % 128 == 0 else r
    body = functools.partial(_adamw_body)
    spec = pl.BlockSpec((tr, c), lambda i: (i, 0))
    return pl.pallas_call(
        body, name=name, grid=(r // tr,),
        out_shape=[jax.ShapeDtypeStruct((r, c), F32)] * 3,
        in_specs=[spec] * 4, out_specs=[spec] * 3,
        compiler_params=_params(("parallel",)),
    )(w, g, m, v)


def _rope_inputs(positions):
    inv_freq = 10000.0 ** (-jnp.arange(0, HEAD_DIM, 2, dtype=F32) / HEAD_DIM)
    pos = jnp.broadcast_to(positions.astype(F32)[:, None], (positions.shape[0], LANES))
    return pos, jnp.tile(inv_freq, 4)[None, :]


def _pad_rows(v, rows=SUBLANES):
    return jnp.pad(v, ((0, rows - v.shape[0]), (0, 0)))


def kernel(x, c, positions, w_ada, b_ada, g_norm, w_in, b_f, sinks, w_o_swa, w_o_fox, w_out, g_final, loss_target, m_w_ada, m_b_ada, m_g_norm, m_w_in, m_b_f, m_sinks, m_w_o_swa, m_w_o_fox, m_w_out, m_g_final, v_w_ada, v_b_ada, v_g_norm, v_w_in, v_b_f, v_sinks, v_w_o_swa, v_w_o_fox, v_w_out, v_g_final):
    ix, iy, ic = lax.axis_index("x"), lax.axis_index("y"), lax.axis_index("c")
    chip = 2 * ix + iy
    dev = 2 * chip + ic
    xs, tgt = x[0], loss_target[0]
    s = xs.shape[0]

    b_ada_shard = lax.dynamic_slice(b_ada, (0, chip * 768), (1, 768))
    ada_parts, g_in, g_oa, g_ob, g_out = _gather_inputs(
        _pad_rows(c), w_ada[0], b_ada_shard, [w_in[0], w_o_swa[0], w_o_fox[0], w_out[0]], "gather_inputs")
    ada = lax.dynamic_index_in_dim(ada_parts, dev, axis=1, keepdims=False).reshape(1, 3 * D_MODEL)
    shift, scale, gate = ada[:, :D_MODEL], ada[:, D_MODEL:2 * D_MODEL], ada[:, 2 * D_MODEL:]
    scale1 = 1.0 + scale

    w_ref_order = jnp.transpose(g_in, (1, 0, 2)).reshape(D_MODEL, R_END)
    w_int = jnp.concatenate([
        w_ref_order[:, :R_ZA], w_ref_order[:, R_QB:R_FB], w_ref_order[:, R_FB:R_ZB],
        jnp.zeros((D_MODEL, W_F - N_HEADS), BF16), w_ref_order[:, R_ZA:R_QB], w_ref_order[:, R_ZB:]], axis=1)
    w_int_t = w_int.T
    wo_a = jnp.transpose(g_oa, (1, 0, 2)).reshape(512, D_MODEL)
    wo_b = jnp.transpose(g_ob, (1, 0, 2)).reshape(512, D_MODEL)
    w_o = g_out.reshape(D_MODEL, D_MODEL)

    pos, freq = _rope_inputs(positions[0])
    bf_pad = jnp.pad(b_f, ((0, 0), (0, LANES - N_HEADS)))
    sink_vec = sinks[0]

    a, b, f, g, h_t, cos, sin = _norm_proj(xs, g_norm * scale1, shift, w_int, pos, freq)
    att_a, l_swa = _swa_fwd(a, sink_vec)
    cum = _fox_cumsum(f, bf_pad)
    qa, ka, va, stats = _fox_prep(b, cum, FOX_TILE)
    ranges = _fox_tile_ranges(stats)
    att_b, lse = _fox_fwd(qa, ka, va, ranges, FOX_TILE)

    dx2, datt_a, datt_b, d_g, delta8, dwo_a, dwo_b, dw_out, vec_mid = _mid(
        att_a, att_b, g, xs, tgt, gate, g_final.reshape(1, D_MODEL), wo_a, wo_b, w_o)
    delta = jnp.pad(delta8.reshape(4, 2, s), ((0, 0), (0, SUBLANES - 2), (0, 0)))
    d_a, dsink = _swa_bwd(a, datt_a, l_swa, sink_vec, cos, sin)
    dq, dk, dv, dcum_k, dcum_q = _fox_bwd(qa, ka, b, datt_b, lse, delta, ranges, FOX_TILE)
    d_f, dbf = _fox_cumsum_bwd(dcum_k, dcum_q, f, bf_pad)
    grad_x, vec_dh = _dh_norm_bwd(d_a, dq, dk, dv, d_f, d_g, w_int_t, xs, dx2, g_norm, scale1)
    dw_a, dw_q, dw_k, dw_v, dw_f, dw_g = _dw_in(h_t, [d_a, dq, dk, dv, d_f, d_g])
    dw_in = jnp.concatenate([dw_a, dw_g[:, :512], dw_q, dw_k, dw_v, dw_f[:, :N_HEADS], dw_g[:, 512:]], axis=1)

    tail = jnp.pad(jnp.concatenate([dbf[0:1, :N_HEADS], dsink[0:1, :N_HEADS]], axis=1), ((0, 0), (0, D_MODEL - 2 * N_HEADS)))
    pack = jnp.concatenate([c, vec_dh[0:2], vec_mid[1:2], vec_dh[2:3], vec_mid[0:1], tail, vec_mid[2:3]], axis=0)

    def slots(w, axis):
        if axis == 1:
            return jnp.transpose(w.reshape(w.shape[0], 4, w.shape[1] // 4), (1, 0, 2))
        return w.reshape(4, w.shape[0] // 4, w.shape[1])

    packs, g_wo_a, g_wo_b, g_w_out, g_w_in = _reduce_scatter(
        [slots(dwo_a, 1), slots(dwo_b, 1), slots(dw_out, 0), slots(dw_in, 1)], pack, "reduce_grads")
    dada_all = packs[:, 1:4, :].reshape(8, 3 * D_MODEL)
    dada_shard = lax.dynamic_slice(dada_all, (0, chip * 768), (8, 768))
    sums, g_w_ada = _small_grads(packs, packs[:, 0, :].T, dada_shard)
    g_b_ada = sums[1:4].reshape(1, 3 * D_MODEL)
    g_g_norm = sums[4:5]
    g_g_final = sums[5]
    g_b_f = sums[6:7, :N_HEADS]
    g_sinks = sums[6:7, N_HEADS:2 * N_HEADS]
    loss = sums[7, 0]

    grads = {
        "w_ada": g_w_ada, "b_ada": g_b_ada, "g_norm": g_g_norm, "w_in": g_w_in, "b_f": g_b_f, "sinks": g_sinks,
        "w_o_swa": g_wo_a, "w_o_fox": g_wo_b, "w_out": g_w_out, "g_final": g_g_final,
    }
    params = {
        "w_ada": (w_ada, m_w_ada, v_w_ada), "b_ada": (b_ada, m_b_ada, v_b_ada), "g_norm": (g_norm, m_g_norm, v_g_norm),
        "w_in": (w_in, m_w_in, v_w_in), "b_f": (b_f, m_b_f, v_b_f), "sinks": (sinks, m_sinks, v_sinks),
        "w_o_swa": (w_o_swa, m_w_o_swa, v_w_o_swa), "w_o_fox": (w_o_fox, m_w_o_fox, v_w_o_fox),
        "w_out": (w_out, m_w_out, v_w_out), "g_final": (g_final, m_g_final, v_g_final),
    }
    n_col = w_in.shape[2]

    def as_stored(t):
        return jnp.transpose(t, (2, 0, 1)).reshape(n_col, SUBLANES, LANES)

    def from_stored(t):
        return jnp.transpose(t, (1, 2, 0)).reshape(1, D_MODEL, n_col)

    names = list(grads)
    out_g, out_d, out_m, out_v = [], [], [], []
    for nm in names:
        w, m, v = params[nm]
        if nm == "w_in":
            g_st = as_stored(grads[nm][None])
            d_, m_, v_ = _adamw3(as_stored(w), g_st, as_stored(m), as_stored(v), "adamw_" + nm)
            res = [from_stored(t) for t in (g_st, d_, m_, v_)]
        else:
            shape2 = (w.shape[-2], w.shape[-1]) if w.ndim >= 2 else (1, w.shape[0])
            d_, m_, v_ = _adamw(w.reshape(shape2), grads[nm].reshape(shape2), m.reshape(shape2), v.reshape(shape2), "adamw_" + nm)
            res = [t.reshape(w.shape) for t in (grads[nm], d_, m_, v_)]
        out_g.append(res[0])
        out_d.append(res[1])
        out_m.append(res[2])
        out_v.append(res[3])
    return (loss, grad_x[None], *out_g, *out_d, *out_m, *out_v)
```

```python
import functools

import numpy as np
import jax
import jax.numpy as jnp
from jax import lax
from jax.experimental import pallas as pl
from jax.experimental.pallas import tpu as pltpu

F32 = jnp.float32
BF16 = jnp.bfloat16
MESH = pl.DeviceIdType.MESH

D_MODEL = 1024
HEAD_DIM = 64
N_HEADS = 8
WINDOW = 128
NORM_EPS = 1e-6
SCALE = HEAD_DIM ** -0.5
NEG = -1e30
LANES = 128
SUBLANES = 8
VMEM_LIMIT = 60 * 1024 * 1024
FOX_TILE = 512

W_A, W_B, W_F, W_G = 768, 1536, 128, 3072
OFF_A, OFF_B, OFF_F, OFF_G = 0, 768, 2304, 2432
W_INT = W_A + W_B + W_F + W_G
R_ZA, R_QB, R_FB, R_ZB, R_END = 768, 1280, 2816, 2824, 5384

ADAM_LR, ADAM_B1, ADAM_B2, ADAM_EPS, ADAM_WD, ADAM_STEP = 0.001, 0.9, 0.999, 1e-08, 0.01, 10

NT = (((1,), (1,)), ((), ()))
TN = (((0,), (0,)), ((), ()))


def _dot(a, b, dims=None):
    if dims is None:
        return jnp.dot(a, b, preferred_element_type=F32)
    return lax.dot_general(a, b, dims, preferred_element_type=F32)


def _split3(v):
    hi = v.astype(BF16)
    r1 = v - hi.astype(F32)
    mid = r1.astype(BF16)
    lo = (r1 - mid.astype(F32)).astype(BF16)
    return hi, mid, lo


def _sigmoid(v):
    return 1.0 / (1.0 + jnp.exp(-v))


def _params(sem=None, vmem=None):
    return pltpu.CompilerParams(dimension_semantics=sem, vmem_limit_bytes=vmem)


def _const_spec(shape):
    nd = len(shape)
    return pl.BlockSpec(shape, lambda *_: (0,) * nd, pipeline_mode=pl.Buffered(1))


def _flip(v, f):
    return 1 - v if f else v


_CHIP_FLIPS = ((1, 0), (0, 1), (1, 1))


def _gather_inputs(c_pad, w_ada, b_ada_shard, shards, name):
    n = len(shards)
    n_col = w_ada.shape[1]

    def body(*refs):
        c_ref, wa_ref, ba_ref = refs[:3]
        ins = refs[3:3 + n]
        ada_ref = refs[3 + n]
        outs = refs[4 + n:4 + 2 * n]
        call_ref, send_sems, recv_sems = refs[4 + 2 * n:7 + 2 * n]
        x, y, c = lax.axis_index("x"), lax.axis_index("y"), lax.axis_index("c")
        k_me = 2 * x + y
        me = 2 * k_me + c
        sibling = (x, y, 1 - c)
        chips = [(_flip(x, fx), _flip(y, fy)) for fx, fy in _CHIP_FLIPS]

        def piece(i, chip_k, half):
            hr = ins[i].shape[0] // 2
            return outs[i].at[chip_k, pl.ds(half * hr, hr), :]

        def copy(i, slot, chip_k, half, to):
            return pltpu.make_async_remote_copy(
                src_ref=piece(i, chip_k, half), dst_ref=piece(i, chip_k, half),
                send_sem=send_sems.at[6 * i + slot], recv_sem=recv_sems.at[6 * i + slot],
                device_id=to, device_id_type=MESH)

        def small(ref, slot, sem, to):
            return pltpu.make_async_remote_copy(
                src_ref=ref.at[slot], dst_ref=ref.at[slot], send_sem=send_sems.at[6 * n + sem],
                recv_sem=recv_sems.at[6 * n + sem], device_id=to, device_id_type=MESH)

        for i in range(n):
            outs[i][k_me] = ins[i][...].astype(BF16)
        started = []
        for i in range(n):
            for j, chip in enumerate(chips):
                cp = copy(i, j, k_me, c, (chip[0], chip[1], c))
                cp.start()
                started.append(cp)

        call_ref[me] = c_ref[...]
        peers = [(_flip(x, k & 4), _flip(y, k & 2), _flip(c, k & 1)) for k in range(1, 8)]
        for k, peer in enumerate(peers):
            cp = small(call_ref, me, k, peer)
            cp.start()
            started.append(cp)
        for k, peer in enumerate(peers):
            small(call_ref, 4 * peer[0] + 2 * peer[1] + peer[2], k, peer).wait_recv()
        c_all = call_ref[:, 0, :].astype(BF16)
        ada_ref[k_me] = _dot(c_all, wa_ref[...].astype(BF16)) + ba_ref[...]
        for j, chip in enumerate(chips):
            cp = small(ada_ref, k_me, 7 + j, (chip[0], chip[1], c))
            cp.start()
            started.append(cp)

        for j, chip in enumerate(chips):
            chip_k = 2 * chip[0] + chip[1]
            for i in range(n):
                copy(i, j, chip_k, c, (chip[0], chip[1], c)).wait_recv()
                cp = copy(i, 3 + j, chip_k, c, sibling)
                cp.start()
                started.append(cp)
        for j, chip in enumerate(chips):
            chip_k = 2 * chip[0] + chip[1]
            small(ada_ref, chip_k, 7 + j, (chip[0], chip[1], c)).wait_recv()
            for i in range(n):
                copy(i, 3 + j, chip_k, 1 - c, sibling).wait_recv()
        for cp in started:
            cp.wait_send()

    vmem = pl.BlockSpec(memory_space=pltpu.VMEM)
    return pl.pallas_call(
        body, name=name,
        out_shape=[jax.ShapeDtypeStruct((4, 8, n_col), F32)] + [jax.ShapeDtypeStruct((4,) + s.shape, BF16) for s in shards],
        in_specs=[vmem] * (3 + n),
        out_specs=[vmem] * (1 + n),
        scratch_shapes=[pltpu.VMEM((8,) + c_pad.shape, F32),
                        pltpu.SemaphoreType.DMA((6 * n + 10,)), pltpu.SemaphoreType.DMA((6 * n + 10,))],
        compiler_params=_params(vmem=VMEM_LIMIT),
    )(c_pad, w_ada, b_ada_shard, *shards)


def _reduce_scatter(pieces, pack, name):
    n = len(pieces)

    def body(*refs):
        pack_ref, ins = refs[0], refs[1:1 + n]
        packs_ref, outs = refs[1 + n], refs[2 + n:2 + 2 * n]
        rest = refs[2 + 2 * n:]
        own, got = rest[:n], rest[n:2 * n]
        sendb, recvb = rest[2 * n:3 * n], rest[3 * n:4 * n]
        send_sems, recv_sems, local_sems = rest[4 * n:4 * n + 3]
        x, y, c = lax.axis_index("x"), lax.axis_index("y"), lax.axis_index("c")
        k_me = 2 * x + y
        me = 2 * k_me + c
        sibling = (x, y, 1 - c)
        chips = [(_flip(x, fx), _flip(y, fy)) for fx, fy in _CHIP_FLIPS]
        hrs = [p.shape[1] // 2 for p in pieces]

        def remote(i, slot, src, dst, to):
            return pltpu.make_async_remote_copy(
                src_ref=src, dst_ref=dst, send_sem=send_sems.at[5 * i + slot], recv_sem=recv_sems.at[5 * i + slot],
                device_id=to, device_id_type=MESH)

        started = []
        packs_ref[me] = pack_ref[...]
        peers = [(_flip(x, k & 4), _flip(y, k & 2), _flip(c, k & 1)) for k in range(1, 8)]
        for k, peer in enumerate(peers):
            cp = pltpu.make_async_remote_copy(
                src_ref=pack_ref, dst_ref=packs_ref.at[me], send_sem=send_sems.at[5 * n + k],
                recv_sem=recv_sems.at[5 * n + k], device_id=peer, device_id_type=MESH)
            cp.start()
            started.append(cp)
        loads = []
        for i in range(n):
            ld = pltpu.make_async_copy(ins[i].at[:, pl.ds(c * hrs[i], hrs[i]), :], own[i], local_sems.at[i])
            ld.start()
            loads.append(ld)
            cp = remote(i, 0, ins[i].at[:, pl.ds((1 - c) * hrs[i], hrs[i]), :], got[i], sibling)
            cp.start()
            started.append(cp)
        for i in range(n):
            loads[i].wait()
            remote(i, 0, ins[i].at[:, pl.ds(c * hrs[i], hrs[i]), :], got[i], sibling).wait_recv()
            for j, chip in enumerate(chips):
                chip_k = 2 * chip[0] + chip[1]
                sendb[i][j] = (own[i][chip_k] + got[i][chip_k]).astype(BF16)
                cp = remote(i, 1 + j, sendb[i].at[j], recvb[i].at[j], (chip[0], chip[1], c))
                cp.start()
                started.append(cp)
        for i in range(n):
            acc = own[i][k_me] + got[i][k_me]
            for j, chip in enumerate(chips):
                remote(i, 1 + j, sendb[i].at[j], recvb[i].at[j], (chip[0], chip[1], c)).wait_recv()
                acc = acc + recvb[i][j].astype(F32)
            mine = outs[i].at[pl.ds(c * hrs[i], hrs[i]), :]
            outs[i][pl.ds(pl.multiple_of(c * hrs[i], SUBLANES), hrs[i]), :] = acc
            cp = remote(i, 4, mine, mine, sibling)
            cp.start()
            started.append(cp)
        for i in range(n):
            theirs = outs[i].at[pl.ds((1 - c) * hrs[i], hrs[i]), :]
            remote(i, 4, theirs, theirs, sibling).wait_recv()
        for k, peer in enumerate(peers):
            pltpu.make_async_remote_copy(
                src_ref=pack_ref, dst_ref=packs_ref.at[4 * peer[0] + 2 * peer[1] + peer[2]],
                send_sem=send_sems.at[5 * n + k], recv_sem=recv_sems.at[5 * n + k],
                device_id=peer, device_id_type=MESH).wait_recv()
        for cp in started:
            cp.wait_send()

    vmem = pl.BlockSpec(memory_space=pltpu.VMEM)
    scratch = []
    scratch += [pltpu.VMEM((4, p.shape[1] // 2, p.shape[2]), F32) for p in pieces]
    scratch += [pltpu.VMEM((4, p.shape[1] // 2, p.shape[2]), F32) for p in pieces]
    scratch += [pltpu.VMEM((3, p.shape[1] // 2, p.shape[2]), BF16) for p in pieces]
    scratch += [pltpu.VMEM((3, p.shape[1] // 2, p.shape[2]), BF16) for p in pieces]
    scratch += [pltpu.SemaphoreType.DMA((5 * n + 7,)), pltpu.SemaphoreType.DMA((5 * n + 7,)), pltpu.SemaphoreType.DMA((n,))]
    return pl.pallas_call(
        body, name=name,
        out_shape=[jax.ShapeDtypeStruct((8,) + pack.shape, F32)] + [jax.ShapeDtypeStruct(p.shape[1:], F32) for p in pieces],
        in_specs=[vmem] + [pl.BlockSpec(memory_space=pl.ANY)] * n,
        out_specs=[vmem] * (1 + n),
        scratch_shapes=scratch,
        compiler_params=_params(vmem=VMEM_LIMIT),
    )(pack, *pieces)


def _rope_fwd(t, cos, sin, lane):
    lo = (lane % HEAD_DIM) < (HEAD_DIM // 2)
    return t * cos + jnp.where(lo, -pltpu.roll(t, 96, 1), pltpu.roll(t, 32, 1)) * sin


def _norm_proj(x, gmod, shift, w_int, pos, freq, tm=512):
    s = x.shape[0]

    def body(x_ref, g_ref, sh_ref, w_ref, pos_ref, fr_ref, a_ref, b_ref, f_ref, gg_ref, ht_ref, cos_ref, sin_ref):
        xv = x_ref[...]
        r = lax.rsqrt(jnp.mean(xv * xv, axis=-1, keepdims=True) + NORM_EPS)
        hf = (xv * r) * g_ref[...] + sh_ref[...]
        hb = hf.astype(BF16)
        ht_ref[...] = hf.T.astype(BF16)
        pa = _dot(hb, w_ref[:, OFF_A:OFF_A + W_A])
        ang = pos_ref[...] * fr_ref[...]
        cosv, sinv = jnp.cos(ang), jnp.sin(ang)
        cos_ref[...] = cosv
        sin_ref[...] = sinv
        lane = lax.broadcasted_iota(jnp.int32, (tm, LANES), 1)
        for j in range(5):
            t = pa[:, LANES * j:LANES * (j + 1)]
            a_ref[:, LANES * j:LANES * (j + 1)] = _rope_fwd(t, cosv, sinv, lane).astype(BF16)
        a_ref[:, 640:768] = pa[:, 640:768].astype(BF16)
        b_ref[...] = _dot(hb, w_ref[:, OFF_B:OFF_B + W_B]).astype(BF16)
        f_ref[...] = _dot(hb, w_ref[:, OFF_F:OFF_F + W_F])
        gg_ref[...] = _dot(hb, w_ref[:, OFF_G:OFF_G + W_G]).astype(BF16)

    row = lambda w: pl.BlockSpec((tm, w), lambda i: (i, 0))
    return pl.pallas_call(
        body, name="norm_proj", grid=(s // tm,),
        out_shape=[jax.ShapeDtypeStruct((s, W_A), BF16), jax.ShapeDtypeStruct((s, W_B), BF16),
                   jax.ShapeDtypeStruct((s, W_F), F32), jax.ShapeDtypeStruct((s, W_G), BF16),
                   jax.ShapeDtypeStruct((D_MODEL, s), BF16),
                   jax.ShapeDtypeStruct((s, LANES), F32), jax.ShapeDtypeStruct((s, LANES), F32)],
        in_specs=[row(D_MODEL), _const_spec((1, D_MODEL)), _const_spec((1, D_MODEL)), _const_spec((D_MODEL, W_INT)),
                  row(LANES), _const_spec((1, LANES))],
        out_specs=[row(W_A), row(W_B), row(W_F), row(W_G), pl.BlockSpec((D_MODEL, tm), lambda i: (0, i)),
                   row(LANES), row(LANES)],
        compiler_params=_params(("parallel",), VMEM_LIMIT),
    )(x, gmod, shift, w_int, pos, freq)


def _log_sigmoid(u):
    return jnp.minimum(u, 0.0) - jnp.log(1.0 + jnp.exp(-jnp.abs(u)))


def _fox_cumsum(f, bf_pad, tb=512):
    s = f.shape[0]

    def body(f_ref, b_ref, cum_ref, carry):
        @pl.when(pl.program_id(0) == 0)
        def _():
            carry[...] = jnp.zeros_like(carry)

        lane = lax.broadcasted_iota(jnp.int32, (tb, LANES), 1)
        logf = jnp.where(lane < N_HEADS, _log_sigmoid(f_ref[...] + b_ref[...]), 0.0)
        hi, mid, lo = _split3(logf)
        rows = lax.broadcasted_iota(jnp.int32, (tb, tb), 0)
        cols = lax.broadcasted_iota(jnp.int32, (tb, tb), 1)
        tril = (cols <= rows).astype(BF16)
        cum = _dot(tril, hi) + _dot(tril, mid) + _dot(tril, lo) + carry[0:1, :]
        cum_ref[...] = cum
        carry[...] = jnp.broadcast_to(cum[tb - 1:tb, :], carry.shape)

    return pl.pallas_call(
        body, name="fox_cumsum", grid=(s // tb,),
        out_shape=jax.ShapeDtypeStruct((s, LANES), F32),
        in_specs=[pl.BlockSpec((tb, LANES), lambda i: (i, 0)), _const_spec((1, LANES))],
        out_specs=pl.BlockSpec((tb, LANES), lambda i: (i, 0)),
        scratch_shapes=[pltpu.VMEM((SUBLANES, LANES), F32)],
        compiler_params=_params(("arbitrary",)),
    )(f, bf_pad)


def _aug_lane(h):
    return 64 if h % 2 == 0 else 0


def _fox_prep(b, cum, t):
    s = b.shape[0]

    def body(b_ref, cum_ref, q_ref, k_ref, v_ref, st_ref):
        lane = lax.broadcasted_iota(jnp.int32, (t, LANES), 1)
        lane_b = lane.astype(BF16)
        sub8 = lax.broadcasted_iota(jnp.int32, (SUBLANES, LANES), 0)
        lane8 = lax.broadcasted_iota(jnp.int32, (SUBLANES, LANES), 1)
        cumv = cum_ref[...]
        one = jnp.ones((t, LANES), BF16)
        zero = jnp.zeros((t, LANES), BF16)
        stats = jnp.zeros((SUBLANES, LANES), F32)
        for p in range(4):
            qblk = b_ref[:, LANES * p:LANES * (p + 1)] * SCALE
            kblk = b_ref[:, 512 + LANES * p:512 + LANES * (p + 1)]
            vblk = b_ref[:, 1024 + LANES * p:1024 + LANES * (p + 1)]
            qf, kf = qblk.astype(F32), kblk.astype(F32)
            q2, k2, qk = qf * qf, kf * kf, qf * kf
            for odd in range(2):
                h = 2 * p + odd
                a0 = _aug_lane(h)
                data_b = (lane_b < 64) if odd == 0 else (lane_b >= 64)
                data = (lane < 64) if odd == 0 else (lane >= 64)
                hi, mid, lo = _split3(jnp.broadcast_to(cumv[:, h:h + 1], (t, LANES)))
                ones3_q = (lane_b >= a0 + 3) & (lane_b < a0 + 6)
                ones3_k = (lane_b >= a0) & (lane_b < a0 + 3)
                aug_q = jnp.where(lane_b == a0, hi, jnp.where(lane_b == a0 + 1, mid, jnp.where(
                    lane_b == a0 + 2, lo, jnp.where(ones3_q, one, zero))))
                aug_k = jnp.where(ones3_k, one, jnp.where(lane_b == a0 + 3, -hi, jnp.where(
                    lane_b == a0 + 4, -mid, jnp.where(lane_b == a0 + 5, -lo, zero))))
                q_ref[:, LANES * h:LANES * (h + 1)] = jnp.where(data_b, qblk, aug_q)
                k_ref[:, LANES * h:LANES * (h + 1)] = jnp.where(data_b, kblk, aug_k)
                v_ref[:, LANES * h:LANES * (h + 1)] = jnp.where(data_b, vblk, jnp.where(lane_b == a0, one, zero))
                qn = jnp.sqrt(jnp.max(jnp.sum(jnp.where(data, q2, 0.0), axis=-1, keepdims=True)))
                kn = jnp.sqrt(jnp.max(jnp.sum(jnp.where(data, k2, 0.0), axis=-1, keepdims=True)))
                dmin = jnp.min(jnp.sum(jnp.where(data, qk, 0.0), axis=-1, keepdims=True))
                c_first, c_last = cumv[0:1, h:h + 1], cumv[t - 1:t, h:h + 1]
                row =jnp.where(lane8 == 0, qn, jnp.where(lane8 == 1, kn, jnp.where(
                    lane8 == 2, c_first, jnp.where(lane8 == 3, c_last, jnp.where(lane8 == 4, dmin, 0.0)))))
                stats = jnp.where(sub8 == h, row, stats)
        st_ref[0] = stats

    wide = pl.BlockSpec((t, 1024), lambda i: (i, 0))
    return pl.pallas_call(
        body, name="fox_prep", grid=(s // t,),
        out_shape=[jax.ShapeDtypeStruct((s, 1024), BF16)] * 3 + [jax.ShapeDtypeStruct((s // t, SUBLANES, LANES), F32)],
        in_specs=[pl.BlockSpec((t, W_B), lambda i: (i, 0)), pl.BlockSpec((t, LANES), lambda i: (i, 0))],
        out_specs=[wide, wide, wide, pl.BlockSpec((1, SUBLANES, LANES), lambda i: (i, 0, 0))],
        compiler_params=_params(("parallel",)),
    )(b, cum)


PRUNE_MARGIN = 88.0


def _fox_tile_ranges(stats):
    nt = stats.shape[0]
    qn, kn, c_first, c_last, d_min = (stats[:, :, n] for n in range(5))
    bound = (1.01 * qn[:, None, :] * kn[None, :, :] - jnp.minimum(d_min, 0.0)[:, None, :] + 0.05
             + c_first[:, None, :] - c_last[None, :, :])
    idx = jnp.arange(nt)
    skip = (bound <= -PRUNE_MARGIN) & (idx[None, :, None] < idx[:, None, None])
    first_key = jnp.sum(jnp.cumprod(skip, axis=1), axis=1)
    needed = (idx[None, :, None] >= first_key[:, None, :]) & (idx[None, :, None] <= idx[:, None, None])
    last_query = jnp.max(jnp.where(needed, idx[:, None, None], 0), axis=0)
    n_query = last_query - idx[:, None] + 1
    table = jnp.zeros((4, SUBLANES, LANES), F32)
    for odd in range(2):
        table = table.at[:, odd, :nt].set(first_key[:, odd::2].T.astype(F32))
        table = table.at[:, 2 + odd, :nt].set(n_query[:, odd::2].T.astype(F32))
    return table


def _lane_scalar(block, row, lane_idx):
    sub8 = lax.broadcasted_iota(jnp.int32, (SUBLANES, LANES), 0)
    lane8 = lax.broadcasted_iota(jnp.int32, (SUBLANES, LANES), 1)
    return jnp.sum(jnp.where((sub8 == row) & (lane8 == lane_idx), block, 0.0)).astype(jnp.int32)


def _fox_fwd(qa, ka, va, ranges, t):
    s = qa.shape[0]
    nt = s // t
    nc = t // LANES

    def body(rg_ref, q_ref, k_ref, v_ref, o_ref, lse_ref):
        i = pl.program_id(1)
        lane = lax.broadcasted_iota(jnp.int32, (t, LANES), 1)
        rows = lax.broadcasted_iota(jnp.int32, (t, t), 0)
        cols = lax.broadcasted_iota(jnp.int32, (t, t), 1)
        firsts = [jnp.clip(_lane_scalar(rg_ref[0], hh, i), 0, i) for hh in range(2)]
        first = jnp.maximum(firsts[0], firsts[1])

        def update(js, carry, heads=(0, 1), diagonal=False):
            offs = [pl.multiple_of(j * t, t) for j in js]
            kts = [k_ref[pl.ds(off, t), :] for off in offs]
            vts = [v_ref[pl.ds(off, t), :] for off in offs]
            scs = {hh: [_dot(q_ref[:, LANES * hh:LANES * (hh + 1)], kt[:, LANES * hh:LANES * (hh + 1)], NT) for kt in kts]
                   for hh in heads}
            if diagonal:
                scs = {hh: [jnp.where(cols <= rows, sc, NEG) for sc in scs[hh]] for hh in heads}
            m_new = {}
            for hh in heads:
                part = None
                for sc in scs[hh]:
                    for cch in range(nc):
                        chunk = sc[:, LANES * cch:LANES * (cch + 1)]
                        part = chunk if part is None else jnp.maximum(part, chunk)
                m_new[hh] = jnp.maximum(carry[2 * hh], jnp.max(part, axis=-1, keepdims=True))
            alphas = {hh: jnp.exp(carry[2 * hh] - m_new[hh]) for hh in heads}
            ps = {hh: [jnp.exp(sc - m_new[hh]).astype(BF16) for sc in scs[hh]] for hh in heads}
            out = list(carry)
            for hh in heads:
                pv = None
                for p, vt in zip(ps[hh], vts):
                    term = _dot(p, vt[:, LANES * hh:LANES * (hh + 1)])
                    pv = term if pv is None else pv + term
                out[2 * hh], out[2 * hh + 1] = m_new[hh], alphas[hh] * carry[2 * hh + 1] + pv
            return tuple(out)

        col0 = jnp.full((t, 1), NEG, F32)
        zero = jnp.zeros((t, LANES), F32)
        carry = (col0, zero, col0, zero)
        for hh in range(2):
            carry = lax.fori_loop(firsts[hh], first, lambda j, cr, hh=hh: update([j], cr, heads=(hh,)), carry)
        n_off = i - first
        carry = lax.fori_loop(0, n_off // 2, lambda u, cr: update([first + 2 * u, first + 2 * u + 1], cr), carry)
        carry = lax.fori_loop(0, n_off % 2, lambda u, cr: update([i - 1], cr), carry)
        m0, acc0, m1, acc1 = update([i], carry, diagonal=True)
        l0, l1 = acc0[:, _aug_lane(0):_aug_lane(0) + 1], acc1[:, _aug_lane(1):_aug_lane(1) + 1]
        o_ref[...] = jnp.where(lane < 64, acc0 * (1.0 / l0), acc1 * (1.0 / l1)).astype(BF16)
        sub = lax.broadcasted_iota(jnp.int32, (SUBLANES, t), 0)
        lse0 = jnp.broadcast_to(m0 + jnp.log(l0), (t, LANES)).T[0:SUBLANES, :]
        lse1 = jnp.broadcast_to(m1 + jnp.log(l1), (t, LANES)).T[0:SUBLANES, :]
        lse_ref[0] = jnp.where(sub == 0, lse0, jnp.where(sub == 1, lse1, 0.0))

    pair = pl.BlockSpec((s, 2 * LANES), lambda p, i: (0, p))
    return pl.pallas_call(
        body, name="fox_fwd", grid=(4, nt),
        out_shape=[jax.ShapeDtypeStruct((s, 512), BF16), jax.ShapeDtypeStruct((4, SUBLANES, s), F32)],
        in_specs=[pl.BlockSpec((1, SUBLANES, LANES), lambda p, i: (p, 0, 0)),
                  pl.BlockSpec((t, 2 * LANES), lambda p, i: (i, p)), pair, pair],
        out_specs=[pl.BlockSpec((t, LANES), lambda p, i: (i, p)),
                   pl.BlockSpec((1, SUBLANES, t), lambda p, i: (p, 0, i))],
        compiler_params=_params(("parallel", "arbitrary"), VMEM_LIMIT),
    )(ranges, qa, ka, va)


def _dup_halves(blk, lane):
    f = blk.astype(F32)
    r = pltpu.roll(f, 64, 1)
    return jnp.where(lane < 64, f, r).astype(BF16), jnp.where(lane >= 64, f, r).astype(BF16)


GROUP = 4
GROUP_ROWS = GROUP * WINDOW


def _stack_heads(ref, g, lane):
    parts = []
    for pb in (2 * g, 2 * g + 1):
        blk = ref[:, LANES * pb:LANES * (pb + 1)]
        zero = jnp.zeros_like(blk)
        parts += [jnp.where(lane < 64, blk, zero), jnp.where(lane >= 64, blk, zero)]
    return jnp.concatenate(parts, axis=0)


def _swa_band(a_ref, ap_ref, g, lane):
    k = jnp.concatenate([_dup_halves(ap_ref[:, 512:640], lane)[g], _dup_halves(a_ref[:, 512:640], lane)[g]], axis=0)
    v = jnp.concatenate([_dup_halves(ap_ref[:, 640:768], lane)[g], _dup_halves(a_ref[:, 640:768], lane)[g]], axis=0)
    return k, v


def _swa_logits(q, k, has_prev):
    sc = _dot(q, k, NT) * SCALE
    rr = lax.broadcasted_iota(jnp.int32, sc.shape, 0) % WINDOW
    cc = lax.broadcasted_iota(jnp.int32, sc.shape, 1)
    valid = (cc > rr) & (cc <= rr + WINDOW) & (has_prev | (cc >= WINDOW))
    return jnp.where(valid, sc, NEG)


def _per_head_column(values):
    return jnp.concatenate([jnp.broadcast_to(v, (WINDOW, 1)) for v in values], axis=0)


SWA_BLOCKS = 4
SWA_ROWS = SWA_BLOCKS * WINDOW


def _swa_blocks(a_ref, ap_ref):
    return [ap_ref] + [a_ref.at[pl.ds(WINDOW * jb, WINDOW), :] for jb in range(SWA_BLOCKS)]


def _swa_fwd(a, sinks):
    s = a.shape[0]

    def body(sink_ref, a_ref, ap_ref, o_ref, l_ref):
        lane = lax.broadcasted_iota(jnp.int32, (WINDOW, LANES), 1)
        blocks = _swa_blocks(a_ref, ap_ref)
        units = [(jb, g) for jb in range(SWA_BLOCKS) for g in range(2)]
        sinks_col = [_per_head_column([sink_ref[GROUP * g + hh] for hh in range(GROUP)]) for g in range(2)]
        bands = [_swa_band(blocks[jb + 1], blocks[jb], g, lane) for jb, g in units]
        scs = [_swa_logits(_stack_heads(blocks[jb + 1], g, lane), bands[u][0],
                           (pl.program_id(0) > 0) if jb == 0 else True) for u, (jb, g) in enumerate(units)]
        ms = [jnp.maximum(jnp.max(scs[u], axis=-1, keepdims=True), sinks_col[g]) for u, (jb, g) in enumerate(units)]
        ps = [jnp.exp(scs[u] - ms[u]) for u in range(len(units))]
        dens = [jnp.sum(ps[u], axis=-1, keepdims=True) + jnp.exp(sinks_col[g] - ms[u]) for u, (jb, g) in enumerate(units)]
        outs = [_dot((ps[u] * (1.0 / dens[u])).astype(BF16), bands[u][1]) for u in range(len(units))]
        for jb in range(SWA_BLOCKS):
            rows = slice(WINDOW * jb, WINDOW * (jb + 1))
            l_all = jnp.zeros((WINDOW, LANES), F32)
            for g in range(2):
                u = 2 * jb + g
                lcol = ms[u] + jnp.log(dens[u])
                for pb in range(2):
                    r0 = 2 * pb * WINDOW
                    o_ref[rows, LANES * (2 * g + pb):LANES * (2 * g + pb + 1)] = jnp.where(
                        lane < 64, outs[u][r0:r0 + WINDOW], outs[u][r0 + WINDOW:r0 + 2 * WINDOW]).astype(BF16)
                for hh in range(GROUP):
                    l_all = jnp.where(lane == GROUP * g + hh, lcol[WINDOW * hh:WINDOW * (hh + 1)], l_all)
            l_ref[rows, :] = l_all

    return pl.pallas_call(
        body, name="swa_fwd", grid=(s // SWA_ROWS,),
        out_shape=[jax.ShapeDtypeStruct((s, 512), BF16), jax.ShapeDtypeStruct((s, LANES), F32)],
        in_specs=[pl.BlockSpec(memory_space=pltpu.SMEM),
                  pl.BlockSpec((SWA_ROWS, W_A), lambda i: (i, 0)),
                  pl.BlockSpec((WINDOW, W_A), lambda i: (jnp.maximum(SWA_BLOCKS * i - 1, 0), 0))],
        out_specs=[pl.BlockSpec((SWA_ROWS, 512), lambda i: (i, 0)), pl.BlockSpec((SWA_ROWS, LANES), lambda i: (i, 0))],
        compiler_params=_params(("parallel",)),
    )(sinks, a, a)


def _mid(att_a, att_b, g, x, target, gate, g_final, wo_a, wo_b, w_out, tm=256):
    s = x.shape[0]
    nt = s // tm

    def body(aa_ref, ab_ref, g_ref, x_ref, t_ref, gate_ref, gf_ref, woa_ref, wob_ref, wout_ref,
             dx_ref, daa_ref, dab_ref, dg_ref, delta_ref, dwoa_ref, dwob_ref, dwout_ref, vec_ref,
             acc_gf, acc_gate, acc_loss):
        step = pl.program_id(0)

        @pl.when(step == 0)
        def _():
            dwoa_ref[...] = jnp.zeros_like(dwoa_ref)
            dwob_ref[...] = jnp.zeros_like(dwob_ref)
            dwout_ref[...] = jnp.zeros_like(dwout_ref)
            acc_gf[...] = jnp.zeros_like(acc_gf)
            acc_gate[...] = jnp.zeros_like(acc_gate)
            acc_loss[...] = jnp.zeros_like(acc_loss)

        def fold(v):
            return jnp.sum(v.reshape(tm // SUBLANES, SUBLANES, D_MODEL), axis=0)

        gate = gate_ref[...]
        gfin = gf_ref[...]
        branches = []
        for att_ref, z_off, wo_ref in ((aa_ref, 0, woa_ref), (ab_ref, 512, wob_ref)):
            att = att_ref[...].astype(F32)
            z = g_ref[:, z_off:z_off + 512].astype(F32)
            sz = _sigmoid(z)
            silu = z * sz
            u = (att * silu).astype(BF16)
            branches.append((att, z, sz, silu, u, _dot(u, wo_ref[...])))
        ga = g_ref[:, 1024:2048].astype(F32)
        gb = g_ref[:, 2048:3072].astype(F32)
        sga, sgb = _sigmoid(ga), _sigmoid(gb)
        y_a, y_b = branches[0][5], branches[1][5]
        mb = (sga * y_a + sgb * y_b).astype(BF16)
        o = _dot(mb, wout_ref[...])
        x2 = x_ref[...] + gate * o
        r2 = lax.rsqrt(jnp.mean(x2 * x2, axis=-1, keepdims=True) + NORM_EPS)
        xn2 = x2 * r2
        err = xn2 * gfin - t_ref[...]
        acc_loss[...] += fold(err * err)
        dy = err * (1.0 / D_MODEL)
        acc_gf[...] += fold(dy * xn2)
        dxn = dy * gfin
        dx2 = r2 * (dxn - xn2 * jnp.mean(dxn * xn2, axis=-1, keepdims=True))
        dx_ref[...] = dx2
        acc_gate[...] += fold(dx2 * o)
        d_o = (dx2 * gate).astype(BF16)
        dwout_ref[...] += _dot(mb, d_o, TN)
        dm = _dot(d_o, wout_ref[...], NT)
        dg_ref[:, 1024:2048] = (dm * y_a * sga * (1.0 - sga)).astype(BF16)
        dg_ref[:, 2048:3072] = (dm * y_b * sgb * (1.0 - sgb)).astype(BF16)
        for (att, z, sz, silu, u, _), sg, wo_ref, dwo_ref, datt_ref, z_off in (
                (branches[0], sga, woa_ref, dwoa_ref, daa_ref, 0), (branches[1], sgb, wob_ref, dwob_ref, dab_ref, 512)):
            dyb = (dm * sg).astype(BF16)
            dwo_ref[...] += _dot(u, dyb, TN)
            du = _dot(dyb, wo_ref[...], NT)
            datt = du * silu
            datt_ref[...] = datt.astype(BF16)
            dg_ref[:, z_off:z_off + 512] = (du * att * (sz * (1.0 + z * (1.0 - sz)))).astype(BF16)
            if z_off == 512:
                prod = datt * att
                hi = prod.astype(BF16)
                lo = (prod - hi.astype(F32)).astype(BF16)
                er = lax.broadcasted_iota(jnp.int32, (512, LANES), 0)
                ec = lax.broadcasted_iota(jnp.int32, (512, LANES), 1)
                e = (er // HEAD_DIM == ec).astype(BF16)
                delta = _dot(hi, e) + _dot(lo, e)
                delta_ref[...] = delta.T[0:SUBLANES, :]

        @pl.when(step == nt - 1)
        def _():
            sub = lax.broadcasted_iota(jnp.int32, (SUBLANES, D_MODEL), 0)
            dgf = jnp.sum(acc_gf[...], axis=0, keepdims=True)
            dgate = jnp.sum(acc_gate[...], axis=0, keepdims=True)
            loss = 0.5 * jnp.sum(acc_loss[...]) * (1.0 / D_MODEL)
            vec_ref[...] = jnp.where(sub == 0, dgf, jnp.where(sub == 1, dgate, jnp.where(sub == 2, loss, 0.0)))

    row = lambda w: pl.BlockSpec((tm, w), lambda i: (i, 0))
    return pl.pallas_call(
        body, name="mid", grid=(nt,),
        out_shape=[jax.ShapeDtypeStruct((s, D_MODEL), F32), jax.ShapeDtypeStruct((s, 512), BF16),
                   jax.ShapeDtypeStruct((s, 512), BF16), jax.ShapeDtypeStruct((s, W_G), BF16),
                   jax.ShapeDtypeStruct((SUBLANES, s), F32),
                   jax.ShapeDtypeStruct((512, D_MODEL), F32), jax.ShapeDtypeStruct((512, D_MODEL), F32),
                   jax.ShapeDtypeStruct((D_MODEL, D_MODEL), F32), jax.ShapeDtypeStruct((SUBLANES, D_MODEL), F32)],
        in_specs=[row(512), row(512), row(W_G), row(D_MODEL), row(D_MODEL),
                  _const_spec((1, D_MODEL)), _const_spec((1, D_MODEL)),
                  _const_spec((512, D_MODEL)), _const_spec((512, D_MODEL)), _const_spec((D_MODEL, D_MODEL))],
        out_specs=[row(D_MODEL), row(512), row(512), row(W_G),
                   pl.BlockSpec((SUBLANES, tm), lambda i: (0, i)),
                   pl.BlockSpec((512, D_MODEL), lambda i: (0, 0)), pl.BlockSpec((512, D_MODEL), lambda i: (0, 0)),
                   pl.BlockSpec((D_MODEL, D_MODEL), lambda i: (0, 0)), pl.BlockSpec((SUBLANES, D_MODEL), lambda i: (0, 0))],
        scratch_shapes=[pltpu.VMEM((SUBLANES, D_MODEL), F32)] * 3,
        compiler_params=_params(("arbitrary",), VMEM_LIMIT),
    )(att_a, att_b, g, x, target, gate, g_final, wo_a, wo_b, w_out)


def _rope_bwd(dt, cos, sin, lane):
    u = dt * sin
    lo = (lane % HEAD_DIM) < (HEAD_DIM // 2)
    return dt * cos + jnp.where(lo, pltpu.roll(u, 96, 1), -pltpu.roll(u, 32, 1))


def _swa_bwd(a, datt, l_all, sinks, cos, sin):
    s = a.shape[0]
    nt = s // SWA_ROWS

    def body(sink_ref, a_ref, ap_ref, do_ref, l_ref, cos_ref, sin_ref, da_ref, ds_ref, halo):
        step = pl.program_id(0)
        tile = nt - 1 - step

        @pl.when(step == 0)
        def _():
            halo[...] = jnp.zeros_like(halo)
            ds_ref[...] = jnp.zeros_like(ds_ref)

        lane = lax.broadcasted_iota(jnp.int32, (WINDOW, LANES), 1)
        sub8 = lax.broadcasted_iota(jnp.int32, (SUBLANES, LANES), 0)
        lane8 = lax.broadcasted_iota(jnp.int32, (SUBLANES, LANES), 1)
        blocks = _swa_blocks(a_ref, ap_ref)
        dsink = jnp.zeros((SUBLANES, LANES), F32)

        def join(pair, r0):
            x0, x1 = pair[0][r0:r0 + WINDOW], pair[1][r0:r0 + WINDOW]
            return jnp.where(lane < 64, x0 + pltpu.roll(x0, 64, 1), x1 + pltpu.roll(x1, 64, 1))

        units = [(jb, g) for jb in range(SWA_BLOCKS) for g in range(2)]
        n_u = len(units)
        sinks_col = [_per_head_column([sink_ref[GROUP * g + hh] for hh in range(GROUP)]) for g in range(2)]
        bands = [_swa_band(blocks[jb + 1], blocks[jb], g, lane) for jb, g in units]
        qs = [_stack_heads(blocks[jb + 1], g, lane) for jb, g in units]
        doms = [_stack_heads(do_ref.at[pl.ds(WINDOW * jb, WINDOW), :], g, lane) for jb, g in units]
        lcols = []
        for jb, g in units:
            lv = l_ref[WINDOW * jb:WINDOW * (jb + 1), :]
            lcols.append(_per_head_column([lv[:, GROUP * g + hh:GROUP * g + hh + 1] for hh in range(GROUP)]))
        ps = [jnp.exp(_swa_logits(qs[u], bands[u][0], (tile > 0) if jb == 0 else True) - lcols[u])
              for u, (jb, g) in enumerate(units)]
        dps = [_dot(doms[u], bands[u][1], NT) for u in range(n_u)]
        deltas = [jnp.sum(ps[u] * dps[u], axis=-1, keepdims=True) for u in range(n_u)]
        for u, (jb, g) in enumerate(units):
            sink_term = jnp.exp(sinks_col[g] - lcols[u]) * deltas[u]
            for hh in range(GROUP):
                tot = jnp.sum(sink_term[WINDOW * hh:WINDOW * (hh + 1)])
                dsink = dsink + jnp.where((sub8 == 0) & (lane8 == GROUP * g + hh), -tot, 0.0)
        dss = [(ps[u] * (dps[u] - deltas[u])).astype(BF16) for u in range(n_u)]
        dqs = [_dot(dss[u], bands[u][0]) * SCALE for u in range(n_u)]
        dks = [_dot(dss[u], qs[u], TN) * SCALE for u in range(n_u)]
        dvs = [_dot(ps[u].astype(BF16), doms[u], TN) for u in range(n_u)]

        carry_k, carry_v = halo[:, 0:LANES], halo[:, LANES:2 * LANES]
        for jb in reversed(range(SWA_BLOCKS)):
            rows = slice(WINDOW * jb, WINDOW * (jb + 1))
            cosv, sinv = cos_ref[rows, :], sin_ref[rows, :]
            for g in range(2):
                dq = dqs[2 * jb + g]
                for pb in range(2):
                    r0 = 2 * pb * WINDOW
                    dq_pair = jnp.where(lane < 64, dq[r0:r0 + WINDOW], dq[r0 + WINDOW:r0 + 2 * WINDOW])
                    da_ref[rows, LANES * (2 * g + pb):LANES * (2 * g + pb + 1)] = _rope_bwd(
                        dq_pair, cosv, sinv, lane).astype(BF16)
            dkb, dvb = dks[2 * jb:2 * jb + 2], dvs[2 * jb:2 * jb + 2]
            da_ref[rows, 512:640] = _rope_bwd(join(dkb, WINDOW) + carry_k, cosv, sinv, lane).astype(BF16)
            da_ref[rows, 640:768] = (join(dvb, WINDOW) + carry_v).astype(BF16)
            carry_k, carry_v = join(dkb, 0), join(dvb, 0)
        halo[:, 0:LANES] = carry_k
        halo[:, LANES:2 * LANES] = carry_v
        ds_ref[...] += dsink

    rev = lambda w: pl.BlockSpec((SWA_ROWS, w), lambda i: (nt - 1 - i, 0))
    return pl.pallas_call(
        body, name="swa_bwd", grid=(nt,),
        out_shape=[jax.ShapeDtypeStruct((s, W_A), BF16), jax.ShapeDtypeStruct((SUBLANES, LANES), F32)],
        in_specs=[pl.BlockSpec(memory_space=pltpu.SMEM), rev(W_A),
                  pl.BlockSpec((WINDOW, W_A), lambda i: (jnp.maximum(SWA_BLOCKS * (nt - 1 - i) - 1, 0), 0)),
                  rev(512), rev(LANES), rev(LANES), rev(LANES)],
        out_specs=[rev(W_A), pl.BlockSpec((SUBLANES, LANES), lambda i: (0, 0))],
        scratch_shapes=[pltpu.VMEM((WINDOW, 2 * LANES), F32)],
        compiler_params=_params(("arbitrary",)),
    )(sinks, a, a, datt, l_all, cos, sin)


def _fox_bwd(qa, ka, b, do, lse, delta, ranges, t):
    s = qa.shape[0]
    nt = s // t

    def body(rg_ref, q_ref, do_ref, lse_ref, dl_ref, k_ref, v_ref, dq_ref, dk_ref, dv_ref, dc_ref, dr_ref, dq_acc):
        p = pl.program_id(0)
        j = pl.program_id(1)
        n_queries = [jnp.clip(_lane_scalar(rg_ref[0], 2 + hh, j), 1, nt - j) for hh in range(2)]

        @pl.when(j == 0)
        def _():
            dq_acc[...] = jnp.zeros_like(dq_acc)

        lane = lax.broadcasted_iota(jnp.int32, (t, LANES), 1)
        rows = lax.broadcasted_iota(jnp.int32, (t, t), 0)
        cols = lax.broadcasted_iota(jnp.int32, (t, t), 1)
        kt = k_ref[...]
        vt = v_ref[...]

        ks = [kt[:, LANES * hh:LANES * (hh + 1)] for hh in range(2)]

        def tile(qis, carry, heads=(0, 1), diagonal=False):
            dk0, dk1, dv = carry
            offs = [pl.multiple_of(i * t, t) for i in qis]
            units = [(u, hh) for u in range(len(qis)) for hh in heads]
            qts = [q_ref[pl.ds(off, t), :] for off in offs]
            dos = [do_ref[pl.ds(off, t), :] for off in offs]
            lses = [lse_ref[0, :, pl.ds(off, t)] for off in offs]
            dls = [dl_ref[0, :, pl.ds(off, t)] for off in offs]
            qs = [qts[u][:, LANES * hh:LANES * (hh + 1)] for u, hh in units]
            doms = [jnp.where((lane < 64) if hh == 0 else (lane >= 64), dos[u], jnp.zeros_like(dos[u])) for u, hh in units]
            sts = [_dot(ks[hh], qs[n], NT) for n, (u, hh) in enumerate(units)]
            dpts = [_dot(vt, doms[n], NT) for n in range(len(units))]
            if diagonal:
                sts = [jnp.where(cols >= rows, st, NEG) for st in sts]
            pts = [jnp.exp(sts[n] - lses[u][hh:hh + 1, :]) for n, (u, hh) in enumerate(units)]
            dsts = [(pts[n] * (dpts[n] - dls[u][hh:hh + 1, :])).astype(BF16) for n, (u, hh) in enumerate(units)]
            for n, (u, hh) in enumerate(units):
                dv = dv + _dot(pts[n].astype(BF16), doms[n])
                term = _dot(dsts[n], qs[n])
                dk0, dk1 = (dk0 + term, dk1) if hh == 0 else (dk0, dk1 + term)
                dq_acc[hh, pl.ds(offs[u], t), :] += _dot(dsts[n], ks[hh], TN)
            return dk0, dk1, dv

        zero = jnp.zeros((t, LANES), F32)
        carry = tile([j], (zero, zero, zero), diagonal=True)
        n_rest = jnp.minimum(n_queries[0], n_queries[1]) - 1
        carry = lax.fori_loop(0, n_rest // 2, lambda u, cr: tile([j + 1 + 2 * u, j + 2 + 2 * u], cr), carry)
        carry = lax.fori_loop(0, n_rest % 2, lambda u, cr: tile([j + n_rest], cr), carry)
        for hh in range(2):
            carry = lax.fori_loop(j + 1 + n_rest, j + n_queries[hh], lambda i, cr, hh=hh: tile([i], cr, heads=(hh,)), carry)
        dk0, dk1, dv = carry
        e0, e1 = _aug_lane(0), _aug_lane(1)
        dk_ref[...] = jnp.where(lane < 64, dk0, dk1).astype(BF16)
        dv_ref[...] = dv.astype(BF16)
        c0 = jnp.broadcast_to(dk0[:, e0 + 3:e0 + 4], (t, LANES))
        c1 = jnp.broadcast_to(dk1[:, e1 + 3:e1 + 4], (t, LANES))
        dc_ref[0] = jnp.where(lane == 2 * p, -c0, jnp.where(lane == 2 * p + 1, -c1, 0.0))

        @pl.when(j == nt - 1)
        def _():
            lane_s = lax.broadcasted_iota(jnp.int32, (s, LANES), 1)
            a0, a1 = dq_acc[0], dq_acc[1]
            dq_ref[...] = (jnp.where(lane_s < 64, a0, a1) * SCALE).astype(BF16)
            r0 = jnp.broadcast_to(a0[:, e0:e0 + 1], (s, LANES))
            r1 = jnp.broadcast_to(a1[:, e1:e1 + 1], (s, LANES))
            dr_ref[0] = jnp.where(lane_s == 2 * p, r0, jnp.where(lane_s == 2 * p + 1, r1, 0.0))

    return pl.pallas_call(
        body, name="fox_bwd", grid=(4, nt),
        out_shape=[jax.ShapeDtypeStruct((s, 512), BF16), jax.ShapeDtypeStruct((s, 512), BF16),
                   jax.ShapeDtypeStruct((s, 512), BF16), jax.ShapeDtypeStruct((4, s, LANES), F32),
                   jax.ShapeDtypeStruct((4, s, LANES), F32)],
        in_specs=[pl.BlockSpec((1, SUBLANES, LANES), lambda p, j: (p, 0, 0)),
                  pl.BlockSpec((s, 2 * LANES), lambda p, j: (0, p)),
                  pl.BlockSpec((s, LANES), lambda p, j: (0, p)),
                  pl.BlockSpec((1, SUBLANES, s), lambda p, j: (p, 0, 0)),
                  pl.BlockSpec((1, SUBLANES, s), lambda p, j: (p, 0, 0)),
                  pl.BlockSpec((t, 2 * LANES), lambda p, j: (j, p)),
                  pl.BlockSpec((t, LANES), lambda p, j: (j, 8 + p))],
        out_specs=[pl.BlockSpec((s, LANES), lambda p, j: (0, p)),
                   pl.BlockSpec((t, LANES), lambda p, j: (j, p)),
                   pl.BlockSpec((t, LANES), lambda p, j: (j, p)),
                   pl.BlockSpec((1, t, LANES), lambda p, j: (p, j, 0)),
                   pl.BlockSpec((1, s, LANES), lambda p, j: (p, 0, 0))],
        scratch_shapes=[pltpu.VMEM((2, s, LANES), F32)],
        compiler_params=_params(("parallel", "arbitrary"), VMEM_LIMIT),
    )(ranges, qa, do, lse, delta, ka, b)


def _fox_cumsum_bwd(dcum_k, dcum_q, f, bf_pad, tb=512):
    s = f.shape[0]
    nb = s // tb

    def body(dc_ref, dr_ref, f_ref, b_ref, df_ref, db_ref, carry):
        step = pl.program_id(0)

        @pl.when(step == 0)
        def _():
            carry[...] = jnp.zeros_like(carry)
            db_ref[...] = jnp.zeros_like(db_ref)

        lane = lax.broadcasted_iota(jnp.int32, (tb, LANES), 1)
        dc = dc_ref[0] + dr_ref[0]
        for k in range(1, 4):
            dc = dc + (dc_ref[k] + dr_ref[k])
        hi, mid, lo = _split3(dc)
        rows = lax.broadcasted_iota(jnp.int32, (tb, tb), 0)
        cols = lax.broadcasted_iota(jnp.int32, (tb, tb), 1)
        triu = (cols >= rows).astype(BF16)
        dlogf = _dot(triu, hi) + _dot(triu, mid) + _dot(triu, lo) + carry[0:1, :]
        carry[...] = jnp.broadcast_to(dlogf[0:1, :], carry.shape)
        u = f_ref[...] + b_ref[...]
        dfb = jnp.where(lane < N_HEADS, dlogf * _sigmoid(-u), 0.0)
        df_ref[...] = dfb.astype(BF16)
        sub = lax.broadcasted_iota(jnp.int32, (SUBLANES, LANES), 0)
        db_ref[...] += jnp.where(sub == 0, jnp.sum(dfb, axis=0, keepdims=True), 0.0)

    return pl.pallas_call(
        body, name="fox_cumsum_bwd", grid=(nb,),
        out_shape=[jax.ShapeDtypeStruct((s, LANES), BF16), jax.ShapeDtypeStruct((SUBLANES, LANES), F32)],
        in_specs=[pl.BlockSpec((4, tb, LANES), lambda i: (0, nb - 1 - i, 0)),
                  pl.BlockSpec((4, tb, LANES), lambda i: (0, nb - 1 - i, 0)),
                  pl.BlockSpec((tb, LANES), lambda i: (nb - 1 - i, 0)), _const_spec((1, LANES))],
        out_specs=[pl.BlockSpec((tb, LANES), lambda i: (nb - 1 - i, 0)),
                   pl.BlockSpec((SUBLANES, LANES), lambda i: (0, 0))],
        scratch_shapes=[pltpu.VMEM((SUBLANES, LANES), F32)],
        compiler_params=_params(("arbitrary",)),
    )(dcum_k, dcum_q, f, bf_pad)


def _dh_norm_bwd(d_a, d_q, d_k, d_v, d_f, d_g, w_t, x, dx2, gnorm, scale1, tm=512):
    s = x.shape[0]
    nt = s // tm

    def body(da_ref, dq_ref, dk_ref, dv_ref, df_ref, dg_ref, w_ref, x_ref, dx2_ref, g_ref, sc_ref, gx_ref, vec_ref,
             a_sh, a_sc, a_g):
        step = pl.program_id(0)

        @pl.when(step == 0)
        def _():
            a_sh[...] = jnp.zeros_like(a_sh)
            a_sc[...] = jnp.zeros_like(a_sc)
            a_g[...] = jnp.zeros_like(a_g)

        def fold(v):
            return jnp.sum(v.reshape(tm // SUBLANES, SUBLANES, D_MODEL), axis=0)

        d_all = jnp.concatenate([da_ref[...], dq_ref[...], dk_ref[...], dv_ref[...], df_ref[...], dg_ref[...]], axis=1)
        dh = _dot(d_all, w_ref[...])
        xv = x_ref[...]
        r = lax.rsqrt(jnp.mean(xv * xv, axis=-1, keepdims=True) + NORM_EPS)
        xn = xv * r
        gn = g_ref[...]
        a_sh[...] += fold(dh)
        a_sc[...] += fold(dh * (xn * gn))
        dn1 = dh * sc_ref[...]
        a_g[...] += fold(dn1 * xn)
        dxn = dn1 * gn
        gx_ref[...] = dx2_ref[...] + r * (dxn - xn * jnp.mean(dxn * xn, axis=-1, keepdims=True))

        @pl.when(step == nt - 1)
        def _():
            sub = lax.broadcasted_iota(jnp.int32, (SUBLANES, D_MODEL), 0)
            v_sh = jnp.sum(a_sh[...], axis=0, keepdims=True)
            v_sc = jnp.sum(a_sc[...], axis=0, keepdims=True)
            v_g = jnp.sum(a_g[...], axis=0, keepdims=True)
            vec_ref[...] = jnp.where(sub == 0, v_sh, jnp.where(sub == 1, v_sc, jnp.where(sub == 2, v_g, 0.0)))

    row = lambda w: pl.BlockSpec((tm, w), lambda i: (i, 0))
    return pl.pallas_call(
        body, name="dh_norm_bwd", grid=(nt,),
        out_shape=[jax.ShapeDtypeStruct((s, D_MODEL), F32), jax.ShapeDtypeStruct((SUBLANES, D_MODEL), F32)],
        in_specs=[row(W_A), row(512), row(512), row(512), row(W_F), row(W_G), _const_spec((W_INT, D_MODEL)),
                  row(D_MODEL), row(D_MODEL), _const_spec((1, D_MODEL)), _const_spec((1, D_MODEL))],
        out_specs=[row(D_MODEL), pl.BlockSpec((SUBLANES, D_MODEL), lambda i: (0, 0))],
        scratch_shapes=[pltpu.VMEM((SUBLANES, D_MODEL), F32)] * 3,
        compiler_params=_params(("arbitrary",), VMEM_LIMIT),
    )(d_a, d_q, d_k, d_v, d_f, d_g, w_t, x, dx2, gnorm, scale1)


def _dw_in(h_t, ds, ts=1024, tc=512):
    n = len(ds)
    s = h_t.shape[1]
    ns = s // ts

    def body(*refs):
        h_ref, d_refs, o_refs = refs[0], refs[1:1 + n], refs[1 + n:1 + 2 * n]
        accs, sem = refs[1 + 2 * n:1 + 3 * n], refs[1 + 3 * n]
        k = pl.program_id(0)

        @pl.when(k == 0)
        def _():
            for acc in accs:
                acc[...] = jnp.zeros_like(acc)

        hv = h_ref[...]
        for acc, d_ref in zip(accs, d_refs):
            width = acc.shape[1]
            for c0 in range(0, width, tc):
                c1 = min(c0 + tc, width)
                acc[:, c0:c1] += _dot(hv, d_ref[:, c0:c1])

        @pl.when(k == ns - 1)
        def _():
            copies = [pltpu.make_async_copy(acc, o_ref, sem.at[g]) for g, (acc, o_ref) in enumerate(zip(accs, o_refs))]
            for cp in copies:
                cp.start()
            for cp in copies:
                cp.wait()

    return pl.pallas_call(
        body, name="dw_in", grid=(ns,),
        out_shape=[jax.ShapeDtypeStruct((D_MODEL, d.shape[1]), F32) for d in ds],
        in_specs=[pl.BlockSpec((D_MODEL, ts), lambda k: (0, k))] + [pl.BlockSpec((ts, d.shape[1]), lambda k: (k, 0)) for d in ds],
        out_specs=[pl.BlockSpec(memory_space=pl.ANY)] * n,
        scratch_shapes=[pltpu.VMEM((D_MODEL, d.shape[1]), F32) for d in ds] + [pltpu.SemaphoreType.DMA((n,))],
        compiler_params=_params(("arbitrary",), VMEM_LIMIT),
    )(h_t, *ds)


def _small_grads(packs, c_t, dada_shard):
    def body(p_ref, ct_ref, da_ref, sum_ref, gw_ref):
        acc = p_ref[0]
        for dev in range(1, 8):
            acc = acc + p_ref[dev]
        sum_ref[...] = acc
        gw_ref[...] = jnp.dot(ct_ref[...], da_ref[...], preferred_element_type=F32, precision=lax.Precision.HIGHEST)

    return pl.pallas_call(
        body, name="small_grads",
        out_shape=[jax.ShapeDtypeStruct(packs.shape[1:], F32),
                   jax.ShapeDtypeStruct((c_t.shape[0], dada_shard.shape[1]), F32)],
    )(packs, c_t, dada_shard)


def _adamw_body(w_ref, g_ref, m_ref, v_ref, d_ref, mo_ref, vo_ref):
    c1 = 1.0 / (1.0 - ADAM_B1 ** ADAM_STEP)
    c2 = 1.0 / (1.0 - ADAM_B2 ** ADAM_STEP)
    gv = g_ref[...]
    mn = ADAM_B1 * m_ref[...] + (1.0 - ADAM_B1) * gv
    vn = ADAM_B2 * v_ref[...] + (1.0 - ADAM_B2) * (gv * gv)
    mo_ref[...] = mn
    vo_ref[...] = vn
    d_ref[...] = -ADAM_LR * ((mn * c1) / (jnp.sqrt(vn * c2) + ADAM_EPS) + ADAM_WD * w_ref[...])


def _adamw3(w, g, m, v, name, tb=128):
    spec = pl.BlockSpec((tb, SUBLANES, LANES), lambda i: (i, 0, 0))
    return pl.pallas_call(
        functools.partial(_adamw_body), name=name, grid=(pl.cdiv(w.shape[0], tb),),
        out_shape=[jax.ShapeDtypeStruct(w.shape, F32)] * 3,
        in_specs=[spec] * 4, out_specs=[spec] * 3,
        compiler_params=_params(("parallel",)),
    )(w, g, m, v)


def _adamw(w, g, m, v, name):
    r, c = w.shape
    tr = 128 if r % 128 == 0 else r
    body = functools.partial(_adamw_body)
    spec = pl.BlockSpec((tr, c), lambda i: (i, 0))
    return pl.pallas_call(
        body, name=name, grid=(r // tr,),
        out_shape=[jax.ShapeDtypeStruct((r, c), F32)] * 3,
        in_specs=[spec] * 4, out_specs=[spec] * 3,
        compiler_params=_params(("parallel",)),
    )(w, g, m, v)


def _rope_inputs(positions):
    inv_freq = 10000.0 ** (-jnp.arange(0, HEAD_DIM, 2, dtype=F32) / HEAD_DIM)
    pos = jnp.broadcast_to(positions.astype(F32)[:, None], (positions.shape[0], LANES))
    return pos, jnp.tile(inv_freq, 4)[None, :]


def _pad_rows(v, rows=SUBLANES):
    return jnp.pad(v, ((0, rows - v.shape[0]), (0, 0)))


def kernel(x, c, positions, w_ada, b_ada, g_norm, w_in, b_f, sinks, w_o_swa, w_o_fox, w_out, g_final, loss_target, m_w_ada, m_b_ada, m_g_norm, m_w_in, m_b_f, m_sinks, m_w_o_swa, m_w_o_fox, m_w_out, m_g_final, v_w_ada, v_b_ada, v_g_norm, v_w_in, v_b_f, v_sinks, v_w_o_swa, v_w_o_fox, v_w_out, v_g_final):
    ix, iy, ic = lax.axis_index("x"), lax.axis_index("y"), lax.axis_index("c")
    chip = 2 * ix + iy
    dev = 2 * chip + ic
    xs, tgt = x[0], loss_target[0]
    s = xs.shape[0]

    b_ada_shard = lax.dynamic_slice(b_ada, (0, chip * 768), (1, 768))
    ada_parts, g_in, g_oa, g_ob, g_out = _gather_inputs(
        _pad_rows(c), w_ada[0], b_ada_shard, [w_in[0], w_o_swa[0], w_o_fox[0], w_out[0]], "gather_inputs")
    ada = lax.dynamic_index_in_dim(ada_parts, dev, axis=1, keepdims=False).reshape(1, 3 * D_MODEL)
    shift, scale, gate = ada[:, :D_MODEL], ada[:, D_MODEL:2 * D_MODEL], ada[:, 2 * D_MODEL:]
    scale1 = 1.0 + scale

    w_ref_order = jnp.transpose(g_in, (1, 0, 2)).reshape(D_MODEL, R_END)
    w_int = jnp.concatenate([
        w_ref_order[:, :R_ZA], w_ref_order[:, R_QB:R_FB], w_ref_order[:, R_FB:R_ZB],
        jnp.zeros((D_MODEL, W_F - N_HEADS), BF16), w_ref_order[:, R_ZA:R_QB], w_ref_order[:, R_ZB:]], axis=1)
    w_int_t = w_int.T
    wo_a = jnp.transpose(g_oa, (1, 0, 2)).reshape(512, D_MODEL)
    wo_b = jnp.transpose(g_ob, (1, 0, 2)).reshape(512, D_MODEL)
    w_o = g_out.reshape(D_MODEL, D_MODEL)

    pos, freq = _rope_inputs(positions[0])
    bf_pad = jnp.pad(b_f, ((0, 0), (0, LANES - N_HEADS)))
    sink_vec = sinks[0]

    a, b, f, g, h_t, cos, sin = _norm_proj(xs, g_norm * scale1, shift, w_int, pos, freq)
    att_a, l_swa = _swa_fwd(a, sink_vec)
    cum = _fox_cumsum(f, bf_pad)
    qa, ka, va, stats = _fox_prep(b, cum, FOX_TILE)
    ranges = _fox_tile_ranges(stats)
    att_b, lse = _fox_fwd(qa, ka, va, ranges, FOX_TILE)

    dx2, datt_a, datt_b, d_g, delta8, dwo_a, dwo_b, dw_out, vec_mid = _mid(
        att_a, att_b, g, xs, tgt, gate, g_final.reshape(1, D_MODEL), wo_a, wo_b, w_o)
    delta = jnp.pad(delta8.reshape(4, 2, s), ((0, 0), (0, SUBLANES - 2), (0, 0)))
    d_a, dsink = _swa_bwd(a, datt_a, l_swa, sink_vec, cos, sin)
    dq, dk, dv, dcum_k, dcum_q = _fox_bwd(qa, ka, b, datt_b, lse, delta, ranges, FOX_TILE)
    d_f, dbf = _fox_cumsum_bwd(dcum_k, dcum_q, f, bf_pad)
    grad_x, vec_dh = _dh_norm_bwd(d_a, dq, dk, dv, d_f, d_g, w_int_t, xs, dx2, g_norm, scale1)
    dw_a, dw_q, dw_k, dw_v, dw_f, dw_g = _dw_in(h_t, [d_a, dq, dk, dv, d_f, d_g])
    dw_in = jnp.concatenate([dw_a, dw_g[:, :512], dw_q, dw_k, dw_v, dw_f[:, :N_HEADS], dw_g[:, 512:]], axis=1)

    tail = jnp.pad(jnp.concatenate([dbf[0:1, :N_HEADS], dsink[0:1, :N_HEADS]], axis=1), ((0, 0), (0, D_MODEL - 2 * N_HEADS)))
    pack = jnp.concatenate([c, vec_dh[0:2], vec_mid[1:2], vec_dh[2:3], vec_mid[0:1], tail, vec_mid[2:3]], axis=0)

    def slots(w, axis):
        if axis == 1:
            return jnp.transpose(w.reshape(w.shape[0], 4, w.shape[1] // 4), (1, 0, 2))
        return w.reshape(4, w.shape[0] // 4, w.shape[1])

    packs, g_wo_a, g_wo_b, g_w_out, g_w_in = _reduce_scatter(
        [slots(dwo_a, 1), slots(dwo_b, 1), slots(dw_out, 0), slots(dw_in, 1)], pack, "reduce_grads")
    dada_all = packs[:, 1:4, :].reshape(8, 3 * D_MODEL)
    dada_shard = lax.dynamic_slice(dada_all, (0, chip * 768), (8, 768))
    sums, g_w_ada = _small_grads(packs, packs[:, 0, :].T, dada_shard)
    g_b_ada = sums[1:4].reshape(1, 3 * D_MODEL)
    g_g_norm = sums[4:5]
    g_g_final = sums[5]
    g_b_f = sums[6:7, :N_HEADS]
    g_sinks = sums[6:7, N_HEADS:2 * N_HEADS]
    loss = sums[7, 0]

    grads = {
        "w_ada": g_w_ada, "b_ada": g_b_ada, "g_norm": g_g_norm, "w_in": g_w_in, "b_f": g_b_f, "sinks": g_sinks,
        "w_o_swa": g_wo_a, "w_o_fox": g_wo_b, "w_out": g_w_out, "g_final": g_g_final,
    }
    params = {
        "w_ada": (w_ada, m_w_ada, v_w_ada), "b_ada": (b_ada, m_b_ada, v_b_ada), "g_norm": (g_norm, m_g_norm, v_g_norm),
        "w_in": (w_in, m_w_in, v_w_in), "b_f": (b_f, m_b_f, v_b_f), "sinks": (sinks, m_sinks, v_sinks),
        "w_o_swa": (w_o_swa, m_w_o_swa, v_w_o_swa), "w_o_fox": (w_o_fox, m_w_o_fox, v_w_o_fox),
        "w_out": (w_out, m_w_out, v_w_out), "g_final": (g_final, m_g_final, v_g_final),
    }
    n_col = w_in.shape[2]

    def as_stored(t):
        return jnp.transpose(t, (2, 0, 1)).reshape(n_col, SUBLANES, LANES)

    def from_stored(t):
        return jnp.transpose(t, (1, 2, 0)).reshape(1, D_MODEL, n_col)

    names = list(grads)
    out_g, out_d, out_m, out_v = [], [], [], []
    for nm in names:
        w, m, v = params[nm]
        if nm == "w_in":
            g_st = as_stored(grads[nm][None])
            d_, m_, v_ = _adamw3(as_stored(w), g_st, as_stored(m), as_stored(v), "adamw_" + nm)
            res = [from_stored(t) for t in (g_st, d_, m_, v_)]
        else:
            shape2 = (w.shape[-2], w.shape[-1]) if w.ndim >= 2 else (1, w.shape[0])
            d_, m_, v_ = _adamw(w.reshape(shape2), grads[nm].reshape(shape2), m.reshape(shape2), v.reshape(shape2), "adamw_" + nm)
            res = [t.reshape(w.shape) for t in (grads[nm], d_, m_, v_)]
        out_g.append(res[0])
        out_d.append(res[1])
        out_m.append(res[2])
        out_v.append(res[3])
    return (loss, grad_x[None], *out_g, *out_d, *out_m, *out_v)
```

```python
import functools

import numpy as np
import jax
import jax.numpy as jnp
from jax import lax
from jax.experimental import pallas as pl
from jax.experimental.pallas import tpu as pltpu

F32 = jnp.float32
BF16 = jnp.bfloat16
MESH = pl.DeviceIdType.MESH

D_MODEL = 1024
HEAD_DIM = 64
N_HEADS = 8
WINDOW = 128
NORM_EPS = 1e-6
SCALE = HEAD_DIM ** -0.5
NEG = -1e30
LANES = 128
SUBLANES = 8
VMEM_LIMIT = 60 * 1024 * 1024
FOX_TILE = 512

W_A, W_B, W_F, W_G = 768, 1536, 128, 3072
OFF_A, OFF_B, OFF_F, OFF_G = 0, 768, 2304, 2432
W_INT = W_A + W_B + W_F + W_G
R_ZA, R_QB, R_FB, R_ZB, R_END = 768, 1280, 2816, 2824, 5384

ADAM_LR, ADAM_B1, ADAM_B2, ADAM_EPS, ADAM_WD, ADAM_STEP = 0.001, 0.9, 0.999, 1e-08, 0.01, 10

NT = (((1,), (1,)), ((), ()))
TN = (((0,), (0,)), ((), ()))


def _dot(a, b, dims=None):
    if dims is None:
        return jnp.dot(a, b, preferred_element_type=F32)
    return lax.dot_general(a, b, dims, preferred_element_type=F32)


def _split3(v):
    hi = v.astype(BF16)
    r1 = v - hi.astype(F32)
    mid = r1.astype(BF16)
    lo = (r1 - mid.astype(F32)).astype(BF16)
    return hi, mid, lo


def _sigmoid(v):
    return 1.0 / (1.0 + jnp.exp(-v))


def _params(sem=None, vmem=None):
    return pltpu.CompilerParams(dimension_semantics=sem, vmem_limit_bytes=vmem)


def _const_spec(shape):
    nd = len(shape)
    return pl.BlockSpec(shape, lambda *_: (0,) * nd, pipeline_mode=pl.Buffered(1))


def _flip(v, f):
    return 1 - v if f else v


_CHIP_FLIPS = ((1, 0), (0, 1), (1, 1))


def _gather_inputs(c_pad, w_ada, b_ada_shard, shards, name):
    n = len(shards)
    n_col = w_ada.shape[1]

    def body(*refs):
        c_ref, wa_ref, ba_ref = refs[:3]
        ins = refs[3:3 + n]
        ada_ref = refs[3 + n]
        outs = refs[4 + n:4 + 2 * n]
        call_ref, send_sems, recv_sems = refs[4 + 2 * n:7 + 2 * n]
        x, y, c = lax.axis_index("x"), lax.axis_index("y"), lax.axis_index("c")
        k_me = 2 * x + y
        me = 2 * k_me + c
        sibling = (x, y, 1 - c)
        chips = [(_flip(x, fx), _flip(y, fy)) for fx, fy in _CHIP_FLIPS]

        def piece(i, chip_k, half):
            hr = ins[i].shape[0] // 2
            return outs[i].at[chip_k, pl.ds(half * hr, hr), :]

        def copy(i, slot, chip_k, half, to):
            return pltpu.make_async_remote_copy(
                src_ref=piece(i, chip_k, half), dst_ref=piece(i, chip_k, half),
                send_sem=send_sems.at[6 * i + slot], recv_sem=recv_sems.at[6 * i + slot],
                device_id=to, device_id_type=MESH)

        def small(ref, slot, sem, to):
            return pltpu.make_async_remote_copy(
                src_ref=ref.at[slot], dst_ref=ref.at[slot], send_sem=send_sems.at[6 * n + sem],
                recv_sem=recv_sems.at[6 * n + sem], device_id=to, device_id_type=MESH)

        for i in range(n):
            outs[i][k_me] = ins[i][...].astype(BF16)
        started = []
        for i in range(n):
            for j, chip in enumerate(chips):
                cp = copy(i, j, k_me, c, (chip[0], chip[1], c))
                cp.start()
                started.append(cp)

        call_ref[me] = c_ref[...]
        peers = [(_flip(x, k & 4), _flip(y, k & 2), _flip(c, k & 1)) for k in range(1, 8)]
        for k, peer in enumerate(peers):
            cp = small(call_ref, me, k, peer)
            cp.start()
            started.append(cp)
        for k, peer in enumerate(peers):
            small(call_ref, 4 * peer[0] + 2 * peer[1] + peer[2], k, peer).wait_recv()
        c_all = call_ref[:, 0, :].astype(BF16)
        ada_ref[k_me] = _dot(c_all, wa_ref[...].astype(BF16)) + ba_ref[...]
        for j, chip in enumerate(chips):
            cp = small(ada_ref, k_me, 7 + j, (chip[0], chip[1], c))
            cp.start()
            started.append(cp)

        for j, chip in enumerate(chips):
            chip_k = 2 * chip[0] + chip[1]
            for i in range(n):
                copy(i, j, chip_k, c, (chip[0], chip[1], c)).wait_recv()
                cp = copy(i, 3 + j, chip_k, c, sibling)
                cp.start()
                started.append(cp)
        for j, chip in enumerate(chips):
            chip_k = 2 * chip[0] + chip[1]
            small(ada_ref, chip_k, 7 + j, (chip[0], chip[1], c)).wait_recv()
            for i in range(n):
                copy(i, 3 + j, chip_k, 1 - c, sibling).wait_recv()
        for cp in started:
            cp.wait_send()

    vmem = pl.BlockSpec(memory_space=pltpu.VMEM)
    return pl.pallas_call(
        body, name=name,
        out_shape=[jax.ShapeDtypeStruct((4, 8, n_col), F32)] + [jax.ShapeDtypeStruct((4,) + s.shape, BF16) for s in shards],
        in_specs=[vmem] * (3 + n),
        out_specs=[vmem] * (1 + n),
        scratch_shapes=[pltpu.VMEM((8,) + c_pad.shape, F32),
                        pltpu.SemaphoreType.DMA((6 * n + 10,)), pltpu.SemaphoreType.DMA((6 * n + 10,))],
        compiler_params=_params(vmem=VMEM_LIMIT),
    )(c_pad, w_ada, b_ada_shard, *shards)


def _reduce_scatter(pieces, pack, name):
    n = len(pieces)

    def body(*refs):
        pack_ref, ins = refs[0], refs[1:1 + n]
        packs_ref, outs = refs[1 + n], refs[2 + n:2 + 2 * n]
        rest = refs[2 + 2 * n:]
        own, got = rest[:n], rest[n:2 * n]
        sendb, recvb = rest[2 * n:3 * n], rest[3 * n:4 * n]
        send_sems, recv_sems, local_sems = rest[4 * n:4 * n + 3]
        x, y, c = lax.axis_index("x"), lax.axis_index("y"), lax.axis_index("c")
        k_me = 2 * x + y
        me = 2 * k_me + c
        sibling = (x, y, 1 - c)
        chips = [(_flip(x, fx), _flip(y, fy)) for fx, fy in _CHIP_FLIPS]
        hrs = [p.shape[1] // 2 for p in pieces]

        def remote(i, slot, src, dst, to):
            return pltpu.make_async_remote_copy(
                src_ref=src, dst_ref=dst, send_sem=send_sems.at[5 * i + slot], recv_sem=recv_sems.at[5 * i + slot],
                device_id=to, device_id_type=MESH)

        started = []
        packs_ref[me] = pack_ref[...]
        peers = [(_flip(x, k & 4), _flip(y, k & 2), _flip(c, k & 1)) for k in range(1, 8)]
        for k, peer in enumerate(peers):
            cp = pltpu.make_async_remote_copy(
                src_ref=pack_ref, dst_ref=packs_ref.at[me], send_sem=send_sems.at[5 * n + k],
                recv_sem=recv_sems.at[5 * n + k], device_id=peer, device_id_type=MESH)
            cp.start()
            started.append(cp)
        loads = []
        for i in range(n):
            ld = pltpu.make_async_copy(ins[i].at[:, pl.ds(c * hrs[i], hrs[i]), :], own[i], local_sems.at[i])
            ld.start()
            loads.append(ld)
            cp = remote(i, 0, ins[i].at[:, pl.ds((1 - c) * hrs[i], hrs[i]), :], got[i], sibling)
            cp.start()
            started.append(cp)
        for i in range(n):
            loads[i].wait()
            remote(i, 0, ins[i].at[:, pl.ds(c * hrs[i], hrs[i]), :], got[i], sibling).wait_recv()
            for j, chip in enumerate(chips):
                chip_k = 2 * chip[0] + chip[1]
                sendb[i][j] = (own[i][chip_k] + got[i][chip_k]).astype(BF16)
                cp = remote(i, 1 + j, sendb[i].at[j], recvb[i].at[j], (chip[0], chip[1], c))
                cp.start()
                started.append(cp)
        for i in range(n):
            acc = own[i][k_me] + got[i][k_me]
            for j, chip in enumerate(chips):
                remote(i, 1 + j, sendb[i].at[j], recvb[i].at[j], (chip[0], chip[1], c)).wait_recv()
                acc = acc + recvb[i][j].astype(F32)
            mine = outs[i].at[pl.ds(c * hrs[i], hrs[i]), :]
            outs[i][pl.ds(pl.multiple_of(c * hrs[i], SUBLANES), hrs[i]), :] = acc
            cp = remote(i, 4, mine, mine, sibling)
            cp.start()
            started.append(cp)
        for i in range(n):
            theirs = outs[i].at[pl.ds((1 - c) * hrs[i], hrs[i]), :]
            remote(i, 4, theirs, theirs, sibling).wait_recv()
        for k, peer in enumerate(peers):
            pltpu.make_async_remote_copy(
                src_ref=pack_ref, dst_ref=packs_ref.at[4 * peer[0] + 2 * peer[1] + peer[2]],
                send_sem=send_sems.at[5 * n + k], recv_sem=recv_sems.at[5 * n + k],
                device_id=peer, device_id_type=MESH).wait_recv()
        for cp in started:
            cp.wait_send()

    vmem = pl.BlockSpec(memory_space=pltpu.VMEM)
    scratch = []
    scratch += [pltpu.VMEM((4, p.shape[1] // 2, p.shape[2]), F32) for p in pieces]
    scratch += [pltpu.VMEM((4, p.shape[1] // 2, p.shape[2]), F32) for p in pieces]
    scratch += [pltpu.VMEM((3, p.shape[1] // 2, p.shape[2]), BF16) for p in pieces]
    scratch += [pltpu.VMEM((3, p.shape[1] // 2, p.shape[2]), BF16) for p in pieces]
    scratch += [pltpu.SemaphoreType.DMA((5 * n + 7,)), pltpu.SemaphoreType.DMA((5 * n + 7,)), pltpu.SemaphoreType.DMA((n,))]
    return pl.pallas_call(
        body, name=name,
        out_shape=[jax.ShapeDtypeStruct((8,) + pack.shape, F32)] + [jax.ShapeDtypeStruct(p.shape[1:], F32) for p in pieces],
        in_specs=[vmem] + [pl.BlockSpec(memory_space=pl.ANY)] * n,
        out_specs=[vmem] * (1 + n),
        scratch_shapes=scratch,
        compiler_params=_params(vmem=VMEM_LIMIT),
    )(pack, *pieces)


def _rope_fwd(t, cos, sin, lane):
    lo = (lane % HEAD_DIM) < (HEAD_DIM // 2)
    return t * cos + jnp.where(lo, -pltpu.roll(t, 96, 1), pltpu.roll(t, 32, 1)) * sin


def _norm_proj(x, gmod, shift, w_int, pos, freq, bf_pad, tm):
    s = x.shape[0]

    def body(x_ref, g_ref, sh_ref, w_ref, pos_ref, fr_ref, bf_ref,
             a_ref, vb_ref, f_ref, gg_ref, ht_ref, cos_ref, sin_ref, q_ref, k_ref, v_ref, st_ref, carry):
        @pl.when(pl.program_id(0) == 0)
        def _():
            carry[...] = jnp.zeros_like(carry)

        xv = x_ref[...]
        r = lax.rsqrt(jnp.mean(xv * xv, axis=-1, keepdims=True) + NORM_EPS)
        hf = (xv * r) * g_ref[...] + sh_ref[...]
        hb = hf.astype(BF16)
        ht_ref[...] = hf.T.astype(BF16)
        pa = _dot(hb, w_ref[:, OFF_A:OFF_A + W_A])
        ang = pos_ref[...] * fr_ref[...]
        cosv, sinv = jnp.cos(ang), jnp.sin(ang)
        cos_ref[...] = cosv
        sin_ref[...] = sinv
        lane = lax.broadcasted_iota(jnp.int32, (tm, LANES), 1)
        for j in range(5):
            t = pa[:, LANES * j:LANES * (j + 1)]
            a_ref[:, LANES * j:LANES * (j + 1)] = _rope_fwd(t, cosv, sinv, lane).astype(BF16)
        a_ref[:, 640:768] = pa[:, 640:768].astype(BF16)
        pf = _dot(hb, w_ref[:, OFF_F:OFF_F + W_F])
        f_ref[...] = pf
        bblk = _dot(hb, w_ref[:, OFF_B:OFF_B + W_B]).astype(BF16)
        vb_ref[...] = bblk[:, 1024:1536]
        gg_ref[...] = _dot(hb, w_ref[:, OFF_G:OFF_G + W_G]).astype(BF16)
        _augment_heads(bblk, _cumsum_tile(pf, bf_ref[...], carry), q_ref, k_ref, v_ref, st_ref)

    row = lambda w: pl.BlockSpec((tm, w), lambda i: (i, 0))
    return pl.pallas_call(
        body, name="norm_proj", grid=(s // tm,),
        out_shape=[jax.ShapeDtypeStruct((s, W_A), BF16), jax.ShapeDtypeStruct((s, 512), BF16),
                   jax.ShapeDtypeStruct((s, W_F), F32), jax.ShapeDtypeStruct((s, W_G), BF16),
                   jax.ShapeDtypeStruct((D_MODEL, s), BF16),
                   jax.ShapeDtypeStruct((s, LANES), F32), jax.ShapeDtypeStruct((s, LANES), F32)]
        + [jax.ShapeDtypeStruct((s, 1024), BF16)] * 3 + [jax.ShapeDtypeStruct((s // tm, SUBLANES, LANES), F32)],
        in_specs=[row(D_MODEL), _const_spec((1, D_MODEL)), _const_spec((1, D_MODEL)), _const_spec((D_MODEL, W_INT)),
                  row(LANES), _const_spec((1, LANES)), _const_spec((1, LANES))],
        out_specs=[row(W_A), row(512), row(W_F), row(W_G), pl.BlockSpec((D_MODEL, tm), lambda i: (0, i)),
                   row(LANES), row(LANES), row(1024), row(1024), row(1024),
                   pl.BlockSpec((1, SUBLANES, LANES), lambda i: (i, 0, 0))],
        scratch_shapes=[pltpu.VMEM((SUBLANES, LANES), F32)],
        compiler_params=_params(("arbitrary",), VMEM_LIMIT),
    )(x, gmod, shift, w_int, pos, freq, bf_pad)


def _log_sigmoid(u):
    return jnp.minimum(u, 0.0) - jnp.log(1.0 + jnp.exp(-jnp.abs(u)))


def _cumsum_tile(f, b_f, carry):
    tb = f.shape[0]
    lane = lax.broadcasted_iota(jnp.int32, (tb, LANES), 1)
    logf = jnp.where(lane < N_HEADS, _log_sigmoid(f + b_f), 0.0)
    hi, mid, lo = _split3(logf)
    rows = lax.broadcasted_iota(jnp.int32, (tb, tb), 0)
    cols = lax.broadcasted_iota(jnp.int32, (tb, tb), 1)
    tril = (cols <= rows).astype(BF16)
    cum = _dot(tril, hi) + _dot(tril, mid) + _dot(tril, lo) + carry[0:1, :]
    carry[...] = jnp.broadcast_to(cum[tb - 1:tb, :], carry.shape)
    return cum


def _aug_lane(h):
    return 64 if h % 2 == 0 else 0


def _augment_heads(bblk, cumv, q_ref, k_ref, v_ref, st_ref):
    t = bblk.shape[0]
    lane = lax.broadcasted_iota(jnp.int32, (t, LANES), 1)
    lane_b = lane.astype(BF16)
    sub8 = lax.broadcasted_iota(jnp.int32, (SUBLANES, LANES), 0)
    lane8 = lax.broadcasted_iota(jnp.int32, (SUBLANES, LANES), 1)
    one = jnp.ones((t, LANES), BF16)
    zero = jnp.zeros((t, LANES), BF16)
    stats = jnp.zeros((SUBLANES, LANES), F32)
    for p in range(4):
        qblk = bblk[:, LANES * p:LANES * (p + 1)] * SCALE
        kblk = bblk[:, 512 + LANES * p:512 + LANES * (p + 1)]
        vblk = bblk[:, 1024 + LANES * p:1024 + LANES * (p + 1)]
        qf, kf = qblk.astype(F32), kblk.astype(F32)
        q2, k2, qk = qf * qf, kf * kf, qf * kf
        for odd in range(2):
            h = 2 * p + odd
            a0 = _aug_lane(h)
            data_b = (lane_b < 64) if odd == 0 else (lane_b >= 64)
            data = (lane < 64) if odd == 0 else (lane >= 64)
            hi, mid, lo = _split3(jnp.broadcast_to(cumv[:, h:h + 1], (t, LANES)))
            ones3_q = (lane_b >= a0 + 3) & (lane_b < a0 + 6)
            ones3_k = (lane_b >= a0) & (lane_b < a0 + 3)
            aug_q = jnp.where(lane_b == a0, hi, jnp.where(lane_b == a0 + 1, mid, jnp.where(
                lane_b == a0 + 2, lo, jnp.where(ones3_q, one, zero))))
            aug_k = jnp.where(ones3_k, one, jnp.where(lane_b == a0 + 3, -hi, jnp.where(
                lane_b == a0 + 4, -mid, jnp.where(lane_b == a0 + 5, -lo, zero))))
            q_ref[:, LANES * h:LANES * (h + 1)] = jnp.where(data_b, qblk, aug_q)
            k_ref[:, LANES * h:LANES * (h + 1)] = jnp.where(data_b, kblk, aug_k)
            v_ref[:, LANES * h:LANES * (h + 1)] = jnp.where(data_b, vblk, jnp.where(lane_b == a0, one, zero))
            qn = jnp.sqrt(jnp.max(jnp.sum(jnp.where(data, q2, 0.0), axis=-1, keepdims=True)))
            kn = jnp.sqrt(jnp.max(jnp.sum(jnp.where(data, k2, 0.0), axis=-1, keepdims=True)))
            dmin = jnp.min(jnp.sum(jnp.where(data, qk, 0.0), axis=-1, keepdims=True))
            c_first, c_last = cumv[0:1, h:h + 1], cumv[t - 1:t, h:h + 1]
            row = jnp.where(lane8 == 0, qn, jnp.where(lane8 == 1, kn, jnp.where(
                lane8 == 2, c_first, jnp.where(lane8 == 3, c_last, jnp.where(lane8 == 4, dmin, 0.0)))))
            stats = jnp.where(sub8 == h, row, stats)
    st_ref[0] = stats


PRUNE_MARGIN = 88.0


def _fox_tile_ranges(stats):
    nt = stats.shape[0]
    qn, kn, c_first, c_last, d_min = (stats[:, :, n] for n in range(5))
    bound = (1.01 * qn[:, None, :] * kn[None, :, :] - jnp.minimum(d_min, 0.0)[:, None, :] + 0.05
             + c_first[:, None, :] - c_last[None, :, :])
    idx = jnp.arange(nt)
    skip = (bound <= -PRUNE_MARGIN) & (idx[None, :, None] < idx[:, None, None])
    first_key = jnp.sum(jnp.cumprod(skip, axis=1), axis=1)
    needed = (idx[None, :, None] >= first_key[:, None, :]) & (idx[None, :, None] <= idx[:, None, None])
    last_query = jnp.max(jnp.where(needed, idx[:, None, None], 0), axis=0)
    n_query = last_query - idx[:, None] + 1
    table = jnp.zeros((4, SUBLANES, LANES), F32)
    for odd in range(2):
        table = table.at[:, odd, :nt].set(first_key[:, odd::2].T.astype(F32))
        table = table.at[:, 2 + odd, :nt].set(n_query[:, odd::2].T.astype(F32))
    return table


def _lane_scalar(block, row, lane_idx):
    sub8 = lax.broadcasted_iota(jnp.int32, (SUBLANES, LANES), 0)
    lane8 = lax.broadcasted_iota(jnp.int32, (SUBLANES, LANES), 1)
    return jnp.sum(jnp.where((sub8 == row) & (lane8 == lane_idx), block, 0.0)).astype(jnp.int32)


def _fox_fwd(qa, ka, va, ranges, t):
    s = qa.shape[0]
    nt = s // t
    nc = t // LANES

    def body(rg_ref, q_ref, k_ref, v_ref, o_ref, lse_ref):
        i = pl.program_id(1)
        lane = lax.broadcasted_iota(jnp.int32, (t, LANES), 1)
        rows = lax.broadcasted_iota(jnp.int32, (t, t), 0)
        cols = lax.broadcasted_iota(jnp.int32, (t, t), 1)
        firsts = [jnp.clip(_lane_scalar(rg_ref[0], hh, i), 0, i) for hh in range(2)]
        first = jnp.maximum(firsts[0], firsts[1])

        def update(js, carry, heads=(0, 1), diagonal=False):
            offs = [pl.multiple_of(j * t, t) for j in js]
            kts = [k_ref[pl.ds(off, t), :] for off in offs]
            vts = [v_ref[pl.ds(off, t), :] for off in offs]
            scs = {hh: [_dot(q_ref[:, LANES * hh:LANES * (hh + 1)], kt[:, LANES * hh:LANES * (hh + 1)], NT) for kt in kts]
                   for hh in heads}
            if diagonal:
                scs = {hh: [jnp.where(cols <= rows, sc, NEG) for sc in scs[hh]] for hh in heads}
            m_new = {}
            for hh in heads:
                part = None
                for sc in scs[hh]:
                    for cch in range(nc):
                        chunk = sc[:, LANES * cch:LANES * (cch + 1)]
                        part = chunk if part is None else jnp.maximum(part, chunk)
                m_new[hh] = jnp.maximum(carry[2 * hh], jnp.max(part, axis=-1, keepdims=True))
            alphas = {hh: jnp.exp(carry[2 * hh] - m_new[hh]) for hh in heads}
            ps = {hh: [jnp.exp(sc - m_new[hh]).astype(BF16) for sc in scs[hh]] for hh in heads}
            out = list(carry)
            for hh in heads:
                pv = None
                for p, vt in zip(ps[hh], vts):
                    term = _dot(p, vt[:, LANES * hh:LANES * (hh + 1)])
                    pv = term if pv is None else pv + term
                out[2 * hh], out[2 * hh + 1] = m_new[hh], alphas[hh] * carry[2 * hh + 1] + pv
            return tuple(out)

        col0 = jnp.full((t, 1), NEG, F32)
        zero = jnp.zeros((t, LANES), F32)
        carry = (col0, zero, col0, zero)
        for hh in range(2):
            carry = lax.fori_loop(firsts[hh], first, lambda j, cr, hh=hh: update([j], cr, heads=(hh,)), carry)
        n_off = i - first
        carry = lax.fori_loop(0, n_off // 2, lambda u, cr: update([first + 2 * u, first + 2 * u + 1], cr), carry)
        carry = lax.fori_loop(0, n_off % 2, lambda u, cr: update([i - 1], cr), carry)
        m0, acc0, m1, acc1 = update([i], carry, diagonal=True)
        l0, l1 = acc0[:, _aug_lane(0):_aug_lane(0) + 1], acc1[:, _aug_lane(1):_aug_lane(1) + 1]
        o_ref[...] = jnp.where(lane < 64, acc0 * (1.0 / l0), acc1 * (1.0 / l1)).astype(BF16)
        sub = lax.broadcasted_iota(jnp.int32, (SUBLANES, t), 0)
        lse0 = jnp.broadcast_to(m0 + jnp.log(l0), (t, LANES)).T[0:SUBLANES, :]
        lse1 = jnp.broadcast_to(m1 + jnp.log(l1), (t, LANES)).T[0:SUBLANES, :]
        lse_ref[0] = jnp.where(sub == 0, lse0, jnp.where(sub == 1, lse1, 0.0))

    pair = pl.BlockSpec((s, 2 * LANES), lambda p, i: (0, p))
    return pl.pallas_call(
        body, name="fox_fwd", grid=(4, nt),
        out_shape=[jax.ShapeDtypeStruct((s, 512), BF16), jax.ShapeDtypeStruct((4, SUBLANES, s), F32)],
        in_specs=[pl.BlockSpec((1, SUBLANES, LANES), lambda p, i: (p, 0, 0)),
                  pl.BlockSpec((t, 2 * LANES), lambda p, i: (i, p)), pair, pair],
        out_specs=[pl.BlockSpec((t, LANES), lambda p, i: (i, p)),
                   pl.BlockSpec((1, SUBLANES, t), lambda p, i: (p, 0, i))],
        compiler_params=_params(("parallel", "arbitrary"), VMEM_LIMIT),
    )(ranges, qa, ka, va)


def _dup_halves(blk, lane):
    f = blk.astype(F32)
    r = pltpu.roll(f, 64, 1)
    return jnp.where(lane < 64, f, r).astype(BF16), jnp.where(lane >= 64, f, r).astype(BF16)


GROUP = 4
GROUP_ROWS = GROUP * WINDOW


def _stack_heads(ref, g, lane):
    parts = []
    for pb in (2 * g, 2 * g + 1):
        blk = ref[:, LANES * pb:LANES * (pb + 1)]
        zero = jnp.zeros_like(blk)
        parts += [jnp.where(lane < 64, blk, zero), jnp.where(lane >= 64, blk, zero)]
    return jnp.concatenate(parts, axis=0)


def _swa_band(a_ref, ap_ref, g, lane):
    k = jnp.concatenate([_dup_halves(ap_ref[:, 512:640], lane)[g], _dup_halves(a_ref[:, 512:640], lane)[g]], axis=0)
    v = jnp.concatenate([_dup_halves(ap_ref[:, 640:768], lane)[g], _dup_halves(a_ref[:, 640:768], lane)[g]], axis=0)
    return k, v


def _swa_logits(q, k, has_prev):
    sc = _dot(q, k, NT) * SCALE
    rr = lax.broadcasted_iota(jnp.int32, sc.shape, 0) % WINDOW
    cc = lax.broadcasted_iota(jnp.int32, sc.shape, 1)
    valid = (cc > rr) & (cc <= rr + WINDOW) & (has_prev | (cc >= WINDOW))
    return jnp.where(valid, sc, NEG)


def _per_head_column(values):
    return jnp.concatenate([jnp.broadcast_to(v, (WINDOW, 1)) for v in values], axis=0)


SWA_BLOCKS = 4
SWA_ROWS = SWA_BLOCKS * WINDOW


def _swa_blocks(a_ref, ap_ref):
    return [ap_ref] + [a_ref.at[pl.ds(WINDOW * jb, WINDOW), :] for jb in range(SWA_BLOCKS)]


def _swa_fwd(a, sinks):
    s = a.shape[0]

    def body(sink_ref, a_ref, ap_ref, o_ref, l_ref):
        lane = lax.broadcasted_iota(jnp.int32, (WINDOW, LANES), 1)
        blocks = _swa_blocks(a_ref, ap_ref)
        units = [(jb, g) for jb in range(SWA_BLOCKS) for g in range(2)]
        sinks_col = [_per_head_column([sink_ref[GROUP * g + hh] for hh in range(GROUP)]) for g in range(2)]
        bands = [_swa_band(blocks[jb + 1], blocks[jb], g, lane) for jb, g in units]
        scs = [_swa_logits(_stack_heads(blocks[jb + 1], g, lane), bands[u][0],
                           (pl.program_id(0) > 0) if jb == 0 else True) for u, (jb, g) in enumerate(units)]
        ms = [jnp.maximum(jnp.max(scs[u], axis=-1, keepdims=True), sinks_col[g]) for u, (jb, g) in enumerate(units)]
        ps = [jnp.exp(scs[u] - ms[u]) for u in range(len(units))]
        dens = [jnp.sum(ps[u], axis=-1, keepdims=True) + jnp.exp(sinks_col[g] - ms[u]) for u, (jb, g) in enumerate(units)]
        outs = [_dot((ps[u] * (1.0 / dens[u])).astype(BF16), bands[u][1]) for u in range(len(units))]
        for jb in range(SWA_BLOCKS):
            rows = slice(WINDOW * jb, WINDOW * (jb + 1))
            l_all = jnp.zeros((WINDOW, LANES), F32)
            for g in range(2):
                u = 2 * jb + g
                lcol = ms[u] + jnp.log(dens[u])
                for pb in range(2):
                    r0 = 2 * pb * WINDOW
                    o_ref[rows, LANES * (2 * g + pb):LANES * (2 * g + pb + 1)] = jnp.where(
                        lane < 64, outs[u][r0:r0 + WINDOW], outs[u][r0 + WINDOW:r0 + 2 * WINDOW]).astype(BF16)
                for hh in range(GROUP):
                    l_all = jnp.where(lane == GROUP * g + hh, lcol[WINDOW * hh:WINDOW * (hh + 1)], l_all)
            l_ref[rows, :] = l_all

    return pl.pallas_call(
        body, name="swa_fwd", grid=(s // SWA_ROWS,),
        out_shape=[jax.ShapeDtypeStruct((s, 512), BF16), jax.ShapeDtypeStruct((s, LANES), F32)],
        in_specs=[pl.BlockSpec(memory_space=pltpu.SMEM),
                  pl.BlockSpec((SWA_ROWS, W_A), lambda i: (i, 0)),
                  pl.BlockSpec((WINDOW, W_A), lambda i: (jnp.maximum(SWA_BLOCKS * i - 1, 0), 0))],
        out_specs=[pl.BlockSpec((SWA_ROWS, 512), lambda i: (i, 0)), pl.BlockSpec((SWA_ROWS, LANES), lambda i: (i, 0))],
        compiler_params=_params(("parallel",)),
    )(sinks, a, a)


def _mid(att_a, att_b, g, x, target, gate, g_final, wo_a, wo_b, w_out, tm=256):
    s = x.shape[0]
    nt = s // tm

    def body(aa_ref, ab_ref, g_ref, x_ref, t_ref, gate_ref, gf_ref, woa_ref, wob_ref, wout_ref,
             dx_ref, daa_ref, dab_ref, dg_ref, delta_ref, dwoa_ref, dwob_ref, dwout_ref, vec_ref,
             acc_gf, acc_gate, acc_loss):
        step = pl.program_id(0)

        @pl.when(step == 0)
        def _():
            dwoa_ref[...] = jnp.zeros_like(dwoa_ref)
            dwob_ref[...] = jnp.zeros_like(dwob_ref)
            dwout_ref[...] = jnp.zeros_like(dwout_ref)
            acc_gf[...] = jnp.zeros_like(acc_gf)
            acc_gate[...] = jnp.zeros_like(acc_gate)
            acc_loss[...] = jnp.zeros_like(acc_loss)

        def fold(v):
            return jnp.sum(v.reshape(tm // SUBLANES, SUBLANES, D_MODEL), axis=0)

        gate = gate_ref[...]
        gfin = gf_ref[...]
        branches = []
        for att_ref, z_off, wo_ref in ((aa_ref, 0, woa_ref), (ab_ref, 512, wob_ref)):
            att = att_ref[...].astype(F32)
            z = g_ref[:, z_off:z_off + 512].astype(F32)
            sz = _sigmoid(z)
            silu = z * sz
            u = (att * silu).astype(BF16)
            branches.append((att, z, sz, silu, u, _dot(u, wo_ref[...])))
        ga = g_ref[:, 1024:2048].astype(F32)
        gb = g_ref[:, 2048:3072].astype(F32)
        sga, sgb = _sigmoid(ga), _sigmoid(gb)
        y_a, y_b = branches[0][5], branches[1][5]
        mb = (sga * y_a + sgb * y_b).astype(BF16)
        o = _dot(mb, wout_ref[...])
        x2 = x_ref[...] + gate * o
        r2 = lax.rsqrt(jnp.mean(x2 * x2, axis=-1, keepdims=True) + NORM_EPS)
        xn2 = x2 * r2
        err = xn2 * gfin - t_ref[...]
        acc_loss[...] += fold(err * err)
        dy = err * (1.0 / D_MODEL)
        acc_gf[...] += fold(dy * xn2)
        dxn = dy * gfin
        dx2 = r2 * (dxn - xn2 * jnp.mean(dxn * xn2, axis=-1, keepdims=True))
        dx_ref[...] = dx2
        acc_gate[...] += fold(dx2 * o)
        d_o = (dx2 * gate).astype(BF16)
        dwout_ref[...] += _dot(mb, d_o, TN)
        dm = _dot(d_o, wout_ref[...], NT)
        dg_ref[:, 1024:2048] = (dm * y_a * sga * (1.0 - sga)).astype(BF16)
        dg_ref[:, 2048:3072] = (dm * y_b * sgb * (1.0 - sgb)).astype(BF16)
        for (att, z, sz, silu, u, _), sg, wo_ref, dwo_ref, datt_ref, z_off in (
                (branches[0], sga, woa_ref, dwoa_ref, daa_ref, 0), (branches[1], sgb, wob_ref, dwob_ref, dab_ref, 512)):
            dyb = (dm * sg).astype(BF16)
            dwo_ref[...] += _dot(u, dyb, TN)
            du = _dot(dyb, wo_ref[...], NT)
            datt = du * silu
            datt_ref[...] = datt.astype(BF16)
            dg_ref[:, z_off:z_off + 512] = (du * att * (sz * (1.0 + z * (1.0 - sz)))).astype(BF16)
            if z_off == 512:
                prod = datt * att
                hi = prod.astype(BF16)
                lo = (prod - hi.astype(F32)).astype(BF16)
                er = lax.broadcasted_iota(jnp.int32, (512, LANES), 0)
                ec = lax.broadcasted_iota(jnp.int32, (512, LANES), 1)
                e = (er // HEAD_DIM == ec).astype(BF16)
                delta = _dot(hi, e) + _dot(lo, e)
                delta_ref[...] = delta.T[0:SUBLANES, :]

        @pl.when(step == nt - 1)
        def _():
            sub = lax.broadcasted_iota(jnp.int32, (SUBLANES, D_MODEL), 0)
            dgf = jnp.sum(acc_gf[...], axis=0, keepdims=True)
            dgate = jnp.sum(acc_gate[...], axis=0, keepdims=True)
            loss = 0.5 * jnp.sum(acc_loss[...]) * (1.0 / D_MODEL)
            vec_ref[...] = jnp.where(sub == 0, dgf, jnp.where(sub == 1, dgate, jnp.where(sub == 2, loss, 0.0)))

    row = lambda w: pl.BlockSpec((tm, w), lambda i: (i, 0))
    return pl.pallas_call(
        body, name="mid", grid=(nt,),
        out_shape=[jax.ShapeDtypeStruct((s, D_MODEL), F32), jax.ShapeDtypeStruct((s, 512), BF16),
                   jax.ShapeDtypeStruct((s, 512), BF16), jax.ShapeDtypeStruct((s, W_G), BF16),
                   jax.ShapeDtypeStruct((SUBLANES, s), F32),
                   jax.ShapeDtypeStruct((512, D_MODEL), F32), jax.ShapeDtypeStruct((512, D_MODEL), F32),
                   jax.ShapeDtypeStruct((D_MODEL, D_MODEL), F32), jax.ShapeDtypeStruct((SUBLANES, D_MODEL), F32)],
        in_specs=[row(512), row(512), row(W_G), row(D_MODEL), row(D_MODEL),
                  _const_spec((1, D_MODEL)), _const_spec((1, D_MODEL)),
                  _const_spec((512, D_MODEL)), _const_spec((512, D_MODEL)), _const_spec((D_MODEL, D_MODEL))],
        out_specs=[row(D_MODEL), row(512), row(512), row(W_G),
                   pl.BlockSpec((SUBLANES, tm), lambda i: (0, i)),
                   pl.BlockSpec((512, D_MODEL), lambda i: (0, 0)), pl.BlockSpec((512, D_MODEL), lambda i: (0, 0)),
                   pl.BlockSpec((D_MODEL, D_MODEL), lambda i: (0, 0)), pl.BlockSpec((SUBLANES, D_MODEL), lambda i: (0, 0))],
        scratch_shapes=[pltpu.VMEM((SUBLANES, D_MODEL), F32)] * 3,
        compiler_params=_params(("arbitrary",), VMEM_LIMIT),
    )(att_a, att_b, g, x, target, gate, g_final, wo_a, wo_b, w_out)


def _rope_bwd(dt, cos, sin, lane):
    u = dt * sin
    lo = (lane % HEAD_DIM) < (HEAD_DIM // 2)
    return dt * cos + jnp.where(lo, pltpu.roll(u, 96, 1), -pltpu.roll(u, 32, 1))


def _swa_bwd(a, datt, l_all, sinks, cos, sin):
    s = a.shape[0]
    nt = s // SWA_ROWS

    def body(sink_ref, a_ref, ap_ref, do_ref, l_ref, cos_ref, sin_ref, da_ref, ds_ref, halo):
        step = pl.program_id(0)
        tile = nt - 1 - step

        @pl.when(step == 0)
        def _():
            halo[...] = jnp.zeros_like(halo)
            ds_ref[...] = jnp.zeros_like(ds_ref)

        lane = lax.broadcasted_iota(jnp.int32, (WINDOW, LANES), 1)
        sub8 = lax.broadcasted_iota(jnp.int32, (SUBLANES, LANES), 0)
        lane8 = lax.broadcasted_iota(jnp.int32, (SUBLANES, LANES), 1)
        blocks = _swa_blocks(a_ref, ap_ref)
        dsink = jnp.zeros((SUBLANES, LANES), F32)

        def join(pair, r0):
            x0, x1 = pair[0][r0:r0 + WINDOW], pair[1][r0:r0 + WINDOW]
            return jnp.where(lane < 64, x0 + pltpu.roll(x0, 64, 1), x1 + pltpu.roll(x1, 64, 1))

        units = [(jb, g) for jb in range(SWA_BLOCKS) for g in range(2)]
        n_u = len(units)
        sinks_col = [_per_head_column([sink_ref[GROUP * g + hh] for hh in range(GROUP)]) for g in range(2)]
        bands = [_swa_band(blocks[jb + 1], blocks[jb], g, lane) for jb, g in units]
        qs = [_stack_heads(blocks[jb + 1], g, lane) for jb, g in units]
        doms = [_stack_heads(do_ref.at[pl.ds(WINDOW * jb, WINDOW), :], g, lane) for jb, g in units]
        lcols = []
        for jb, g in units:
            lv = l_ref[WINDOW * jb:WINDOW * (jb + 1), :]
            lcols.append(_per_head_column([lv[:, GROUP * g + hh:GROUP * g + hh + 1] for hh in range(GROUP)]))
        ps = [jnp.exp(_swa_logits(qs[u], bands[u][0], (tile > 0) if jb == 0 else True) - lcols[u])
              for u, (jb, g) in enumerate(units)]
        dps = [_dot(doms[u], bands[u][1], NT) for u in range(n_u)]
        deltas = [jnp.sum(ps[u] * dps[u], axis=-1, keepdims=True) for u in range(n_u)]
        for u, (jb, g) in enumerate(units):
            sink_term = jnp.exp(sinks_col[g] - lcols[u]) * deltas[u]
            for hh in range(GROUP):
                tot = jnp.sum(sink_term[WINDOW * hh:WINDOW * (hh + 1)])
                dsink = dsink + jnp.where((sub8 == 0) & (lane8 == GROUP * g + hh), -tot, 0.0)
        dss = [(ps[u] * (dps[u] - deltas[u])).astype(BF16) for u in range(n_u)]
        dqs = [_dot(dss[u], bands[u][0]) * SCALE for u in range(n_u)]
        dks = [_dot(dss[u], qs[u], TN) * SCALE for u in range(n_u)]
        dvs = [_dot(ps[u].astype(BF16), doms[u], TN) for u in range(n_u)]

        carry_k, carry_v = halo[:, 0:LANES], halo[:, LANES:2 * LANES]
        for jb in reversed(range(SWA_BLOCKS)):
            rows = slice(WINDOW * jb, WINDOW * (jb + 1))
            cosv, sinv = cos_ref[rows, :], sin_ref[rows, :]
            for g in range(2):
                dq = dqs[2 * jb + g]
                for pb in range(2):
                    r0 = 2 * pb * WINDOW
                    dq_pair = jnp.where(lane < 64, dq[r0:r0 + WINDOW], dq[r0 + WINDOW:r0 + 2 * WINDOW])
                    da_ref[rows, LANES * (2 * g + pb):LANES * (2 * g + pb + 1)] = _rope_bwd(
                        dq_pair, cosv, sinv, lane).astype(BF16)
            dkb, dvb = dks[2 * jb:2 * jb + 2], dvs[2 * jb:2 * jb + 2]
            da_ref[rows, 512:640] = _rope_bwd(join(dkb, WINDOW) + carry_k, cosv, sinv, lane).astype(BF16)
            da_ref[rows, 640:768] = (join(dvb, WINDOW) + carry_v).astype(BF16)
            carry_k, carry_v = join(dkb, 0), join(dvb, 0)
        halo[:, 0:LANES] = carry_k
        halo[:, LANES:2 * LANES] = carry_v
        ds_ref[...] += dsink

    rev = lambda w: pl.BlockSpec((SWA_ROWS, w), lambda i: (nt - 1 - i, 0))
    return pl.pallas_call(
        body, name="swa_bwd", grid=(nt,),
        out_shape=[jax.ShapeDtypeStruct((s, W_A), BF16), jax.ShapeDtypeStruct((SUBLANES, LANES), F32)],
        in_specs=[pl.BlockSpec(memory_space=pltpu.SMEM), rev(W_A),
                  pl.BlockSpec((WINDOW, W_A), lambda i: (jnp.maximum(SWA_BLOCKS * (nt - 1 - i) - 1, 0), 0)),
                  rev(512), rev(LANES), rev(LANES), rev(LANES)],
        out_specs=[rev(W_A), pl.BlockSpec((SUBLANES, LANES), lambda i: (0, 0))],
        scratch_shapes=[pltpu.VMEM((WINDOW, 2 * LANES), F32)],
        compiler_params=_params(("arbitrary",)),
    )(sinks, a, a, datt, l_all, cos, sin)


def _fox_bwd(qa, ka, vb, do, lse, delta, ranges, t):
    s = qa.shape[0]
    nt = s // t

    def body(rg_ref, q_ref, do_ref, lse_ref, dl_ref, k_ref, v_ref, dq_ref, dk_ref, dv_ref, dc_ref, dr_ref, dq_acc):
        p = pl.program_id(0)
        j = pl.program_id(1)
        n_queries = [jnp.clip(_lane_scalar(rg_ref[0], 2 + hh, j), 1, nt - j) for hh in range(2)]

        @pl.when(j == 0)
        def _():
            dq_acc[...] = jnp.zeros_like(dq_acc)

        lane = lax.broadcasted_iota(jnp.int32, (t, LANES), 1)
        rows = lax.broadcasted_iota(jnp.int32, (t, t), 0)
        cols = lax.broadcasted_iota(jnp.int32, (t, t), 1)
        kt = k_ref[...]
        vt = v_ref[...]

        ks = [kt[:, LANES * hh:LANES * (hh + 1)] for hh in range(2)]

        def tile(qis, carry, heads=(0, 1), diagonal=False):
            dk0, dk1, dv = carry
            offs = [pl.multiple_of(i * t, t) for i in qis]
            units = [(u, hh) for u in range(len(qis)) for hh in heads]
            qts = [q_ref[pl.ds(off, t), :] for off in offs]
            dos = [do_ref[pl.ds(off, t), :] for off in offs]
            lses = [lse_ref[0, :, pl.ds(off, t)] for off in offs]
            dls = [dl_ref[0, :, pl.ds(off, t)] for off in offs]
            qs = [qts[u][:, LANES * hh:LANES * (hh + 1)] for u, hh in units]
            doms = [jnp.where((lane < 64) if hh == 0 else (lane >= 64), dos[u], jnp.zeros_like(dos[u])) for u, hh in units]
            sts = [_dot(ks[hh], qs[n], NT) for n, (u, hh) in enumerate(units)]
            dpts = [_dot(vt, doms[n], NT) for n in range(len(units))]
            if diagonal:
                sts = [jnp.where(cols >= rows, st, NEG) for st in sts]
            pts = [jnp.exp(sts[n] - lses[u][hh:hh + 1, :]) for n, (u, hh) in enumerate(units)]
            dsts = [(pts[n] * (dpts[n] - dls[u][hh:hh + 1, :])).astype(BF16) for n, (u, hh) in enumerate(units)]
            for n, (u, hh) in enumerate(units):
                dv = dv + _dot(pts[n].astype(BF16), doms[n])
                term = _dot(dsts[n], qs[n])
                dk0, dk1 = (dk0 + term, dk1) if hh == 0 else (dk0, dk1 + term)
                dq_acc[hh, pl.ds(offs[u], t), :] += _dot(dsts[n], ks[hh], TN)
            return dk0, dk1, dv

        zero = jnp.zeros((t, LANES), F32)
        carry = tile([j], (zero, zero, zero), diagonal=True)
        n_rest = jnp.minimum(n_queries[0], n_queries[1]) - 1
        carry = lax.fori_loop(0, n_rest // 2, lambda u, cr: tile([j + 1 + 2 * u, j + 2 + 2 * u], cr), carry)
        carry = lax.fori_loop(0, n_rest % 2, lambda u, cr: tile([j + n_rest], cr), carry)
        for hh in range(2):
            carry = lax.fori_loop(j + 1 + n_rest, j + n_queries[hh], lambda i, cr, hh=hh: tile([i], cr, heads=(hh,)), carry)
        dk0, dk1, dv = carry
        e0, e1 = _aug_lane(0), _aug_lane(1)
        dk_ref[...] = jnp.where(lane < 64, dk0, dk1).astype(BF16)
        dv_ref[...] = dv.astype(BF16)
        c0 = jnp.broadcast_to(dk0[:, e0 + 3:e0 + 4], (t, LANES))
        c1 = jnp.broadcast_to(dk1[:, e1 + 3:e1 + 4], (t, LANES))
        dc_ref[0] = jnp.where(lane == 2 * p, -c0, jnp.where(lane == 2 * p + 1, -c1, 0.0))

        @pl.when(j == nt - 1)
        def _():
            lane_s = lax.broadcasted_iota(jnp.int32, (s, LANES), 1)
            a0, a1 = dq_acc[0], dq_acc[1]
            dq_ref[...] = (jnp.where(lane_s < 64, a0, a1) * SCALE).astype(BF16)
            r0 = jnp.broadcast_to(a0[:, e0:e0 + 1], (s, LANES))
            r1 = jnp.broadcast_to(a1[:, e1:e1 + 1], (s, LANES))
            dr_ref[0] = jnp.where(lane_s == 2 * p, r0, jnp.where(lane_s == 2 * p + 1, r1, 0.0))

    return pl.pallas_call(
        body, name="fox_bwd", grid=(4, nt),
        out_shape=[jax.ShapeDtypeStruct((s, 512), BF16), jax.ShapeDtypeStruct((s, 512), BF16),
                   jax.ShapeDtypeStruct((s, 512), BF16), jax.ShapeDtypeStruct((4, s, LANES), F32),
                   jax.ShapeDtypeStruct((4, s, LANES), F32)],
        in_specs=[pl.BlockSpec((1, SUBLANES, LANES), lambda p, j: (p, 0, 0)),
                  pl.BlockSpec((s, 2 * LANES), lambda p, j: (0, p)),
                  pl.BlockSpec((s, LANES), lambda p, j: (0, p)),
                  pl.BlockSpec((1, SUBLANES, s), lambda p, j: (p, 0, 0)),
                  pl.BlockSpec((1, SUBLANES, s), lambda p, j: (p, 0, 0)),
                  pl.BlockSpec((t, 2 * LANES), lambda p, j: (j, p)),
                  pl.BlockSpec((t, LANES), lambda p, j: (j, p))],
        out_specs=[pl.BlockSpec((s, LANES), lambda p, j: (0, p)),
                   pl.BlockSpec((t, LANES), lambda p, j: (j, p)),
                   pl.BlockSpec((t, LANES), lambda p, j: (j, p)),
                   pl.BlockSpec((1, t, LANES), lambda p, j: (p, j, 0)),
                   pl.BlockSpec((1, s, LANES), lambda p, j: (p, 0, 0))],
        scratch_shapes=[pltpu.VMEM((2, s, LANES), F32)],
        compiler_params=_params(("parallel", "arbitrary"), VMEM_LIMIT),
    )(ranges, qa, do, lse, delta, ka, vb)


def _forget_logit_grad(dc_ref, dr_ref, f, b_f, carry):
    tb = f.shape[0]
    lane = lax.broadcasted_iota(jnp.int32, (tb, LANES), 1)
    dc = dc_ref[0] + dr_ref[0]
    for k in range(1, 4):
        dc = dc + (dc_ref[k] + dr_ref[k])
    hi, mid, lo = _split3(dc)
    rows = lax.broadcasted_iota(jnp.int32, (tb, tb), 0)
    cols = lax.broadcasted_iota(jnp.int32, (tb, tb), 1)
    triu = (cols >= rows).astype(BF16)
    dlogf = _dot(triu, hi) + _dot(triu, mid) + _dot(triu, lo) + carry[0:1, :]
    carry[...] = jnp.broadcast_to(dlogf[0:1, :], carry.shape)
    return jnp.where(lane < N_HEADS, dlogf * _sigmoid(-(f + b_f)), 0.0)


def _dh_norm_bwd(d_a, d_q, d_k, d_v, dcum_k, dcum_q, f, bf_pad, d_g, w_t, x, dx2, gnorm, scale1, tm=512):
    s = x.shape[0]
    nt = s // tm

    def body(da_ref, dq_ref, dk_ref, dv_ref, dc_ref, dr_ref, f_ref, bf_ref, dg_ref, w_ref, x_ref, dx2_ref, g_ref, sc_ref,
             gx_ref, vec_ref, df_ref, db_ref, a_sh, a_sc, a_g, carry):
        step = pl.program_id(0)

        @pl.when(step == 0)
        def _():
            a_sh[...] = jnp.zeros_like(a_sh)
            a_sc[...] = jnp.zeros_like(a_sc)
            a_g[...] = jnp.zeros_like(a_g)
            carry[...] = jnp.zeros_like(carry)
            db_ref[...] = jnp.zeros_like(db_ref)

        def fold(v):
            return jnp.sum(v.reshape(tm // SUBLANES, SUBLANES, D_MODEL), axis=0)

        dfb = _forget_logit_grad(dc_ref, dr_ref, f_ref[...], bf_ref[...], carry)
        d_f = dfb.astype(BF16)
        df_ref[...] = d_f
        sub8 = lax.broadcasted_iota(jnp.int32, (SUBLANES, LANES), 0)
        db_ref[...] += jnp.where(sub8 == 0, jnp.sum(dfb, axis=0, keepdims=True), 0.0)
        d_all = jnp.concatenate([da_ref[...], dq_ref[...], dk_ref[...], dv_ref[...], d_f, dg_ref[...]], axis=1)
        dh = _dot(d_all, w_ref[...])
        xv = x_ref[...]
        r = lax.rsqrt(jnp.mean(xv * xv, axis=-1, keepdims=True) + NORM_EPS)
        xn = xv * r
        gn = g_ref[...]
        a_sh[...] += fold(dh)
        a_sc[...] += fold(dh * (xn * gn))
        dn1 = dh * sc_ref[...]
        a_g[...] += fold(dn1 * xn)
        dxn = dn1 * gn
        gx_ref[...] = dx2_ref[...] + r * (dxn - xn * jnp.mean(dxn * xn, axis=-1, keepdims=True))

        @pl.when(step == nt - 1)
        def _():
            sub = lax.broadcasted_iota(jnp.int32, (SUBLANES, D_MODEL), 0)
            v_sh = jnp.sum(a_sh[...], axis=0, keepdims=True)
            v_sc = jnp.sum(a_sc[...], axis=0, keepdims=True)
            v_g = jnp.sum(a_g[...], axis=0, keepdims=True)
            vec_ref[...] = jnp.where(sub == 0, v_sh, jnp.where(sub == 1, v_sc, jnp.where(sub == 2, v_g, 0.0)))

    row = lambda w: pl.BlockSpec((tm, w), lambda i: (nt - 1 - i, 0))
    slabs = pl.BlockSpec((4, tm, LANES), lambda i: (0, nt - 1 - i, 0))
    return pl.pallas_call(
        body, name="dh_norm_bwd", grid=(nt,),
        out_shape=[jax.ShapeDtypeStruct((s, D_MODEL), F32), jax.ShapeDtypeStruct((SUBLANES, D_MODEL), F32),
                   jax.ShapeDtypeStruct((s, LANES), BF16), jax.ShapeDtypeStruct((SUBLANES, LANES), F32)],
        in_specs=[row(W_A), row(512), row(512), row(512), slabs, slabs, row(W_F), _const_spec((1, LANES)), row(W_G),
                  _const_spec((W_INT, D_MODEL)), row(D_MODEL), row(D_MODEL), _const_spec((1, D_MODEL)),
                  _const_spec((1, D_MODEL))],
        out_specs=[row(D_MODEL), pl.BlockSpec((SUBLANES, D_MODEL), lambda i: (0, 0)), row(LANES),
                   pl.BlockSpec((SUBLANES, LANES), lambda i: (0, 0))],
        scratch_shapes=[pltpu.VMEM((SUBLANES, D_MODEL), F32)] * 3 + [pltpu.VMEM((SUBLANES, LANES), F32)],
        compiler_params=_params(("arbitrary",), VMEM_LIMIT),
    )(d_a, d_q, d_k, d_v, dcum_k, dcum_q, f, bf_pad, d_g, w_t, x, dx2, gnorm, scale1)


def _dw_in(h_t, ds, ts=1024, tc=512):
    n = len(ds)
    s = h_t.shape[1]
    ns = s // ts

    def body(*refs):
        h_ref, d_refs, o_refs = refs[0], refs[1:1 + n], refs[1 + n:1 + 2 * n]
        accs, sem = refs[1 + 2 * n:1 + 3 * n], refs[1 + 3 * n]
        k = pl.program_id(0)

        @pl.when(k == 0)
        def _():
            for acc in accs:
                acc[...] = jnp.zeros_like(acc)

        hv = h_ref[...]
        for acc, d_ref in zip(accs, d_refs):
            width = acc.shape[1]
            for c0 in range(0, width, tc):
                c1 = min(c0 + tc, width)
                acc[:, c0:c1] += _dot(hv, d_ref[:, c0:c1])

        @pl.when(k == ns - 1)
        def _():
            copies = [pltpu.make_async_copy(acc, o_ref, sem.at[g]) for g, (acc, o_ref) in enumerate(zip(accs, o_refs))]
            for cp in copies:
                cp.start()
            for cp in copies:
                cp.wait()

    return pl.pallas_call(
        body, name="dw_in", grid=(ns,),
        out_shape=[jax.ShapeDtypeStruct((D_MODEL, d.shape[1]), F32) for d in ds],
        in_specs=[pl.BlockSpec((D_MODEL, ts), lambda k: (0, k))] + [pl.BlockSpec((ts, d.shape[1]), lambda k: (k, 0)) for d in ds],
        out_specs=[pl.BlockSpec(memory_space=pl.ANY)] * n,
        scratch_shapes=[pltpu.VMEM((D_MODEL, d.shape[1]), F32) for d in ds] + [pltpu.SemaphoreType.DMA((n,))],
        compiler_params=_params(("arbitrary",), VMEM_LIMIT),
    )(h_t, *ds)


def _small_grads(packs, c_t, dada_shard):
    def body(p_ref, ct_ref, da_ref, sum_ref, gw_ref):
        acc = p_ref[0]
        for dev in range(1, 8):
            acc = acc + p_ref[dev]
        sum_ref[...] = acc
        gw_ref[...] = jnp.dot(ct_ref[...], da_ref[...], preferred_element_type=F32, precision=lax.Precision.HIGHEST)

    return pl.pallas_call(
        body, name="small_grads",
        out_shape=[jax.ShapeDtypeStruct(packs.shape[1:], F32),
                   jax.ShapeDtypeStruct((c_t.shape[0], dada_shard.shape[1]), F32)],
    )(packs, c_t, dada_shard)


def _adamw_body(w_ref, g_ref, m_ref, v_ref, d_ref, mo_ref, vo_ref):
    c1 = 1.0 / (1.0 - ADAM_B1 ** ADAM_STEP)
    c2 = 1.0 / (1.0 - ADAM_B2 ** ADAM_STEP)
    gv = g_ref[...]
    mn = ADAM_B1 * m_ref[...] + (1.0 - ADAM_B1) * gv
    vn = ADAM_B2 * v_ref[...] + (1.0 - ADAM_B2) * (gv * gv)
    mo_ref[...] = mn
    vo_ref[...] = vn
    d_ref[...] = -ADAM_LR * ((mn * c1) / (jnp.sqrt(vn * c2) + ADAM_EPS) + ADAM_WD * w_ref[...])


def _adamw3(w, g, m, v, name, tb=128):
    spec = pl.BlockSpec((tb, SUBLANES, LANES), lambda i: (i, 0, 0))
    return pl.pallas_call(
        functools.partial(_adamw_body), name=name, grid=(pl.cdiv(w.shape[0], tb),),
        out_shape=[jax.ShapeDtypeStruct(w.shape, F32)] * 3,
        in_specs=[spec] * 4, out_specs=[spec] * 3,
        compiler_params=_params(("parallel",)),
    )(w, g, m, v)


def _adamw(w, g, m, v, name):
    r, c = w.shape
    tr = 128 if r % 128 == 0 else r
    body = functools.partial(_adamw_body)
    spec = pl.BlockSpec((tr, c), lambda i: (i, 0))
    return pl.pallas_call(
        body, name=name, grid=(r // tr,),
        out_shape=[jax.ShapeDtypeStruct((r, c), F32)] * 3,
        in_specs=[spec] * 4, out_specs=[spec] * 3,
        compiler_params=_params(("parallel",)),
    )(w, g, m, v)


def _rope_inputs(positions):
    inv_freq = 10000.0 ** (-jnp.arange(0, HEAD_DIM, 2, dtype=F32) / HEAD_DIM)
    pos = jnp.broadcast_to(positions.astype(F32)[:, None], (positions.shape[0], LANES))
    return pos, jnp.tile(inv_freq, 4)[None, :]


def _pad_rows(v, rows=SUBLANES):
    return jnp.pad(v, ((0, rows - v.shape[0]), (0, 0)))


def kernel(x, c, positions, w_ada, b_ada, g_norm, w_in, b_f, sinks, w_o_swa, w_o_fox, w_out, g_final, loss_target, m_w_ada, m_b_ada, m_g_norm, m_w_in, m_b_f, m_sinks, m_w_o_swa, m_w_o_fox, m_w_out, m_g_final, v_w_ada, v_b_ada, v_g_norm, v_w_in, v_b_f, v_sinks, v_w_o_swa, v_w_o_fox, v_w_out, v_g_final):
    ix, iy, ic = lax.axis_index("x"), lax.axis_index("y"), lax.axis_index("c")
    chip = 2 * ix + iy
    dev = 2 * chip + ic
    xs, tgt = x[0], loss_target[0]
    s = xs.shape[0]

    b_ada_shard = lax.dynamic_slice(b_ada, (0, chip * 768), (1, 768))
    ada_parts, g_in, g_oa, g_ob, g_out = _gather_inputs(
        _pad_rows(c), w_ada[0], b_ada_shard, [w_in[0], w_o_swa[0], w_o_fox[0], w_out[0]], "gather_inputs")
    ada = lax.dynamic_index_in_dim(ada_parts, dev, axis=1, keepdims=False).reshape(1, 3 * D_MODEL)
    shift, scale, gate = ada[:, :D_MODEL], ada[:, D_MODEL:2 * D_MODEL], ada[:, 2 * D_MODEL:]
    scale1 = 1.0 + scale

    w_ref_order = jnp.transpose(g_in, (1, 0, 2)).reshape(D_MODEL, R_END)
    w_int = jnp.concatenate([
        w_ref_order[:, :R_ZA], w_ref_order[:, R_QB:R_FB], w_ref_order[:, R_FB:R_ZB],
        jnp.zeros((D_MODEL, W_F - N_HEADS), BF16), w_ref_order[:, R_ZA:R_QB], w_ref_order[:, R_ZB:]], axis=1)
    w_int_t = w_int.T
    wo_a = jnp.transpose(g_oa, (1, 0, 2)).reshape(512, D_MODEL)
    wo_b = jnp.transpose(g_ob, (1, 0, 2)).reshape(512, D_MODEL)
    w_o = g_out.reshape(D_MODEL, D_MODEL)

    pos, freq = _rope_inputs(positions[0])
    bf_pad = jnp.pad(b_f, ((0, 0), (0, LANES - N_HEADS)))
    sink_vec = sinks[0]

    a, vb, f, g, h_t, cos, sin, qa, ka, va, stats = _norm_proj(
        xs, g_norm * scale1, shift, w_int, pos, freq, bf_pad, FOX_TILE)
    att_a, l_swa = _swa_fwd(a, sink_vec)
    ranges = _fox_tile_ranges(stats)
    att_b, lse = _fox_fwd(qa, ka, va, ranges, FOX_TILE)

    dx2, datt_a, datt_b, d_g, delta8, dwo_a, dwo_b, dw_out, vec_mid = _mid(
        att_a, att_b, g, xs, tgt, gate, g_final.reshape(1, D_MODEL), wo_a, wo_b, w_o)
    delta = jnp.pad(delta8.reshape(4, 2, s), ((0, 0), (0, SUBLANES - 2), (0, 0)))
    d_a, dsink = _swa_bwd(a, datt_a, l_swa, sink_vec, cos, sin)
    dq, dk, dv, dcum_k, dcum_q = _fox_bwd(qa, ka, vb, datt_b, lse, delta, ranges, FOX_TILE)
    grad_x, vec_dh, d_f, dbf = _dh_norm_bwd(
        d_a, dq, dk, dv, dcum_k, dcum_q, f, bf_pad, d_g, w_int_t, xs, dx2, g_norm, scale1)
    dw_a, dw_q, dw_k, dw_v, dw_f, dw_g = _dw_in(h_t, [d_a, dq, dk, dv, d_f, d_g])
    dw_in = jnp.concatenate([dw_a, dw_g[:, :512], dw_q, dw_k, dw_v, dw_f[:, :N_HEADS], dw_g[:, 512:]], axis=1)

    tail = jnp.pad(jnp.concatenate([dbf[0:1, :N_HEADS], dsink[0:1, :N_HEADS]], axis=1), ((0, 0), (0, D_MODEL - 2 * N_HEADS)))
    pack = jnp.concatenate([c, vec_dh[0:2], vec_mid[1:2], vec_dh[2:3], vec_mid[0:1], tail, vec_mid[2:3]], axis=0)

    def slots(w, axis):
        if axis == 1:
            return jnp.transpose(w.reshape(w.shape[0], 4, w.shape[1] // 4), (1, 0, 2))
        return w.reshape(4, w.shape[0] // 4, w.shape[1])

    packs, g_wo_a, g_wo_b, g_w_out, g_w_in = _reduce_scatter(
        [slots(dwo_a, 1), slots(dwo_b, 1), slots(dw_out, 0), slots(dw_in, 1)], pack, "reduce_grads")
    dada_all = packs[:, 1:4, :].reshape(8, 3 * D_MODEL)
    dada_shard = lax.dynamic_slice(dada_all, (0, chip * 768), (8, 768))
    sums, g_w_ada = _small_grads(packs, packs[:, 0, :].T, dada_shard)
    g_b_ada = sums[1:4].reshape(1, 3 * D_MODEL)
    g_g_norm = sums[4:5]
    g_g_final = sums[5]
    g_b_f = sums[6:7, :N_HEADS]
    g_sinks = sums[6:7, N_HEADS:2 * N_HEADS]
    loss = sums[7, 0]

    grads = {
        "w_ada": g_w_ada, "b_ada": g_b_ada, "g_norm": g_g_norm, "w_in": g_w_in, "b_f": g_b_f, "sinks": g_sinks,
        "w_o_swa": g_wo_a, "w_o_fox": g_wo_b, "w_out": g_w_out, "g_final": g_g_final,
    }
    params = {
        "w_ada": (w_ada, m_w_ada, v_w_ada), "b_ada": (b_ada, m_b_ada, v_b_ada), "g_norm": (g_norm, m_g_norm, v_g_norm),
        "w_in": (w_in, m_w_in, v_w_in), "b_f": (b_f, m_b_f, v_b_f), "sinks": (sinks, m_sinks, v_sinks),
        "w_o_swa": (w_o_swa, m_w_o_swa, v_w_o_swa), "w_o_fox": (w_o_fox, m_w_o_fox, v_w_o_fox),
        "w_out": (w_out, m_w_out, v_w_out), "g_final": (g_final, m_g_final, v_g_final),
    }
    n_col = w_in.shape[2]

    def as_stored(t):
        return jnp.transpose(t, (2, 0, 1)).reshape(n_col, SUBLANES, LANES)

    def from_stored(t):
        return jnp.transpose(t, (1, 2, 0)).reshape(1, D_MODEL, n_col)

    names = list(grads)
    out_g, out_d, out_m, out_v = [], [], [], []
    for nm in names:
        w, m, v = params[nm]
        if nm == "w_in":
            g_st = as_stored(grads[nm][None])
            d_, m_, v_ = _adamw3(as_stored(w), g_st, as_stored(m), as_stored(v), "adamw_" + nm)
            res = [from_stored(t) for t in (g_st, d_, m_, v_)]
        else:
            shape2 = (w.shape[-2], w.shape[-1]) if w.ndim >= 2 else (1, w.shape[0])
            d_, m_, v_ = _adamw(w.reshape(shape2), grads[nm].reshape(shape2), m.reshape(shape2), v.reshape(shape2), "adamw_" + nm)
            res = [t.reshape(w.shape) for t in (grads[nm], d_, m_, v_)]
        out_g.append(res[0])
        out_d.append(res[1])
        out_m.append(res[2])
        out_v.append(res[3])
    return (loss, grad_x[None], *out_g, *out_d, *out_m, *out_v)
```

```python
import functools

import numpy as np
import jax
import jax.numpy as jnp
from jax import lax
from jax.experimental import pallas as pl
from jax.experimental.pallas import tpu as pltpu

F32 = jnp.float32
BF16 = jnp.bfloat16
MESH = pl.DeviceIdType.MESH

D_MODEL = 1024
HEAD_DIM = 64
N_HEADS = 8
WINDOW = 128
NORM_EPS = 1e-6
SCALE = HEAD_DIM ** -0.5
NEG = -1e30
LANES = 128
SUBLANES = 8
VMEM_LIMIT = 60 * 1024 * 1024
FOX_TILE = 512

W_A, W_B, W_F, W_G = 768, 1536, 128, 3072
OFF_A, OFF_B, OFF_F, OFF_G = 0, 768, 2304, 2432
W_INT = W_A + W_B + W_F + W_G
R_ZA, R_QB, R_FB, R_ZB, R_END = 768, 1280, 2816, 2824, 5384

ADAM_LR, ADAM_B1, ADAM_B2, ADAM_EPS, ADAM_WD, ADAM_STEP = 0.001, 0.9, 0.999, 1e-08, 0.01, 10

NT = (((1,), (1,)), ((), ()))
TN = (((0,), (0,)), ((), ()))


def _dot(a, b, dims=None):
    if dims is None:
        return jnp.dot(a, b, preferred_element_type=F32)
    return lax.dot_general(a, b, dims, preferred_element_type=F32)


def _split3(v):
    hi = v.astype(BF16)
    r1 = v - hi.astype(F32)
    mid = r1.astype(BF16)
    lo = (r1 - mid.astype(F32)).astype(BF16)
    return hi, mid, lo


def _sigmoid(v):
    return 1.0 / (1.0 + jnp.exp(-v))


def _params(sem=None, vmem=None):
    return pltpu.CompilerParams(dimension_semantics=sem, vmem_limit_bytes=vmem)


def _const_spec(shape):
    nd = len(shape)
    return pl.BlockSpec(shape, lambda *_: (0,) * nd, pipeline_mode=pl.Buffered(1))


def _flip(v, f):
    return 1 - v if f else v


_CHIP_FLIPS = ((1, 0), (0, 1), (1, 1))


def _gather_inputs(c_pad, w_ada, b_ada_shard, shards, name):
    n = len(shards)
    n_col = w_ada.shape[1]

    def body(*refs):
        c_ref, wa_ref, ba_ref = refs[:3]
        ins = refs[3:3 + n]
        ada_ref = refs[3 + n]
        outs = refs[4 + n:4 + 2 * n]
        call_ref, send_sems, recv_sems = refs[4 + 2 * n:7 + 2 * n]
        x, y, c = lax.axis_index("x"), lax.axis_index("y"), lax.axis_index("c")
        k_me = 2 * x + y
        me = 2 * k_me + c
        sibling = (x, y, 1 - c)
        chips = [(_flip(x, fx), _flip(y, fy)) for fx, fy in _CHIP_FLIPS]

        def piece(i, chip_k, half):
            hr = ins[i].shape[0] // 2
            return outs[i].at[chip_k, pl.ds(half * hr, hr), :]

        def copy(i, slot, chip_k, half, to):
            return pltpu.make_async_remote_copy(
                src_ref=piece(i, chip_k, half), dst_ref=piece(i, chip_k, half),
                send_sem=send_sems.at[6 * i + slot], recv_sem=recv_sems.at[6 * i + slot],
                device_id=to, device_id_type=MESH)

        def small(ref, slot, sem, to):
            return pltpu.make_async_remote_copy(
                src_ref=ref.at[slot], dst_ref=ref.at[slot], send_sem=send_sems.at[6 * n + sem],
                recv_sem=recv_sems.at[6 * n + sem], device_id=to, device_id_type=MESH)

        for i in range(n):
            outs[i][k_me] = ins[i][...].astype(BF16)
        started = []
        for i in range(n):
            for j, chip in enumerate(chips):
                cp = copy(i, j, k_me, c, (chip[0], chip[1], c))
                cp.start()
                started.append(cp)

        call_ref[me] = c_ref[...]
        peers = [(_flip(x, k & 4), _flip(y, k & 2), _flip(c, k & 1)) for k in range(1, 8)]
        for k, peer in enumerate(peers):
            cp = small(call_ref, me, k, peer)
            cp.start()
            started.append(cp)
        for k, peer in enumerate(peers):
            small(call_ref, 4 * peer[0] + 2 * peer[1] + peer[2], k, peer).wait_recv()
        c_all = call_ref[:, 0, :].astype(BF16)
        ada_ref[k_me] = _dot(c_all, wa_ref[...].astype(BF16)) + ba_ref[...]
        for j, chip in enumerate(chips):
            cp = small(ada_ref, k_me, 7 + j, (chip[0], chip[1], c))
            cp.start()
            started.append(cp)

        for j, chip in enumerate(chips):
            chip_k = 2 * chip[0] + chip[1]
            for i in range(n):
                copy(i, j, chip_k, c, (chip[0], chip[1], c)).wait_recv()
                cp = copy(i, 3 + j, chip_k, c, sibling)
                cp.start()
                started.append(cp)
        for j, chip in enumerate(chips):
            chip_k = 2 * chip[0] + chip[1]
            small(ada_ref, chip_k, 7 + j, (chip[0], chip[1], c)).wait_recv()
            for i in range(n):
                copy(i, 3 + j, chip_k, 1 - c, sibling).wait_recv()
        for cp in started:
            cp.wait_send()

    vmem = pl.BlockSpec(memory_space=pltpu.VMEM)
    return pl.pallas_call(
        body, name=name,
        out_shape=[jax.ShapeDtypeStruct((4, 8, n_col), F32)] + [jax.ShapeDtypeStruct((4,) + s.shape, BF16) for s in shards],
        in_specs=[vmem] * (3 + n),
        out_specs=[vmem] * (1 + n),
        scratch_shapes=[pltpu.VMEM((8,) + c_pad.shape, F32),
                        pltpu.SemaphoreType.DMA((6 * n + 10,)), pltpu.SemaphoreType.DMA((6 * n + 10,))],
        compiler_params=_params(vmem=VMEM_LIMIT),
    )(c_pad, w_ada, b_ada_shard, *shards)


def _reduce_scatter(pieces, pack, name):
    n = len(pieces)

    def body(*refs):
        pack_ref, ins = refs[0], refs[1:1 + n]
        packs_ref, outs = refs[1 + n], refs[2 + n:2 + 2 * n]
        rest = refs[2 + 2 * n:]
        own, got = rest[:n], rest[n:2 * n]
        sendb, recvb = rest[2 * n:3 * n], rest[3 * n:4 * n]
        send_sems, recv_sems, local_sems = rest[4 * n:4 * n + 3]
        x, y, c = lax.axis_index("x"), lax.axis_index("y"), lax.axis_index("c")
        k_me = 2 * x + y
        me = 2 * k_me + c
        sibling = (x, y, 1 - c)
        chips = [(_flip(x, fx), _flip(y, fy)) for fx, fy in _CHIP_FLIPS]
        hrs = [p.shape[1] // 2 for p in pieces]

        def remote(i, slot, src, dst, to):
            return pltpu.make_async_remote_copy(
                src_ref=src, dst_ref=dst, send_sem=send_sems.at[5 * i + slot], recv_sem=recv_sems.at[5 * i + slot],
                device_id=to, device_id_type=MESH)

        started = []
        packs_ref[me] = pack_ref[...]
        peers = [(_flip(x, k & 4), _flip(y, k & 2), _flip(c, k & 1)) for k in range(1, 8)]
        for k, peer in enumerate(peers):
            cp = pltpu.make_async_remote_copy(
                src_ref=pack_ref, dst_ref=packs_ref.at[me], send_sem=send_sems.at[5 * n + k],
                recv_sem=recv_sems.at[5 * n + k], device_id=peer, device_id_type=MESH)
            cp.start()
            started.append(cp)
        loads = []
        for i in range(n):
            ld = pltpu.make_async_copy(ins[i].at[:, pl.ds(c * hrs[i], hrs[i]), :], own[i], local_sems.at[i])
            ld.start()
            loads.append(ld)
            cp = remote(i, 0, ins[i].at[:, pl.ds((1 - c) * hrs[i], hrs[i]), :], got[i], sibling)
            cp.start()
            started.append(cp)
        for i in range(n):
            loads[i].wait()
            remote(i, 0, ins[i].at[:, pl.ds(c * hrs[i], hrs[i]), :], got[i], sibling).wait_recv()
            for j, chip in enumerate(chips):
                chip_k = 2 * chip[0] + chip[1]
                sendb[i][j] = (own[i][chip_k] + got[i][chip_k]).astype(BF16)
                cp = remote(i, 1 + j, sendb[i].at[j], recvb[i].at[j], (chip[0], chip[1], c))
                cp.start()
                started.append(cp)
        for i in range(n):
            acc = own[i][k_me] + got[i][k_me]
            for j, chip in enumerate(chips):
                remote(i, 1 + j, sendb[i].at[j], recvb[i].at[j], (chip[0], chip[1], c)).wait_recv()
                acc = acc + recvb[i][j].astype(F32)
            mine = outs[i].at[pl.ds(c * hrs[i], hrs[i]), :]
            outs[i][pl.ds(pl.multiple_of(c * hrs[i], SUBLANES), hrs[i]), :] = acc
            cp = remote(i, 4, mine, mine, sibling)
            cp.start()
            started.append(cp)
        for i in range(n):
            theirs = outs[i].at[pl.ds((1 - c) * hrs[i], hrs[i]), :]
            remote(i, 4, theirs, theirs, sibling).wait_recv()
        for k, peer in enumerate(peers):
            pltpu.make_async_remote_copy(
                src_ref=pack_ref, dst_ref=packs_ref.at[4 * peer[0] + 2 * peer[1] + peer[2]],
                send_sem=send_sems.at[5 * n + k], recv_sem=recv_sems.at[5 * n + k],
                device_id=peer, device_id_type=MESH).wait_recv()
        for cp in started:
            cp.wait_send()

    vmem = pl.BlockSpec(memory_space=pltpu.VMEM)
    scratch = []
    scratch += [pltpu.VMEM((4, p.shape[1] // 2, p.shape[2]), F32) for p in pieces]
    scratch += [pltpu.VMEM((4, p.shape[1] // 2, p.shape[2]), F32) for p in pieces]
    scratch += [pltpu.VMEM((3, p.shape[1] // 2, p.shape[2]), BF16) for p in pieces]
    scratch += [pltpu.VMEM((3, p.shape[1] // 2, p.shape[2]), BF16) for p in pieces]
    scratch += [pltpu.SemaphoreType.DMA((5 * n + 7,)), pltpu.SemaphoreType.DMA((5 * n + 7,)), pltpu.SemaphoreType.DMA((n,))]
    return pl.pallas_call(
        body, name=name,
        out_shape=[jax.ShapeDtypeStruct((8,) + pack.shape, F32)] + [jax.ShapeDtypeStruct(p.shape[1:], F32) for p in pieces],
        in_specs=[vmem] + [pl.BlockSpec(memory_space=pl.ANY)] * n,
        out_specs=[vmem] * (1 + n),
        scratch_shapes=scratch,
        compiler_params=_params(vmem=VMEM_LIMIT),
    )(pack, *pieces)


def _rope_fwd(t, cos, sin, lane):
    lo = (lane % HEAD_DIM) < (HEAD_DIM // 2)
    return t * cos + jnp.where(lo, -pltpu.roll(t, 96, 1), pltpu.roll(t, 32, 1)) * sin


def _norm_proj(x, gmod, shift, w_int, pos, freq, bf_pad, tm):
    s = x.shape[0]

    def body(x_ref, g_ref, sh_ref, w_ref, pos_ref, fr_ref, bf_ref,
             a_ref, vb_ref, f_ref, gg_ref, ht_ref, cos_ref, sin_ref, q_ref, k_ref, v_ref, st_ref, carry):
        @pl.when(pl.program_id(0) == 0)
        def _():
            carry[...] = jnp.zeros_like(carry)

        xv = x_ref[...]
        r = lax.rsqrt(jnp.mean(xv * xv, axis=-1, keepdims=True) + NORM_EPS)
        hf = (xv * r) * g_ref[...] + sh_ref[...]
        hb = hf.astype(BF16)
        ht_ref[...] = hb.T
        pa = _dot(hb, w_ref[:, OFF_A:OFF_A + W_A])
        ang = pos_ref[...] * fr_ref[...]
        cosv, sinv = jnp.cos(ang), jnp.sin(ang)
        cos_ref[...] = cosv
        sin_ref[...] = sinv
        lane = lax.broadcasted_iota(jnp.int32, (tm, LANES), 1)
        for j in range(5):
            t = pa[:, LANES * j:LANES * (j + 1)]
            a_ref[:, LANES * j:LANES * (j + 1)] = _rope_fwd(t, cosv, sinv, lane).astype(BF16)
        a_ref[:, 640:768] = pa[:, 640:768].astype(BF16)
        pf = _dot(hb, w_ref[:, OFF_F:OFF_F + W_F])
        f_ref[...] = pf
        bblk = _dot(hb, w_ref[:, OFF_B:OFF_B + W_B]).astype(BF16)
        vb_ref[...] = bblk[:, 1024:1536]
        gg_ref[...] = _dot(hb, w_ref[:, OFF_G:OFF_G + W_G]).astype(BF16)
        _augment_heads(bblk, _cumsum_tile(pf, bf_ref[...], carry), q_ref, k_ref, v_ref, st_ref)

    row = lambda w: pl.BlockSpec((tm, w), lambda i: (i, 0))
    return pl.pallas_call(
        body, name="norm_proj", grid=(s // tm,),
        out_shape=[jax.ShapeDtypeStruct((s, W_A), BF16), jax.ShapeDtypeStruct((s, 512), BF16),
                   jax.ShapeDtypeStruct((s, W_F), F32), jax.ShapeDtypeStruct((s, W_G), BF16),
                   jax.ShapeDtypeStruct((D_MODEL, s), BF16),
                   jax.ShapeDtypeStruct((s, LANES), F32), jax.ShapeDtypeStruct((s, LANES), F32)]
        + [jax.ShapeDtypeStruct((s, 1024), BF16)] * 3 + [jax.ShapeDtypeStruct((s // tm, SUBLANES, LANES), F32)],
        in_specs=[row(D_MODEL), _const_spec((1, D_MODEL)), _const_spec((1, D_MODEL)), _const_spec((D_MODEL, W_INT)),
                  row(LANES), _const_spec((1, LANES)), _const_spec((1, LANES))],
        out_specs=[row(W_A), row(512), row(W_F), row(W_G), pl.BlockSpec((D_MODEL, tm), lambda i: (0, i)),
                   row(LANES), row(LANES), row(1024), row(1024), row(1024),
                   pl.BlockSpec((1, SUBLANES, LANES), lambda i: (i, 0, 0))],
        scratch_shapes=[pltpu.VMEM((SUBLANES, LANES), F32)],
        compiler_params=_params(("arbitrary",), VMEM_LIMIT),
    )(x, gmod, shift, w_int, pos, freq, bf_pad)


def _log_sigmoid(u):
    return jnp.minimum(u, 0.0) - jnp.log(1.0 + jnp.exp(-jnp.abs(u)))


def _cumsum_tile(f, b_f, carry):
    tb = f.shape[0]
    lane = lax.broadcasted_iota(jnp.int32, (tb, LANES), 1)
    logf = jnp.where(lane < N_HEADS, _log_sigmoid(f + b_f), 0.0)
    hi, mid, lo = _split3(logf)
    rows = lax.broadcasted_iota(jnp.int32, (tb, tb), 0)
    cols = lax.broadcasted_iota(jnp.int32, (tb, tb), 1)
    tril = (cols <= rows).astype(BF16)
    cum = _dot(tril, hi) + _dot(tril, mid) + _dot(tril, lo) + carry[0:1, :]
    carry[...] = jnp.broadcast_to(cum[tb - 1:tb, :], carry.shape)
    return cum


def _aug_lane(h):
    return 64 if h % 2 == 0 else 0


def _augment_heads(bblk, cumv, q_ref, k_ref, v_ref, st_ref):
    t = bblk.shape[0]
    lane = lax.broadcasted_iota(jnp.int32, (t, LANES), 1)
    lane_b = lane.astype(BF16)
    sub8 = lax.broadcasted_iota(jnp.int32, (SUBLANES, LANES), 0)
    lane8 = lax.broadcasted_iota(jnp.int32, (SUBLANES, LANES), 1)
    one = jnp.ones((t, LANES), BF16)
    zero = jnp.zeros((t, LANES), BF16)
    stats = jnp.zeros((SUBLANES, LANES), F32)
    for p in range(4):
        qblk = bblk[:, LANES * p:LANES * (p + 1)] * SCALE
        kblk = bblk[:, 512 + LANES * p:512 + LANES * (p + 1)]
        vblk = bblk[:, 1024 + LANES * p:1024 + LANES * (p + 1)]
        qf, kf = qblk.astype(F32), kblk.astype(F32)
        q2, k2, qk = qf * qf, kf * kf, qf * kf
        for odd in range(2):
            h = 2 * p + odd
            a0 = _aug_lane(h)
            data_b = (lane_b < 64) if odd == 0 else (lane_b >= 64)
            data = (lane < 64) if odd == 0 else (lane >= 64)
            hi, mid, lo = _split3(jnp.broadcast_to(cumv[:, h:h + 1], (t, LANES)))
            ones3_q = (lane_b >= a0 + 3) & (lane_b < a0 + 6)
            ones3_k = (lane_b >= a0) & (lane_b < a0 + 3)
            aug_q = jnp.where(lane_b == a0, hi, jnp.where(lane_b == a0 + 1, mid, jnp.where(
                lane_b == a0 + 2, lo, jnp.where(ones3_q, one, zero))))
            aug_k = jnp.where(ones3_k, one, jnp.where(lane_b == a0 + 3, -hi, jnp.where(
                lane_b == a0 + 4, -mid, jnp.where(lane_b == a0 + 5, -lo, zero))))
            q_ref[:, LANES * h:LANES * (h + 1)] = jnp.where(data_b, qblk, aug_q)
            k_ref[:, LANES * h:LANES * (h + 1)] = jnp.where(data_b, kblk, aug_k)
            v_ref[:, LANES * h:LANES * (h + 1)] = jnp.where(data_b, vblk, jnp.where(lane_b == a0, one, zero))
            qn = jnp.sqrt(jnp.max(jnp.sum(jnp.where(data, q2, 0.0), axis=-1, keepdims=True)))
            kn = jnp.sqrt(jnp.max(jnp.sum(jnp.where(data, k2, 0.0), axis=-1, keepdims=True)))
            dmin = jnp.min(jnp.sum(jnp.where(data, qk, 0.0), axis=-1, keepdims=True))
            c_first, c_last = cumv[0:1, h:h + 1], cumv[t - 1:t, h:h + 1]
            row = jnp.where(lane8 == 0, qn, jnp.where(lane8 == 1, kn, jnp.where(
                lane8 == 2, c_first, jnp.where(lane8 == 3, c_last, jnp.where(lane8 == 4, dmin, 0.0)))))
            stats = jnp.where(sub8 == h, row, stats)
    st_ref[0] = stats


PRUNE_MARGIN = 88.0


def _fox_tile_ranges(stats):
    nt = stats.shape[0]
    qn, kn, c_first, c_last, d_min = (stats[:, :, n] for n in range(5))
    bound = (1.01 * qn[:, None, :] * kn[None, :, :] - jnp.minimum(d_min, 0.0)[:, None, :] + 0.05
             + c_first[:, None, :] - c_last[None, :, :])
    idx = jnp.arange(nt)
    skip = (bound <= -PRUNE_MARGIN) & (idx[None, :, None] < idx[:, None, None])
    first_key = jnp.sum(jnp.cumprod(skip, axis=1), axis=1)
    needed = (idx[None, :, None] >= first_key[:, None, :]) & (idx[None, :, None] <= idx[:, None, None])
    last_query = jnp.max(jnp.where(needed, idx[:, None, None], 0), axis=0)
    n_query = last_query - idx[:, None] + 1
    table = jnp.zeros((4, SUBLANES, LANES), F32)
    for odd in range(2):
        table = table.at[:, odd, :nt].set(first_key[:, odd::2].T.astype(F32))
        table = table.at[:, 2 + odd, :nt].set(n_query[:, odd::2].T.astype(F32))
    return table


def _lane_scalar(block, row, lane_idx):
    sub8 = lax.broadcasted_iota(jnp.int32, (SUBLANES, LANES), 0)
    lane8 = lax.broadcasted_iota(jnp.int32, (SUBLANES, LANES), 1)
    return jnp.sum(jnp.where((sub8 == row) & (lane8 == lane_idx), block, 0.0)).astype(jnp.int32)


def _fox_fwd(qa, ka, va, ranges, t):
    s = qa.shape[0]
    nt = s // t
    nc = t // LANES

    def body(rg_ref, q_ref, k_ref, v_ref, o_ref, lse_ref):
        i = pl.program_id(1)
        lane = lax.broadcasted_iota(jnp.int32, (t, LANES), 1)
        rows = lax.broadcasted_iota(jnp.int32, (t, t), 0)
        cols = lax.broadcasted_iota(jnp.int32, (t, t), 1)
        firsts = [jnp.clip(_lane_scalar(rg_ref[0], hh, i), 0, i) for hh in range(2)]
        first = jnp.maximum(firsts[0], firsts[1])

        def update(js, carry, heads=(0, 1), diagonal=False):
            offs = [pl.multiple_of(j * t, t) for j in js]
            kts = [k_ref[pl.ds(off, t), :] for off in offs]
            vts = [v_ref[pl.ds(off, t), :] for off in offs]
            scs = {hh: [_dot(q_ref[:, LANES * hh:LANES * (hh + 1)], kt[:, LANES * hh:LANES * (hh + 1)], NT) for kt in kts]
                   for hh in heads}
            if diagonal:
                scs = {hh: [jnp.where(cols <= rows, sc, NEG) for sc in scs[hh]] for hh in heads}
            m_new = {}
            for hh in heads:
                part = None
                for sc in scs[hh]:
                    for cch in range(nc):
                        chunk = sc[:, LANES * cch:LANES * (cch + 1)]
                        part = chunk if part is None else jnp.maximum(part, chunk)
                m_new[hh] = jnp.maximum(carry[2 * hh], jnp.max(part, axis=-1, keepdims=True))
            alphas = {hh: jnp.exp(carry[2 * hh] - m_new[hh]) for hh in heads}
            ps = {hh: [jnp.exp(sc - m_new[hh]).astype(BF16) for sc in scs[hh]] for hh in heads}
            out = list(carry)
            for hh in heads:
                pv = None
                for p, vt in zip(ps[hh], vts):
                    term = _dot(p, vt[:, LANES * hh:LANES * (hh + 1)])
                    pv = term if pv is None else pv + term
                out[2 * hh], out[2 * hh + 1] = m_new[hh], alphas[hh] * carry[2 * hh + 1] + pv
            return tuple(out)

        col0 = jnp.full((t, 1), NEG, F32)
        zero = jnp.zeros((t, LANES), F32)
        carry = (col0, zero, col0, zero)
        for hh in range(2):
            carry = lax.fori_loop(firsts[hh], first, lambda j, cr, hh=hh: update([j], cr, heads=(hh,)), carry)
        n_off = i - first
        carry = lax.fori_loop(0, n_off // 2, lambda u, cr: update([first + 2 * u, first + 2 * u + 1], cr), carry)
        carry = lax.fori_loop(0, n_off % 2, lambda u, cr: update([i - 1], cr), carry)
        m0, acc0, m1, acc1 = update([i], carry, diagonal=True)
        l0, l1 = acc0[:, _aug_lane(0):_aug_lane(0) + 1], acc1[:, _aug_lane(1):_aug_lane(1) + 1]
        o_ref[...] = jnp.where(lane < 64, acc0 * (1.0 / l0), acc1 * (1.0 / l1)).astype(BF16)
        sub = lax.broadcasted_iota(jnp.int32, (SUBLANES, t), 0)
        lse0 = jnp.broadcast_to(m0 + jnp.log(l0), (t, LANES)).T[0:SUBLANES, :]
        lse1 = jnp.broadcast_to(m1 + jnp.log(l1), (t, LANES)).T[0:SUBLANES, :]
        lse_ref[0] = jnp.where(sub == 0, lse0, jnp.where(sub == 1, lse1, 0.0))

    pair = pl.BlockSpec((s, 2 * LANES), lambda p, i: (0, p))
    return pl.pallas_call(
        body, name="fox_fwd", grid=(4, nt),
        out_shape=[jax.ShapeDtypeStruct((s, 512), BF16), jax.ShapeDtypeStruct((4, SUBLANES, s), F32)],
        in_specs=[pl.BlockSpec((1, SUBLANES, LANES), lambda p, i: (p, 0, 0)),
                  pl.BlockSpec((t, 2 * LANES), lambda p, i: (i, p)), pair, pair],
        out_specs=[pl.BlockSpec((t, LANES), lambda p, i: (i, p)),
                   pl.BlockSpec((1, SUBLANES, t), lambda p, i: (p, 0, i))],
        compiler_params=_params(("parallel", "arbitrary"), VMEM_LIMIT),
    )(ranges, qa, ka, va)


def _dup_halves(blk, lane):
    f = blk.astype(F32)
    r = pltpu.roll(f, 64, 1)
    return jnp.where(lane < 64, f, r).astype(BF16), jnp.where(lane >= 64, f, r).astype(BF16)


GROUP = 4
GROUP_ROWS = GROUP * WINDOW


def _stack_heads(ref, g, lane):
    parts = []
    for pb in (2 * g, 2 * g + 1):
        blk = ref[:, LANES * pb:LANES * (pb + 1)]
        zero = jnp.zeros_like(blk)
        parts += [jnp.where(lane < 64, blk, zero), jnp.where(lane >= 64, blk, zero)]
    return jnp.concatenate(parts, axis=0)


def _swa_band(a_ref, ap_ref, g, lane):
    k = jnp.concatenate([_dup_halves(ap_ref[:, 512:640], lane)[g], _dup_halves(a_ref[:, 512:640], lane)[g]], axis=0)
    v = jnp.concatenate([_dup_halves(ap_ref[:, 640:768], lane)[g], _dup_halves(a_ref[:, 640:768], lane)[g]], axis=0)
    return k, v


def _swa_logits(q, k, has_prev):
    sc = _dot(q, k, NT) * SCALE
    rr = lax.broadcasted_iota(jnp.int32, sc.shape, 0) % WINDOW
    cc = lax.broadcasted_iota(jnp.int32, sc.shape, 1)
    valid = (cc > rr) & (cc <= rr + WINDOW) & (has_prev | (cc >= WINDOW))
    return jnp.where(valid, sc, NEG)


def _per_head_column(values):
    return jnp.concatenate([jnp.broadcast_to(v, (WINDOW, 1)) for v in values], axis=0)


SWA_BLOCKS = 4
SWA_ROWS = SWA_BLOCKS * WINDOW


def _swa_blocks(a_ref, ap_ref):
    return [ap_ref] + [a_ref.at[pl.ds(WINDOW * jb, WINDOW), :] for jb in range(SWA_BLOCKS)]


def _swa_fwd(a, sinks):
    s = a.shape[0]

    def body(sink_ref, a_ref, ap_ref, o_ref, l_ref):
        lane = lax.broadcasted_iota(jnp.int32, (WINDOW, LANES), 1)
        blocks = _swa_blocks(a_ref, ap_ref)
        units = [(jb, g) for jb in range(SWA_BLOCKS) for g in range(2)]
        sinks_col = [_per_head_column([sink_ref[GROUP * g + hh] for hh in range(GROUP)]) for g in range(2)]
        bands = [_swa_band(blocks[jb + 1], blocks[jb], g, lane) for jb, g in units]
        scs = [_swa_logits(_stack_heads(blocks[jb + 1], g, lane), bands[u][0],
                           (pl.program_id(0) > 0) if jb == 0 else True) for u, (jb, g) in enumerate(units)]
        ms = [jnp.maximum(jnp.max(scs[u], axis=-1, keepdims=True), sinks_col[g]) for u, (jb, g) in enumerate(units)]
        ps = [jnp.exp(scs[u] - ms[u]) for u in range(len(units))]
        dens = [jnp.sum(ps[u], axis=-1, keepdims=True) + jnp.exp(sinks_col[g] - ms[u]) for u, (jb, g) in enumerate(units)]
        outs = [_dot((ps[u] * (1.0 / dens[u])).astype(BF16), bands[u][1]) for u in range(len(units))]
        for jb in range(SWA_BLOCKS):
            rows = slice(WINDOW * jb, WINDOW * (jb + 1))
            l_all = jnp.zeros((WINDOW, LANES), F32)
            for g in range(2):
                u = 2 * jb + g
                lcol = ms[u] + jnp.log(dens[u])
                for pb in range(2):
                    r0 = 2 * pb * WINDOW
                    o_ref[rows, LANES * (2 * g + pb):LANES * (2 * g + pb + 1)] = jnp.where(
                        lane < 64, outs[u][r0:r0 + WINDOW], outs[u][r0 + WINDOW:r0 + 2 * WINDOW]).astype(BF16)
                for hh in range(GROUP):
                    l_all = jnp.where(lane == GROUP * g + hh, lcol[WINDOW * hh:WINDOW * (hh + 1)], l_all)
            l_ref[rows, :] = l_all

    return pl.pallas_call(
        body, name="swa_fwd", grid=(s // SWA_ROWS,),
        out_shape=[jax.ShapeDtypeStruct((s, 512), BF16), jax.ShapeDtypeStruct((s, LANES), F32)],
        in_specs=[pl.BlockSpec(memory_space=pltpu.SMEM),
                  pl.BlockSpec((SWA_ROWS, W_A), lambda i: (i, 0)),
                  pl.BlockSpec((WINDOW, W_A), lambda i: (jnp.maximum(SWA_BLOCKS * i - 1, 0), 0))],
        out_specs=[pl.BlockSpec((SWA_ROWS, 512), lambda i: (i, 0)), pl.BlockSpec((SWA_ROWS, LANES), lambda i: (i, 0))],
        compiler_params=_params(("parallel",)),
    )(sinks, a, a)


def _mid(att_a, att_b, g, x, target, gate, g_final, wo_a, wo_b, w_out, tm=256):
    s = x.shape[0]
    nt = s // tm

    def body(aa_ref, ab_ref, g_ref, x_ref, t_ref, gate_ref, gf_ref, woa_ref, wob_ref, wout_ref,
             dx_ref, daa_ref, dab_ref, dg_ref, delta_ref, dwoa_ref, dwob_ref, dwout_ref, vec_ref,
             acc_gf, acc_gate, acc_loss):
        step = pl.program_id(0)

        @pl.when(step == 0)
        def _():
            dwoa_ref[...] = jnp.zeros_like(dwoa_ref)
            dwob_ref[...] = jnp.zeros_like(dwob_ref)
            dwout_ref[...] = jnp.zeros_like(dwout_ref)
            acc_gf[...] = jnp.zeros_like(acc_gf)
            acc_gate[...] = jnp.zeros_like(acc_gate)
            acc_loss[...] = jnp.zeros_like(acc_loss)

        def fold(v):
            return jnp.sum(v.reshape(tm // SUBLANES, SUBLANES, D_MODEL), axis=0)

        gate = gate_ref[...]
        gfin = gf_ref[...]
        branches = []
        for att_ref, z_off, wo_ref in ((aa_ref, 0, woa_ref), (ab_ref, 512, wob_ref)):
            att = att_ref[...].astype(F32)
            z = g_ref[:, z_off:z_off + 512].astype(F32)
            sz = _sigmoid(z)
            silu = z * sz
            u = (att * silu).astype(BF16)
            branches.append((att, z, sz, silu, u, _dot(u, wo_ref[...])))
        ga = g_ref[:, 1024:2048].astype(F32)
        gb = g_ref[:, 2048:3072].astype(F32)
        sga, sgb = _sigmoid(ga), _sigmoid(gb)
        y_a, y_b = branches[0][5], branches[1][5]
        mb = (sga * y_a + sgb * y_b).astype(BF16)
        o = _dot(mb, wout_ref[...])
        x2 = x_ref[...] + gate * o
        r2 = lax.rsqrt(jnp.mean(x2 * x2, axis=-1, keepdims=True) + NORM_EPS)
        xn2 = x2 * r2
        err = xn2 * gfin - t_ref[...]
        acc_loss[...] += fold(err * err)
        dy = err * (1.0 / D_MODEL)
        acc_gf[...] += fold(dy * xn2)
        dxn = dy * gfin
        dx2 = r2 * (dxn - xn2 * jnp.mean(dxn * xn2, axis=-1, keepdims=True))
        dx_ref[...] = dx2
        acc_gate[...] += fold(dx2 * o)
        d_o = (dx2 * gate).astype(BF16)
        dwout_ref[...] += _dot(mb, d_o, TN)
        dm = _dot(d_o, wout_ref[...], NT)
        dg_ref[:, 1024:2048] = (dm * y_a * sga * (1.0 - sga)).astype(BF16)
        dg_ref[:, 2048:3072] = (dm * y_b * sgb * (1.0 - sgb)).astype(BF16)
        for (att, z, sz, silu, u, _), sg, wo_ref, dwo_ref, datt_ref, z_off in (
                (branches[0], sga, woa_ref, dwoa_ref, daa_ref, 0), (branches[1], sgb, wob_ref, dwob_ref, dab_ref, 512)):
            dyb = (dm * sg).astype(BF16)
            dwo_ref[...] += _dot(u, dyb, TN)
            du = _dot(dyb, wo_ref[...], NT)
            datt = du * silu
            datt_ref[...] = datt.astype(BF16)
            dg_ref[:, z_off:z_off + 512] = (du * att * (sz * (1.0 + z * (1.0 - sz)))).astype(BF16)
            if z_off == 512:
                prod = datt * att
                hi = prod.astype(BF16)
                lo = (prod - hi.astype(F32)).astype(BF16)
                er = lax.broadcasted_iota(jnp.int32, (512, LANES), 0)
                ec = lax.broadcasted_iota(jnp.int32, (512, LANES), 1)
                e = (er // HEAD_DIM == ec).astype(BF16)
                delta = _dot(hi, e) + _dot(lo, e)
                delta_ref[...] = delta.T[0:SUBLANES, :]

        @pl.when(step == nt - 1)
        def _():
            sub = lax.broadcasted_iota(jnp.int32, (SUBLANES, D_MODEL), 0)
            dgf = jnp.sum(acc_gf[...], axis=0, keepdims=True)
            dgate = jnp.sum(acc_gate[...], axis=0, keepdims=True)
            loss = 0.5 * jnp.sum(acc_loss[...]) * (1.0 / D_MODEL)
            vec_ref[...] = jnp.where(sub == 0, dgf, jnp.where(sub == 1, dgate, jnp.where(sub == 2, loss, 0.0)))

    row = lambda w: pl.BlockSpec((tm, w), lambda i: (i, 0))
    return pl.pallas_call(
        body, name="mid", grid=(nt,),
        out_shape=[jax.ShapeDtypeStruct((s, D_MODEL), F32), jax.ShapeDtypeStruct((s, 512), BF16),
                   jax.ShapeDtypeStruct((s, 512), BF16), jax.ShapeDtypeStruct((s, W_G), BF16),
                   jax.ShapeDtypeStruct((SUBLANES, s), F32),
                   jax.ShapeDtypeStruct((512, D_MODEL), F32), jax.ShapeDtypeStruct((512, D_MODEL), F32),
                   jax.ShapeDtypeStruct((D_MODEL, D_MODEL), F32), jax.ShapeDtypeStruct((SUBLANES, D_MODEL), F32)],
        in_specs=[row(512), row(512), row(W_G), row(D_MODEL), row(D_MODEL),
                  _const_spec((1, D_MODEL)), _const_spec((1, D_MODEL)),
                  _const_spec((512, D_MODEL)), _const_spec((512, D_MODEL)), _const_spec((D_MODEL, D_MODEL))],
        out_specs=[row(D_MODEL), row(512), row(512), row(W_G),
                   pl.BlockSpec((SUBLANES, tm), lambda i: (0, i)),
                   pl.BlockSpec((512, D_MODEL), lambda i: (0, 0)), pl.BlockSpec((512, D_MODEL), lambda i: (0, 0)),
                   pl.BlockSpec((D_MODEL, D_MODEL), lambda i: (0, 0)), pl.BlockSpec((SUBLANES, D_MODEL), lambda i: (0, 0))],
        scratch_shapes=[pltpu.VMEM((SUBLANES, D_MODEL), F32)] * 3,
        compiler_params=_params(("arbitrary",), VMEM_LIMIT),
    )(att_a, att_b, g, x, target, gate, g_final, wo_a, wo_b, w_out)


def _rope_bwd(dt, cos, sin, lane):
    u = dt * sin
    lo = (lane % HEAD_DIM) < (HEAD_DIM // 2)
    return dt * cos + jnp.where(lo, pltpu.roll(u, 96, 1), -pltpu.roll(u, 32, 1))


def _swa_bwd(a, datt, l_all, sinks, cos, sin):
    s = a.shape[0]
    nt = s // SWA_ROWS

    def body(sink_ref, a_ref, ap_ref, do_ref, l_ref, cos_ref, sin_ref, da_ref, ds_ref, halo):
        step = pl.program_id(0)
        tile = nt - 1 - step

        @pl.when(step == 0)
        def _():
            halo[...] = jnp.zeros_like(halo)
            ds_ref[...] = jnp.zeros_like(ds_ref)

        lane = lax.broadcasted_iota(jnp.int32, (WINDOW, LANES), 1)
        sub8 = lax.broadcasted_iota(jnp.int32, (SUBLANES, LANES), 0)
        lane8 = lax.broadcasted_iota(jnp.int32, (SUBLANES, LANES), 1)
        blocks = _swa_blocks(a_ref, ap_ref)
        dsink = jnp.zeros((SUBLANES, LANES), F32)

        def join(pair, r0):
            x0, x1 = pair[0][r0:r0 + WINDOW], pair[1][r0:r0 + WINDOW]
            return jnp.where(lane < 64, x0 + pltpu.roll(x0, 64, 1), x1 + pltpu.roll(x1, 64, 1))

        units = [(jb, g) for jb in range(SWA_BLOCKS) for g in range(2)]
        n_u = len(units)
        sinks_col = [_per_head_column([sink_ref[GROUP * g + hh] for hh in range(GROUP)]) for g in range(2)]
        bands = [_swa_band(blocks[jb + 1], blocks[jb], g, lane) for jb, g in units]
        qs = [_stack_heads(blocks[jb + 1], g, lane) for jb, g in units]
        doms = [_stack_heads(do_ref.at[pl.ds(WINDOW * jb, WINDOW), :], g, lane) for jb, g in units]
        lcols = []
        for jb, g in units:
            lv = l_ref[WINDOW * jb:WINDOW * (jb + 1), :]
            lcols.append(_per_head_column([lv[:, GROUP * g + hh:GROUP * g + hh + 1] for hh in range(GROUP)]))
        ps = [jnp.exp(_swa_logits(qs[u], bands[u][0], (tile > 0) if jb == 0 else True) - lcols[u])
              for u, (jb, g) in enumerate(units)]
        dps = [_dot(doms[u], bands[u][1], NT) for u in range(n_u)]
        deltas = [jnp.sum(ps[u] * dps[u], axis=-1, keepdims=True) for u in range(n_u)]
        for u, (jb, g) in enumerate(units):
            sink_term = jnp.exp(sinks_col[g] - lcols[u]) * deltas[u]
            for hh in range(GROUP):
                tot = jnp.sum(sink_term[WINDOW * hh:WINDOW * (hh + 1)])
                dsink = dsink + jnp.where((sub8 == 0) & (lane8 == GROUP * g + hh), -tot, 0.0)
        dss = [(ps[u] * (dps[u] - deltas[u])).astype(BF16) for u in range(n_u)]
        dqs = [_dot(dss[u], bands[u][0]) * SCALE for u in range(n_u)]
        dks = [_dot(dss[u], qs[u], TN) * SCALE for u in range(n_u)]
        dvs = [_dot(ps[u].astype(BF16), doms[u], TN) for u in range(n_u)]

        carry_k, carry_v = halo[:, 0:LANES], halo[:, LANES:2 * LANES]
        for jb in reversed(range(SWA_BLOCKS)):
            rows = slice(WINDOW * jb, WINDOW * (jb + 1))
            cosv, sinv = cos_ref[rows, :], sin_ref[rows, :]
            for g in range(2):
                dq = dqs[2 * jb + g]
                for pb in range(2):
                    r0 = 2 * pb * WINDOW
                    dq_pair = jnp.where(lane < 64, dq[r0:r0 + WINDOW], dq[r0 + WINDOW:r0 + 2 * WINDOW])
                    da_ref[rows, LANES * (2 * g + pb):LANES * (2 * g + pb + 1)] = _rope_bwd(
                        dq_pair, cosv, sinv, lane).astype(BF16)
            dkb, dvb = dks[2 * jb:2 * jb + 2], dvs[2 * jb:2 * jb + 2]
            da_ref[rows, 512:640] = _rope_bwd(join(dkb, WINDOW) + carry_k, cosv, sinv, lane).astype(BF16)
            da_ref[rows, 640:768] = (join(dvb, WINDOW) + carry_v).astype(BF16)
            carry_k, carry_v = join(dkb, 0), join(dvb, 0)
        halo[:, 0:LANES] = carry_k
        halo[:, LANES:2 * LANES] = carry_v
        ds_ref[...] += dsink

    rev = lambda w: pl.BlockSpec((SWA_ROWS, w), lambda i: (nt - 1 - i, 0))
    return pl.pallas_call(
        body, name="swa_bwd", grid=(nt,),
        out_shape=[jax.ShapeDtypeStruct((s, W_A), BF16), jax.ShapeDtypeStruct((SUBLANES, LANES), F32)],
        in_specs=[pl.BlockSpec(memory_space=pltpu.SMEM), rev(W_A),
                  pl.BlockSpec((WINDOW, W_A), lambda i: (jnp.maximum(SWA_BLOCKS * (nt - 1 - i) - 1, 0), 0)),
                  rev(512), rev(LANES), rev(LANES), rev(LANES)],
        out_specs=[rev(W_A), pl.BlockSpec((SUBLANES, LANES), lambda i: (0, 0))],
        scratch_shapes=[pltpu.VMEM((WINDOW, 2 * LANES), F32)],
        compiler_params=_params(("arbitrary",)),
    )(sinks, a, a, datt, l_all, cos, sin)


def _fox_bwd(qa, ka, vb, do, lse, delta, ranges, t):
    s = qa.shape[0]
    nt = s // t

    def body(rg_ref, q_ref, do_ref, lse_ref, dl_ref, k_ref, v_ref, dq_ref, dk_ref, dv_ref, dc_ref, dr_ref, dq_acc):
        p = pl.program_id(0)
        j = pl.program_id(1)
        n_queries = [jnp.clip(_lane_scalar(rg_ref[0], 2 + hh, j), 1, nt - j) for hh in range(2)]

        @pl.when(j == 0)
        def _():
            dq_acc[...] = jnp.zeros_like(dq_acc)

        lane = lax.broadcasted_iota(jnp.int32, (t, LANES), 1)
        rows = lax.broadcasted_iota(jnp.int32, (t, t), 0)
        cols = lax.broadcasted_iota(jnp.int32, (t, t), 1)
        kt = k_ref[...]
        vt = v_ref[...]

        ks = [kt[:, LANES * hh:LANES * (hh + 1)] for hh in range(2)]

        def tile(qis, carry, heads=(0, 1), diagonal=False):
            dk0, dk1, dv = carry
            offs = [pl.multiple_of(i * t, t) for i in qis]
            units = [(u, hh) for u in range(len(qis)) for hh in heads]
            qts = [q_ref[pl.ds(off, t), :] for off in offs]
            dos = [do_ref[pl.ds(off, t), :] for off in offs]
            lses = [lse_ref[0, :, pl.ds(off, t)] for off in offs]
            dls = [dl_ref[0, :, pl.ds(off, t)] for off in offs]
            qs = [qts[u][:, LANES * hh:LANES * (hh + 1)] for u, hh in units]
            doms = [jnp.where((lane < 64) if hh == 0 else (lane >= 64), dos[u], jnp.zeros_like(dos[u])) for u, hh in units]
            sts = [_dot(ks[hh], qs[n], NT) for n, (u, hh) in enumerate(units)]
            dpts = [_dot(vt, doms[n], NT) for n in range(len(units))]
            if diagonal:
                sts = [jnp.where(cols >= rows, st, NEG) for st in sts]
            pts = [jnp.exp(sts[n] - lses[u][hh:hh + 1, :]) for n, (u, hh) in enumerate(units)]
            dsts = [(pts[n] * (dpts[n] - dls[u][hh:hh + 1, :])).astype(BF16) for n, (u, hh) in enumerate(units)]
            for n, (u, hh) in enumerate(units):
                dv = dv + _dot(pts[n].astype(BF16), doms[n])
                term = _dot(dsts[n], qs[n])
                dk0, dk1 = (dk0 + term, dk1) if hh == 0 else (dk0, dk1 + term)
                dq_acc[hh, pl.ds(offs[u], t), :] += _dot(dsts[n], ks[hh], TN)
            return dk0, dk1, dv

        zero = jnp.zeros((t, LANES), F32)
        carry = tile([j], (zero, zero, zero), diagonal=True)
        n_rest = jnp.minimum(n_queries[0], n_queries[1]) - 1
        carry = lax.fori_loop(0, n_rest // 2, lambda u, cr: tile([j + 1 + 2 * u, j + 2 + 2 * u], cr), carry)
        carry = lax.fori_loop(0, n_rest % 2, lambda u, cr: tile([j + n_rest], cr), carry)
        for hh in range(2):
            carry = lax.fori_loop(j + 1 + n_rest, j + n_queries[hh], lambda i, cr, hh=hh: tile([i], cr, heads=(hh,)), carry)
        dk0, dk1, dv = carry
        e0, e1 = _aug_lane(0), _aug_lane(1)
        dk_ref[...] = jnp.where(lane < 64, dk0, dk1).astype(BF16)
        dv_ref[...] = dv.astype(BF16)
        c0 = jnp.broadcast_to(dk0[:, e0 + 3:e0 + 4], (t, LANES))
        c1 = jnp.broadcast_to(dk1[:, e1 + 3:e1 + 4], (t, LANES))
        dc_ref[0] = jnp.where(lane == 2 * p, -c0, jnp.where(lane == 2 * p + 1, -c1, 0.0))

        @pl.when(j == nt - 1)
        def _():
            lane_s = lax.broadcasted_iota(jnp.int32, (s, LANES), 1)
            a0, a1 = dq_acc[0], dq_acc[1]
            dq_ref[...] = (jnp.where(lane_s < 64, a0, a1) * SCALE).astype(BF16)
            r0 = jnp.broadcast_to(a0[:, e0:e0 + 1], (s, LANES))
            r1 = jnp.broadcast_to(a1[:, e1:e1 + 1], (s, LANES))
            dr_ref[0] = jnp.where(lane_s == 2 * p, r0, jnp.where(lane_s == 2 * p + 1, r1, 0.0))

    return pl.pallas_call(
        body, name="fox_bwd", grid=(4, nt),
        out_shape=[jax.ShapeDtypeStruct((s, 512), BF16), jax.ShapeDtypeStruct((s, 512), BF16),
                   jax.ShapeDtypeStruct((s, 512), BF16), jax.ShapeDtypeStruct((4, s, LANES), F32),
                   jax.ShapeDtypeStruct((4, s, LANES), F32)],
        in_specs=[pl.BlockSpec((1, SUBLANES, LANES), lambda p, j: (p, 0, 0)),
                  pl.BlockSpec((s, 2 * LANES), lambda p, j: (0, p)),
                  pl.BlockSpec((s, LANES), lambda p, j: (0, p)),
                  pl.BlockSpec((1, SUBLANES, s), lambda p, j: (p, 0, 0)),
                  pl.BlockSpec((1, SUBLANES, s), lambda p, j: (p, 0, 0)),
                  pl.BlockSpec((t, 2 * LANES), lambda p, j: (j, p)),
                  pl.BlockSpec((t, LANES), lambda p, j: (j, p))],
        out_specs=[pl.BlockSpec((s, LANES), lambda p, j: (0, p)),
                   pl.BlockSpec((t, LANES), lambda p, j: (j, p)),
                   pl.BlockSpec((t, LANES), lambda p, j: (j, p)),
                   pl.BlockSpec((1, t, LANES), lambda p, j: (p, j, 0)),
                   pl.BlockSpec((1, s, LANES), lambda p, j: (p, 0, 0))],
        scratch_shapes=[pltpu.VMEM((2, s, LANES), F32)],
        compiler_params=_params(("parallel", "arbitrary"), VMEM_LIMIT),
    )(ranges, qa, do, lse, delta, ka, vb)


def _forget_logit_grad(dc_ref, dr_ref, f, b_f, carry):
    tb = f.shape[0]
    lane = lax.broadcasted_iota(jnp.int32, (tb, LANES), 1)
    dc = dc_ref[0] + dr_ref[0]
    for k in range(1, 4):
        dc = dc + (dc_ref[k] + dr_ref[k])
    hi, mid, lo = _split3(dc)
    rows = lax.broadcasted_iota(jnp.int32, (tb, tb), 0)
    cols = lax.broadcasted_iota(jnp.int32, (tb, tb), 1)
    triu = (cols >= rows).astype(BF16)
    dlogf = _dot(triu, hi) + _dot(triu, mid) + _dot(triu, lo) + carry[0:1, :]
    carry[...] = jnp.broadcast_to(dlogf[0:1, :], carry.shape)
    return jnp.where(lane < N_HEADS, dlogf * _sigmoid(-(f + b_f)), 0.0)


def _dh_norm_bwd(d_a, d_q, d_k, d_v, dcum_k, dcum_q, f, bf_pad, d_g, w_t, x, dx2, gnorm, scale1, tm=512):
    s = x.shape[0]
    nt = s // tm

    def body(da_ref, dq_ref, dk_ref, dv_ref, dc_ref, dr_ref, f_ref, bf_ref, dg_ref, w_ref, x_ref, dx2_ref, g_ref, sc_ref,
             gx_ref, vec_ref, df_ref, db_ref, a_sh, a_sc, a_g, carry):
        step = pl.program_id(0)

        @pl.when(step == 0)
        def _():
            a_sh[...] = jnp.zeros_like(a_sh)
            a_sc[...] = jnp.zeros_like(a_sc)
            a_g[...] = jnp.zeros_like(a_g)
            carry[...] = jnp.zeros_like(carry)
            db_ref[...] = jnp.zeros_like(db_ref)

        def fold(v):
            return jnp.sum(v.reshape(tm // SUBLANES, SUBLANES, D_MODEL), axis=0)

        dfb = _forget_logit_grad(dc_ref, dr_ref, f_ref[...], bf_ref[...], carry)
        d_f = dfb.astype(BF16)
        df_ref[...] = d_f
        sub8 = lax.broadcasted_iota(jnp.int32, (SUBLANES, LANES), 0)
        db_ref[...] += jnp.where(sub8 == 0, jnp.sum(dfb, axis=0, keepdims=True), 0.0)
        d_all = jnp.concatenate([da_ref[...], dq_ref[...], dk_ref[...], dv_ref[...], d_f, dg_ref[...]], axis=1)
        dh = _dot(d_all, w_ref[...])
        xv = x_ref[...]
        r = lax.rsqrt(jnp.mean(xv * xv, axis=-1, keepdims=True) + NORM_EPS)
        xn = xv * r
        gn = g_ref[...]
        a_sh[...] += fold(dh)
        a_sc[...] += fold(dh * (xn * gn))
        dn1 = dh * sc_ref[...]
        a_g[...] += fold(dn1 * xn)
        dxn = dn1 * gn
        gx_ref[...] = dx2_ref[...] + r * (dxn - xn * jnp.mean(dxn * xn, axis=-1, keepdims=True))

        @pl.when(step == nt - 1)
        def _():
            sub = lax.broadcasted_iota(jnp.int32, (SUBLANES, D_MODEL), 0)
            v_sh = jnp.sum(a_sh[...], axis=0, keepdims=True)
            v_sc = jnp.sum(a_sc[...], axis=0, keepdims=True)
            v_g = jnp.sum(a_g[...], axis=0, keepdims=True)
            vec_ref[...] = jnp.where(sub == 0, v_sh, jnp.where(sub == 1, v_sc, jnp.where(sub == 2, v_g, 0.0)))

    row = lambda w: pl.BlockSpec((tm, w), lambda i: (nt - 1 - i, 0))
    slabs = pl.BlockSpec((4, tm, LANES), lambda i: (0, nt - 1 - i, 0))
    return pl.pallas_call(
        body, name="dh_norm_bwd", grid=(nt,),
        out_shape=[jax.ShapeDtypeStruct((s, D_MODEL), F32), jax.ShapeDtypeStruct((SUBLANES, D_MODEL), F32),
                   jax.ShapeDtypeStruct((s, LANES), BF16), jax.ShapeDtypeStruct((SUBLANES, LANES), F32)],
        in_specs=[row(W_A), row(512), row(512), row(512), slabs, slabs, row(W_F), _const_spec((1, LANES)), row(W_G),
                  _const_spec((W_INT, D_MODEL)), row(D_MODEL), row(D_MODEL), _const_spec((1, D_MODEL)),
                  _const_spec((1, D_MODEL))],
        out_specs=[row(D_MODEL), pl.BlockSpec((SUBLANES, D_MODEL), lambda i: (0, 0)), row(LANES),
                   pl.BlockSpec((SUBLANES, LANES), lambda i: (0, 0))],
        scratch_shapes=[pltpu.VMEM((SUBLANES, D_MODEL), F32)] * 3 + [pltpu.VMEM((SUBLANES, LANES), F32)],
        compiler_params=_params(("arbitrary",), VMEM_LIMIT),
    )(d_a, d_q, d_k, d_v, dcum_k, dcum_q, f, bf_pad, d_g, w_t, x, dx2, gnorm, scale1)


def _dw_in(h_t, d_a, d_q, d_k, d_v, d_f, d_g, ts=1024, tc=512):
    s = h_t.shape[1]
    ns = s // ts
    w_tail = W_F + W_G - 512
    w_fg = 512 + w_tail
    rows = 128

    def body(h_ref, da_ref, dq_ref, dk_ref, dv_ref, df_ref, dg_ref, o_ref, acc_a, acc_q, acc_k, acc_v, acc_fg, sem):
        k = pl.program_id(0)
        accs = (acc_a, acc_q, acc_k, acc_v, acc_fg)

        @pl.when(k == 0)
        def _():
            for acc in accs:
                acc[...] = jnp.zeros_like(acc)

        hv = h_ref[...]

        def add(acc, c_acc, d_ref, c_d, width):
            for c0 in range(0, width, tc):
                w = min(tc, width - c0)
                acc[:, c_acc + c0:c_acc + c0 + w] += _dot(hv, d_ref[:, c_d + c0:c_d + c0 + w])

        add(acc_a, 0, da_ref, 0, W_A)
        add(acc_q, 0, dq_ref, 0, 512)
        add(acc_k, 0, dk_ref, 0, 512)
        add(acc_v, 0, dv_ref, 0, 512)
        add(acc_fg, 0, dg_ref, 0, 512)
        add(acc_fg, 512, df_ref, 0, W_F)
        add(acc_fg, 512 + W_F, dg_ref, 512, W_G - 512)

        @pl.when(k == ns - 1)
        def _():
            for r0 in range(0, D_MODEL, rows):
                rs = slice(r0, r0 + rows)
                acc_fg[rs, 512:w_fg] = jnp.concatenate(
                    [acc_fg[rs, 512:512 + N_HEADS], acc_fg[rs, 512 + W_F:w_fg],
                     jnp.zeros((rows, W_F - N_HEADS), F32)], axis=1)
            pieces = [(acc_a, 0, W_A, 0), (acc_fg, 0, 512, R_ZA), (acc_q, 0, 512, R_QB), (acc_k, 0, 512, R_QB + 512),
                      (acc_v, 0, 512, R_QB + 1024), (acc_fg, 512, w_tail, R_FB)]
            copies = [pltpu.make_async_copy(acc.at[:, pl.ds(c0, w)], o_ref.at[:, pl.ds(dst, w)], sem.at[g])
                      for g, (acc, c0, w, dst) in enumerate(pieces)]
            for cp in copies:
                cp.start()
            for cp in copies:
                cp.wait()

    spec = lambda d: pl.BlockSpec((ts, d.shape[1]), lambda k: (k, 0))
    return pl.pallas_call(
        body, name="dw_in", grid=(ns,),
        out_shape=jax.ShapeDtypeStruct((D_MODEL, R_FB + w_tail), F32),
        in_specs=[pl.BlockSpec((D_MODEL, ts), lambda k: (0, k))] + [spec(d) for d in (d_a, d_q, d_k, d_v, d_f, d_g)],
        out_specs=pl.BlockSpec(memory_space=pl.ANY),
        scratch_shapes=[pltpu.VMEM((D_MODEL, W_A), F32)] + [pltpu.VMEM((D_MODEL, 512), F32)] * 3
        + [pltpu.VMEM((D_MODEL, w_fg), F32), pltpu.SemaphoreType.DMA((6,))],
        compiler_params=_params(("arbitrary",), VMEM_LIMIT),
    )(h_t, d_a, d_q, d_k, d_v, d_f, d_g)


def _small_grads(packs, c_t, dada_shard):
    def body(p_ref, ct_ref, da_ref, sum_ref, gw_ref):
        acc = p_ref[0]
        for dev in range(1, 8):
            acc = acc + p_ref[dev]
        sum_ref[...] = acc
        gw_ref[...] = jnp.dot(ct_ref[...], da_ref[...], preferred_element_type=F32, precision=lax.Precision.HIGHEST)

    return pl.pallas_call(
        body, name="small_grads",
        out_shape=[jax.ShapeDtypeStruct(packs.shape[1:], F32),
                   jax.ShapeDtypeStruct((c_t.shape[0], dada_shard.shape[1]), F32)],
    )(packs, c_t, dada_shard)


def _adamw_body(w_ref, g_ref, m_ref, v_ref, d_ref, mo_ref, vo_ref):
    c1 = 1.0 / (1.0 - ADAM_B1 ** ADAM_STEP)
    c2 = 1.0 / (1.0 - ADAM_B2 ** ADAM_STEP)
    gv = g_ref[...]
    mn = ADAM_B1 * m_ref[...] + (1.0 - ADAM_B1) * gv
    vn = ADAM_B2 * v_ref[...] + (1.0 - ADAM_B2) * (gv * gv)
    mo_ref[...] = mn
    vo_ref[...] = vn
    d_ref[...] = -ADAM_LR * ((mn * c1) / (jnp.sqrt(vn * c2) + ADAM_EPS) + ADAM_WD * w_ref[...])


def _adamw3(w, g, m, v, name, tb=128):
    spec = pl.BlockSpec((tb, SUBLANES, LANES), lambda i: (i, 0, 0))
    return pl.pallas_call(
        functools.partial(_adamw_body), name=name, grid=(pl.cdiv(w.shape[0], tb),),
        out_shape=[jax.ShapeDtypeStruct(w.shape, F32)] * 3,
        in_specs=[spec] * 4, out_specs=[spec] * 3,
        compiler_params=_params(("parallel",)),
    )(w, g, m, v)


def _adamw(w, g, m, v, name):
    r, c = w.shape
    tr = 128 if r % 128 == 0 else r
    body = functools.partial(_adamw_body)
    spec = pl.BlockSpec((tr, c), lambda i: (i, 0))
    return pl.pallas_call(
        body, name=name, grid=(r // tr,),
        out_shape=[jax.ShapeDtypeStruct((r, c), F32)] * 3,
        in_specs=[spec] * 4, out_specs=[spec] * 3,
        compiler_params=_params(("parallel",)),
    )(w, g, m, v)


def _rope_inputs(positions):
    inv_freq = 10000.0 ** (-jnp.arange(0, HEAD_DIM, 2, dtype=F32) / HEAD_DIM)
    pos = jnp.broadcast_to(positions.astype(F32)[:, None], (positions.shape[0], LANES))
    return pos, jnp.tile(inv_freq, 4)[None, :]


def _pad_rows(v, rows=SUBLANES):
    return jnp.pad(v, ((0, rows - v.shape[0]), (0, 0)))


def kernel(x, c, positions, w_ada, b_ada, g_norm, w_in, b_f, sinks, w_o_swa, w_o_fox, w_out, g_final, loss_target, m_w_ada, m_b_ada, m_g_norm, m_w_in, m_b_f, m_sinks, m_w_o_swa, m_w_o_fox, m_w_out, m_g_final, v_w_ada, v_b_ada, v_g_norm, v_w_in, v_b_f, v_sinks, v_w_o_swa, v_w_o_fox, v_w_out, v_g_final):
    ix, iy, ic = lax.axis_index("x"), lax.axis_index("y"), lax.axis_index("c")
    chip = 2 * ix + iy
    dev = 2 * chip + ic
    xs, tgt = x[0], loss_target[0]
    s = xs.shape[0]

    b_ada_shard = lax.dynamic_slice(b_ada, (0, chip * 768), (1, 768))
    ada_parts, g_in, g_oa, g_ob, g_out = _gather_inputs(
        _pad_rows(c), w_ada[0], b_ada_shard, [w_in[0], w_o_swa[0], w_o_fox[0], w_out[0]], "gather_inputs")
    ada = lax.dynamic_index_in_dim(ada_parts, dev, axis=1, keepdims=False).reshape(1, 3 * D_MODEL)
    shift, scale, gate = ada[:, :D_MODEL], ada[:, D_MODEL:2 * D_MODEL], ada[:, 2 * D_MODEL:]
    scale1 = 1.0 + scale

    w_ref_order = jnp.transpose(g_in, (1, 0, 2)).reshape(D_MODEL, R_END)
    w_int = jnp.concatenate([
        w_ref_order[:, :R_ZA], w_ref_order[:, R_QB:R_FB], w_ref_order[:, R_FB:R_ZB],
        jnp.zeros((D_MODEL, W_F - N_HEADS), BF16), w_ref_order[:, R_ZA:R_QB], w_ref_order[:, R_ZB:]], axis=1)
    w_int_t = w_int.T
    wo_a = jnp.transpose(g_oa, (1, 0, 2)).reshape(512, D_MODEL)
    wo_b = jnp.transpose(g_ob, (1, 0, 2)).reshape(512, D_MODEL)
    w_o = g_out.reshape(D_MODEL, D_MODEL)

    pos, freq = _rope_inputs(positions[0])
    bf_pad = jnp.pad(b_f, ((0, 0), (0, LANES - N_HEADS)))
    sink_vec = sinks[0]

    a, vb, f, g, h_t, cos, sin, qa, ka, va, stats = _norm_proj(
        xs, g_norm * scale1, shift, w_int, pos, freq, bf_pad, FOX_TILE)
    att_a, l_swa = _swa_fwd(a, sink_vec)
    ranges = _fox_tile_ranges(stats)
    att_b, lse = _fox_fwd(qa, ka, va, ranges, FOX_TILE)

    dx2, datt_a, datt_b, d_g, delta8, dwo_a, dwo_b, dw_out, vec_mid = _mid(
        att_a, att_b, g, xs, tgt, gate, g_final.reshape(1, D_MODEL), wo_a, wo_b, w_o)
    delta = jnp.pad(delta8.reshape(4, 2, s), ((0, 0), (0, SUBLANES - 2), (0, 0)))
    d_a, dsink = _swa_bwd(a, datt_a, l_swa, sink_vec, cos, sin)
    dq, dk, dv, dcum_k, dcum_q = _fox_bwd(qa, ka, vb, datt_b, lse, delta, ranges, FOX_TILE)
    grad_x, vec_dh, d_f, dbf = _dh_norm_bwd(
        d_a, dq, dk, dv, dcum_k, dcum_q, f, bf_pad, d_g, w_int_t, xs, dx2, g_norm, scale1)
    dw_in = _dw_in(h_t, d_a, dq, dk, dv, d_f, d_g)[:, :R_END]

    tail = jnp.pad(jnp.concatenate([dbf[0:1, :N_HEADS], dsink[0:1, :N_HEADS]], axis=1), ((0, 0), (0, D_MODEL - 2 * N_HEADS)))
    pack = jnp.concatenate([c, vec_dh[0:2], vec_mid[1:2], vec_dh[2:3], vec_mid[0:1], tail, vec_mid[2:3]], axis=0)

    def slots(w, axis):
        if axis == 1:
            return jnp.transpose(w.reshape(w.shape[0], 4, w.shape[1] // 4), (1, 0, 2))
        return w.reshape(4, w.shape[0] // 4, w.shape[1])

    packs, g_wo_a, g_wo_b, g_w_out, g_w_in = _reduce_scatter(
        [slots(dwo_a, 1), slots(dwo_b, 1), slots(dw_out, 0), slots(dw_in, 1)], pack, "reduce_grads")
    dada_all = packs[:, 1:4, :].reshape(8, 3 * D_MODEL)
    dada_shard = lax.dynamic_slice(dada_all, (0, chip * 768), (8, 768))
    sums, g_w_ada = _small_grads(packs, packs[:, 0, :].T, dada_shard)
    g_b_ada = sums[1:4].reshape(1, 3 * D_MODEL)
    g_g_norm = sums[4:5]
    g_g_final = sums[5]
    g_b_f = sums[6:7, :N_HEADS]
    g_sinks = sums[6:7, N_HEADS:2 * N_HEADS]
    loss = sums[7, 0]

    grads = {
        "w_ada": g_w_ada, "b_ada": g_b_ada, "g_norm": g_g_norm, "w_in": g_w_in, "b_f": g_b_f, "sinks": g_sinks,
        "w_o_swa": g_wo_a, "w_o_fox": g_wo_b, "w_out": g_w_out, "g_final": g_g_final,
    }
    params = {
        "w_ada": (w_ada, m_w_ada, v_w_ada), "b_ada": (b_ada, m_b_ada, v_b_ada), "g_norm": (g_norm, m_g_norm, v_g_norm),
        "w_in": (w_in, m_w_in, v_w_in), "b_f": (b_f, m_b_f, v_b_f), "sinks": (sinks, m_sinks, v_sinks),
        "w_o_swa": (w_o_swa, m_w_o_swa, v_w_o_swa), "w_o_fox": (w_o_fox, m_w_o_fox, v_w_o_fox),
        "w_out": (w_out, m_w_out, v_w_out), "g_final": (g_final, m_g_final, v_g_final),
    }
    n_col = w_in.shape[2]

    def as_stored(t):
        return jnp.transpose(t, (2, 0, 1)).reshape(n_col, SUBLANES, LANES)

    def from_stored(t):
        return jnp.transpose(t, (1, 2, 0)).reshape(1, D_MODEL, n_col)

    names = list(grads)
    out_g, out_d, out_m, out_v = [], [], [], []
    for nm in names:
        w, m, v = params[nm]
        if nm == "w_in":
            g_st = as_stored(grads[nm][None])
            d_, m_, v_ = _adamw3(as_stored(w), g_st, as_stored(m), as_stored(v), "adamw_" + nm)
            res = [from_stored(t) for t in (g_st, d_, m_, v_)]
        else:
            shape2 = (w.shape[-2], w.shape[-1]) if w.ndim >= 2 else (1, w.shape[0])
            d_, m_, v_ = _adamw(w.reshape(shape2), grads[nm].reshape(shape2), m.reshape(shape2), v.reshape(shape2), "adamw_" + nm)
            res = [t.reshape(w.shape) for t in (grads[nm], d_, m_, v_)]
        out_g.append(res[0])
        out_d.append(res[1])
        out_m.append(res[2])
        out_v.append(res[3])
    return (loss, grad_x[None], *out_g, *out_d, *out_m, *out_v)
```

```python
import functools

import numpy as np
import jax
import jax.numpy as jnp
from jax import lax
from jax.experimental import pallas as pl
from jax.experimental.pallas import tpu as pltpu

F32 = jnp.float32
BF16 = jnp.bfloat16
MESH = pl.DeviceIdType.MESH

D_MODEL = 1024
HEAD_DIM = 64
N_HEADS = 8
WINDOW = 128
NORM_EPS = 1e-6
SCALE = HEAD_DIM ** -0.5
NEG = -1e30
LANES = 128
SUBLANES = 8
VMEM_LIMIT = 60 * 1024 * 1024
FOX_TILE = 512

W_A, W_B, W_F, W_G = 768, 1536, 128, 3072
OFF_A, OFF_B, OFF_F, OFF_G = 0, 768, 2304, 2432
W_INT = W_A + W_B + W_F + W_G
R_ZA, R_QB, R_FB, R_ZB, R_END = 768, 1280, 2816, 2824, 5384

ADAM_LR, ADAM_B1, ADAM_B2, ADAM_EPS, ADAM_WD, ADAM_STEP = 0.001, 0.9, 0.999, 1e-08, 0.01, 10

NT = (((1,), (1,)), ((), ()))
TN = (((0,), (0,)), ((), ()))


def _dot(a, b, dims=None):
    if dims is None:
        return jnp.dot(a, b, preferred_element_type=F32)
    return lax.dot_general(a, b, dims, preferred_element_type=F32)


def _split3(v):
    hi = v.astype(BF16)
    r1 = v - hi.astype(F32)
    mid = r1.astype(BF16)
    lo = (r1 - mid.astype(F32)).astype(BF16)
    return hi, mid, lo


def _sigmoid(v):
    return 1.0 / (1.0 + jnp.exp(-v))


def _params(sem=None, vmem=None):
    return pltpu.CompilerParams(dimension_semantics=sem, vmem_limit_bytes=vmem)


def _const_spec(shape):
    nd = len(shape)
    return pl.BlockSpec(shape, lambda *_: (0,) * nd, pipeline_mode=pl.Buffered(1))


def _flip(v, f):
    return 1 - v if f else v


_CHIP_FLIPS = ((1, 0), (0, 1), (1, 1))


def _gather_inputs(c_pad, w_ada, b_ada_shard, shards, name):
    n = len(shards)
    n_col = w_ada.shape[1]

    def body(*refs):
        c_ref, wa_ref, ba_ref = refs[:3]
        ins = refs[3:3 + n]
        ada_ref = refs[3 + n]
        outs = refs[4 + n:4 + 2 * n]
        call_ref, send_sems, recv_sems = refs[4 + 2 * n:7 + 2 * n]
        x, y, c = lax.axis_index("x"), lax.axis_index("y"), lax.axis_index("c")
        k_me = 2 * x + y
        me = 2 * k_me + c
        sibling = (x, y, 1 - c)
        chips = [(_flip(x, fx), _flip(y, fy)) for fx, fy in _CHIP_FLIPS]

        def piece(i, chip_k, half):
            hr = ins[i].shape[0] // 2
            return outs[i].at[chip_k, pl.ds(half * hr, hr), :]

        def copy(i, slot, chip_k, half, to):
            return pltpu.make_async_remote_copy(
                src_ref=piece(i, chip_k, half), dst_ref=piece(i, chip_k, half),
                send_sem=send_sems.at[6 * i + slot], recv_sem=recv_sems.at[6 * i + slot],
                device_id=to, device_id_type=MESH)

        def small(ref, slot, sem, to):
            return pltpu.make_async_remote_copy(
                src_ref=ref.at[slot], dst_ref=ref.at[slot], send_sem=send_sems.at[6 * n + sem],
                recv_sem=recv_sems.at[6 * n + sem], device_id=to, device_id_type=MESH)

        for i in range(n):
            outs[i][k_me] = ins[i][...].astype(BF16)
        started = []
        for i in range(n):
            for j, chip in enumerate(chips):
                cp = copy(i, j, k_me, c, (chip[0], chip[1], c))
                cp.start()
                started.append(cp)

        call_ref[me] = c_ref[...]
        peers = [(_flip(x, k & 4), _flip(y, k & 2), _flip(c, k & 1)) for k in range(1, 8)]
        for k, peer in enumerate(peers):
            cp = small(call_ref, me, k, peer)
            cp.start()
            started.append(cp)
        for k, peer in enumerate(peers):
            small(call_ref, 4 * peer[0] + 2 * peer[1] + peer[2], k, peer).wait_recv()
        c_all = call_ref[:, 0, :].astype(BF16)
        ada_ref[k_me] = _dot(c_all, wa_ref[...].astype(BF16)) + ba_ref[...]
        for j, chip in enumerate(chips):
            cp = small(ada_ref, k_me, 7 + j, (chip[0], chip[1], c))
            cp.start()
            started.append(cp)

        for j, chip in enumerate(chips):
            chip_k = 2 * chip[0] + chip[1]
            for i in range(n):
                copy(i, j, chip_k, c, (chip[0], chip[1], c)).wait_recv()
                cp = copy(i, 3 + j, chip_k, c, sibling)
                cp.start()
                started.append(cp)
        for j, chip in enumerate(chips):
            chip_k = 2 * chip[0] + chip[1]
            small(ada_ref, chip_k, 7 + j, (chip[0], chip[1], c)).wait_recv()
            for i in range(n):
                copy(i, 3 + j, chip_k, 1 - c, sibling).wait_recv()
        for cp in started:
            cp.wait_send()

    vmem = pl.BlockSpec(memory_space=pltpu.VMEM)
    return pl.pallas_call(
        body, name=name,
        out_shape=[jax.ShapeDtypeStruct((4, 8, n_col), F32)] + [jax.ShapeDtypeStruct((4,) + s.shape, BF16) for s in shards],
        in_specs=[vmem] * (3 + n),
        out_specs=[vmem] * (1 + n),
        scratch_shapes=[pltpu.VMEM((8,) + c_pad.shape, F32),
                        pltpu.SemaphoreType.DMA((6 * n + 10,)), pltpu.SemaphoreType.DMA((6 * n + 10,))],
        compiler_params=_params(vmem=VMEM_LIMIT),
    )(c_pad, w_ada, b_ada_shard, *shards)


def _reduce_scatter(pieces, pack, name):
    n = len(pieces)

    def body(*refs):
        pack_ref, ins = refs[0], refs[1:1 + n]
        packs_ref, outs = refs[1 + n], refs[2 + n:2 + 2 * n]
        rest = refs[2 + 2 * n:]
        own, got = rest[:n], rest[n:2 * n]
        sendb, recvb = rest[2 * n:3 * n], rest[3 * n:4 * n]
        send_sems, recv_sems, local_sems = rest[4 * n:4 * n + 3]
        x, y, c = lax.axis_index("x"), lax.axis_index("y"), lax.axis_index("c")
        k_me = 2 * x + y
        me = 2 * k_me + c
        sibling = (x, y, 1 - c)
        chips = [(_flip(x, fx), _flip(y, fy)) for fx, fy in _CHIP_FLIPS]
        hrs = [p.shape[1] // 2 for p in pieces]

        def remote(i, slot, src, dst, to):
            return pltpu.make_async_remote_copy(
                src_ref=src, dst_ref=dst, send_sem=send_sems.at[5 * i + slot], recv_sem=recv_sems.at[5 * i + slot],
                device_id=to, device_id_type=MESH)

        started = []
        packs_ref[me] = pack_ref[...]
        peers = [(_flip(x, k & 4), _flip(y, k & 2), _flip(c, k & 1)) for k in range(1, 8)]
        for k, peer in enumerate(peers):
            cp = pltpu.make_async_remote_copy(
                src_ref=pack_ref, dst_ref=packs_ref.at[me], send_sem=send_sems.at[5 * n + k],
                recv_sem=recv_sems.at[5 * n + k], device_id=peer, device_id_type=MESH)
            cp.start()
            started.append(cp)
        loads = []
        for i in range(n):
            ld = pltpu.make_async_copy(ins[i].at[:, pl.ds(c * hrs[i], hrs[i]), :], own[i], local_sems.at[i])
            ld.start()
            loads.append(ld)
            cp = remote(i, 0, ins[i].at[:, pl.ds((1 - c) * hrs[i], hrs[i]), :], got[i], sibling)
            cp.start()
            started.append(cp)
        for i in range(n):
            loads[i].wait()
            remote(i, 0, ins[i].at[:, pl.ds(c * hrs[i], hrs[i]), :], got[i], sibling).wait_recv()
            for j, chip in enumerate(chips):
                chip_k = 2 * chip[0] + chip[1]
                sendb[i][j] = (own[i][chip_k] + got[i][chip_k]).astype(BF16)
                cp = remote(i, 1 + j, sendb[i].at[j], recvb[i].at[j], (chip[0], chip[1], c))
                cp.start()
                started.append(cp)
        for i in range(n):
            acc = own[i][k_me] + got[i][k_me]
            for j, chip in enumerate(chips):
                remote(i, 1 + j, sendb[i].at[j], recvb[i].at[j], (chip[0], chip[1], c)).wait_recv()
                acc = acc + recvb[i][j].astype(F32)
            mine = outs[i].at[pl.ds(c * hrs[i], hrs[i]), :]
            outs[i][pl.ds(pl.multiple_of(c * hrs[i], SUBLANES), hrs[i]), :] = acc
            cp = remote(i, 4, mine, mine, sibling)
            cp.start()
            started.append(cp)
        for i in range(n):
            theirs = outs[i].at[pl.ds((1 - c) * hrs[i], hrs[i]), :]
            remote(i, 4, theirs, theirs, sibling).wait_recv()
        for k, peer in enumerate(peers):
            pltpu.make_async_remote_copy(
                src_ref=pack_ref, dst_ref=packs_ref.at[4 * peer[0] + 2 * peer[1] + peer[2]],
                send_sem=send_sems.at[5 * n + k], recv_sem=recv_sems.at[5 * n + k],
                device_id=peer, device_id_type=MESH).wait_recv()
        for cp in started:
            cp.wait_send()

    vmem = pl.BlockSpec(memory_space=pltpu.VMEM)
    scratch = []
    scratch += [pltpu.VMEM((4, p.shape[1] // 2, p.shape[2]), F32) for p in pieces]
    scratch += [pltpu.VMEM((4, p.shape[1] // 2, p.shape[2]), F32) for p in pieces]
    scratch += [pltpu.VMEM((3, p.shape[1] // 2, p.shape[2]), BF16) for p in pieces]
    scratch += [pltpu.VMEM((3, p.shape[1] // 2, p.shape[2]), BF16) for p in pieces]
    scratch += [pltpu.SemaphoreType.DMA((5 * n + 7,)), pltpu.SemaphoreType.DMA((5 * n + 7,)), pltpu.SemaphoreType.DMA((n,))]
    return pl.pallas_call(
        body, name=name,
        out_shape=[jax.ShapeDtypeStruct((8,) + pack.shape, F32)] + [jax.ShapeDtypeStruct(p.shape[1:], F32) for p in pieces],
        in_specs=[vmem] + [pl.BlockSpec(memory_space=pl.ANY)] * n,
        out_specs=[vmem] * (1 + n),
        scratch_shapes=scratch,
        compiler_params=_params(vmem=VMEM_LIMIT),
    )(pack, *pieces)


def _rope_fwd(t, cos, sin, lane):
    lo = (lane % HEAD_DIM) < (HEAD_DIM // 2)
    return t * cos + jnp.where(lo, -pltpu.roll(t, 96, 1), pltpu.roll(t, 32, 1)) * sin


def _norm_proj(x, gmod, shift, w_int, pos, freq, bf_pad, tm):
    s = x.shape[0]

    def body(x_ref, g_ref, sh_ref, w_ref, pos_ref, fr_ref, bf_ref,
             a_ref, vb_ref, f_ref, gg_ref, ht_ref, cos_ref, sin_ref, q_ref, k_ref, v_ref, st_ref, carry):
        @pl.when(pl.program_id(0) == 0)
        def _():
            carry[...] = jnp.zeros_like(carry)

        xv = x_ref[...]
        r = lax.rsqrt(jnp.mean(xv * xv, axis=-1, keepdims=True) + NORM_EPS)
        hf = (xv * r) * g_ref[...] + sh_ref[...]
        hb = hf.astype(BF16)
        ht_ref[...] = hb.T
        pa = _dot(hb, w_ref[:, OFF_A:OFF_A + W_A])
        ang = pos_ref[...] * fr_ref[...]
        cosv, sinv = jnp.cos(ang), jnp.sin(ang)
        cos_ref[...] = cosv
        sin_ref[...] = sinv
        lane = lax.broadcasted_iota(jnp.int32, (tm, LANES), 1)
        for j in range(5):
            t = pa[:, LANES * j:LANES * (j + 1)]
            a_ref[:, LANES * j:LANES * (j + 1)] = _rope_fwd(t, cosv, sinv, lane).astype(BF16)
        a_ref[:, 640:768] = pa[:, 640:768].astype(BF16)
        pf = _dot(hb, w_ref[:, OFF_F:OFF_F + W_F])
        f_ref[...] = pf
        bblk = _dot(hb, w_ref[:, OFF_B:OFF_B + W_B]).astype(BF16)
        vb_ref[...] = bblk[:, 1024:1536]
        gg_ref[...] = _dot(hb, w_ref[:, OFF_G:OFF_G + W_G]).astype(BF16)
        _augment_heads(bblk, _cumsum_tile(pf, bf_ref[...], carry), q_ref, k_ref, v_ref, st_ref)

    row = lambda w: pl.BlockSpec((tm, w), lambda i: (i, 0))
    return pl.pallas_call(
        body, name="norm_proj", grid=(s // tm,),
        out_shape=[jax.ShapeDtypeStruct((s, W_A), BF16), jax.ShapeDtypeStruct((s, 512), BF16),
                   jax.ShapeDtypeStruct((s, W_F), F32), jax.ShapeDtypeStruct((s, W_G), BF16),
                   jax.ShapeDtypeStruct((D_MODEL, s), BF16),
                   jax.ShapeDtypeStruct((s, LANES), F32), jax.ShapeDtypeStruct((s, LANES), F32)]
        + [jax.ShapeDtypeStruct((s, 1024), BF16)] * 3 + [jax.ShapeDtypeStruct((s // tm, SUBLANES, LANES), F32)],
        in_specs=[row(D_MODEL), _const_spec((1, D_MODEL)), _const_spec((1, D_MODEL)), _const_spec((D_MODEL, W_INT)),
                  row(LANES), _const_spec((1, LANES)), _const_spec((1, LANES))],
        out_specs=[row(W_A), row(512), row(W_F), row(W_G), pl.BlockSpec((D_MODEL, tm), lambda i: (0, i)),
                   row(LANES), row(LANES), row(1024), row(1024), row(1024),
                   pl.BlockSpec((1, SUBLANES, LANES), lambda i: (i, 0, 0))],
        scratch_shapes=[pltpu.VMEM((SUBLANES, LANES), F32)],
        compiler_params=_params(("arbitrary",), VMEM_LIMIT),
    )(x, gmod, shift, w_int, pos, freq, bf_pad)


def _log_sigmoid(u):
    return jnp.minimum(u, 0.0) - jnp.log(1.0 + jnp.exp(-jnp.abs(u)))


def _cumsum_tile(f, b_f, carry):
    tb = f.shape[0]
    lane = lax.broadcasted_iota(jnp.int32, (tb, LANES), 1)
    logf = jnp.where(lane < N_HEADS, _log_sigmoid(f + b_f), 0.0)
    hi, mid, lo = _split3(logf)
    rows = lax.broadcasted_iota(jnp.int32, (tb, tb), 0)
    cols = lax.broadcasted_iota(jnp.int32, (tb, tb), 1)
    tril = (cols <= rows).astype(BF16)
    cum = _dot(tril, hi) + _dot(tril, mid) + _dot(tril, lo) + carry[0:1, :]
    carry[...] = jnp.broadcast_to(cum[tb - 1:tb, :], carry.shape)
    return cum


def _aug_lane(h):
    return 64 if h % 2 == 0 else 0


def _augment_heads(bblk, cumv, q_ref, k_ref, v_ref, st_ref):
    t = bblk.shape[0]
    lane = lax.broadcasted_iota(jnp.int32, (t, LANES), 1)
    lane_b = lane.astype(BF16)
    sub8 = lax.broadcasted_iota(jnp.int32, (SUBLANES, LANES), 0)
    lane8 = lax.broadcasted_iota(jnp.int32, (SUBLANES, LANES), 1)
    one = jnp.ones((t, LANES), BF16)
    zero = jnp.zeros((t, LANES), BF16)
    stats = jnp.zeros((SUBLANES, LANES), F32)
    for p in range(4):
        qblk = bblk[:, LANES * p:LANES * (p + 1)] * SCALE
        kblk = bblk[:, 512 + LANES * p:512 + LANES * (p + 1)]
        vblk = bblk[:, 1024 + LANES * p:1024 + LANES * (p + 1)]
        qf, kf = qblk.astype(F32), kblk.astype(F32)
        q2, k2, qk = qf * qf, kf * kf, qf * kf
        for odd in range(2):
            h = 2 * p + odd
            a0 = _aug_lane(h)
            data_b = (lane_b < 64) if odd == 0 else (lane_b >= 64)
            data = (lane < 64) if odd == 0 else (lane >= 64)
            hi, mid, lo = _split3(jnp.broadcast_to(cumv[:, h:h + 1], (t, LANES)))
            ones3_q = (lane_b >= a0 + 3) & (lane_b < a0 + 6)
            ones3_k = (lane_b >= a0) & (lane_b < a0 + 3)
            aug_q = jnp.where(lane_b == a0, hi, jnp.where(lane_b == a0 + 1, mid, jnp.where(
                lane_b == a0 + 2, lo, jnp.where(ones3_q, one, zero))))
            aug_k = jnp.where(ones3_k, one, jnp.where(lane_b == a0 + 3, -hi, jnp.where(
                lane_b == a0 + 4, -mid, jnp.where(lane_b == a0 + 5, -lo, zero))))
            q_ref[:, LANES * h:LANES * (h + 1)] = jnp.where(data_b, qblk, aug_q)
            k_ref[:, LANES * h:LANES * (h + 1)] = jnp.where(data_b, kblk, aug_k)
            v_ref[:, LANES * h:LANES * (h + 1)] = jnp.where(data_b, vblk, jnp.where(lane_b == a0, one, zero))
            qn = jnp.sqrt(jnp.max(jnp.sum(jnp.where(data, q2, 0.0), axis=-1, keepdims=True)))
            kn = jnp.sqrt(jnp.max(jnp.sum(jnp.where(data, k2, 0.0), axis=-1, keepdims=True)))
            dmin = jnp.min(jnp.sum(jnp.where(data, qk, 0.0), axis=-1, keepdims=True))
            c_first, c_last = cumv[0:1, h:h + 1], cumv[t - 1:t, h:h + 1]
            row = jnp.where(lane8 == 0, qn, jnp.where(lane8 == 1, kn, jnp.where(
                lane8 == 2, c_first, jnp.where(lane8 == 3, c_last, jnp.where(lane8 == 4, dmin, 0.0)))))
            stats = jnp.where(sub8 == h, row, stats)
    st_ref[0] = stats


PRUNE_MARGIN = 88.0


def _fox_tile_ranges(stats):
    nt = stats.shape[0]
    qn, kn, c_first, c_last, d_min = (stats[:, :, n] for n in range(5))
    bound = (1.01 * qn[:, None, :] * kn[None, :, :] - jnp.minimum(d_min, 0.0)[:, None, :] + 0.05
             + c_first[:, None, :] - c_last[None, :, :])
    idx = jnp.arange(nt)
    skip = (bound <= -PRUNE_MARGIN) & (idx[None, :, None] < idx[:, None, None])
    first_key = jnp.sum(jnp.cumprod(skip, axis=1), axis=1)
    needed = (idx[None, :, None] >= first_key[:, None, :]) & (idx[None, :, None] <= idx[:, None, None])
    last_query = jnp.max(jnp.where(needed, idx[:, None, None], 0), axis=0)
    n_query = last_query - idx[:, None] + 1
    table = jnp.zeros((4, SUBLANES, LANES), F32)
    for odd in range(2):
        table = table.at[:, odd, :nt].set(first_key[:, odd::2].T.astype(F32))
        table = table.at[:, 2 + odd, :nt].set(n_query[:, odd::2].T.astype(F32))
    return table


def _lane_scalar(block, row, lane_idx):
    sub8 = lax.broadcasted_iota(jnp.int32, (SUBLANES, LANES), 0)
    lane8 = lax.broadcasted_iota(jnp.int32, (SUBLANES, LANES), 1)
    return jnp.sum(jnp.where((sub8 == row) & (lane8 == lane_idx), block, 0.0)).astype(jnp.int32)


def _fox_fwd(qa, ka, va, ranges, t):
    s = qa.shape[0]
    nt = s // t
    nc = t // LANES

    def body(rg_ref, q_ref, k_ref, v_ref, o_ref, lse_ref):
        i = pl.program_id(1)
        lane = lax.broadcasted_iota(jnp.int32, (t, LANES), 1)
        rows = lax.broadcasted_iota(jnp.int32, (t, t), 0)
        cols = lax.broadcasted_iota(jnp.int32, (t, t), 1)
        firsts = [jnp.clip(_lane_scalar(rg_ref[0], hh, i), 0, i) for hh in range(2)]
        first = jnp.maximum(firsts[0], firsts[1])

        def update(js, carry, heads=(0, 1), diagonal=False):
            offs = [pl.multiple_of(j * t, t) for j in js]
            kts = [k_ref[pl.ds(off, t), :] for off in offs]
            vts = [v_ref[pl.ds(off, t), :] for off in offs]
            scs = {hh: [_dot(q_ref[:, LANES * hh:LANES * (hh + 1)], kt[:, LANES * hh:LANES * (hh + 1)], NT) for kt in kts]
                   for hh in heads}
            if diagonal:
                scs = {hh: [jnp.where(cols <= rows, sc, NEG) for sc in scs[hh]] for hh in heads}
            m_new = {}
            for hh in heads:
                part = None
                for sc in scs[hh]:
                    for cch in range(nc):
                        chunk = sc[:, LANES * cch:LANES * (cch + 1)]
                        part = chunk if part is None else jnp.maximum(part, chunk)
                m_new[hh] = jnp.maximum(carry[2 * hh], jnp.max(part, axis=-1, keepdims=True))
            alphas = {hh: jnp.exp(carry[2 * hh] - m_new[hh]) for hh in heads}
            ps = {hh: [jnp.exp(sc - m_new[hh]).astype(BF16) for sc in scs[hh]] for hh in heads}
            out = list(carry)
            for hh in heads:
                pv = None
                for p, vt in zip(ps[hh], vts):
                    term = _dot(p, vt[:, LANES * hh:LANES * (hh + 1)])
                    pv = term if pv is None else pv + term
                out[2 * hh], out[2 * hh + 1] = m_new[hh], alphas[hh] * carry[2 * hh + 1] + pv
            return tuple(out)

        col0 = jnp.full((t, 1), NEG, F32)
        zero = jnp.zeros((t, LANES), F32)
        carry = (col0, zero, col0, zero)
        for hh in range(2):
            carry = lax.fori_loop(firsts[hh], first, lambda j, cr, hh=hh: update([j], cr, heads=(hh,)), carry)
        n_off = i - first
        carry = lax.fori_loop(0, n_off // 2, lambda u, cr: update([first + 2 * u, first + 2 * u + 1], cr), carry)
        carry = lax.fori_loop(0, n_off % 2, lambda u, cr: update([i - 1], cr), carry)
        m0, acc0, m1, acc1 = update([i], carry, diagonal=True)
        l0, l1 = acc0[:, _aug_lane(0):_aug_lane(0) + 1], acc1[:, _aug_lane(1):_aug_lane(1) + 1]
        o_ref[...] = jnp.where(lane < 64, acc0 * (1.0 / l0), acc1 * (1.0 / l1)).astype(BF16)
        sub = lax.broadcasted_iota(jnp.int32, (SUBLANES, t), 0)
        lse0 = jnp.broadcast_to(m0 + jnp.log(l0), (t, LANES)).T[0:SUBLANES, :]
        lse1 = jnp.broadcast_to(m1 + jnp.log(l1), (t, LANES)).T[0:SUBLANES, :]
        lse_ref[0] = jnp.where(sub == 0, lse0, jnp.where(sub == 1, lse1, 0.0))

    pair = pl.BlockSpec((s, 2 * LANES), lambda p, i: (0, p))
    return pl.pallas_call(
        body, name="fox_fwd", grid=(4, nt),
        out_shape=[jax.ShapeDtypeStruct((s, 512), BF16), jax.ShapeDtypeStruct((4, SUBLANES, s), F32)],
        in_specs=[pl.BlockSpec((1, SUBLANES, LANES), lambda p, i: (p, 0, 0)),
                  pl.BlockSpec((t, 2 * LANES), lambda p, i: (i, p)), pair, pair],
        out_specs=[pl.BlockSpec((t, LANES), lambda p, i: (i, p)),
                   pl.BlockSpec((1, SUBLANES, t), lambda p, i: (p, 0, i))],
        compiler_params=_params(("parallel", "arbitrary"), VMEM_LIMIT),
    )(ranges, qa, ka, va)


def _dup_halves(blk, lane):
    f = blk.astype(F32)
    r = pltpu.roll(f, 64, 1)
    return jnp.where(lane < 64, f, r).astype(BF16), jnp.where(lane >= 64, f, r).astype(BF16)


GROUP = 4
GROUP_ROWS = GROUP * WINDOW


def _stack_heads(ref, g, lane):
    parts = []
    for pb in (2 * g, 2 * g + 1):
        blk = ref[:, LANES * pb:LANES * (pb + 1)]
        zero = jnp.zeros_like(blk)
        parts += [jnp.where(lane < 64, blk, zero), jnp.where(lane >= 64, blk, zero)]
    return jnp.concatenate(parts, axis=0)


def _swa_band(a_ref, ap_ref, g, lane):
    k = jnp.concatenate([_dup_halves(ap_ref[:, 512:640], lane)[g], _dup_halves(a_ref[:, 512:640], lane)[g]], axis=0)
    v = jnp.concatenate([_dup_halves(ap_ref[:, 640:768], lane)[g], _dup_halves(a_ref[:, 640:768], lane)[g]], axis=0)
    return k, v


def _swa_logits(q, k, has_prev):
    sc = _dot(q, k, NT) * SCALE
    rr = lax.broadcasted_iota(jnp.int32, sc.shape, 0) % WINDOW
    cc = lax.broadcasted_iota(jnp.int32, sc.shape, 1)
    valid = (cc > rr) & (cc <= rr + WINDOW) & (has_prev | (cc >= WINDOW))
    return jnp.where(valid, sc, NEG)


def _per_head_column(values):
    return jnp.concatenate([jnp.broadcast_to(v, (WINDOW, 1)) for v in values], axis=0)


SWA_BLOCKS = 4
SWA_ROWS = SWA_BLOCKS * WINDOW


def _swa_blocks(a_ref, ap_ref):
    return [ap_ref] + [a_ref.at[pl.ds(WINDOW * jb, WINDOW), :] for jb in range(SWA_BLOCKS)]


def _swa_fwd(a, sinks):
    s = a.shape[0]

    def body(sink_ref, a_ref, ap_ref, o_ref, l_ref):
        lane = lax.broadcasted_iota(jnp.int32, (WINDOW, LANES), 1)
        blocks = _swa_blocks(a_ref, ap_ref)
        units = [(jb, g) for jb in range(SWA_BLOCKS) for g in range(2)]
        sinks_col = [_per_head_column([sink_ref[GROUP * g + hh] for hh in range(GROUP)]) for g in range(2)]
        bands = [_swa_band(blocks[jb + 1], blocks[jb], g, lane) for jb, g in units]
        scs = [_swa_logits(_stack_heads(blocks[jb + 1], g, lane), bands[u][0],
                           (pl.program_id(0) > 0) if jb == 0 else True) for u, (jb, g) in enumerate(units)]
        ms = [jnp.maximum(jnp.max(scs[u], axis=-1, keepdims=True), sinks_col[g]) for u, (jb, g) in enumerate(units)]
        ps = [jnp.exp(scs[u] - ms[u]) for u in range(len(units))]
        dens = [jnp.sum(ps[u], axis=-1, keepdims=True) + jnp.exp(sinks_col[g] - ms[u]) for u, (jb, g) in enumerate(units)]
        outs = [_dot((ps[u] * (1.0 / dens[u])).astype(BF16), bands[u][1]) for u in range(len(units))]
        for jb in range(SWA_BLOCKS):
            rows = slice(WINDOW * jb, WINDOW * (jb + 1))
            l_all = jnp.zeros((WINDOW, LANES), F32)
            for g in range(2):
                u = 2 * jb + g
                lcol = ms[u] + jnp.log(dens[u])
                for pb in range(2):
                    r0 = 2 * pb * WINDOW
                    o_ref[rows, LANES * (2 * g + pb):LANES * (2 * g + pb + 1)] = jnp.where(
                        lane < 64, outs[u][r0:r0 + WINDOW], outs[u][r0 + WINDOW:r0 + 2 * WINDOW]).astype(BF16)
                for hh in range(GROUP):
                    l_all = jnp.where(lane == GROUP * g + hh, lcol[WINDOW * hh:WINDOW * (hh + 1)], l_all)
            l_ref[rows, :] = l_all

    return pl.pallas_call(
        body, name="swa_fwd", grid=(s // SWA_ROWS,),
        out_shape=[jax.ShapeDtypeStruct((s, 512), BF16), jax.ShapeDtypeStruct((s, LANES), F32)],
        in_specs=[pl.BlockSpec(memory_space=pltpu.SMEM),
                  pl.BlockSpec((SWA_ROWS, W_A), lambda i: (i, 0)),
                  pl.BlockSpec((WINDOW, W_A), lambda i: (jnp.maximum(SWA_BLOCKS * i - 1, 0), 0))],
        out_specs=[pl.BlockSpec((SWA_ROWS, 512), lambda i: (i, 0)), pl.BlockSpec((SWA_ROWS, LANES), lambda i: (i, 0))],
        compiler_params=_params(("parallel",)),
    )(sinks, a, a)


def _mid(att_a, att_b, g, x, target, gate, g_final, wo_a, wo_b, w_out, tm=256):
    s = x.shape[0]
    nt = s // tm

    def body(aa_ref, ab_ref, g_ref, x_ref, t_ref, gate_ref, gf_ref, woa_ref, wob_ref, wout_ref,
             dx_ref, daa_ref, dab_ref, dg_ref, delta_ref, dwoa_ref, dwob_ref, dwout_ref, vec_ref,
             acc_gf, acc_gate, acc_loss):
        step = pl.program_id(0)

        @pl.when(step == 0)
        def _():
            dwoa_ref[...] = jnp.zeros_like(dwoa_ref)
            dwob_ref[...] = jnp.zeros_like(dwob_ref)
            dwout_ref[...] = jnp.zeros_like(dwout_ref)
            acc_gf[...] = jnp.zeros_like(acc_gf)
            acc_gate[...] = jnp.zeros_like(acc_gate)
            acc_loss[...] = jnp.zeros_like(acc_loss)

        def fold(v):
            return jnp.sum(v.reshape(tm // SUBLANES, SUBLANES, D_MODEL), axis=0)

        gate = gate_ref[...]
        gfin = gf_ref[...]
        branches = []
        for att_ref, z_off, wo_ref in ((aa_ref, 0, woa_ref), (ab_ref, 512, wob_ref)):
            att = att_ref[...].astype(F32)
            z = g_ref[:, z_off:z_off + 512].astype(F32)
            sz = _sigmoid(z)
            silu = z * sz
            u = (att * silu).astype(BF16)
            branches.append((att, z, sz, silu, u, _dot(u, wo_ref[...])))
        ga = g_ref[:, 1024:2048].astype(F32)
        gb = g_ref[:, 2048:3072].astype(F32)
        sga, sgb = _sigmoid(ga), _sigmoid(gb)
        y_a, y_b = branches[0][5], branches[1][5]
        mb = (sga * y_a + sgb * y_b).astype(BF16)
        o = _dot(mb, wout_ref[...])
        x2 = x_ref[...] + gate * o
        r2 = lax.rsqrt(jnp.mean(x2 * x2, axis=-1, keepdims=True) + NORM_EPS)
        xn2 = x2 * r2
        err = xn2 * gfin - t_ref[...]
        acc_loss[...] += fold(err * err)
        dy = err * (1.0 / D_MODEL)
        acc_gf[...] += fold(dy * xn2)
        dxn = dy * gfin
        dx2 = r2 * (dxn - xn2 * jnp.mean(dxn * xn2, axis=-1, keepdims=True))
        dx_ref[...] = dx2
        acc_gate[...] += fold(dx2 * o)
        d_o = (dx2 * gate).astype(BF16)
        dwout_ref[...] += _dot(mb, d_o, TN)
        dm = _dot(d_o, wout_ref[...], NT)
        dg_ref[:, 1024:2048] = (dm * y_a * sga * (1.0 - sga)).astype(BF16)
        dg_ref[:, 2048:3072] = (dm * y_b * sgb * (1.0 - sgb)).astype(BF16)
        for (att, z, sz, silu, u, _), sg, wo_ref, dwo_ref, datt_ref, z_off in (
                (branches[0], sga, woa_ref, dwoa_ref, daa_ref, 0), (branches[1], sgb, wob_ref, dwob_ref, dab_ref, 512)):
            dyb = (dm * sg).astype(BF16)
            dwo_ref[...] += _dot(u, dyb, TN)
            du = _dot(dyb, wo_ref[...], NT)
            datt = du * silu
            datt_ref[...] = datt.astype(BF16)
            dg_ref[:, z_off:z_off + 512] = (du * att * (sz * (1.0 + z * (1.0 - sz)))).astype(BF16)
            if z_off == 512:
                prod = datt * att
                hi = prod.astype(BF16)
                lo = (prod - hi.astype(F32)).astype(BF16)
                er = lax.broadcasted_iota(jnp.int32, (512, LANES), 0)
                ec = lax.broadcasted_iota(jnp.int32, (512, LANES), 1)
                e = (er // HEAD_DIM == ec).astype(BF16)
                delta = _dot(hi, e) + _dot(lo, e)
                delta_ref[...] = delta.T[0:SUBLANES, :]

        @pl.when(step == nt - 1)
        def _():
            sub = lax.broadcasted_iota(jnp.int32, (SUBLANES, D_MODEL), 0)
            dgf = jnp.sum(acc_gf[...], axis=0, keepdims=True)
            dgate = jnp.sum(acc_gate[...], axis=0, keepdims=True)
            loss = 0.5 * jnp.sum(acc_loss[...]) * (1.0 / D_MODEL)
            vec_ref[...] = jnp.where(sub == 0, dgf, jnp.where(sub == 1, dgate, jnp.where(sub == 2, loss, 0.0)))

    row = lambda w: pl.BlockSpec((tm, w), lambda i: (i, 0))
    return pl.pallas_call(
        body, name="mid", grid=(nt,),
        out_shape=[jax.ShapeDtypeStruct((s, D_MODEL), F32), jax.ShapeDtypeStruct((s, 512), BF16),
                   jax.ShapeDtypeStruct((s, 512), BF16), jax.ShapeDtypeStruct((s, W_G), BF16),
                   jax.ShapeDtypeStruct((SUBLANES, s), F32),
                   jax.ShapeDtypeStruct((512, D_MODEL), F32), jax.ShapeDtypeStruct((512, D_MODEL), F32),
                   jax.ShapeDtypeStruct((D_MODEL, D_MODEL), F32), jax.ShapeDtypeStruct((SUBLANES, D_MODEL), F32)],
        in_specs=[row(512), row(512), row(W_G), row(D_MODEL), row(D_MODEL),
                  _const_spec((1, D_MODEL)), _const_spec((1, D_MODEL)),
                  _const_spec((512, D_MODEL)), _const_spec((512, D_MODEL)), _const_spec((D_MODEL, D_MODEL))],
        out_specs=[row(D_MODEL), row(512), row(512), row(W_G),
                   pl.BlockSpec((SUBLANES, tm), lambda i: (0, i)),
                   pl.BlockSpec((512, D_MODEL), lambda i: (0, 0)), pl.BlockSpec((512, D_MODEL), lambda i: (0, 0)),
                   pl.BlockSpec((D_MODEL, D_MODEL), lambda i: (0, 0)), pl.BlockSpec((SUBLANES, D_MODEL), lambda i: (0, 0))],
        scratch_shapes=[pltpu.VMEM((SUBLANES, D_MODEL), F32)] * 3,
        compiler_params=_params(("arbitrary",), VMEM_LIMIT),
    )(att_a, att_b, g, x, target, gate, g_final, wo_a, wo_b, w_out)


def _rope_bwd(dt, cos, sin, lane):
    u = dt * sin
    lo = (lane % HEAD_DIM) < (HEAD_DIM // 2)
    return dt * cos + jnp.where(lo, pltpu.roll(u, 96, 1), -pltpu.roll(u, 32, 1))


def _swa_bwd(a, datt, l_all, sinks, cos, sin):
    s = a.shape[0]
    nt = s // SWA_ROWS

    def body(sink_ref, a_ref, ap_ref, do_ref, l_ref, cos_ref, sin_ref, da_ref, ds_ref, halo):
        step = pl.program_id(0)
        tile = nt - 1 - step

        @pl.when(step == 0)
        def _():
            halo[...] = jnp.zeros_like(halo)
            ds_ref[...] = jnp.zeros_like(ds_ref)

        lane = lax.broadcasted_iota(jnp.int32, (WINDOW, LANES), 1)
        sub8 = lax.broadcasted_iota(jnp.int32, (SUBLANES, LANES), 0)
        lane8 = lax.broadcasted_iota(jnp.int32, (SUBLANES, LANES), 1)
        blocks = _swa_blocks(a_ref, ap_ref)
        dsink = jnp.zeros((SUBLANES, LANES), F32)

        def join(pair, r0):
            x0, x1 = pair[0][r0:r0 + WINDOW], pair[1][r0:r0 + WINDOW]
            return jnp.where(lane < 64, x0 + pltpu.roll(x0, 64, 1), x1 + pltpu.roll(x1, 64, 1))

        units = [(jb, g) for jb in range(SWA_BLOCKS) for g in range(2)]
        n_u = len(units)
        sinks_col = [_per_head_column([sink_ref[GROUP * g + hh] for hh in range(GROUP)]) for g in range(2)]
        bands = [_swa_band(blocks[jb + 1], blocks[jb], g, lane) for jb, g in units]
        qs = [_stack_heads(blocks[jb + 1], g, lane) for jb, g in units]
        doms = [_stack_heads(do_ref.at[pl.ds(WINDOW * jb, WINDOW), :], g, lane) for jb, g in units]
        lcols = []
        for jb, g in units:
            lv = l_ref[WINDOW * jb:WINDOW * (jb + 1), :]
            lcols.append(_per_head_column([lv[:, GROUP * g + hh:GROUP * g + hh + 1] for hh in range(GROUP)]))
        ps = [jnp.exp(_swa_logits(qs[u], bands[u][0], (tile > 0) if jb == 0 else True) - lcols[u])
              for u, (jb, g) in enumerate(units)]
        dps = [_dot(doms[u], bands[u][1], NT) for u in range(n_u)]
        deltas = [jnp.sum(ps[u] * dps[u], axis=-1, keepdims=True) for u in range(n_u)]
        for u, (jb, g) in enumerate(units):
            sink_term = jnp.exp(sinks_col[g] - lcols[u]) * deltas[u]
            for hh in range(GROUP):
                tot = jnp.sum(sink_term[WINDOW * hh:WINDOW * (hh + 1)])
                dsink = dsink + jnp.where((sub8 == 0) & (lane8 == GROUP * g + hh), -tot, 0.0)
        dss = [(ps[u] * (dps[u] - deltas[u])).astype(BF16) for u in range(n_u)]
        dqs = [_dot(dss[u], bands[u][0]) * SCALE for u in range(n_u)]
        dks = [_dot(dss[u], qs[u], TN) * SCALE for u in range(n_u)]
        dvs = [_dot(ps[u].astype(BF16), doms[u], TN) for u in range(n_u)]

        carry_k, carry_v = halo[:, 0:LANES], halo[:, LANES:2 * LANES]
        for jb in reversed(range(SWA_BLOCKS)):
            rows = slice(WINDOW * jb, WINDOW * (jb + 1))
            cosv, sinv = cos_ref[rows, :], sin_ref[rows, :]
            for g in range(2):
                dq = dqs[2 * jb + g]
                for pb in range(2):
                    r0 = 2 * pb * WINDOW
                    dq_pair = jnp.where(lane < 64, dq[r0:r0 + WINDOW], dq[r0 + WINDOW:r0 + 2 * WINDOW])
                    da_ref[rows, LANES * (2 * g + pb):LANES * (2 * g + pb + 1)] = _rope_bwd(
                        dq_pair, cosv, sinv, lane).astype(BF16)
            dkb, dvb = dks[2 * jb:2 * jb + 2], dvs[2 * jb:2 * jb + 2]
            da_ref[rows, 512:640] = _rope_bwd(join(dkb, WINDOW) + carry_k, cosv, sinv, lane).astype(BF16)
            da_ref[rows, 640:768] = (join(dvb, WINDOW) + carry_v).astype(BF16)
            carry_k, carry_v = join(dkb, 0), join(dvb, 0)
        halo[:, 0:LANES] = carry_k
        halo[:, LANES:2 * LANES] = carry_v
        ds_ref[...] += dsink

    rev = lambda w: pl.BlockSpec((SWA_ROWS, w), lambda i: (nt - 1 - i, 0))
    return pl.pallas_call(
        body, name="swa_bwd", grid=(nt,),
        out_shape=[jax.ShapeDtypeStruct((s, W_A), BF16), jax.ShapeDtypeStruct((SUBLANES, LANES), F32)],
        in_specs=[pl.BlockSpec(memory_space=pltpu.SMEM), rev(W_A),
                  pl.BlockSpec((WINDOW, W_A), lambda i: (jnp.maximum(SWA_BLOCKS * (nt - 1 - i) - 1, 0), 0)),
                  rev(512), rev(LANES), rev(LANES), rev(LANES)],
        out_specs=[rev(W_A), pl.BlockSpec((SUBLANES, LANES), lambda i: (0, 0))],
        scratch_shapes=[pltpu.VMEM((WINDOW, 2 * LANES), F32)],
        compiler_params=_params(("arbitrary",)),
    )(sinks, a, a, datt, l_all, cos, sin)


def _fox_bwd(qa, ka, vb, do, lse, delta, ranges, t):
    s = qa.shape[0]
    nt = s // t

    def body(rg_ref, q_ref, do_ref, lse_ref, dl_ref, k_ref, v_ref, dq_ref, dk_ref, dv_ref, dc_ref, dr_ref, dq_acc):
        p = pl.program_id(0)
        j = pl.program_id(1)
        n_queries = [jnp.clip(_lane_scalar(rg_ref[0], 2 + hh, j), 1, nt - j) for hh in range(2)]

        @pl.when(j == 0)
        def _():
            dq_acc[...] = jnp.zeros_like(dq_acc)

        lane = lax.broadcasted_iota(jnp.int32, (t, LANES), 1)
        rows = lax.broadcasted_iota(jnp.int32, (t, t), 0)
        cols = lax.broadcasted_iota(jnp.int32, (t, t), 1)
        kt = k_ref[...]
        vt = v_ref[...]

        ks = [kt[:, LANES * hh:LANES * (hh + 1)] for hh in range(2)]

        def tile(qis, carry, heads=(0, 1), diagonal=False):
            dk0, dk1, dv = carry
            offs = [pl.multiple_of(i * t, t) for i in qis]
            units = [(u, hh) for u in range(len(qis)) for hh in heads]
            qts = [q_ref[pl.ds(off, t), :] for off in offs]
            dos = [do_ref[pl.ds(off, t), :] for off in offs]
            lses = [lse_ref[0, :, pl.ds(off, t)] for off in offs]
            dls = [dl_ref[0, :, pl.ds(off, t)] for off in offs]
            qs = [qts[u][:, LANES * hh:LANES * (hh + 1)] for u, hh in units]
            doms = [jnp.where((lane < 64) if hh == 0 else (lane >= 64), dos[u], jnp.zeros_like(dos[u])) for u, hh in units]
            sts = [_dot(ks[hh], qs[n], NT) for n, (u, hh) in enumerate(units)]
            dpts = [_dot(vt, doms[n], NT) for n in range(len(units))]
            if diagonal:
                sts = [jnp.where(cols >= rows, st, NEG) for st in sts]
            pts = [jnp.exp(sts[n] - lses[u][hh:hh + 1, :]) for n, (u, hh) in enumerate(units)]
            dsts = [(pts[n] * (dpts[n] - dls[u][hh:hh + 1, :])).astype(BF16) for n, (u, hh) in enumerate(units)]
            for n, (u, hh) in enumerate(units):
                dv = dv + _dot(pts[n].astype(BF16), doms[n])
                term = _dot(dsts[n], qs[n])
                dk0, dk1 = (dk0 + term, dk1) if hh == 0 else (dk0, dk1 + term)
                dq_acc[hh, pl.ds(offs[u], t), :] += _dot(dsts[n], ks[hh], TN)
            return dk0, dk1, dv

        zero = jnp.zeros((t, LANES), F32)
        carry = tile([j], (zero, zero, zero), diagonal=True)
        n_rest = jnp.minimum(n_queries[0], n_queries[1]) - 1
        carry = lax.fori_loop(0, n_rest // 2, lambda u, cr: tile([j + 1 + 2 * u, j + 2 + 2 * u], cr), carry)
        carry = lax.fori_loop(0, n_rest % 2, lambda u, cr: tile([j + n_rest], cr), carry)
        for hh in range(2):
            carry = lax.fori_loop(j + 1 + n_rest, j + n_queries[hh], lambda i, cr, hh=hh: tile([i], cr, heads=(hh,)), carry)
        dk0, dk1, dv = carry
        e0, e1 = _aug_lane(0), _aug_lane(1)
        dk_ref[...] = jnp.where(lane < 64, dk0, dk1).astype(BF16)
        dv_ref[...] = dv.astype(BF16)
        c0 = jnp.broadcast_to(dk0[:, e0 + 3:e0 + 4], (t, LANES))
        c1 = jnp.broadcast_to(dk1[:, e1 + 3:e1 + 4], (t, LANES))
        dc_ref[0] = jnp.where(lane == 2 * p, -c0, jnp.where(lane == 2 * p + 1, -c1, 0.0))

        @pl.when(j == nt - 1)
        def _():
            lane_s = lax.broadcasted_iota(jnp.int32, (s, LANES), 1)
            a0, a1 = dq_acc[0], dq_acc[1]
            dq_ref[...] = (jnp.where(lane_s < 64, a0, a1) * SCALE).astype(BF16)
            r0 = jnp.broadcast_to(a0[:, e0:e0 + 1], (s, LANES))
            r1 = jnp.broadcast_to(a1[:, e1:e1 + 1], (s, LANES))
            dr_ref[0] = jnp.where(lane_s == 2 * p, r0, jnp.where(lane_s == 2 * p + 1, r1, 0.0))

    return pl.pallas_call(
        body, name="fox_bwd", grid=(4, nt),
        out_shape=[jax.ShapeDtypeStruct((s, 512), BF16), jax.ShapeDtypeStruct((s, 512), BF16),
                   jax.ShapeDtypeStruct((s, 512), BF16), jax.ShapeDtypeStruct((4, s, LANES), F32),
                   jax.ShapeDtypeStruct((4, s, LANES), F32)],
        in_specs=[pl.BlockSpec((1, SUBLANES, LANES), lambda p, j: (p, 0, 0)),
                  pl.BlockSpec((s, 2 * LANES), lambda p, j: (0, p)),
                  pl.BlockSpec((s, LANES), lambda p, j: (0, p)),
                  pl.BlockSpec((1, SUBLANES, s), lambda p, j: (p, 0, 0)),
                  pl.BlockSpec((1, SUBLANES, s), lambda p, j: (p, 0, 0)),
                  pl.BlockSpec((t, 2 * LANES), lambda p, j: (j, p)),
                  pl.BlockSpec((t, LANES), lambda p, j: (j, p))],
        out_specs=[pl.BlockSpec((s, LANES), lambda p, j: (0, p)),
                   pl.BlockSpec((t, LANES), lambda p, j: (j, p)),
                   pl.BlockSpec((t, LANES), lambda p, j: (j, p)),
                   pl.BlockSpec((1, t, LANES), lambda p, j: (p, j, 0)),
                   pl.BlockSpec((1, s, LANES), lambda p, j: (p, 0, 0))],
        scratch_shapes=[pltpu.VMEM((2, s, LANES), F32)],
        compiler_params=_params(("parallel", "arbitrary"), VMEM_LIMIT),
    )(ranges, qa, do, lse, delta, ka, vb)


def _forget_logit_grad(dc_ref, dr_ref, f, b_f, carry):
    tb = f.shape[0]
    lane = lax.broadcasted_iota(jnp.int32, (tb, LANES), 1)
    dc = dc_ref[0] + dr_ref[0]
    for k in range(1, 4):
        dc = dc + (dc_ref[k] + dr_ref[k])
    hi, mid, lo = _split3(dc)
    rows = lax.broadcasted_iota(jnp.int32, (tb, tb), 0)
    cols = lax.broadcasted_iota(jnp.int32, (tb, tb), 1)
    triu = (cols >= rows).astype(BF16)
    dlogf = _dot(triu, hi) + _dot(triu, mid) + _dot(triu, lo) + carry[0:1, :]
    carry[...] = jnp.broadcast_to(dlogf[0:1, :], carry.shape)
    return jnp.where(lane < N_HEADS, dlogf * _sigmoid(-(f + b_f)), 0.0)


def _dh_norm_bwd(d_a, d_q, d_k, d_v, dcum_k, dcum_q, f, bf_pad, d_g, w_t, x, dx2, gnorm, scale1, tm=512):
    s = x.shape[0]
    nt = s // tm

    def body(da_ref, dq_ref, dk_ref, dv_ref, dc_ref, dr_ref, f_ref, bf_ref, dg_ref, w_ref, x_ref, dx2_ref, g_ref, sc_ref,
             gx_ref, vec_ref, df_ref, db_ref, a_sh, a_sc, a_g, carry):
        step = pl.program_id(0)

        @pl.when(step == 0)
        def _():
            a_sh[...] = jnp.zeros_like(a_sh)
            a_sc[...] = jnp.zeros_like(a_sc)
            a_g[...] = jnp.zeros_like(a_g)
            carry[...] = jnp.zeros_like(carry)
            db_ref[...] = jnp.zeros_like(db_ref)

        def fold(v):
            return jnp.sum(v.reshape(tm // SUBLANES, SUBLANES, D_MODEL), axis=0)

        dfb = _forget_logit_grad(dc_ref, dr_ref, f_ref[...], bf_ref[...], carry)
        d_f = dfb.astype(BF16)
        df_ref[...] = d_f
        sub8 = lax.broadcasted_iota(jnp.int32, (SUBLANES, LANES), 0)
        db_ref[...] += jnp.where(sub8 == 0, jnp.sum(dfb, axis=0, keepdims=True), 0.0)
        d_all = jnp.concatenate([da_ref[...], dq_ref[...], dk_ref[...], dv_ref[...], d_f, dg_ref[...]], axis=1)
        dh = _dot(d_all, w_ref[...])
        xv = x_ref[...]
        r = lax.rsqrt(jnp.mean(xv * xv, axis=-1, keepdims=True) + NORM_EPS)
        xn = xv * r
        gn = g_ref[...]
        a_sh[...] += fold(dh)
        a_sc[...] += fold(dh * (xn * gn))
        dn1 = dh * sc_ref[...]
        a_g[...] += fold(dn1 * xn)
        dxn = dn1 * gn
        gx_ref[...] = dx2_ref[...] + r * (dxn - xn * jnp.mean(dxn * xn, axis=-1, keepdims=True))

        @pl.when(step == nt - 1)
        def _():
            sub = lax.broadcasted_iota(jnp.int32, (SUBLANES, D_MODEL), 0)
            v_sh = jnp.sum(a_sh[...], axis=0, keepdims=True)
            v_sc = jnp.sum(a_sc[...], axis=0, keepdims=True)
            v_g = jnp.sum(a_g[...], axis=0, keepdims=True)
            vec_ref[...] = jnp.where(sub == 0, v_sh, jnp.where(sub == 1, v_sc, jnp.where(sub == 2, v_g, 0.0)))

    row = lambda w: pl.BlockSpec((tm, w), lambda i: (nt - 1 - i, 0))
    slabs = pl.BlockSpec((4, tm, LANES), lambda i: (0, nt - 1 - i, 0))
    return pl.pallas_call(
        body, name="dh_norm_bwd", grid=(nt,),
        out_shape=[jax.ShapeDtypeStruct((s, D_MODEL), F32), jax.ShapeDtypeStruct((SUBLANES, D_MODEL), F32),
                   jax.ShapeDtypeStruct((s, LANES), BF16), jax.ShapeDtypeStruct((SUBLANES, LANES), F32)],
        in_specs=[row(W_A), row(512), row(512), row(512), slabs, slabs, row(W_F), _const_spec((1, LANES)), row(W_G),
                  _const_spec((W_INT, D_MODEL)), row(D_MODEL), row(D_MODEL), _const_spec((1, D_MODEL)),
                  _const_spec((1, D_MODEL))],
        out_specs=[row(D_MODEL), pl.BlockSpec((SUBLANES, D_MODEL), lambda i: (0, 0)), row(LANES),
                   pl.BlockSpec((SUBLANES, LANES), lambda i: (0, 0))],
        scratch_shapes=[pltpu.VMEM((SUBLANES, D_MODEL), F32)] * 3 + [pltpu.VMEM((SUBLANES, LANES), F32)],
        compiler_params=_params(("arbitrary",), VMEM_LIMIT),
    )(d_a, d_q, d_k, d_v, dcum_k, dcum_q, f, bf_pad, d_g, w_t, x, dx2, gnorm, scale1)


def _dw_in(h_t, d_a, d_q, d_k, d_v, d_f, d_g, ts=1024, tc=512):
    s = h_t.shape[1]
    ns = s // ts
    w_fg = 512 + W_F + W_G - 512
    rows = 128
    n_slot = (R_END // 4 + LANES - 1) // LANES * LANES
    order = [(0, 0, W_A), (4, 0, 512), (1, 0, 512), (2, 0, 512), (3, 0, 512), (4, 512, N_HEADS), (4, 512 + W_F, W_G - 512)]

    def slot_pieces(k):
        lo, hi, out, col = (R_END // 4) * k, (R_END // 4) * (k + 1), [], 0
        for acc_i, c0, w in order:
            a, b = max(lo, col), min(hi, col + w)
            if a < b:
                out.append((acc_i, c0 + a - col, b - a))
            col += w
        return out

    def body(h_ref, da_ref, dq_ref, dk_ref, dv_ref, df_ref, dg_ref, o_ref, acc_a, acc_q, acc_k, acc_v, acc_fg, stage, sem):
        k = pl.program_id(0)
        accs = (acc_a, acc_q, acc_k, acc_v, acc_fg)

        @pl.when(k == 0)
        def _():
            for acc in accs:
                acc[...] = jnp.zeros_like(acc)

        hv = h_ref[...]

        def add(acc, c_acc, d_ref, c_d, width):
            for c0 in range(0, width, tc):
                w = min(tc, width - c0)
                acc[:, c_acc + c0:c_acc + c0 + w] += _dot(hv, d_ref[:, c_d + c0:c_d + c0 + w])

        add(acc_a, 0, da_ref, 0, W_A)
        add(acc_q, 0, dq_ref, 0, 512)
        add(acc_k, 0, dk_ref, 0, 512)
        add(acc_v, 0, dv_ref, 0, 512)
        add(acc_fg, 0, dg_ref, 0, 512)
        add(acc_fg, 512, df_ref, 0, W_F)
        add(acc_fg, 512 + W_F, dg_ref, 512, W_G - 512)

        @pl.when(k == ns - 1)
        def _():
            pending = [None, None]
            for n, r0 in enumerate(range(0, D_MODEL, rows)):
                buf = n % 2
                if pending[buf] is not None:
                    for cp in pending[buf]:
                        cp.wait()
                rs = slice(r0, r0 + rows)
                copies = []
                for slot in range(4):
                    parts = [accs[acc_i][rs, c0:c0 + w] for acc_i, c0, w in slot_pieces(slot)]
                    parts.append(jnp.zeros((rows, n_slot - R_END // 4), F32))
                    stage[buf, slot] = jnp.concatenate(parts, axis=1)
                    cp = pltpu.make_async_copy(stage.at[buf, slot], o_ref.at[slot, pl.ds(r0, rows), :], sem.at[4 * buf + slot])
                    cp.start()
                    copies.append(cp)
                pending[buf] = copies
            for copies in pending:
                for cp in copies:
                    cp.wait()

    spec = lambda d: pl.BlockSpec((ts, d.shape[1]), lambda k: (k, 0))
    return pl.pallas_call(
        body, name="dw_in", grid=(ns,),
        out_shape=jax.ShapeDtypeStruct((4, D_MODEL, n_slot), F32),
        in_specs=[pl.BlockSpec((D_MODEL, ts), lambda k: (0, k))] + [spec(d) for d in (d_a, d_q, d_k, d_v, d_f, d_g)],
        out_specs=pl.BlockSpec(memory_space=pl.ANY),
        scratch_shapes=[pltpu.VMEM((D_MODEL, W_A), F32)] + [pltpu.VMEM((D_MODEL, 512), F32)] * 3
        + [pltpu.VMEM((D_MODEL, w_fg), F32), pltpu.VMEM((2, 4, rows, n_slot), F32), pltpu.SemaphoreType.DMA((8,))],
        compiler_params=_params(("arbitrary",), VMEM_LIMIT),
    )(h_t, d_a, d_q, d_k, d_v, d_f, d_g)


def _small_grads(packs, c_t, dada_shard):
    def body(p_ref, ct_ref, da_ref, sum_ref, gw_ref):
        acc = p_ref[0]
        for dev in range(1, 8):
            acc = acc + p_ref[dev]
        sum_ref[...] = acc
        gw_ref[...] = jnp.dot(ct_ref[...], da_ref[...], preferred_element_type=F32, precision=lax.Precision.HIGHEST)

    return pl.pallas_call(
        body, name="small_grads",
        out_shape=[jax.ShapeDtypeStruct(packs.shape[1:], F32),
                   jax.ShapeDtypeStruct((c_t.shape[0], dada_shard.shape[1]), F32)],
    )(packs, c_t, dada_shard)


def _adamw_body(w_ref, g_ref, m_ref, v_ref, d_ref, mo_ref, vo_ref):
    c1 = 1.0 / (1.0 - ADAM_B1 ** ADAM_STEP)
    c2 = 1.0 / (1.0 - ADAM_B2 ** ADAM_STEP)
    gv = g_ref[...]
    mn = ADAM_B1 * m_ref[...] + (1.0 - ADAM_B1) * gv
    vn = ADAM_B2 * v_ref[...] + (1.0 - ADAM_B2) * (gv * gv)
    mo_ref[...] = mn
    vo_ref[...] = vn
    d_ref[...] = -ADAM_LR * ((mn * c1) / (jnp.sqrt(vn * c2) + ADAM_EPS) + ADAM_WD * w_ref[...])


def _adamw3(w, g, m, v, name, tb=128):
    spec = pl.BlockSpec((tb, SUBLANES, LANES), lambda i: (i, 0, 0))
    return pl.pallas_call(
        functools.partial(_adamw_body), name=name, grid=(pl.cdiv(w.shape[0], tb),),
        out_shape=[jax.ShapeDtypeStruct(w.shape, F32)] * 3,
        in_specs=[spec] * 4, out_specs=[spec] * 3,
        compiler_params=_params(("parallel",)),
    )(w, g, m, v)


def _adamw(w, g, m, v, name):
    r, c = w.shape
    tr = 128 if r % 128 == 0 else r
    body = functools.partial(_adamw_body)
    spec = pl.BlockSpec((tr, c), lambda i: (i, 0))
    return pl.pallas_call(
        body, name=name, grid=(r // tr,),
        out_shape=[jax.ShapeDtypeStruct((r, c), F32)] * 3,
        in_specs=[spec] * 4, out_specs=[spec] * 3,
        compiler_params=_params(("parallel",)),
    )(w, g, m, v)


def _rope_inputs(positions):
    inv_freq = 10000.0 ** (-jnp.arange(0, HEAD_DIM, 2, dtype=F32) / HEAD_DIM)
    pos = jnp.broadcast_to(positions.astype(F32)[:, None], (positions.shape[0], LANES))
    return pos, jnp.tile(inv_freq, 4)[None, :]


def _pad_rows(v, rows=SUBLANES):
    return jnp.pad(v, ((0, rows - v.shape[0]), (0, 0)))


def kernel(x, c, positions, w_ada, b_ada, g_norm, w_in, b_f, sinks, w_o_swa, w_o_fox, w_out, g_final, loss_target, m_w_ada, m_b_ada, m_g_norm, m_w_in, m_b_f, m_sinks, m_w_o_swa, m_w_o_fox, m_w_out, m_g_final, v_w_ada, v_b_ada, v_g_norm, v_w_in, v_b_f, v_sinks, v_w_o_swa, v_w_o_fox, v_w_out, v_g_final):
    ix, iy, ic = lax.axis_index("x"), lax.axis_index("y"), lax.axis_index("c")
    chip = 2 * ix + iy
    dev = 2 * chip + ic
    xs, tgt = x[0], loss_target[0]
    s = xs.shape[0]

    b_ada_shard = lax.dynamic_slice(b_ada, (0, chip * 768), (1, 768))
    ada_parts, g_in, g_oa, g_ob, g_out = _gather_inputs(
        _pad_rows(c), w_ada[0], b_ada_shard, [w_in[0], w_o_swa[0], w_o_fox[0], w_out[0]], "gather_inputs")
    ada = lax.dynamic_index_in_dim(ada_parts, dev, axis=1, keepdims=False).reshape(1, 3 * D_MODEL)
    shift, scale, gate = ada[:, :D_MODEL], ada[:, D_MODEL:2 * D_MODEL], ada[:, 2 * D_MODEL:]
    scale1 = 1.0 + scale

    w_ref_order = jnp.transpose(g_in, (1, 0, 2)).reshape(D_MODEL, R_END)
    w_int = jnp.concatenate([
        w_ref_order[:, :R_ZA], w_ref_order[:, R_QB:R_FB], w_ref_order[:, R_FB:R_ZB],
        jnp.zeros((D_MODEL, W_F - N_HEADS), BF16), w_ref_order[:, R_ZA:R_QB], w_ref_order[:, R_ZB:]], axis=1)
    w_int_t = w_int.T
    wo_a = jnp.transpose(g_oa, (1, 0, 2)).reshape(512, D_MODEL)
    wo_b = jnp.transpose(g_ob, (1, 0, 2)).reshape(512, D_MODEL)
    w_o = g_out.reshape(D_MODEL, D_MODEL)

    pos, freq = _rope_inputs(positions[0])
    bf_pad = jnp.pad(b_f, ((0, 0), (0, LANES - N_HEADS)))
    sink_vec = sinks[0]

    a, vb, f, g, h_t, cos, sin, qa, ka, va, stats = _norm_proj(
        xs, g_norm * scale1, shift, w_int, pos, freq, bf_pad, FOX_TILE)
    att_a, l_swa = _swa_fwd(a, sink_vec)
    ranges = _fox_tile_ranges(stats)
    att_b, lse = _fox_fwd(qa, ka, va, ranges, FOX_TILE)

    dx2, datt_a, datt_b, d_g, delta8, dwo_a, dwo_b, dw_out, vec_mid = _mid(
        att_a, att_b, g, xs, tgt, gate, g_final.reshape(1, D_MODEL), wo_a, wo_b, w_o)
    delta = jnp.pad(delta8.reshape(4, 2, s), ((0, 0), (0, SUBLANES - 2), (0, 0)))
    d_a, dsink = _swa_bwd(a, datt_a, l_swa, sink_vec, cos, sin)
    dq, dk, dv, dcum_k, dcum_q = _fox_bwd(qa, ka, vb, datt_b, lse, delta, ranges, FOX_TILE)
    grad_x, vec_dh, d_f, dbf = _dh_norm_bwd(
        d_a, dq, dk, dv, dcum_k, dcum_q, f, bf_pad, d_g, w_int_t, xs, dx2, g_norm, scale1)
    dw_in_slots = _dw_in(h_t, d_a, dq, dk, dv, d_f, d_g)

    tail = jnp.pad(jnp.concatenate([dbf[0:1, :N_HEADS], dsink[0:1, :N_HEADS]], axis=1), ((0, 0), (0, D_MODEL - 2 * N_HEADS)))
    pack = jnp.concatenate([c, vec_dh[0:2], vec_mid[1:2], vec_dh[2:3], vec_mid[0:1], tail, vec_mid[2:3]], axis=0)

    def slots(w, axis):
        if axis == 1:
            return jnp.transpose(w.reshape(w.shape[0], 4, w.shape[1] // 4), (1, 0, 2))
        return w.reshape(4, w.shape[0] // 4, w.shape[1])

    packs, g_wo_a, g_wo_b, g_w_out, g_w_in = _reduce_scatter(
        [slots(dwo_a, 1), slots(dwo_b, 1), slots(dw_out, 0), dw_in_slots], pack, "reduce_grads")
    g_w_in = g_w_in[:, :w_in.shape[2]]
    dada_all = packs[:, 1:4, :].reshape(8, 3 * D_MODEL)
    dada_shard = lax.dynamic_slice(dada_all, (0, chip * 768), (8, 768))
    sums, g_w_ada = _small_grads(packs, packs[:, 0, :].T, dada_shard)
    g_b_ada = sums[1:4].reshape(1, 3 * D_MODEL)
    g_g_norm = sums[4:5]
    g_g_final = sums[5]
    g_b_f = sums[6:7, :N_HEADS]
    g_sinks = sums[6:7, N_HEADS:2 * N_HEADS]
    loss = sums[7, 0]

    grads = {
        "w_ada": g_w_ada, "b_ada": g_b_ada, "g_norm": g_g_norm, "w_in": g_w_in, "b_f": g_b_f, "sinks": g_sinks,
        "w_o_swa": g_wo_a, "w_o_fox": g_wo_b, "w_out": g_w_out, "g_final": g_g_final,
    }
    params = {
        "w_ada": (w_ada, m_w_ada, v_w_ada), "b_ada": (b_ada, m_b_ada, v_b_ada), "g_norm": (g_norm, m_g_norm, v_g_norm),
        "w_in": (w_in, m_w_in, v_w_in), "b_f": (b_f, m_b_f, v_b_f), "sinks": (sinks, m_sinks, v_sinks),
        "w_o_swa": (w_o_swa, m_w_o_swa, v_w_o_swa), "w_o_fox": (w_o_fox, m_w_o_fox, v_w_o_fox),
        "w_out": (w_out, m_w_out, v_w_out), "g_final": (g_final, m_g_final, v_g_final),
    }
    n_col = w_in.shape[2]

    def as_stored(t):
        return jnp.transpose(t, (2, 0, 1)).reshape(n_col, SUBLANES, LANES)

    def from_stored(t):
        return jnp.transpose(t, (1, 2, 0)).reshape(1, D_MODEL, n_col)

    names = list(grads)
    out_g, out_d, out_m, out_v = [], [], [], []
    for nm in names:
        w, m, v = params[nm]
        if nm == "w_in":
            g_st = as_stored(grads[nm][None])
            d_, m_, v_ = _adamw3(as_stored(w), g_st, as_stored(m), as_stored(v), "adamw_" + nm)
            res = [from_stored(t) for t in (g_st, d_, m_, v_)]
        else:
            shape2 = (w.shape[-2], w.shape[-1]) if w.ndim >= 2 else (1, w.shape[0])
            d_, m_, v_ = _adamw(w.reshape(shape2), grads[nm].reshape(shape2), m.reshape(shape2), v.reshape(shape2), "adamw_" + nm)
            res = [t.reshape(w.shape) for t in (grads[nm], d_, m_, v_)]
        out_g.append(res[0])
        out_d.append(res[1])
        out_m.append(res[2])
        out_v.append(res[3])
    return (loss, grad_x[None], *out_g, *out_d, *out_m, *out_v)
```

```python
import functools

import numpy as np
import jax
import jax.numpy as jnp
from jax import lax
from jax.experimental import pallas as pl
from jax.experimental.pallas import tpu as pltpu

F32 = jnp.float32
BF16 = jnp.bfloat16
MESH = pl.DeviceIdType.MESH

D_MODEL = 1024
HEAD_DIM = 64
N_HEADS = 8
WINDOW = 128
NORM_EPS = 1e-6
SCALE = HEAD_DIM ** -0.5
NEG = -1e30
LANES = 128
SUBLANES = 8
VMEM_LIMIT = 60 * 1024 * 1024
FOX_TILE = 512

W_A, W_B, W_F, W_G = 768, 1536, 128, 3072
OFF_A, OFF_B, OFF_F, OFF_G = 0, 768, 2304, 2432
W_INT = W_A + W_B + W_F + W_G
R_ZA, R_QB, R_FB, R_ZB, R_END = 768, 1280, 2816, 2824, 5384

ADAM_LR, ADAM_B1, ADAM_B2, ADAM_EPS, ADAM_WD, ADAM_STEP = 0.001, 0.9, 0.999, 1e-08, 0.01, 10

NT = (((1,), (1,)), ((), ()))
TN = (((0,), (0,)), ((), ()))


def _dot(a, b, dims=None):
    if dims is None:
        return jnp.dot(a, b, preferred_element_type=F32)
    return lax.dot_general(a, b, dims, preferred_element_type=F32)


def _split3(v):
    hi = v.astype(BF16)
    r1 = v - hi.astype(F32)
    mid = r1.astype(BF16)
    lo = (r1 - mid.astype(F32)).astype(BF16)
    return hi, mid, lo


def _sigmoid(v):
    return 1.0 / (1.0 + jnp.exp(-v))


def _params(sem=None, vmem=None):
    return pltpu.CompilerParams(dimension_semantics=sem, vmem_limit_bytes=vmem)


def _const_spec(shape):
    nd = len(shape)
    return pl.BlockSpec(shape, lambda *_: (0,) * nd, pipeline_mode=pl.Buffered(1))


def _flip(v, f):
    return 1 - v if f else v


_CHIP_FLIPS = ((1, 0), (0, 1), (1, 1))


def _gather_inputs(c_pad, w_ada, b_ada_shard, w_in_shard, small_shards, name):
    shards = [w_in_shard] + list(small_shards)
    n = len(shards)
    n_col = w_ada.shape[1]
    shard_w = w_in_shard.shape[1]
    rows = 128

    def body(*refs):
        c_ref, wa_ref, ba_ref = refs[:3]
        ins = refs[3:3 + n]
        ada_ref, wint_ref, wintt_ref = refs[3 + n:6 + n]
        g_in, call_ref, send_sems, recv_sems = refs[5 + 2 * n:9 + 2 * n]
        outs = (g_in,) + tuple(refs[6 + n:5 + 2 * n])
        x, y, c = lax.axis_index("x"), lax.axis_index("y"), lax.axis_index("c")
        k_me = 2 * x + y
        me = 2 * k_me + c
        sibling = (x, y, 1 - c)
        chips = [(_flip(x, fx), _flip(y, fy)) for fx, fy in _CHIP_FLIPS]

        def piece(i, chip_k, half):
            hr = ins[i].shape[0] // 2
            return outs[i].at[chip_k, pl.ds(half * hr, hr), :]

        def copy(i, slot, chip_k, half, to):
            return pltpu.make_async_remote_copy(
                src_ref=piece(i, chip_k, half), dst_ref=piece(i, chip_k, half),
                send_sem=send_sems.at[6 * i + slot], recv_sem=recv_sems.at[6 * i + slot],
                device_id=to, device_id_type=MESH)

        def small(ref, slot, sem, to):
            return pltpu.make_async_remote_copy(
                src_ref=ref.at[slot], dst_ref=ref.at[slot], send_sem=send_sems.at[6 * n + sem],
                recv_sem=recv_sems.at[6 * n + sem], device_id=to, device_id_type=MESH)

        for i in range(n):
            outs[i][k_me] = ins[i][...].astype(BF16)
        started = []
        for i in range(n):
            for j, chip in enumerate(chips):
                cp = copy(i, j, k_me, c, (chip[0], chip[1], c))
                cp.start()
                started.append(cp)

        call_ref[me] = c_ref[...]
        peers = [(_flip(x, k & 4), _flip(y, k & 2), _flip(c, k & 1)) for k in range(1, 8)]
        for k, peer in enumerate(peers):
            cp = small(call_ref, me, k, peer)
            cp.start()
            started.append(cp)
        for k, peer in enumerate(peers):
            small(call_ref, 4 * peer[0] + 2 * peer[1] + peer[2], k, peer).wait_recv()
        c_all = call_ref[:, 0, :].astype(BF16)
        ada_ref[k_me] = _dot(c_all, wa_ref[...].astype(BF16)) + ba_ref[...]
        for j, chip in enumerate(chips):
            cp = small(ada_ref, k_me, 7 + j, (chip[0], chip[1], c))
            cp.start()
            started.append(cp)

        for j, chip in enumerate(chips):
            chip_k = 2 * chip[0] + chip[1]
            for i in range(n):
                copy(i, j, chip_k, c, (chip[0], chip[1], c)).wait_recv()
                cp = copy(i, 3 + j, chip_k, c, sibling)
                cp.start()
                started.append(cp)
        for j, chip in enumerate(chips):
            chip_k = 2 * chip[0] + chip[1]
            small(ada_ref, chip_k, 7 + j, (chip[0], chip[1], c)).wait_recv()
            for i in range(n):
                copy(i, 3 + j, chip_k, 1 - c, sibling).wait_recv()
        for cp in started:
            cp.wait_send()

        def ref_cols(slots, a, b):
            runs = []
            for k in range(4):
                lo, hi = max(a, shard_w * k), min(b, shard_w * (k + 1))
                if lo < hi:
                    runs.append(slots[k][:, lo - shard_w * k:hi - shard_w * k])
            return runs

        for r0 in range(0, D_MODEL, rows):
            rs = slice(r0, r0 + rows)
            slots = [g_in[k, rs, :] for k in range(4)]
            row = jnp.concatenate(
                ref_cols(slots, 0, R_ZA) + ref_cols(slots, R_QB, R_FB) + ref_cols(slots, R_FB, R_ZB)
                + [jnp.zeros((rows, W_F - N_HEADS), BF16)] + ref_cols(slots, R_ZA, R_QB) + ref_cols(slots, R_ZB, R_END),
                axis=1)
            wint_ref[rs, :] = row
            wintt_ref[:, rs] = row.T

    vmem = pl.BlockSpec(memory_space=pltpu.VMEM)
    return pl.pallas_call(
        body, name=name,
        out_shape=[jax.ShapeDtypeStruct((4, 8, n_col), F32), jax.ShapeDtypeStruct((D_MODEL, W_INT), BF16),
                   jax.ShapeDtypeStruct((W_INT, D_MODEL), BF16)]
        + [jax.ShapeDtypeStruct((4,) + s.shape, BF16) for s in small_shards],
        in_specs=[vmem] * (3 + n),
        out_specs=[vmem] * (2 + n),
        scratch_shapes=[pltpu.VMEM((4,) + w_in_shard.shape, BF16), pltpu.VMEM((8,) + c_pad.shape, F32),
                        pltpu.SemaphoreType.DMA((6 * n + 10,)), pltpu.SemaphoreType.DMA((6 * n + 10,))],
        compiler_params=_params(vmem=VMEM_LIMIT),
    )(c_pad, w_ada, b_ada_shard, *shards)


def _reduce_scatter(pieces, pack, name):
    n = len(pieces)

    def body(*refs):
        pack_ref, ins = refs[0], refs[1:1 + n]
        packs_ref, outs = refs[1 + n], refs[2 + n:2 + 2 * n]
        rest = refs[2 + 2 * n:]
        own, got = rest[:n], rest[n:2 * n]
        sendb, recvb = rest[2 * n:3 * n], rest[3 * n:4 * n]
        send_sems, recv_sems, local_sems = rest[4 * n:4 * n + 3]
        x, y, c = lax.axis_index("x"), lax.axis_index("y"), lax.axis_index("c")
        k_me = 2 * x + y
        me = 2 * k_me + c
        sibling = (x, y, 1 - c)
        chips = [(_flip(x, fx), _flip(y, fy)) for fx, fy in _CHIP_FLIPS]
        hrs = [p.shape[1] // 2 for p in pieces]

        def remote(i, slot, src, dst, to):
            return pltpu.make_async_remote_copy(
                src_ref=src, dst_ref=dst, send_sem=send_sems.at[5 * i + slot], recv_sem=recv_sems.at[5 * i + slot],
                device_id=to, device_id_type=MESH)

        started = []
        packs_ref[me] = pack_ref[...]
        peers = [(_flip(x, k & 4), _flip(y, k & 2), _flip(c, k & 1)) for k in range(1, 8)]
        for k, peer in enumerate(peers):
            cp = pltpu.make_async_remote_copy(
                src_ref=pack_ref, dst_ref=packs_ref.at[me], send_sem=send_sems.at[5 * n + k],
                recv_sem=recv_sems.at[5 * n + k], device_id=peer, device_id_type=MESH)
            cp.start()
            started.append(cp)
        loads = []
        for i in range(n):
            ld = pltpu.make_async_copy(ins[i].at[:, pl.ds(c * hrs[i], hrs[i]), :], own[i], local_sems.at[i])
            ld.start()
            loads.append(ld)
            cp = remote(i, 0, ins[i].at[:, pl.ds((1 - c) * hrs[i], hrs[i]), :], got[i], sibling)
            cp.start()
            started.append(cp)
        for i in range(n):
            loads[i].wait()
            remote(i, 0, ins[i].at[:, pl.ds(c * hrs[i], hrs[i]), :], got[i], sibling).wait_recv()
            for j, chip in enumerate(chips):
                chip_k = 2 * chip[0] + chip[1]
                sendb[i][j] = (own[i][chip_k] + got[i][chip_k]).astype(BF16)
                cp = remote(i, 1 + j, sendb[i].at[j], recvb[i].at[j], (chip[0], chip[1], c))
                cp.start()
                started.append(cp)
        for i in range(n):
            acc = own[i][k_me] + got[i][k_me]
            for j, chip in enumerate(chips):
                remote(i, 1 + j, sendb[i].at[j], recvb[i].at[j], (chip[0], chip[1], c)).wait_recv()
                acc = acc + recvb[i][j].astype(F32)
            mine = outs[i].at[pl.ds(c * hrs[i], hrs[i]), :]
            outs[i][pl.ds(pl.multiple_of(c * hrs[i], SUBLANES), hrs[i]), :] = acc
            cp = remote(i, 4, mine, mine, sibling)
            cp.start()
            started.append(cp)
        for i in range(n):
            theirs = outs[i].at[pl.ds((1 - c) * hrs[i], hrs[i]), :]
            remote(i, 4, theirs, theirs, sibling).wait_recv()
        for k, peer in enumerate(peers):
            pltpu.make_async_remote_copy(
                src_ref=pack_ref, dst_ref=packs_ref.at[4 * peer[0] + 2 * peer[1] + peer[2]],
                send_sem=send_sems.at[5 * n + k], recv_sem=recv_sems.at[5 * n + k],
                device_id=peer, device_id_type=MESH).wait_recv()
        for cp in started:
            cp.wait_send()

    vmem = pl.BlockSpec(memory_space=pltpu.VMEM)
    scratch = []
    scratch += [pltpu.VMEM((4, p.shape[1] // 2, p.shape[2]), F32) for p in pieces]
    scratch += [pltpu.VMEM((4, p.shape[1] // 2, p.shape[2]), F32) for p in pieces]
    scratch += [pltpu.VMEM((3, p.shape[1] // 2, p.shape[2]), BF16) for p in pieces]
    scratch += [pltpu.VMEM((3, p.shape[1] // 2, p.shape[2]), BF16) for p in pieces]
    scratch += [pltpu.SemaphoreType.DMA((5 * n + 7,)), pltpu.SemaphoreType.DMA((5 * n + 7,)), pltpu.SemaphoreType.DMA((n,))]
    return pl.pallas_call(
        body, name=name,
        out_shape=[jax.ShapeDtypeStruct((8,) + pack.shape, F32)] + [jax.ShapeDtypeStruct(p.shape[1:], F32) for p in pieces],
        in_specs=[vmem] + [pl.BlockSpec(memory_space=pl.ANY)] * n,
        out_specs=[vmem] * (1 + n),
        scratch_shapes=scratch,
        compiler_params=_params(vmem=VMEM_LIMIT),
    )(pack, *pieces)


def _rope_fwd(t, cos, sin, lane):
    lo = (lane % HEAD_DIM) < (HEAD_DIM // 2)
    return t * cos + jnp.where(lo, -pltpu.roll(t, 96, 1), pltpu.roll(t, 32, 1)) * sin


def _norm_proj(x, gmod, shift, w_int, pos, freq, bf_pad, tm):
    s = x.shape[0]

    def body(x_ref, g_ref, sh_ref, w_ref, pos_ref, fr_ref, bf_ref,
             a_ref, vb_ref, f_ref, gg_ref, ht_ref, cos_ref, sin_ref, q_ref, k_ref, v_ref, st_ref, carry):
        @pl.when(pl.program_id(0) == 0)
        def _():
            carry[...] = jnp.zeros_like(carry)

        xv = x_ref[...]
        r = lax.rsqrt(jnp.mean(xv * xv, axis=-1, keepdims=True) + NORM_EPS)
        hf = (xv * r) * g_ref[...] + sh_ref[...]
        hb = hf.astype(BF16)
        ht_ref[...] = hb.T
        pa = _dot(hb, w_ref[:, OFF_A:OFF_A + W_A])
        ang = pos_ref[...] * fr_ref[...]
        cosv, sinv = jnp.cos(ang), jnp.sin(ang)
        cos_ref[...] = cosv
        sin_ref[...] = sinv
        lane = lax.broadcasted_iota(jnp.int32, (tm, LANES), 1)
        for j in range(5):
            t = pa[:, LANES * j:LANES * (j + 1)]
            a_ref[:, LANES * j:LANES * (j + 1)] = _rope_fwd(t, cosv, sinv, lane).astype(BF16)
        a_ref[:, 640:768] = pa[:, 640:768].astype(BF16)
        pf = _dot(hb, w_ref[:, OFF_F:OFF_F + W_F])
        f_ref[...] = pf
        bblk = _dot(hb, w_ref[:, OFF_B:OFF_B + W_B]).astype(BF16)
        vb_ref[...] = bblk[:, 1024:1536]
        gg_ref[...] = _dot(hb, w_ref[:, OFF_G:OFF_G + W_G]).astype(BF16)
        _augment_heads(bblk, _cumsum_tile(pf, bf_ref[...], carry), q_ref, k_ref, v_ref, st_ref)

    row = lambda w: pl.BlockSpec((tm, w), lambda i: (i, 0))
    return pl.pallas_call(
        body, name="norm_proj", grid=(s // tm,),
        out_shape=[jax.ShapeDtypeStruct((s, W_A), BF16), jax.ShapeDtypeStruct((s, 512), BF16),
                   jax.ShapeDtypeStruct((s, W_F), F32), jax.ShapeDtypeStruct((s, W_G), BF16),
                   jax.ShapeDtypeStruct((D_MODEL, s), BF16),
                   jax.ShapeDtypeStruct((s, LANES), F32), jax.ShapeDtypeStruct((s, LANES), F32)]
        + [jax.ShapeDtypeStruct((s, 1024), BF16)] * 3 + [jax.ShapeDtypeStruct((s // tm, SUBLANES, LANES), F32)],
        in_specs=[row(D_MODEL), _const_spec((1, D_MODEL)), _const_spec((1, D_MODEL)), _const_spec((D_MODEL, W_INT)),
                  row(LANES), _const_spec((1, LANES)), _const_spec((1, LANES))],
        out_specs=[row(W_A), row(512), row(W_F), row(W_G), pl.BlockSpec((D_MODEL, tm), lambda i: (0, i)),
                   row(LANES), row(LANES), row(1024), row(1024), row(1024),
                   pl.BlockSpec((1, SUBLANES, LANES), lambda i: (i, 0, 0))],
        scratch_shapes=[pltpu.VMEM((SUBLANES, LANES), F32)],
        compiler_params=_params(("arbitrary",), VMEM_LIMIT),
    )(x, gmod, shift, w_int, pos, freq, bf_pad)


def _log_sigmoid(u):
    return jnp.minimum(u, 0.0) - jnp.log(1.0 + jnp.exp(-jnp.abs(u)))


def _cumsum_tile(f, b_f, carry):
    tb = f.shape[0]
    lane = lax.broadcasted_iota(jnp.int32, (tb, LANES), 1)
    logf = jnp.where(lane < N_HEADS, _log_sigmoid(f + b_f), 0.0)
    hi, mid, lo = _split3(logf)
    rows = lax.broadcasted_iota(jnp.int32, (tb, tb), 0)
    cols = lax.broadcasted_iota(jnp.int32, (tb, tb), 1)
    tril = (cols <= rows).astype(BF16)
    cum = _dot(tril, hi) + _dot(tril, mid) + _dot(tril, lo) + carry[0:1, :]
    carry[...] = jnp.broadcast_to(cum[tb - 1:tb, :], carry.shape)
    return cum


def _aug_lane(h):
    return 64 if h % 2 == 0 else 0


def _augment_heads(bblk, cumv, q_ref, k_ref, v_ref, st_ref):
    t = bblk.shape[0]
    lane = lax.broadcasted_iota(jnp.int32, (t, LANES), 1)
    lane_b = lane.astype(BF16)
    sub8 = lax.broadcasted_iota(jnp.int32, (SUBLANES, LANES), 0)
    lane8 = lax.broadcasted_iota(jnp.int32, (SUBLANES, LANES), 1)
    one = jnp.ones((t, LANES), BF16)
    zero = jnp.zeros((t, LANES), BF16)
    stats = jnp.zeros((SUBLANES, LANES), F32)
    for p in range(4):
        qblk = bblk[:, LANES * p:LANES * (p + 1)] * SCALE
        kblk = bblk[:, 512 + LANES * p:512 + LANES * (p + 1)]
        vblk = bblk[:, 1024 + LANES * p:1024 + LANES * (p + 1)]
        qf, kf = qblk.astype(F32), kblk.astype(F32)
        q2, k2, qk = qf * qf, kf * kf, qf * kf
        for odd in range(2):
            h = 2 * p + odd
            a0 = _aug_lane(h)
            data_b = (lane_b < 64) if odd == 0 else (lane_b >= 64)
            data = (lane < 64) if odd == 0 else (lane >= 64)
            hi, mid, lo = _split3(jnp.broadcast_to(cumv[:, h:h + 1], (t, LANES)))
            ones3_q = (lane_b >= a0 + 3) & (lane_b < a0 + 6)
            ones3_k = (lane_b >= a0) & (lane_b < a0 + 3)
            aug_q = jnp.where(lane_b == a0, hi, jnp.where(lane_b == a0 + 1, mid, jnp.where(
                lane_b == a0 + 2, lo, jnp.where(ones3_q, one, zero))))
            aug_k = jnp.where(ones3_k, one, jnp.where(lane_b == a0 + 3, -hi, jnp.where(
                lane_b == a0 + 4, -mid, jnp.where(lane_b == a0 + 5, -lo, zero))))
            q_ref[:, LANES * h:LANES * (h + 1)] = jnp.where(data_b, qblk, aug_q)
            k_ref[:, LANES * h:LANES * (h + 1)] = jnp.where(data_b, kblk, aug_k)
            v_ref[:, LANES * h:LANES * (h + 1)] = jnp.where(data_b, vblk, jnp.where(lane_b == a0, one, zero))
            qn = jnp.sqrt(jnp.max(jnp.sum(jnp.where(data, q2, 0.0), axis=-1, keepdims=True)))
            kn = jnp.sqrt(jnp.max(jnp.sum(jnp.where(data, k2, 0.0), axis=-1, keepdims=True)))
            dmin = jnp.min(jnp.sum(jnp.where(data, qk, 0.0), axis=-1, keepdims=True))
            c_first, c_last = cumv[0:1, h:h + 1], cumv[t - 1:t, h:h + 1]
            row = jnp.where(lane8 == 0, qn, jnp.where(lane8 == 1, kn, jnp.where(
                lane8 == 2, c_first, jnp.where(lane8 == 3, c_last, jnp.where(lane8 == 4, dmin, 0.0)))))
            stats = jnp.where(sub8 == h, row, stats)
    st_ref[0] = stats


PRUNE_MARGIN = 88.0


def _fox_tile_ranges(stats):
    nt = stats.shape[0]
    qn, kn, c_first, c_last, d_min = (stats[:, :, n] for n in range(5))
    bound = (1.01 * qn[:, None, :] * kn[None, :, :] - jnp.minimum(d_min, 0.0)[:, None, :] + 0.05
             + c_first[:, None, :] - c_last[None, :, :])
    idx = jnp.arange(nt)
    skip = (bound <= -PRUNE_MARGIN) & (idx[None, :, None] < idx[:, None, None])
    first_key = jnp.sum(jnp.cumprod(skip, axis=1), axis=1)
    needed = (idx[None, :, None] >= first_key[:, None, :]) & (idx[None, :, None] <= idx[:, None, None])
    last_query = jnp.max(jnp.where(needed, idx[:, None, None], 0), axis=0)
    n_query = last_query - idx[:, None] + 1
    table = jnp.zeros((4, SUBLANES, LANES), F32)
    for odd in range(2):
        table = table.at[:, odd, :nt].set(first_key[:, odd::2].T.astype(F32))
        table = table.at[:, 2 + odd, :nt].set(n_query[:, odd::2].T.astype(F32))
    return table


def _lane_scalar(block, row, lane_idx):
    sub8 = lax.broadcasted_iota(jnp.int32, (SUBLANES, LANES), 0)
    lane8 = lax.broadcasted_iota(jnp.int32, (SUBLANES, LANES), 1)
    return jnp.sum(jnp.where((sub8 == row) & (lane8 == lane_idx), block, 0.0)).astype(jnp.int32)


def _fox_fwd(qa, ka, va, ranges, t):
    s = qa.shape[0]
    nt = s // t
    nc = t // LANES

    def body(rg_ref, q_ref, k_ref, v_ref, o_ref, lse_ref):
        i = pl.program_id(1)
        lane = lax.broadcasted_iota(jnp.int32, (t, LANES), 1)
        rows = lax.broadcasted_iota(jnp.int32, (t, t), 0)
        cols = lax.broadcasted_iota(jnp.int32, (t, t), 1)
        firsts = [jnp.clip(_lane_scalar(rg_ref[0], hh, i), 0, i) for hh in range(2)]
        first = jnp.maximum(firsts[0], firsts[1])

        def update(js, carry, heads=(0, 1), diagonal=False):
            offs = [pl.multiple_of(j * t, t) for j in js]
            kts = [k_ref[pl.ds(off, t), :] for off in offs]
            vts = [v_ref[pl.ds(off, t), :] for off in offs]
            scs = {hh: [_dot(q_ref[:, LANES * hh:LANES * (hh + 1)], kt[:, LANES * hh:LANES * (hh + 1)], NT) for kt in kts]
                   for hh in heads}
            if diagonal:
                scs = {hh: [jnp.where(cols <= rows, sc, NEG) for sc in scs[hh]] for hh in heads}
            m_new = {}
            for hh in heads:
                part = None
                for sc in scs[hh]:
                    for cch in range(nc):
                        chunk = sc[:, LANES * cch:LANES * (cch + 1)]
                        part = chunk if part is None else jnp.maximum(part, chunk)
                m_new[hh] = jnp.maximum(carry[2 * hh], jnp.max(part, axis=-1, keepdims=True))
            alphas = {hh: jnp.exp(carry[2 * hh] - m_new[hh]) for hh in heads}
            ps = {hh: [jnp.exp(sc - m_new[hh]).astype(BF16) for sc in scs[hh]] for hh in heads}
            out = list(carry)
            for hh in heads:
                pv = None
                for p, vt in zip(ps[hh], vts):
                    term = _dot(p, vt[:, LANES * hh:LANES * (hh + 1)])
                    pv = term if pv is None else pv + term
                out[2 * hh], out[2 * hh + 1] = m_new[hh], alphas[hh] * carry[2 * hh + 1] + pv
            return tuple(out)

        col0 = jnp.full((t, 1), NEG, F32)
        zero = jnp.zeros((t, LANES), F32)
        carry = (col0, zero, col0, zero)
        for hh in range(2):
            carry = lax.fori_loop(firsts[hh], first, lambda j, cr, hh=hh: update([j], cr, heads=(hh,)), carry)
        n_off = i - first
        carry = lax.fori_loop(0, n_off // 2, lambda u, cr: update([first + 2 * u, first + 2 * u + 1], cr), carry)
        carry = lax.fori_loop(0, n_off % 2, lambda u, cr: update([i - 1], cr), carry)
        m0, acc0, m1, acc1 = update([i], carry, diagonal=True)
        l0, l1 = acc0[:, _aug_lane(0):_aug_lane(0) + 1], acc1[:, _aug_lane(1):_aug_lane(1) + 1]
        o_ref[...] = jnp.where(lane < 64, acc0 * (1.0 / l0), acc1 * (1.0 / l1)).astype(BF16)
        sub = lax.broadcasted_iota(jnp.int32, (SUBLANES, t), 0)
        lse0 = jnp.broadcast_to(m0 + jnp.log(l0), (t, LANES)).T[0:SUBLANES, :]
        lse1 = jnp.broadcast_to(m1 + jnp.log(l1), (t, LANES)).T[0:SUBLANES, :]
        lse_ref[0] = jnp.where(sub == 0, lse0, jnp.where(sub == 1, lse1, 0.0))

    pair = pl.BlockSpec((s, 2 * LANES), lambda p, i: (0, p))
    return pl.pallas_call(
        body, name="fox_fwd", grid=(4, nt),
        out_shape=[jax.ShapeDtypeStruct((s, 512), BF16), jax.ShapeDtypeStruct((4, SUBLANES, s), F32)],
        in_specs=[pl.BlockSpec((1, SUBLANES, LANES), lambda p, i: (p, 0, 0)),
                  pl.BlockSpec((t, 2 * LANES), lambda p, i: (i, p)), pair, pair],
        out_specs=[pl.BlockSpec((t, LANES), lambda p, i: (i, p)),
                   pl.BlockSpec((1, SUBLANES, t), lambda p, i: (p, 0, i))],
        compiler_params=_params(("parallel", "arbitrary"), VMEM_LIMIT),
    )(ranges, qa, ka, va)


def _dup_halves(blk, lane):
    f = blk.astype(F32)
    r = pltpu.roll(f, 64, 1)
    return jnp.where(lane < 64, f, r).astype(BF16), jnp.where(lane >= 64, f, r).astype(BF16)


GROUP = 4
GROUP_ROWS = GROUP * WINDOW


def _stack_heads(ref, g, lane):
    parts = []
    for pb in (2 * g, 2 * g + 1):
        blk = ref[:, LANES * pb:LANES * (pb + 1)]
        zero = jnp.zeros_like(blk)
        parts += [jnp.where(lane < 64, blk, zero), jnp.where(lane >= 64, blk, zero)]
    return jnp.concatenate(parts, axis=0)


def _swa_band(a_ref, ap_ref, g, lane):
    k = jnp.concatenate([_dup_halves(ap_ref[:, 512:640], lane)[g], _dup_halves(a_ref[:, 512:640], lane)[g]], axis=0)
    v = jnp.concatenate([_dup_halves(ap_ref[:, 640:768], lane)[g], _dup_halves(a_ref[:, 640:768], lane)[g]], axis=0)
    return k, v


def _swa_logits(q, k, has_prev):
    sc = _dot(q, k, NT) * SCALE
    rr = lax.broadcasted_iota(jnp.int32, sc.shape, 0) % WINDOW
    cc = lax.broadcasted_iota(jnp.int32, sc.shape, 1)
    valid = (cc > rr) & (cc <= rr + WINDOW) & (has_prev | (cc >= WINDOW))
    return jnp.where(valid, sc, NEG)


def _per_head_column(values):
    return jnp.concatenate([jnp.broadcast_to(v, (WINDOW, 1)) for v in values], axis=0)


SWA_BLOCKS = 4
SWA_ROWS = SWA_BLOCKS * WINDOW


def _swa_blocks(a_ref, ap_ref):
    return [ap_ref] + [a_ref.at[pl.ds(WINDOW * jb, WINDOW), :] for jb in range(SWA_BLOCKS)]


def _swa_fwd(a, sinks):
    s = a.shape[0]

    def body(sink_ref, a_ref, ap_ref, o_ref, l_ref):
        lane = lax.broadcasted_iota(jnp.int32, (WINDOW, LANES), 1)
        blocks = _swa_blocks(a_ref, ap_ref)
        units = [(jb, g) for jb in range(SWA_BLOCKS) for g in range(2)]
        sinks_col = [_per_head_column([sink_ref[GROUP * g + hh] for hh in range(GROUP)]) for g in range(2)]
        bands = [_swa_band(blocks[jb + 1], blocks[jb], g, lane) for jb, g in units]
        scs = [_swa_logits(_stack_heads(blocks[jb + 1], g, lane), bands[u][0],
                           (pl.program_id(0) > 0) if jb == 0 else True) for u, (jb, g) in enumerate(units)]
        ms = [jnp.maximum(jnp.max(scs[u], axis=-1, keepdims=True), sinks_col[g]) for u, (jb, g) in enumerate(units)]
        ps = [jnp.exp(scs[u] - ms[u]) for u in range(len(units))]
        dens = [jnp.sum(ps[u], axis=-1, keepdims=True) + jnp.exp(sinks_col[g] - ms[u]) for u, (jb, g) in enumerate(units)]
        outs = [_dot((ps[u] * (1.0 / dens[u])).astype(BF16), bands[u][1]) for u in range(len(units))]
        for jb in range(SWA_BLOCKS):
            rows = slice(WINDOW * jb, WINDOW * (jb + 1))
            l_all = jnp.zeros((WINDOW, LANES), F32)
            for g in range(2):
                u = 2 * jb + g
                lcol = ms[u] + jnp.log(dens[u])
                for pb in range(2):
                    r0 = 2 * pb * WINDOW
                    o_ref[rows, LANES * (2 * g + pb):LANES * (2 * g + pb + 1)] = jnp.where(
                        lane < 64, outs[u][r0:r0 + WINDOW], outs[u][r0 + WINDOW:r0 + 2 * WINDOW]).astype(BF16)
                for hh in range(GROUP):
                    l_all = jnp.where(lane == GROUP * g + hh, lcol[WINDOW * hh:WINDOW * (hh + 1)], l_all)
            l_ref[rows, :] = l_all

    return pl.pallas_call(
        body, name="swa_fwd", grid=(s // SWA_ROWS,),
        out_shape=[jax.ShapeDtypeStruct((s, 512), BF16), jax.ShapeDtypeStruct((s, LANES), F32)],
        in_specs=[pl.BlockSpec(memory_space=pltpu.SMEM),
                  pl.BlockSpec((SWA_ROWS, W_A), lambda i: (i, 0)),
                  pl.BlockSpec((WINDOW, W_A), lambda i: (jnp.maximum(SWA_BLOCKS * i - 1, 0), 0))],
        out_specs=[pl.BlockSpec((SWA_ROWS, 512), lambda i: (i, 0)), pl.BlockSpec((SWA_ROWS, LANES), lambda i: (i, 0))],
        compiler_params=_params(("parallel",)),
    )(sinks, a, a)


def _mid(att_a, att_b, g, x, target, gate, g_final, wo_a, wo_b, w_out, tm=256):
    s = x.shape[0]
    nt = s // tm

    def body(aa_ref, ab_ref, g_ref, x_ref, t_ref, gate_ref, gf_ref, woa_ref, wob_ref, wout_ref,
             dx_ref, daa_ref, dab_ref, dg_ref, delta_ref, dwoa_ref, dwob_ref, dwout_ref, vec_ref,
             acc_gf, acc_gate, acc_loss):
        step = pl.program_id(0)

        @pl.when(step == 0)
        def _():
            dwoa_ref[...] = jnp.zeros_like(dwoa_ref)
            dwob_ref[...] = jnp.zeros_like(dwob_ref)
            dwout_ref[...] = jnp.zeros_like(dwout_ref)
            acc_gf[...] = jnp.zeros_like(acc_gf)
            acc_gate[...] = jnp.zeros_like(acc_gate)
            acc_loss[...] = jnp.zeros_like(acc_loss)

        def fold(v):
            return jnp.sum(v.reshape(tm // SUBLANES, SUBLANES, D_MODEL), axis=0)

        gate = gate_ref[...]
        gfin = gf_ref[...]
        branches = []
        for att_ref, z_off, wo_ref in ((aa_ref, 0, woa_ref), (ab_ref, 512, wob_ref)):
            att = att_ref[...].astype(F32)
            z = g_ref[:, z_off:z_off + 512].astype(F32)
            sz = _sigmoid(z)
            silu = z * sz
            u = (att * silu).astype(BF16)
            branches.append((att, z, sz, silu, u, _dot(u, wo_ref[...])))
        ga = g_ref[:, 1024:2048].astype(F32)
        gb = g_ref[:, 2048:3072].astype(F32)
        sga, sgb = _sigmoid(ga), _sigmoid(gb)
        y_a, y_b = branches[0][5], branches[1][5]
        mb = (sga * y_a + sgb * y_b).astype(BF16)
        o = _dot(mb, wout_ref[...])
        x2 = x_ref[...] + gate * o
        r2 = lax.rsqrt(jnp.mean(x2 * x2, axis=-1, keepdims=True) + NORM_EPS)
        xn2 = x2 * r2
        err = xn2 * gfin - t_ref[...]
        acc_loss[...] += fold(err * err)
        dy = err * (1.0 / D_MODEL)
        acc_gf[...] += fold(dy * xn2)
        dxn = dy * gfin
        dx2 = r2 * (dxn - xn2 * jnp.mean(dxn * xn2, axis=-1, keepdims=True))
        dx_ref[...] = dx2
        acc_gate[...] += fold(dx2 * o)
        d_o = (dx2 * gate).astype(BF16)
        dwout_ref[...] += _dot(mb, d_o, TN)
        dm = _dot(d_o, wout_ref[...], NT)
        dg_ref[:, 1024:2048] = (dm * y_a * sga * (1.0 - sga)).astype(BF16)
        dg_ref[:, 2048:3072] = (dm * y_b * sgb * (1.0 - sgb)).astype(BF16)
        for (att, z, sz, silu, u, _), sg, wo_ref, dwo_ref, datt_ref, z_off in (
                (branches[0], sga, woa_ref, dwoa_ref, daa_ref, 0), (branches[1], sgb, wob_ref, dwob_ref, dab_ref, 512)):
            dyb = (dm * sg).astype(BF16)
            dwo_ref[...] += _dot(u, dyb, TN)
            du = _dot(dyb, wo_ref[...], NT)
            datt = du * silu
            datt_ref[...] = datt.astype(BF16)
            dg_ref[:, z_off:z_off + 512] = (du * att * (sz * (1.0 + z * (1.0 - sz)))).astype(BF16)
            if z_off == 512:
                prod = datt * att
                hi = prod.astype(BF16)
                lo = (prod - hi.astype(F32)).astype(BF16)
                er = lax.broadcasted_iota(jnp.int32, (512, LANES), 0)
                ec = lax.broadcasted_iota(jnp.int32, (512, LANES), 1)
                e = (er // HEAD_DIM == ec).astype(BF16)
                delta = _dot(hi, e) + _dot(lo, e)
                delta_ref[...] = delta.T[0:SUBLANES, :]

        @pl.when(step == nt - 1)
        def _():
            sub = lax.broadcasted_iota(jnp.int32, (SUBLANES, D_MODEL), 0)
            dgf = jnp.sum(acc_gf[...], axis=0, keepdims=True)
            dgate = jnp.sum(acc_gate[...], axis=0, keepdims=True)
            loss = 0.5 * jnp.sum(acc_loss[...]) * (1.0 / D_MODEL)
            vec_ref[...] = jnp.where(sub == 0, dgf, jnp.where(sub == 1, dgate, jnp.where(sub == 2, loss, 0.0)))

    row = lambda w: pl.BlockSpec((tm, w), lambda i: (i, 0))
    return pl.pallas_call(
        body, name="mid", grid=(nt,),
        out_shape=[jax.ShapeDtypeStruct((s, D_MODEL), F32), jax.ShapeDtypeStruct((s, 512), BF16),
                   jax.ShapeDtypeStruct((s, 512), BF16), jax.ShapeDtypeStruct((s, W_G), BF16),
                   jax.ShapeDtypeStruct((SUBLANES, s), F32),
                   jax.ShapeDtypeStruct((512, D_MODEL), F32), jax.ShapeDtypeStruct((512, D_MODEL), F32),
                   jax.ShapeDtypeStruct((D_MODEL, D_MODEL), F32), jax.ShapeDtypeStruct((SUBLANES, D_MODEL), F32)],
        in_specs=[row(512), row(512), row(W_G), row(D_MODEL), row(D_MODEL),
                  _const_spec((1, D_MODEL)), _const_spec((1, D_MODEL)),
                  _const_spec((512, D_MODEL)), _const_spec((512, D_MODEL)), _const_spec((D_MODEL, D_MODEL))],
        out_specs=[row(D_MODEL), row(512), row(512), row(W_G),
                   pl.BlockSpec((SUBLANES, tm), lambda i: (0, i)),
                   pl.BlockSpec((512, D_MODEL), lambda i: (0, 0)), pl.BlockSpec((512, D_MODEL), lambda i: (0, 0)),
                   pl.BlockSpec((D_MODEL, D_MODEL), lambda i: (0, 0)), pl.BlockSpec((SUBLANES, D_MODEL), lambda i: (0, 0))],
        scratch_shapes=[pltpu.VMEM((SUBLANES, D_MODEL), F32)] * 3,
        compiler_params=_params(("arbitrary",), VMEM_LIMIT),
    )(att_a, att_b, g, x, target, gate, g_final, wo_a, wo_b, w_out)


def _rope_bwd(dt, cos, sin, lane):
    u = dt * sin
    lo = (lane % HEAD_DIM) < (HEAD_DIM // 2)
    return dt * cos + jnp.where(lo, pltpu.roll(u, 96, 1), -pltpu.roll(u, 32, 1))


def _swa_bwd(a, datt, l_all, sinks, cos, sin):
    s = a.shape[0]
    nt = s // SWA_ROWS

    def body(sink_ref, a_ref, ap_ref, do_ref, l_ref, cos_ref, sin_ref, da_ref, ds_ref, halo):
        step = pl.program_id(0)
        tile = nt - 1 - step

        @pl.when(step == 0)
        def _():
            halo[...] = jnp.zeros_like(halo)
            ds_ref[...] = jnp.zeros_like(ds_ref)

        lane = lax.broadcasted_iota(jnp.int32, (WINDOW, LANES), 1)
        sub8 = lax.broadcasted_iota(jnp.int32, (SUBLANES, LANES), 0)
        lane8 = lax.broadcasted_iota(jnp.int32, (SUBLANES, LANES), 1)
        blocks = _swa_blocks(a_ref, ap_ref)
        dsink = jnp.zeros((SUBLANES, LANES), F32)

        def join(pair, r0):
            x0, x1 = pair[0][r0:r0 + WINDOW], pair[1][r0:r0 + WINDOW]
            return jnp.where(lane < 64, x0 + pltpu.roll(x0, 64, 1), x1 + pltpu.roll(x1, 64, 1))

        units = [(jb, g) for jb in range(SWA_BLOCKS) for g in range(2)]
        n_u = len(units)
        sinks_col = [_per_head_column([sink_ref[GROUP * g + hh] for hh in range(GROUP)]) for g in range(2)]
        bands = [_swa_band(blocks[jb + 1], blocks[jb], g, lane) for jb, g in units]
        qs = [_stack_heads(blocks[jb + 1], g, lane) for jb, g in units]
        doms = [_stack_heads(do_ref.at[pl.ds(WINDOW * jb, WINDOW), :], g, lane) for jb, g in units]
        lcols = []
        for jb, g in units:
            lv = l_ref[WINDOW * jb:WINDOW * (jb + 1), :]
            lcols.append(_per_head_column([lv[:, GROUP * g + hh:GROUP * g + hh + 1] for hh in range(GROUP)]))
        ps = [jnp.exp(_swa_logits(qs[u], bands[u][0], (tile > 0) if jb == 0 else True) - lcols[u])
              for u, (jb, g) in enumerate(units)]
        dps = [_dot(doms[u], bands[u][1], NT) for u in range(n_u)]
        deltas = [jnp.sum(ps[u] * dps[u], axis=-1, keepdims=True) for u in range(n_u)]
        for u, (jb, g) in enumerate(units):
            sink_term = jnp.exp(sinks_col[g] - lcols[u]) * deltas[u]
            for hh in range(GROUP):
                tot = jnp.sum(sink_term[WINDOW * hh:WINDOW * (hh + 1)])
                dsink = dsink + jnp.where((sub8 == 0) & (lane8 == GROUP * g + hh), -tot, 0.0)
        dss = [(ps[u] * (dps[u] - deltas[u])).astype(BF16) for u in range(n_u)]
        dqs = [_dot(dss[u], bands[u][0]) * SCALE for u in range(n_u)]
        dks = [_dot(dss[u], qs[u], TN) * SCALE for u in range(n_u)]
        dvs = [_dot(ps[u].astype(BF16), doms[u], TN) for u in range(n_u)]

        carry_k, carry_v = halo[:, 0:LANES], halo[:, LANES:2 * LANES]
        for jb in reversed(range(SWA_BLOCKS)):
            rows = slice(WINDOW * jb, WINDOW * (jb + 1))
            cosv, sinv = cos_ref[rows, :], sin_ref[rows, :]
            for g in range(2):
                dq = dqs[2 * jb + g]
                for pb in range(2):
                    r0 = 2 * pb * WINDOW
                    dq_pair = jnp.where(lane < 64, dq[r0:r0 + WINDOW], dq[r0 + WINDOW:r0 + 2 * WINDOW])
                    da_ref[rows, LANES * (2 * g + pb):LANES * (2 * g + pb + 1)] = _rope_bwd(
                        dq_pair, cosv, sinv, lane).astype(BF16)
            dkb, dvb = dks[2 * jb:2 * jb + 2], dvs[2 * jb:2 * jb + 2]
            da_ref[rows, 512:640] = _rope_bwd(join(dkb, WINDOW) + carry_k, cosv, sinv, lane).astype(BF16)
            da_ref[rows, 640:768] = (join(dvb, WINDOW) + carry_v).astype(BF16)
            carry_k, carry_v = join(dkb, 0), join(dvb, 0)
        halo[:, 0:LANES] = carry_k
        halo[:, LANES:2 * LANES] = carry_v
        ds_ref[...] += dsink

    rev = lambda w: pl.BlockSpec((SWA_ROWS, w), lambda i: (nt - 1 - i, 0))
    return pl.pallas_call(
        body, name="swa_bwd", grid=(nt,),
        out_shape=[jax.ShapeDtypeStruct((s, W_A), BF16), jax.ShapeDtypeStruct((SUBLANES, LANES), F32)],
        in_specs=[pl.BlockSpec(memory_space=pltpu.SMEM), rev(W_A),
                  pl.BlockSpec((WINDOW, W_A), lambda i: (jnp.maximum(SWA_BLOCKS * (nt - 1 - i) - 1, 0), 0)),
                  rev(512), rev(LANES), rev(LANES), rev(LANES)],
        out_specs=[rev(W_A), pl.BlockSpec((SUBLANES, LANES), lambda i: (0, 0))],
        scratch_shapes=[pltpu.VMEM((WINDOW, 2 * LANES), F32)],
        compiler_params=_params(("arbitrary",)),
    )(sinks, a, a, datt, l_all, cos, sin)


def _fox_bwd(qa, ka, vb, do, lse, delta, ranges, t):
    s = qa.shape[0]
    nt = s // t

    def body(rg_ref, q_ref, do_ref, lse_ref, dl_ref, k_ref, v_ref, dq_ref, dk_ref, dv_ref, dc_ref, dr_ref, dq_acc):
        p = pl.program_id(0)
        j = pl.program_id(1)
        n_queries = [jnp.clip(_lane_scalar(rg_ref[0], 2 + hh, j), 1, nt - j) for hh in range(2)]

        @pl.when(j == 0)
        def _():
            dq_acc[...] = jnp.zeros_like(dq_acc)

        lane = lax.broadcasted_iota(jnp.int32, (t, LANES), 1)
        rows = lax.broadcasted_iota(jnp.int32, (t, t), 0)
        cols = lax.broadcasted_iota(jnp.int32, (t, t), 1)
        kt = k_ref[...]
        vt = v_ref[...]

        ks = [kt[:, LANES * hh:LANES * (hh + 1)] for hh in range(2)]

        def tile(qis, carry, heads=(0, 1), diagonal=False):
            dk0, dk1, dv = carry
            offs = [pl.multiple_of(i * t, t) for i in qis]
            units = [(u, hh) for u in range(len(qis)) for hh in heads]
            qts = [q_ref[pl.ds(off, t), :] for off in offs]
            dos = [do_ref[pl.ds(off, t), :] for off in offs]
            lses = [lse_ref[0, :, pl.ds(off, t)] for off in offs]
            dls = [dl_ref[0, :, pl.ds(off, t)] for off in offs]
            qs = [qts[u][:, LANES * hh:LANES * (hh + 1)] for u, hh in units]
            doms = [jnp.where((lane < 64) if hh == 0 else (lane >= 64), dos[u], jnp.zeros_like(dos[u])) for u, hh in units]
            sts = [_dot(ks[hh], qs[n], NT) for n, (u, hh) in enumerate(units)]
            dpts = [_dot(vt, doms[n], NT) for n in range(len(units))]
            if diagonal:
                sts = [jnp.where(cols >= rows, st, NEG) for st in sts]
            pts = [jnp.exp(sts[n] - lses[u][hh:hh + 1, :]) for n, (u, hh) in enumerate(units)]
            dsts = [(pts[n] * (dpts[n] - dls[u][hh:hh + 1, :])).astype(BF16) for n, (u, hh) in enumerate(units)]
            for n, (u, hh) in enumerate(units):
                dv = dv + _dot(pts[n].astype(BF16), doms[n])
                term = _dot(dsts[n], qs[n])
                dk0, dk1 = (dk0 + term, dk1) if hh == 0 else (dk0, dk1 + term)
                dq_acc[hh, pl.ds(offs[u], t), :] += _dot(dsts[n], ks[hh], TN)
            return dk0, dk1, dv

        zero = jnp.zeros((t, LANES), F32)
        carry = tile([j], (zero, zero, zero), diagonal=True)
        n_rest = jnp.minimum(n_queries[0], n_queries[1]) - 1
        carry = lax.fori_loop(0, n_rest // 2, lambda u, cr: tile([j + 1 + 2 * u, j + 2 + 2 * u], cr), carry)
        carry = lax.fori_loop(0, n_rest % 2, lambda u, cr: tile([j + n_rest], cr), carry)
        for hh in range(2):
            carry = lax.fori_loop(j + 1 + n_rest, j + n_queries[hh], lambda i, cr, hh=hh: tile([i], cr, heads=(hh,)), carry)
        dk0, dk1, dv = carry
        e0, e1 = _aug_lane(0), _aug_lane(1)
        dk_ref[...] = jnp.where(lane < 64, dk0, dk1).astype(BF16)
        dv_ref[...] = dv.astype(BF16)
        c0 = jnp.broadcast_to(dk0[:, e0 + 3:e0 + 4], (t, LANES))
        c1 = jnp.broadcast_to(dk1[:, e1 + 3:e1 + 4], (t, LANES))
        dc_ref[0] = jnp.where(lane == 2 * p, -c0, jnp.where(lane == 2 * p + 1, -c1, 0.0))

        @pl.when(j == nt - 1)
        def _():
            lane_s = lax.broadcasted_iota(jnp.int32, (s, LANES), 1)
            a0, a1 = dq_acc[0], dq_acc[1]
            dq_ref[...] = (jnp.where(lane_s < 64, a0, a1) * SCALE).astype(BF16)
            r0 = jnp.broadcast_to(a0[:, e0:e0 + 1], (s, LANES))
            r1 = jnp.broadcast_to(a1[:, e1:e1 + 1], (s, LANES))
            dr_ref[0] = jnp.where(lane_s == 2 * p, r0, jnp.where(lane_s == 2 * p + 1, r1, 0.0))

    return pl.pallas_call(
        body, name="fox_bwd", grid=(4, nt),
        out_shape=[jax.ShapeDtypeStruct((s, 512), BF16), jax.ShapeDtypeStruct((s, 512), BF16),
                   jax.ShapeDtypeStruct((s, 512), BF16), jax.ShapeDtypeStruct((4, s, LANES), F32),
                   jax.ShapeDtypeStruct((4, s, LANES), F32)],
        in_specs=[pl.BlockSpec((1, SUBLANES, LANES), lambda p, j: (p, 0, 0)),
                  pl.BlockSpec((s, 2 * LANES), lambda p, j: (0, p)),
                  pl.BlockSpec((s, LANES), lambda p, j: (0, p)),
                  pl.BlockSpec((1, SUBLANES, s), lambda p, j: (p, 0, 0)),
                  pl.BlockSpec((1, SUBLANES, s), lambda p, j: (p, 0, 0)),
                  pl.BlockSpec((t, 2 * LANES), lambda p, j: (j, p)),
                  pl.BlockSpec((t, LANES), lambda p, j: (j, p))],
        out_specs=[pl.BlockSpec((s, LANES), lambda p, j: (0, p)),
                   pl.BlockSpec((t, LANES), lambda p, j: (j, p)),
                   pl.BlockSpec((t, LANES), lambda p, j: (j, p)),
                   pl.BlockSpec((1, t, LANES), lambda p, j: (p, j, 0)),
                   pl.BlockSpec((1, s, LANES), lambda p, j: (p, 0, 0))],
        scratch_shapes=[pltpu.VMEM((2, s, LANES), F32)],
        compiler_params=_params(("parallel", "arbitrary"), VMEM_LIMIT),
    )(ranges, qa, do, lse, delta, ka, vb)


def _forget_logit_grad(dc_ref, dr_ref, f, b_f, carry):
    tb = f.shape[0]
    lane = lax.broadcasted_iota(jnp.int32, (tb, LANES), 1)
    dc = dc_ref[0] + dr_ref[0]
    for k in range(1, 4):
        dc = dc + (dc_ref[k] + dr_ref[k])
    hi, mid, lo = _split3(dc)
    rows = lax.broadcasted_iota(jnp.int32, (tb, tb), 0)
    cols = lax.broadcasted_iota(jnp.int32, (tb, tb), 1)
    triu = (cols >= rows).astype(BF16)
    dlogf = _dot(triu, hi) + _dot(triu, mid) + _dot(triu, lo) + carry[0:1, :]
    carry[...] = jnp.broadcast_to(dlogf[0:1, :], carry.shape)
    return jnp.where(lane < N_HEADS, dlogf * _sigmoid(-(f + b_f)), 0.0)


def _dh_norm_bwd(d_a, d_q, d_k, d_v, dcum_k, dcum_q, f, bf_pad, d_g, w_t, x, dx2, gnorm, scale1, tm=512):
    s = x.shape[0]
    nt = s // tm

    def body(da_ref, dq_ref, dk_ref, dv_ref, dc_ref, dr_ref, f_ref, bf_ref, dg_ref, w_ref, x_ref, dx2_ref, g_ref, sc_ref,
             gx_ref, vec_ref, df_ref, db_ref, a_sh, a_sc, a_g, carry):
        step = pl.program_id(0)

        @pl.when(step == 0)
        def _():
            a_sh[...] = jnp.zeros_like(a_sh)
            a_sc[...] = jnp.zeros_like(a_sc)
            a_g[...] = jnp.zeros_like(a_g)
            carry[...] = jnp.zeros_like(carry)
            db_ref[...] = jnp.zeros_like(db_ref)

        def fold(v):
            return jnp.sum(v.reshape(tm // SUBLANES, SUBLANES, D_MODEL), axis=0)

        dfb = _forget_logit_grad(dc_ref, dr_ref, f_ref[...], bf_ref[...], carry)
        d_f = dfb.astype(BF16)
        df_ref[...] = d_f
        sub8 = lax.broadcasted_iota(jnp.int32, (SUBLANES, LANES), 0)
        db_ref[...] += jnp.where(sub8 == 0, jnp.sum(dfb, axis=0, keepdims=True), 0.0)
        d_all = jnp.concatenate([da_ref[...], dq_ref[...], dk_ref[...], dv_ref[...], d_f, dg_ref[...]], axis=1)
        dh = _dot(d_all, w_ref[...])
        xv = x_ref[...]
        r = lax.rsqrt(jnp.mean(xv * xv, axis=-1, keepdims=True) + NORM_EPS)
        xn = xv * r
        gn = g_ref[...]
        a_sh[...] += fold(dh)
        a_sc[...] += fold(dh * (xn * gn))
        dn1 = dh * sc_ref[...]
        a_g[...] += fold(dn1 * xn)
        dxn = dn1 * gn
        gx_ref[...] = dx2_ref[...] + r * (dxn - xn * jnp.mean(dxn * xn, axis=-1, keepdims=True))

        @pl.when(step == nt - 1)
        def _():
            sub = lax.broadcasted_iota(jnp.int32, (SUBLANES, D_MODEL), 0)
            v_sh = jnp.sum(a_sh[...], axis=0, keepdims=True)
            v_sc = jnp.sum(a_sc[...], axis=0, keepdims=True)
            v_g = jnp.sum(a_g[...], axis=0, keepdims=True)
            vec_ref[...] = jnp.where(sub == 0, v_sh, jnp.where(sub == 1, v_sc, jnp.where(sub == 2, v_g, 0.0)))

    row = lambda w: pl.BlockSpec((tm, w), lambda i: (nt - 1 - i, 0))
    slabs = pl.BlockSpec((4, tm, LANES), lambda i: (0, nt - 1 - i, 0))
    return pl.pallas_call(
        body, name="dh_norm_bwd", grid=(nt,),
        out_shape=[jax.ShapeDtypeStruct((s, D_MODEL), F32), jax.ShapeDtypeStruct((SUBLANES, D_MODEL), F32),
                   jax.ShapeDtypeStruct((s, LANES), BF16), jax.ShapeDtypeStruct((SUBLANES, LANES), F32)],
        in_specs=[row(W_A), row(512), row(512), row(512), slabs, slabs, row(W_F), _const_spec((1, LANES)), row(W_G),
                  _const_spec((W_INT, D_MODEL)), row(D_MODEL), row(D_MODEL), _const_spec((1, D_MODEL)),
                  _const_spec((1, D_MODEL))],
        out_specs=[row(D_MODEL), pl.BlockSpec((SUBLANES, D_MODEL), lambda i: (0, 0)), row(LANES),
                   pl.BlockSpec((SUBLANES, LANES), lambda i: (0, 0))],
        scratch_shapes=[pltpu.VMEM((SUBLANES, D_MODEL), F32)] * 3 + [pltpu.VMEM((SUBLANES, LANES), F32)],
        compiler_params=_params(("arbitrary",), VMEM_LIMIT),
    )(d_a, d_q, d_k, d_v, dcum_k, dcum_q, f, bf_pad, d_g, w_t, x, dx2, gnorm, scale1)


def _dw_in(h_t, d_a, d_q, d_k, d_v, d_f, d_g, ts=1024, tc=512):
    s = h_t.shape[1]
    ns = s // ts
    w_fg = 512 + W_F + W_G - 512
    rows = 128
    n_slot = (R_END // 4 + LANES - 1) // LANES * LANES
    order = [(0, 0, W_A), (4, 0, 512), (1, 0, 512), (2, 0, 512), (3, 0, 512), (4, 512, N_HEADS), (4, 512 + W_F, W_G - 512)]

    def slot_pieces(k):
        lo, hi, out, col = (R_END // 4) * k, (R_END // 4) * (k + 1), [], 0
        for acc_i, c0, w in order:
            a, b = max(lo, col), min(hi, col + w)
            if a < b:
                out.append((acc_i, c0 + a - col, b - a))
            col += w
        return out

    def body(h_ref, da_ref, dq_ref, dk_ref, dv_ref, df_ref, dg_ref, o_ref, acc_a, acc_q, acc_k, acc_v, acc_fg, stage, sem):
        k = pl.program_id(0)
        accs = (acc_a, acc_q, acc_k, acc_v, acc_fg)

        @pl.when(k == 0)
        def _():
            for acc in accs:
                acc[...] = jnp.zeros_like(acc)

        hv = h_ref[...]

        def add(acc, c_acc, d_ref, c_d, width):
            for c0 in range(0, width, tc):
                w = min(tc, width - c0)
                acc[:, c_acc + c0:c_acc + c0 + w] += _dot(hv, d_ref[:, c_d + c0:c_d + c0 + w])

        add(acc_a, 0, da_ref, 0, W_A)
        add(acc_q, 0, dq_ref, 0, 512)
        add(acc_k, 0, dk_ref, 0, 512)
        add(acc_v, 0, dv_ref, 0, 512)
        add(acc_fg, 0, dg_ref, 0, 512)
        add(acc_fg, 512, df_ref, 0, W_F)
        add(acc_fg, 512 + W_F, dg_ref, 512, W_G - 512)

        @pl.when(k == ns - 1)
        def _():
            pending = [None, None]
            for n, r0 in enumerate(range(0, D_MODEL, rows)):
                buf = n % 2
                if pending[buf] is not None:
                    for cp in pending[buf]:
                        cp.wait()
                rs = slice(r0, r0 + rows)
                copies = []
                for slot in range(4):
                    parts = [accs[acc_i][rs, c0:c0 + w] for acc_i, c0, w in slot_pieces(slot)]
                    parts.append(jnp.zeros((rows, n_slot - R_END // 4), F32))
                    stage[buf, slot] = jnp.concatenate(parts, axis=1)
                    cp = pltpu.make_async_copy(stage.at[buf, slot], o_ref.at[slot, pl.ds(r0, rows), :], sem.at[4 * buf + slot])
                    cp.start()
                    copies.append(cp)
                pending[buf] = copies
            for copies in pending:
                for cp in copies:
                    cp.wait()

    spec = lambda d: pl.BlockSpec((ts, d.shape[1]), lambda k: (k, 0))
    return pl.pallas_call(
        body, name="dw_in", grid=(ns,),
        out_shape=jax.ShapeDtypeStruct((4, D_MODEL, n_slot), F32),
        in_specs=[pl.BlockSpec((D_MODEL, ts), lambda k: (0, k))] + [spec(d) for d in (d_a, d_q, d_k, d_v, d_f, d_g)],
        out_specs=pl.BlockSpec(memory_space=pl.ANY),
        scratch_shapes=[pltpu.VMEM((D_MODEL, W_A), F32)] + [pltpu.VMEM((D_MODEL, 512), F32)] * 3
        + [pltpu.VMEM((D_MODEL, w_fg), F32), pltpu.VMEM((2, 4, rows, n_slot), F32), pltpu.SemaphoreType.DMA((8,))],
        compiler_params=_params(("arbitrary",), VMEM_LIMIT),
    )(h_t, d_a, d_q, d_k, d_v, d_f, d_g)


def _small_grads(packs, c_t, dada_shard):
    def body(p_ref, ct_ref, da_ref, sum_ref, gw_ref):
        acc = p_ref[0]
        for dev in range(1, 8):
            acc = acc + p_ref[dev]
        sum_ref[...] = acc
        gw_ref[...] = jnp.dot(ct_ref[...], da_ref[...], preferred_element_type=F32, precision=lax.Precision.HIGHEST)

    return pl.pallas_call(
        body, name="small_grads",
        out_shape=[jax.ShapeDtypeStruct(packs.shape[1:], F32),
                   jax.ShapeDtypeStruct((c_t.shape[0], dada_shard.shape[1]), F32)],
    )(packs, c_t, dada_shard)


def _adamw_body(w_ref, g_ref, m_ref, v_ref, d_ref, mo_ref, vo_ref):
    c1 = 1.0 / (1.0 - ADAM_B1 ** ADAM_STEP)
    c2 = 1.0 / (1.0 - ADAM_B2 ** ADAM_STEP)
    gv = g_ref[...]
    mn = ADAM_B1 * m_ref[...] + (1.0 - ADAM_B1) * gv
    vn = ADAM_B2 * v_ref[...] + (1.0 - ADAM_B2) * (gv * gv)
    mo_ref[...] = mn
    vo_ref[...] = vn
    d_ref[...] = -ADAM_LR * ((mn * c1) / (jnp.sqrt(vn * c2) + ADAM_EPS) + ADAM_WD * w_ref[...])


def _adamw3(w, g, m, v, name, tb=128):
    spec = pl.BlockSpec((tb, SUBLANES, LANES), lambda i: (i, 0, 0))
    return pl.pallas_call(
        functools.partial(_adamw_body), name=name, grid=(pl.cdiv(w.shape[0], tb),),
        out_shape=[jax.ShapeDtypeStruct(w.shape, F32)] * 3,
        in_specs=[spec] * 4, out_specs=[spec] * 3,
        compiler_params=_params(("parallel",)),
    )(w, g, m, v)


def _adamw(w, g, m, v, name):
    r, c = w.shape
    tr = 128 if r % 128 == 0 else r
    body = functools.partial(_adamw_body)
    spec = pl.BlockSpec((tr, c), lambda i: (i, 0))
    return pl.pallas_call(
        body, name=name, grid=(r // tr,),
        out_shape=[jax.ShapeDtypeStruct((r, c), F32)] * 3,
        in_specs=[spec] * 4, out_specs=[spec] * 3,
        compiler_params=_params(("parallel",)),
    )(w, g, m, v)


def _rope_inputs(positions):
    inv_freq = 10000.0 ** (-jnp.arange(0, HEAD_DIM, 2, dtype=F32) / HEAD_DIM)
    pos = jnp.broadcast_to(positions.astype(F32)[:, None], (positions.shape[0], LANES))
    return pos, jnp.tile(inv_freq, 4)[None, :]


def _pad_rows(v, rows=SUBLANES):
    return jnp.pad(v, ((0, rows - v.shape[0]), (0, 0)))


def kernel(x, c, positions, w_ada, b_ada, g_norm, w_in, b_f, sinks, w_o_swa, w_o_fox, w_out, g_final, loss_target, m_w_ada, m_b_ada, m_g_norm, m_w_in, m_b_f, m_sinks, m_w_o_swa, m_w_o_fox, m_w_out, m_g_final, v_w_ada, v_b_ada, v_g_norm, v_w_in, v_b_f, v_sinks, v_w_o_swa, v_w_o_fox, v_w_out, v_g_final):
    ix, iy, ic = lax.axis_index("x"), lax.axis_index("y"), lax.axis_index("c")
    chip = 2 * ix + iy
    dev = 2 * chip + ic
    xs, tgt = x[0], loss_target[0]
    s = xs.shape[0]

    b_ada_shard = lax.dynamic_slice(b_ada, (0, chip * 768), (1, 768))
    ada_parts, w_int, w_int_t, g_oa, g_ob, g_out = _gather_inputs(
        _pad_rows(c), w_ada[0], b_ada_shard, w_in[0], [w_o_swa[0], w_o_fox[0], w_out[0]], "gather_inputs")
    ada = lax.dynamic_index_in_dim(ada_parts, dev, axis=1, keepdims=False).reshape(1, 3 * D_MODEL)
    shift, scale, gate = ada[:, :D_MODEL], ada[:, D_MODEL:2 * D_MODEL], ada[:, 2 * D_MODEL:]
    scale1 = 1.0 + scale

    wo_a = jnp.transpose(g_oa, (1, 0, 2)).reshape(512, D_MODEL)
    wo_b = jnp.transpose(g_ob, (1, 0, 2)).reshape(512, D_MODEL)
    w_o = g_out.reshape(D_MODEL, D_MODEL)

    pos, freq = _rope_inputs(positions[0])
    bf_pad = jnp.pad(b_f, ((0, 0), (0, LANES - N_HEADS)))
    sink_vec = sinks[0]

    a, vb, f, g, h_t, cos, sin, qa, ka, va, stats = _norm_proj(
        xs, g_norm * scale1, shift, w_int, pos, freq, bf_pad, FOX_TILE)
    att_a, l_swa = _swa_fwd(a, sink_vec)
    ranges = _fox_tile_ranges(stats)
    att_b, lse = _fox_fwd(qa, ka, va, ranges, FOX_TILE)

    dx2, datt_a, datt_b, d_g, delta8, dwo_a, dwo_b, dw_out, vec_mid = _mid(
        att_a, att_b, g, xs, tgt, gate, g_final.reshape(1, D_MODEL), wo_a, wo_b, w_o)
    delta = jnp.pad(delta8.reshape(4, 2, s), ((0, 0), (0, SUBLANES - 2), (0, 0)))
    d_a, dsink = _swa_bwd(a, datt_a, l_swa, sink_vec, cos, sin)
    dq, dk, dv, dcum_k, dcum_q = _fox_bwd(qa, ka, vb, datt_b, lse, delta, ranges, FOX_TILE)
    grad_x, vec_dh, d_f, dbf = _dh_norm_bwd(
        d_a, dq, dk, dv, dcum_k, dcum_q, f, bf_pad, d_g, w_int_t, xs, dx2, g_norm, scale1)
    dw_in_slots = _dw_in(h_t, d_a, dq, dk, dv, d_f, d_g)

    tail = jnp.pad(jnp.concatenate([dbf[0:1, :N_HEADS], dsink[0:1, :N_HEADS]], axis=1), ((0, 0), (0, D_MODEL - 2 * N_HEADS)))
    pack = jnp.concatenate([c, vec_dh[0:2], vec_mid[1:2], vec_dh[2:3], vec_mid[0:1], tail, vec_mid[2:3]], axis=0)

    def slots(w, axis):
        if axis == 1:
            return jnp.transpose(w.reshape(w.shape[0], 4, w.shape[1] // 4), (1, 0, 2))
        return w.reshape(4, w.shape[0] // 4, w.shape[1])

    packs, g_wo_a, g_wo_b, g_w_out, g_w_in = _reduce_scatter(
        [slots(dwo_a, 1), slots(dwo_b, 1), slots(dw_out, 0), dw_in_slots], pack, "reduce_grads")
    g_w_in = g_w_in[:, :w_in.shape[2]]
    dada_all = packs[:, 1:4, :].reshape(8, 3 * D_MODEL)
    dada_shard = lax.dynamic_slice(dada_all, (0, chip * 768), (8, 768))
    sums, g_w_ada = _small_grads(packs, packs[:, 0, :].T, dada_shard)
    g_b_ada = sums[1:4].reshape(1, 3 * D_MODEL)
    g_g_norm = sums[4:5]
    g_g_final = sums[5]
    g_b_f = sums[6:7, :N_HEADS]
    g_sinks = sums[6:7, N_HEADS:2 * N_HEADS]
    loss = sums[7, 0]

    grads = {
        "w_ada": g_w_ada, "b_ada": g_b_ada, "g_norm": g_g_norm, "w_in": g_w_in, "b_f": g_b_f, "sinks": g_sinks,
        "w_o_swa": g_wo_a, "w_o_fox": g_wo_b, "w_out": g_w_out, "g_final": g_g_final,
    }
    params = {
        "w_ada": (w_ada, m_w_ada, v_w_ada), "b_ada": (b_ada, m_b_ada, v_b_ada), "g_norm": (g_norm, m_g_norm, v_g_norm),
        "w_in": (w_in, m_w_in, v_w_in), "b_f": (b_f, m_b_f, v_b_f), "sinks": (sinks, m_sinks, v_sinks),
        "w_o_swa": (w_o_swa, m_w_o_swa, v_w_o_swa), "w_o_fox": (w_o_fox, m_w_o_fox, v_w_o_fox),
        "w_out": (w_out, m_w_out, v_w_out), "g_final": (g_final, m_g_final, v_g_final),
    }
    n_col = w_in.shape[2]

    def as_stored(t):
        return jnp.transpose(t, (2, 0, 1)).reshape(n_col, SUBLANES, LANES)

    def from_stored(t):
        return jnp.transpose(t, (1, 2, 0)).reshape(1, D_MODEL, n_col)

    names = list(grads)
    out_g, out_d, out_m, out_v = [], [], [], []
    for nm in names:
        w, m, v = params[nm]
        if nm == "w_in":
            g_st = as_stored(grads[nm][None])
            d_, m_, v_ = _adamw3(as_stored(w), g_st, as_stored(m), as_stored(v), "adamw_" + nm)
            res = [from_stored(t) for t in (g_st, d_, m_, v_)]
        else:
            shape2 = (w.shape[-2], w.shape[-1]) if w.ndim >= 2 else (1, w.shape[0])
            d_, m_, v_ = _adamw(w.reshape(shape2), grads[nm].reshape(shape2), m.reshape(shape2), v.reshape(shape2), "adamw_" + nm)
            res = [t.reshape(w.shape) for t in (grads[nm], d_, m_, v_)]
        out_g.append(res[0])
        out_d.append(res[1])
        out_m.append(res[2])
        out_v.append(res[3])
    return (loss, grad_x[None], *out_g, *out_d, *out_m, *out_v)
```

```python
import functools

import numpy as np
import jax
import jax.numpy as jnp
from jax import lax
from jax.experimental import pallas as pl
from jax.experimental.pallas import tpu as pltpu

F32 = jnp.float32
BF16 = jnp.bfloat16
MESH = pl.DeviceIdType.MESH

D_MODEL = 1024
HEAD_DIM = 64
N_HEADS = 8
WINDOW = 128
NORM_EPS = 1e-6
SCALE = HEAD_DIM ** -0.5
NEG = -1e30
LANES = 128
SUBLANES = 8
VMEM_LIMIT = 60 * 1024 * 1024
FOX_TILE = 512

W_A, W_B, W_F, W_G = 768, 1536, 128, 3072
OFF_A, OFF_B, OFF_F, OFF_G = 0, 768, 2304, 2432
W_INT = W_A + W_B + W_F + W_G
R_ZA, R_QB, R_FB, R_ZB, R_END = 768, 1280, 2816, 2824, 5384

ADAM_LR, ADAM_B1, ADAM_B2, ADAM_EPS, ADAM_WD, ADAM_STEP = 0.001, 0.9, 0.999, 1e-08, 0.01, 10

NT = (((1,), (1,)), ((), ()))
TN = (((0,), (0,)), ((), ()))


def _dot(a, b, dims=None):
    if dims is None:
        return jnp.dot(a, b, preferred_element_type=F32)
    return lax.dot_general(a, b, dims, preferred_element_type=F32)


def _split3(v):
    hi = v.astype(BF16)
    r1 = v - hi.astype(F32)
    mid = r1.astype(BF16)
    lo = (r1 - mid.astype(F32)).astype(BF16)
    return hi, mid, lo


def _sigmoid(v):
    return 1.0 / (1.0 + jnp.exp(-v))


def _params(sem=None, vmem=None):
    return pltpu.CompilerParams(dimension_semantics=sem, vmem_limit_bytes=vmem)


def _const_spec(shape):
    nd = len(shape)
    return pl.BlockSpec(shape, lambda *_: (0,) * nd, pipeline_mode=pl.Buffered(1))


def _flip(v, f):
    return 1 - v if f else v


_CHIP_FLIPS = ((1, 0), (0, 1), (1, 1))


def _gather_inputs(c_pad, w_ada, b_ada_shard, w_in_shard, small_shards, name):
    shards = [w_in_shard] + list(small_shards)
    n = len(shards)
    n_col = w_ada.shape[1]
    shard_w = w_in_shard.shape[1]
    rows = 128

    def body(*refs):
        c_ref, wa_ref, ba_ref = refs[:3]
        ins = refs[3:3 + n]
        ada_ref, wint_ref, wintt_ref = refs[3 + n:6 + n]
        g_in, call_ref, send_sems, recv_sems = refs[5 + 2 * n:9 + 2 * n]
        outs = (g_in,) + tuple(refs[6 + n:5 + 2 * n])
        x, y, c = lax.axis_index("x"), lax.axis_index("y"), lax.axis_index("c")
        k_me = 2 * x + y
        me = 2 * k_me + c
        sibling = (x, y, 1 - c)
        chips = [(_flip(x, fx), _flip(y, fy)) for fx, fy in _CHIP_FLIPS]

        def piece(i, chip_k, half):
            hr = ins[i].shape[0] // 2
            return outs[i].at[chip_k, pl.ds(half * hr, hr), :]

        def copy(i, slot, chip_k, half, to):
            return pltpu.make_async_remote_copy(
                src_ref=piece(i, chip_k, half), dst_ref=piece(i, chip_k, half),
                send_sem=send_sems.at[6 * i + slot], recv_sem=recv_sems.at[6 * i + slot],
                device_id=to, device_id_type=MESH)

        def small(ref, slot, sem, to):
            return pltpu.make_async_remote_copy(
                src_ref=ref.at[slot], dst_ref=ref.at[slot], send_sem=send_sems.at[6 * n + sem],
                recv_sem=recv_sems.at[6 * n + sem], device_id=to, device_id_type=MESH)

        for i in range(n):
            outs[i][k_me] = ins[i][...].astype(BF16)
        started = []
        for i in range(n):
            for j, chip in enumerate(chips):
                cp = copy(i, j, k_me, c, (chip[0], chip[1], c))
                cp.start()
                started.append(cp)

        call_ref[me] = c_ref[...]
        peers = [(_flip(x, k & 4), _flip(y, k & 2), _flip(c, k & 1)) for k in range(1, 8)]
        for k, peer in enumerate(peers):
            cp = small(call_ref, me, k, peer)
            cp.start()
            started.append(cp)
        for k, peer in enumerate(peers):
            small(call_ref, 4 * peer[0] + 2 * peer[1] + peer[2], k, peer).wait_recv()
        c_all = call_ref[:, 0, :].astype(BF16)
        ada_ref[k_me] = _dot(c_all, wa_ref[...].astype(BF16)) + ba_ref[...]
        for j, chip in enumerate(chips):
            cp = small(ada_ref, k_me, 7 + j, (chip[0], chip[1], c))
            cp.start()
            started.append(cp)

        for j, chip in enumerate(chips):
            chip_k = 2 * chip[0] + chip[1]
            for i in range(n):
                copy(i, j, chip_k, c, (chip[0], chip[1], c)).wait_recv()
                cp = copy(i, 3 + j, chip_k, c, sibling)
                cp.start()
                started.append(cp)
        for j, chip in enumerate(chips):
            chip_k = 2 * chip[0] + chip[1]
            small(ada_ref, chip_k, 7 + j, (chip[0], chip[1], c)).wait_recv()
            for i in range(n):
                copy(i, 3 + j, chip_k, 1 - c, sibling).wait_recv()
        for cp in started:
            cp.wait_send()

        def ref_cols(slots, a, b):
            runs = []
            for k in range(4):
                lo, hi = max(a, shard_w * k), min(b, shard_w * (k + 1))
                if lo < hi:
                    runs.append(slots[k][:, lo - shard_w * k:hi - shard_w * k])
            return runs

        for r0 in range(0, D_MODEL, rows):
            rs = slice(r0, r0 + rows)
            slots = [g_in[k, rs, :] for k in range(4)]
            row = jnp.concatenate(
                ref_cols(slots, 0, R_ZA) + ref_cols(slots, R_QB, R_FB) + ref_cols(slots, R_FB, R_ZB)
                + [jnp.zeros((rows, W_F - N_HEADS), BF16)] + ref_cols(slots, R_ZA, R_QB) + ref_cols(slots, R_ZB, R_END),
                axis=1)
            wint_ref[rs, :] = row
            wintt_ref[:, rs] = row.T

    vmem = pl.BlockSpec(memory_space=pltpu.VMEM)
    return pl.pallas_call(
        body, name=name,
        out_shape=[jax.ShapeDtypeStruct((4, 8, n_col), F32), jax.ShapeDtypeStruct((D_MODEL, W_INT), BF16),
                   jax.ShapeDtypeStruct((W_INT, D_MODEL), BF16)]
        + [jax.ShapeDtypeStruct((4,) + s.shape, BF16) for s in small_shards],
        in_specs=[vmem] * (3 + n),
        out_specs=[vmem] * (2 + n),
        scratch_shapes=[pltpu.VMEM((4,) + w_in_shard.shape, BF16), pltpu.VMEM((8,) + c_pad.shape, F32),
                        pltpu.SemaphoreType.DMA((6 * n + 10,)), pltpu.SemaphoreType.DMA((6 * n + 10,))],
        compiler_params=_params(vmem=VMEM_LIMIT),
    )(c_pad, w_ada, b_ada_shard, *shards)


def _reduce_scatter(pieces, pack, name):
    n = len(pieces)

    def body(*refs):
        pack_ref, ins = refs[0], refs[1:1 + n]
        packs_ref, outs = refs[1 + n], refs[2 + n:2 + 2 * n]
        rest = refs[2 + 2 * n:]
        own, got = rest[:n], rest[n:2 * n]
        sendb, recvb = rest[2 * n:3 * n], rest[3 * n:4 * n]
        send_sems, recv_sems, local_sems = rest[4 * n:4 * n + 3]
        x, y, c = lax.axis_index("x"), lax.axis_index("y"), lax.axis_index("c")
        k_me = 2 * x + y
        me = 2 * k_me + c
        sibling = (x, y, 1 - c)
        chips = [(_flip(x, fx), _flip(y, fy)) for fx, fy in _CHIP_FLIPS]
        hrs = [p.shape[1] // 2 for p in pieces]

        def remote(i, slot, src, dst, to):
            return pltpu.make_async_remote_copy(
                src_ref=src, dst_ref=dst, send_sem=send_sems.at[5 * i + slot], recv_sem=recv_sems.at[5 * i + slot],
                device_id=to, device_id_type=MESH)

        started = []
        packs_ref[me] = pack_ref[...]
        peers = [(_flip(x, k & 4), _flip(y, k & 2), _flip(c, k & 1)) for k in range(1, 8)]
        for k, peer in enumerate(peers):
            cp = pltpu.make_async_remote_copy(
                src_ref=pack_ref, dst_ref=packs_ref.at[me], send_sem=send_sems.at[5 * n + k],
                recv_sem=recv_sems.at[5 * n + k], device_id=peer, device_id_type=MESH)
            cp.start()
            started.append(cp)
        loads = []
        for i in range(n):
            ld = pltpu.make_async_copy(ins[i].at[:, pl.ds(c * hrs[i], hrs[i]), :], own[i], local_sems.at[i])
            ld.start()
            loads.append(ld)
            cp = remote(i, 0, ins[i].at[:, pl.ds((1 - c) * hrs[i], hrs[i]), :], got[i], sibling)
            cp.start()
            started.append(cp)
        for i in range(n):
            loads[i].wait()
            remote(i, 0, ins[i].at[:, pl.ds(c * hrs[i], hrs[i]), :], got[i], sibling).wait_recv()
            for j, chip in enumerate(chips):
                chip_k = 2 * chip[0] + chip[1]
                sendb[i][j] = (own[i][chip_k] + got[i][chip_k]).astype(BF16)
                cp = remote(i, 1 + j, sendb[i].at[j], recvb[i].at[j], (chip[0], chip[1], c))
                cp.start()
                started.append(cp)
        for i in range(n):
            acc = own[i][k_me] + got[i][k_me]
            for j, chip in enumerate(chips):
                remote(i, 1 + j, sendb[i].at[j], recvb[i].at[j], (chip[0], chip[1], c)).wait_recv()
                acc = acc + recvb[i][j].astype(F32)
            mine = outs[i].at[pl.ds(c * hrs[i], hrs[i]), :]
            outs[i][pl.ds(pl.multiple_of(c * hrs[i], SUBLANES), hrs[i]), :] = acc
            cp = remote(i, 4, mine, mine, sibling)
            cp.start()
            started.append(cp)
        for i in range(n):
            theirs = outs[i].at[pl.ds((1 - c) * hrs[i], hrs[i]), :]
            remote(i, 4, theirs, theirs, sibling).wait_recv()
        for k, peer in enumerate(peers):
            pltpu.make_async_remote_copy(
                src_ref=pack_ref, dst_ref=packs_ref.at[4 * peer[0] + 2 * peer[1] + peer[2]],
                send_sem=send_sems.at[5 * n + k], recv_sem=recv_sems.at[5 * n + k],
                device_id=peer, device_id_type=MESH).wait_recv()
        for cp in started:
            cp.wait_send()

    vmem = pl.BlockSpec(memory_space=pltpu.VMEM)
    scratch = []
    scratch += [pltpu.VMEM((4, p.shape[1] // 2, p.shape[2]), F32) for p in pieces]
    scratch += [pltpu.VMEM((4, p.shape[1] // 2, p.shape[2]), F32) for p in pieces]
    scratch += [pltpu.VMEM((3, p.shape[1] // 2, p.shape[2]), BF16) for p in pieces]
    scratch += [pltpu.VMEM((3, p.shape[1] // 2, p.shape[2]), BF16) for p in pieces]
    scratch += [pltpu.SemaphoreType.DMA((5 * n + 7,)), pltpu.SemaphoreType.DMA((5 * n + 7,)), pltpu.SemaphoreType.DMA((n,))]
    return pl.pallas_call(
        body, name=name,
        out_shape=[jax.ShapeDtypeStruct((8,) + pack.shape, F32)] + [jax.ShapeDtypeStruct(p.shape[1:], F32) for p in pieces],
        in_specs=[vmem] + [pl.BlockSpec(memory_space=pl.ANY)] * n,
        out_specs=[vmem] * (1 + n),
        scratch_shapes=scratch,
        compiler_params=_params(vmem=VMEM_LIMIT),
    )(pack, *pieces)


def _rope_fwd(t, cos, sin, lane):
    lo = (lane % HEAD_DIM) < (HEAD_DIM // 2)
    return t * cos + jnp.where(lo, -pltpu.roll(t, 96, 1), pltpu.roll(t, 32, 1)) * sin


def _norm_proj(x, gmod, shift, w_int, pos, freq, bf_pad, tm):
    s = x.shape[0]

    def body(x_ref, g_ref, sh_ref, w_ref, pos_ref, fr_ref, bf_ref,
             a_ref, vb_ref, f_ref, gg_ref, ht_ref, cos_ref, sin_ref, q_ref, k_ref, v_ref, st_ref, carry):
        @pl.when(pl.program_id(0) == 0)
        def _():
            carry[...] = jnp.zeros_like(carry)

        xv = x_ref[...]
        r = lax.rsqrt(jnp.mean(xv * xv, axis=-1, keepdims=True) + NORM_EPS)
        hf = (xv * r) * g_ref[...] + sh_ref[...]
        hb = hf.astype(BF16)
        ht_ref[...] = hb.T
        pa = _dot(hb, w_ref[:, OFF_A:OFF_A + W_A])
        ang = pos_ref[...] * fr_ref[...]
        cosv, sinv = jnp.cos(ang), jnp.sin(ang)
        cos_ref[...] = cosv
        sin_ref[...] = sinv
        lane = lax.broadcasted_iota(jnp.int32, (tm, LANES), 1)
        for j in range(5):
            t = pa[:, LANES * j:LANES * (j + 1)]
            a_ref[:, LANES * j:LANES * (j + 1)] = _rope_fwd(t, cosv, sinv, lane).astype(BF16)
        a_ref[:, 640:768] = pa[:, 640:768].astype(BF16)
        pf = _dot(hb, w_ref[:, OFF_F:OFF_F + W_F])
        f_ref[...] = pf
        bblk = _dot(hb, w_ref[:, OFF_B:OFF_B + W_B]).astype(BF16)
        vb_ref[...] = bblk[:, 1024:1536]
        gg_ref[...] = _dot(hb, w_ref[:, OFF_G:OFF_G + W_G]).astype(BF16)
        _augment_heads(bblk, _cumsum_tile(pf, bf_ref[...], carry), q_ref, k_ref, v_ref, st_ref)

    row = lambda w: pl.BlockSpec((tm, w), lambda i: (i, 0))
    return pl.pallas_call(
        body, name="norm_proj", grid=(s // tm,),
        out_shape=[jax.ShapeDtypeStruct((s, W_A), BF16), jax.ShapeDtypeStruct((s, 512), BF16),
                   jax.ShapeDtypeStruct((s, W_F), F32), jax.ShapeDtypeStruct((s, W_G), BF16),
                   jax.ShapeDtypeStruct((D_MODEL, s), BF16),
                   jax.ShapeDtypeStruct((s, LANES), F32), jax.ShapeDtypeStruct((s, LANES), F32)]
        + [jax.ShapeDtypeStruct((s, 1024), BF16)] * 3 + [jax.ShapeDtypeStruct((s // tm, SUBLANES, LANES), F32)],
        in_specs=[row(D_MODEL), _const_spec((1, D_MODEL)), _const_spec((1, D_MODEL)), _const_spec((D_MODEL, W_INT)),
                  row(LANES), _const_spec((1, LANES)), _const_spec((1, LANES))],
        out_specs=[row(W_A), row(512), row(W_F), row(W_G), pl.BlockSpec((D_MODEL, tm), lambda i: (0, i)),
                   row(LANES), row(LANES), row(1024), row(1024), row(1024),
                   pl.BlockSpec((1, SUBLANES, LANES), lambda i: (i, 0, 0))],
        scratch_shapes=[pltpu.VMEM((SUBLANES, LANES), F32)],
        compiler_params=_params(("arbitrary",), VMEM_LIMIT),
    )(x, gmod, shift, w_int, pos, freq, bf_pad)


def _log_sigmoid(u):
    return jnp.minimum(u, 0.0) - jnp.log(1.0 + jnp.exp(-jnp.abs(u)))


def _cumsum_tile(f, b_f, carry):
    tb = f.shape[0]
    lane = lax.broadcasted_iota(jnp.int32, (tb, LANES), 1)
    logf = jnp.where(lane < N_HEADS, _log_sigmoid(f + b_f), 0.0)
    hi, mid, lo = _split3(logf)
    rows = lax.broadcasted_iota(jnp.int32, (tb, tb), 0)
    cols = lax.broadcasted_iota(jnp.int32, (tb, tb), 1)
    tril = (cols <= rows).astype(BF16)
    cum = _dot(tril, hi) + _dot(tril, mid) + _dot(tril, lo) + carry[0:1, :]
    carry[...] = jnp.broadcast_to(cum[tb - 1:tb, :], carry.shape)
    return cum


def _aug_lane(h):
    return 64 if h % 2 == 0 else 0


def _augment_heads(bblk, cumv, q_ref, k_ref, v_ref, st_ref):
    t = bblk.shape[0]
    lane = lax.broadcasted_iota(jnp.int32, (t, LANES), 1)
    lane_b = lane.astype(BF16)
    sub8 = lax.broadcasted_iota(jnp.int32, (SUBLANES, LANES), 0)
    lane8 = lax.broadcasted_iota(jnp.int32, (SUBLANES, LANES), 1)
    one = jnp.ones((t, LANES), BF16)
    zero = jnp.zeros((t, LANES), BF16)
    stats = jnp.zeros((SUBLANES, LANES), F32)
    for p in range(4):
        qblk = bblk[:, LANES * p:LANES * (p + 1)] * SCALE
        kblk = bblk[:, 512 + LANES * p:512 + LANES * (p + 1)]
        vblk = bblk[:, 1024 + LANES * p:1024 + LANES * (p + 1)]
        qf, kf = qblk.astype(F32), kblk.astype(F32)
        q2, k2, qk = qf * qf, kf * kf, qf * kf
        for odd in range(2):
            h = 2 * p + odd
            a0 = _aug_lane(h)
            data_b = (lane_b < 64) if odd == 0 else (lane_b >= 64)
            data = (lane < 64) if odd == 0 else (lane >= 64)
            hi, mid, lo = _split3(jnp.broadcast_to(cumv[:, h:h + 1], (t, LANES)))
            ones3_q = (lane_b >= a0 + 3) & (lane_b < a0 + 6)
            ones3_k = (lane_b >= a0) & (lane_b < a0 + 3)
            aug_q = jnp.where(lane_b == a0, hi, jnp.where(lane_b == a0 + 1, mid, jnp.where(
                lane_b == a0 + 2, lo, jnp.where(ones3_q, one, zero))))
            aug_k = jnp.where(ones3_k, one, jnp.where(lane_b == a0 + 3, -hi, jnp.where(
                lane_b == a0 + 4, -mid, jnp.where(lane_b == a0 + 5, -lo, zero))))
            q_ref[:, LANES * h:LANES * (h + 1)] = jnp.where(data_b, qblk, aug_q)
            k_ref[:, LANES * h:LANES * (h + 1)] = jnp.where(data_b, kblk, aug_k)
            v_ref[:, LANES * h:LANES * (h + 1)] = jnp.where(data_b, vblk, jnp.where(lane_b == a0, one, zero))
            qn = jnp.sqrt(jnp.max(jnp.sum(jnp.where(data, q2, 0.0), axis=-1, keepdims=True)))
            kn = jnp.sqrt(jnp.max(jnp.sum(jnp.where(data, k2, 0.0), axis=-1, keepdims=True)))
            dmin = jnp.min(jnp.sum(jnp.where(data, qk, 0.0), axis=-1, keepdims=True))
            c_first, c_last = cumv[0:1, h:h + 1], cumv[t - 1:t, h:h + 1]
            row = jnp.where(lane8 == 0, qn, jnp.where(lane8 == 1, kn, jnp.where(
                lane8 == 2, c_first, jnp.where(lane8 == 3, c_last, jnp.where(lane8 == 4, dmin, 0.0)))))
            stats = jnp.where(sub8 == h, row, stats)
    st_ref[0] = stats


PRUNE_MARGIN = 88.0


def _fox_tile_ranges(stats):
    nt = stats.shape[0]
    qn, kn, c_first, c_last, d_min = (stats[:, :, n] for n in range(5))
    bound = (1.01 * qn[:, None, :] * kn[None, :, :] - jnp.minimum(d_min, 0.0)[:, None, :] + 0.05
             + c_first[:, None, :] - c_last[None, :, :])
    idx = jnp.arange(nt)
    skip = (bound <= -PRUNE_MARGIN) & (idx[None, :, None] < idx[:, None, None])
    first_key = jnp.sum(jnp.cumprod(skip, axis=1), axis=1)
    needed = (idx[None, :, None] >= first_key[:, None, :]) & (idx[None, :, None] <= idx[:, None, None])
    last_query = jnp.max(jnp.where(needed, idx[:, None, None], 0), axis=0)
    n_query = last_query - idx[:, None] + 1
    table = jnp.zeros((4, SUBLANES, LANES), F32)
    for odd in range(2):
        table = table.at[:, odd, :nt].set(first_key[:, odd::2].T.astype(F32))
        table = table.at[:, 2 + odd, :nt].set(n_query[:, odd::2].T.astype(F32))
    return table


def _lane_scalar(block, row, lane_idx):
    sub8 = lax.broadcasted_iota(jnp.int32, (SUBLANES, LANES), 0)
    lane8 = lax.broadcasted_iota(jnp.int32, (SUBLANES, LANES), 1)
    return jnp.sum(jnp.where((sub8 == row) & (lane8 == lane_idx), block, 0.0)).astype(jnp.int32)


def _fox_fwd(qa, ka, va, ranges, t):
    s = qa.shape[0]
    nt = s // t
    nc = t // LANES

    def body(rg_ref, q_ref, k_ref, v_ref, o_ref, lse_ref):
        i = pl.program_id(1)
        lane = lax.broadcasted_iota(jnp.int32, (t, LANES), 1)
        rows = lax.broadcasted_iota(jnp.int32, (t, t), 0)
        cols = lax.broadcasted_iota(jnp.int32, (t, t), 1)
        firsts = [jnp.clip(_lane_scalar(rg_ref[0], hh, i), 0, i) for hh in range(2)]
        first = jnp.maximum(firsts[0], firsts[1])

        def update(js, carry, heads=(0, 1), diagonal=False):
            offs = [pl.multiple_of(j * t, t) for j in js]
            kts = [k_ref[pl.ds(off, t), :] for off in offs]
            vts = [v_ref[pl.ds(off, t), :] for off in offs]
            scs = {hh: [_dot(q_ref[:, LANES * hh:LANES * (hh + 1)], kt[:, LANES * hh:LANES * (hh + 1)], NT) for kt in kts]
                   for hh in heads}
            if diagonal:
                scs = {hh: [jnp.where(cols <= rows, sc, NEG) for sc in scs[hh]] for hh in heads}
            m_new = {}
            for hh in heads:
                part = None
                for sc in scs[hh]:
                    for cch in range(nc):
                        chunk = sc[:, LANES * cch:LANES * (cch + 1)]
                        part = chunk if part is None else jnp.maximum(part, chunk)
                m_new[hh] = jnp.maximum(carry[2 * hh], jnp.max(part, axis=-1, keepdims=True))
            alphas = {hh: jnp.exp(carry[2 * hh] - m_new[hh]) for hh in heads}
            ps = {hh: [jnp.exp(sc - m_new[hh]).astype(BF16) for sc in scs[hh]] for hh in heads}
            out = list(carry)
            for hh in heads:
                pv = None
                for p, vt in zip(ps[hh], vts):
                    term = _dot(p, vt[:, LANES * hh:LANES * (hh + 1)])
                    pv = term if pv is None else pv + term
                out[2 * hh], out[2 * hh + 1] = m_new[hh], alphas[hh] * carry[2 * hh + 1] + pv
            return tuple(out)

        col0 = jnp.full((t, 1), NEG, F32)
        zero = jnp.zeros((t, LANES), F32)
        carry = (col0, zero, col0, zero)
        for hh in range(2):
            carry = lax.fori_loop(firsts[hh], first, lambda j, cr, hh=hh: update([j], cr, heads=(hh,)), carry)
        n_off = i - first
        carry = lax.fori_loop(0, n_off // 2, lambda u, cr: update([first + 2 * u, first + 2 * u + 1], cr), carry)
        carry = lax.fori_loop(0, n_off % 2, lambda u, cr: update([i - 1], cr), carry)
        m0, acc0, m1, acc1 = update([i], carry, diagonal=True)
        l0, l1 = acc0[:, _aug_lane(0):_aug_lane(0) + 1], acc1[:, _aug_lane(1):_aug_lane(1) + 1]
        o_ref[...] = jnp.where(lane < 64, acc0 * (1.0 / l0), acc1 * (1.0 / l1)).astype(BF16)
        sub = lax.broadcasted_iota(jnp.int32, (SUBLANES, t), 0)
        lse0 = jnp.broadcast_to(m0 + jnp.log(l0), (t, LANES)).T[0:SUBLANES, :]
        lse1 = jnp.broadcast_to(m1 + jnp.log(l1), (t, LANES)).T[0:SUBLANES, :]
        lse_ref[0] = jnp.where(sub == 0, lse0, jnp.where(sub == 1, lse1, 0.0))

    pair = pl.BlockSpec((s, 2 * LANES), lambda p, i: (0, p))
    return pl.pallas_call(
        body, name="fox_fwd", grid=(4, nt),
        out_shape=[jax.ShapeDtypeStruct((s, 512), BF16), jax.ShapeDtypeStruct((4, SUBLANES, s), F32)],
        in_specs=[pl.BlockSpec((1, SUBLANES, LANES), lambda p, i: (p, 0, 0)),
                  pl.BlockSpec((t, 2 * LANES), lambda p, i: (i, p)), pair, pair],
        out_specs=[pl.BlockSpec((t, LANES), lambda p, i: (i, p)),
                   pl.BlockSpec((1, SUBLANES, t), lambda p, i: (p, 0, i))],
        compiler_params=_params(("parallel", "arbitrary"), VMEM_LIMIT),
    )(ranges, qa, ka, va)


def _dup_halves(blk, lane):
    f = blk.astype(F32)
    r = pltpu.roll(f, 64, 1)
    return jnp.where(lane < 64, f, r).astype(BF16), jnp.where(lane >= 64, f, r).astype(BF16)


GROUP = 4
GROUP_ROWS = GROUP * WINDOW


def _stack_heads(ref, g, lane):
    parts = []
    for pb in (2 * g, 2 * g + 1):
        blk = ref[:, LANES * pb:LANES * (pb + 1)]
        zero = jnp.zeros_like(blk)
        parts += [jnp.where(lane < 64, blk, zero), jnp.where(lane >= 64, blk, zero)]
    return jnp.concatenate(parts, axis=0)


def _swa_band(a_ref, ap_ref, g, lane):
    k = jnp.concatenate([_dup_halves(ap_ref[:, 512:640], lane)[g], _dup_halves(a_ref[:, 512:640], lane)[g]], axis=0)
    v = jnp.concatenate([_dup_halves(ap_ref[:, 640:768], lane)[g], _dup_halves(a_ref[:, 640:768], lane)[g]], axis=0)
    return k, v


def _swa_logits(q, k, has_prev):
    sc = _dot(q, k, NT) * SCALE
    rr = lax.broadcasted_iota(jnp.int32, sc.shape, 0) % WINDOW
    cc = lax.broadcasted_iota(jnp.int32, sc.shape, 1)
    valid = (cc > rr) & (cc <= rr + WINDOW) & (has_prev | (cc >= WINDOW))
    return jnp.where(valid, sc, NEG)


def _per_head_column(values):
    return jnp.concatenate([jnp.broadcast_to(v, (WINDOW, 1)) for v in values], axis=0)


SWA_BLOCKS = 4
SWA_ROWS = SWA_BLOCKS * WINDOW


def _swa_blocks(a_ref, ap_ref):
    return [ap_ref] + [a_ref.at[pl.ds(WINDOW * jb, WINDOW), :] for jb in range(SWA_BLOCKS)]


def _swa_fwd(a, sinks):
    s = a.shape[0]

    def body(sink_ref, a_ref, ap_ref, o_ref, l_ref):
        lane = lax.broadcasted_iota(jnp.int32, (WINDOW, LANES), 1)
        blocks = _swa_blocks(a_ref, ap_ref)
        units = [(jb, g) for jb in range(SWA_BLOCKS) for g in range(2)]
        sinks_col = [_per_head_column([sink_ref[GROUP * g + hh] for hh in range(GROUP)]) for g in range(2)]
        bands = [_swa_band(blocks[jb + 1], blocks[jb], g, lane) for jb, g in units]
        scs = [_swa_logits(_stack_heads(blocks[jb + 1], g, lane), bands[u][0],
                           (pl.program_id(0) > 0) if jb == 0 else True) for u, (jb, g) in enumerate(units)]
        ms = [jnp.maximum(jnp.max(scs[u], axis=-1, keepdims=True), sinks_col[g]) for u, (jb, g) in enumerate(units)]
        ps = [jnp.exp(scs[u] - ms[u]) for u in range(len(units))]
        dens = [jnp.sum(ps[u], axis=-1, keepdims=True) + jnp.exp(sinks_col[g] - ms[u]) for u, (jb, g) in enumerate(units)]
        outs = [_dot((ps[u] * (1.0 / dens[u])).astype(BF16), bands[u][1]) for u in range(len(units))]
        for jb in range(SWA_BLOCKS):
            rows = slice(WINDOW * jb, WINDOW * (jb + 1))
            l_all = jnp.zeros((WINDOW, LANES), F32)
            for g in range(2):
                u = 2 * jb + g
                lcol = ms[u] + jnp.log(dens[u])
                for pb in range(2):
                    r0 = 2 * pb * WINDOW
                    o_ref[rows, LANES * (2 * g + pb):LANES * (2 * g + pb + 1)] = jnp.where(
                        lane < 64, outs[u][r0:r0 + WINDOW], outs[u][r0 + WINDOW:r0 + 2 * WINDOW]).astype(BF16)
                for hh in range(GROUP):
                    l_all = jnp.where(lane == GROUP * g + hh, lcol[WINDOW * hh:WINDOW * (hh + 1)], l_all)
            l_ref[rows, :] = l_all

    return pl.pallas_call(
        body, name="swa_fwd", grid=(s // SWA_ROWS,),
        out_shape=[jax.ShapeDtypeStruct((s, 512), BF16), jax.ShapeDtypeStruct((s, LANES), F32)],
        in_specs=[pl.BlockSpec(memory_space=pltpu.SMEM),
                  pl.BlockSpec((SWA_ROWS, W_A), lambda i: (i, 0)),
                  pl.BlockSpec((WINDOW, W_A), lambda i: (jnp.maximum(SWA_BLOCKS * i - 1, 0), 0))],
        out_specs=[pl.BlockSpec((SWA_ROWS, 512), lambda i: (i, 0)), pl.BlockSpec((SWA_ROWS, LANES), lambda i: (i, 0))],
        compiler_params=_params(("parallel",)),
    )(sinks, a, a)


def _mid(att_a, att_b, g, x, target, gate, g_final, wo_a, wo_b, w_out, tm=256):
    s = x.shape[0]
    nt = s // tm

    def body(aa_ref, ab_ref, g_ref, x_ref, t_ref, gate_ref, gf_ref, woa_ref, wob_ref, wout_ref,
             dx_ref, daa_ref, dab_ref, dg_ref, delta_ref, dwoa_ref, dwob_ref, dwout_ref, vec_ref,
             acc_gf, acc_gate, acc_loss):
        step = pl.program_id(0)

        @pl.when(step == 0)
        def _():
            dwoa_ref[...] = jnp.zeros_like(dwoa_ref)
            dwob_ref[...] = jnp.zeros_like(dwob_ref)
            dwout_ref[...] = jnp.zeros_like(dwout_ref)
            acc_gf[...] = jnp.zeros_like(acc_gf)
            acc_gate[...] = jnp.zeros_like(acc_gate)
            acc_loss[...] = jnp.zeros_like(acc_loss)

        def fold(v):
            return jnp.sum(v.reshape(tm // SUBLANES, SUBLANES, D_MODEL), axis=0)

        gate = gate_ref[...]
        gfin = gf_ref[...]
        branches = []
        for att_ref, z_off, wo_ref in ((aa_ref, 0, woa_ref), (ab_ref, 512, wob_ref)):
            att = att_ref[...].astype(F32)
            z = g_ref[:, z_off:z_off + 512].astype(F32)
            sz = _sigmoid(z)
            silu = z * sz
            u = (att * silu).astype(BF16)
            branches.append((att, z, sz, silu, u, _dot(u, wo_ref[...])))
        ga = g_ref[:, 1024:2048].astype(F32)
        gb = g_ref[:, 2048:3072].astype(F32)
        sga, sgb = _sigmoid(ga), _sigmoid(gb)
        y_a, y_b = branches[0][5], branches[1][5]
        mb = (sga * y_a + sgb * y_b).astype(BF16)
        o = _dot(mb, wout_ref[...])
        x2 = x_ref[...] + gate * o
        r2 = lax.rsqrt(jnp.mean(x2 * x2, axis=-1, keepdims=True) + NORM_EPS)
        xn2 = x2 * r2
        err = xn2 * gfin - t_ref[...]
        acc_loss[...] += fold(err * err)
        dy = err * (1.0 / D_MODEL)
        acc_gf[...] += fold(dy * xn2)
        dxn = dy * gfin
        dx2 = r2 * (dxn - xn2 * jnp.mean(dxn * xn2, axis=-1, keepdims=True))
        dx_ref[...] = dx2
        acc_gate[...] += fold(dx2 * o)
        d_o = (dx2 * gate).astype(BF16)
        dwout_ref[...] += _dot(mb, d_o, TN)
        dm = _dot(d_o, wout_ref[...], NT)
        dg_ref[:, 1024:2048] = (dm * y_a * sga * (1.0 - sga)).astype(BF16)
        dg_ref[:, 2048:3072] = (dm * y_b * sgb * (1.0 - sgb)).astype(BF16)
        for (att, z, sz, silu, u, _), sg, wo_ref, dwo_ref, datt_ref, z_off in (
                (branches[0], sga, woa_ref, dwoa_ref, daa_ref, 0), (branches[1], sgb, wob_ref, dwob_ref, dab_ref, 512)):
            dyb = (dm * sg).astype(BF16)
            dwo_ref[...] += _dot(u, dyb, TN)
            du = _dot(dyb, wo_ref[...], NT)
            datt = du * silu
            datt_ref[...] = datt.astype(BF16)
            dg_ref[:, z_off:z_off + 512] = (du * att * (sz * (1.0 + z * (1.0 - sz)))).astype(BF16)
            if z_off == 512:
                prod = datt * att
                hi = prod.astype(BF16)
                lo = (prod - hi.astype(F32)).astype(BF16)
                er = lax.broadcasted_iota(jnp.int32, (512, LANES), 0)
                ec = lax.broadcasted_iota(jnp.int32, (512, LANES), 1)
                e = (er // HEAD_DIM == ec).astype(BF16)
                delta = _dot(hi, e) + _dot(lo, e)
                delta_ref[...] = delta.T[0:SUBLANES, :]

        @pl.when(step == nt - 1)
        def _():
            sub = lax.broadcasted_iota(jnp.int32, (SUBLANES, D_MODEL), 0)
            dgf = jnp.sum(acc_gf[...], axis=0, keepdims=True)
            dgate = jnp.sum(acc_gate[...], axis=0, keepdims=True)
            loss = 0.5 * jnp.sum(acc_loss[...]) * (1.0 / D_MODEL)
            vec_ref[...] = jnp.where(sub == 0, dgf, jnp.where(sub == 1, dgate, jnp.where(sub == 2, loss, 0.0)))

    row = lambda w: pl.BlockSpec((tm, w), lambda i: (i, 0))
    return pl.pallas_call(
        body, name="mid", grid=(nt,),
        out_shape=[jax.ShapeDtypeStruct((s, D_MODEL), F32), jax.ShapeDtypeStruct((s, 512), BF16),
                   jax.ShapeDtypeStruct((s, 512), BF16), jax.ShapeDtypeStruct((s, W_G), BF16),
                   jax.ShapeDtypeStruct((SUBLANES, s), F32),
                   jax.ShapeDtypeStruct((512, D_MODEL), F32), jax.ShapeDtypeStruct((512, D_MODEL), F32),
                   jax.ShapeDtypeStruct((D_MODEL, D_MODEL), F32), jax.ShapeDtypeStruct((SUBLANES, D_MODEL), F32)],
        in_specs=[row(512), row(512), row(W_G), row(D_MODEL), row(D_MODEL),
                  _const_spec((1, D_MODEL)), _const_spec((1, D_MODEL)),
                  _const_spec((512, D_MODEL)), _const_spec((512, D_MODEL)), _const_spec((D_MODEL, D_MODEL))],
        out_specs=[row(D_MODEL), row(512), row(512), row(W_G),
                   pl.BlockSpec((SUBLANES, tm), lambda i: (0, i)),
                   pl.BlockSpec((512, D_MODEL), lambda i: (0, 0)), pl.BlockSpec((512, D_MODEL), lambda i: (0, 0)),
                   pl.BlockSpec((D_MODEL, D_MODEL), lambda i: (0, 0)), pl.BlockSpec((SUBLANES, D_MODEL), lambda i: (0, 0))],
        scratch_shapes=[pltpu.VMEM((SUBLANES, D_MODEL), F32)] * 3,
        compiler_params=_params(("arbitrary",), VMEM_LIMIT),
    )(att_a, att_b, g, x, target, gate, g_final, wo_a, wo_b, w_out)


def _rope_bwd(dt, cos, sin, lane):
    u = dt * sin
    lo = (lane % HEAD_DIM) < (HEAD_DIM // 2)
    return dt * cos + jnp.where(lo, pltpu.roll(u, 96, 1), -pltpu.roll(u, 32, 1))


def _reduce_in_phases(step, last, ins, outs, own, got, sendb, recvb, fin, send_sems, recv_sems, local_sems):
    n = len(ins)
    x, y, c = lax.axis_index("x"), lax.axis_index("y"), lax.axis_index("c")
    k_me = 2 * x + y
    sibling = (x, y, 1 - c)
    chips = [(_flip(x, fx), _flip(y, fy)) for fx, fy in _CHIP_FLIPS]
    hrs = [ref.shape[1] // 2 for ref in ins]

    def remote(i, slot, src, dst, to):
        return pltpu.make_async_remote_copy(
            src_ref=src, dst_ref=dst, send_sem=send_sems.at[5 * i + slot], recv_sem=recv_sems.at[5 * i + slot],
            device_id=to, device_id_type=MESH)

    def half_rows(i, half):
        return pl.ds(pl.multiple_of(half * hrs[i], SUBLANES), hrs[i])

    def load(i):
        return pltpu.make_async_copy(ins[i].at[:, half_rows(i, c), :], own[i], local_sems.at[i])

    def to_sibling(i):
        return remote(i, 0, ins[i].at[:, half_rows(i, 1 - c), :], got[i], sibling)

    def to_chip(i, j):
        return remote(i, 1 + j, sendb[i].at[j], recvb[i].at[j], (chips[j][0], chips[j][1], c))

    def swap(i, half):
        rows = fin[i].at[half_rows(i, half), :]
        return remote(i, 4, rows, rows, sibling)

    @pl.when(step == 0)
    def _():
        for i in range(n):
            load(i).start()
            to_sibling(i).start()

    @pl.when(step == last // 4)
    def _():
        for i in range(n):
            load(i).wait()
            to_sibling(i).wait_recv()
            for j, chip in enumerate(chips):
                chip_k = 2 * chip[0] + chip[1]
                sendb[i][j] = (own[i][chip_k] + got[i][chip_k]).astype(BF16)
                to_chip(i, j).start()

    @pl.when(step == (3 * last) // 4)
    def _():
        for i in range(n):
            acc = own[i][k_me] + got[i][k_me]
            for j in range(3):
                to_chip(i, j).wait_recv()
                acc = acc + recvb[i][j].astype(F32)
            fin[i][half_rows(i, c), :] = acc
            swap(i, c).start()

    @pl.when(step == last)
    def _():
        for i in range(n):
            swap(i, 1 - c).wait_recv()
        for i in range(n):
            to_sibling(i).wait_send()
            for j in range(3):
                to_chip(i, j).wait_send()
            swap(i, c).wait_send()
            outs[i][...] = fin[i][...]


def _swa_bwd(a, datt, l_all, sinks, cos, sin, small_grads):
    s = a.shape[0]
    nt = s // SWA_ROWS
    n_sm = len(small_grads)

    def body(*refs):
        sink_ref, a_ref, ap_ref, do_ref, l_ref, cos_ref, sin_ref = refs[:7]
        sm_in = refs[7:7 + n_sm]
        da_ref, ds_ref = refs[7 + n_sm:9 + n_sm]
        sm_out = refs[9 + n_sm:9 + 2 * n_sm]
        halo = refs[9 + 2 * n_sm]
        bufs = refs[10 + 2 * n_sm:]
        own, got, sendb, recvb, fin = (bufs[m * n_sm:(m + 1) * n_sm] for m in range(5))
        step = pl.program_id(0)
        tile = nt - 1 - step
        _reduce_in_phases(step, nt - 1, sm_in, sm_out, own, got, sendb, recvb, fin, *bufs[5 * n_sm:5 * n_sm + 3])

        @pl.when(step == 0)
        def _():
            halo[...] = jnp.zeros_like(halo)
            ds_ref[...] = jnp.zeros_like(ds_ref)

        lane = lax.broadcasted_iota(jnp.int32, (WINDOW, LANES), 1)
        sub8 = lax.broadcasted_iota(jnp.int32, (SUBLANES, LANES), 0)
        lane8 = lax.broadcasted_iota(jnp.int32, (SUBLANES, LANES), 1)
        blocks = _swa_blocks(a_ref, ap_ref)
        dsink = jnp.zeros((SUBLANES, LANES), F32)

        def join(pair, r0):
            x0, x1 = pair[0][r0:r0 + WINDOW], pair[1][r0:r0 + WINDOW]
            return jnp.where(lane < 64, x0 + pltpu.roll(x0, 64, 1), x1 + pltpu.roll(x1, 64, 1))

        units = [(jb, g) for jb in range(SWA_BLOCKS) for g in range(2)]
        n_u = len(units)
        sinks_col = [_per_head_column([sink_ref[GROUP * g + hh] for hh in range(GROUP)]) for g in range(2)]
        bands = [_swa_band(blocks[jb + 1], blocks[jb], g, lane) for jb, g in units]
        qs = [_stack_heads(blocks[jb + 1], g, lane) for jb, g in units]
        doms = [_stack_heads(do_ref.at[pl.ds(WINDOW * jb, WINDOW), :], g, lane) for jb, g in units]
        lcols = []
        for jb, g in units:
            lv = l_ref[WINDOW * jb:WINDOW * (jb + 1), :]
            lcols.append(_per_head_column([lv[:, GROUP * g + hh:GROUP * g + hh + 1] for hh in range(GROUP)]))
        ps = [jnp.exp(_swa_logits(qs[u], bands[u][0], (tile > 0) if jb == 0 else True) - lcols[u])
              for u, (jb, g) in enumerate(units)]
        dps = [_dot(doms[u], bands[u][1], NT) for u in range(n_u)]
        deltas = [jnp.sum(ps[u] * dps[u], axis=-1, keepdims=True) for u in range(n_u)]
        for u, (jb, g) in enumerate(units):
            sink_term = jnp.exp(sinks_col[g] - lcols[u]) * deltas[u]
            for hh in range(GROUP):
                tot = jnp.sum(sink_term[WINDOW * hh:WINDOW * (hh + 1)])
                dsink = dsink + jnp.where((sub8 == 0) & (lane8 == GROUP * g + hh), -tot, 0.0)
        dss = [(ps[u] * (dps[u] - deltas[u])).astype(BF16) for u in range(n_u)]
        dqs = [_dot(dss[u], bands[u][0]) * SCALE for u in range(n_u)]
        dks = [_dot(dss[u], qs[u], TN) * SCALE for u in range(n_u)]
        dvs = [_dot(ps[u].astype(BF16), doms[u], TN) for u in range(n_u)]

        carry_k, carry_v = halo[:, 0:LANES], halo[:, LANES:2 * LANES]
        for jb in reversed(range(SWA_BLOCKS)):
            rows = slice(WINDOW * jb, WINDOW * (jb + 1))
            cosv, sinv = cos_ref[rows, :], sin_ref[rows, :]
            for g in range(2):
                dq = dqs[2 * jb + g]
                for pb in range(2):
                    r0 = 2 * pb * WINDOW
                    dq_pair = jnp.where(lane < 64, dq[r0:r0 + WINDOW], dq[r0 + WINDOW:r0 + 2 * WINDOW])
                    da_ref[rows, LANES * (2 * g + pb):LANES * (2 * g + pb + 1)] = _rope_bwd(
                        dq_pair, cosv, sinv, lane).astype(BF16)
            dkb, dvb = dks[2 * jb:2 * jb + 2], dvs[2 * jb:2 * jb + 2]
            da_ref[rows, 512:640] = _rope_bwd(join(dkb, WINDOW) + carry_k, cosv, sinv, lane).astype(BF16)
            da_ref[rows, 640:768] = (join(dvb, WINDOW) + carry_v).astype(BF16)
            carry_k, carry_v = join(dkb, 0), join(dvb, 0)
        halo[:, 0:LANES] = carry_k
        halo[:, LANES:2 * LANES] = carry_v
        ds_ref[...] += dsink

    rev = lambda w: pl.BlockSpec((SWA_ROWS, w), lambda i: (nt - 1 - i, 0))
    halves = [(4, p.shape[1] // 2, p.shape[2]) for p in small_grads]
    scratch = [pltpu.VMEM((WINDOW, 2 * LANES), F32)]
    scratch += [pltpu.VMEM(h, F32) for h in halves] * 2
    scratch += [pltpu.VMEM((3,) + h[1:], BF16) for h in halves] * 2
    scratch += [pltpu.VMEM(p.shape[1:], F32) for p in small_grads]
    scratch += [pltpu.SemaphoreType.DMA((5 * n_sm,)), pltpu.SemaphoreType.DMA((5 * n_sm,)), pltpu.SemaphoreType.DMA((n_sm,))]
    return pl.pallas_call(
        body, name="swa_bwd", grid=(nt,),
        out_shape=[jax.ShapeDtypeStruct((s, W_A), BF16), jax.ShapeDtypeStruct((SUBLANES, LANES), F32)]
        + [jax.ShapeDtypeStruct(p.shape[1:], F32) for p in small_grads],
        in_specs=[pl.BlockSpec(memory_space=pltpu.SMEM), rev(W_A),
                  pl.BlockSpec((WINDOW, W_A), lambda i: (jnp.maximum(SWA_BLOCKS * (nt - 1 - i) - 1, 0), 0)),
                  rev(512), rev(LANES), rev(LANES), rev(LANES)] + [pl.BlockSpec(memory_space=pl.ANY)] * n_sm,
        out_specs=[rev(W_A), pl.BlockSpec((SUBLANES, LANES), lambda i: (0, 0))]
        + [pl.BlockSpec(p.shape[1:], lambda i: (0, 0)) for p in small_grads],
        scratch_shapes=scratch,
        compiler_params=_params(("arbitrary",), VMEM_LIMIT),
    )(sinks, a, a, datt, l_all, cos, sin, *small_grads)


def _fox_bwd(qa, ka, vb, do, lse, delta, ranges, t):
    s = qa.shape[0]
    nt = s // t

    def body(rg_ref, q_ref, do_ref, lse_ref, dl_ref, k_ref, v_ref, dq_ref, dk_ref, dv_ref, dc_ref, dr_ref, dq_acc):
        p = pl.program_id(0)
        j = pl.program_id(1)
        n_queries = [jnp.clip(_lane_scalar(rg_ref[0], 2 + hh, j), 1, nt - j) for hh in range(2)]

        @pl.when(j == 0)
        def _():
            dq_acc[...] = jnp.zeros_like(dq_acc)

        lane = lax.broadcasted_iota(jnp.int32, (t, LANES), 1)
        rows = lax.broadcasted_iota(jnp.int32, (t, t), 0)
        cols = lax.broadcasted_iota(jnp.int32, (t, t), 1)
        kt = k_ref[...]
        vt = v_ref[...]

        ks = [kt[:, LANES * hh:LANES * (hh + 1)] for hh in range(2)]

        def tile(qis, carry, heads=(0, 1), diagonal=False):
            dk0, dk1, dv = carry
            offs = [pl.multiple_of(i * t, t) for i in qis]
            units = [(u, hh) for u in range(len(qis)) for hh in heads]
            qts = [q_ref[pl.ds(off, t), :] for off in offs]
            dos = [do_ref[pl.ds(off, t), :] for off in offs]
            lses = [lse_ref[0, :, pl.ds(off, t)] for off in offs]
            dls = [dl_ref[0, :, pl.ds(off, t)] for off in offs]
            qs = [qts[u][:, LANES * hh:LANES * (hh + 1)] for u, hh in units]
            doms = [jnp.where((lane < 64) if hh == 0 else (lane >= 64), dos[u], jnp.zeros_like(dos[u])) for u, hh in units]
            sts = [_dot(ks[hh], qs[n], NT) for n, (u, hh) in enumerate(units)]
            dpts = [_dot(vt, doms[n], NT) for n in range(len(units))]
            if diagonal:
                sts = [jnp.where(cols >= rows, st, NEG) for st in sts]
            pts = [jnp.exp(sts[n] - lses[u][hh:hh + 1, :]) for n, (u, hh) in enumerate(units)]
            dsts = [(pts[n] * (dpts[n] - dls[u][hh:hh + 1, :])).astype(BF16) for n, (u, hh) in enumerate(units)]
            for n, (u, hh) in enumerate(units):
                dv = dv + _dot(pts[n].astype(BF16), doms[n])
                term = _dot(dsts[n], qs[n])
                dk0, dk1 = (dk0 + term, dk1) if hh == 0 else (dk0, dk1 + term)
                dq_acc[hh, pl.ds(offs[u], t), :] += _dot(dsts[n], ks[hh], TN)
            return dk0, dk1, dv

        zero = jnp.zeros((t, LANES), F32)
        carry = tile([j], (zero, zero, zero), diagonal=True)
        n_rest = jnp.minimum(n_queries[0], n_queries[1]) - 1
        carry = lax.fori_loop(0, n_rest // 2, lambda u, cr: tile([j + 1 + 2 * u, j + 2 + 2 * u], cr), carry)
        carry = lax.fori_loop(0, n_rest % 2, lambda u, cr: tile([j + n_rest], cr), carry)
        for hh in range(2):
            carry = lax.fori_loop(j + 1 + n_rest, j + n_queries[hh], lambda i, cr, hh=hh: tile([i], cr, heads=(hh,)), carry)
        dk0, dk1, dv = carry
        e0, e1 = _aug_lane(0), _aug_lane(1)
        dk_ref[...] = jnp.where(lane < 64, dk0, dk1).astype(BF16)
        dv_ref[...] = dv.astype(BF16)
        c0 = jnp.broadcast_to(dk0[:, e0 + 3:e0 + 4], (t, LANES))
        c1 = jnp.broadcast_to(dk1[:, e1 + 3:e1 + 4], (t, LANES))
        dc_ref[0] = jnp.where(lane == 2 * p, -c0, jnp.where(lane == 2 * p + 1, -c1, 0.0))

        @pl.when(j == nt - 1)
        def _():
            lane_s = lax.broadcasted_iota(jnp.int32, (s, LANES), 1)
            a0, a1 = dq_acc[0], dq_acc[1]
            dq_ref[...] = (jnp.where(lane_s < 64, a0, a1) * SCALE).astype(BF16)
            r0 = jnp.broadcast_to(a0[:, e0:e0 + 1], (s, LANES))
            r1 = jnp.broadcast_to(a1[:, e1:e1 + 1], (s, LANES))
            dr_ref[0] = jnp.where(lane_s == 2 * p, r0, jnp.where(lane_s == 2 * p + 1, r1, 0.0))

    return pl.pallas_call(
        body, name="fox_bwd", grid=(4, nt),
        out_shape=[jax.ShapeDtypeStruct((s, 512), BF16), jax.ShapeDtypeStruct((s, 512), BF16),
                   jax.ShapeDtypeStruct((s, 512), BF16), jax.ShapeDtypeStruct((4, s, LANES), F32),
                   jax.ShapeDtypeStruct((4, s, LANES), F32)],
        in_specs=[pl.BlockSpec((1, SUBLANES, LANES), lambda p, j: (p, 0, 0)),
                  pl.BlockSpec((s, 2 * LANES), lambda p, j: (0, p)),
                  pl.BlockSpec((s, LANES), lambda p, j: (0, p)),
                  pl.BlockSpec((1, SUBLANES, s), lambda p, j: (p, 0, 0)),
                  pl.BlockSpec((1, SUBLANES, s), lambda p, j: (p, 0, 0)),
                  pl.BlockSpec((t, 2 * LANES), lambda p, j: (j, p)),
                  pl.BlockSpec((t, LANES), lambda p, j: (j, p))],
        out_specs=[pl.BlockSpec((s, LANES), lambda p, j: (0, p)),
                   pl.BlockSpec((t, LANES), lambda p, j: (j, p)),
                   pl.BlockSpec((t, LANES), lambda p, j: (j, p)),
                   pl.BlockSpec((1, t, LANES), lambda p, j: (p, j, 0)),
                   pl.BlockSpec((1, s, LANES), lambda p, j: (p, 0, 0))],
        scratch_shapes=[pltpu.VMEM((2, s, LANES), F32)],
        compiler_params=_params(("parallel", "arbitrary"), VMEM_LIMIT),
    )(ranges, qa, do, lse, delta, ka, vb)


def _forget_logit_grad(dc_ref, dr_ref, f, b_f, carry):
    tb = f.shape[0]
    lane = lax.broadcasted_iota(jnp.int32, (tb, LANES), 1)
    dc = dc_ref[0] + dr_ref[0]
    for k in range(1, 4):
        dc = dc + (dc_ref[k] + dr_ref[k])
    hi, mid, lo = _split3(dc)
    rows = lax.broadcasted_iota(jnp.int32, (tb, tb), 0)
    cols = lax.broadcasted_iota(jnp.int32, (tb, tb), 1)
    triu = (cols >= rows).astype(BF16)
    dlogf = _dot(triu, hi) + _dot(triu, mid) + _dot(triu, lo) + carry[0:1, :]
    carry[...] = jnp.broadcast_to(dlogf[0:1, :], carry.shape)
    return jnp.where(lane < N_HEADS, dlogf * _sigmoid(-(f + b_f)), 0.0)


def _dh_norm_bwd(d_a, d_q, d_k, d_v, dcum_k, dcum_q, f, bf_pad, d_g, w_t, x, dx2, gnorm, scale1, tm=512):
    s = x.shape[0]
    nt = s // tm

    def body(da_ref, dq_ref, dk_ref, dv_ref, dc_ref, dr_ref, f_ref, bf_ref, dg_ref, w_ref, x_ref, dx2_ref, g_ref, sc_ref,
             gx_ref, vec_ref, df_ref, db_ref, a_sh, a_sc, a_g, carry):
        step = pl.program_id(0)

        @pl.when(step == 0)
        def _():
            a_sh[...] = jnp.zeros_like(a_sh)
            a_sc[...] = jnp.zeros_like(a_sc)
            a_g[...] = jnp.zeros_like(a_g)
            carry[...] = jnp.zeros_like(carry)
            db_ref[...] = jnp.zeros_like(db_ref)

        def fold(v):
            return jnp.sum(v.reshape(tm // SUBLANES, SUBLANES, D_MODEL), axis=0)

        dfb = _forget_logit_grad(dc_ref, dr_ref, f_ref[...], bf_ref[...], carry)
        d_f = dfb.astype(BF16)
        df_ref[...] = d_f
        sub8 = lax.broadcasted_iota(jnp.int32, (SUBLANES, LANES), 0)
        db_ref[...] += jnp.where(sub8 == 0, jnp.sum(dfb, axis=0, keepdims=True), 0.0)
        d_all = jnp.concatenate([da_ref[...], dq_ref[...], dk_ref[...], dv_ref[...], d_f, dg_ref[...]], axis=1)
        dh = _dot(d_all, w_ref[...])
        xv = x_ref[...]
        r = lax.rsqrt(jnp.mean(xv * xv, axis=-1, keepdims=True) + NORM_EPS)
        xn = xv * r
        gn = g_ref[...]
        a_sh[...] += fold(dh)
        a_sc[...] += fold(dh * (xn * gn))
        dn1 = dh * sc_ref[...]
        a_g[...] += fold(dn1 * xn)
        dxn = dn1 * gn
        gx_ref[...] = dx2_ref[...] + r * (dxn - xn * jnp.mean(dxn * xn, axis=-1, keepdims=True))

        @pl.when(step == nt - 1)
        def _():
            sub = lax.broadcasted_iota(jnp.int32, (SUBLANES, D_MODEL), 0)
            v_sh = jnp.sum(a_sh[...], axis=0, keepdims=True)
            v_sc = jnp.sum(a_sc[...], axis=0, keepdims=True)
            v_g = jnp.sum(a_g[...], axis=0, keepdims=True)
            vec_ref[...] = jnp.where(sub == 0, v_sh, jnp.where(sub == 1, v_sc, jnp.where(sub == 2, v_g, 0.0)))

    row = lambda w: pl.BlockSpec((tm, w), lambda i: (nt - 1 - i, 0))
    slabs = pl.BlockSpec((4, tm, LANES), lambda i: (0, nt - 1 - i, 0))
    return pl.pallas_call(
        body, name="dh_norm_bwd", grid=(nt,),
        out_shape=[jax.ShapeDtypeStruct((s, D_MODEL), F32), jax.ShapeDtypeStruct((SUBLANES, D_MODEL), F32),
                   jax.ShapeDtypeStruct((s, LANES), BF16), jax.ShapeDtypeStruct((SUBLANES, LANES), F32)],
        in_specs=[row(W_A), row(512), row(512), row(512), slabs, slabs, row(W_F), _const_spec((1, LANES)), row(W_G),
                  _const_spec((W_INT, D_MODEL)), row(D_MODEL), row(D_MODEL), _const_spec((1, D_MODEL)),
                  _const_spec((1, D_MODEL))],
        out_specs=[row(D_MODEL), pl.BlockSpec((SUBLANES, D_MODEL), lambda i: (0, 0)), row(LANES),
                   pl.BlockSpec((SUBLANES, LANES), lambda i: (0, 0))],
        scratch_shapes=[pltpu.VMEM((SUBLANES, D_MODEL), F32)] * 3 + [pltpu.VMEM((SUBLANES, LANES), F32)],
        compiler_params=_params(("arbitrary",), VMEM_LIMIT),
    )(d_a, d_q, d_k, d_v, dcum_k, dcum_q, f, bf_pad, d_g, w_t, x, dx2, gnorm, scale1)


def _dw_in(h_t, d_a, d_q, d_k, d_v, d_f, d_g, ts=1024, tc=512):
    s = h_t.shape[1]
    ns = s // ts
    w_fg = 512 + W_F + W_G - 512
    rows = 128
    n_slot = (R_END // 4 + LANES - 1) // LANES * LANES
    order = [(0, 0, W_A), (4, 0, 512), (1, 0, 512), (2, 0, 512), (3, 0, 512), (4, 512, N_HEADS), (4, 512 + W_F, W_G - 512)]

    def slot_pieces(k):
        lo, hi, out, col = (R_END // 4) * k, (R_END // 4) * (k + 1), [], 0
        for acc_i, c0, w in order:
            a, b = max(lo, col), min(hi, col + w)
            if a < b:
                out.append((acc_i, c0 + a - col, b - a))
            col += w
        return out

    def body(h_ref, da_ref, dq_ref, dk_ref, dv_ref, df_ref, dg_ref, o_ref, acc_a, acc_q, acc_k, acc_v, acc_fg, stage, sem):
        k = pl.program_id(0)
        accs = (acc_a, acc_q, acc_k, acc_v, acc_fg)

        @pl.when(k == 0)
        def _():
            for acc in accs:
                acc[...] = jnp.zeros_like(acc)

        hv = h_ref[...]

        def add(acc, c_acc, d_ref, c_d, width):
            for c0 in range(0, width, tc):
                w = min(tc, width - c0)
                acc[:, c_acc + c0:c_acc + c0 + w] += _dot(hv, d_ref[:, c_d + c0:c_d + c0 + w])

        add(acc_a, 0, da_ref, 0, W_A)
        add(acc_q, 0, dq_ref, 0, 512)
        add(acc_k, 0, dk_ref, 0, 512)
        add(acc_v, 0, dv_ref, 0, 512)
        add(acc_fg, 0, dg_ref, 0, 512)
        add(acc_fg, 512, df_ref, 0, W_F)
        add(acc_fg, 512 + W_F, dg_ref, 512, W_G - 512)

        @pl.when(k == ns - 1)
        def _():
            pending = [None, None]
            for n, r0 in enumerate(range(0, D_MODEL, rows)):
                buf = n % 2
                if pending[buf] is not None:
                    for cp in pending[buf]:
                        cp.wait()
                rs = slice(r0, r0 + rows)
                copies = []
                for slot in range(4):
                    parts = [accs[acc_i][rs, c0:c0 + w] for acc_i, c0, w in slot_pieces(slot)]
                    parts.append(jnp.zeros((rows, n_slot - R_END // 4), F32))
                    stage[buf, slot] = jnp.concatenate(parts, axis=1)
                    cp = pltpu.make_async_copy(stage.at[buf, slot], o_ref.at[slot, pl.ds(r0, rows), :], sem.at[4 * buf + slot])
                    cp.start()
                    copies.append(cp)
                pending[buf] = copies
            for copies in pending:
                for cp in copies:
                    cp.wait()

    spec = lambda d: pl.BlockSpec((ts, d.shape[1]), lambda k: (k, 0))
    return pl.pallas_call(
        body, name="dw_in", grid=(ns,),
        out_shape=jax.ShapeDtypeStruct((4, D_MODEL, n_slot), F32),
        in_specs=[pl.BlockSpec((D_MODEL, ts), lambda k: (0, k))] + [spec(d) for d in (d_a, d_q, d_k, d_v, d_f, d_g)],
        out_specs=pl.BlockSpec(memory_space=pl.ANY),
        scratch_shapes=[pltpu.VMEM((D_MODEL, W_A), F32)] + [pltpu.VMEM((D_MODEL, 512), F32)] * 3
        + [pltpu.VMEM((D_MODEL, w_fg), F32), pltpu.VMEM((2, 4, rows, n_slot), F32), pltpu.SemaphoreType.DMA((8,))],
        compiler_params=_params(("arbitrary",), VMEM_LIMIT),
    )(h_t, d_a, d_q, d_k, d_v, d_f, d_g)


def _small_grads(packs, c_t, dada_shard):
    def body(p_ref, ct_ref, da_ref, sum_ref, gw_ref):
        acc = p_ref[0]
        for dev in range(1, 8):
            acc = acc + p_ref[dev]
        sum_ref[...] = acc
        gw_ref[...] = jnp.dot(ct_ref[...], da_ref[...], preferred_element_type=F32, precision=lax.Precision.HIGHEST)

    return pl.pallas_call(
        body, name="small_grads",
        out_shape=[jax.ShapeDtypeStruct(packs.shape[1:], F32),
                   jax.ShapeDtypeStruct((c_t.shape[0], dada_shard.shape[1]), F32)],
    )(packs, c_t, dada_shard)


def _adamw_body(w_ref, g_ref, m_ref, v_ref, d_ref, mo_ref, vo_ref):
    c1 = 1.0 / (1.0 - ADAM_B1 ** ADAM_STEP)
    c2 = 1.0 / (1.0 - ADAM_B2 ** ADAM_STEP)
    gv = g_ref[...]
    mn = ADAM_B1 * m_ref[...] + (1.0 - ADAM_B1) * gv
    vn = ADAM_B2 * v_ref[...] + (1.0 - ADAM_B2) * (gv * gv)
    mo_ref[...] = mn
    vo_ref[...] = vn
    d_ref[...] = -ADAM_LR * ((mn * c1) / (jnp.sqrt(vn * c2) + ADAM_EPS) + ADAM_WD * w_ref[...])


def _adamw3(w, g, m, v, name, tb=128):
    spec = pl.BlockSpec((tb, SUBLANES, LANES), lambda i: (i, 0, 0))
    return pl.pallas_call(
        functools.partial(_adamw_body), name=name, grid=(pl.cdiv(w.shape[0], tb),),
        out_shape=[jax.ShapeDtypeStruct(w.shape, F32)] * 3,
        in_specs=[spec] * 4, out_specs=[spec] * 3,
        compiler_params=_params(("parallel",)),
    )(w, g, m, v)


def _adamw(w, g, m, v, name):
    r, c = w.shape
    tr = 128 if r % 128 == 0 else r
    body = functools.partial(_adamw_body)
    spec = pl.BlockSpec((tr, c), lambda i: (i, 0))
    return pl.pallas_call(
        body, name=name, grid=(r // tr,),
        out_shape=[jax.ShapeDtypeStruct((r, c), F32)] * 3,
        in_specs=[spec] * 4, out_specs=[spec] * 3,
        compiler_params=_params(("parallel",)),
    )(w, g, m, v)


def _rope_inputs(positions):
    inv_freq = 10000.0 ** (-jnp.arange(0, HEAD_DIM, 2, dtype=F32) / HEAD_DIM)
    pos = jnp.broadcast_to(positions.astype(F32)[:, None], (positions.shape[0], LANES))
    return pos, jnp.tile(inv_freq, 4)[None, :]


def _pad_rows(v, rows=SUBLANES):
    return jnp.pad(v, ((0, rows - v.shape[0]), (0, 0)))


def kernel(x, c, positions, w_ada, b_ada, g_norm, w_in, b_f, sinks, w_o_swa, w_o_fox, w_out, g_final, loss_target, m_w_ada, m_b_ada, m_g_norm, m_w_in, m_b_f, m_sinks, m_w_o_swa, m_w_o_fox, m_w_out, m_g_final, v_w_ada, v_b_ada, v_g_norm, v_w_in, v_b_f, v_sinks, v_w_o_swa, v_w_o_fox, v_w_out, v_g_final):
    ix, iy, ic = lax.axis_index("x"), lax.axis_index("y"), lax.axis_index("c")
    chip = 2 * ix + iy
    dev = 2 * chip + ic
    xs, tgt = x[0], loss_target[0]
    s = xs.shape[0]

    b_ada_shard = lax.dynamic_slice(b_ada, (0, chip * 768), (1, 768))
    ada_parts, w_int, w_int_t, g_oa, g_ob, g_out = _gather_inputs(
        _pad_rows(c), w_ada[0], b_ada_shard, w_in[0], [w_o_swa[0], w_o_fox[0], w_out[0]], "gather_inputs")
    ada = lax.dynamic_index_in_dim(ada_parts, dev, axis=1, keepdims=False).reshape(1, 3 * D_MODEL)
    shift, scale, gate = ada[:, :D_MODEL], ada[:, D_MODEL:2 * D_MODEL], ada[:, 2 * D_MODEL:]
    scale1 = 1.0 + scale

    wo_a = jnp.transpose(g_oa, (1, 0, 2)).reshape(512, D_MODEL)
    wo_b = jnp.transpose(g_ob, (1, 0, 2)).reshape(512, D_MODEL)
    w_o = g_out.reshape(D_MODEL, D_MODEL)

    pos, freq = _rope_inputs(positions[0])
    bf_pad = jnp.pad(b_f, ((0, 0), (0, LANES - N_HEADS)))
    sink_vec = sinks[0]

    a, vb, f, g, h_t, cos, sin, qa, ka, va, stats = _norm_proj(
        xs, g_norm * scale1, shift, w_int, pos, freq, bf_pad, FOX_TILE)
    att_a, l_swa = _swa_fwd(a, sink_vec)
    ranges = _fox_tile_ranges(stats)
    att_b, lse = _fox_fwd(qa, ka, va, ranges, FOX_TILE)

    dx2, datt_a, datt_b, d_g, delta8, dwo_a, dwo_b, dw_out, vec_mid = _mid(
        att_a, att_b, g, xs, tgt, gate, g_final.reshape(1, D_MODEL), wo_a, wo_b, w_o)
    delta = jnp.pad(delta8.reshape(4, 2, s), ((0, 0), (0, SUBLANES - 2), (0, 0)))
    def slots(w, axis):
        if axis == 1:
            return jnp.transpose(w.reshape(w.shape[0], 4, w.shape[1] // 4), (1, 0, 2))
        return w.reshape(4, w.shape[0] // 4, w.shape[1])

    d_a, dsink, g_wo_a, g_wo_b, g_w_out = _swa_bwd(
        a, datt_a, l_swa, sink_vec, cos, sin, [slots(dwo_a, 1), slots(dwo_b, 1), slots(dw_out, 0)])
    dq, dk, dv, dcum_k, dcum_q = _fox_bwd(qa, ka, vb, datt_b, lse, delta, ranges, FOX_TILE)
    grad_x, vec_dh, d_f, dbf = _dh_norm_bwd(
        d_a, dq, dk, dv, dcum_k, dcum_q, f, bf_pad, d_g, w_int_t, xs, dx2, g_norm, scale1)
    dw_in_slots = _dw_in(h_t, d_a, dq, dk, dv, d_f, d_g)

    tail = jnp.pad(jnp.concatenate([dbf[0:1, :N_HEADS], dsink[0:1, :N_HEADS]], axis=1), ((0, 0), (0, D_MODEL - 2 * N_HEADS)))
    pack = jnp.concatenate([c, vec_dh[0:2], vec_mid[1:2], vec_dh[2:3], vec_mid[0:1], tail, vec_mid[2:3]], axis=0)

    packs, g_w_in = _reduce_scatter([dw_in_slots], pack, "reduce_grads")
    g_w_in = g_w_in[:, :w_in.shape[2]]
    dada_all = packs[:, 1:4, :].reshape(8, 3 * D_MODEL)
    dada_shard = lax.dynamic_slice(dada_all, (0, chip * 768), (8, 768))
    sums, g_w_ada = _small_grads(packs, packs[:, 0, :].T, dada_shard)
    g_b_ada = sums[1:4].reshape(1, 3 * D_MODEL)
    g_g_norm = sums[4:5]
    g_g_final = sums[5]
    g_b_f = sums[6:7, :N_HEADS]
    g_sinks = sums[6:7, N_HEADS:2 * N_HEADS]
    loss = sums[7, 0]

    grads = {
        "w_ada": g_w_ada, "b_ada": g_b_ada, "g_norm": g_g_norm, "w_in": g_w_in, "b_f": g_b_f, "sinks": g_sinks,
        "w_o_swa": g_wo_a, "w_o_fox": g_wo_b, "w_out": g_w_out, "g_final": g_g_final,
    }
    params = {
        "w_ada": (w_ada, m_w_ada, v_w_ada), "b_ada": (b_ada, m_b_ada, v_b_ada), "g_norm": (g_norm, m_g_norm, v_g_norm),
        "w_in": (w_in, m_w_in, v_w_in), "b_f": (b_f, m_b_f, v_b_f), "sinks": (sinks, m_sinks, v_sinks),
        "w_o_swa": (w_o_swa, m_w_o_swa, v_w_o_swa), "w_o_fox": (w_o_fox, m_w_o_fox, v_w_o_fox),
        "w_out": (w_out, m_w_out, v_w_out), "g_final": (g_final, m_g_final, v_g_final),
    }
    n_col = w_in.shape[2]

    def as_stored(t):
        return jnp.transpose(t, (2, 0, 1)).reshape(n_col, SUBLANES, LANES)

    def from_stored(t):
        return jnp.transpose(t, (1, 2, 0)).reshape(1, D_MODEL, n_col)

    names = list(grads)
    out_g, out_d, out_m, out_v = [], [], [], []
    for nm in names:
        w, m, v = params[nm]
        if nm == "w_in":
            g_st = as_stored(grads[nm][None])
            d_, m_, v_ = _adamw3(as_stored(w), g_st, as_stored(m), as_stored(v), "adamw_" + nm)
            res = [from_stored(t) for t in (g_st, d_, m_, v_)]
        else:
            shape2 = (w.shape[-2], w.shape[-1]) if w.ndim >= 2 else (1, w.shape[0])
            d_, m_, v_ = _adamw(w.reshape(shape2), grads[nm].reshape(shape2), m.reshape(shape2), v.reshape(shape2), "adamw_" + nm)
            res = [t.reshape(w.shape) for t in (grads[nm], d_, m_, v_)]
        out_g.append(res[0])
        out_d.append(res[1])
        out_m.append(res[2])
        out_v.append(res[3])
    return (loss, grad_x[None], *out_g, *out_d, *out_m, *out_v)
```

```python
import functools

import numpy as np
import jax
import jax.numpy as jnp
from jax import lax
from jax.experimental import pallas as pl
from jax.experimental.pallas import tpu as pltpu

F32 = jnp.float32
BF16 = jnp.bfloat16
MESH = pl.DeviceIdType.MESH

D_MODEL = 1024
HEAD_DIM = 64
N_HEADS = 8
WINDOW = 128
NORM_EPS = 1e-6
SCALE = HEAD_DIM ** -0.5
NEG = -1e30
LANES = 128
SUBLANES = 8
VMEM_LIMIT = 60 * 1024 * 1024
FOX_TILE = 512

W_A, W_B, W_F, W_G = 768, 1536, 128, 3072
OFF_A, OFF_B, OFF_F, OFF_G = 0, 768, 2304, 2432
W_INT = W_A + W_B + W_F + W_G
R_ZA, R_QB, R_FB, R_ZB, R_END = 768, 1280, 2816, 2824, 5384

ADAM_LR, ADAM_B1, ADAM_B2, ADAM_EPS, ADAM_WD, ADAM_STEP = 0.001, 0.9, 0.999, 1e-08, 0.01, 10

NT = (((1,), (1,)), ((), ()))
TN = (((0,), (0,)), ((), ()))


def _dot(a, b, dims=None):
    if dims is None:
        return jnp.dot(a, b, preferred_element_type=F32)
    return lax.dot_general(a, b, dims, preferred_element_type=F32)


def _split3(v):
    hi = v.astype(BF16)
    r1 = v - hi.astype(F32)
    mid = r1.astype(BF16)
    lo = (r1 - mid.astype(F32)).astype(BF16)
    return hi, mid, lo


def _sigmoid(v):
    return 1.0 / (1.0 + jnp.exp(-v))


def _params(sem=None, vmem=None):
    return pltpu.CompilerParams(dimension_semantics=sem, vmem_limit_bytes=vmem)


def _const_spec(shape):
    nd = len(shape)
    return pl.BlockSpec(shape, lambda *_: (0,) * nd, pipeline_mode=pl.Buffered(1))


def _flip(v, f):
    return 1 - v if f else v


_CHIP_FLIPS = ((1, 0), (0, 1), (1, 1))


def _gather_inputs(c_pad, w_ada, b_ada_shard, w_in_shard, small_shards, name):
    shards = [w_in_shard] + list(small_shards)
    n = len(shards)
    n_col = w_ada.shape[1]
    shard_w = w_in_shard.shape[0] // SUBLANES
    rows = 128

    def body(*refs):
        c_ref, wa_ref, ba_ref = refs[:3]
        ins = refs[3:3 + n]
        ada_ref, wint_ref, wintt_ref = refs[3 + n:6 + n]
        g_in, call_ref, send_sems, recv_sems = refs[5 + 2 * n:9 + 2 * n]
        outs = (g_in,) + tuple(refs[6 + n:5 + 2 * n])
        x, y, c = lax.axis_index("x"), lax.axis_index("y"), lax.axis_index("c")
        k_me = 2 * x + y
        me = 2 * k_me + c
        sibling = (x, y, 1 - c)
        chips = [(_flip(x, fx), _flip(y, fy)) for fx, fy in _CHIP_FLIPS]

        def piece(i, chip_k, half):
            hr = outs[i].shape[1] // 2
            return outs[i].at[chip_k, pl.ds(half * hr, hr), :]

        def copy(i, slot, chip_k, half, to):
            return pltpu.make_async_remote_copy(
                src_ref=piece(i, chip_k, half), dst_ref=piece(i, chip_k, half),
                send_sem=send_sems.at[6 * i + slot], recv_sem=recv_sems.at[6 * i + slot],
                device_id=to, device_id_type=MESH)

        def small(ref, slot, sem, to):
            return pltpu.make_async_remote_copy(
                src_ref=ref.at[slot], dst_ref=ref.at[slot], send_sem=send_sems.at[6 * n + sem],
                recv_sem=recv_sems.at[6 * n + sem], device_id=to, device_id_type=MESH)

        whole = shard_w // LANES * LANES
        for a in range(SUBLANES):
            main = ins[0][pl.ds(a, whole, stride=SUBLANES), :]
            tail = ins[0][pl.ds(a + SUBLANES * whole, shard_w - whole, stride=SUBLANES), :]
            tail = jnp.concatenate([tail, jnp.zeros((LANES - (shard_w - whole), LANES), F32)], axis=0)
            blk = jnp.concatenate([main.T, tail.T[:, :shard_w - whole]], axis=1)
            g_in[k_me, LANES * a:LANES * (a + 1), :] = blk.astype(BF16)
        for i in range(1, n):
            outs[i][k_me] = ins[i][...].astype(BF16)
        started = []
        for i in range(n):
            for j, chip in enumerate(chips):
                cp = copy(i, j, k_me, c, (chip[0], chip[1], c))
                cp.start()
                started.append(cp)

        call_ref[me] = c_ref[...]
        peers = [(_flip(x, k & 4), _flip(y, k & 2), _flip(c, k & 1)) for k in range(1, 8)]
        for k, peer in enumerate(peers):
            cp = small(call_ref, me, k, peer)
            cp.start()
            started.append(cp)
        for k, peer in enumerate(peers):
            small(call_ref, 4 * peer[0] + 2 * peer[1] + peer[2], k, peer).wait_recv()
        c_all = call_ref[:, 0, :].astype(BF16)
        ada_ref[k_me] = _dot(c_all, wa_ref[...].astype(BF16)) + ba_ref[...]
        for j, chip in enumerate(chips):
            cp = small(ada_ref, k_me, 7 + j, (chip[0], chip[1], c))
            cp.start()
            started.append(cp)

        for j, chip in enumerate(chips):
            chip_k = 2 * chip[0] + chip[1]
            for i in range(n):
                copy(i, j, chip_k, c, (chip[0], chip[1], c)).wait_recv()
                cp = copy(i, 3 + j, chip_k, c, sibling)
                cp.start()
                started.append(cp)
        for j, chip in enumerate(chips):
            chip_k = 2 * chip[0] + chip[1]
            small(ada_ref, chip_k, 7 + j, (chip[0], chip[1], c)).wait_recv()
            for i in range(n):
                copy(i, 3 + j, chip_k, 1 - c, sibling).wait_recv()
        for cp in started:
            cp.wait_send()

        def ref_cols(slots, a, b):
            runs = []
            for k in range(4):
                lo, hi = max(a, shard_w * k), min(b, shard_w * (k + 1))
                if lo < hi:
                    runs.append(slots[k][:, lo - shard_w * k:hi - shard_w * k])
            return runs

        for r0 in range(0, D_MODEL, rows):
            rs = slice(r0, r0 + rows)
            slots = [g_in[k, rs, :] for k in range(4)]
            row = jnp.concatenate(
                ref_cols(slots, 0, R_ZA) + ref_cols(slots, R_QB, R_FB) + ref_cols(slots, R_FB, R_ZB)
                + [jnp.zeros((rows, W_F - N_HEADS), BF16)] + ref_cols(slots, R_ZA, R_QB) + ref_cols(slots, R_ZB, R_END),
                axis=1)
            wint_ref[rs, :] = row
            wintt_ref[:, rs] = row.T

    vmem = pl.BlockSpec(memory_space=pltpu.VMEM)
    return pl.pallas_call(
        body, name=name,
        out_shape=[jax.ShapeDtypeStruct((4, 8, n_col), F32), jax.ShapeDtypeStruct((D_MODEL, W_INT), BF16),
                   jax.ShapeDtypeStruct((W_INT, D_MODEL), BF16)]
        + [jax.ShapeDtypeStruct((4,) + s.shape, BF16) for s in small_shards],
        in_specs=[vmem] * (3 + n),
        out_specs=[vmem] * (2 + n),
        scratch_shapes=[pltpu.VMEM((4, D_MODEL, shard_w), BF16), pltpu.VMEM((8,) + c_pad.shape, F32),
                        pltpu.SemaphoreType.DMA((6 * n + 10,)), pltpu.SemaphoreType.DMA((6 * n + 10,))],
        compiler_params=_params(vmem=VMEM_LIMIT),
    )(c_pad, w_ada, b_ada_shard, *shards)


def _reduce_scatter(pieces, pack, name):
    n = len(pieces)

    def body(*refs):
        pack_ref, ins = refs[0], refs[1:1 + n]
        packs_ref, outs = refs[1 + n], refs[2 + n:2 + 2 * n]
        rest = refs[2 + 2 * n:]
        own, got = rest[:n], rest[n:2 * n]
        sendb, recvb = rest[2 * n:3 * n], rest[3 * n:4 * n]
        send_sems, recv_sems, local_sems = rest[4 * n:4 * n + 3]
        x, y, c = lax.axis_index("x"), lax.axis_index("y"), lax.axis_index("c")
        k_me = 2 * x + y
        me = 2 * k_me + c
        sibling = (x, y, 1 - c)
        chips = [(_flip(x, fx), _flip(y, fy)) for fx, fy in _CHIP_FLIPS]
        hrs = [p.shape[1] // 2 for p in pieces]

        def remote(i, slot, src, dst, to):
            return pltpu.make_async_remote_copy(
                src_ref=src, dst_ref=dst, send_sem=send_sems.at[5 * i + slot], recv_sem=recv_sems.at[5 * i + slot],
                device_id=to, device_id_type=MESH)

        started = []
        packs_ref[me] = pack_ref[...]
        peers = [(_flip(x, k & 4), _flip(y, k & 2), _flip(c, k & 1)) for k in range(1, 8)]
        for k, peer in enumerate(peers):
            cp = pltpu.make_async_remote_copy(
                src_ref=pack_ref, dst_ref=packs_ref.at[me], send_sem=send_sems.at[5 * n + k],
                recv_sem=recv_sems.at[5 * n + k], device_id=peer, device_id_type=MESH)
            cp.start()
            started.append(cp)
        loads = []
        for i in range(n):
            ld = pltpu.make_async_copy(ins[i].at[:, pl.ds(c * hrs[i], hrs[i]), :], own[i], local_sems.at[i])
            ld.start()
            loads.append(ld)
            cp = remote(i, 0, ins[i].at[:, pl.ds((1 - c) * hrs[i], hrs[i]), :], got[i], sibling)
            cp.start()
            started.append(cp)
        for i in range(n):
            loads[i].wait()
            remote(i, 0, ins[i].at[:, pl.ds(c * hrs[i], hrs[i]), :], got[i], sibling).wait_recv()
            for j, chip in enumerate(chips):
                chip_k = 2 * chip[0] + chip[1]
                sendb[i][j] = (own[i][chip_k] + got[i][chip_k]).astype(BF16)
                cp = remote(i, 1 + j, sendb[i].at[j], recvb[i].at[j], (chip[0], chip[1], c))
                cp.start()
                started.append(cp)
        for i in range(n):
            acc = own[i][k_me] + got[i][k_me]
            for j, chip in enumerate(chips):
                remote(i, 1 + j, sendb[i].at[j], recvb[i].at[j], (chip[0], chip[1], c)).wait_recv()
                acc = acc + recvb[i][j].astype(F32)
            mine = outs[i].at[pl.ds(c * hrs[i], hrs[i]), :]
            outs[i][pl.ds(pl.multiple_of(c * hrs[i], SUBLANES), hrs[i]), :] = acc
            cp = remote(i, 4, mine, mine, sibling)
            cp.start()
            started.append(cp)
        for i in range(n):
            theirs = outs[i].at[pl.ds((1 - c) * hrs[i], hrs[i]), :]
            remote(i, 4, theirs, theirs, sibling).wait_recv()
        for k, peer in enumerate(peers):
            pltpu.make_async_remote_copy(
                src_ref=pack_ref, dst_ref=packs_ref.at[4 * peer[0] + 2 * peer[1] + peer[2]],
                send_sem=send_sems.at[5 * n + k], recv_sem=recv_sems.at[5 * n + k],
                device_id=peer, device_id_type=MESH).wait_recv()
        for cp in started:
            cp.wait_send()

    vmem = pl.BlockSpec(memory_space=pltpu.VMEM)
    scratch = []
    scratch += [pltpu.VMEM((4, p.shape[1] // 2, p.shape[2]), F32) for p in pieces]
    scratch += [pltpu.VMEM((4, p.shape[1] // 2, p.shape[2]), F32) for p in pieces]
    scratch += [pltpu.VMEM((3, p.shape[1] // 2, p.shape[2]), BF16) for p in pieces]
    scratch += [pltpu.VMEM((3, p.shape[1] // 2, p.shape[2]), BF16) for p in pieces]
    scratch += [pltpu.SemaphoreType.DMA((5 * n + 7,)), pltpu.SemaphoreType.DMA((5 * n + 7,)), pltpu.SemaphoreType.DMA((n,))]
    return pl.pallas_call(
        body, name=name,
        out_shape=[jax.ShapeDtypeStruct((8,) + pack.shape, F32)] + [jax.ShapeDtypeStruct(p.shape[1:], F32) for p in pieces],
        in_specs=[vmem] + [pl.BlockSpec(memory_space=pl.ANY)] * n,
        out_specs=[vmem] * (1 + n),
        scratch_shapes=scratch,
        compiler_params=_params(vmem=VMEM_LIMIT),
    )(pack, *pieces)


def _rope_fwd(t, cos, sin, lane):
    lo = (lane % HEAD_DIM) < (HEAD_DIM // 2)
    return t * cos + jnp.where(lo, -pltpu.roll(t, 96, 1), pltpu.roll(t, 32, 1)) * sin


def _norm_proj(x, gmod, shift, w_int, pos, freq, bf_pad, tm):
    s = x.shape[0]

    def body(x_ref, g_ref, sh_ref, w_ref, pos_ref, fr_ref, bf_ref,
             a_ref, vb_ref, f_ref, gg_ref, ht_ref, cos_ref, sin_ref, q_ref, k_ref, v_ref, st_ref, carry):
        @pl.when(pl.program_id(0) == 0)
        def _():
            carry[...] = jnp.zeros_like(carry)

        xv = x_ref[...]
        r = lax.rsqrt(jnp.mean(xv * xv, axis=-1, keepdims=True) + NORM_EPS)
        hf = (xv * r) * g_ref[...] + sh_ref[...]
        hb = hf.astype(BF16)
        ht_ref[...] = hb.T
        pa = _dot(hb, w_ref[:, OFF_A:OFF_A + W_A])
        ang = pos_ref[...] * fr_ref[...]
        cosv, sinv = jnp.cos(ang), jnp.sin(ang)
        cos_ref[...] = cosv
        sin_ref[...] = sinv
        lane = lax.broadcasted_iota(jnp.int32, (tm, LANES), 1)
        for j in range(5):
            t = pa[:, LANES * j:LANES * (j + 1)]
            a_ref[:, LANES * j:LANES * (j + 1)] = _rope_fwd(t, cosv, sinv, lane).astype(BF16)
        a_ref[:, 640:768] = pa[:, 640:768].astype(BF16)
        pf = _dot(hb, w_ref[:, OFF_F:OFF_F + W_F])
        f_ref[...] = pf
        bblk = _dot(hb, w_ref[:, OFF_B:OFF_B + W_B]).astype(BF16)
        vb_ref[...] = bblk[:, 1024:1536]
        gg_ref[...] = _dot(hb, w_ref[:, OFF_G:OFF_G + W_G]).astype(BF16)
        _augment_heads(bblk, _cumsum_tile(pf, bf_ref[...], carry), q_ref, k_ref, v_ref, st_ref)

    row = lambda w: pl.BlockSpec((tm, w), lambda i: (i, 0))
    return pl.pallas_call(
        body, name="norm_proj", grid=(s // tm,),
        out_shape=[jax.ShapeDtypeStruct((s, W_A), BF16), jax.ShapeDtypeStruct((s, 512), BF16),
                   jax.ShapeDtypeStruct((s, W_F), F32), jax.ShapeDtypeStruct((s, W_G), BF16),
                   jax.ShapeDtypeStruct((D_MODEL, s), BF16),
                   jax.ShapeDtypeStruct((s, LANES), F32), jax.ShapeDtypeStruct((s, LANES), F32)]
        + [jax.ShapeDtypeStruct((s, 1024), BF16)] * 3 + [jax.ShapeDtypeStruct((s // tm, SUBLANES, LANES), F32)],
        in_specs=[row(D_MODEL), _const_spec((1, D_MODEL)), _const_spec((1, D_MODEL)), _const_spec((D_MODEL, W_INT)),
                  row(LANES), _const_spec((1, LANES)), _const_spec((1, LANES))],
        out_specs=[row(W_A), row(512), row(W_F), row(W_G), pl.BlockSpec((D_MODEL, tm), lambda i: (0, i)),
                   row(LANES), row(LANES), row(1024), row(1024), row(1024),
                   pl.BlockSpec((1, SUBLANES, LANES), lambda i: (i, 0, 0))],
        scratch_shapes=[pltpu.VMEM((SUBLANES, LANES), F32)],
        compiler_params=_params(("arbitrary",), VMEM_LIMIT),
    )(x, gmod, shift, w_int, pos, freq, bf_pad)


def _log_sigmoid(u):
    return jnp.minimum(u, 0.0) - jnp.log(1.0 + jnp.exp(-jnp.abs(u)))


def _cumsum_tile(f, b_f, carry):
    tb = f.shape[0]
    lane = lax.broadcasted_iota(jnp.int32, (tb, LANES), 1)
    logf = jnp.where(lane < N_HEADS, _log_sigmoid(f + b_f), 0.0)
    hi, mid, lo = _split3(logf)
    rows = lax.broadcasted_iota(jnp.int32, (tb, tb), 0)
    cols = lax.broadcasted_iota(jnp.int32, (tb, tb), 1)
    tril = (cols <= rows).astype(BF16)
    cum = _dot(tril, hi) + _dot(tril, mid) + _dot(tril, lo) + carry[0:1, :]
    carry[...] = jnp.broadcast_to(cum[tb - 1:tb, :], carry.shape)
    return cum


def _aug_lane(h):
    return 64 if h % 2 == 0 else 0


def _augment_heads(bblk, cumv, q_ref, k_ref, v_ref, st_ref):
    t = bblk.shape[0]
    lane = lax.broadcasted_iota(jnp.int32, (t, LANES), 1)
    lane_b = lane.astype(BF16)
    sub8 = lax.broadcasted_iota(jnp.int32, (SUBLANES, LANES), 0)
    lane8 = lax.broadcasted_iota(jnp.int32, (SUBLANES, LANES), 1)
    one = jnp.ones((t, LANES), BF16)
    zero = jnp.zeros((t, LANES), BF16)
    stats = jnp.zeros((SUBLANES, LANES), F32)
    for p in range(4):
        qblk = bblk[:, LANES * p:LANES * (p + 1)] * SCALE
        kblk = bblk[:, 512 + LANES * p:512 + LANES * (p + 1)]
        vblk = bblk[:, 1024 + LANES * p:1024 + LANES * (p + 1)]
        qf, kf = qblk.astype(F32), kblk.astype(F32)
        q2, k2, qk = qf * qf, kf * kf, qf * kf
        for odd in range(2):
            h = 2 * p + odd
            a0 = _aug_lane(h)
            data_b = (lane_b < 64) if odd == 0 else (lane_b >= 64)
            data = (lane < 64) if odd == 0 else (lane >= 64)
            hi, mid, lo = _split3(jnp.broadcast_to(cumv[:, h:h + 1], (t, LANES)))
            ones3_q = (lane_b >= a0 + 3) & (lane_b < a0 + 6)
            ones3_k = (lane_b >= a0) & (lane_b < a0 + 3)
            aug_q = jnp.where(lane_b == a0, hi, jnp.where(lane_b == a0 + 1, mid, jnp.where(
                lane_b == a0 + 2, lo, jnp.where(ones3_q, one, zero))))
            aug_k = jnp.where(ones3_k, one, jnp.where(lane_b == a0 + 3, -hi, jnp.where(
                lane_b == a0 + 4, -mid, jnp.where(lane_b == a0 + 5, -lo, zero))))
            q_ref[:, LANES * h:LANES * (h + 1)] = jnp.where(data_b, qblk, aug_q)
            k_ref[:, LANES * h:LANES * (h + 1)] = jnp.where(data_b, kblk, aug_k)
            v_ref[:, LANES * h:LANES * (h + 1)] = jnp.where(data_b, vblk, jnp.where(lane_b == a0, one, zero))
            qn = jnp.sqrt(jnp.max(jnp.sum(jnp.where(data, q2, 0.0), axis=-1, keepdims=True)))
            kn = jnp.sqrt(jnp.max(jnp.sum(jnp.where(data, k2, 0.0), axis=-1, keepdims=True)))
            dmin = jnp.min(jnp.sum(jnp.where(data, qk, 0.0), axis=-1, keepdims=True))
            c_first, c_last = cumv[0:1, h:h + 1], cumv[t - 1:t, h:h + 1]
            row = jnp.where(lane8 == 0, qn, jnp.where(lane8 == 1, kn, jnp.where(
                lane8 == 2, c_first, jnp.where(lane8 == 3, c_last, jnp.where(lane8 == 4, dmin, 0.0)))))
            stats = jnp.where(sub8 == h, row, stats)
    st_ref[0] = stats


PRUNE_MARGIN = 88.0


def _fox_tile_ranges(stats):
    nt = stats.shape[0]
    qn, kn, c_first, c_last, d_min = (stats[:, :, n] for n in range(5))
    bound = (1.01 * qn[:, None, :] * kn[None, :, :] - jnp.minimum(d_min, 0.0)[:, None, :] + 0.05
             + c_first[:, None, :] - c_last[None, :, :])
    idx = jnp.arange(nt)
    skip = (bound <= -PRUNE_MARGIN) & (idx[None, :, None] < idx[:, None, None])
    first_key = jnp.sum(jnp.cumprod(skip, axis=1), axis=1)
    needed = (idx[None, :, None] >= first_key[:, None, :]) & (idx[None, :, None] <= idx[:, None, None])
    last_query = jnp.max(jnp.where(needed, idx[:, None, None], 0), axis=0)
    n_query = last_query - idx[:, None] + 1
    table = jnp.zeros((4, SUBLANES, LANES), F32)
    for odd in range(2):
        table = table.at[:, odd, :nt].set(first_key[:, odd::2].T.astype(F32))
        table = table.at[:, 2 + odd, :nt].set(n_query[:, odd::2].T.astype(F32))
    return table


def _lane_scalar(block, row, lane_idx):
    sub8 = lax.broadcasted_iota(jnp.int32, (SUBLANES, LANES), 0)
    lane8 = lax.broadcasted_iota(jnp.int32, (SUBLANES, LANES), 1)
    return jnp.sum(jnp.where((sub8 == row) & (lane8 == lane_idx), block, 0.0)).astype(jnp.int32)


def _fox_fwd(qa, ka, va, ranges, t):
    s = qa.shape[0]
    nt = s // t
    nc = t // LANES

    def body(rg_ref, q_ref, k_ref, v_ref, o_ref, lse_ref):
        i = pl.program_id(1)
        lane = lax.broadcasted_iota(jnp.int32, (t, LANES), 1)
        rows = lax.broadcasted_iota(jnp.int32, (t, t), 0)
        cols = lax.broadcasted_iota(jnp.int32, (t, t), 1)
        firsts = [jnp.clip(_lane_scalar(rg_ref[0], hh, i), 0, i) for hh in range(2)]
        first = jnp.maximum(firsts[0], firsts[1])

        def update(js, carry, heads=(0, 1), diagonal=False):
            offs = [pl.multiple_of(j * t, t) for j in js]
            kts = [k_ref[pl.ds(off, t), :] for off in offs]
            vts = [v_ref[pl.ds(off, t), :] for off in offs]
            scs = {hh: [_dot(q_ref[:, LANES * hh:LANES * (hh + 1)], kt[:, LANES * hh:LANES * (hh + 1)], NT) for kt in kts]
                   for hh in heads}
            if diagonal:
                scs = {hh: [jnp.where(cols <= rows, sc, NEG) for sc in scs[hh]] for hh in heads}
            m_new = {}
            for hh in heads:
                part = None
                for sc in scs[hh]:
                    for cch in range(nc):
                        chunk = sc[:, LANES * cch:LANES * (cch + 1)]
                        part = chunk if part is None else jnp.maximum(part, chunk)
                m_new[hh] = jnp.maximum(carry[2 * hh], jnp.max(part, axis=-1, keepdims=True))
            alphas = {hh: jnp.exp(carry[2 * hh] - m_new[hh]) for hh in heads}
            ps = {hh: [jnp.exp(sc - m_new[hh]).astype(BF16) for sc in scs[hh]] for hh in heads}
            out = list(carry)
            for hh in heads:
                pv = None
                for p, vt in zip(ps[hh], vts):
                    term = _dot(p, vt[:, LANES * hh:LANES * (hh + 1)])
                    pv = term if pv is None else pv + term
                out[2 * hh], out[2 * hh + 1] = m_new[hh], alphas[hh] * carry[2 * hh + 1] + pv
            return tuple(out)

        col0 = jnp.full((t, 1), NEG, F32)
        zero = jnp.zeros((t, LANES), F32)
        carry = (col0, zero, col0, zero)
        for hh in range(2):
            carry = lax.fori_loop(firsts[hh], first, lambda j, cr, hh=hh: update([j], cr, heads=(hh,)), carry)
        n_off = i - first
        carry = lax.fori_loop(0, n_off // 2, lambda u, cr: update([first + 2 * u, first + 2 * u + 1], cr), carry)
        carry = lax.fori_loop(0, n_off % 2, lambda u, cr: update([i - 1], cr), carry)
        m0, acc0, m1, acc1 = update([i], carry, diagonal=True)
        l0, l1 = acc0[:, _aug_lane(0):_aug_lane(0) + 1], acc1[:, _aug_lane(1):_aug_lane(1) + 1]
        o_ref[...] = jnp.where(lane < 64, acc0 * (1.0 / l0), acc1 * (1.0 / l1)).astype(BF16)
        sub = lax.broadcasted_iota(jnp.int32, (SUBLANES, t), 0)
        lse0 = jnp.broadcast_to(m0 + jnp.log(l0), (t, LANES)).T[0:SUBLANES, :]
        lse1 = jnp.broadcast_to(m1 + jnp.log(l1), (t, LANES)).T[0:SUBLANES, :]
        lse_ref[0] = jnp.where(sub == 0, lse0, jnp.where(sub == 1, lse1, 0.0))

    pair = pl.BlockSpec((s, 2 * LANES), lambda p, i: (0, p))
    return pl.pallas_call(
        body, name="fox_fwd", grid=(4, nt),
        out_shape=[jax.ShapeDtypeStruct((s, 512), BF16), jax.ShapeDtypeStruct((4, SUBLANES, s), F32)],
        in_specs=[pl.BlockSpec((1, SUBLANES, LANES), lambda p, i: (p, 0, 0)),
                  pl.BlockSpec((t, 2 * LANES), lambda p, i: (i, p)), pair, pair],
        out_specs=[pl.BlockSpec((t, LANES), lambda p, i: (i, p)),
                   pl.BlockSpec((1, SUBLANES, t), lambda p, i: (p, 0, i))],
        compiler_params=_params(("parallel", "arbitrary"), VMEM_LIMIT),
    )(ranges, qa, ka, va)


def _dup_halves(blk, lane):
    f = blk.astype(F32)
    r = pltpu.roll(f, 64, 1)
    return jnp.where(lane < 64, f, r).astype(BF16), jnp.where(lane >= 64, f, r).astype(BF16)


GROUP = 4
GROUP_ROWS = GROUP * WINDOW


def _stack_heads(ref, g, lane):
    parts = []
    for pb in (2 * g, 2 * g + 1):
        blk = ref[:, LANES * pb:LANES * (pb + 1)]
        zero = jnp.zeros_like(blk)
        parts += [jnp.where(lane < 64, blk, zero), jnp.where(lane >= 64, blk, zero)]
    return jnp.concatenate(parts, axis=0)


def _swa_band(a_ref, ap_ref, g, lane):
    k = jnp.concatenate([_dup_halves(ap_ref[:, 512:640], lane)[g], _dup_halves(a_ref[:, 512:640], lane)[g]], axis=0)
    v = jnp.concatenate([_dup_halves(ap_ref[:, 640:768], lane)[g], _dup_halves(a_ref[:, 640:768], lane)[g]], axis=0)
    return k, v


def _swa_logits(q, k, has_prev):
    sc = _dot(q, k, NT) * SCALE
    rr = lax.broadcasted_iota(jnp.int32, sc.shape, 0) % WINDOW
    cc = lax.broadcasted_iota(jnp.int32, sc.shape, 1)
    valid = (cc > rr) & (cc <= rr + WINDOW) & (has_prev | (cc >= WINDOW))
    return jnp.where(valid, sc, NEG)


def _per_head_column(values):
    return jnp.concatenate([jnp.broadcast_to(v, (WINDOW, 1)) for v in values], axis=0)


SWA_BLOCKS = 4
SWA_ROWS = SWA_BLOCKS * WINDOW


def _swa_blocks(a_ref, ap_ref):
    return [ap_ref] + [a_ref.at[pl.ds(WINDOW * jb, WINDOW), :] for jb in range(SWA_BLOCKS)]


def _swa_fwd(a, sinks):
    s = a.shape[0]

    def body(sink_ref, a_ref, ap_ref, o_ref, l_ref):
        lane = lax.broadcasted_iota(jnp.int32, (WINDOW, LANES), 1)
        blocks = _swa_blocks(a_ref, ap_ref)
        units = [(jb, g) for jb in range(SWA_BLOCKS) for g in range(2)]
        sinks_col = [_per_head_column([sink_ref[GROUP * g + hh] for hh in range(GROUP)]) for g in range(2)]
        bands = [_swa_band(blocks[jb + 1], blocks[jb], g, lane) for jb, g in units]
        scs = [_swa_logits(_stack_heads(blocks[jb + 1], g, lane), bands[u][0],
                           (pl.program_id(0) > 0) if jb == 0 else True) for u, (jb, g) in enumerate(units)]
        ms = [jnp.maximum(jnp.max(scs[u], axis=-1, keepdims=True), sinks_col[g]) for u, (jb, g) in enumerate(units)]
        ps = [jnp.exp(scs[u] - ms[u]) for u in range(len(units))]
        dens = [jnp.sum(ps[u], axis=-1, keepdims=True) + jnp.exp(sinks_col[g] - ms[u]) for u, (jb, g) in enumerate(units)]
        outs = [_dot((ps[u] * (1.0 / dens[u])).astype(BF16), bands[u][1]) for u in range(len(units))]
        for jb in range(SWA_BLOCKS):
            rows = slice(WINDOW * jb, WINDOW * (jb + 1))
            l_all = jnp.zeros((WINDOW, LANES), F32)
            for g in range(2):
                u = 2 * jb + g
                lcol = ms[u] + jnp.log(dens[u])
                for pb in range(2):
                    r0 = 2 * pb * WINDOW
                    o_ref[rows, LANES * (2 * g + pb):LANES * (2 * g + pb + 1)] = jnp.where(
                        lane < 64, outs[u][r0:r0 + WINDOW], outs[u][r0 + WINDOW:r0 + 2 * WINDOW]).astype(BF16)
                for hh in range(GROUP):
                    l_all = jnp.where(lane == GROUP * g + hh, lcol[WINDOW * hh:WINDOW * (hh + 1)], l_all)
            l_ref[rows, :] = l_all

    return pl.pallas_call(
        body, name="swa_fwd", grid=(s // SWA_ROWS,),
        out_shape=[jax.ShapeDtypeStruct((s, 512), BF16), jax.ShapeDtypeStruct((s, LANES), F32)],
        in_specs=[pl.BlockSpec(memory_space=pltpu.SMEM),
                  pl.BlockSpec((SWA_ROWS, W_A), lambda i: (i, 0)),
                  pl.BlockSpec((WINDOW, W_A), lambda i: (jnp.maximum(SWA_BLOCKS * i - 1, 0), 0))],
        out_specs=[pl.BlockSpec((SWA_ROWS, 512), lambda i: (i, 0)), pl.BlockSpec((SWA_ROWS, LANES), lambda i: (i, 0))],
        compiler_params=_params(("parallel",)),
    )(sinks, a, a)


def _mid(att_a, att_b, g, x, target, gate, g_final, wo_a, wo_b, w_out, tm=256):
    s = x.shape[0]
    nt = s // tm

    def body(aa_ref, ab_ref, g_ref, x_ref, t_ref, gate_ref, gf_ref, woa_ref, wob_ref, wout_ref,
             dx_ref, daa_ref, dab_ref, dg_ref, delta_ref, dwoa_ref, dwob_ref, dwout_ref, vec_ref,
             acc_gf, acc_gate, acc_loss):
        step = pl.program_id(0)

        @pl.when(step == 0)
        def _():
            dwoa_ref[...] = jnp.zeros_like(dwoa_ref)
            dwob_ref[...] = jnp.zeros_like(dwob_ref)
            dwout_ref[...] = jnp.zeros_like(dwout_ref)
            acc_gf[...] = jnp.zeros_like(acc_gf)
            acc_gate[...] = jnp.zeros_like(acc_gate)
            acc_loss[...] = jnp.zeros_like(acc_loss)

        def fold(v):
            return jnp.sum(v.reshape(tm // SUBLANES, SUBLANES, D_MODEL), axis=0)

        gate = gate_ref[...]
        gfin = gf_ref[...]
        branches = []
        for att_ref, z_off, wo_ref in ((aa_ref, 0, woa_ref), (ab_ref, 512, wob_ref)):
            att = att_ref[...].astype(F32)
            z = g_ref[:, z_off:z_off + 512].astype(F32)
            sz = _sigmoid(z)
            silu = z * sz
            u = (att * silu).astype(BF16)
            branches.append((att, z, sz, silu, u, _dot(u, wo_ref[...])))
        ga = g_ref[:, 1024:2048].astype(F32)
        gb = g_ref[:, 2048:3072].astype(F32)
        sga, sgb = _sigmoid(ga), _sigmoid(gb)
        y_a, y_b = branches[0][5], branches[1][5]
        mb = (sga * y_a + sgb * y_b).astype(BF16)
        o = _dot(mb, wout_ref[...])
        x2 = x_ref[...] + gate * o
        r2 = lax.rsqrt(jnp.mean(x2 * x2, axis=-1, keepdims=True) + NORM_EPS)
        xn2 = x2 * r2
        err = xn2 * gfin - t_ref[...]
        acc_loss[...] += fold(err * err)
        dy = err * (1.0 / D_MODEL)
        acc_gf[...] += fold(dy * xn2)
        dxn = dy * gfin
        dx2 = r2 * (dxn - xn2 * jnp.mean(dxn * xn2, axis=-1, keepdims=True))
        dx_ref[...] = dx2
        acc_gate[...] += fold(dx2 * o)
        d_o = (dx2 * gate).astype(BF16)
        dwout_ref[...] += _dot(mb, d_o, TN)
        dm = _dot(d_o, wout_ref[...], NT)
        dg_ref[:, 1024:2048] = (dm * y_a * sga * (1.0 - sga)).astype(BF16)
        dg_ref[:, 2048:3072] = (dm * y_b * sgb * (1.0 - sgb)).astype(BF16)
        for (att, z, sz, silu, u, _), sg, wo_ref, dwo_ref, datt_ref, z_off in (
                (branches[0], sga, woa_ref, dwoa_ref, daa_ref, 0), (branches[1], sgb, wob_ref, dwob_ref, dab_ref, 512)):
            dyb = (dm * sg).astype(BF16)
            dwo_ref[...] += _dot(u, dyb, TN)
            du = _dot(dyb, wo_ref[...], NT)
            datt = du * silu
            datt_ref[...] = datt.astype(BF16)
            dg_ref[:, z_off:z_off + 512] = (du * att * (sz * (1.0 + z * (1.0 - sz)))).astype(BF16)
            if z_off == 512:
                prod = datt * att
                hi = prod.astype(BF16)
                lo = (prod - hi.astype(F32)).astype(BF16)
                er = lax.broadcasted_iota(jnp.int32, (512, LANES), 0)
                ec = lax.broadcasted_iota(jnp.int32, (512, LANES), 1)
                e = (er // HEAD_DIM == ec).astype(BF16)
                delta = _dot(hi, e) + _dot(lo, e)
                delta_ref[...] = delta.T[0:SUBLANES, :]

        @pl.when(step == nt - 1)
        def _():
            sub = lax.broadcasted_iota(jnp.int32, (SUBLANES, D_MODEL), 0)
            dgf = jnp.sum(acc_gf[...], axis=0, keepdims=True)
            dgate = jnp.sum(acc_gate[...], axis=0, keepdims=True)
            loss = 0.5 * jnp.sum(acc_loss[...]) * (1.0 / D_MODEL)
            vec_ref[...] = jnp.where(sub == 0, dgf, jnp.where(sub == 1, dgate, jnp.where(sub == 2, loss, 0.0)))

    row = lambda w: pl.BlockSpec((tm, w), lambda i: (i, 0))
    return pl.pallas_call(
        body, name="mid", grid=(nt,),
        out_shape=[jax.ShapeDtypeStruct((s, D_MODEL), F32), jax.ShapeDtypeStruct((s, 512), BF16),
                   jax.ShapeDtypeStruct((s, 512), BF16), jax.ShapeDtypeStruct((s, W_G), BF16),
                   jax.ShapeDtypeStruct((SUBLANES, s), F32),
                   jax.ShapeDtypeStruct((512, D_MODEL), F32), jax.ShapeDtypeStruct((512, D_MODEL), F32),
                   jax.ShapeDtypeStruct((D_MODEL, D_MODEL), F32), jax.ShapeDtypeStruct((SUBLANES, D_MODEL), F32)],
        in_specs=[row(512), row(512), row(W_G), row(D_MODEL), row(D_MODEL),
                  _const_spec((1, D_MODEL)), _const_spec((1, D_MODEL)),
                  _const_spec((512, D_MODEL)), _const_spec((512, D_MODEL)), _const_spec((D_MODEL, D_MODEL))],
        out_specs=[row(D_MODEL), row(512), row(512), row(W_G),
                   pl.BlockSpec((SUBLANES, tm), lambda i: (0, i)),
                   pl.BlockSpec((512, D_MODEL), lambda i: (0, 0)), pl.BlockSpec((512, D_MODEL), lambda i: (0, 0)),
                   pl.BlockSpec((D_MODEL, D_MODEL), lambda i: (0, 0)), pl.BlockSpec((SUBLANES, D_MODEL), lambda i: (0, 0))],
        scratch_shapes=[pltpu.VMEM((SUBLANES, D_MODEL), F32)] * 3,
        compiler_params=_params(("arbitrary",), VMEM_LIMIT),
    )(att_a, att_b, g, x, target, gate, g_final, wo_a, wo_b, w_out)


def _rope_bwd(dt, cos, sin, lane):
    u = dt * sin
    lo = (lane % HEAD_DIM) < (HEAD_DIM // 2)
    return dt * cos + jnp.where(lo, pltpu.roll(u, 96, 1), -pltpu.roll(u, 32, 1))


def _swa_bwd(a, datt, l_all, sinks, cos, sin):
    s = a.shape[0]
    nt = s // SWA_ROWS

    def body(sink_ref, a_ref, ap_ref, do_ref, l_ref, cos_ref, sin_ref, da_ref, ds_ref, halo):
        step = pl.program_id(0)
        tile = nt - 1 - step

        @pl.when(step == 0)
        def _():
            halo[...] = jnp.zeros_like(halo)
            ds_ref[...] = jnp.zeros_like(ds_ref)

        lane = lax.broadcasted_iota(jnp.int32, (WINDOW, LANES), 1)
        sub8 = lax.broadcasted_iota(jnp.int32, (SUBLANES, LANES), 0)
        lane8 = lax.broadcasted_iota(jnp.int32, (SUBLANES, LANES), 1)
        blocks = _swa_blocks(a_ref, ap_ref)
        dsink = jnp.zeros((SUBLANES, LANES), F32)

        def join(pair, r0):
            x0, x1 = pair[0][r0:r0 + WINDOW], pair[1][r0:r0 + WINDOW]
            return jnp.where(lane < 64, x0 + pltpu.roll(x0, 64, 1), x1 + pltpu.roll(x1, 64, 1))

        units = [(jb, g) for jb in range(SWA_BLOCKS) for g in range(2)]
        n_u = len(units)
        sinks_col = [_per_head_column([sink_ref[GROUP * g + hh] for hh in range(GROUP)]) for g in range(2)]
        bands = [_swa_band(blocks[jb + 1], blocks[jb], g, lane) for jb, g in units]
        qs = [_stack_heads(blocks[jb + 1], g, lane) for jb, g in units]
        doms = [_stack_heads(do_ref.at[pl.ds(WINDOW * jb, WINDOW), :], g, lane) for jb, g in units]
        lcols = []
        for jb, g in units:
            lv = l_ref[WINDOW * jb:WINDOW * (jb + 1), :]
            lcols.append(_per_head_column([lv[:, GROUP * g + hh:GROUP * g + hh + 1] for hh in range(GROUP)]))
        ps = [jnp.exp(_swa_logits(qs[u], bands[u][0], (tile > 0) if jb == 0 else True) - lcols[u])
              for u, (jb, g) in enumerate(units)]
        dps = [_dot(doms[u], bands[u][1], NT) for u in range(n_u)]
        deltas = [jnp.sum(ps[u] * dps[u], axis=-1, keepdims=True) for u in range(n_u)]
        for u, (jb, g) in enumerate(units):
            sink_term = jnp.exp(sinks_col[g] - lcols[u]) * deltas[u]
            for hh in range(GROUP):
                tot = jnp.sum(sink_term[WINDOW * hh:WINDOW * (hh + 1)])
                dsink = dsink + jnp.where((sub8 == 0) & (lane8 == GROUP * g + hh), -tot, 0.0)
        dss = [(ps[u] * (dps[u] - deltas[u])).astype(BF16) for u in range(n_u)]
        dqs = [_dot(dss[u], bands[u][0]) * SCALE for u in range(n_u)]
        dks = [_dot(dss[u], qs[u], TN) * SCALE for u in range(n_u)]
        dvs = [_dot(ps[u].astype(BF16), doms[u], TN) for u in range(n_u)]

        carry_k, carry_v = halo[:, 0:LANES], halo[:, LANES:2 * LANES]
        for jb in reversed(range(SWA_BLOCKS)):
            rows = slice(WINDOW * jb, WINDOW * (jb + 1))
            cosv, sinv = cos_ref[rows, :], sin_ref[rows, :]
            for g in range(2):
                dq = dqs[2 * jb + g]
                for pb in range(2):
                    r0 = 2 * pb * WINDOW
                    dq_pair = jnp.where(lane < 64, dq[r0:r0 + WINDOW], dq[r0 + WINDOW:r0 + 2 * WINDOW])
                    da_ref[rows, LANES * (2 * g + pb):LANES * (2 * g + pb + 1)] = _rope_bwd(
                        dq_pair, cosv, sinv, lane).astype(BF16)
            dkb, dvb = dks[2 * jb:2 * jb + 2], dvs[2 * jb:2 * jb + 2]
            da_ref[rows, 512:640] = _rope_bwd(join(dkb, WINDOW) + carry_k, cosv, sinv, lane).astype(BF16)
            da_ref[rows, 640:768] = (join(dvb, WINDOW) + carry_v).astype(BF16)
            carry_k, carry_v = join(dkb, 0), join(dvb, 0)
        halo[:, 0:LANES] = carry_k
        halo[:, LANES:2 * LANES] = carry_v
        ds_ref[...] += dsink

    rev = lambda w: pl.BlockSpec((SWA_ROWS, w), lambda i: (nt - 1 - i, 0))
    return pl.pallas_call(
        body, name="swa_bwd", grid=(nt,),
        out_shape=[jax.ShapeDtypeStruct((s, W_A), BF16), jax.ShapeDtypeStruct((SUBLANES, LANES), F32)],
        in_specs=[pl.BlockSpec(memory_space=pltpu.SMEM), rev(W_A),
                  pl.BlockSpec((WINDOW, W_A), lambda i: (jnp.maximum(SWA_BLOCKS * (nt - 1 - i) - 1, 0), 0)),
                  rev(512), rev(LANES), rev(LANES), rev(LANES)],
        out_specs=[rev(W_A), pl.BlockSpec((SUBLANES, LANES), lambda i: (0, 0))],
        scratch_shapes=[pltpu.VMEM((WINDOW, 2 * LANES), F32)],
        compiler_params=_params(("arbitrary",)),
    )(sinks, a, a, datt, l_all, cos, sin)


def _fox_bwd(qa, ka, vb, do, lse, delta, ranges, t):
    s = qa.shape[0]
    nt = s // t

    def body(rg_ref, q_ref, do_ref, lse_ref, dl_ref, k_ref, v_ref, dq_ref, dk_ref, dv_ref, dc_ref, dr_ref, dq_acc):
        p = pl.program_id(0)
        j = pl.program_id(1)
        n_queries = [jnp.clip(_lane_scalar(rg_ref[0], 2 + hh, j), 1, nt - j) for hh in range(2)]

        @pl.when(j == 0)
        def _():
            dq_acc[...] = jnp.zeros_like(dq_acc)

        lane = lax.broadcasted_iota(jnp.int32, (t, LANES), 1)
        rows = lax.broadcasted_iota(jnp.int32, (t, t), 0)
        cols = lax.broadcasted_iota(jnp.int32, (t, t), 1)
        kt = k_ref[...]
        vt = v_ref[...]

        ks = [kt[:, LANES * hh:LANES * (hh + 1)] for hh in range(2)]

        def tile(qis, carry, heads=(0, 1), diagonal=False):
            dk0, dk1, dv = carry
            offs = [pl.multiple_of(i * t, t) for i in qis]
            units = [(u, hh) for u in range(len(qis)) for hh in heads]
            qts = [q_ref[pl.ds(off, t), :] for off in offs]
            dos = [do_ref[pl.ds(off, t), :] for off in offs]
            lses = [lse_ref[0, :, pl.ds(off, t)] for off in offs]
            dls = [dl_ref[0, :, pl.ds(off, t)] for off in offs]
            qs = [qts[u][:, LANES * hh:LANES * (hh + 1)] for u, hh in units]
            doms = [jnp.where((lane < 64) if hh == 0 else (lane >= 64), dos[u], jnp.zeros_like(dos[u])) for u, hh in units]
            sts = [_dot(ks[hh], qs[n], NT) for n, (u, hh) in enumerate(units)]
            dpts = [_dot(vt, doms[n], NT) for n in range(len(units))]
            if diagonal:
                sts = [jnp.where(cols >= rows, st, NEG) for st in sts]
            pts = [jnp.exp(sts[n] - lses[u][hh:hh + 1, :]) for n, (u, hh) in enumerate(units)]
            dsts = [(pts[n] * (dpts[n] - dls[u][hh:hh + 1, :])).astype(BF16) for n, (u, hh) in enumerate(units)]
            for n, (u, hh) in enumerate(units):
                dv = dv + _dot(pts[n].astype(BF16), doms[n])
                term = _dot(dsts[n], qs[n])
                dk0, dk1 = (dk0 + term, dk1) if hh == 0 else (dk0, dk1 + term)
                dq_acc[hh, pl.ds(offs[u], t), :] += _dot(dsts[n], ks[hh], TN)
            return dk0, dk1, dv

        zero = jnp.zeros((t, LANES), F32)
        carry = tile([j], (zero, zero, zero), diagonal=True)
        n_rest = jnp.minimum(n_queries[0], n_queries[1]) - 1
        carry = lax.fori_loop(0, n_rest // 2, lambda u, cr: tile([j + 1 + 2 * u, j + 2 + 2 * u], cr), carry)
        carry = lax.fori_loop(0, n_rest % 2, lambda u, cr: tile([j + n_rest], cr), carry)
        for hh in range(2):
            carry = lax.fori_loop(j + 1 + n_rest, j + n_queries[hh], lambda i, cr, hh=hh: tile([i], cr, heads=(hh,)), carry)
        dk0, dk1, dv = carry
        e0, e1 = _aug_lane(0), _aug_lane(1)
        dk_ref[...] = jnp.where(lane < 64, dk0, dk1).astype(BF16)
        dv_ref[...] = dv.astype(BF16)
        c0 = jnp.broadcast_to(dk0[:, e0 + 3:e0 + 4], (t, LANES))
        c1 = jnp.broadcast_to(dk1[:, e1 + 3:e1 + 4], (t, LANES))
        dc_ref[0] = jnp.where(lane == 2 * p, -c0, jnp.where(lane == 2 * p + 1, -c1, 0.0))

        @pl.when(j == nt - 1)
        def _():
            lane_s = lax.broadcasted_iota(jnp.int32, (s, LANES), 1)
            a0, a1 = dq_acc[0], dq_acc[1]
            dq_ref[...] = (jnp.where(lane_s < 64, a0, a1) * SCALE).astype(BF16)
            r0 = jnp.broadcast_to(a0[:, e0:e0 + 1], (s, LANES))
            r1 = jnp.broadcast_to(a1[:, e1:e1 + 1], (s, LANES))
            dr_ref[0] = jnp.where(lane_s == 2 * p, r0, jnp.where(lane_s == 2 * p + 1, r1, 0.0))

    return pl.pallas_call(
        body, name="fox_bwd", grid=(4, nt),
        out_shape=[jax.ShapeDtypeStruct((s, 512), BF16), jax.ShapeDtypeStruct((s, 512), BF16),
                   jax.ShapeDtypeStruct((s, 512), BF16), jax.ShapeDtypeStruct((4, s, LANES), F32),
                   jax.ShapeDtypeStruct((4, s, LANES), F32)],
        in_specs=[pl.BlockSpec((1, SUBLANES, LANES), lambda p, j: (p, 0, 0)),
                  pl.BlockSpec((s, 2 * LANES), lambda p, j: (0, p)),
                  pl.BlockSpec((s, LANES), lambda p, j: (0, p)),
                  pl.BlockSpec((1, SUBLANES, s), lambda p, j: (p, 0, 0)),
                  pl.BlockSpec((1, SUBLANES, s), lambda p, j: (p, 0, 0)),
                  pl.BlockSpec((t, 2 * LANES), lambda p, j: (j, p)),
                  pl.BlockSpec((t, LANES), lambda p, j: (j, p))],
        out_specs=[pl.BlockSpec((s, LANES), lambda p, j: (0, p)),
                   pl.BlockSpec((t, LANES), lambda p, j: (j, p)),
                   pl.BlockSpec((t, LANES), lambda p, j: (j, p)),
                   pl.BlockSpec((1, t, LANES), lambda p, j: (p, j, 0)),
                   pl.BlockSpec((1, s, LANES), lambda p, j: (p, 0, 0))],
        scratch_shapes=[pltpu.VMEM((2, s, LANES), F32)],
        compiler_params=_params(("parallel", "arbitrary"), VMEM_LIMIT),
    )(ranges, qa, do, lse, delta, ka, vb)


def _forget_logit_grad(dc_ref, dr_ref, f, b_f, carry):
    tb = f.shape[0]
    lane = lax.broadcasted_iota(jnp.int32, (tb, LANES), 1)
    dc = dc_ref[0] + dr_ref[0]
    for k in range(1, 4):
        dc = dc + (dc_ref[k] + dr_ref[k])
    hi, mid, lo = _split3(dc)
    rows = lax.broadcasted_iota(jnp.int32, (tb, tb), 0)
    cols = lax.broadcasted_iota(jnp.int32, (tb, tb), 1)
    triu = (cols >= rows).astype(BF16)
    dlogf = _dot(triu, hi) + _dot(triu, mid) + _dot(triu, lo) + carry[0:1, :]
    carry[...] = jnp.broadcast_to(dlogf[0:1, :], carry.shape)
    return jnp.where(lane < N_HEADS, dlogf * _sigmoid(-(f + b_f)), 0.0)


def _dh_norm_bwd(d_a, d_q, d_k, d_v, dcum_k, dcum_q, f, bf_pad, d_g, w_t, x, dx2, gnorm, scale1, tm=512):
    s = x.shape[0]
    nt = s // tm

    def body(da_ref, dq_ref, dk_ref, dv_ref, dc_ref, dr_ref, f_ref, bf_ref, dg_ref, w_ref, x_ref, dx2_ref, g_ref, sc_ref,
             gx_ref, vec_ref, df_ref, db_ref, a_sh, a_sc, a_g, carry):
        step = pl.program_id(0)

        @pl.when(step == 0)
        def _():
            a_sh[...] = jnp.zeros_like(a_sh)
            a_sc[...] = jnp.zeros_like(a_sc)
            a_g[...] = jnp.zeros_like(a_g)
            carry[...] = jnp.zeros_like(carry)
            db_ref[...] = jnp.zeros_like(db_ref)

        def fold(v):
            return jnp.sum(v.reshape(tm // SUBLANES, SUBLANES, D_MODEL), axis=0)

        dfb = _forget_logit_grad(dc_ref, dr_ref, f_ref[...], bf_ref[...], carry)
        d_f = dfb.astype(BF16)
        df_ref[...] = d_f
        sub8 = lax.broadcasted_iota(jnp.int32, (SUBLANES, LANES), 0)
        db_ref[...] += jnp.where(sub8 == 0, jnp.sum(dfb, axis=0, keepdims=True), 0.0)
        d_all = jnp.concatenate([da_ref[...], dq_ref[...], dk_ref[...], dv_ref[...], d_f, dg_ref[...]], axis=1)
        dh = _dot(d_all, w_ref[...])
        xv = x_ref[...]
        r = lax.rsqrt(jnp.mean(xv * xv, axis=-1, keepdims=True) + NORM_EPS)
        xn = xv * r
        gn = g_ref[...]
        a_sh[...] += fold(dh)
        a_sc[...] += fold(dh * (xn * gn))
        dn1 = dh * sc_ref[...]
        a_g[...] += fold(dn1 * xn)
        dxn = dn1 * gn
        gx_ref[...] = dx2_ref[...] + r * (dxn - xn * jnp.mean(dxn * xn, axis=-1, keepdims=True))

        @pl.when(step == nt - 1)
        def _():
            sub = lax.broadcasted_iota(jnp.int32, (SUBLANES, D_MODEL), 0)
            v_sh = jnp.sum(a_sh[...], axis=0, keepdims=True)
            v_sc = jnp.sum(a_sc[...], axis=0, keepdims=True)
            v_g = jnp.sum(a_g[...], axis=0, keepdims=True)
            vec_ref[...] = jnp.where(sub == 0, v_sh, jnp.where(sub == 1, v_sc, jnp.where(sub == 2, v_g, 0.0)))

    row = lambda w: pl.BlockSpec((tm, w), lambda i: (nt - 1 - i, 0))
    slabs = pl.BlockSpec((4, tm, LANES), lambda i: (0, nt - 1 - i, 0))
    return pl.pallas_call(
        body, name="dh_norm_bwd", grid=(nt,),
        out_shape=[jax.ShapeDtypeStruct((s, D_MODEL), F32), jax.ShapeDtypeStruct((SUBLANES, D_MODEL), F32),
                   jax.ShapeDtypeStruct((s, LANES), BF16), jax.ShapeDtypeStruct((SUBLANES, LANES), F32)],
        in_specs=[row(W_A), row(512), row(512), row(512), slabs, slabs, row(W_F), _const_spec((1, LANES)), row(W_G),
                  _const_spec((W_INT, D_MODEL)), row(D_MODEL), row(D_MODEL), _const_spec((1, D_MODEL)),
                  _const_spec((1, D_MODEL))],
        out_specs=[row(D_MODEL), pl.BlockSpec((SUBLANES, D_MODEL), lambda i: (0, 0)), row(LANES),
                   pl.BlockSpec((SUBLANES, LANES), lambda i: (0, 0))],
        scratch_shapes=[pltpu.VMEM((SUBLANES, D_MODEL), F32)] * 3 + [pltpu.VMEM((SUBLANES, LANES), F32)],
        compiler_params=_params(("arbitrary",), VMEM_LIMIT),
    )(d_a, d_q, d_k, d_v, dcum_k, dcum_q, f, bf_pad, d_g, w_t, x, dx2, gnorm, scale1)


def _dw_in(h_t, d_a, d_q, d_k, d_v, d_f, d_g, ts=1024, tc=512):
    s = h_t.shape[1]
    ns = s // ts
    w_fg = 512 + W_F + W_G - 512
    rows = 128
    n_slot = (R_END // 4 + LANES - 1) // LANES * LANES
    order = [(0, 0, W_A), (4, 0, 512), (1, 0, 512), (2, 0, 512), (3, 0, 512), (4, 512, N_HEADS), (4, 512 + W_F, W_G - 512)]

    def slot_pieces(k):
        lo, hi, out, col = (R_END // 4) * k, (R_END // 4) * (k + 1), [], 0
        for acc_i, c0, w in order:
            a, b = max(lo, col), min(hi, col + w)
            if a < b:
                out.append((acc_i, c0 + a - col, b - a))
            col += w
        return out

    def body(h_ref, da_ref, dq_ref, dk_ref, dv_ref, df_ref, dg_ref, o_ref, acc_a, acc_q, acc_k, acc_v, acc_fg, stage, sem):
        k = pl.program_id(0)
        accs = (acc_a, acc_q, acc_k, acc_v, acc_fg)

        @pl.when(k == 0)
        def _():
            for acc in accs:
                acc[...] = jnp.zeros_like(acc)

        hv = h_ref[...]

        def add(acc, c_acc, d_ref, c_d, width):
            for c0 in range(0, width, tc):
                w = min(tc, width - c0)
                acc[:, c_acc + c0:c_acc + c0 + w] += _dot(hv, d_ref[:, c_d + c0:c_d + c0 + w])

        add(acc_a, 0, da_ref, 0, W_A)
        add(acc_q, 0, dq_ref, 0, 512)
        add(acc_k, 0, dk_ref, 0, 512)
        add(acc_v, 0, dv_ref, 0, 512)
        add(acc_fg, 0, dg_ref, 0, 512)
        add(acc_fg, 512, df_ref, 0, W_F)
        add(acc_fg, 512 + W_F, dg_ref, 512, W_G - 512)

        @pl.when(k == ns - 1)
        def _():
            pending = [None, None]
            for n, r0 in enumerate(range(0, D_MODEL, rows)):
                buf = n % 2
                if pending[buf] is not None:
                    for cp in pending[buf]:
                        cp.wait()
                rs = slice(r0, r0 + rows)
                copies = []
                for slot in range(4):
                    parts = [accs[acc_i][rs, c0:c0 + w] for acc_i, c0, w in slot_pieces(slot)]
                    parts.append(jnp.zeros((rows, n_slot - R_END // 4), F32))
                    stage[buf, slot] = jnp.concatenate(parts, axis=1)
                    cp = pltpu.make_async_copy(stage.at[buf, slot], o_ref.at[slot, pl.ds(r0, rows), :], sem.at[4 * buf + slot])
                    cp.start()
                    copies.append(cp)
                pending[buf] = copies
            for copies in pending:
                for cp in copies:
                    cp.wait()

    spec = lambda d: pl.BlockSpec((ts, d.shape[1]), lambda k: (k, 0))
    return pl.pallas_call(
        body, name="dw_in", grid=(ns,),
        out_shape=jax.ShapeDtypeStruct((4, D_MODEL, n_slot), F32),
        in_specs=[pl.BlockSpec((D_MODEL, ts), lambda k: (0, k))] + [spec(d) for d in (d_a, d_q, d_k, d_v, d_f, d_g)],
        out_specs=pl.BlockSpec(memory_space=pl.ANY),
        scratch_shapes=[pltpu.VMEM((D_MODEL, W_A), F32)] + [pltpu.VMEM((D_MODEL, 512), F32)] * 3
        + [pltpu.VMEM((D_MODEL, w_fg), F32), pltpu.VMEM((2, 4, rows, n_slot), F32), pltpu.SemaphoreType.DMA((8,))],
        compiler_params=_params(("arbitrary",), VMEM_LIMIT),
    )(h_t, d_a, d_q, d_k, d_v, d_f, d_g)


def _small_grads(packs, c_t, dada_shard):
    def body(p_ref, ct_ref, da_ref, sum_ref, gw_ref):
        acc = p_ref[0]
        for dev in range(1, 8):
            acc = acc + p_ref[dev]
        sum_ref[...] = acc
        gw_ref[...] = jnp.dot(ct_ref[...], da_ref[...], preferred_element_type=F32, precision=lax.Precision.HIGHEST)

    return pl.pallas_call(
        body, name="small_grads",
        out_shape=[jax.ShapeDtypeStruct(packs.shape[1:], F32),
                   jax.ShapeDtypeStruct((c_t.shape[0], dada_shard.shape[1]), F32)],
    )(packs, c_t, dada_shard)


def _adamw_body(w_ref, g_ref, m_ref, v_ref, d_ref, mo_ref, vo_ref):
    c1 = 1.0 / (1.0 - ADAM_B1 ** ADAM_STEP)
    c2 = 1.0 / (1.0 - ADAM_B2 ** ADAM_STEP)
    gv = g_ref[...]
    mn = ADAM_B1 * m_ref[...] + (1.0 - ADAM_B1) * gv
    vn = ADAM_B2 * v_ref[...] + (1.0 - ADAM_B2) * (gv * gv)
    mo_ref[...] = mn
    vo_ref[...] = vn
    d_ref[...] = -ADAM_LR * ((mn * c1) / (jnp.sqrt(vn * c2) + ADAM_EPS) + ADAM_WD * w_ref[...])


def _adamw3(w, g, m, v, name, tb=128):
    spec = pl.BlockSpec((tb, SUBLANES, LANES), lambda i: (i, 0, 0))
    return pl.pallas_call(
        functools.partial(_adamw_body), name=name, grid=(pl.cdiv(w.shape[0], tb),),
        out_shape=[jax.ShapeDtypeStruct(w.shape, F32)] * 3,
        in_specs=[spec] * 4, out_specs=[spec] * 3,
        compiler_params=_params(("parallel",)),
    )(w, g, m, v)


def _adamw_many(items, name):
    n = len(items)

    def body(*refs):
        for i in range(n):
            _adamw_body(*refs[4 * i:4 * i + 4], *refs[4 * n + 3 * i:4 * n + 3 * i + 3])

    return pl.pallas_call(
        body, name=name,
        out_shape=[jax.ShapeDtypeStruct(it[0].shape, F32) for it in items for _ in range(3)],
        compiler_params=_params(vmem=VMEM_LIMIT),
    )(*[arr for it in items for arr in it])


def _rope_inputs(positions):
    inv_freq = 10000.0 ** (-jnp.arange(0, HEAD_DIM, 2, dtype=F32) / HEAD_DIM)
    pos = jnp.broadcast_to(positions.astype(F32)[:, None], (positions.shape[0], LANES))
    return pos, jnp.tile(inv_freq, 4)[None, :]


def _pad_rows(v, rows=SUBLANES):
    return jnp.pad(v, ((0, rows - v.shape[0]), (0, 0)))


def kernel(x, c, positions, w_ada, b_ada, g_norm, w_in, b_f, sinks, w_o_swa, w_o_fox, w_out, g_final, loss_target, m_w_ada, m_b_ada, m_g_norm, m_w_in, m_b_f, m_sinks, m_w_o_swa, m_w_o_fox, m_w_out, m_g_final, v_w_ada, v_b_ada, v_g_norm, v_w_in, v_b_f, v_sinks, v_w_o_swa, v_w_o_fox, v_w_out, v_g_final):
    ix, iy, ic = lax.axis_index("x"), lax.axis_index("y"), lax.axis_index("c")
    chip = 2 * ix + iy
    dev = 2 * chip + ic
    xs, tgt = x[0], loss_target[0]
    s = xs.shape[0]

    b_ada_shard = lax.dynamic_slice(b_ada, (0, chip * 768), (1, 768))
    ada_parts, w_int, w_int_t, g_oa, g_ob, g_out = _gather_inputs(
        _pad_rows(c), w_ada[0], b_ada_shard, jnp.transpose(w_in, (2, 0, 1)).reshape(-1, LANES),
        [w_o_swa[0], w_o_fox[0], w_out[0]], "gather_inputs")
    ada = lax.dynamic_index_in_dim(ada_parts, dev, axis=1, keepdims=False).reshape(1, 3 * D_MODEL)
    shift, scale, gate = ada[:, :D_MODEL], ada[:, D_MODEL:2 * D_MODEL], ada[:, 2 * D_MODEL:]
    scale1 = 1.0 + scale

    wo_a = jnp.transpose(g_oa, (1, 0, 2)).reshape(512, D_MODEL)
    wo_b = jnp.transpose(g_ob, (1, 0, 2)).reshape(512, D_MODEL)
    w_o = g_out.reshape(D_MODEL, D_MODEL)

    pos, freq = _rope_inputs(positions[0])
    bf_pad = jnp.pad(b_f, ((0, 0), (0, LANES - N_HEADS)))
    sink_vec = sinks[0]

    a, vb, f, g, h_t, cos, sin, qa, ka, va, stats = _norm_proj(
        xs, g_norm * scale1, shift, w_int, pos, freq, bf_pad, FOX_TILE)
    att_a, l_swa = _swa_fwd(a, sink_vec)
    ranges = _fox_tile_ranges(stats)
    att_b, lse = _fox_fwd(qa, ka, va, ranges, FOX_TILE)

    dx2, datt_a, datt_b, d_g, delta8, dwo_a, dwo_b, dw_out, vec_mid = _mid(
        att_a, att_b, g, xs, tgt, gate, g_final.reshape(1, D_MODEL), wo_a, wo_b, w_o)
    delta = jnp.pad(delta8.reshape(4, 2, s), ((0, 0), (0, SUBLANES - 2), (0, 0)))
    d_a, dsink = _swa_bwd(a, datt_a, l_swa, sink_vec, cos, sin)
    dq, dk, dv, dcum_k, dcum_q = _fox_bwd(qa, ka, vb, datt_b, lse, delta, ranges, FOX_TILE)
    grad_x, vec_dh, d_f, dbf = _dh_norm_bwd(
        d_a, dq, dk, dv, dcum_k, dcum_q, f, bf_pad, d_g, w_int_t, xs, dx2, g_norm, scale1)
    dw_in_slots = _dw_in(h_t, d_a, dq, dk, dv, d_f, d_g)

    tail = jnp.pad(jnp.concatenate([dbf[0:1, :N_HEADS], dsink[0:1, :N_HEADS]], axis=1), ((0, 0), (0, D_MODEL - 2 * N_HEADS)))
    pack = jnp.concatenate([c, vec_dh[0:2], vec_mid[1:2], vec_dh[2:3], vec_mid[0:1], tail, vec_mid[2:3]], axis=0)

    def slots(w, axis):
        if axis == 1:
            return jnp.transpose(w.reshape(w.shape[0], 4, w.shape[1] // 4), (1, 0, 2))
        return w.reshape(4, w.shape[0] // 4, w.shape[1])

    packs, g_wo_a, g_wo_b, g_w_out, g_w_in = _reduce_scatter(
        [slots(dwo_a, 1), slots(dwo_b, 1), slots(dw_out, 0), dw_in_slots], pack, "reduce_grads")
    g_w_in = g_w_in[:, :w_in.shape[2]]
    dada_all = packs[:, 1:4, :].reshape(8, 3 * D_MODEL)
    dada_shard = lax.dynamic_slice(dada_all, (0, chip * 768), (8, 768))
    sums, g_w_ada = _small_grads(packs, packs[:, 0, :].T, dada_shard)
    g_b_ada = sums[1:4].reshape(1, 3 * D_MODEL)
    g_g_norm = sums[4:5]
    g_g_final = sums[5]
    g_b_f = sums[6:7, :N_HEADS]
    g_sinks = sums[6:7, N_HEADS:2 * N_HEADS]
    loss = sums[7, 0]

    grads = {
        "w_ada": g_w_ada, "b_ada": g_b_ada, "g_norm": g_g_norm, "w_in": g_w_in, "b_f": g_b_f, "sinks": g_sinks,
        "w_o_swa": g_wo_a, "w_o_fox": g_wo_b, "w_out": g_w_out, "g_final": g_g_final,
    }
    params = {
        "w_ada": (w_ada, m_w_ada, v_w_ada), "b_ada": (b_ada, m_b_ada, v_b_ada), "g_norm": (g_norm, m_g_norm, v_g_norm),
        "w_in": (w_in, m_w_in, v_w_in), "b_f": (b_f, m_b_f, v_b_f), "sinks": (sinks, m_sinks, v_sinks),
        "w_o_swa": (w_o_swa, m_w_o_swa, v_w_o_swa), "w_o_fox": (w_o_fox, m_w_o_fox, v_w_o_fox),
        "w_out": (w_out, m_w_out, v_w_out), "g_final": (g_final, m_g_final, v_g_final),
    }
    n_col = w_in.shape[2]

    def as_stored(t):
        return jnp.transpose(t, (2, 0, 1)).reshape(n_col, SUBLANES, LANES)

    def from_stored(t):
        return jnp.transpose(t, (1, 2, 0)).reshape(1, D_MODEL, n_col)

    names = list(grads)
    others = [nm for nm in names if nm != "w_in"]

    def as_2d(t):
        return t.reshape((t.shape[-2], t.shape[-1]) if t.ndim >= 2 else (1, t.shape[0]))

    flat = _adamw_many([tuple(as_2d(t) for t in (params[nm][0], grads[nm], params[nm][1], params[nm][2])) for nm in others],
                       "adamw_small")
    results = {}
    for i, nm in enumerate(others):
        shape = params[nm][0].shape
        results[nm] = [t.reshape(shape) for t in (grads[nm], *flat[3 * i:3 * i + 3])]
    w, m, v = params["w_in"]
    g_st = as_stored(grads["w_in"][None])
    d_, m_, v_ = _adamw3(as_stored(w), g_st, as_stored(m), as_stored(v), "adamw_w_in")
    results["w_in"] = [from_stored(t) for t in (g_st, d_, m_, v_)]
    return (loss, grad_x[None], *[results[nm][0] for nm in names], *[results[nm][1] for nm in names],
            *[results[nm][2] for nm in names], *[results[nm][3] for nm in names])
```

```python
import functools

import numpy as np
import jax
import jax.numpy as jnp
from jax import lax
from jax.experimental import pallas as pl
from jax.experimental.pallas import tpu as pltpu

F32 = jnp.float32
BF16 = jnp.bfloat16
MESH = pl.DeviceIdType.MESH

D_MODEL = 1024
HEAD_DIM = 64
N_HEADS = 8
WINDOW = 128
NORM_EPS = 1e-6
SCALE = HEAD_DIM ** -0.5
NEG = -1e30
LANES = 128
SUBLANES = 8
VMEM_LIMIT = 60 * 1024 * 1024
FOX_TILE = 512

W_A, W_B, W_F, W_G = 768, 1536, 128, 3072
OFF_A, OFF_B, OFF_F, OFF_G = 0, 768, 2304, 2432
W_INT = W_A + W_B + W_F + W_G
R_ZA, R_QB, R_FB, R_ZB, R_END = 768, 1280, 2816, 2824, 5384

ADAM_LR, ADAM_B1, ADAM_B2, ADAM_EPS, ADAM_WD, ADAM_STEP = 0.001, 0.9, 0.999, 1e-08, 0.01, 10

NT = (((1,), (1,)), ((), ()))
TN = (((0,), (0,)), ((), ()))


def _dot(a, b, dims=None):
    if dims is None:
        return jnp.dot(a, b, preferred_element_type=F32)
    return lax.dot_general(a, b, dims, preferred_element_type=F32)


def _split3(v):
    hi = v.astype(BF16)
    r1 = v - hi.astype(F32)
    mid = r1.astype(BF16)
    lo = (r1 - mid.astype(F32)).astype(BF16)
    return hi, mid, lo


def _sigmoid(v):
    return 1.0 / (1.0 + jnp.exp(-v))


def _params(sem=None, vmem=None):
    return pltpu.CompilerParams(dimension_semantics=sem, vmem_limit_bytes=vmem)


def _const_spec(shape):
    nd = len(shape)
    return pl.BlockSpec(shape, lambda *_: (0,) * nd, pipeline_mode=pl.Buffered(1))


def _flip(v, f):
    return 1 - v if f else v


_CHIP_FLIPS = ((1, 0), (0, 1), (1, 1))


def _gather_inputs(c_pad, w_ada, b_ada_shard, w_in_shard, small_shards, name):
    shards = [w_in_shard] + list(small_shards)
    n = len(shards)
    n_col = w_ada.shape[1]
    shard_w = w_in_shard.shape[0] // SUBLANES
    rows = 128

    def body(*refs):
        c_ref, wa_ref, ba_ref = refs[:3]
        ins = refs[3:3 + n]
        ada_ref, wint_ref, wintt_ref = refs[3 + n:6 + n]
        g_in, call_ref, send_sems, recv_sems = refs[5 + 2 * n:9 + 2 * n]
        outs = (g_in,) + tuple(refs[6 + n:5 + 2 * n])
        x, y, c = lax.axis_index("x"), lax.axis_index("y"), lax.axis_index("c")
        k_me = 2 * x + y
        me = 2 * k_me + c
        sibling = (x, y, 1 - c)
        chips = [(_flip(x, fx), _flip(y, fy)) for fx, fy in _CHIP_FLIPS]

        def piece(i, chip_k, half):
            hr = outs[i].shape[1] // 2
            return outs[i].at[chip_k, pl.ds(half * hr, hr), :]

        def copy(i, slot, chip_k, half, to):
            return pltpu.make_async_remote_copy(
                src_ref=piece(i, chip_k, half), dst_ref=piece(i, chip_k, half),
                send_sem=send_sems.at[6 * i + slot], recv_sem=recv_sems.at[6 * i + slot],
                device_id=to, device_id_type=MESH)

        def small(ref, slot, sem, to):
            return pltpu.make_async_remote_copy(
                src_ref=ref.at[slot], dst_ref=ref.at[slot], send_sem=send_sems.at[6 * n + sem],
                recv_sem=recv_sems.at[6 * n + sem], device_id=to, device_id_type=MESH)

        whole = shard_w // LANES * LANES
        for a in range(SUBLANES):
            main = ins[0][pl.ds(a, whole, stride=SUBLANES), :]
            tail = ins[0][pl.ds(a + SUBLANES * whole, shard_w - whole, stride=SUBLANES), :]
            tail = jnp.concatenate([tail, jnp.zeros((LANES - (shard_w - whole), LANES), F32)], axis=0)
            blk = jnp.concatenate([main.T, tail.T[:, :shard_w - whole]], axis=1)
            g_in[k_me, LANES * a:LANES * (a + 1), :] = blk.astype(BF16)
        for i in range(1, n):
            outs[i][k_me] = ins[i][...].astype(BF16)
        started = []
        for i in range(n):
            for j, chip in enumerate(chips):
                cp = copy(i, j, k_me, c, (chip[0], chip[1], c))
                cp.start()
                started.append(cp)

        call_ref[me] = c_ref[...]
        peers = [(_flip(x, k & 4), _flip(y, k & 2), _flip(c, k & 1)) for k in range(1, 8)]
        for k, peer in enumerate(peers):
            cp = small(call_ref, me, k, peer)
            cp.start()
            started.append(cp)
        for k, peer in enumerate(peers):
            small(call_ref, 4 * peer[0] + 2 * peer[1] + peer[2], k, peer).wait_recv()
        c_all = call_ref[:, 0, :].astype(BF16)
        ada_ref[k_me] = _dot(c_all, wa_ref[...].astype(BF16)) + ba_ref[...]
        for j, chip in enumerate(chips):
            cp = small(ada_ref, k_me, 7 + j, (chip[0], chip[1], c))
            cp.start()
            started.append(cp)

        for j, chip in enumerate(chips):
            chip_k = 2 * chip[0] + chip[1]
            for i in range(n):
                copy(i, j, chip_k, c, (chip[0], chip[1], c)).wait_recv()
                cp = copy(i, 3 + j, chip_k, c, sibling)
                cp.start()
                started.append(cp)
        for j, chip in enumerate(chips):
            chip_k = 2 * chip[0] + chip[1]
            small(ada_ref, chip_k, 7 + j, (chip[0], chip[1], c)).wait_recv()
            for i in range(n):
                copy(i, 3 + j, chip_k, 1 - c, sibling).wait_recv()
        for cp in started:
            cp.wait_send()

        def ref_cols(slots, a, b):
            runs = []
            for k in range(4):
                lo, hi = max(a, shard_w * k), min(b, shard_w * (k + 1))
                if lo < hi:
                    runs.append(slots[k][:, lo - shard_w * k:hi - shard_w * k])
            return runs

        for r0 in range(0, D_MODEL, rows):
            rs = slice(r0, r0 + rows)
            slots = [g_in[k, rs, :] for k in range(4)]
            row = jnp.concatenate(
                ref_cols(slots, 0, R_ZA) + ref_cols(slots, R_QB, R_FB) + ref_cols(slots, R_FB, R_ZB)
                + [jnp.zeros((rows, W_F - N_HEADS), BF16)] + ref_cols(slots, R_ZA, R_QB) + ref_cols(slots, R_ZB, R_END),
                axis=1)
            wint_ref[rs, :] = row
            wintt_ref[:, rs] = row.T

    vmem = pl.BlockSpec(memory_space=pltpu.VMEM)
    return pl.pallas_call(
        body, name=name,
        out_shape=[jax.ShapeDtypeStruct((4, 8, n_col), F32), jax.ShapeDtypeStruct((D_MODEL, W_INT), BF16),
                   jax.ShapeDtypeStruct((W_INT, D_MODEL), BF16)]
        + [jax.ShapeDtypeStruct((4,) + s.shape, BF16) for s in small_shards],
        in_specs=[vmem] * (3 + n),
        out_specs=[vmem] * (2 + n),
        scratch_shapes=[pltpu.VMEM((4, D_MODEL, shard_w), BF16), pltpu.VMEM((8,) + c_pad.shape, F32),
                        pltpu.SemaphoreType.DMA((6 * n + 10,)), pltpu.SemaphoreType.DMA((6 * n + 10,))],
        compiler_params=_params(vmem=VMEM_LIMIT),
    )(c_pad, w_ada, b_ada_shard, *shards)


def _reduce_scatter(pieces, pack, name):
    n = len(pieces)

    def body(*refs):
        pack_ref, ins = refs[0], refs[1:1 + n]
        packs_ref, outs = refs[1 + n], refs[2 + n:2 + 2 * n]
        rest = refs[2 + 2 * n:]
        own, got = rest[:n], rest[n:2 * n]
        sendb, recvb = rest[2 * n:3 * n], rest[3 * n:4 * n]
        send_sems, recv_sems, local_sems = rest[4 * n:4 * n + 3]
        x, y, c = lax.axis_index("x"), lax.axis_index("y"), lax.axis_index("c")
        k_me = 2 * x + y
        me = 2 * k_me + c
        sibling = (x, y, 1 - c)
        chips = [(_flip(x, fx), _flip(y, fy)) for fx, fy in _CHIP_FLIPS]
        hrs = [p.shape[1] // 2 for p in pieces]

        def remote(i, slot, src, dst, to):
            return pltpu.make_async_remote_copy(
                src_ref=src, dst_ref=dst, send_sem=send_sems.at[5 * i + slot], recv_sem=recv_sems.at[5 * i + slot],
                device_id=to, device_id_type=MESH)

        started = []
        packs_ref[me] = pack_ref[...]
        peers = [(_flip(x, k & 4), _flip(y, k & 2), _flip(c, k & 1)) for k in range(1, 8)]
        for k, peer in enumerate(peers):
            cp = pltpu.make_async_remote_copy(
                src_ref=pack_ref, dst_ref=packs_ref.at[me], send_sem=send_sems.at[5 * n + k],
                recv_sem=recv_sems.at[5 * n + k], device_id=peer, device_id_type=MESH)
            cp.start()
            started.append(cp)
        loads = []
        for i in range(n):
            ld = pltpu.make_async_copy(ins[i].at[:, pl.ds(c * hrs[i], hrs[i]), :], own[i], local_sems.at[i])
            ld.start()
            loads.append(ld)
            cp = remote(i, 0, ins[i].at[:, pl.ds((1 - c) * hrs[i], hrs[i]), :], got[i], sibling)
            cp.start()
            started.append(cp)
        for i in range(n):
            loads[i].wait()
            remote(i, 0, ins[i].at[:, pl.ds(c * hrs[i], hrs[i]), :], got[i], sibling).wait_recv()
            for j, chip in enumerate(chips):
                chip_k = 2 * chip[0] + chip[1]
                sendb[i][j] = (own[i][chip_k] + got[i][chip_k]).astype(BF16)
                cp = remote(i, 1 + j, sendb[i].at[j], recvb[i].at[j], (chip[0], chip[1], c))
                cp.start()
                started.append(cp)
        for i in range(n):
            acc = own[i][k_me] + got[i][k_me]
            for j, chip in enumerate(chips):
                remote(i, 1 + j, sendb[i].at[j], recvb[i].at[j], (chip[0], chip[1], c)).wait_recv()
                acc = acc + recvb[i][j].astype(F32)
            mine = outs[i].at[pl.ds(c * hrs[i], hrs[i]), :]
            outs[i][pl.ds(pl.multiple_of(c * hrs[i], SUBLANES), hrs[i]), :] = acc
            cp = remote(i, 4, mine, mine, sibling)
            cp.start()
            started.append(cp)
        for i in range(n):
            theirs = outs[i].at[pl.ds((1 - c) * hrs[i], hrs[i]), :]
            remote(i, 4, theirs, theirs, sibling).wait_recv()
        for k, peer in enumerate(peers):
            pltpu.make_async_remote_copy(
                src_ref=pack_ref, dst_ref=packs_ref.at[4 * peer[0] + 2 * peer[1] + peer[2]],
                send_sem=send_sems.at[5 * n + k], recv_sem=recv_sems.at[5 * n + k],
                device_id=peer, device_id_type=MESH).wait_recv()
        for cp in started:
            cp.wait_send()

    vmem = pl.BlockSpec(memory_space=pltpu.VMEM)
    scratch = []
    scratch += [pltpu.VMEM((4, p.shape[1] // 2, p.shape[2]), F32) for p in pieces]
    scratch += [pltpu.VMEM((4, p.shape[1] // 2, p.shape[2]), F32) for p in pieces]
    scratch += [pltpu.VMEM((3, p.shape[1] // 2, p.shape[2]), BF16) for p in pieces]
    scratch += [pltpu.VMEM((3, p.shape[1] // 2, p.shape[2]), BF16) for p in pieces]
    scratch += [pltpu.SemaphoreType.DMA((5 * n + 7,)), pltpu.SemaphoreType.DMA((5 * n + 7,)), pltpu.SemaphoreType.DMA((n,))]
    return pl.pallas_call(
        body, name=name,
        out_shape=[jax.ShapeDtypeStruct((8,) + pack.shape, F32)] + [jax.ShapeDtypeStruct(p.shape[1:], F32) for p in pieces],
        in_specs=[vmem] + [pl.BlockSpec(memory_space=pl.ANY)] * n,
        out_specs=[vmem] * (1 + n),
        scratch_shapes=scratch,
        compiler_params=_params(vmem=VMEM_LIMIT),
    )(pack, *pieces)


def _rope_fwd(t, cos, sin, lane):
    lo = (lane % HEAD_DIM) < (HEAD_DIM // 2)
    return t * cos + jnp.where(lo, -pltpu.roll(t, 96, 1), pltpu.roll(t, 32, 1)) * sin


def _norm_proj(x, gmod, shift, w_int, pos, freq, bf_pad, tm):
    s = x.shape[0]

    def body(x_ref, g_ref, sh_ref, w_ref, pos_ref, fr_ref, bf_ref,
             a_ref, vb_ref, f_ref, gg_ref, ht_ref, cos_ref, sin_ref, q_ref, k_ref, v_ref, st_ref, carry):
        @pl.when(pl.program_id(0) == 0)
        def _():
            carry[...] = jnp.zeros_like(carry)

        xv = x_ref[...]
        r = lax.rsqrt(jnp.mean(xv * xv, axis=-1, keepdims=True) + NORM_EPS)
        hf = (xv * r) * g_ref[...] + sh_ref[...]
        hb = hf.astype(BF16)
        ht_ref[...] = hb.T
        pa = _dot(hb, w_ref[:, OFF_A:OFF_A + W_A])
        ang = pos_ref[...] * fr_ref[...]
        cosv, sinv = jnp.cos(ang), jnp.sin(ang)
        cos_ref[...] = cosv
        sin_ref[...] = sinv
        lane = lax.broadcasted_iota(jnp.int32, (tm, LANES), 1)
        for j in range(5):
            t = pa[:, LANES * j:LANES * (j + 1)]
            a_ref[:, LANES * j:LANES * (j + 1)] = _rope_fwd(t, cosv, sinv, lane).astype(BF16)
        a_ref[:, 640:768] = pa[:, 640:768].astype(BF16)
        pf = _dot(hb, w_ref[:, OFF_F:OFF_F + W_F])
        f_ref[...] = pf
        bblk = _dot(hb, w_ref[:, OFF_B:OFF_B + W_B]).astype(BF16)
        vb_ref[...] = bblk[:, 1024:1536]
        gg_ref[...] = _dot(hb, w_ref[:, OFF_G:OFF_G + W_G]).astype(BF16)
        _augment_heads(bblk, _cumsum_tile(pf, bf_ref[...], carry), q_ref, k_ref, v_ref, st_ref)

    row = lambda w: pl.BlockSpec((tm, w), lambda i: (i, 0))
    return pl.pallas_call(
        body, name="norm_proj", grid=(s // tm,),
        out_shape=[jax.ShapeDtypeStruct((s, W_A), BF16), jax.ShapeDtypeStruct((s, 512), BF16),
                   jax.ShapeDtypeStruct((s, W_F), F32), jax.ShapeDtypeStruct((s, W_G), BF16),
                   jax.ShapeDtypeStruct((D_MODEL, s), BF16),
                   jax.ShapeDtypeStruct((s, LANES), F32), jax.ShapeDtypeStruct((s, LANES), F32)]
        + [jax.ShapeDtypeStruct((s, 1024), BF16)] * 3 + [jax.ShapeDtypeStruct((s // tm, SUBLANES, LANES), F32)],
        in_specs=[row(D_MODEL), _const_spec((1, D_MODEL)), _const_spec((1, D_MODEL)), _const_spec((D_MODEL, W_INT)),
                  row(LANES), _const_spec((1, LANES)), _const_spec((1, LANES))],
        out_specs=[row(W_A), row(512), row(W_F), row(W_G), pl.BlockSpec((D_MODEL, tm), lambda i: (0, i)),
                   row(LANES), row(LANES), row(1024), row(1024), row(1024),
                   pl.BlockSpec((1, SUBLANES, LANES), lambda i: (i, 0, 0))],
        scratch_shapes=[pltpu.VMEM((SUBLANES, LANES), F32)],
        compiler_params=_params(("arbitrary",), VMEM_LIMIT),
    )(x, gmod, shift, w_int, pos, freq, bf_pad)


def _log_sigmoid(u):
    return jnp.minimum(u, 0.0) - jnp.log(1.0 + jnp.exp(-jnp.abs(u)))


def _cumsum_tile(f, b_f, carry):
    tb = f.shape[0]
    lane = lax.broadcasted_iota(jnp.int32, (tb, LANES), 1)
    logf = jnp.where(lane < N_HEADS, _log_sigmoid(f + b_f), 0.0)
    hi, mid, lo = _split3(logf)
    rows = lax.broadcasted_iota(jnp.int32, (tb, tb), 0)
    cols = lax.broadcasted_iota(jnp.int32, (tb, tb), 1)
    tril = (cols <= rows).astype(BF16)
    cum = _dot(tril, hi) + _dot(tril, mid) + _dot(tril, lo) + carry[0:1, :]
    carry[...] = jnp.broadcast_to(cum[tb - 1:tb, :], carry.shape)
    return cum


def _aug_lane(h):
    return 64 if h % 2 == 0 else 0


def _augment_heads(bblk, cumv, q_ref, k_ref, v_ref, st_ref):
    t = bblk.shape[0]
    lane = lax.broadcasted_iota(jnp.int32, (t, LANES), 1)
    lane_b = lane.astype(BF16)
    sub8 = lax.broadcasted_iota(jnp.int32, (SUBLANES, LANES), 0)
    lane8 = lax.broadcasted_iota(jnp.int32, (SUBLANES, LANES), 1)
    one = jnp.ones((t, LANES), BF16)
    zero = jnp.zeros((t, LANES), BF16)
    stats = jnp.zeros((SUBLANES, LANES), F32)
    for p in range(4):
        qblk = bblk[:, LANES * p:LANES * (p + 1)] * SCALE
        kblk = bblk[:, 512 + LANES * p:512 + LANES * (p + 1)]
        vblk = bblk[:, 1024 + LANES * p:1024 + LANES * (p + 1)]
        qf, kf = qblk.astype(F32), kblk.astype(F32)
        q2, k2, qk = qf * qf, kf * kf, qf * kf
        for odd in range(2):
            h = 2 * p + odd
            a0 = _aug_lane(h)
            data_b = (lane_b < 64) if odd == 0 else (lane_b >= 64)
            data = (lane < 64) if odd == 0 else (lane >= 64)
            hi, mid, lo = _split3(jnp.broadcast_to(cumv[:, h:h + 1], (t, LANES)))
            ones3_q = (lane_b >= a0 + 3) & (lane_b < a0 + 6)
            ones3_k = (lane_b >= a0) & (lane_b < a0 + 3)
            aug_q = jnp.where(lane_b == a0, hi, jnp.where(lane_b == a0 + 1, mid, jnp.where(
                lane_b == a0 + 2, lo, jnp.where(ones3_q, one, zero))))
            aug_k = jnp.where(ones3_k, one, jnp.where(lane_b == a0 + 3, -hi, jnp.where(
                lane_b == a0 + 4, -mid, jnp.where(lane_b == a0 + 5, -lo, zero))))
            q_ref[:, LANES * h:LANES * (h + 1)] = jnp.where(data_b, qblk, aug_q)
            k_ref[:, LANES * h:LANES * (h + 1)] = jnp.where(data_b, kblk, aug_k)
            v_ref[:, LANES * h:LANES * (h + 1)] = jnp.where(data_b, vblk, jnp.where(lane_b == a0, one, zero))
            qn = jnp.sqrt(jnp.max(jnp.sum(jnp.where(data, q2, 0.0), axis=-1, keepdims=True)))
            kn = jnp.sqrt(jnp.max(jnp.sum(jnp.where(data, k2, 0.0), axis=-1, keepdims=True)))
            dmin = jnp.min(jnp.sum(jnp.where(data, qk, 0.0), axis=-1, keepdims=True))
            c_first, c_last = cumv[0:1, h:h + 1], cumv[t - 1:t, h:h + 1]
            row = jnp.where(lane8 == 0, qn, jnp.where(lane8 == 1, kn, jnp.where(
                lane8 == 2, c_first, jnp.where(lane8 == 3, c_last, jnp.where(lane8 == 4, dmin, 0.0)))))
            stats = jnp.where(sub8 == h, row, stats)
    st_ref[0] = stats


PRUNE_MARGIN = 88.0


def _fox_tile_ranges(stats):
    nt = stats.shape[0]
    qn, kn, c_first, c_last, d_min = (stats[:, :, n] for n in range(5))
    bound = (1.01 * qn[:, None, :] * kn[None, :, :] - jnp.minimum(d_min, 0.0)[:, None, :] + 0.05
             + c_first[:, None, :] - c_last[None, :, :])
    idx = jnp.arange(nt)
    skip = (bound <= -PRUNE_MARGIN) & (idx[None, :, None] < idx[:, None, None])
    first_key = jnp.sum(jnp.cumprod(skip, axis=1), axis=1)
    needed = (idx[None, :, None] >= first_key[:, None, :]) & (idx[None, :, None] <= idx[:, None, None])
    last_query = jnp.max(jnp.where(needed, idx[:, None, None], 0), axis=0)
    n_query = last_query - idx[:, None] + 1
    table = jnp.zeros((4, SUBLANES, LANES), F32)
    for odd in range(2):
        table = table.at[:, odd, :nt].set(first_key[:, odd::2].T.astype(F32))
        table = table.at[:, 2 + odd, :nt].set(n_query[:, odd::2].T.astype(F32))
    return table


def _lane_scalar(block, row, lane_idx):
    sub8 = lax.broadcasted_iota(jnp.int32, (SUBLANES, LANES), 0)
    lane8 = lax.broadcasted_iota(jnp.int32, (SUBLANES, LANES), 1)
    return jnp.sum(jnp.where((sub8 == row) & (lane8 == lane_idx), block, 0.0)).astype(jnp.int32)


def _fox_fwd(qa, ka, va, ranges, t):
    s = qa.shape[0]
    nt = s // t
    nc = t // LANES

    def body(rg_ref, q_ref, k_ref, v_ref, o_ref, lse_ref):
        i = pl.program_id(1)
        lane = lax.broadcasted_iota(jnp.int32, (t, LANES), 1)
        rows = lax.broadcasted_iota(jnp.int32, (t, t), 0)
        cols = lax.broadcasted_iota(jnp.int32, (t, t), 1)
        firsts = [jnp.clip(_lane_scalar(rg_ref[0], hh, i), 0, i) for hh in range(2)]
        first = jnp.maximum(firsts[0], firsts[1])

        def update(js, carry, heads=(0, 1), diagonal=False):
            offs = [pl.multiple_of(j * t, t) for j in js]
            kts = [k_ref[pl.ds(off, t), :] for off in offs]
            vts = [v_ref[pl.ds(off, t), :] for off in offs]
            scs = {hh: [_dot(q_ref[:, LANES * hh:LANES * (hh + 1)], kt[:, LANES * hh:LANES * (hh + 1)], NT) for kt in kts]
                   for hh in heads}
            if diagonal:
                scs = {hh: [jnp.where(cols <= rows, sc, NEG) for sc in scs[hh]] for hh in heads}
            m_new = {}
            for hh in heads:
                part = None
                for sc in scs[hh]:
                    for cch in range(nc):
                        chunk = sc[:, LANES * cch:LANES * (cch + 1)]
                        part = chunk if part is None else jnp.maximum(part, chunk)
                m_new[hh] = jnp.maximum(carry[2 * hh], jnp.max(part, axis=-1, keepdims=True))
            alphas = {hh: jnp.exp(carry[2 * hh] - m_new[hh]) for hh in heads}
            ps = {hh: [jnp.exp(sc - m_new[hh]).astype(BF16) for sc in scs[hh]] for hh in heads}
            out = list(carry)
            for hh in heads:
                pv = None
                for p, vt in zip(ps[hh], vts):
                    term = _dot(p, vt[:, LANES * hh:LANES * (hh + 1)])
                    pv = term if pv is None else pv + term
                out[2 * hh], out[2 * hh + 1] = m_new[hh], alphas[hh] * carry[2 * hh + 1] + pv
            return tuple(out)

        col0 = jnp.full((t, 1), NEG, F32)
        zero = jnp.zeros((t, LANES), F32)
        carry = (col0, zero, col0, zero)
        for hh in range(2):
            carry = lax.fori_loop(firsts[hh], first, lambda j, cr, hh=hh: update([j], cr, heads=(hh,)), carry)
        n_off = i - first
        carry = lax.fori_loop(0, n_off // 2, lambda u, cr: update([first + 2 * u, first + 2 * u + 1], cr), carry)
        carry = lax.fori_loop(0, n_off % 2, lambda u, cr: update([i - 1], cr), carry)

        def diagonal(carry):
            hf = t // 2
            off = pl.multiple_of(i * t, t)
            tri = cols[0:hf, 0:hf] <= rows[0:hf, 0:hf]
            out = list(carry)
            for part, (r0, k0) in enumerate(((0, 0), (hf, hf))):
                kt = k_ref[pl.ds(off + k0, hf), :]
                vt = v_ref[pl.ds(off + k0, hf), :]
                for hh in range(2):
                    m_old, acc_old = out[2 * hh][r0:], out[2 * hh + 1][r0:]
                    sc = _dot(q_ref[r0:, LANES * hh:LANES * (hh + 1)], kt[:, LANES * hh:LANES * (hh + 1)], NT)
                    if part == 0:
                        sc = jnp.concatenate([jnp.where(tri, sc[0:hf], NEG), sc[hf:]], axis=0)
                    else:
                        sc = jnp.where(tri, sc, NEG)
                    m_new = jnp.maximum(m_old, jnp.max(sc, axis=-1, keepdims=True))
                    p = jnp.exp(sc - m_new).astype(BF16)
                    acc_new = jnp.exp(m_old - m_new) * acc_old + _dot(p, vt[:, LANES * hh:LANES * (hh + 1)])
                    if r0:
                        m_new = jnp.concatenate([out[2 * hh][:r0], m_new], axis=0)
                        acc_new = jnp.concatenate([out[2 * hh + 1][:r0], acc_new], axis=0)
                    out[2 * hh], out[2 * hh + 1] = m_new, acc_new
            return tuple(out)

        m0, acc0, m1, acc1 = diagonal(carry)
        l0, l1 = acc0[:, _aug_lane(0):_aug_lane(0) + 1], acc1[:, _aug_lane(1):_aug_lane(1) + 1]
        o_ref[...] = jnp.where(lane < 64, acc0 * (1.0 / l0), acc1 * (1.0 / l1)).astype(BF16)
        sub = lax.broadcasted_iota(jnp.int32, (SUBLANES, t), 0)
        lse0 = jnp.broadcast_to(m0 + jnp.log(l0), (t, LANES)).T[0:SUBLANES, :]
        lse1 = jnp.broadcast_to(m1 + jnp.log(l1), (t, LANES)).T[0:SUBLANES, :]
        lse_ref[0] = jnp.where(sub == 0, lse0, jnp.where(sub == 1, lse1, 0.0))

    pair = pl.BlockSpec((s, 2 * LANES), lambda p, i: (0, p))
    return pl.pallas_call(
        body, name="fox_fwd", grid=(4, nt),
        out_shape=[jax.ShapeDtypeStruct((s, 512), BF16), jax.ShapeDtypeStruct((4, SUBLANES, s), F32)],
        in_specs=[pl.BlockSpec((1, SUBLANES, LANES), lambda p, i: (p, 0, 0)),
                  pl.BlockSpec((t, 2 * LANES), lambda p, i: (i, p)), pair, pair],
        out_specs=[pl.BlockSpec((t, LANES), lambda p, i: (i, p)),
                   pl.BlockSpec((1, SUBLANES, t), lambda p, i: (p, 0, i))],
        compiler_params=_params(("parallel", "arbitrary"), VMEM_LIMIT),
    )(ranges, qa, ka, va)


def _dup_halves(blk, lane):
    f = blk.astype(F32)
    r = pltpu.roll(f, 64, 1)
    return jnp.where(lane < 64, f, r).astype(BF16), jnp.where(lane >= 64, f, r).astype(BF16)


GROUP = 4
GROUP_ROWS = GROUP * WINDOW


def _stack_heads(ref, g, lane):
    parts = []
    for pb in (2 * g, 2 * g + 1):
        blk = ref[:, LANES * pb:LANES * (pb + 1)]
        zero = jnp.zeros_like(blk)
        parts += [jnp.where(lane < 64, blk, zero), jnp.where(lane >= 64, blk, zero)]
    return jnp.concatenate(parts, axis=0)


def _swa_band(a_ref, ap_ref, g, lane):
    k = jnp.concatenate([_dup_halves(ap_ref[:, 512:640], lane)[g], _dup_halves(a_ref[:, 512:640], lane)[g]], axis=0)
    v = jnp.concatenate([_dup_halves(ap_ref[:, 640:768], lane)[g], _dup_halves(a_ref[:, 640:768], lane)[g]], axis=0)
    return k, v


def _swa_logits(q, k, has_prev):
    sc = _dot(q, k, NT) * SCALE
    rr = lax.broadcasted_iota(jnp.int32, sc.shape, 0) % WINDOW
    cc = lax.broadcasted_iota(jnp.int32, sc.shape, 1)
    valid = (cc > rr) & (cc <= rr + WINDOW) & (has_prev | (cc >= WINDOW))
    return jnp.where(valid, sc, NEG)


def _per_head_column(values):
    return jnp.concatenate([jnp.broadcast_to(v, (WINDOW, 1)) for v in values], axis=0)


SWA_BLOCKS = 4
SWA_ROWS = SWA_BLOCKS * WINDOW


def _swa_blocks(a_ref, ap_ref):
    return [ap_ref] + [a_ref.at[pl.ds(WINDOW * jb, WINDOW), :] for jb in range(SWA_BLOCKS)]


def _swa_fwd(a, sinks):
    s = a.shape[0]

    def body(sink_ref, a_ref, ap_ref, o_ref, l_ref):
        lane = lax.broadcasted_iota(jnp.int32, (WINDOW, LANES), 1)
        blocks = _swa_blocks(a_ref, ap_ref)
        units = [(jb, g) for jb in range(SWA_BLOCKS) for g in range(2)]
        sinks_col = [_per_head_column([sink_ref[GROUP * g + hh] for hh in range(GROUP)]) for g in range(2)]
        bands = [_swa_band(blocks[jb + 1], blocks[jb], g, lane) for jb, g in units]
        scs = [_swa_logits(_stack_heads(blocks[jb + 1], g, lane), bands[u][0],
                           (pl.program_id(0) > 0) if jb == 0 else True) for u, (jb, g) in enumerate(units)]
        ms = [jnp.maximum(jnp.max(scs[u], axis=-1, keepdims=True), sinks_col[g]) for u, (jb, g) in enumerate(units)]
        ps = [jnp.exp(scs[u] - ms[u]) for u in range(len(units))]
        dens = [jnp.sum(ps[u], axis=-1, keepdims=True) + jnp.exp(sinks_col[g] - ms[u]) for u, (jb, g) in enumerate(units)]
        outs = [_dot((ps[u] * (1.0 / dens[u])).astype(BF16), bands[u][1]) for u in range(len(units))]
        for jb in range(SWA_BLOCKS):
            rows = slice(WINDOW * jb, WINDOW * (jb + 1))
            l_all = jnp.zeros((WINDOW, LANES), F32)
            for g in range(2):
                u = 2 * jb + g
                lcol = ms[u] + jnp.log(dens[u])
                for pb in range(2):
                    r0 = 2 * pb * WINDOW
                    o_ref[rows, LANES * (2 * g + pb):LANES * (2 * g + pb + 1)] = jnp.where(
                        lane < 64, outs[u][r0:r0 + WINDOW], outs[u][r0 + WINDOW:r0 + 2 * WINDOW]).astype(BF16)
                for hh in range(GROUP):
                    l_all = jnp.where(lane == GROUP * g + hh, lcol[WINDOW * hh:WINDOW * (hh + 1)], l_all)
            l_ref[rows, :] = l_all

    return pl.pallas_call(
        body, name="swa_fwd", grid=(s // SWA_ROWS,),
        out_shape=[jax.ShapeDtypeStruct((s, 512), BF16), jax.ShapeDtypeStruct((s, LANES), F32)],
        in_specs=[pl.BlockSpec(memory_space=pltpu.SMEM),
                  pl.BlockSpec((SWA_ROWS, W_A), lambda i: (i, 0)),
                  pl.BlockSpec((WINDOW, W_A), lambda i: (jnp.maximum(SWA_BLOCKS * i - 1, 0), 0))],
        out_specs=[pl.BlockSpec((SWA_ROWS, 512), lambda i: (i, 0)), pl.BlockSpec((SWA_ROWS, LANES), lambda i: (i, 0))],
        compiler_params=_params(("parallel",)),
    )(sinks, a, a)


def _mid(att_a, att_b, g, x, target, gate, g_final, wo_a, wo_b, w_out, tm=256):
    s = x.shape[0]
    nt = s // tm

    def body(aa_ref, ab_ref, g_ref, x_ref, t_ref, gate_ref, gf_ref, woa_ref, wob_ref, wout_ref,
             dx_ref, daa_ref, dab_ref, dg_ref, delta_ref, dwoa_ref, dwob_ref, dwout_ref, vec_ref,
             acc_gf, acc_gate, acc_loss):
        step = pl.program_id(0)

        @pl.when(step == 0)
        def _():
            dwoa_ref[...] = jnp.zeros_like(dwoa_ref)
            dwob_ref[...] = jnp.zeros_like(dwob_ref)
            dwout_ref[...] = jnp.zeros_like(dwout_ref)
            acc_gf[...] = jnp.zeros_like(acc_gf)
            acc_gate[...] = jnp.zeros_like(acc_gate)
            acc_loss[...] = jnp.zeros_like(acc_loss)

        def fold(v):
            return jnp.sum(v.reshape(tm // SUBLANES, SUBLANES, D_MODEL), axis=0)

        gate = gate_ref[...]
        gfin = gf_ref[...]
        branches = []
        for att_ref, z_off, wo_ref in ((aa_ref, 0, woa_ref), (ab_ref, 512, wob_ref)):
            att = att_ref[...].astype(F32)
            z = g_ref[:, z_off:z_off + 512].astype(F32)
            sz = _sigmoid(z)
            silu = z * sz
            u = (att * silu).astype(BF16)
            branches.append((att, z, sz, silu, u, _dot(u, wo_ref[...])))
        ga = g_ref[:, 1024:2048].astype(F32)
        gb = g_ref[:, 2048:3072].astype(F32)
        sga, sgb = _sigmoid(ga), _sigmoid(gb)
        y_a, y_b = branches[0][5], branches[1][5]
        mb = (sga * y_a + sgb * y_b).astype(BF16)
        o = _dot(mb, wout_ref[...])
        x2 = x_ref[...] + gate * o
        r2 = lax.rsqrt(jnp.mean(x2 * x2, axis=-1, keepdims=True) + NORM_EPS)
        xn2 = x2 * r2
        err = xn2 * gfin - t_ref[...]
        acc_loss[...] += fold(err * err)
        dy = err * (1.0 / D_MODEL)
        acc_gf[...] += fold(dy * xn2)
        dxn = dy * gfin
        dx2 = r2 * (dxn - xn2 * jnp.mean(dxn * xn2, axis=-1, keepdims=True))
        dx_ref[...] = dx2
        acc_gate[...] += fold(dx2 * o)
        d_o = (dx2 * gate).astype(BF16)
        dwout_ref[...] += _dot(mb, d_o, TN)
        dm = _dot(d_o, wout_ref[...], NT)
        dg_ref[:, 1024:2048] = (dm * y_a * sga * (1.0 - sga)).astype(BF16)
        dg_ref[:, 2048:3072] = (dm * y_b * sgb * (1.0 - sgb)).astype(BF16)
        for (att, z, sz, silu, u, _), sg, wo_ref, dwo_ref, datt_ref, z_off in (
                (branches[0], sga, woa_ref, dwoa_ref, daa_ref, 0), (branches[1], sgb, wob_ref, dwob_ref, dab_ref, 512)):
            dyb = (dm * sg).astype(BF16)
            dwo_ref[...] += _dot(u, dyb, TN)
            du = _dot(dyb, wo_ref[...], NT)
            datt = du * silu
            datt_ref[...] = datt.astype(BF16)
            dg_ref[:, z_off:z_off + 512] = (du * att * (sz * (1.0 + z * (1.0 - sz)))).astype(BF16)
            if z_off == 512:
                prod = datt * att
                hi = prod.astype(BF16)
                lo = (prod - hi.astype(F32)).astype(BF16)
                er = lax.broadcasted_iota(jnp.int32, (512, LANES), 0)
                ec = lax.broadcasted_iota(jnp.int32, (512, LANES), 1)
                e = (er // HEAD_DIM == ec).astype(BF16)
                delta = _dot(hi, e) + _dot(lo, e)
                delta_ref[...] = delta.T[0:SUBLANES, :]

        @pl.when(step == nt - 1)
        def _():
            sub = lax.broadcasted_iota(jnp.int32, (SUBLANES, D_MODEL), 0)
            dgf = jnp.sum(acc_gf[...], axis=0, keepdims=True)
            dgate = jnp.sum(acc_gate[...], axis=0, keepdims=True)
            loss = 0.5 * jnp.sum(acc_loss[...]) * (1.0 / D_MODEL)
            vec_ref[...] = jnp.where(sub == 0, dgf, jnp.where(sub == 1, dgate, jnp.where(sub == 2, loss, 0.0)))

    row = lambda w: pl.BlockSpec((tm, w), lambda i: (i, 0))
    return pl.pallas_call(
        body, name="mid", grid=(nt,),
        out_shape=[jax.ShapeDtypeStruct((s, D_MODEL), F32), jax.ShapeDtypeStruct((s, 512), BF16),
                   jax.ShapeDtypeStruct((s, 512), BF16), jax.ShapeDtypeStruct((s, W_G), BF16),
                   jax.ShapeDtypeStruct((SUBLANES, s), F32),
                   jax.ShapeDtypeStruct((512, D_MODEL), F32), jax.ShapeDtypeStruct((512, D_MODEL), F32),
                   jax.ShapeDtypeStruct((D_MODEL, D_MODEL), F32), jax.ShapeDtypeStruct((SUBLANES, D_MODEL), F32)],
        in_specs=[row(512), row(512), row(W_G), row(D_MODEL), row(D_MODEL),
                  _const_spec((1, D_MODEL)), _const_spec((1, D_MODEL)),
                  _const_spec((512, D_MODEL)), _const_spec((512, D_MODEL)), _const_spec((D_MODEL, D_MODEL))],
        out_specs=[row(D_MODEL), row(512), row(512), row(W_G),
                   pl.BlockSpec((SUBLANES, tm), lambda i: (0, i)),
                   pl.BlockSpec((512, D_MODEL), lambda i: (0, 0)), pl.BlockSpec((512, D_MODEL), lambda i: (0, 0)),
                   pl.BlockSpec((D_MODEL, D_MODEL), lambda i: (0, 0)), pl.BlockSpec((SUBLANES, D_MODEL), lambda i: (0, 0))],
        scratch_shapes=[pltpu.VMEM((SUBLANES, D_MODEL), F32)] * 3,
        compiler_params=_params(("arbitrary",), VMEM_LIMIT),
    )(att_a, att_b, g, x, target, gate, g_final, wo_a, wo_b, w_out)


def _rope_bwd(dt, cos, sin, lane):
    u = dt * sin
    lo = (lane % HEAD_DIM) < (HEAD_DIM // 2)
    return dt * cos + jnp.where(lo, pltpu.roll(u, 96, 1), -pltpu.roll(u, 32, 1))


def _swa_bwd(a, datt, l_all, sinks, cos, sin):
    s = a.shape[0]
    nt = s // SWA_ROWS

    def body(sink_ref, a_ref, ap_ref, do_ref, l_ref, cos_ref, sin_ref, da_ref, ds_ref, halo):
        step = pl.program_id(0)
        tile = nt - 1 - step

        @pl.when(step == 0)
        def _():
            halo[...] = jnp.zeros_like(halo)
            ds_ref[...] = jnp.zeros_like(ds_ref)

        lane = lax.broadcasted_iota(jnp.int32, (WINDOW, LANES), 1)
        sub8 = lax.broadcasted_iota(jnp.int32, (SUBLANES, LANES), 0)
        lane8 = lax.broadcasted_iota(jnp.int32, (SUBLANES, LANES), 1)
        blocks = _swa_blocks(a_ref, ap_ref)
        dsink = jnp.zeros((SUBLANES, LANES), F32)

        def join(pair, r0):
            x0, x1 = pair[0][r0:r0 + WINDOW], pair[1][r0:r0 + WINDOW]
            return jnp.where(lane < 64, x0 + pltpu.roll(x0, 64, 1), x1 + pltpu.roll(x1, 64, 1))

        units = [(jb, g) for jb in range(SWA_BLOCKS) for g in range(2)]
        n_u = len(units)
        sinks_col = [_per_head_column([sink_ref[GROUP * g + hh] for hh in range(GROUP)]) for g in range(2)]
        bands = [_swa_band(blocks[jb + 1], blocks[jb], g, lane) for jb, g in units]
        qs = [_stack_heads(blocks[jb + 1], g, lane) for jb, g in units]
        doms = [_stack_heads(do_ref.at[pl.ds(WINDOW * jb, WINDOW), :], g, lane) for jb, g in units]
        lcols = []
        for jb, g in units:
            lv = l_ref[WINDOW * jb:WINDOW * (jb + 1), :]
            lcols.append(_per_head_column([lv[:, GROUP * g + hh:GROUP * g + hh + 1] for hh in range(GROUP)]))
        ps = [jnp.exp(_swa_logits(qs[u], bands[u][0], (tile > 0) if jb == 0 else True) - lcols[u])
              for u, (jb, g) in enumerate(units)]
        dps = [_dot(doms[u], bands[u][1], NT) for u in range(n_u)]
        deltas = [jnp.sum(ps[u] * dps[u], axis=-1, keepdims=True) for u in range(n_u)]
        for u, (jb, g) in enumerate(units):
            sink_term = jnp.exp(sinks_col[g] - lcols[u]) * deltas[u]
            for hh in range(GROUP):
                tot = jnp.sum(sink_term[WINDOW * hh:WINDOW * (hh + 1)])
                dsink = dsink + jnp.where((sub8 == 0) & (lane8 == GROUP * g + hh), -tot, 0.0)
        dss = [(ps[u] * (dps[u] - deltas[u])).astype(BF16) for u in range(n_u)]
        dqs = [_dot(dss[u], bands[u][0]) * SCALE for u in range(n_u)]
        dks = [_dot(dss[u], qs[u], TN) * SCALE for u in range(n_u)]
        dvs = [_dot(ps[u].astype(BF16), doms[u], TN) for u in range(n_u)]

        carry_k, carry_v = halo[:, 0:LANES], halo[:, LANES:2 * LANES]
        for jb in reversed(range(SWA_BLOCKS)):
            rows = slice(WINDOW * jb, WINDOW * (jb + 1))
            cosv, sinv = cos_ref[rows, :], sin_ref[rows, :]
            for g in range(2):
                dq = dqs[2 * jb + g]
                for pb in range(2):
                    r0 = 2 * pb * WINDOW
                    dq_pair = jnp.where(lane < 64, dq[r0:r0 + WINDOW], dq[r0 + WINDOW:r0 + 2 * WINDOW])
                    da_ref[rows, LANES * (2 * g + pb):LANES * (2 * g + pb + 1)] = _rope_bwd(
                        dq_pair, cosv, sinv, lane).astype(BF16)
            dkb, dvb = dks[2 * jb:2 * jb + 2], dvs[2 * jb:2 * jb + 2]
            da_ref[rows, 512:640] = _rope_bwd(join(dkb, WINDOW) + carry_k, cosv, sinv, lane).astype(BF16)
            da_ref[rows, 640:768] = (join(dvb, WINDOW) + carry_v).astype(BF16)
            carry_k, carry_v = join(dkb, 0), join(dvb, 0)
        halo[:, 0:LANES] = carry_k
        halo[:, LANES:2 * LANES] = carry_v
        ds_ref[...] += dsink

    rev = lambda w: pl.BlockSpec((SWA_ROWS, w), lambda i: (nt - 1 - i, 0))
    return pl.pallas_call(
        body, name="swa_bwd", grid=(nt,),
        out_shape=[jax.ShapeDtypeStruct((s, W_A), BF16), jax.ShapeDtypeStruct((SUBLANES, LANES), F32)],
        in_specs=[pl.BlockSpec(memory_space=pltpu.SMEM), rev(W_A),
                  pl.BlockSpec((WINDOW, W_A), lambda i: (jnp.maximum(SWA_BLOCKS * (nt - 1 - i) - 1, 0), 0)),
                  rev(512), rev(LANES), rev(LANES), rev(LANES)],
        out_specs=[rev(W_A), pl.BlockSpec((SUBLANES, LANES), lambda i: (0, 0))],
        scratch_shapes=[pltpu.VMEM((WINDOW, 2 * LANES), F32)],
        compiler_params=_params(("arbitrary",)),
    )(sinks, a, a, datt, l_all, cos, sin)


def _fox_bwd(qa, ka, vb, do, lse, delta, ranges, t):
    s = qa.shape[0]
    nt = s // t

    def body(rg_ref, q_ref, do_ref, lse_ref, dl_ref, k_ref, v_ref, dq_ref, dk_ref, dv_ref, dc_ref, dr_ref, dq_acc):
        p = pl.program_id(0)
        j = pl.program_id(1)
        n_queries = [jnp.clip(_lane_scalar(rg_ref[0], 2 + hh, j), 1, nt - j) for hh in range(2)]

        @pl.when(j == 0)
        def _():
            dq_acc[...] = jnp.zeros_like(dq_acc)

        lane = lax.broadcasted_iota(jnp.int32, (t, LANES), 1)
        rows = lax.broadcasted_iota(jnp.int32, (t, t), 0)
        cols = lax.broadcasted_iota(jnp.int32, (t, t), 1)
        kt = k_ref[...]
        vt = v_ref[...]

        ks = [kt[:, LANES * hh:LANES * (hh + 1)] for hh in range(2)]

        def tile(qis, carry, heads=(0, 1), diagonal=False):
            dk0, dk1, dv = carry
            offs = [pl.multiple_of(i * t, t) for i in qis]
            units = [(u, hh) for u in range(len(qis)) for hh in heads]
            qts = [q_ref[pl.ds(off, t), :] for off in offs]
            dos = [do_ref[pl.ds(off, t), :] for off in offs]
            lses = [lse_ref[0, :, pl.ds(off, t)] for off in offs]
            dls = [dl_ref[0, :, pl.ds(off, t)] for off in offs]
            qs = [qts[u][:, LANES * hh:LANES * (hh + 1)] for u, hh in units]
            doms = [jnp.where((lane < 64) if hh == 0 else (lane >= 64), dos[u], jnp.zeros_like(dos[u])) for u, hh in units]
            sts = [_dot(ks[hh], qs[n], NT) for n, (u, hh) in enumerate(units)]
            dpts = [_dot(vt, doms[n], NT) for n in range(len(units))]
            if diagonal:
                sts = [jnp.where(cols >= rows, st, NEG) for st in sts]
            pts = [jnp.exp(sts[n] - lses[u][hh:hh + 1, :]) for n, (u, hh) in enumerate(units)]
            dsts = [(pts[n] * (dpts[n] - dls[u][hh:hh + 1, :])).astype(BF16) for n, (u, hh) in enumerate(units)]
            for n, (u, hh) in enumerate(units):
                dv = dv + _dot(pts[n].astype(BF16), doms[n])
                term = _dot(dsts[n], qs[n])
                dk0, dk1 = (dk0 + term, dk1) if hh == 0 else (dk0, dk1 + term)
                dq_acc[hh, pl.ds(offs[u], t), :] += _dot(dsts[n], ks[hh], TN)
            return dk0, dk1, dv

        def diagonal_tile():
            hf = t // 2
            off = pl.multiple_of(j * t, t)
            tri = cols[0:hf, 0:hf] >= rows[0:hf, 0:hf]
            zero = jnp.zeros((t, LANES), F32)
            dk, dv = [zero, zero], zero
            for k1, q0 in ((t, hf), (hf, 0)):
                qoff = off + q0
                qt = q_ref[pl.ds(qoff, hf), :]
                dot_ = do_ref[pl.ds(qoff, hf), :]
                lse_t = lse_ref[0, :, pl.ds(qoff, hf)]
                dl_t = dl_ref[0, :, pl.ds(qoff, hf)]
                lane_h = lane[0:hf]
                pad = [] if k1 == t else [jnp.zeros((t - k1, LANES), F32)]
                for hh in range(2):
                    q = qt[:, LANES * hh:LANES * (hh + 1)]
                    k = ks[hh][0:k1]
                    st = _dot(k, q, NT)
                    if k1 == t:
                        st = jnp.concatenate([st[0:hf], jnp.where(tri, st[hf:], NEG)], axis=0)
                    else:
                        st = jnp.where(tri, st, NEG)
                    pt = jnp.exp(st - lse_t[hh:hh + 1, :])
                    dom = jnp.where((lane_h < 64) if hh == 0 else (lane_h >= 64), dot_, jnp.zeros_like(dot_))
                    dst = (pt * (_dot(vt[0:k1], dom, NT) - dl_t[hh:hh + 1, :])).astype(BF16)
                    dv = dv + jnp.concatenate([_dot(pt.astype(BF16), dom)] + pad, axis=0)
                    dk[hh] = dk[hh] + jnp.concatenate([_dot(dst, q)] + pad, axis=0)
                    dq_acc[hh, pl.ds(qoff, hf), :] += _dot(dst, k, TN)
            return dk[0], dk[1], dv

        carry = diagonal_tile()
        n_rest = jnp.minimum(n_queries[0], n_queries[1]) - 1
        carry = lax.fori_loop(0, n_rest // 2, lambda u, cr: tile([j + 1 + 2 * u, j + 2 + 2 * u], cr), carry)
        carry = lax.fori_loop(0, n_rest % 2, lambda u, cr: tile([j + n_rest], cr), carry)
        for hh in range(2):
            carry = lax.fori_loop(j + 1 + n_rest, j + n_queries[hh], lambda i, cr, hh=hh: tile([i], cr, heads=(hh,)), carry)
        dk0, dk1, dv = carry
        e0, e1 = _aug_lane(0), _aug_lane(1)
        dk_ref[...] = jnp.where(lane < 64, dk0, dk1).astype(BF16)
        dv_ref[...] = dv.astype(BF16)
        c0 = jnp.broadcast_to(dk0[:, e0 + 3:e0 + 4], (t, LANES))
        c1 = jnp.broadcast_to(dk1[:, e1 + 3:e1 + 4], (t, LANES))
        dc_ref[0] = jnp.where(lane == 2 * p, -c0, jnp.where(lane == 2 * p + 1, -c1, 0.0))

        @pl.when(j == nt - 1)
        def _():
            lane_s = lax.broadcasted_iota(jnp.int32, (s, LANES), 1)
            a0, a1 = dq_acc[0], dq_acc[1]
            dq_ref[...] = (jnp.where(lane_s < 64, a0, a1) * SCALE).astype(BF16)
            r0 = jnp.broadcast_to(a0[:, e0:e0 + 1], (s, LANES))
            r1 = jnp.broadcast_to(a1[:, e1:e1 + 1], (s, LANES))
            dr_ref[0] = jnp.where(lane_s == 2 * p, r0, jnp.where(lane_s == 2 * p + 1, r1, 0.0))

    return pl.pallas_call(
        body, name="fox_bwd", grid=(4, nt),
        out_shape=[jax.ShapeDtypeStruct((s, 512), BF16), jax.ShapeDtypeStruct((s, 512), BF16),
                   jax.ShapeDtypeStruct((s, 512), BF16), jax.ShapeDtypeStruct((4, s, LANES), F32),
                   jax.ShapeDtypeStruct((4, s, LANES), F32)],
        in_specs=[pl.BlockSpec((1, SUBLANES, LANES), lambda p, j: (p, 0, 0)),
                  pl.BlockSpec((s, 2 * LANES), lambda p, j: (0, p)),
                  pl.BlockSpec((s, LANES), lambda p, j: (0, p)),
                  pl.BlockSpec((1, SUBLANES, s), lambda p, j: (p, 0, 0)),
                  pl.BlockSpec((1, SUBLANES, s), lambda p, j: (p, 0, 0)),
                  pl.BlockSpec((t, 2 * LANES), lambda p, j: (j, p)),
                  pl.BlockSpec((t, LANES), lambda p, j: (j, p))],
        out_specs=[pl.BlockSpec((s, LANES), lambda p, j: (0, p)),
                   pl.BlockSpec((t, LANES), lambda p, j: (j, p)),
                   pl.BlockSpec((t, LANES), lambda p, j: (j, p)),
                   pl.BlockSpec((1, t, LANES), lambda p, j: (p, j, 0)),
                   pl.BlockSpec((1, s, LANES), lambda p, j: (p, 0, 0))],
        scratch_shapes=[pltpu.VMEM((2, s, LANES), F32)],
        compiler_params=_params(("parallel", "arbitrary"), VMEM_LIMIT),
    )(ranges, qa, do, lse, delta, ka, vb)


def _forget_logit_grad(dc_ref, dr_ref, f, b_f, carry):
    tb = f.shape[0]
    lane = lax.broadcasted_iota(jnp.int32, (tb, LANES), 1)
    dc = dc_ref[0] + dr_ref[0]
    for k in range(1, 4):
        dc = dc + (dc_ref[k] + dr_ref[k])
    hi, mid, lo = _split3(dc)
    rows = lax.broadcasted_iota(jnp.int32, (tb, tb), 0)
    cols = lax.broadcasted_iota(jnp.int32, (tb, tb), 1)
    triu = (cols >= rows).astype(BF16)
    dlogf = _dot(triu, hi) + _dot(triu, mid) + _dot(triu, lo) + carry[0:1, :]
    carry[...] = jnp.broadcast_to(dlogf[0:1, :], carry.shape)
    return jnp.where(lane < N_HEADS, dlogf * _sigmoid(-(f + b_f)), 0.0)


def _dh_norm_bwd(d_a, d_q, d_k, d_v, dcum_k, dcum_q, f, bf_pad, d_g, w_t, x, dx2, gnorm, scale1, tm=512):
    s = x.shape[0]
    nt = s // tm

    def body(da_ref, dq_ref, dk_ref, dv_ref, dc_ref, dr_ref, f_ref, bf_ref, dg_ref, w_ref, x_ref, dx2_ref, g_ref, sc_ref,
             gx_ref, vec_ref, df_ref, db_ref, a_sh, a_sc, a_g, carry):
        step = pl.program_id(0)

        @pl.when(step == 0)
        def _():
            a_sh[...] = jnp.zeros_like(a_sh)
            a_sc[...] = jnp.zeros_like(a_sc)
            a_g[...] = jnp.zeros_like(a_g)
            carry[...] = jnp.zeros_like(carry)
            db_ref[...] = jnp.zeros_like(db_ref)

        def fold(v):
            return jnp.sum(v.reshape(tm // SUBLANES, SUBLANES, D_MODEL), axis=0)

        dfb = _forget_logit_grad(dc_ref, dr_ref, f_ref[...], bf_ref[...], carry)
        d_f = dfb.astype(BF16)
        df_ref[...] = d_f
        sub8 = lax.broadcasted_iota(jnp.int32, (SUBLANES, LANES), 0)
        db_ref[...] += jnp.where(sub8 == 0, jnp.sum(dfb, axis=0, keepdims=True), 0.0)
        d_all = jnp.concatenate([da_ref[...], dq_ref[...], dk_ref[...], dv_ref[...], d_f, dg_ref[...]], axis=1)
        dh = _dot(d_all, w_ref[...])
        xv = x_ref[...]
        r = lax.rsqrt(jnp.mean(xv * xv, axis=-1, keepdims=True) + NORM_EPS)
        xn = xv * r
        gn = g_ref[...]
        a_sh[...] += fold(dh)
        a_sc[...] += fold(dh * (xn * gn))
        dn1 = dh * sc_ref[...]
        a_g[...] += fold(dn1 * xn)
        dxn = dn1 * gn
        gx_ref[...] = dx2_ref[...] + r * (dxn - xn * jnp.mean(dxn * xn, axis=-1, keepdims=True))

        @pl.when(step == nt - 1)
        def _():
            sub = lax.broadcasted_iota(jnp.int32, (SUBLANES, D_MODEL), 0)
            v_sh = jnp.sum(a_sh[...], axis=0, keepdims=True)
            v_sc = jnp.sum(a_sc[...], axis=0, keepdims=True)
            v_g = jnp.sum(a_g[...], axis=0, keepdims=True)
            vec_ref[...] = jnp.where(sub == 0, v_sh, jnp.where(sub == 1, v_sc, jnp.where(sub == 2, v_g, 0.0)))

    row = lambda w: pl.BlockSpec((tm, w), lambda i: (nt - 1 - i, 0))
    slabs = pl.BlockSpec((4, tm, LANES), lambda i: (0, nt - 1 - i, 0))
    return pl.pallas_call(
        body, name="dh_norm_bwd", grid=(nt,),
        out_shape=[jax.ShapeDtypeStruct((s, D_MODEL), F32), jax.ShapeDtypeStruct((SUBLANES, D_MODEL), F32),
                   jax.ShapeDtypeStruct((s, LANES), BF16), jax.ShapeDtypeStruct((SUBLANES, LANES), F32)],
        in_specs=[row(W_A), row(512), row(512), row(512), slabs, slabs, row(W_F), _const_spec((1, LANES)), row(W_G),
                  _const_spec((W_INT, D_MODEL)), row(D_MODEL), row(D_MODEL), _const_spec((1, D_MODEL)),
                  _const_spec((1, D_MODEL))],
        out_specs=[row(D_MODEL), pl.BlockSpec((SUBLANES, D_MODEL), lambda i: (0, 0)), row(LANES),
                   pl.BlockSpec((SUBLANES, LANES), lambda i: (0, 0))],
        scratch_shapes=[pltpu.VMEM((SUBLANES, D_MODEL), F32)] * 3 + [pltpu.VMEM((SUBLANES, LANES), F32)],
        compiler_params=_params(("arbitrary",), VMEM_LIMIT),
    )(d_a, d_q, d_k, d_v, dcum_k, dcum_q, f, bf_pad, d_g, w_t, x, dx2, gnorm, scale1)


def _dw_in(h_t, d_a, d_q, d_k, d_v, d_f, d_g, ts=1024, tc=512):
    s = h_t.shape[1]
    ns = s // ts
    w_fg = 512 + W_F + W_G - 512
    rows = 128
    n_slot = (R_END // 4 + LANES - 1) // LANES * LANES
    order = [(0, 0, W_A), (4, 0, 512), (1, 0, 512), (2, 0, 512), (3, 0, 512), (4, 512, N_HEADS), (4, 512 + W_F, W_G - 512)]

    def slot_pieces(k):
        lo, hi, out, col = (R_END // 4) * k, (R_END // 4) * (k + 1), [], 0
        for acc_i, c0, w in order:
            a, b = max(lo, col), min(hi, col + w)
            if a < b:
                out.append((acc_i, c0 + a - col, b - a))
            col += w
        return out

    def body(h_ref, da_ref, dq_ref, dk_ref, dv_ref, df_ref, dg_ref, o_ref, acc_a, acc_q, acc_k, acc_v, acc_fg, stage, sem):
        k = pl.program_id(0)
        accs = (acc_a, acc_q, acc_k, acc_v, acc_fg)

        @pl.when(k == 0)
        def _():
            for acc in accs:
                acc[...] = jnp.zeros_like(acc)

        hv = h_ref[...]

        def add(acc, c_acc, d_ref, c_d, width):
            for c0 in range(0, width, tc):
                w = min(tc, width - c0)
                acc[:, c_acc + c0:c_acc + c0 + w] += _dot(hv, d_ref[:, c_d + c0:c_d + c0 + w])

        add(acc_a, 0, da_ref, 0, W_A)
        add(acc_q, 0, dq_ref, 0, 512)
        add(acc_k, 0, dk_ref, 0, 512)
        add(acc_v, 0, dv_ref, 0, 512)
        add(acc_fg, 0, dg_ref, 0, 512)
        add(acc_fg, 512, df_ref, 0, W_F)
        add(acc_fg, 512 + W_F, dg_ref, 512, W_G - 512)

        @pl.when(k == ns - 1)
        def _():
            pending = [None, None]
            for n, r0 in enumerate(range(0, D_MODEL, rows)):
                buf = n % 2
                if pending[buf] is not None:
                    for cp in pending[buf]:
                        cp.wait()
                rs = slice(r0, r0 + rows)
                copies = []
                for slot in range(4):
                    parts = [accs[acc_i][rs, c0:c0 + w] for acc_i, c0, w in slot_pieces(slot)]
                    parts.append(jnp.zeros((rows, n_slot - R_END // 4), F32))
                    stage[buf, slot] = jnp.concatenate(parts, axis=1)
                    cp = pltpu.make_async_copy(stage.at[buf, slot], o_ref.at[slot, pl.ds(r0, rows), :], sem.at[4 * buf + slot])
                    cp.start()
                    copies.append(cp)
                pending[buf] = copies
            for copies in pending:
                for cp in copies:
                    cp.wait()

    spec = lambda d: pl.BlockSpec((ts, d.shape[1]), lambda k: (k, 0))
    return pl.pallas_call(
        body, name="dw_in", grid=(ns,),
        out_shape=jax.ShapeDtypeStruct((4, D_MODEL, n_slot), F32),
        in_specs=[pl.BlockSpec((D_MODEL, ts), lambda k: (0, k))] + [spec(d) for d in (d_a, d_q, d_k, d_v, d_f, d_g)],
        out_specs=pl.BlockSpec(memory_space=pl.ANY),
        scratch_shapes=[pltpu.VMEM((D_MODEL, W_A), F32)] + [pltpu.VMEM((D_MODEL, 512), F32)] * 3
        + [pltpu.VMEM((D_MODEL, w_fg), F32), pltpu.VMEM((2, 4, rows, n_slot), F32), pltpu.SemaphoreType.DMA((8,))],
        compiler_params=_params(("arbitrary",), VMEM_LIMIT),
    )(h_t, d_a, d_q, d_k, d_v, d_f, d_g)


def _small_grads(packs, c_t, dada_shard):
    def body(p_ref, ct_ref, da_ref, sum_ref, gw_ref):
        acc = p_ref[0]
        for dev in range(1, 8):
            acc = acc + p_ref[dev]
        sum_ref[...] = acc
        gw_ref[...] = jnp.dot(ct_ref[...], da_ref[...], preferred_element_type=F32, precision=lax.Precision.HIGHEST)

    return pl.pallas_call(
        body, name="small_grads",
        out_shape=[jax.ShapeDtypeStruct(packs.shape[1:], F32),
                   jax.ShapeDtypeStruct((c_t.shape[0], dada_shard.shape[1]), F32)],
    )(packs, c_t, dada_shard)


def _adamw_body(w_ref, g_ref, m_ref, v_ref, d_ref, mo_ref, vo_ref):
    c1 = 1.0 / (1.0 - ADAM_B1 ** ADAM_STEP)
    c2 = 1.0 / (1.0 - ADAM_B2 ** ADAM_STEP)
    gv = g_ref[...]
    mn = ADAM_B1 * m_ref[...] + (1.0 - ADAM_B1) * gv
    vn = ADAM_B2 * v_ref[...] + (1.0 - ADAM_B2) * (gv * gv)
    mo_ref[...] = mn
    vo_ref[...] = vn
    d_ref[...] = -ADAM_LR * ((mn * c1) / (jnp.sqrt(vn * c2) + ADAM_EPS) + ADAM_WD * w_ref[...])


def _adamw3(w, g, m, v, name, tb=128):
    spec = pl.BlockSpec((tb, SUBLANES, LANES), lambda i: (i, 0, 0))
    return pl.pallas_call(
        functools.partial(_adamw_body), name=name, grid=(pl.cdiv(w.shape[0], tb),),
        out_shape=[jax.ShapeDtypeStruct(w.shape, F32)] * 3,
        in_specs=[spec] * 4, out_specs=[spec] * 3,
        compiler_params=_params(("parallel",)),
    )(w, g, m, v)


def _adamw_many(items, name):
    n = len(items)

    def body(*refs):
        for i in range(n):
            _adamw_body(*refs[4 * i:4 * i + 4], *refs[4 * n + 3 * i:4 * n + 3 * i + 3])

    return pl.pallas_call(
        body, name=name,
        out_shape=[jax.ShapeDtypeStruct(it[0].shape, F32) for it in items for _ in range(3)],
        compiler_params=_params(vmem=VMEM_LIMIT),
    )(*[arr for it in items for arr in it])


def _rope_inputs(positions):
    inv_freq = 10000.0 ** (-jnp.arange(0, HEAD_DIM, 2, dtype=F32) / HEAD_DIM)
    pos = jnp.broadcast_to(positions.astype(F32)[:, None], (positions.shape[0], LANES))
    return pos, jnp.tile(inv_freq, 4)[None, :]


def _pad_rows(v, rows=SUBLANES):
    return jnp.pad(v, ((0, rows - v.shape[0]), (0, 0)))


def kernel(x, c, positions, w_ada, b_ada, g_norm, w_in, b_f, sinks, w_o_swa, w_o_fox, w_out, g_final, loss_target, m_w_ada, m_b_ada, m_g_norm, m_w_in, m_b_f, m_sinks, m_w_o_swa, m_w_o_fox, m_w_out, m_g_final, v_w_ada, v_b_ada, v_g_norm, v_w_in, v_b_f, v_sinks, v_w_o_swa, v_w_o_fox, v_w_out, v_g_final):
    ix, iy, ic = lax.axis_index("x"), lax.axis_index("y"), lax.axis_index("c")
    chip = 2 * ix + iy
    dev = 2 * chip + ic
    xs, tgt = x[0], loss_target[0]
    s = xs.shape[0]

    b_ada_shard = lax.dynamic_slice(b_ada, (0, chip * 768), (1, 768))
    ada_parts, w_int, w_int_t, g_oa, g_ob, g_out = _gather_inputs(
        _pad_rows(c), w_ada[0], b_ada_shard, jnp.transpose(w_in, (2, 0, 1)).reshape(-1, LANES),
        [w_o_swa[0], w_o_fox[0], w_out[0]], "gather_inputs")
    ada = lax.dynamic_index_in_dim(ada_parts, dev, axis=1, keepdims=False).reshape(1, 3 * D_MODEL)
    shift, scale, gate = ada[:, :D_MODEL], ada[:, D_MODEL:2 * D_MODEL], ada[:, 2 * D_MODEL:]
    scale1 = 1.0 + scale

    wo_a = jnp.transpose(g_oa, (1, 0, 2)).reshape(512, D_MODEL)
    wo_b = jnp.transpose(g_ob, (1, 0, 2)).reshape(512, D_MODEL)
    w_o = g_out.reshape(D_MODEL, D_MODEL)

    pos, freq = _rope_inputs(positions[0])
    bf_pad = jnp.pad(b_f, ((0, 0), (0, LANES - N_HEADS)))
    sink_vec = sinks[0]

    a, vb, f, g, h_t, cos, sin, qa, ka, va, stats = _norm_proj(
        xs, g_norm * scale1, shift, w_int, pos, freq, bf_pad, FOX_TILE)
    att_a, l_swa = _swa_fwd(a, sink_vec)
    ranges = _fox_tile_ranges(stats)
    att_b, lse = _fox_fwd(qa, ka, va, ranges, FOX_TILE)

    dx2, datt_a, datt_b, d_g, delta8, dwo_a, dwo_b, dw_out, vec_mid = _mid(
        att_a, att_b, g, xs, tgt, gate, g_final.reshape(1, D_MODEL), wo_a, wo_b, w_o)
    delta = jnp.pad(delta8.reshape(4, 2, s), ((0, 0), (0, SUBLANES - 2), (0, 0)))
    d_a, dsink = _swa_bwd(a, datt_a, l_swa, sink_vec, cos, sin)
    dq, dk, dv, dcum_k, dcum_q = _fox_bwd(qa, ka, vb, datt_b, lse, delta, ranges, FOX_TILE)
    grad_x, vec_dh, d_f, dbf = _dh_norm_bwd(
        d_a, dq, dk, dv, dcum_k, dcum_q, f, bf_pad, d_g, w_int_t, xs, dx2, g_norm, scale1)
    dw_in_slots = _dw_in(h_t, d_a, dq, dk, dv, d_f, d_g)

    tail = jnp.pad(jnp.concatenate([dbf[0:1, :N_HEADS], dsink[0:1, :N_HEADS]], axis=1), ((0, 0), (0, D_MODEL - 2 * N_HEADS)))
    pack = jnp.concatenate([c, vec_dh[0:2], vec_mid[1:2], vec_dh[2:3], vec_mid[0:1], tail, vec_mid[2:3]], axis=0)

    def slots(w, axis):
        if axis == 1:
            return jnp.transpose(w.reshape(w.shape[0], 4, w.shape[1] // 4), (1, 0, 2))
        return w.reshape(4, w.shape[0] // 4, w.shape[1])

    packs, g_wo_a, g_wo_b, g_w_out, g_w_in = _reduce_scatter(
        [slots(dwo_a, 1), slots(dwo_b, 1), slots(dw_out, 0), dw_in_slots], pack, "reduce_grads")
    g_w_in = g_w_in[:, :w_in.shape[2]]
    dada_all = packs[:, 1:4, :].reshape(8, 3 * D_MODEL)
    dada_shard = lax.dynamic_slice(dada_all, (0, chip * 768), (8, 768))
    sums, g_w_ada = _small_grads(packs, packs[:, 0, :].T, dada_shard)
    g_b_ada = sums[1:4].reshape(1, 3 * D_MODEL)
    g_g_norm = sums[4:5]
    g_g_final = sums[5]
    g_b_f = sums[6:7, :N_HEADS]
    g_sinks = sums[6:7, N_HEADS:2 * N_HEADS]
    loss = sums[7, 0]

    grads = {
        "w_ada": g_w_ada, "b_ada": g_b_ada, "g_norm": g_g_norm, "w_in": g_w_in, "b_f": g_b_f, "sinks": g_sinks,
        "w_o_swa": g_wo_a, "w_o_fox": g_wo_b, "w_out": g_w_out, "g_final": g_g_final,
    }
    params = {
        "w_ada": (w_ada, m_w_ada, v_w_ada), "b_ada": (b_ada, m_b_ada, v_b_ada), "g_norm": (g_norm, m_g_norm, v_g_norm),
        "w_in": (w_in, m_w_in, v_w_in), "b_f": (b_f, m_b_f, v_b_f), "sinks": (sinks, m_sinks, v_sinks),
        "w_o_swa": (w_o_swa, m_w_o_swa, v_w_o_swa), "w_o_fox": (w_o_fox, m_w_o_fox, v_w_o_fox),
        "w_out": (w_out, m_w_out, v_w_out), "g_final": (g_final, m_g_final, v_g_final),
    }
    n_col = w_in.shape[2]

    def as_stored(t):
        return jnp.transpose(t, (2, 0, 1)).reshape(n_col, SUBLANES, LANES)

    def from_stored(t):
        return jnp.transpose(t, (1, 2, 0)).reshape(1, D_MODEL, n_col)

    names = list(grads)
    others = [nm for nm in names if nm != "w_in"]

    def as_2d(t):
        return t.reshape((t.shape[-2], t.shape[-1]) if t.ndim >= 2 else (1, t.shape[0]))

    flat = _adamw_many([tuple(as_2d(t) for t in (params[nm][0], grads[nm], params[nm][1], params[nm][2])) for nm in others],
                       "adamw_small")
    results = {}
    for i, nm in enumerate(others):
        shape = params[nm][0].shape
        results[nm] = [t.reshape(shape) for t in (grads[nm], *flat[3 * i:3 * i + 3])]
    w, m, v = params["w_in"]
    g_st = as_stored(grads["w_in"][None])
    d_, m_, v_ = _adamw3(as_stored(w), g_st, as_stored(m), as_stored(v), "adamw_w_in")
    results["w_in"] = [from_stored(t) for t in (g_st, d_, m_, v_)]
    return (loss, grad_x[None], *[results[nm][0] for nm in names], *[results[nm][1] for nm in names],
            *[results[nm][2] for nm in names], *[results[nm][3] for nm in names])
```

```python
import functools

import numpy as np
import jax
import jax.numpy as jnp
from jax import lax
from jax.experimental import pallas as pl
from jax.experimental.pallas import tpu as pltpu

F32 = jnp.float32
BF16 = jnp.bfloat16
MESH = pl.DeviceIdType.MESH

D_MODEL = 1024
HEAD_DIM = 64
N_HEADS = 8
WINDOW = 128
NORM_EPS = 1e-6
SCALE = HEAD_DIM ** -0.5
NEG = -1e30
LANES = 128
SUBLANES = 8
VMEM_LIMIT = 60 * 1024 * 1024
FOX_TILE = 512

W_A, W_B, W_F, W_G = 768, 1536, 128, 3072
OFF_A, OFF_B, OFF_F, OFF_G = 0, 768, 2304, 2432
W_INT = W_A + W_B + W_F + W_G
R_ZA, R_QB, R_FB, R_ZB, R_END = 768, 1280, 2816, 2824, 5384

ADAM_LR, ADAM_B1, ADAM_B2, ADAM_EPS, ADAM_WD, ADAM_STEP = 0.001, 0.9, 0.999, 1e-08, 0.01, 10

NT = (((1,), (1,)), ((), ()))
TN = (((0,), (0,)), ((), ()))


def _dot(a, b, dims=None):
    if dims is None:
        return jnp.dot(a, b, preferred_element_type=F32)
    return lax.dot_general(a, b, dims, preferred_element_type=F32)


def _split3(v):
    hi = v.astype(BF16)
    r1 = v - hi.astype(F32)
    mid = r1.astype(BF16)
    lo = (r1 - mid.astype(F32)).astype(BF16)
    return hi, mid, lo


def _sigmoid(v):
    return 1.0 / (1.0 + jnp.exp(-v))


def _params(sem=None, vmem=None):
    return pltpu.CompilerParams(dimension_semantics=sem, vmem_limit_bytes=vmem)


def _const_spec(shape):
    nd = len(shape)
    return pl.BlockSpec(shape, lambda *_: (0,) * nd, pipeline_mode=pl.Buffered(1))


def _flip(v, f):
    return 1 - v if f else v


_CHIP_FLIPS = ((1, 0), (0, 1), (1, 1))


def _gather_inputs(c_pad, w_ada, b_ada_shard, w_in_shard, small_shards, name):
    shards = [w_in_shard] + list(small_shards)
    n = len(shards)
    n_col = w_ada.shape[1]
    shard_w = w_in_shard.shape[0] // SUBLANES
    rows = 128

    def body(*refs):
        c_ref, wa_ref, ba_ref = refs[:3]
        ins = refs[3:3 + n]
        ada_ref, wint_ref, wintt_ref = refs[3 + n:6 + n]
        g_in, call_ref, send_sems, recv_sems = refs[5 + 2 * n:9 + 2 * n]
        outs = (g_in,) + tuple(refs[6 + n:5 + 2 * n])
        x, y, c = lax.axis_index("x"), lax.axis_index("y"), lax.axis_index("c")
        k_me = 2 * x + y
        me = 2 * k_me + c
        sibling = (x, y, 1 - c)
        chips = [(_flip(x, fx), _flip(y, fy)) for fx, fy in _CHIP_FLIPS]

        def piece(i, chip_k, half):
            hr = outs[i].shape[1] // 2
            return outs[i].at[chip_k, pl.ds(half * hr, hr), :]

        def copy(i, slot, chip_k, half, to):
            return pltpu.make_async_remote_copy(
                src_ref=piece(i, chip_k, half), dst_ref=piece(i, chip_k, half),
                send_sem=send_sems.at[6 * i + slot], recv_sem=recv_sems.at[6 * i + slot],
                device_id=to, device_id_type=MESH)

        def small(ref, slot, sem, to):
            return pltpu.make_async_remote_copy(
                src_ref=ref.at[slot], dst_ref=ref.at[slot], send_sem=send_sems.at[6 * n + sem],
                recv_sem=recv_sems.at[6 * n + sem], device_id=to, device_id_type=MESH)

        whole = shard_w // LANES * LANES
        for a in range(SUBLANES):
            main = ins[0][pl.ds(a, whole, stride=SUBLANES), :]
            tail = ins[0][pl.ds(a + SUBLANES * whole, shard_w - whole, stride=SUBLANES), :]
            tail = jnp.concatenate([tail, jnp.zeros((LANES - (shard_w - whole), LANES), F32)], axis=0)
            blk = jnp.concatenate([main.T, tail.T[:, :shard_w - whole]], axis=1)
            g_in[k_me, LANES * a:LANES * (a + 1), :] = blk.astype(BF16)
        for i in range(1, n):
            outs[i][k_me] = ins[i][...].astype(BF16)
        started = []
        for i in range(n):
            for j, chip in enumerate(chips):
                cp = copy(i, j, k_me, c, (chip[0], chip[1], c))
                cp.start()
                started.append(cp)

        call_ref[me] = c_ref[...]
        peers = [(_flip(x, k & 4), _flip(y, k & 2), _flip(c, k & 1)) for k in range(1, 8)]
        for k, peer in enumerate(peers):
            cp = small(call_ref, me, k, peer)
            cp.start()
            started.append(cp)
        for k, peer in enumerate(peers):
            small(call_ref, 4 * peer[0] + 2 * peer[1] + peer[2], k, peer).wait_recv()
        c_all = call_ref[:, 0, :].astype(BF16)
        ada_ref[k_me] = _dot(c_all, wa_ref[...].astype(BF16)) + ba_ref[...]
        for j, chip in enumerate(chips):
            cp = small(ada_ref, k_me, 7 + j, (chip[0], chip[1], c))
            cp.start()
            started.append(cp)

        for j, chip in enumerate(chips):
            chip_k = 2 * chip[0] + chip[1]
            for i in range(n):
                copy(i, j, chip_k, c, (chip[0], chip[1], c)).wait_recv()
                cp = copy(i, 3 + j, chip_k, c, sibling)
                cp.start()
                started.append(cp)
        for j, chip in enumerate(chips):
            chip_k = 2 * chip[0] + chip[1]
            small(ada_ref, chip_k, 7 + j, (chip[0], chip[1], c)).wait_recv()
            for i in range(n):
                copy(i, 3 + j, chip_k, 1 - c, sibling).wait_recv()
        for cp in started:
            cp.wait_send()

        def ref_cols(slots, a, b):
            runs = []
            for k in range(4):
                lo, hi = max(a, shard_w * k), min(b, shard_w * (k + 1))
                if lo < hi:
                    runs.append(slots[k][:, lo - shard_w * k:hi - shard_w * k])
            return runs

        for r0 in range(0, D_MODEL, rows):
            rs = slice(r0, r0 + rows)
            slots = [g_in[k, rs, :] for k in range(4)]
            row = jnp.concatenate(
                ref_cols(slots, 0, R_ZA) + ref_cols(slots, R_QB, R_FB) + ref_cols(slots, R_FB, R_ZB)
                + [jnp.zeros((rows, W_F - N_HEADS), BF16)] + ref_cols(slots, R_ZA, R_QB) + ref_cols(slots, R_ZB, R_END),
                axis=1)
            wint_ref[rs, :] = row
            wintt_ref[:, rs] = row.T

    vmem = pl.BlockSpec(memory_space=pltpu.VMEM)
    return pl.pallas_call(
        body, name=name,
        out_shape=[jax.ShapeDtypeStruct((4, 8, n_col), F32), jax.ShapeDtypeStruct((D_MODEL, W_INT), BF16),
                   jax.ShapeDtypeStruct((W_INT, D_MODEL), BF16)]
        + [jax.ShapeDtypeStruct((4,) + s.shape, BF16) for s in small_shards],
        in_specs=[vmem] * (3 + n),
        out_specs=[vmem] * (2 + n),
        scratch_shapes=[pltpu.VMEM((4, D_MODEL, shard_w), BF16), pltpu.VMEM((8,) + c_pad.shape, F32),
                        pltpu.SemaphoreType.DMA((6 * n + 10,)), pltpu.SemaphoreType.DMA((6 * n + 10,))],
        compiler_params=_params(vmem=VMEM_LIMIT),
    )(c_pad, w_ada, b_ada_shard, *shards)


def _reduce_scatter(pieces, pack, name):
    n = len(pieces)

    def body(*refs):
        pack_ref, ins = refs[0], refs[1:1 + n]
        packs_ref, outs = refs[1 + n], refs[2 + n:2 + 2 * n]
        rest = refs[2 + 2 * n:]
        own, got = rest[:n], rest[n:2 * n]
        sendb, recvb = rest[2 * n:3 * n], rest[3 * n:4 * n]
        send_sems, recv_sems, local_sems = rest[4 * n:4 * n + 3]
        x, y, c = lax.axis_index("x"), lax.axis_index("y"), lax.axis_index("c")
        k_me = 2 * x + y
        me = 2 * k_me + c
        sibling = (x, y, 1 - c)
        chips = [(_flip(x, fx), _flip(y, fy)) for fx, fy in _CHIP_FLIPS]
        hrs = [p.shape[1] // 2 for p in pieces]

        def remote(i, slot, src, dst, to):
            return pltpu.make_async_remote_copy(
                src_ref=src, dst_ref=dst, send_sem=send_sems.at[5 * i + slot], recv_sem=recv_sems.at[5 * i + slot],
                device_id=to, device_id_type=MESH)

        started = []
        packs_ref[me] = pack_ref[...]
        peers = [(_flip(x, k & 4), _flip(y, k & 2), _flip(c, k & 1)) for k in range(1, 8)]
        for k, peer in enumerate(peers):
            cp = pltpu.make_async_remote_copy(
                src_ref=pack_ref, dst_ref=packs_ref.at[me], send_sem=send_sems.at[5 * n + k],
                recv_sem=recv_sems.at[5 * n + k], device_id=peer, device_id_type=MESH)
            cp.start()
            started.append(cp)
        loads = []
        for i in range(n):
            ld = pltpu.make_async_copy(ins[i].at[:, pl.ds(c * hrs[i], hrs[i]), :], own[i], local_sems.at[i])
            ld.start()
            loads.append(ld)
            cp = remote(i, 0, ins[i].at[:, pl.ds((1 - c) * hrs[i], hrs[i]), :], got[i], sibling)
            cp.start()
            started.append(cp)
        for i in range(n):
            loads[i].wait()
            remote(i, 0, ins[i].at[:, pl.ds(c * hrs[i], hrs[i]), :], got[i], sibling).wait_recv()
            for j, chip in enumerate(chips):
                chip_k = 2 * chip[0] + chip[1]
                sendb[i][j] = (own[i][chip_k] + got[i][chip_k]).astype(BF16)
                cp = remote(i, 1 + j, sendb[i].at[j], recvb[i].at[j], (chip[0], chip[1], c))
                cp.start()
                started.append(cp)
        for i in range(n):
            acc = own[i][k_me] + got[i][k_me]
            for j, chip in enumerate(chips):
                remote(i, 1 + j, sendb[i].at[j], recvb[i].at[j], (chip[0], chip[1], c)).wait_recv()
                acc = acc + recvb[i][j].astype(F32)
            mine = outs[i].at[pl.ds(c * hrs[i], hrs[i]), :]
            outs[i][pl.ds(pl.multiple_of(c * hrs[i], SUBLANES), hrs[i]), :] = acc
            cp = remote(i, 4, mine, mine, sibling)
            cp.start()
            started.append(cp)
        for i in range(n):
            theirs = outs[i].at[pl.ds((1 - c) * hrs[i], hrs[i]), :]
            remote(i, 4, theirs, theirs, sibling).wait_recv()
        for k, peer in enumerate(peers):
            pltpu.make_async_remote_copy(
                src_ref=pack_ref, dst_ref=packs_ref.at[4 * peer[0] + 2 * peer[1] + peer[2]],
                send_sem=send_sems.at[5 * n + k], recv_sem=recv_sems.at[5 * n + k],
                device_id=peer, device_id_type=MESH).wait_recv()
        for cp in started:
            cp.wait_send()

    vmem = pl.BlockSpec(memory_space=pltpu.VMEM)
    scratch = []
    scratch += [pltpu.VMEM((4, p.shape[1] // 2, p.shape[2]), F32) for p in pieces]
    scratch += [pltpu.VMEM((4, p.shape[1] // 2, p.shape[2]), F32) for p in pieces]
    scratch += [pltpu.VMEM((3, p.shape[1] // 2, p.shape[2]), BF16) for p in pieces]
    scratch += [pltpu.VMEM((3, p.shape[1] // 2, p.shape[2]), BF16) for p in pieces]
    scratch += [pltpu.SemaphoreType.DMA((5 * n + 7,)), pltpu.SemaphoreType.DMA((5 * n + 7,)), pltpu.SemaphoreType.DMA((n,))]
    return pl.pallas_call(
        body, name=name,
        out_shape=[jax.ShapeDtypeStruct((8,) + pack.shape, F32)] + [jax.ShapeDtypeStruct(p.shape[1:], F32) for p in pieces],
        in_specs=[vmem] + [pl.BlockSpec(memory_space=pl.ANY)] * n,
        out_specs=[vmem] * (1 + n),
        scratch_shapes=scratch,
        compiler_params=_params(vmem=VMEM_LIMIT),
    )(pack, *pieces)


def _rope_fwd(t, cos, sin, lane):
    lo = (lane % HEAD_DIM) < (HEAD_DIM // 2)
    return t * cos + jnp.where(lo, -pltpu.roll(t, 96, 1), pltpu.roll(t, 32, 1)) * sin


def _norm_proj(x, gmod, shift, w_int, pos, freq, bf_pad, tm):
    s = x.shape[0]

    def body(x_ref, g_ref, sh_ref, w_ref, pos_ref, fr_ref, bf_ref,
             a_ref, vb_ref, f_ref, gg_ref, ht_ref, cos_ref, sin_ref, q_ref, k_ref, v_ref, st_ref, carry):
        @pl.when(pl.program_id(0) == 0)
        def _():
            carry[...] = jnp.zeros_like(carry)

        xv = x_ref[...]
        r = lax.rsqrt(jnp.mean(xv * xv, axis=-1, keepdims=True) + NORM_EPS)
        hf = (xv * r) * g_ref[...] + sh_ref[...]
        hb = hf.astype(BF16)
        ht_ref[...] = hb.T
        pa = _dot(hb, w_ref[:, OFF_A:OFF_A + W_A])
        ang = pos_ref[...] * fr_ref[...]
        cosv, sinv = jnp.cos(ang), jnp.sin(ang)
        cos_ref[...] = cosv
        sin_ref[...] = sinv
        lane = lax.broadcasted_iota(jnp.int32, (tm, LANES), 1)
        for j in range(5):
            t = pa[:, LANES * j:LANES * (j + 1)]
            a_ref[:, LANES * j:LANES * (j + 1)] = _rope_fwd(t, cosv, sinv, lane).astype(BF16)
        a_ref[:, 640:768] = pa[:, 640:768].astype(BF16)
        pf = _dot(hb, w_ref[:, OFF_F:OFF_F + W_F])
        f_ref[...] = pf
        bblk = _dot(hb, w_ref[:, OFF_B:OFF_B + W_B]).astype(BF16)
        vb_ref[...] = bblk[:, 1024:1536]
        gg_ref[...] = _dot(hb, w_ref[:, OFF_G:OFF_G + W_G]).astype(BF16)
        _augment_heads(bblk, _cumsum_tile(pf, bf_ref[...], carry), q_ref, k_ref, v_ref, st_ref)

    row = lambda w: pl.BlockSpec((tm, w), lambda i: (i, 0))
    return pl.pallas_call(
        body, name="norm_proj", grid=(s // tm,),
        out_shape=[jax.ShapeDtypeStruct((s, W_A), BF16), jax.ShapeDtypeStruct((s, 512), BF16),
                   jax.ShapeDtypeStruct((s, W_F), F32), jax.ShapeDtypeStruct((s, W_G), BF16),
                   jax.ShapeDtypeStruct((D_MODEL, s), BF16),
                   jax.ShapeDtypeStruct((s, LANES), F32), jax.ShapeDtypeStruct((s, LANES), F32)]
        + [jax.ShapeDtypeStruct((s, 1024), BF16)] * 3 + [jax.ShapeDtypeStruct((s // tm, SUBLANES, LANES), F32)],
        in_specs=[row(D_MODEL), _const_spec((1, D_MODEL)), _const_spec((1, D_MODEL)), _const_spec((D_MODEL, W_INT)),
                  row(LANES), _const_spec((1, LANES)), _const_spec((1, LANES))],
        out_specs=[row(W_A), row(512), row(W_F), row(W_G), pl.BlockSpec((D_MODEL, tm), lambda i: (0, i)),
                   row(LANES), row(LANES), row(1024), row(1024), row(1024),
                   pl.BlockSpec((1, SUBLANES, LANES), lambda i: (i, 0, 0))],
        scratch_shapes=[pltpu.VMEM((SUBLANES, LANES), F32)],
        compiler_params=_params(("arbitrary",), VMEM_LIMIT),
    )(x, gmod, shift, w_int, pos, freq, bf_pad)


def _log_sigmoid(u):
    return jnp.minimum(u, 0.0) - jnp.log(1.0 + jnp.exp(-jnp.abs(u)))


def _cumsum_tile(f, b_f, carry):
    tb = f.shape[0]
    lane = lax.broadcasted_iota(jnp.int32, (tb, LANES), 1)
    logf = jnp.where(lane < N_HEADS, _log_sigmoid(f + b_f), 0.0)
    hi, mid, lo = _split3(logf)
    rows = lax.broadcasted_iota(jnp.int32, (tb, tb), 0)
    cols = lax.broadcasted_iota(jnp.int32, (tb, tb), 1)
    tril = (cols <= rows).astype(BF16)
    cum = _dot(tril, hi) + _dot(tril, mid) + _dot(tril, lo) + carry[0:1, :]
    carry[...] = jnp.broadcast_to(cum[tb - 1:tb, :], carry.shape)
    return cum


def _aug_lane(h):
    return 64 if h % 2 == 0 else 0


def _augment_heads(bblk, cumv, q_ref, k_ref, v_ref, st_ref):
    t = bblk.shape[0]
    lane = lax.broadcasted_iota(jnp.int32, (t, LANES), 1)
    lane_b = lane.astype(BF16)
    sub8 = lax.broadcasted_iota(jnp.int32, (SUBLANES, LANES), 0)
    lane8 = lax.broadcasted_iota(jnp.int32, (SUBLANES, LANES), 1)
    one = jnp.ones((t, LANES), BF16)
    zero = jnp.zeros((t, LANES), BF16)
    stats = jnp.zeros((SUBLANES, LANES), F32)
    for p in range(4):
        qblk = bblk[:, LANES * p:LANES * (p + 1)] * SCALE
        kblk = bblk[:, 512 + LANES * p:512 + LANES * (p + 1)]
        vblk = bblk[:, 1024 + LANES * p:1024 + LANES * (p + 1)]
        qf, kf = qblk.astype(F32), kblk.astype(F32)
        q2, k2, qk = qf * qf, kf * kf, qf * kf
        for odd in range(2):
            h = 2 * p + odd
            a0 = _aug_lane(h)
            data_b = (lane_b < 64) if odd == 0 else (lane_b >= 64)
            data = (lane < 64) if odd == 0 else (lane >= 64)
            hi, mid, lo = _split3(jnp.broadcast_to(cumv[:, h:h + 1], (t, LANES)))
            ones3_q = (lane_b >= a0 + 3) & (lane_b < a0 + 6)
            ones3_k = (lane_b >= a0) & (lane_b < a0 + 3)
            aug_q = jnp.where(lane_b == a0, hi, jnp.where(lane_b == a0 + 1, mid, jnp.where(
                lane_b == a0 + 2, lo, jnp.where(ones3_q, one, zero))))
            aug_k = jnp.where(ones3_k, one, jnp.where(lane_b == a0 + 3, -hi, jnp.where(
                lane_b == a0 + 4, -mid, jnp.where(lane_b == a0 + 5, -lo, zero))))
            q_ref[:, LANES * h:LANES * (h + 1)] = jnp.where(data_b, qblk, aug_q)
            k_ref[:, LANES * h:LANES * (h + 1)] = jnp.where(data_b, kblk, aug_k)
            v_ref[:, LANES * h:LANES * (h + 1)] = jnp.where(data_b, vblk, jnp.where(lane_b == a0, one, zero))
            qn = jnp.sqrt(jnp.max(jnp.sum(jnp.where(data, q2, 0.0), axis=-1, keepdims=True)))
            kn = jnp.sqrt(jnp.max(jnp.sum(jnp.where(data, k2, 0.0), axis=-1, keepdims=True)))
            dmin = jnp.min(jnp.sum(jnp.where(data, qk, 0.0), axis=-1, keepdims=True))
            c_first, c_last = cumv[0:1, h:h + 1], cumv[t - 1:t, h:h + 1]
            row = jnp.where(lane8 == 0, qn, jnp.where(lane8 == 1, kn, jnp.where(
                lane8 == 2, c_first, jnp.where(lane8 == 3, c_last, jnp.where(lane8 == 4, dmin, 0.0)))))
            stats = jnp.where(sub8 == h, row, stats)
    st_ref[0] = stats


PRUNE_MARGIN = 88.0


def _fox_tile_ranges(stats):
    nt = stats.shape[0]
    qn, kn, c_first, c_last, d_min = (stats[:, :, n] for n in range(5))
    bound = (1.01 * qn[:, None, :] * kn[None, :, :] - jnp.minimum(d_min, 0.0)[:, None, :] + 0.05
             + c_first[:, None, :] - c_last[None, :, :])
    idx = jnp.arange(nt)
    skip = (bound <= -PRUNE_MARGIN) & (idx[None, :, None] < idx[:, None, None])
    first_key = jnp.sum(jnp.cumprod(skip, axis=1), axis=1)
    needed = (idx[None, :, None] >= first_key[:, None, :]) & (idx[None, :, None] <= idx[:, None, None])
    last_query = jnp.max(jnp.where(needed, idx[:, None, None], 0), axis=0)
    n_query = last_query - idx[:, None] + 1
    table = jnp.zeros((4, SUBLANES, LANES), F32)
    for odd in range(2):
        table = table.at[:, odd, :nt].set(first_key[:, odd::2].T.astype(F32))
        table = table.at[:, 2 + odd, :nt].set(n_query[:, odd::2].T.astype(F32))
    return table


def _lane_scalar(block, row, lane_idx):
    sub8 = lax.broadcasted_iota(jnp.int32, (SUBLANES, LANES), 0)
    lane8 = lax.broadcasted_iota(jnp.int32, (SUBLANES, LANES), 1)
    return jnp.sum(jnp.where((sub8 == row) & (lane8 == lane_idx), block, 0.0)).astype(jnp.int32)


def _fox_fwd(qa, ka, va, ranges, t):
    s = qa.shape[0]
    nt = s // t
    nc = t // LANES

    def body(rg_ref, q_ref, k_ref, v_ref, o_ref, lse_ref):
        i = pl.program_id(1)
        lane = lax.broadcasted_iota(jnp.int32, (t, LANES), 1)
        rows = lax.broadcasted_iota(jnp.int32, (t, t), 0)
        cols = lax.broadcasted_iota(jnp.int32, (t, t), 1)
        firsts = [jnp.clip(_lane_scalar(rg_ref[0], hh, i), 0, i) for hh in range(2)]
        first = jnp.maximum(firsts[0], firsts[1])

        def update(js, carry, heads=(0, 1), diagonal=()):
            offs = [pl.multiple_of(j * t, t) for j in js]
            kts = [k_ref[pl.ds(off, t), :] for off in offs]
            vts = [v_ref[pl.ds(off, t), :] for off in offs]
            scs = {hh: [_dot(q_ref[:, LANES * hh:LANES * (hh + 1)], kt[:, LANES * hh:LANES * (hh + 1)], NT) for kt in kts]
                   for hh in heads}
            for n, is_diagonal in enumerate(diagonal):
                if is_diagonal:
                    for hh in heads:
                        scs[hh][n] = jnp.where(cols <= rows, scs[hh][n], NEG)
            m_new = {}
            for hh in heads:
                part = None
                for sc in scs[hh]:
                    for cch in range(nc):
                        chunk = sc[:, LANES * cch:LANES * (cch + 1)]
                        part = chunk if part is None else jnp.maximum(part, chunk)
                m_new[hh] = jnp.maximum(carry[2 * hh], jnp.max(part, axis=-1, keepdims=True))
            alphas = {hh: jnp.exp(carry[2 * hh] - m_new[hh]) for hh in heads}
            ps = {hh: [jnp.exp(sc - m_new[hh]).astype(BF16) for sc in scs[hh]] for hh in heads}
            out = list(carry)
            for hh in heads:
                pv = None
                for p, vt in zip(ps[hh], vts):
                    term = _dot(p, vt[:, LANES * hh:LANES * (hh + 1)])
                    pv = term if pv is None else pv + term
                out[2 * hh], out[2 * hh + 1] = m_new[hh], alphas[hh] * carry[2 * hh + 1] + pv
            return tuple(out)

        col0 = jnp.full((t, 1), NEG, F32)
        zero = jnp.zeros((t, LANES), F32)
        carry = (col0, zero, col0, zero)
        for hh in range(2):
            carry = lax.fori_loop(firsts[hh], first, lambda j, cr, hh=hh: update([j], cr, heads=(hh,)), carry)
        n_off = i - first
        with_diagonal = jnp.minimum(n_off, 1)
        rest = n_off - with_diagonal
        carry = lax.fori_loop(0, rest // 2, lambda u, cr: update([first + 2 * u, first + 2 * u + 1], cr), carry)
        carry = lax.fori_loop(0, rest % 2, lambda u, cr: update([first + rest - 1], cr), carry)
        carry = lax.fori_loop(0, with_diagonal, lambda u, cr: update([i - 1, i], cr, diagonal=(False, True)), carry)
        m0, acc0, m1, acc1 = lax.fori_loop(0, 1 - with_diagonal, lambda u, cr: update([i], cr, diagonal=(True,)), carry)
        l0, l1 = acc0[:, _aug_lane(0):_aug_lane(0) + 1], acc1[:, _aug_lane(1):_aug_lane(1) + 1]
        o_ref[...] = jnp.where(lane < 64, acc0 * (1.0 / l0), acc1 * (1.0 / l1)).astype(BF16)
        sub = lax.broadcasted_iota(jnp.int32, (SUBLANES, t), 0)
        lse0 = jnp.broadcast_to(m0 + jnp.log(l0), (t, LANES)).T[0:SUBLANES, :]
        lse1 = jnp.broadcast_to(m1 + jnp.log(l1), (t, LANES)).T[0:SUBLANES, :]
        lse_ref[0] = jnp.where(sub == 0, lse0, jnp.where(sub == 1, lse1, 0.0))

    pair = pl.BlockSpec((s, 2 * LANES), lambda p, i: (0, p))
    return pl.pallas_call(
        body, name="fox_fwd", grid=(4, nt),
        out_shape=[jax.ShapeDtypeStruct((s, 512), BF16), jax.ShapeDtypeStruct((4, SUBLANES, s), F32)],
        in_specs=[pl.BlockSpec((1, SUBLANES, LANES), lambda p, i: (p, 0, 0)),
                  pl.BlockSpec((t, 2 * LANES), lambda p, i: (i, p)), pair, pair],
        out_specs=[pl.BlockSpec((t, LANES), lambda p, i: (i, p)),
                   pl.BlockSpec((1, SUBLANES, t), lambda p, i: (p, 0, i))],
        compiler_params=_params(("parallel", "arbitrary"), VMEM_LIMIT),
    )(ranges, qa, ka, va)


def _dup_halves(blk, lane):
    f = blk.astype(F32)
    r = pltpu.roll(f, 64, 1)
    return jnp.where(lane < 64, f, r).astype(BF16), jnp.where(lane >= 64, f, r).astype(BF16)


GROUP = 4
GROUP_ROWS = GROUP * WINDOW


def _stack_heads(ref, g, lane):
    parts = []
    for pb in (2 * g, 2 * g + 1):
        blk = ref[:, LANES * pb:LANES * (pb + 1)]
        zero = jnp.zeros_like(blk)
        parts += [jnp.where(lane < 64, blk, zero), jnp.where(lane >= 64, blk, zero)]
    return jnp.concatenate(parts, axis=0)


def _swa_band(a_ref, ap_ref, g, lane):
    k = jnp.concatenate([_dup_halves(ap_ref[:, 512:640], lane)[g], _dup_halves(a_ref[:, 512:640], lane)[g]], axis=0)
    v = jnp.concatenate([_dup_halves(ap_ref[:, 640:768], lane)[g], _dup_halves(a_ref[:, 640:768], lane)[g]], axis=0)
    return k, v


def _swa_logits(q, k, has_prev):
    sc = _dot(q, k, NT) * SCALE
    rr = lax.broadcasted_iota(jnp.int32, sc.shape, 0) % WINDOW
    cc = lax.broadcasted_iota(jnp.int32, sc.shape, 1)
    valid = (cc > rr) & (cc <= rr + WINDOW) & (has_prev | (cc >= WINDOW))
    return jnp.where(valid, sc, NEG)


def _per_head_column(values):
    return jnp.concatenate([jnp.broadcast_to(v, (WINDOW, 1)) for v in values], axis=0)


SWA_BLOCKS = 4
SWA_ROWS = SWA_BLOCKS * WINDOW


def _swa_blocks(a_ref, ap_ref):
    return [ap_ref] + [a_ref.at[pl.ds(WINDOW * jb, WINDOW), :] for jb in range(SWA_BLOCKS)]


def _swa_fwd(a, sinks):
    s = a.shape[0]

    def body(sink_ref, a_ref, ap_ref, o_ref, l_ref):
        lane = lax.broadcasted_iota(jnp.int32, (WINDOW, LANES), 1)
        blocks = _swa_blocks(a_ref, ap_ref)
        units = [(jb, g) for jb in range(SWA_BLOCKS) for g in range(2)]
        sinks_col = [_per_head_column([sink_ref[GROUP * g + hh] for hh in range(GROUP)]) for g in range(2)]
        bands = [_swa_band(blocks[jb + 1], blocks[jb], g, lane) for jb, g in units]
        scs = [_swa_logits(_stack_heads(blocks[jb + 1], g, lane), bands[u][0],
                           (pl.program_id(0) > 0) if jb == 0 else True) for u, (jb, g) in enumerate(units)]
        ms = [jnp.maximum(jnp.max(scs[u], axis=-1, keepdims=True), sinks_col[g]) for u, (jb, g) in enumerate(units)]
        ps = [jnp.exp(scs[u] - ms[u]) for u in range(len(units))]
        dens = [jnp.sum(ps[u], axis=-1, keepdims=True) + jnp.exp(sinks_col[g] - ms[u]) for u, (jb, g) in enumerate(units)]
        outs = [_dot((ps[u] * (1.0 / dens[u])).astype(BF16), bands[u][1]) for u in range(len(units))]
        for jb in range(SWA_BLOCKS):
            rows = slice(WINDOW * jb, WINDOW * (jb + 1))
            l_all = jnp.zeros((WINDOW, LANES), F32)
            for g in range(2):
                u = 2 * jb + g
                lcol = ms[u] + jnp.log(dens[u])
                for pb in range(2):
                    r0 = 2 * pb * WINDOW
                    o_ref[rows, LANES * (2 * g + pb):LANES * (2 * g + pb + 1)] = jnp.where(
                        lane < 64, outs[u][r0:r0 + WINDOW], outs[u][r0 + WINDOW:r0 + 2 * WINDOW]).astype(BF16)
                for hh in range(GROUP):
                    l_all = jnp.where(lane == GROUP * g + hh, lcol[WINDOW * hh:WINDOW * (hh + 1)], l_all)
            l_ref[rows, :] = l_all

    return pl.pallas_call(
        body, name="swa_fwd", grid=(s // SWA_ROWS,),
        out_shape=[jax.ShapeDtypeStruct((s, 512), BF16), jax.ShapeDtypeStruct((s, LANES), F32)],
        in_specs=[pl.BlockSpec(memory_space=pltpu.SMEM),
                  pl.BlockSpec((SWA_ROWS, W_A), lambda i: (i, 0)),
                  pl.BlockSpec((WINDOW, W_A), lambda i: (jnp.maximum(SWA_BLOCKS * i - 1, 0), 0))],
        out_specs=[pl.BlockSpec((SWA_ROWS, 512), lambda i: (i, 0)), pl.BlockSpec((SWA_ROWS, LANES), lambda i: (i, 0))],
        compiler_params=_params(("parallel",)),
    )(sinks, a, a)


def _mid(att_a, att_b, g, x, target, gate, g_final, wo_a, wo_b, w_out, tm=256):
    s = x.shape[0]
    nt = s // tm

    def body(aa_ref, ab_ref, g_ref, x_ref, t_ref, gate_ref, gf_ref, woa_ref, wob_ref, wout_ref,
             dx_ref, daa_ref, dab_ref, dg_ref, delta_ref, dwoa_ref, dwob_ref, dwout_ref, vec_ref,
             acc_gf, acc_gate, acc_loss):
        step = pl.program_id(0)

        @pl.when(step == 0)
        def _():
            dwoa_ref[...] = jnp.zeros_like(dwoa_ref)
            dwob_ref[...] = jnp.zeros_like(dwob_ref)
            dwout_ref[...] = jnp.zeros_like(dwout_ref)
            acc_gf[...] = jnp.zeros_like(acc_gf)
            acc_gate[...] = jnp.zeros_like(acc_gate)
            acc_loss[...] = jnp.zeros_like(acc_loss)

        def fold(v):
            return jnp.sum(v.reshape(tm // SUBLANES, SUBLANES, D_MODEL), axis=0)

        gate = gate_ref[...]
        gfin = gf_ref[...]
        branches = []
        for att_ref, z_off, wo_ref in ((aa_ref, 0, woa_ref), (ab_ref, 512, wob_ref)):
            att = att_ref[...].astype(F32)
            z = g_ref[:, z_off:z_off + 512].astype(F32)
            sz = _sigmoid(z)
            silu = z * sz
            u = (att * silu).astype(BF16)
            branches.append((att, z, sz, silu, u, _dot(u, wo_ref[...])))
        ga = g_ref[:, 1024:2048].astype(F32)
        gb = g_ref[:, 2048:3072].astype(F32)
        sga, sgb = _sigmoid(ga), _sigmoid(gb)
        y_a, y_b = branches[0][5], branches[1][5]
        mb = (sga * y_a + sgb * y_b).astype(BF16)
        o = _dot(mb, wout_ref[...])
        x2 = x_ref[...] + gate * o
        r2 = lax.rsqrt(jnp.mean(x2 * x2, axis=-1, keepdims=True) + NORM_EPS)
        xn2 = x2 * r2
        err = xn2 * gfin - t_ref[...]
        acc_loss[...] += fold(err * err)
        dy = err * (1.0 / D_MODEL)
        acc_gf[...] += fold(dy * xn2)
        dxn = dy * gfin
        dx2 = r2 * (dxn - xn2 * jnp.mean(dxn * xn2, axis=-1, keepdims=True))
        dx_ref[...] = dx2
        acc_gate[...] += fold(dx2 * o)
        d_o = (dx2 * gate).astype(BF16)
        dwout_ref[...] += _dot(mb, d_o, TN)
        dm = _dot(d_o, wout_ref[...], NT)
        dg_ref[:, 1024:2048] = (dm * y_a * sga * (1.0 - sga)).astype(BF16)
        dg_ref[:, 2048:3072] = (dm * y_b * sgb * (1.0 - sgb)).astype(BF16)
        for (att, z, sz, silu, u, _), sg, wo_ref, dwo_ref, datt_ref, z_off in (
                (branches[0], sga, woa_ref, dwoa_ref, daa_ref, 0), (branches[1], sgb, wob_ref, dwob_ref, dab_ref, 512)):
            dyb = (dm * sg).astype(BF16)
            dwo_ref[...] += _dot(u, dyb, TN)
            du = _dot(dyb, wo_ref[...], NT)
            datt = du * silu
            datt_ref[...] = datt.astype(BF16)
            dg_ref[:, z_off:z_off + 512] = (du * att * (sz * (1.0 + z * (1.0 - sz)))).astype(BF16)
            if z_off == 512:
                prod = datt * att
                hi = prod.astype(BF16)
                lo = (prod - hi.astype(F32)).astype(BF16)
                er = lax.broadcasted_iota(jnp.int32, (512, LANES), 0)
                ec = lax.broadcasted_iota(jnp.int32, (512, LANES), 1)
                e = (er // HEAD_DIM == ec).astype(BF16)
                delta = _dot(hi, e) + _dot(lo, e)
                delta_ref[...] = delta.T[0:SUBLANES, :]

        @pl.when(step == nt - 1)
        def _():
            sub = lax.broadcasted_iota(jnp.int32, (SUBLANES, D_MODEL), 0)
            dgf = jnp.sum(acc_gf[...], axis=0, keepdims=True)
            dgate = jnp.sum(acc_gate[...], axis=0, keepdims=True)
            loss = 0.5 * jnp.sum(acc_loss[...]) * (1.0 / D_MODEL)
            vec_ref[...] = jnp.where(sub == 0, dgf, jnp.where(sub == 1, dgate, jnp.where(sub == 2, loss, 0.0)))

    row = lambda w: pl.BlockSpec((tm, w), lambda i: (i, 0))
    return pl.pallas_call(
        body, name="mid", grid=(nt,),
        out_shape=[jax.ShapeDtypeStruct((s, D_MODEL), F32), jax.ShapeDtypeStruct((s, 512), BF16),
                   jax.ShapeDtypeStruct((s, 512), BF16), jax.ShapeDtypeStruct((s, W_G), BF16),
                   jax.ShapeDtypeStruct((SUBLANES, s), F32),
                   jax.ShapeDtypeStruct((512, D_MODEL), F32), jax.ShapeDtypeStruct((512, D_MODEL), F32),
                   jax.ShapeDtypeStruct((D_MODEL, D_MODEL), F32), jax.ShapeDtypeStruct((SUBLANES, D_MODEL), F32)],
        in_specs=[row(512), row(512), row(W_G), row(D_MODEL), row(D_MODEL),
                  _const_spec((1, D_MODEL)), _const_spec((1, D_MODEL)),
                  _const_spec((512, D_MODEL)), _const_spec((512, D_MODEL)), _const_spec((D_MODEL, D_MODEL))],
        out_specs=[row(D_MODEL), row(512), row(512), row(W_G),
                   pl.BlockSpec((SUBLANES, tm), lambda i: (0, i)),
                   pl.BlockSpec((512, D_MODEL), lambda i: (0, 0)), pl.BlockSpec((512, D_MODEL), lambda i: (0, 0)),
                   pl.BlockSpec((D_MODEL, D_MODEL), lambda i: (0, 0)), pl.BlockSpec((SUBLANES, D_MODEL), lambda i: (0, 0))],
        scratch_shapes=[pltpu.VMEM((SUBLANES, D_MODEL), F32)] * 3,
        compiler_params=_params(("arbitrary",), VMEM_LIMIT),
    )(att_a, att_b, g, x, target, gate, g_final, wo_a, wo_b, w_out)


def _rope_bwd(dt, cos, sin, lane):
    u = dt * sin
    lo = (lane % HEAD_DIM) < (HEAD_DIM // 2)
    return dt * cos + jnp.where(lo, pltpu.roll(u, 96, 1), -pltpu.roll(u, 32, 1))


def _swa_bwd(a, datt, l_all, sinks, cos, sin):
    s = a.shape[0]
    nt = s // SWA_ROWS

    def body(sink_ref, a_ref, ap_ref, do_ref, l_ref, cos_ref, sin_ref, da_ref, ds_ref, halo):
        step = pl.program_id(0)
        tile = nt - 1 - step

        @pl.when(step == 0)
        def _():
            halo[...] = jnp.zeros_like(halo)
            ds_ref[...] = jnp.zeros_like(ds_ref)

        lane = lax.broadcasted_iota(jnp.int32, (WINDOW, LANES), 1)
        sub8 = lax.broadcasted_iota(jnp.int32, (SUBLANES, LANES), 0)
        lane8 = lax.broadcasted_iota(jnp.int32, (SUBLANES, LANES), 1)
        blocks = _swa_blocks(a_ref, ap_ref)
        dsink = jnp.zeros((SUBLANES, LANES), F32)

        def join(pair, r0):
            x0, x1 = pair[0][r0:r0 + WINDOW], pair[1][r0:r0 + WINDOW]
            return jnp.where(lane < 64, x0 + pltpu.roll(x0, 64, 1), x1 + pltpu.roll(x1, 64, 1))

        units = [(jb, g) for jb in range(SWA_BLOCKS) for g in range(2)]
        n_u = len(units)
        sinks_col = [_per_head_column([sink_ref[GROUP * g + hh] for hh in range(GROUP)]) for g in range(2)]
        bands = [_swa_band(blocks[jb + 1], blocks[jb], g, lane) for jb, g in units]
        qs = [_stack_heads(blocks[jb + 1], g, lane) for jb, g in units]
        doms = [_stack_heads(do_ref.at[pl.ds(WINDOW * jb, WINDOW), :], g, lane) for jb, g in units]
        lcols = []
        for jb, g in units:
            lv = l_ref[WINDOW * jb:WINDOW * (jb + 1), :]
            lcols.append(_per_head_column([lv[:, GROUP * g + hh:GROUP * g + hh + 1] for hh in range(GROUP)]))
        ps = [jnp.exp(_swa_logits(qs[u], bands[u][0], (tile > 0) if jb == 0 else True) - lcols[u])
              for u, (jb, g) in enumerate(units)]
        dps = [_dot(doms[u], bands[u][1], NT) for u in range(n_u)]
        deltas = [jnp.sum(ps[u] * dps[u], axis=-1, keepdims=True) for u in range(n_u)]
        for u, (jb, g) in enumerate(units):
            sink_term = jnp.exp(sinks_col[g] - lcols[u]) * deltas[u]
            for hh in range(GROUP):
                tot = jnp.sum(sink_term[WINDOW * hh:WINDOW * (hh + 1)])
                dsink = dsink + jnp.where((sub8 == 0) & (lane8 == GROUP * g + hh), -tot, 0.0)
        dss = [(ps[u] * (dps[u] - deltas[u])).astype(BF16) for u in range(n_u)]
        dqs = [_dot(dss[u], bands[u][0]) * SCALE for u in range(n_u)]
        dks = [_dot(dss[u], qs[u], TN) * SCALE for u in range(n_u)]
        dvs = [_dot(ps[u].astype(BF16), doms[u], TN) for u in range(n_u)]

        carry_k, carry_v = halo[:, 0:LANES], halo[:, LANES:2 * LANES]
        for jb in reversed(range(SWA_BLOCKS)):
            rows = slice(WINDOW * jb, WINDOW * (jb + 1))
            cosv, sinv = cos_ref[rows, :], sin_ref[rows, :]
            for g in range(2):
                dq = dqs[2 * jb + g]
                for pb in range(2):
                    r0 = 2 * pb * WINDOW
                    dq_pair = jnp.where(lane < 64, dq[r0:r0 + WINDOW], dq[r0 + WINDOW:r0 + 2 * WINDOW])
                    da_ref[rows, LANES * (2 * g + pb):LANES * (2 * g + pb + 1)] = _rope_bwd(
                        dq_pair, cosv, sinv, lane).astype(BF16)
            dkb, dvb = dks[2 * jb:2 * jb + 2], dvs[2 * jb:2 * jb + 2]
            da_ref[rows, 512:640] = _rope_bwd(join(dkb, WINDOW) + carry_k, cosv, sinv, lane).astype(BF16)
            da_ref[rows, 640:768] = (join(dvb, WINDOW) + carry_v).astype(BF16)
            carry_k, carry_v = join(dkb, 0), join(dvb, 0)
        halo[:, 0:LANES] = carry_k
        halo[:, LANES:2 * LANES] = carry_v
        ds_ref[...] += dsink

    rev = lambda w: pl.BlockSpec((SWA_ROWS, w), lambda i: (nt - 1 - i, 0))
    return pl.pallas_call(
        body, name="swa_bwd", grid=(nt,),
        out_shape=[jax.ShapeDtypeStruct((s, W_A), BF16), jax.ShapeDtypeStruct((SUBLANES, LANES), F32)],
        in_specs=[pl.BlockSpec(memory_space=pltpu.SMEM), rev(W_A),
                  pl.BlockSpec((WINDOW, W_A), lambda i: (jnp.maximum(SWA_BLOCKS * (nt - 1 - i) - 1, 0), 0)),
                  rev(512), rev(LANES), rev(LANES), rev(LANES)],
        out_specs=[rev(W_A), pl.BlockSpec((SUBLANES, LANES), lambda i: (0, 0))],
        scratch_shapes=[pltpu.VMEM((WINDOW, 2 * LANES), F32)],
        compiler_params=_params(("arbitrary",)),
    )(sinks, a, a, datt, l_all, cos, sin)


def _fox_bwd(qa, ka, vb, do, lse, delta, ranges, t):
    s = qa.shape[0]
    nt = s // t

    def body(rg_ref, q_ref, do_ref, lse_ref, dl_ref, k_ref, v_ref, dq_ref, dk_ref, dv_ref, dc_ref, dr_ref, dq_acc):
        p = pl.program_id(0)
        j = pl.program_id(1)
        n_queries = [jnp.clip(_lane_scalar(rg_ref[0], 2 + hh, j), 1, nt - j) for hh in range(2)]

        @pl.when(j == 0)
        def _():
            dq_acc[...] = jnp.zeros_like(dq_acc)

        lane = lax.broadcasted_iota(jnp.int32, (t, LANES), 1)
        rows = lax.broadcasted_iota(jnp.int32, (t, t), 0)
        cols = lax.broadcasted_iota(jnp.int32, (t, t), 1)
        kt = k_ref[...]
        vt = v_ref[...]

        ks = [kt[:, LANES * hh:LANES * (hh + 1)] for hh in range(2)]

        def tile(qis, carry, heads=(0, 1), diagonal=()):
            dk0, dk1, dv = carry
            offs = [pl.multiple_of(i * t, t) for i in qis]
            units = [(u, hh) for u in range(len(qis)) for hh in heads]
            qts = [q_ref[pl.ds(off, t), :] for off in offs]
            dos = [do_ref[pl.ds(off, t), :] for off in offs]
            lses = [lse_ref[0, :, pl.ds(off, t)] for off in offs]
            dls = [dl_ref[0, :, pl.ds(off, t)] for off in offs]
            qs = [qts[u][:, LANES * hh:LANES * (hh + 1)] for u, hh in units]
            doms = [jnp.where((lane < 64) if hh == 0 else (lane >= 64), dos[u], jnp.zeros_like(dos[u])) for u, hh in units]
            sts = [_dot(ks[hh], qs[n], NT) for n, (u, hh) in enumerate(units)]
            dpts = [_dot(vt, doms[n], NT) for n in range(len(units))]
            for n, (u, hh) in enumerate(units):
                if u < len(diagonal) and diagonal[u]:
                    sts[n] = jnp.where(cols >= rows, sts[n], NEG)
            pts = [jnp.exp(sts[n] - lses[u][hh:hh + 1, :]) for n, (u, hh) in enumerate(units)]
            dsts = [(pts[n] * (dpts[n] - dls[u][hh:hh + 1, :])).astype(BF16) for n, (u, hh) in enumerate(units)]
            for n, (u, hh) in enumerate(units):
                dv = dv + _dot(pts[n].astype(BF16), doms[n])
                term = _dot(dsts[n], qs[n])
                dk0, dk1 = (dk0 + term, dk1) if hh == 0 else (dk0, dk1 + term)
                dq_acc[hh, pl.ds(offs[u], t), :] += _dot(dsts[n], ks[hh], TN)
            return dk0, dk1, dv

        zero = jnp.zeros((t, LANES), F32)
        n_rest = jnp.minimum(n_queries[0], n_queries[1]) - 1
        with_diagonal = jnp.minimum(n_rest, 1)
        later = n_rest - with_diagonal
        carry = (zero, zero, zero)
        carry = lax.fori_loop(0, with_diagonal, lambda u, cr: tile([j, j + 1], cr, diagonal=(True, False)), carry)
        carry = lax.fori_loop(0, 1 - with_diagonal, lambda u, cr: tile([j], cr, diagonal=(True,)), carry)
        carry = lax.fori_loop(0, later // 2, lambda u, cr: tile([j + 2 + 2 * u, j + 3 + 2 * u], cr), carry)
        carry = lax.fori_loop(0, later % 2, lambda u, cr: tile([j + n_rest], cr), carry)
        for hh in range(2):
            carry = lax.fori_loop(j + 1 + n_rest, j + n_queries[hh], lambda i, cr, hh=hh: tile([i], cr, heads=(hh,)), carry)
        dk0, dk1, dv = carry
        e0, e1 = _aug_lane(0), _aug_lane(1)
        dk_ref[...] = jnp.where(lane < 64, dk0, dk1).astype(BF16)
        dv_ref[...] = dv.astype(BF16)
        c0 = jnp.broadcast_to(dk0[:, e0 + 3:e0 + 4], (t, LANES))
        c1 = jnp.broadcast_to(dk1[:, e1 + 3:e1 + 4], (t, LANES))
        dc_ref[0] = jnp.where(lane == 2 * p, -c0, jnp.where(lane == 2 * p + 1, -c1, 0.0))

        @pl.when(j == nt - 1)
        def _():
            lane_s = lax.broadcasted_iota(jnp.int32, (s, LANES), 1)
            a0, a1 = dq_acc[0], dq_acc[1]
            dq_ref[...] = (jnp.where(lane_s < 64, a0, a1) * SCALE).astype(BF16)
            r0 = jnp.broadcast_to(a0[:, e0:e0 + 1], (s, LANES))
            r1 = jnp.broadcast_to(a1[:, e1:e1 + 1], (s, LANES))
            dr_ref[0] = jnp.where(lane_s == 2 * p, r0, jnp.where(lane_s == 2 * p + 1, r1, 0.0))

    return pl.pallas_call(
        body, name="fox_bwd", grid=(4, nt),
        out_shape=[jax.ShapeDtypeStruct((s, 512), BF16), jax.ShapeDtypeStruct((s, 512), BF16),
                   jax.ShapeDtypeStruct((s, 512), BF16), jax.ShapeDtypeStruct((4, s, LANES), F32),
                   jax.ShapeDtypeStruct((4, s, LANES), F32)],
        in_specs=[pl.BlockSpec((1, SUBLANES, LANES), lambda p, j: (p, 0, 0)),
                  pl.BlockSpec((s, 2 * LANES), lambda p, j: (0, p)),
                  pl.BlockSpec((s, LANES), lambda p, j: (0, p)),
                  pl.BlockSpec((1, SUBLANES, s), lambda p, j: (p, 0, 0)),
                  pl.BlockSpec((1, SUBLANES, s), lambda p, j: (p, 0, 0)),
                  pl.BlockSpec((t, 2 * LANES), lambda p, j: (j, p)),
                  pl.BlockSpec((t, LANES), lambda p, j: (j, p))],
        out_specs=[pl.BlockSpec((s, LANES), lambda p, j: (0, p)),
                   pl.BlockSpec((t, LANES), lambda p, j: (j, p)),
                   pl.BlockSpec((t, LANES), lambda p, j: (j, p)),
                   pl.BlockSpec((1, t, LANES), lambda p, j: (p, j, 0)),
                   pl.BlockSpec((1, s, LANES), lambda p, j: (p, 0, 0))],
        scratch_shapes=[pltpu.VMEM((2, s, LANES), F32)],
        compiler_params=_params(("parallel", "arbitrary"), VMEM_LIMIT),
    )(ranges, qa, do, lse, delta, ka, vb)


def _forget_logit_grad(dc_ref, dr_ref, f, b_f, carry):
    tb = f.shape[0]
    lane = lax.broadcasted_iota(jnp.int32, (tb, LANES), 1)
    dc = dc_ref[0] + dr_ref[0]
    for k in range(1, 4):
        dc = dc + (dc_ref[k] + dr_ref[k])
    hi, mid, lo = _split3(dc)
    rows = lax.broadcasted_iota(jnp.int32, (tb, tb), 0)
    cols = lax.broadcasted_iota(jnp.int32, (tb, tb), 1)
    triu = (cols >= rows).astype(BF16)
    dlogf = _dot(triu, hi) + _dot(triu, mid) + _dot(triu, lo) + carry[0:1, :]
    carry[...] = jnp.broadcast_to(dlogf[0:1, :], carry.shape)
    return jnp.where(lane < N_HEADS, dlogf * _sigmoid(-(f + b_f)), 0.0)


def _dh_norm_bwd(d_a, d_q, d_k, d_v, dcum_k, dcum_q, f, bf_pad, d_g, w_t, x, dx2, gnorm, scale1, tm=512):
    s = x.shape[0]
    nt = s // tm

    def body(da_ref, dq_ref, dk_ref, dv_ref, dc_ref, dr_ref, f_ref, bf_ref, dg_ref, w_ref, x_ref, dx2_ref, g_ref, sc_ref,
             gx_ref, vec_ref, df_ref, db_ref, a_sh, a_sc, a_g, carry):
        step = pl.program_id(0)

        @pl.when(step == 0)
        def _():
            a_sh[...] = jnp.zeros_like(a_sh)
            a_sc[...] = jnp.zeros_like(a_sc)
            a_g[...] = jnp.zeros_like(a_g)
            carry[...] = jnp.zeros_like(carry)
            db_ref[...] = jnp.zeros_like(db_ref)

        def fold(v):
            return jnp.sum(v.reshape(tm // SUBLANES, SUBLANES, D_MODEL), axis=0)

        dfb = _forget_logit_grad(dc_ref, dr_ref, f_ref[...], bf_ref[...], carry)
        d_f = dfb.astype(BF16)
        df_ref[...] = d_f
        sub8 = lax.broadcasted_iota(jnp.int32, (SUBLANES, LANES), 0)
        db_ref[...] += jnp.where(sub8 == 0, jnp.sum(dfb, axis=0, keepdims=True), 0.0)
        d_all = jnp.concatenate([da_ref[...], dq_ref[...], dk_ref[...], dv_ref[...], d_f, dg_ref[...]], axis=1)
        dh = _dot(d_all, w_ref[...])
        xv = x_ref[...]
        r = lax.rsqrt(jnp.mean(xv * xv, axis=-1, keepdims=True) + NORM_EPS)
        xn = xv * r
        gn = g_ref[...]
        a_sh[...] += fold(dh)
        a_sc[...] += fold(dh * (xn * gn))
        dn1 = dh * sc_ref[...]
        a_g[...] += fold(dn1 * xn)
        dxn = dn1 * gn
        gx_ref[...] = dx2_ref[...] + r * (dxn - xn * jnp.mean(dxn * xn, axis=-1, keepdims=True))

        @pl.when(step == nt - 1)
        def _():
            sub = lax.broadcasted_iota(jnp.int32, (SUBLANES, D_MODEL), 0)
            v_sh = jnp.sum(a_sh[...], axis=0, keepdims=True)
            v_sc = jnp.sum(a_sc[...], axis=0, keepdims=True)
            v_g = jnp.sum(a_g[...], axis=0, keepdims=True)
            vec_ref[...] = jnp.where(sub == 0, v_sh, jnp.where(sub == 1, v_sc, jnp.where(sub == 2, v_g, 0.0)))

    row = lambda w: pl.BlockSpec((tm, w), lambda i: (nt - 1 - i, 0))
    slabs = pl.BlockSpec((4, tm, LANES), lambda i: (0, nt - 1 - i, 0))
    return pl.pallas_call(
        body, name="dh_norm_bwd", grid=(nt,),
        out_shape=[jax.ShapeDtypeStruct((s, D_MODEL), F32), jax.ShapeDtypeStruct((SUBLANES, D_MODEL), F32),
                   jax.ShapeDtypeStruct((s, LANES), BF16), jax.ShapeDtypeStruct((SUBLANES, LANES), F32)],
        in_specs=[row(W_A), row(512), row(512), row(512), slabs, slabs, row(W_F), _const_spec((1, LANES)), row(W_G),
                  _const_spec((W_INT, D_MODEL)), row(D_MODEL), row(D_MODEL), _const_spec((1, D_MODEL)),
                  _const_spec((1, D_MODEL))],
        out_specs=[row(D_MODEL), pl.BlockSpec((SUBLANES, D_MODEL), lambda i: (0, 0)), row(LANES),
                   pl.BlockSpec((SUBLANES, LANES), lambda i: (0, 0))],
        scratch_shapes=[pltpu.VMEM((SUBLANES, D_MODEL), F32)] * 3 + [pltpu.VMEM((SUBLANES, LANES), F32)],
        compiler_params=_params(("arbitrary",), VMEM_LIMIT),
    )(d_a, d_q, d_k, d_v, dcum_k, dcum_q, f, bf_pad, d_g, w_t, x, dx2, gnorm, scale1)


def _dw_in(h_t, d_a, d_q, d_k, d_v, d_f, d_g, ts=1024, tc=512):
    s = h_t.shape[1]
    ns = s // ts
    w_fg = 512 + W_F + W_G - 512
    rows = 128
    n_slot = (R_END // 4 + LANES - 1) // LANES * LANES
    order = [(0, 0, W_A), (4, 0, 512), (1, 0, 512), (2, 0, 512), (3, 0, 512), (4, 512, N_HEADS), (4, 512 + W_F, W_G - 512)]

    def slot_pieces(k):
        lo, hi, out, col = (R_END // 4) * k, (R_END // 4) * (k + 1), [], 0
        for acc_i, c0, w in order:
            a, b = max(lo, col), min(hi, col + w)
            if a < b:
                out.append((acc_i, c0 + a - col, b - a))
            col += w
        return out

    def body(h_ref, da_ref, dq_ref, dk_ref, dv_ref, df_ref, dg_ref, o_ref, acc_a, acc_q, acc_k, acc_v, acc_fg, stage, sem):
        k = pl.program_id(0)
        accs = (acc_a, acc_q, acc_k, acc_v, acc_fg)

        @pl.when(k == 0)
        def _():
            for acc in accs:
                acc[...] = jnp.zeros_like(acc)

        hv = h_ref[...]

        def add(acc, c_acc, d_ref, c_d, width):
            for c0 in range(0, width, tc):
                w = min(tc, width - c0)
                acc[:, c_acc + c0:c_acc + c0 + w] += _dot(hv, d_ref[:, c_d + c0:c_d + c0 + w])

        add(acc_a, 0, da_ref, 0, W_A)
        add(acc_q, 0, dq_ref, 0, 512)
        add(acc_k, 0, dk_ref, 0, 512)
        add(acc_v, 0, dv_ref, 0, 512)
        add(acc_fg, 0, dg_ref, 0, 512)
        add(acc_fg, 512, df_ref, 0, W_F)
        add(acc_fg, 512 + W_F, dg_ref, 512, W_G - 512)

        @pl.when(k == ns - 1)
        def _():
            pending = [None, None]
            for n, r0 in enumerate(range(0, D_MODEL, rows)):
                buf = n % 2
                if pending[buf] is not None:
                    for cp in pending[buf]:
                        cp.wait()
                rs = slice(r0, r0 + rows)
                copies = []
                for slot in range(4):
                    parts = [accs[acc_i][rs, c0:c0 + w] for acc_i, c0, w in slot_pieces(slot)]
                    parts.append(jnp.zeros((rows, n_slot - R_END // 4), F32))
                    stage[buf, slot] = jnp.concatenate(parts, axis=1)
                    cp = pltpu.make_async_copy(stage.at[buf, slot], o_ref.at[slot, pl.ds(r0, rows), :], sem.at[4 * buf + slot])
                    cp.start()
                    copies.append(cp)
                pending[buf] = copies
            for copies in pending:
                for cp in copies:
                    cp.wait()

    spec = lambda d: pl.BlockSpec((ts, d.shape[1]), lambda k: (k, 0))
    return pl.pallas_call(
        body, name="dw_in", grid=(ns,),
        out_shape=jax.ShapeDtypeStruct((4, D_MODEL, n_slot), F32),
        in_specs=[pl.BlockSpec((D_MODEL, ts), lambda k: (0, k))] + [spec(d) for d in (d_a, d_q, d_k, d_v, d_f, d_g)],
        out_specs=pl.BlockSpec(memory_space=pl.ANY),
        scratch_shapes=[pltpu.VMEM((D_MODEL, W_A), F32)] + [pltpu.VMEM((D_MODEL, 512), F32)] * 3
        + [pltpu.VMEM((D_MODEL, w_fg), F32), pltpu.VMEM((2, 4, rows, n_slot), F32), pltpu.SemaphoreType.DMA((8,))],
        compiler_params=_params(("arbitrary",), VMEM_LIMIT),
    )(h_t, d_a, d_q, d_k, d_v, d_f, d_g)


def _small_grads(packs, c_t, dada_shard):
    def body(p_ref, ct_ref, da_ref, sum_ref, gw_ref):
        acc = p_ref[0]
        for dev in range(1, 8):
            acc = acc + p_ref[dev]
        sum_ref[...] = acc
        gw_ref[...] = jnp.dot(ct_ref[...], da_ref[...], preferred_element_type=F32, precision=lax.Precision.HIGHEST)

    return pl.pallas_call(
        body, name="small_grads",
        out_shape=[jax.ShapeDtypeStruct(packs.shape[1:], F32),
                   jax.ShapeDtypeStruct((c_t.shape[0], dada_shard.shape[1]), F32)],
    )(packs, c_t, dada_shard)


def _adamw_body(w_ref, g_ref, m_ref, v_ref, d_ref, mo_ref, vo_ref):
    c1 = 1.0 / (1.0 - ADAM_B1 ** ADAM_STEP)
    c2 = 1.0 / (1.0 - ADAM_B2 ** ADAM_STEP)
    gv = g_ref[...]
    mn = ADAM_B1 * m_ref[...] + (1.0 - ADAM_B1) * gv
    vn = ADAM_B2 * v_ref[...] + (1.0 - ADAM_B2) * (gv * gv)
    mo_ref[...] = mn
    vo_ref[...] = vn
    d_ref[...] = -ADAM_LR * ((mn * c1) / (jnp.sqrt(vn * c2) + ADAM_EPS) + ADAM_WD * w_ref[...])


def _adamw3(w, g, m, v, name, tb=128):
    spec = pl.BlockSpec((tb, SUBLANES, LANES), lambda i: (i, 0, 0))
    return pl.pallas_call(
        functools.partial(_adamw_body), name=name, grid=(pl.cdiv(w.shape[0], tb),),
        out_shape=[jax.ShapeDtypeStruct(w.shape, F32)] * 3,
        in_specs=[spec] * 4, out_specs=[spec] * 3,
        compiler_params=_params(("parallel",)),
    )(w, g, m, v)


def _adamw_many(items, name):
    n = len(items)

    def body(*refs):
        for i in range(n):
            _adamw_body(*refs[4 * i:4 * i + 4], *refs[4 * n + 3 * i:4 * n + 3 * i + 3])

    return pl.pallas_call(
        body, name=name,
        out_shape=[jax.ShapeDtypeStruct(it[0].shape, F32) for it in items for _ in range(3)],
        compiler_params=_params(vmem=VMEM_LIMIT),
    )(*[arr for it in items for arr in it])


def _rope_inputs(positions):
    inv_freq = 10000.0 ** (-jnp.arange(0, HEAD_DIM, 2, dtype=F32) / HEAD_DIM)
    pos = jnp.broadcast_to(positions.astype(F32)[:, None], (positions.shape[0], LANES))
    return pos, jnp.tile(inv_freq, 4)[None, :]


def _pad_rows(v, rows=SUBLANES):
    return jnp.pad(v, ((0, rows - v.shape[0]), (0, 0)))


def kernel(x, c, positions, w_ada, b_ada, g_norm, w_in, b_f, sinks, w_o_swa, w_o_fox, w_out, g_final, loss_target, m_w_ada, m_b_ada, m_g_norm, m_w_in, m_b_f, m_sinks, m_w_o_swa, m_w_o_fox, m_w_out, m_g_final, v_w_ada, v_b_ada, v_g_norm, v_w_in, v_b_f, v_sinks, v_w_o_swa, v_w_o_fox, v_w_out, v_g_final):
    ix, iy, ic = lax.axis_index("x"), lax.axis_index("y"), lax.axis_index("c")
    chip = 2 * ix + iy
    dev = 2 * chip + ic
    xs, tgt = x[0], loss_target[0]
    s = xs.shape[0]

    b_ada_shard = lax.dynamic_slice(b_ada, (0, chip * 768), (1, 768))
    ada_parts, w_int, w_int_t, g_oa, g_ob, g_out = _gather_inputs(
        _pad_rows(c), w_ada[0], b_ada_shard, jnp.transpose(w_in, (2, 0, 1)).reshape(-1, LANES),
        [w_o_swa[0], w_o_fox[0], w_out[0]], "gather_inputs")
    ada = lax.dynamic_index_in_dim(ada_parts, dev, axis=1, keepdims=False).reshape(1, 3 * D_MODEL)
    shift, scale, gate = ada[:, :D_MODEL], ada[:, D_MODEL:2 * D_MODEL], ada[:, 2 * D_MODEL:]
    scale1 = 1.0 + scale

    wo_a = jnp.transpose(g_oa, (1, 0, 2)).reshape(512, D_MODEL)
    wo_b = jnp.transpose(g_ob, (1, 0, 2)).reshape(512, D_MODEL)
    w_o = g_out.reshape(D_MODEL, D_MODEL)

    pos, freq = _rope_inputs(positions[0])
    bf_pad = jnp.pad(b_f, ((0, 0), (0, LANES - N_HEADS)))
    sink_vec = sinks[0]

    a, vb, f, g, h_t, cos, sin, qa, ka, va, stats = _norm_proj(
        xs, g_norm * scale1, shift, w_int, pos, freq, bf_pad, FOX_TILE)
    att_a, l_swa = _swa_fwd(a, sink_vec)
    ranges = _fox_tile_ranges(stats)
    att_b, lse = _fox_fwd(qa, ka, va, ranges, FOX_TILE)

    dx2, datt_a, datt_b, d_g, delta8, dwo_a, dwo_b, dw_out, vec_mid = _mid(
        att_a, att_b, g, xs, tgt, gate, g_final.reshape(1, D_MODEL), wo_a, wo_b, w_o)
    delta = jnp.pad(delta8.reshape(4, 2, s), ((0, 0), (0, SUBLANES - 2), (0, 0)))
    d_a, dsink = _swa_bwd(a, datt_a, l_swa, sink_vec, cos, sin)
    dq, dk, dv, dcum_k, dcum_q = _fox_bwd(qa, ka, vb, datt_b, lse, delta, ranges, FOX_TILE)
    grad_x, vec_dh, d_f, dbf = _dh_norm_bwd(
        d_a, dq, dk, dv, dcum_k, dcum_q, f, bf_pad, d_g, w_int_t, xs, dx2, g_norm, scale1)
    dw_in_slots = _dw_in(h_t, d_a, dq, dk, dv, d_f, d_g)

    tail = jnp.pad(jnp.concatenate([dbf[0:1, :N_HEADS], dsink[0:1, :N_HEADS]], axis=1), ((0, 0), (0, D_MODEL - 2 * N_HEADS)))
    pack = jnp.concatenate([c, vec_dh[0:2], vec_mid[1:2], vec_dh[2:3], vec_mid[0:1], tail, vec_mid[2:3]], axis=0)

    def slots(w, axis):
        if axis == 1:
            return jnp.transpose(w.reshape(w.shape[0], 4, w.shape[1] // 4), (1, 0, 2))
        return w.reshape(4, w.shape[0] // 4, w.shape[1])

    packs, g_wo_a, g_wo_b, g_w_out, g_w_in = _reduce_scatter(
        [slots(dwo_a, 1), slots(dwo_b, 1), slots(dw_out, 0), dw_in_slots], pack, "reduce_grads")
    g_w_in = g_w_in[:, :w_in.shape[2]]
    dada_all = packs[:, 1:4, :].reshape(8, 3 * D_MODEL)
    dada_shard = lax.dynamic_slice(dada_all, (0, chip * 768), (8, 768))
    sums, g_w_ada = _small_grads(packs, packs[:, 0, :].T, dada_shard)
    g_b_ada = sums[1:4].reshape(1, 3 * D_MODEL)
    g_g_norm = sums[4:5]
    g_g_final = sums[5]
    g_b_f = sums[6:7, :N_HEADS]
    g_sinks = sums[6:7, N_HEADS:2 * N_HEADS]
    loss = sums[7, 0]

    grads = {
        "w_ada": g_w_ada, "b_ada": g_b_ada, "g_norm": g_g_norm, "w_in": g_w_in, "b_f": g_b_f, "sinks": g_sinks,
        "w_o_swa": g_wo_a, "w_o_fox": g_wo_b, "w_out": g_w_out, "g_final": g_g_final,
    }
    params = {
        "w_ada": (w_ada, m_w_ada, v_w_ada), "b_ada": (b_ada, m_b_ada, v_b_ada), "g_norm": (g_norm, m_g_norm, v_g_norm),
        "w_in": (w_in, m_w_in, v_w_in), "b_f": (b_f, m_b_f, v_b_f), "sinks": (sinks, m_sinks, v_sinks),
        "w_o_swa": (w_o_swa, m_w_o_swa, v_w_o_swa), "w_o_fox": (w_o_fox, m_w_o_fox, v_w_o_fox),
        "w_out": (w_out, m_w_out, v_w_out), "g_final": (g_final, m_g_final, v_g_final),
    }
    n_col = w_in.shape[2]

    def as_stored(t):
        return jnp.transpose(t, (2, 0, 1)).reshape(n_col, SUBLANES, LANES)

    def from_stored(t):
        return jnp.transpose(t, (1, 2, 0)).reshape(1, D_MODEL, n_col)

    names = list(grads)
    others = [nm for nm in names if nm != "w_in"]

    def as_2d(t):
        return t.reshape((t.shape[-2], t.shape[-1]) if t.ndim >= 2 else (1, t.shape[0]))

    flat = _adamw_many([tuple(as_2d(t) for t in (params[nm][0], grads[nm], params[nm][1], params[nm][2])) for nm in others],
                       "adamw_small")
    results = {}
    for i, nm in enumerate(others):
        shape = params[nm][0].shape
        results[nm] = [t.reshape(shape) for t in (grads[nm], *flat[3 * i:3 * i + 3])]
    w, m, v = params["w_in"]
    g_st = as_stored(grads["w_in"][None])
    d_, m_, v_ = _adamw3(as_stored(w), g_st, as_stored(m), as_stored(v), "adamw_w_in")
    results["w_in"] = [from_stored(t) for t in (g_st, d_, m_, v_)]
    return (loss, grad_x[None], *[results[nm][0] for nm in names], *[results[nm][1] for nm in names],
            *[results[nm][2] for nm in names], *[results[nm][3] for nm in names])
```

```python
import functools

import numpy as np
import jax
import jax.numpy as jnp
from jax import lax
from jax.experimental import pallas as pl
from jax.experimental.pallas import tpu as pltpu

F32 = jnp.float32
BF16 = jnp.bfloat16
MESH = pl.DeviceIdType.MESH

D_MODEL = 1024
HEAD_DIM = 64
N_HEADS = 8
WINDOW = 128
NORM_EPS = 1e-6
SCALE = HEAD_DIM ** -0.5
NEG = -1e30
LANES = 128
SUBLANES = 8
VMEM_LIMIT = 60 * 1024 * 1024
FOX_TILE = 512

W_A, W_B, W_F, W_G = 768, 1536, 128, 3072
OFF_A, OFF_B, OFF_F, OFF_G = 0, 768, 2304, 2432
W_INT = W_A + W_B + W_F + W_G
R_ZA, R_QB, R_FB, R_ZB, R_END = 768, 1280, 2816, 2824, 5384

ADAM_LR, ADAM_B1, ADAM_B2, ADAM_EPS, ADAM_WD, ADAM_STEP = 0.001, 0.9, 0.999, 1e-08, 0.01, 10

NT = (((1,), (1,)), ((), ()))
TN = (((0,), (0,)), ((), ()))


def _dot(a, b, dims=None):
    if dims is None:
        return jnp.dot(a, b, preferred_element_type=F32)
    return lax.dot_general(a, b, dims, preferred_element_type=F32)


def _split3(v):
    hi = v.astype(BF16)
    r1 = v - hi.astype(F32)
    mid = r1.astype(BF16)
    lo = (r1 - mid.astype(F32)).astype(BF16)
    return hi, mid, lo


def _sigmoid(v):
    return 1.0 / (1.0 + jnp.exp(-v))


def _params(sem=None, vmem=None):
    return pltpu.CompilerParams(dimension_semantics=sem, vmem_limit_bytes=vmem)


def _const_spec(shape):
    nd = len(shape)
    return pl.BlockSpec(shape, lambda *_: (0,) * nd, pipeline_mode=pl.Buffered(1))


def _flip(v, f):
    return 1 - v if f else v


_CHIP_FLIPS = ((1, 0), (0, 1), (1, 1))


def _gather_inputs(c_pad, w_ada, b_ada_shard, w_in_shard, small_shards, name):
    shards = [w_in_shard] + list(small_shards)
    n = len(shards)
    n_col = w_ada.shape[1]
    shard_w = w_in_shard.shape[0] // SUBLANES
    rows = 128

    def body(*refs):
        c_ref, wa_ref, ba_ref = refs[:3]
        ins = refs[3:3 + n]
        ada_ref, wint_ref, wintt_ref = refs[3 + n:6 + n]
        g_in, call_ref, send_sems, recv_sems = refs[5 + 2 * n:9 + 2 * n]
        outs = (g_in,) + tuple(refs[6 + n:5 + 2 * n])
        x, y, c = lax.axis_index("x"), lax.axis_index("y"), lax.axis_index("c")
        k_me = 2 * x + y
        me = 2 * k_me + c
        sibling = (x, y, 1 - c)
        chips = [(_flip(x, fx), _flip(y, fy)) for fx, fy in _CHIP_FLIPS]

        def piece(i, chip_k, half):
            hr = outs[i].shape[1] // 2
            return outs[i].at[chip_k, pl.ds(half * hr, hr), :]

        def copy(i, slot, chip_k, half, to):
            return pltpu.make_async_remote_copy(
                src_ref=piece(i, chip_k, half), dst_ref=piece(i, chip_k, half),
                send_sem=send_sems.at[6 * i + slot], recv_sem=recv_sems.at[6 * i + slot],
                device_id=to, device_id_type=MESH)

        def small(ref, slot, sem, to):
            return pltpu.make_async_remote_copy(
                src_ref=ref.at[slot], dst_ref=ref.at[slot], send_sem=send_sems.at[6 * n + sem],
                recv_sem=recv_sems.at[6 * n + sem], device_id=to, device_id_type=MESH)

        whole = shard_w // LANES * LANES
        for a in range(SUBLANES):
            main = ins[0][pl.ds(a, whole, stride=SUBLANES), :]
            tail = ins[0][pl.ds(a + SUBLANES * whole, shard_w - whole, stride=SUBLANES), :]
            tail = jnp.concatenate([tail, jnp.zeros((LANES - (shard_w - whole), LANES), F32)], axis=0)
            blk = jnp.concatenate([main.T, tail.T[:, :shard_w - whole]], axis=1)
            g_in[k_me, LANES * a:LANES * (a + 1), :] = blk.astype(BF16)
        for i in range(1, n):
            outs[i][k_me] = ins[i][...].astype(BF16)
        started = []
        for i in range(n):
            for j, chip in enumerate(chips):
                cp = copy(i, j, k_me, c, (chip[0], chip[1], c))
                cp.start()
                started.append(cp)

        call_ref[me] = c_ref[...]
        peers = [(_flip(x, k & 4), _flip(y, k & 2), _flip(c, k & 1)) for k in range(1, 8)]
        for k, peer in enumerate(peers):
            cp = small(call_ref, me, k, peer)
            cp.start()
            started.append(cp)
        for k, peer in enumerate(peers):
            small(call_ref, 4 * peer[0] + 2 * peer[1] + peer[2], k, peer).wait_recv()
        c_all = call_ref[:, 0, :].astype(BF16)
        ada_ref[k_me] = _dot(c_all, wa_ref[...].astype(BF16)) + ba_ref[...]
        for j, chip in enumerate(chips):
            cp = small(ada_ref, k_me, 7 + j, (chip[0], chip[1], c))
            cp.start()
            started.append(cp)

        for j, chip in enumerate(chips):
            chip_k = 2 * chip[0] + chip[1]
            for i in range(n):
                copy(i, j, chip_k, c, (chip[0], chip[1], c)).wait_recv()
                cp = copy(i, 3 + j, chip_k, c, sibling)
                cp.start()
                started.append(cp)
        for j, chip in enumerate(chips):
            chip_k = 2 * chip[0] + chip[1]
            small(ada_ref, chip_k, 7 + j, (chip[0], chip[1], c)).wait_recv()
            for i in range(n):
                copy(i, 3 + j, chip_k, 1 - c, sibling).wait_recv()
        for cp in started:
            cp.wait_send()

        def ref_cols(slots, a, b):
            runs = []
            for k in range(4):
                lo, hi = max(a, shard_w * k), min(b, shard_w * (k + 1))
                if lo < hi:
                    runs.append(slots[k][:, lo - shard_w * k:hi - shard_w * k])
            return runs

        for r0 in range(0, D_MODEL, rows):
            rs = slice(r0, r0 + rows)
            slots = [g_in[k, rs, :] for k in range(4)]
            row = jnp.concatenate(
                ref_cols(slots, 0, R_ZA) + ref_cols(slots, R_QB, R_FB) + ref_cols(slots, R_FB, R_ZB)
                + [jnp.zeros((rows, W_F - N_HEADS), BF16)] + ref_cols(slots, R_ZA, R_QB) + ref_cols(slots, R_ZB, R_END),
                axis=1)
            wint_ref[rs, :] = row
            wintt_ref[:, rs] = row.T

    vmem = pl.BlockSpec(memory_space=pltpu.VMEM)
    return pl.pallas_call(
        body, name=name,
        out_shape=[jax.ShapeDtypeStruct((4, 8, n_col), F32), jax.ShapeDtypeStruct((D_MODEL, W_INT), BF16),
                   jax.ShapeDtypeStruct((W_INT, D_MODEL), BF16)]
        + [jax.ShapeDtypeStruct((4,) + s.shape, BF16) for s in small_shards],
        in_specs=[vmem] * (3 + n),
        out_specs=[vmem] * (2 + n),
        scratch_shapes=[pltpu.VMEM((4, D_MODEL, shard_w), BF16), pltpu.VMEM((8,) + c_pad.shape, F32),
                        pltpu.SemaphoreType.DMA((6 * n + 10,)), pltpu.SemaphoreType.DMA((6 * n + 10,))],
        compiler_params=_params(vmem=VMEM_LIMIT),
    )(c_pad, w_ada, b_ada_shard, *shards)


def _reduce_scatter(pieces, pack, name):
    n = len(pieces)

    def body(*refs):
        pack_ref, ins = refs[0], refs[1:1 + n]
        packs_ref, outs = refs[1 + n], refs[2 + n:2 + 2 * n]
        rest = refs[2 + 2 * n:]
        own, got = rest[:n], rest[n:2 * n]
        sendb, recvb = rest[2 * n:3 * n], rest[3 * n:4 * n]
        send_sems, recv_sems, local_sems = rest[4 * n:4 * n + 3]
        x, y, c = lax.axis_index("x"), lax.axis_index("y"), lax.axis_index("c")
        k_me = 2 * x + y
        me = 2 * k_me + c
        sibling = (x, y, 1 - c)
        chips = [(_flip(x, fx), _flip(y, fy)) for fx, fy in _CHIP_FLIPS]
        hrs = [p.shape[1] // 2 for p in pieces]

        def remote(i, slot, src, dst, to):
            return pltpu.make_async_remote_copy(
                src_ref=src, dst_ref=dst, send_sem=send_sems.at[5 * i + slot], recv_sem=recv_sems.at[5 * i + slot],
                device_id=to, device_id_type=MESH)

        started = []
        packs_ref[me] = pack_ref[...]
        peers = [(_flip(x, k & 4), _flip(y, k & 2), _flip(c, k & 1)) for k in range(1, 8)]
        for k, peer in enumerate(peers):
            cp = pltpu.make_async_remote_copy(
                src_ref=pack_ref, dst_ref=packs_ref.at[me], send_sem=send_sems.at[5 * n + k],
                recv_sem=recv_sems.at[5 * n + k], device_id=peer, device_id_type=MESH)
            cp.start()
            started.append(cp)
        loads = []
        for i in range(n):
            ld = pltpu.make_async_copy(ins[i].at[:, pl.ds(c * hrs[i], hrs[i]), :], own[i], local_sems.at[i])
            ld.start()
            loads.append(ld)
            cp = remote(i, 0, ins[i].at[:, pl.ds((1 - c) * hrs[i], hrs[i]), :], got[i], sibling)
            cp.start()
            started.append(cp)
        for i in range(n):
            loads[i].wait()
            remote(i, 0, ins[i].at[:, pl.ds(c * hrs[i], hrs[i]), :], got[i], sibling).wait_recv()
            for j, chip in enumerate(chips):
                chip_k = 2 * chip[0] + chip[1]
                sendb[i][j] = (own[i][chip_k] + got[i][chip_k]).astype(BF16)
                cp = remote(i, 1 + j, sendb[i].at[j], recvb[i].at[j], (chip[0], chip[1], c))
                cp.start()
                started.append(cp)
        for i in range(n):
            acc = own[i][k_me] + got[i][k_me]
            for j, chip in enumerate(chips):
                remote(i, 1 + j, sendb[i].at[j], recvb[i].at[j], (chip[0], chip[1], c)).wait_recv()
                acc = acc + recvb[i][j].astype(F32)
            mine = outs[i].at[pl.ds(c * hrs[i], hrs[i]), :]
            outs[i][pl.ds(pl.multiple_of(c * hrs[i], SUBLANES), hrs[i]), :] = acc
            cp = remote(i, 4, mine, mine, sibling)
            cp.start()
            started.append(cp)
        for i in range(n):
            theirs = outs[i].at[pl.ds((1 - c) * hrs[i], hrs[i]), :]
            remote(i, 4, theirs, theirs, sibling).wait_recv()
        for k, peer in enumerate(peers):
            pltpu.make_async_remote_copy(
                src_ref=pack_ref, dst_ref=packs_ref.at[4 * peer[0] + 2 * peer[1] + peer[2]],
                send_sem=send_sems.at[5 * n + k], recv_sem=recv_sems.at[5 * n + k],
                device_id=peer, device_id_type=MESH).wait_recv()
        for cp in started:
            cp.wait_send()

    vmem = pl.BlockSpec(memory_space=pltpu.VMEM)
    scratch = []
    scratch += [pltpu.VMEM((4, p.shape[1] // 2, p.shape[2]), F32) for p in pieces]
    scratch += [pltpu.VMEM((4, p.shape[1] // 2, p.shape[2]), F32) for p in pieces]
    scratch += [pltpu.VMEM((3, p.shape[1] // 2, p.shape[2]), BF16) for p in pieces]
    scratch += [pltpu.VMEM((3, p.shape[1] // 2, p.shape[2]), BF16) for p in pieces]
    scratch += [pltpu.SemaphoreType.DMA((5 * n + 7,)), pltpu.SemaphoreType.DMA((5 * n + 7,)), pltpu.SemaphoreType.DMA((n,))]
    return pl.pallas_call(
        body, name=name,
        out_shape=[jax.ShapeDtypeStruct((8,) + pack.shape, F32)] + [jax.ShapeDtypeStruct(p.shape[1:], F32) for p in pieces],
        in_specs=[vmem] + [pl.BlockSpec(memory_space=pl.ANY)] * n,
        out_specs=[vmem] * (1 + n),
        scratch_shapes=scratch,
        compiler_params=_params(vmem=VMEM_LIMIT),
    )(pack, *pieces)


def _rope_fwd(t, cos, sin, lane):
    lo = (lane % HEAD_DIM) < (HEAD_DIM // 2)
    return t * cos + jnp.where(lo, -pltpu.roll(t, 96, 1), pltpu.roll(t, 32, 1)) * sin


def _norm_proj(x, gmod, shift, w_int, pos, freq, bf_pad, tm):
    s = x.shape[0]

    def body(x_ref, g_ref, sh_ref, w_ref, pos_ref, fr_ref, bf_ref,
             a_ref, vb_ref, f_ref, gg_ref, ht_ref, cos_ref, sin_ref, q_ref, k_ref, v_ref, st_ref, carry):
        @pl.when(pl.program_id(0) == 0)
        def _():
            carry[...] = jnp.zeros_like(carry)

        xv = x_ref[...]
        r = lax.rsqrt(jnp.mean(xv * xv, axis=-1, keepdims=True) + NORM_EPS)
        hf = (xv * r) * g_ref[...] + sh_ref[...]
        hb = hf.astype(BF16)
        ht_ref[...] = hb.T
        pa = _dot(hb, w_ref[:, OFF_A:OFF_A + W_A])
        ang = pos_ref[...] * fr_ref[...]
        cosv, sinv = jnp.cos(ang), jnp.sin(ang)
        cos_ref[...] = cosv
        sin_ref[...] = sinv
        lane = lax.broadcasted_iota(jnp.int32, (tm, LANES), 1)
        for j in range(5):
            t = pa[:, LANES * j:LANES * (j + 1)]
            a_ref[:, LANES * j:LANES * (j + 1)] = _rope_fwd(t, cosv, sinv, lane).astype(BF16)
        a_ref[:, 640:768] = pa[:, 640:768].astype(BF16)
        pf = _dot(hb, w_ref[:, OFF_F:OFF_F + W_F])
        f_ref[...] = pf
        bblk = _dot(hb, w_ref[:, OFF_B:OFF_B + W_B]).astype(BF16)
        vb_ref[...] = bblk[:, 1024:1536]
        gg_ref[...] = _dot(hb, w_ref[:, OFF_G:OFF_G + W_G]).astype(BF16)
        _augment_heads(bblk, _cumsum_tile(pf, bf_ref[...], carry), q_ref, k_ref, v_ref, st_ref)

    row = lambda w: pl.BlockSpec((tm, w), lambda i: (i, 0))
    return pl.pallas_call(
        body, name="norm_proj", grid=(s // tm,),
        out_shape=[jax.ShapeDtypeStruct((s, W_A), BF16), jax.ShapeDtypeStruct((s, 512), BF16),
                   jax.ShapeDtypeStruct((s, W_F), F32), jax.ShapeDtypeStruct((s, W_G), BF16),
                   jax.ShapeDtypeStruct((D_MODEL, s), BF16),
                   jax.ShapeDtypeStruct((s, LANES), F32), jax.ShapeDtypeStruct((s, LANES), F32)]
        + [jax.ShapeDtypeStruct((s, 1024), BF16)] * 3 + [jax.ShapeDtypeStruct((s // tm, SUBLANES, LANES), F32)],
        in_specs=[row(D_MODEL), _const_spec((1, D_MODEL)), _const_spec((1, D_MODEL)), _const_spec((D_MODEL, W_INT)),
                  row(LANES), _const_spec((1, LANES)), _const_spec((1, LANES))],
        out_specs=[row(W_A), row(512), row(W_F), row(W_G), pl.BlockSpec((D_MODEL, tm), lambda i: (0, i)),
                   row(LANES), row(LANES), row(1024), row(1024), row(1024),
                   pl.BlockSpec((1, SUBLANES, LANES), lambda i: (i, 0, 0))],
        scratch_shapes=[pltpu.VMEM((SUBLANES, LANES), F32)],
        compiler_params=_params(("arbitrary",), VMEM_LIMIT),
    )(x, gmod, shift, w_int, pos, freq, bf_pad)


def _log_sigmoid(u):
    return jnp.minimum(u, 0.0) - jnp.log(1.0 + jnp.exp(-jnp.abs(u)))


def _cumsum_tile(f, b_f, carry):
    tb = f.shape[0]
    lane = lax.broadcasted_iota(jnp.int32, (tb, LANES), 1)
    logf = jnp.where(lane < N_HEADS, _log_sigmoid(f + b_f), 0.0)
    hi, mid, lo = _split3(logf)
    rows = lax.broadcasted_iota(jnp.int32, (tb, tb), 0)
    cols = lax.broadcasted_iota(jnp.int32, (tb, tb), 1)
    tril = (cols <= rows).astype(BF16)
    cum = _dot(tril, hi) + _dot(tril, mid) + _dot(tril, lo) + carry[0:1, :]
    carry[...] = jnp.broadcast_to(cum[tb - 1:tb, :], carry.shape)
    return cum


def _aug_lane(h):
    return 64 if h % 2 == 0 else 0


def _augment_heads(bblk, cumv, q_ref, k_ref, v_ref, st_ref):
    t = bblk.shape[0]
    lane = lax.broadcasted_iota(jnp.int32, (t, LANES), 1)
    lane_b = lane.astype(BF16)
    sub8 = lax.broadcasted_iota(jnp.int32, (SUBLANES, LANES), 0)
    lane8 = lax.broadcasted_iota(jnp.int32, (SUBLANES, LANES), 1)
    one = jnp.ones((t, LANES), BF16)
    zero = jnp.zeros((t, LANES), BF16)
    stats = jnp.zeros((SUBLANES, LANES), F32)
    for p in range(4):
        qblk = bblk[:, LANES * p:LANES * (p + 1)] * SCALE
        kblk = bblk[:, 512 + LANES * p:512 + LANES * (p + 1)]
        vblk = bblk[:, 1024 + LANES * p:1024 + LANES * (p + 1)]
        qf, kf = qblk.astype(F32), kblk.astype(F32)
        q2, k2, qk = qf * qf, kf * kf, qf * kf
        for odd in range(2):
            h = 2 * p + odd
            a0 = _aug_lane(h)
            data_b = (lane_b < 64) if odd == 0 else (lane_b >= 64)
            data = (lane < 64) if odd == 0 else (lane >= 64)
            hi, mid, lo = _split3(jnp.broadcast_to(cumv[:, h:h + 1], (t, LANES)))
            ones3_q = (lane_b >= a0 + 3) & (lane_b < a0 + 6)
            ones3_k = (lane_b >= a0) & (lane_b < a0 + 3)
            aug_q = jnp.where(lane_b == a0, hi, jnp.where(lane_b == a0 + 1, mid, jnp.where(
                lane_b == a0 + 2, lo, jnp.where(ones3_q, one, zero))))
            aug_k = jnp.where(ones3_k, one, jnp.where(lane_b == a0 + 3, -hi, jnp.where(
                lane_b == a0 + 4, -mid, jnp.where(lane_b == a0 + 5, -lo, zero))))
            q_ref[:, LANES * h:LANES * (h + 1)] = jnp.where(data_b, qblk, aug_q)
            k_ref[:, LANES * h:LANES * (h + 1)] = jnp.where(data_b, kblk, aug_k)
            v_ref[:, LANES * h:LANES * (h + 1)] = jnp.where(data_b, vblk, jnp.where(lane_b == a0, one, zero))
            qn = jnp.sqrt(jnp.max(jnp.sum(jnp.where(data, q2, 0.0), axis=-1, keepdims=True)))
            kn = jnp.sqrt(jnp.max(jnp.sum(jnp.where(data, k2, 0.0), axis=-1, keepdims=True)))
            dmin = jnp.min(jnp.sum(jnp.where(data, qk, 0.0), axis=-1, keepdims=True))
            c_first, c_last = cumv[0:1, h:h + 1], cumv[t - 1:t, h:h + 1]
            row = jnp.where(lane8 == 0, qn, jnp.where(lane8 == 1, kn, jnp.where(
                lane8 == 2, c_first, jnp.where(lane8 == 3, c_last, jnp.where(lane8 == 4, dmin, 0.0)))))
            stats = jnp.where(sub8 == h, row, stats)
    st_ref[0] = stats


PRUNE_MARGIN = 88.0


def _fox_tile_ranges(stats):
    nt = stats.shape[0]
    qn, kn, c_first, c_last, d_min = (stats[:, :, n] for n in range(5))
    bound = (1.01 * qn[:, None, :] * kn[None, :, :] - jnp.minimum(d_min, 0.0)[:, None, :] + 0.05
             + c_first[:, None, :] - c_last[None, :, :])
    idx = jnp.arange(nt)
    skip = (bound <= -PRUNE_MARGIN) & (idx[None, :, None] < idx[:, None, None])
    first_key = jnp.sum(jnp.cumprod(skip, axis=1), axis=1)
    needed = (idx[None, :, None] >= first_key[:, None, :]) & (idx[None, :, None] <= idx[:, None, None])
    last_query = jnp.max(jnp.where(needed, idx[:, None, None], 0), axis=0)
    n_query = last_query - idx[:, None] + 1
    table = jnp.zeros((4, SUBLANES, LANES), F32)
    for odd in range(2):
        table = table.at[:, odd, :nt].set(first_key[:, odd::2].T.astype(F32))
        table = table.at[:, 2 + odd, :nt].set(n_query[:, odd::2].T.astype(F32))
    return table


def _lane_scalar(block, row, lane_idx):
    sub8 = lax.broadcasted_iota(jnp.int32, (SUBLANES, LANES), 0)
    lane8 = lax.broadcasted_iota(jnp.int32, (SUBLANES, LANES), 1)
    return jnp.sum(jnp.where((sub8 == row) & (lane8 == lane_idx), block, 0.0)).astype(jnp.int32)


def _fox_fwd(qa, ka, va, ranges, t):
    s = qa.shape[0]
    nt = s // t
    nc = t // LANES

    def body(rg_ref, q_ref, k_ref, v_ref, o_ref, lse_ref):
        i = pl.program_id(1)
        lane = lax.broadcasted_iota(jnp.int32, (t, LANES), 1)
        rows = lax.broadcasted_iota(jnp.int32, (t, t), 0)
        cols = lax.broadcasted_iota(jnp.int32, (t, t), 1)
        firsts = [jnp.clip(_lane_scalar(rg_ref[0], hh, i), 0, i) for hh in range(2)]
        first = jnp.maximum(firsts[0], firsts[1])

        def update(js, carry, heads=(0, 1), diagonal=False):
            offs = [pl.multiple_of(j * t, t) for j in js]
            kts = [k_ref[pl.ds(off, t), :] for off in offs]
            vts = [v_ref[pl.ds(off, t), :] for off in offs]
            scs = {hh: [_dot(q_ref[:, LANES * hh:LANES * (hh + 1)], kt[:, LANES * hh:LANES * (hh + 1)], NT) for kt in kts]
                   for hh in heads}
            if diagonal:
                scs = {hh: [jnp.where(cols <= rows, sc, NEG) for sc in scs[hh]] for hh in heads}
            m_new = {}
            for hh in heads:
                part = None
                for sc in scs[hh]:
                    for cch in range(nc):
                        chunk = sc[:, LANES * cch:LANES * (cch + 1)]
                        part = chunk if part is None else jnp.maximum(part, chunk)
                m_new[hh] = jnp.maximum(carry[2 * hh], jnp.max(part, axis=-1, keepdims=True))
            alphas = {hh: jnp.exp(carry[2 * hh] - m_new[hh]) for hh in heads}
            ps = {hh: [jnp.exp(sc - m_new[hh]).astype(BF16) for sc in scs[hh]] for hh in heads}
            out = list(carry)
            for hh in heads:
                pv = None
                for p, vt in zip(ps[hh], vts):
                    term = _dot(p, vt[:, LANES * hh:LANES * (hh + 1)])
                    pv = term if pv is None else pv + term
                out[2 * hh], out[2 * hh + 1] = m_new[hh], alphas[hh] * carry[2 * hh + 1] + pv
            return tuple(out)

        col0 = jnp.full((t, 1), NEG, F32)
        zero = jnp.zeros((t, LANES), F32)
        carry = (col0, zero, col0, zero)
        for hh in range(2):
            carry = lax.fori_loop(firsts[hh], first, lambda j, cr, hh=hh: update([j], cr, heads=(hh,)), carry)
        n_off = i - first
        carry = lax.fori_loop(0, n_off // 2, lambda u, cr: update([first + 2 * u, first + 2 * u + 1], cr), carry)
        carry = lax.fori_loop(0, n_off % 2, lambda u, cr: update([i - 1], cr), carry)
        m0, acc0, m1, acc1 = update([i], carry, diagonal=True)
        l0, l1 = acc0[:, _aug_lane(0):_aug_lane(0) + 1], acc1[:, _aug_lane(1):_aug_lane(1) + 1]
        o_ref[...] = jnp.where(lane < 64, acc0 * (1.0 / l0), acc1 * (1.0 / l1)).astype(BF16)
        sub = lax.broadcasted_iota(jnp.int32, (SUBLANES, t), 0)
        lse0 = jnp.broadcast_to(m0 + jnp.log(l0), (t, LANES)).T[0:SUBLANES, :]
        lse1 = jnp.broadcast_to(m1 + jnp.log(l1), (t, LANES)).T[0:SUBLANES, :]
        lse_ref[0] = jnp.where(sub == 0, lse0, jnp.where(sub == 1, lse1, 0.0))

    pair = pl.BlockSpec((s, 2 * LANES), lambda p, i: (0, p))
    return pl.pallas_call(
        body, name="fox_fwd", grid=(4, nt),
        out_shape=[jax.ShapeDtypeStruct((s, 512), BF16), jax.ShapeDtypeStruct((4, SUBLANES, s), F32)],
        in_specs=[pl.BlockSpec((1, SUBLANES, LANES), lambda p, i: (p, 0, 0)),
                  pl.BlockSpec((t, 2 * LANES), lambda p, i: (i, p)), pair, pair],
        out_specs=[pl.BlockSpec((t, LANES), lambda p, i: (i, p)),
                   pl.BlockSpec((1, SUBLANES, t), lambda p, i: (p, 0, i))],
        compiler_params=_params(("parallel", "arbitrary"), VMEM_LIMIT),
    )(ranges, qa, ka, va)


def _dup_halves(blk, lane):
    f = blk.astype(F32)
    r = pltpu.roll(f, 64, 1)
    return jnp.where(lane < 64, f, r).astype(BF16), jnp.where(lane >= 64, f, r).astype(BF16)


GROUP = 4
GROUP_ROWS = GROUP * WINDOW


def _stack_heads(ref, g, lane):
    parts = []
    for pb in (2 * g, 2 * g + 1):
        blk = ref[:, LANES * pb:LANES * (pb + 1)]
        zero = jnp.zeros_like(blk)
        parts += [jnp.where(lane < 64, blk, zero), jnp.where(lane >= 64, blk, zero)]
    return jnp.concatenate(parts, axis=0)


def _swa_band(a_ref, ap_ref, g, lane):
    k = jnp.concatenate([_dup_halves(ap_ref[:, 512:640], lane)[g], _dup_halves(a_ref[:, 512:640], lane)[g]], axis=0)
    v = jnp.concatenate([_dup_halves(ap_ref[:, 640:768], lane)[g], _dup_halves(a_ref[:, 640:768], lane)[g]], axis=0)
    return k, v


def _swa_logits(q, k, has_prev):
    sc = _dot(q, k, NT) * SCALE
    rr = lax.broadcasted_iota(jnp.int32, sc.shape, 0) % WINDOW
    cc = lax.broadcasted_iota(jnp.int32, sc.shape, 1)
    valid = (cc > rr) & (cc <= rr + WINDOW) & (has_prev | (cc >= WINDOW))
    return jnp.where(valid, sc, NEG)


def _per_head_column(values):
    return jnp.concatenate([jnp.broadcast_to(v, (WINDOW, 1)) for v in values], axis=0)


SWA_BLOCKS = 4
SWA_ROWS = SWA_BLOCKS * WINDOW
SWA_FWD_BLOCKS = 8


def _swa_blocks(a_ref, ap_ref, n_blocks):
    return [ap_ref] + [a_ref.at[pl.ds(WINDOW * jb, WINDOW), :] for jb in range(n_blocks)]


def _swa_fwd(a, sinks):
    s = a.shape[0]
    n_blocks = SWA_FWD_BLOCKS
    n_rows = n_blocks * WINDOW

    def body(sink_ref, a_ref, ap_ref, o_ref, l_ref):
        lane = lax.broadcasted_iota(jnp.int32, (WINDOW, LANES), 1)
        blocks = _swa_blocks(a_ref, ap_ref, n_blocks)
        units = [(jb, g) for jb in range(n_blocks) for g in range(2)]
        sinks_col = [_per_head_column([sink_ref[GROUP * g + hh] for hh in range(GROUP)]) for g in range(2)]
        bands = [_swa_band(blocks[jb + 1], blocks[jb], g, lane) for jb, g in units]
        scs = [_swa_logits(_stack_heads(blocks[jb + 1], g, lane), bands[u][0],
                           (pl.program_id(0) > 0) if jb == 0 else True) for u, (jb, g) in enumerate(units)]
        ms = [jnp.maximum(jnp.max(scs[u], axis=-1, keepdims=True), sinks_col[g]) for u, (jb, g) in enumerate(units)]
        ps = [jnp.exp(scs[u] - ms[u]) for u in range(len(units))]
        dens = [jnp.sum(ps[u], axis=-1, keepdims=True) + jnp.exp(sinks_col[g] - ms[u]) for u, (jb, g) in enumerate(units)]
        outs = [_dot((ps[u] * (1.0 / dens[u])).astype(BF16), bands[u][1]) for u in range(len(units))]
        for jb in range(n_blocks):
            rows = slice(WINDOW * jb, WINDOW * (jb + 1))
            l_all = jnp.zeros((WINDOW, LANES), F32)
            for g in range(2):
                u = 2 * jb + g
                lcol = ms[u] + jnp.log(dens[u])
                for pb in range(2):
                    r0 = 2 * pb * WINDOW
                    o_ref[rows, LANES * (2 * g + pb):LANES * (2 * g + pb + 1)] = jnp.where(
                        lane < 64, outs[u][r0:r0 + WINDOW], outs[u][r0 + WINDOW:r0 + 2 * WINDOW]).astype(BF16)
                for hh in range(GROUP):
                    l_all = jnp.where(lane == GROUP * g + hh, lcol[WINDOW * hh:WINDOW * (hh + 1)], l_all)
            l_ref[rows, :] = l_all

    return pl.pallas_call(
        body, name="swa_fwd", grid=(s // n_rows,),
        out_shape=[jax.ShapeDtypeStruct((s, 512), BF16), jax.ShapeDtypeStruct((s, LANES), F32)],
        in_specs=[pl.BlockSpec(memory_space=pltpu.SMEM),
                  pl.BlockSpec((n_rows, W_A), lambda i: (i, 0)),
                  pl.BlockSpec((WINDOW, W_A), lambda i: (jnp.maximum(n_blocks * i - 1, 0), 0))],
        out_specs=[pl.BlockSpec((n_rows, 512), lambda i: (i, 0)), pl.BlockSpec((n_rows, LANES), lambda i: (i, 0))],
        compiler_params=_params(("parallel",)),
    )(sinks, a, a)


def _mid(att_a, att_b, g, x, target, gate, g_final, wo_a, wo_b, w_out, tm=256):
    s = x.shape[0]
    nt = s // tm

    def body(aa_ref, ab_ref, g_ref, x_ref, t_ref, gate_ref, gf_ref, woa_ref, wob_ref, wout_ref,
             dx_ref, daa_ref, dab_ref, dg_ref, delta_ref, dwoa_ref, dwob_ref, dwout_ref, vec_ref,
             acc_gf, acc_gate, acc_loss):
        step = pl.program_id(0)

        @pl.when(step == 0)
        def _():
            dwoa_ref[...] = jnp.zeros_like(dwoa_ref)
            dwob_ref[...] = jnp.zeros_like(dwob_ref)
            dwout_ref[...] = jnp.zeros_like(dwout_ref)
            acc_gf[...] = jnp.zeros_like(acc_gf)
            acc_gate[...] = jnp.zeros_like(acc_gate)
            acc_loss[...] = jnp.zeros_like(acc_loss)

        def fold(v):
            return jnp.sum(v.reshape(tm // SUBLANES, SUBLANES, D_MODEL), axis=0)

        gate = gate_ref[...]
        gfin = gf_ref[...]
        branches = []
        for att_ref, z_off, wo_ref in ((aa_ref, 0, woa_ref), (ab_ref, 512, wob_ref)):
            att = att_ref[...].astype(F32)
            z = g_ref[:, z_off:z_off + 512].astype(F32)
            sz = _sigmoid(z)
            silu = z * sz
            u = (att * silu).astype(BF16)
            branches.append((att, z, sz, silu, u, _dot(u, wo_ref[...])))
        ga = g_ref[:, 1024:2048].astype(F32)
        gb = g_ref[:, 2048:3072].astype(F32)
        sga, sgb = _sigmoid(ga), _sigmoid(gb)
        y_a, y_b = branches[0][5], branches[1][5]
        mb = (sga * y_a + sgb * y_b).astype(BF16)
        o = _dot(mb, wout_ref[...])
        x2 = x_ref[...] + gate * o
        r2 = lax.rsqrt(jnp.mean(x2 * x2, axis=-1, keepdims=True) + NORM_EPS)
        xn2 = x2 * r2
        err = xn2 * gfin - t_ref[...]
        acc_loss[...] += fold(err * err)
        dy = err * (1.0 / D_MODEL)
        acc_gf[...] += fold(dy * xn2)
        dxn = dy * gfin
        dx2 = r2 * (dxn - xn2 * jnp.mean(dxn * xn2, axis=-1, keepdims=True))
        dx_ref[...] = dx2
        acc_gate[...] += fold(dx2 * o)
        d_o = (dx2 * gate).astype(BF16)
        dwout_ref[...] += _dot(mb, d_o, TN)
        dm = _dot(d_o, wout_ref[...], NT)
        dg_ref[:, 1024:2048] = (dm * y_a * sga * (1.0 - sga)).astype(BF16)
        dg_ref[:, 2048:3072] = (dm * y_b * sgb * (1.0 - sgb)).astype(BF16)
        for (att, z, sz, silu, u, _), sg, wo_ref, dwo_ref, datt_ref, z_off in (
                (branches[0], sga, woa_ref, dwoa_ref, daa_ref, 0), (branches[1], sgb, wob_ref, dwob_ref, dab_ref, 512)):
            dyb = (dm * sg).astype(BF16)
            dwo_ref[...] += _dot(u, dyb, TN)
            du = _dot(dyb, wo_ref[...], NT)
            datt = du * silu
            datt_ref[...] = datt.astype(BF16)
            dg_ref[:, z_off:z_off + 512] = (du * att * (sz * (1.0 + z * (1.0 - sz)))).astype(BF16)
            if z_off == 512:
                prod = datt * att
                hi = prod.astype(BF16)
                lo = (prod - hi.astype(F32)).astype(BF16)
                er = lax.broadcasted_iota(jnp.int32, (512, LANES), 0)
                ec = lax.broadcasted_iota(jnp.int32, (512, LANES), 1)
                e = (er // HEAD_DIM == ec).astype(BF16)
                delta = _dot(hi, e) + _dot(lo, e)
                delta_ref[...] = delta.T[0:SUBLANES, :]

        @pl.when(step == nt - 1)
        def _():
            sub = lax.broadcasted_iota(jnp.int32, (SUBLANES, D_MODEL), 0)
            dgf = jnp.sum(acc_gf[...], axis=0, keepdims=True)
            dgate = jnp.sum(acc_gate[...], axis=0, keepdims=True)
            loss = 0.5 * jnp.sum(acc_loss[...]) * (1.0 / D_MODEL)
            vec_ref[...] = jnp.where(sub == 0, dgf, jnp.where(sub == 1, dgate, jnp.where(sub == 2, loss, 0.0)))

    row = lambda w: pl.BlockSpec((tm, w), lambda i: (i, 0))
    return pl.pallas_call(
        body, name="mid", grid=(nt,),
        out_shape=[jax.ShapeDtypeStruct((s, D_MODEL), F32), jax.ShapeDtypeStruct((s, 512), BF16),
                   jax.ShapeDtypeStruct((s, 512), BF16), jax.ShapeDtypeStruct((s, W_G), BF16),
                   jax.ShapeDtypeStruct((SUBLANES, s), F32),
                   jax.ShapeDtypeStruct((512, D_MODEL), F32), jax.ShapeDtypeStruct((512, D_MODEL), F32),
                   jax.ShapeDtypeStruct((D_MODEL, D_MODEL), F32), jax.ShapeDtypeStruct((SUBLANES, D_MODEL), F32)],
        in_specs=[row(512), row(512), row(W_G), row(D_MODEL), row(D_MODEL),
                  _const_spec((1, D_MODEL)), _const_spec((1, D_MODEL)),
                  _const_spec((512, D_MODEL)), _const_spec((512, D_MODEL)), _const_spec((D_MODEL, D_MODEL))],
        out_specs=[row(D_MODEL), row(512), row(512), row(W_G),
                   pl.BlockSpec((SUBLANES, tm), lambda i: (0, i)),
                   pl.BlockSpec((512, D_MODEL), lambda i: (0, 0)), pl.BlockSpec((512, D_MODEL), lambda i: (0, 0)),
                   pl.BlockSpec((D_MODEL, D_MODEL), lambda i: (0, 0)), pl.BlockSpec((SUBLANES, D_MODEL), lambda i: (0, 0))],
        scratch_shapes=[pltpu.VMEM((SUBLANES, D_MODEL), F32)] * 3,
        compiler_params=_params(("arbitrary",), VMEM_LIMIT),
    )(att_a, att_b, g, x, target, gate, g_final, wo_a, wo_b, w_out)


def _rope_bwd(dt, cos, sin, lane):
    u = dt * sin
    lo = (lane % HEAD_DIM) < (HEAD_DIM // 2)
    return dt * cos + jnp.where(lo, pltpu.roll(u, 96, 1), -pltpu.roll(u, 32, 1))


def _swa_bwd(a, datt, l_all, sinks, cos, sin):
    s = a.shape[0]
    nt = s // SWA_ROWS

    def body(sink_ref, a_ref, ap_ref, do_ref, l_ref, cos_ref, sin_ref, da_ref, ds_ref, halo):
        step = pl.program_id(0)
        tile = nt - 1 - step

        @pl.when(step == 0)
        def _():
            halo[...] = jnp.zeros_like(halo)
            ds_ref[...] = jnp.zeros_like(ds_ref)

        lane = lax.broadcasted_iota(jnp.int32, (WINDOW, LANES), 1)
        sub8 = lax.broadcasted_iota(jnp.int32, (SUBLANES, LANES), 0)
        lane8 = lax.broadcasted_iota(jnp.int32, (SUBLANES, LANES), 1)
        blocks = _swa_blocks(a_ref, ap_ref, SWA_BLOCKS)
        dsink = jnp.zeros((SUBLANES, LANES), F32)

        def join(pair, r0):
            x0, x1 = pair[0][r0:r0 + WINDOW], pair[1][r0:r0 + WINDOW]
            return jnp.where(lane < 64, x0 + pltpu.roll(x0, 64, 1), x1 + pltpu.roll(x1, 64, 1))

        units = [(jb, g) for jb in range(SWA_BLOCKS) for g in range(2)]
        n_u = len(units)
        sinks_col = [_per_head_column([sink_ref[GROUP * g + hh] for hh in range(GROUP)]) for g in range(2)]
        bands = [_swa_band(blocks[jb + 1], blocks[jb], g, lane) for jb, g in units]
        qs = [_stack_heads(blocks[jb + 1], g, lane) for jb, g in units]
        doms = [_stack_heads(do_ref.at[pl.ds(WINDOW * jb, WINDOW), :], g, lane) for jb, g in units]
        lcols = []
        for jb, g in units:
            lv = l_ref[WINDOW * jb:WINDOW * (jb + 1), :]
            lcols.append(_per_head_column([lv[:, GROUP * g + hh:GROUP * g + hh + 1] for hh in range(GROUP)]))
        ps = [jnp.exp(_swa_logits(qs[u], bands[u][0], (tile > 0) if jb == 0 else True) - lcols[u])
              for u, (jb, g) in enumerate(units)]
        dps = [_dot(doms[u], bands[u][1], NT) for u in range(n_u)]
        deltas = [jnp.sum(ps[u] * dps[u], axis=-1, keepdims=True) for u in range(n_u)]
        for u, (jb, g) in enumerate(units):
            sink_term = jnp.exp(sinks_col[g] - lcols[u]) * deltas[u]
            for hh in range(GROUP):
                tot = jnp.sum(sink_term[WINDOW * hh:WINDOW * (hh + 1)])
                dsink = dsink + jnp.where((sub8 == 0) & (lane8 == GROUP * g + hh), -tot, 0.0)
        dss = [(ps[u] * (dps[u] - deltas[u])).astype(BF16) for u in range(n_u)]
        dqs = [_dot(dss[u], bands[u][0]) * SCALE for u in range(n_u)]
        dks = [_dot(dss[u], qs[u], TN) * SCALE for u in range(n_u)]
        dvs = [_dot(ps[u].astype(BF16), doms[u], TN) for u in range(n_u)]

        carry_k, carry_v = halo[:, 0:LANES], halo[:, LANES:2 * LANES]
        for jb in reversed(range(SWA_BLOCKS)):
            rows = slice(WINDOW * jb, WINDOW * (jb + 1))
            cosv, sinv = cos_ref[rows, :], sin_ref[rows, :]
            for g in range(2):
                dq = dqs[2 * jb + g]
                for pb in range(2):
                    r0 = 2 * pb * WINDOW
                    dq_pair = jnp.where(lane < 64, dq[r0:r0 + WINDOW], dq[r0 + WINDOW:r0 + 2 * WINDOW])
                    da_ref[rows, LANES * (2 * g + pb):LANES * (2 * g + pb + 1)] = _rope_bwd(
                        dq_pair, cosv, sinv, lane).astype(BF16)
            dkb, dvb = dks[2 * jb:2 * jb + 2], dvs[2 * jb:2 * jb + 2]
            da_ref[rows, 512:640] = _rope_bwd(join(dkb, WINDOW) + carry_k, cosv, sinv, lane).astype(BF16)
            da_ref[rows, 640:768] = (join(dvb, WINDOW) + carry_v).astype(BF16)
            carry_k, carry_v = join(dkb, 0), join(dvb, 0)
        halo[:, 0:LANES] = carry_k
        halo[:, LANES:2 * LANES] = carry_v
        ds_ref[...] += dsink

    rev = lambda w: pl.BlockSpec((SWA_ROWS, w), lambda i: (nt - 1 - i, 0))
    return pl.pallas_call(
        body, name="swa_bwd", grid=(nt,),
        out_shape=[jax.ShapeDtypeStruct((s, W_A), BF16), jax.ShapeDtypeStruct((SUBLANES, LANES), F32)],
        in_specs=[pl.BlockSpec(memory_space=pltpu.SMEM), rev(W_A),
                  pl.BlockSpec((WINDOW, W_A), lambda i: (jnp.maximum(SWA_BLOCKS * (nt - 1 - i) - 1, 0), 0)),
                  rev(512), rev(LANES), rev(LANES), rev(LANES)],
        out_specs=[rev(W_A), pl.BlockSpec((SUBLANES, LANES), lambda i: (0, 0))],
        scratch_shapes=[pltpu.VMEM((WINDOW, 2 * LANES), F32)],
        compiler_params=_params(("arbitrary",)),
    )(sinks, a, a, datt, l_all, cos, sin)


def _fox_bwd(qa, ka, vb, do, lse, delta, ranges, t):
    s = qa.shape[0]
    nt = s // t

    def body(rg_ref, q_ref, do_ref, lse_ref, dl_ref, k_ref, v_ref, dq_ref, dk_ref, dv_ref, dc_ref, dr_ref, dq_acc):
        p = pl.program_id(0)
        j = pl.program_id(1)
        n_queries = [jnp.clip(_lane_scalar(rg_ref[0], 2 + hh, j), 1, nt - j) for hh in range(2)]

        @pl.when(j == 0)
        def _():
            dq_acc[...] = jnp.zeros_like(dq_acc)

        lane = lax.broadcasted_iota(jnp.int32, (t, LANES), 1)
        rows = lax.broadcasted_iota(jnp.int32, (t, t), 0)
        cols = lax.broadcasted_iota(jnp.int32, (t, t), 1)
        kt = k_ref[...]
        vt = v_ref[...]

        ks = [kt[:, LANES * hh:LANES * (hh + 1)] for hh in range(2)]

        def tile(qis, carry, heads=(0, 1), diagonal=False):
            dk0, dk1, dv = carry
            offs = [pl.multiple_of(i * t, t) for i in qis]
            units = [(u, hh) for u in range(len(qis)) for hh in heads]
            qts = [q_ref[pl.ds(off, t), :] for off in offs]
            dos = [do_ref[pl.ds(off, t), :] for off in offs]
            lses = [lse_ref[0, :, pl.ds(off, t)] for off in offs]
            dls = [dl_ref[0, :, pl.ds(off, t)] for off in offs]
            qs = [qts[u][:, LANES * hh:LANES * (hh + 1)] for u, hh in units]
            doms = [jnp.where((lane < 64) if hh == 0 else (lane >= 64), dos[u], jnp.zeros_like(dos[u])) for u, hh in units]
            sts = [_dot(ks[hh], qs[n], NT) for n, (u, hh) in enumerate(units)]
            dpts = [_dot(vt, doms[n], NT) for n in range(len(units))]
            if diagonal:
                sts = [jnp.where(cols >= rows, st, NEG) for st in sts]
            pts = [jnp.exp(sts[n] - lses[u][hh:hh + 1, :]) for n, (u, hh) in enumerate(units)]
            dsts = [(pts[n] * (dpts[n] - dls[u][hh:hh + 1, :])).astype(BF16) for n, (u, hh) in enumerate(units)]
            for n, (u, hh) in enumerate(units):
                dv = dv + _dot(pts[n].astype(BF16), doms[n])
                term = _dot(dsts[n], qs[n])
                dk0, dk1 = (dk0 + term, dk1) if hh == 0 else (dk0, dk1 + term)
                dq_acc[hh, pl.ds(offs[u], t), :] += _dot(dsts[n], ks[hh], TN)
            return dk0, dk1, dv

        zero = jnp.zeros((t, LANES), F32)
        carry = tile([j], (zero, zero, zero), diagonal=True)
        n_rest = jnp.minimum(n_queries[0], n_queries[1]) - 1
        carry = lax.fori_loop(0, n_rest // 2, lambda u, cr: tile([j + 1 + 2 * u, j + 2 + 2 * u], cr), carry)
        carry = lax.fori_loop(0, n_rest % 2, lambda u, cr: tile([j + n_rest], cr), carry)
        for hh in range(2):
            carry = lax.fori_loop(j + 1 + n_rest, j + n_queries[hh], lambda i, cr, hh=hh: tile([i], cr, heads=(hh,)), carry)
        dk0, dk1, dv = carry
        e0, e1 = _aug_lane(0), _aug_lane(1)
        dk_ref[...] = jnp.where(lane < 64, dk0, dk1).astype(BF16)
        dv_ref[...] = dv.astype(BF16)
        c0 = jnp.broadcast_to(dk0[:, e0 + 3:e0 + 4], (t, LANES))
        c1 = jnp.broadcast_to(dk1[:, e1 + 3:e1 + 4], (t, LANES))
        dc_ref[0] = jnp.where(lane == 2 * p, -c0, jnp.where(lane == 2 * p + 1, -c1, 0.0))

        @pl.when(j == nt - 1)
        def _():
            lane_s = lax.broadcasted_iota(jnp.int32, (s, LANES), 1)
            a0, a1 = dq_acc[0], dq_acc[1]
            dq_ref[...] = (jnp.where(lane_s < 64, a0, a1) * SCALE).astype(BF16)
            r0 = jnp.broadcast_to(a0[:, e0:e0 + 1], (s, LANES))
            r1 = jnp.broadcast_to(a1[:, e1:e1 + 1], (s, LANES))
            dr_ref[0] = jnp.where(lane_s == 2 * p, r0, jnp.where(lane_s == 2 * p + 1, r1, 0.0))

    return pl.pallas_call(
        body, name="fox_bwd", grid=(4, nt),
        out_shape=[jax.ShapeDtypeStruct((s, 512), BF16), jax.ShapeDtypeStruct((s, 512), BF16),
                   jax.ShapeDtypeStruct((s, 512), BF16), jax.ShapeDtypeStruct((4, s, LANES), F32),
                   jax.ShapeDtypeStruct((4, s, LANES), F32)],
        in_specs=[pl.BlockSpec((1, SUBLANES, LANES), lambda p, j: (p, 0, 0)),
                  pl.BlockSpec((s, 2 * LANES), lambda p, j: (0, p)),
                  pl.BlockSpec((s, LANES), lambda p, j: (0, p)),
                  pl.BlockSpec((1, SUBLANES, s), lambda p, j: (p, 0, 0)),
                  pl.BlockSpec((1, SUBLANES, s), lambda p, j: (p, 0, 0)),
                  pl.BlockSpec((t, 2 * LANES), lambda p, j: (j, p)),
                  pl.BlockSpec((t, LANES), lambda p, j: (j, p))],
        out_specs=[pl.BlockSpec((s, LANES), lambda p, j: (0, p)),
                   pl.BlockSpec((t, LANES), lambda p, j: (j, p)),
                   pl.BlockSpec((t, LANES), lambda p, j: (j, p)),
                   pl.BlockSpec((1, t, LANES), lambda p, j: (p, j, 0)),
                   pl.BlockSpec((1, s, LANES), lambda p, j: (p, 0, 0))],
        scratch_shapes=[pltpu.VMEM((2, s, LANES), F32)],
        compiler_params=_params(("parallel", "arbitrary"), VMEM_LIMIT),
    )(ranges, qa, do, lse, delta, ka, vb)


def _forget_logit_grad(dc_ref, dr_ref, f, b_f, carry):
    tb = f.shape[0]
    lane = lax.broadcasted_iota(jnp.int32, (tb, LANES), 1)
    dc = dc_ref[0] + dr_ref[0]
    for k in range(1, 4):
        dc = dc + (dc_ref[k] + dr_ref[k])
    hi, mid, lo = _split3(dc)
    rows = lax.broadcasted_iota(jnp.int32, (tb, tb), 0)
    cols = lax.broadcasted_iota(jnp.int32, (tb, tb), 1)
    triu = (cols >= rows).astype(BF16)
    dlogf = _dot(triu, hi) + _dot(triu, mid) + _dot(triu, lo) + carry[0:1, :]
    carry[...] = jnp.broadcast_to(dlogf[0:1, :], carry.shape)
    return jnp.where(lane < N_HEADS, dlogf * _sigmoid(-(f + b_f)), 0.0)


def _dh_norm_bwd(d_a, d_q, d_k, d_v, dcum_k, dcum_q, f, bf_pad, d_g, w_t, x, dx2, gnorm, scale1, tm=512):
    s = x.shape[0]
    nt = s // tm

    def body(da_ref, dq_ref, dk_ref, dv_ref, dc_ref, dr_ref, f_ref, bf_ref, dg_ref, w_ref, x_ref, dx2_ref, g_ref, sc_ref,
             gx_ref, vec_ref, df_ref, db_ref, a_sh, a_sc, a_g, carry):
        step = pl.program_id(0)

        @pl.when(step == 0)
        def _():
            a_sh[...] = jnp.zeros_like(a_sh)
            a_sc[...] = jnp.zeros_like(a_sc)
            a_g[...] = jnp.zeros_like(a_g)
            carry[...] = jnp.zeros_like(carry)
            db_ref[...] = jnp.zeros_like(db_ref)

        def fold(v):
            return jnp.sum(v.reshape(tm // SUBLANES, SUBLANES, D_MODEL), axis=0)

        dfb = _forget_logit_grad(dc_ref, dr_ref, f_ref[...], bf_ref[...], carry)
        d_f = dfb.astype(BF16)
        df_ref[...] = d_f
        sub8 = lax.broadcasted_iota(jnp.int32, (SUBLANES, LANES), 0)
        db_ref[...] += jnp.where(sub8 == 0, jnp.sum(dfb, axis=0, keepdims=True), 0.0)
        d_all = jnp.concatenate([da_ref[...], dq_ref[...], dk_ref[...], dv_ref[...], d_f, dg_ref[...]], axis=1)
        dh = _dot(d_all, w_ref[...])
        xv = x_ref[...]
        r = lax.rsqrt(jnp.mean(xv * xv, axis=-1, keepdims=True) + NORM_EPS)
        xn = xv * r
        gn = g_ref[...]
        a_sh[...] += fold(dh)
        a_sc[...] += fold(dh * (xn * gn))
        dn1 = dh * sc_ref[...]
        a_g[...] += fold(dn1 * xn)
        dxn = dn1 * gn
        gx_ref[...] = dx2_ref[...] + r * (dxn - xn * jnp.mean(dxn * xn, axis=-1, keepdims=True))

        @pl.when(step == nt - 1)
        def _():
            sub = lax.broadcasted_iota(jnp.int32, (SUBLANES, D_MODEL), 0)
            v_sh = jnp.sum(a_sh[...], axis=0, keepdims=True)
            v_sc = jnp.sum(a_sc[...], axis=0, keepdims=True)
            v_g = jnp.sum(a_g[...], axis=0, keepdims=True)
            vec_ref[...] = jnp.where(sub == 0, v_sh, jnp.where(sub == 1, v_sc, jnp.where(sub == 2, v_g, 0.0)))

    row = lambda w: pl.BlockSpec((tm, w), lambda i: (nt - 1 - i, 0))
    slabs = pl.BlockSpec((4, tm, LANES), lambda i: (0, nt - 1 - i, 0))
    return pl.pallas_call(
        body, name="dh_norm_bwd", grid=(nt,),
        out_shape=[jax.ShapeDtypeStruct((s, D_MODEL), F32), jax.ShapeDtypeStruct((SUBLANES, D_MODEL), F32),
                   jax.ShapeDtypeStruct((s, LANES), BF16), jax.ShapeDtypeStruct((SUBLANES, LANES), F32)],
        in_specs=[row(W_A), row(512), row(512), row(512), slabs, slabs, row(W_F), _const_spec((1, LANES)), row(W_G),
                  _const_spec((W_INT, D_MODEL)), row(D_MODEL), row(D_MODEL), _const_spec((1, D_MODEL)),
                  _const_spec((1, D_MODEL))],
        out_specs=[row(D_MODEL), pl.BlockSpec((SUBLANES, D_MODEL), lambda i: (0, 0)), row(LANES),
                   pl.BlockSpec((SUBLANES, LANES), lambda i: (0, 0))],
        scratch_shapes=[pltpu.VMEM((SUBLANES, D_MODEL), F32)] * 3 + [pltpu.VMEM((SUBLANES, LANES), F32)],
        compiler_params=_params(("arbitrary",), VMEM_LIMIT),
    )(d_a, d_q, d_k, d_v, dcum_k, dcum_q, f, bf_pad, d_g, w_t, x, dx2, gnorm, scale1)


def _dw_in(h_t, d_a, d_q, d_k, d_v, d_f, d_g, ts=1024, tc=512):
    s = h_t.shape[1]
    ns = s // ts
    w_fg = 512 + W_F + W_G - 512
    rows = 128
    n_slot = (R_END // 4 + LANES - 1) // LANES * LANES
    order = [(0, 0, W_A), (4, 0, 512), (1, 0, 512), (2, 0, 512), (3, 0, 512), (4, 512, N_HEADS), (4, 512 + W_F, W_G - 512)]

    def slot_pieces(k):
        lo, hi, out, col = (R_END // 4) * k, (R_END // 4) * (k + 1), [], 0
        for acc_i, c0, w in order:
            a, b = max(lo, col), min(hi, col + w)
            if a < b:
                out.append((acc_i, c0 + a - col, b - a))
            col += w
        return out

    def body(h_ref, da_ref, dq_ref, dk_ref, dv_ref, df_ref, dg_ref, o_ref, acc_a, acc_q, acc_k, acc_v, acc_fg, stage, sem):
        k = pl.program_id(0)
        accs = (acc_a, acc_q, acc_k, acc_v, acc_fg)

        @pl.when(k == 0)
        def _():
            for acc in accs:
                acc[...] = jnp.zeros_like(acc)

        hv = h_ref[...]

        def add(acc, c_acc, d_ref, c_d, width):
            for c0 in range(0, width, tc):
                w = min(tc, width - c0)
                acc[:, c_acc + c0:c_acc + c0 + w] += _dot(hv, d_ref[:, c_d + c0:c_d + c0 + w])

        add(acc_a, 0, da_ref, 0, W_A)
        add(acc_q, 0, dq_ref, 0, 512)
        add(acc_k, 0, dk_ref, 0, 512)
        add(acc_v, 0, dv_ref, 0, 512)
        add(acc_fg, 0, dg_ref, 0, 512)
        add(acc_fg, 512, df_ref, 0, W_F)
        add(acc_fg, 512 + W_F, dg_ref, 512, W_G - 512)

        @pl.when(k == ns - 1)
        def _():
            pending = [None, None]
            for n, r0 in enumerate(range(0, D_MODEL, rows)):
                buf = n % 2
                if pending[buf] is not None:
                    for cp in pending[buf]:
                        cp.wait()
                rs = slice(r0, r0 + rows)
                copies = []
                for slot in range(4):
                    parts = [accs[acc_i][rs, c0:c0 + w] for acc_i, c0, w in slot_pieces(slot)]
                    parts.append(jnp.zeros((rows, n_slot - R_END // 4), F32))
                    stage[buf, slot] = jnp.concatenate(parts, axis=1)
                    cp = pltpu.make_async_copy(stage.at[buf, slot], o_ref.at[slot, pl.ds(r0, rows), :], sem.at[4 * buf + slot])
                    cp.start()
                    copies.append(cp)
                pending[buf] = copies
            for copies in pending:
                for cp in copies:
                    cp.wait()

    spec = lambda d: pl.BlockSpec((ts, d.shape[1]), lambda k: (k, 0))
    return pl.pallas_call(
        body, name="dw_in", grid=(ns,),
        out_shape=jax.ShapeDtypeStruct((4, D_MODEL, n_slot), F32),
        in_specs=[pl.BlockSpec((D_MODEL, ts), lambda k: (0, k))] + [spec(d) for d in (d_a, d_q, d_k, d_v, d_f, d_g)],
        out_specs=pl.BlockSpec(memory_space=pl.ANY),
        scratch_shapes=[pltpu.VMEM((D_MODEL, W_A), F32)] + [pltpu.VMEM((D_MODEL, 512), F32)] * 3
        + [pltpu.VMEM((D_MODEL, w_fg), F32), pltpu.VMEM((2, 4, rows, n_slot), F32), pltpu.SemaphoreType.DMA((8,))],
        compiler_params=_params(("arbitrary",), VMEM_LIMIT),
    )(h_t, d_a, d_q, d_k, d_v, d_f, d_g)


def _small_grads(packs, c_t, dada_shard):
    def body(p_ref, ct_ref, da_ref, sum_ref, gw_ref):
        acc = p_ref[0]
        for dev in range(1, 8):
            acc = acc + p_ref[dev]
        sum_ref[...] = acc
        gw_ref[...] = jnp.dot(ct_ref[...], da_ref[...], preferred_element_type=F32, precision=lax.Precision.HIGHEST)

    return pl.pallas_call(
        body, name="small_grads",
        out_shape=[jax.ShapeDtypeStruct(packs.shape[1:], F32),
                   jax.ShapeDtypeStruct((c_t.shape[0], dada_shard.shape[1]), F32)],
    )(packs, c_t, dada_shard)


def _adamw_body(w_ref, g_ref, m_ref, v_ref, d_ref, mo_ref, vo_ref):
    c1 = 1.0 / (1.0 - ADAM_B1 ** ADAM_STEP)
    c2 = 1.0 / (1.0 - ADAM_B2 ** ADAM_STEP)
    gv = g_ref[...]
    mn = ADAM_B1 * m_ref[...] + (1.0 - ADAM_B1) * gv
    vn = ADAM_B2 * v_ref[...] + (1.0 - ADAM_B2) * (gv * gv)
    mo_ref[...] = mn
    vo_ref[...] = vn
    d_ref[...] = -ADAM_LR * ((mn * c1) / (jnp.sqrt(vn * c2) + ADAM_EPS) + ADAM_WD * w_ref[...])


def _adamw3(w, g, m, v, name, tb=128):
    spec = pl.BlockSpec((tb, SUBLANES, LANES), lambda i: (i, 0, 0))
    return pl.pallas_call(
        functools.partial(_adamw_body), name=name, grid=(pl.cdiv(w.shape[0], tb),),
        out_shape=[jax.ShapeDtypeStruct(w.shape, F32)] * 3,
        in_specs=[spec] * 4, out_specs=[spec] * 3,
        compiler_params=_params(("parallel",)),
    )(w, g, m, v)


def _adamw_many(items, name):
    n = len(items)

    def body(*refs):
        for i in range(n):
            _adamw_body(*refs[4 * i:4 * i + 4], *refs[4 * n + 3 * i:4 * n + 3 * i + 3])

    return pl.pallas_call(
        body, name=name,
        out_shape=[jax.ShapeDtypeStruct(it[0].shape, F32) for it in items for _ in range(3)],
        compiler_params=_params(vmem=VMEM_LIMIT),
    )(*[arr for it in items for arr in it])


def _rope_inputs(positions):
    inv_freq = 10000.0 ** (-jnp.arange(0, HEAD_DIM, 2, dtype=F32) / HEAD_DIM)
    pos = jnp.broadcast_to(positions.astype(F32)[:, None], (positions.shape[0], LANES))
    return pos, jnp.tile(inv_freq, 4)[None, :]


def _pad_rows(v, rows=SUBLANES):
    return jnp.pad(v, ((0, rows - v.shape[0]), (0, 0)))


def kernel(x, c, positions, w_ada, b_ada, g_norm, w_in, b_f, sinks, w_o_swa, w_o_fox, w_out, g_final, loss_target, m_w_ada, m_b_ada, m_g_norm, m_w_in, m_b_f, m_sinks, m_w_o_swa, m_w_o_fox, m_w_out, m_g_final, v_w_ada, v_b_ada, v_g_norm, v_w_in, v_b_f, v_sinks, v_w_o_swa, v_w_o_fox, v_w_out, v_g_final):
    ix, iy, ic = lax.axis_index("x"), lax.axis_index("y"), lax.axis_index("c")
    chip = 2 * ix + iy
    dev = 2 * chip + ic
    xs, tgt = x[0], loss_target[0]
    s = xs.shape[0]

    b_ada_shard = lax.dynamic_slice(b_ada, (0, chip * 768), (1, 768))
    ada_parts, w_int, w_int_t, g_oa, g_ob, g_out = _gather_inputs(
        _pad_rows(c), w_ada[0], b_ada_shard, jnp.transpose(w_in, (2, 0, 1)).reshape(-1, LANES),
        [w_o_swa[0], w_o_fox[0], w_out[0]], "gather_inputs")
    ada = lax.dynamic_index_in_dim(ada_parts, dev, axis=1, keepdims=False).reshape(1, 3 * D_MODEL)
    shift, scale, gate = ada[:, :D_MODEL], ada[:, D_MODEL:2 * D_MODEL], ada[:, 2 * D_MODEL:]
    scale1 = 1.0 + scale

    wo_a = jnp.transpose(g_oa, (1, 0, 2)).reshape(512, D_MODEL)
    wo_b = jnp.transpose(g_ob, (1, 0, 2)).reshape(512, D_MODEL)
    w_o = g_out.reshape(D_MODEL, D_MODEL)

    pos, freq = _rope_inputs(positions[0])
    bf_pad = jnp.pad(b_f, ((0, 0), (0, LANES - N_HEADS)))
    sink_vec = sinks[0]

    a, vb, f, g, h_t, cos, sin, qa, ka, va, stats = _norm_proj(
        xs, g_norm * scale1, shift, w_int, pos, freq, bf_pad, FOX_TILE)
    att_a, l_swa = _swa_fwd(a, sink_vec)
    ranges = _fox_tile_ranges(stats)
    att_b, lse = _fox_fwd(qa, ka, va, ranges, FOX_TILE)

    dx2, datt_a, datt_b, d_g, delta8, dwo_a, dwo_b, dw_out, vec_mid = _mid(
        att_a, att_b, g, xs, tgt, gate, g_final.reshape(1, D_MODEL), wo_a, wo_b, w_o)
    delta = jnp.pad(delta8.reshape(4, 2, s), ((0, 0), (0, SUBLANES - 2), (0, 0)))
    d_a, dsink = _swa_bwd(a, datt_a, l_swa, sink_vec, cos, sin)
    dq, dk, dv, dcum_k, dcum_q = _fox_bwd(qa, ka, vb, datt_b, lse, delta, ranges, FOX_TILE)
    grad_x, vec_dh, d_f, dbf = _dh_norm_bwd(
        d_a, dq, dk, dv, dcum_k, dcum_q, f, bf_pad, d_g, w_int_t, xs, dx2, g_norm, scale1)
    dw_in_slots = _dw_in(h_t, d_a, dq, dk, dv, d_f, d_g)

    tail = jnp.pad(jnp.concatenate([dbf[0:1, :N_HEADS], dsink[0:1, :N_HEADS]], axis=1), ((0, 0), (0, D_MODEL - 2 * N_HEADS)))
    pack = jnp.concatenate([c, vec_dh[0:2], vec_mid[1:2], vec_dh[2:3], vec_mid[0:1], tail, vec_mid[2:3]], axis=0)

    def slots(w, axis):
        if axis == 1:
            return jnp.transpose(w.reshape(w.shape[0], 4, w.shape[1] // 4), (1, 0, 2))
        return w.reshape(4, w.shape[0] // 4, w.shape[1])

    packs, g_wo_a, g_wo_b, g_w_out, g_w_in = _reduce_scatter(
        [slots(dwo_a, 1), slots(dwo_b, 1), slots(dw_out, 0), dw_in_slots], pack, "reduce_grads")
    g_w_in = g_w_in[:, :w_in.shape[2]]
    dada_all = packs[:, 1:4, :].reshape(8, 3 * D_MODEL)
    dada_shard = lax.dynamic_slice(dada_all, (0, chip * 768), (8, 768))
    sums, g_w_ada = _small_grads(packs, packs[:, 0, :].T, dada_shard)
    g_b_ada = sums[1:4].reshape(1, 3 * D_MODEL)
    g_g_norm = sums[4:5]
    g_g_final = sums[5]
    g_b_f = sums[6:7, :N_HEADS]
    g_sinks = sums[6:7, N_HEADS:2 * N_HEADS]
    loss = sums[7, 0]

    grads = {
        "w_ada": g_w_ada, "b_ada": g_b_ada, "g_norm": g_g_norm, "w_in": g_w_in, "b_f": g_b_f, "sinks": g_sinks,
        "w_o_swa": g_wo_a, "w_o_fox": g_wo_b, "w_out": g_w_out, "g_final": g_g_final,
    }
    params = {
        "w_ada": (w_ada, m_w_ada, v_w_ada), "b_ada": (b_ada, m_b_ada, v_b_ada), "g_norm": (g_norm, m_g_norm, v_g_norm),
        "w_in": (w_in, m_w_in, v_w_in), "b_f": (b_f, m_b_f, v_b_f), "sinks": (sinks, m_sinks, v_sinks),
        "w_o_swa": (w_o_swa, m_w_o_swa, v_w_o_swa), "w_o_fox": (w_o_fox, m_w_o_fox, v_w_o_fox),
        "w_out": (w_out, m_w_out, v_w_out), "g_final": (g_final, m_g_final, v_g_final),
    }
    n_col = w_in.shape[2]

    def as_stored(t):
        return jnp.transpose(t, (2, 0, 1)).reshape(n_col, SUBLANES, LANES)

    def from_stored(t):
        return jnp.transpose(t, (1, 2, 0)).reshape(1, D_MODEL, n_col)

    names = list(grads)
    others = [nm for nm in names if nm != "w_in"]

    def as_2d(t):
        return t.reshape((t.shape[-2], t.shape[-1]) if t.ndim >= 2 else (1, t.shape[0]))

    flat = _adamw_many([tuple(as_2d(t) for t in (params[nm][0], grads[nm], params[nm][1], params[nm][2])) for nm in others],
                       "adamw_small")
    results = {}
    for i, nm in enumerate(others):
        shape = params[nm][0].shape
        results[nm] = [t.reshape(shape) for t in (grads[nm], *flat[3 * i:3 * i + 3])]
    w, m, v = params["w_in"]
    g_st = as_stored(grads["w_in"][None])
    d_, m_, v_ = _adamw3(as_stored(w), g_st, as_stored(m), as_stored(v), "adamw_w_in")
    results["w_in"] = [from_stored(t) for t in (g_st, d_, m_, v_)]
    return (loss, grad_x[None], *[results[nm][0] for nm in names], *[results[nm][1] for nm in names],
            *[results[nm][2] for nm in names], *[results[nm][3] for nm in names])
```

```python
import functools

import numpy as np
import jax
import jax.numpy as jnp
from jax import lax
from jax.experimental import pallas as pl
from jax.experimental.pallas import tpu as pltpu

F32 = jnp.float32
BF16 = jnp.bfloat16
MESH = pl.DeviceIdType.MESH

D_MODEL = 1024
HEAD_DIM = 64
N_HEADS = 8
WINDOW = 128
NORM_EPS = 1e-6
SCALE = HEAD_DIM ** -0.5
NEG = -1e30
LANES = 128
SUBLANES = 8
VMEM_LIMIT = 60 * 1024 * 1024
FOX_TILE = 512

W_A, W_B, W_F, W_G = 768, 1536, 128, 3072
OFF_A, OFF_B, OFF_F, OFF_G = 0, 768, 2304, 2432
W_INT = W_A + W_B + W_F + W_G
R_ZA, R_QB, R_FB, R_ZB, R_END = 768, 1280, 2816, 2824, 5384

ADAM_LR, ADAM_B1, ADAM_B2, ADAM_EPS, ADAM_WD, ADAM_STEP = 0.001, 0.9, 0.999, 1e-08, 0.01, 10

NT = (((1,), (1,)), ((), ()))
TN = (((0,), (0,)), ((), ()))


def _dot(a, b, dims=None):
    if dims is None:
        return jnp.dot(a, b, preferred_element_type=F32)
    return lax.dot_general(a, b, dims, preferred_element_type=F32)


def _split3(v):
    hi = v.astype(BF16)
    r1 = v - hi.astype(F32)
    mid = r1.astype(BF16)
    lo = (r1 - mid.astype(F32)).astype(BF16)
    return hi, mid, lo


def _sigmoid(v):
    return 1.0 / (1.0 + jnp.exp(-v))


def _params(sem=None, vmem=None):
    return pltpu.CompilerParams(dimension_semantics=sem, vmem_limit_bytes=vmem)


def _const_spec(shape):
    nd = len(shape)
    return pl.BlockSpec(shape, lambda *_: (0,) * nd, pipeline_mode=pl.Buffered(1))


def _flip(v, f):
    return 1 - v if f else v


_CHIP_FLIPS = ((1, 0), (0, 1), (1, 1))


def _gather_inputs(c_pad, w_ada, b_ada_shard, w_in_shard, small_shards, name):
    shards = [w_in_shard] + list(small_shards)
    n = len(shards)
    n_col = w_ada.shape[1]
    shard_w = w_in_shard.shape[0] // SUBLANES
    rows = 128

    def body(*refs):
        c_ref, wa_ref, ba_ref = refs[:3]
        ins = refs[3:3 + n]
        ada_ref, wint_ref, wintt_ref = refs[3 + n:6 + n]
        g_in, call_ref, send_sems, recv_sems = refs[5 + 2 * n:9 + 2 * n]
        outs = (g_in,) + tuple(refs[6 + n:5 + 2 * n])
        x, y, c = lax.axis_index("x"), lax.axis_index("y"), lax.axis_index("c")
        k_me = 2 * x + y
        me = 2 * k_me + c
        sibling = (x, y, 1 - c)
        chips = [(_flip(x, fx), _flip(y, fy)) for fx, fy in _CHIP_FLIPS]

        def piece(i, chip_k, half):
            hr = outs[i].shape[1] // 2
            return outs[i].at[chip_k, pl.ds(half * hr, hr), :]

        def copy(i, slot, chip_k, half, to):
            return pltpu.make_async_remote_copy(
                src_ref=piece(i, chip_k, half), dst_ref=piece(i, chip_k, half),
                send_sem=send_sems.at[6 * i + slot], recv_sem=recv_sems.at[6 * i + slot],
                device_id=to, device_id_type=MESH)

        def small(ref, slot, sem, to):
            return pltpu.make_async_remote_copy(
                src_ref=ref.at[slot], dst_ref=ref.at[slot], send_sem=send_sems.at[6 * n + sem],
                recv_sem=recv_sems.at[6 * n + sem], device_id=to, device_id_type=MESH)

        whole = shard_w // LANES * LANES
        for a in range(SUBLANES):
            main = ins[0][pl.ds(a, whole, stride=SUBLANES), :]
            tail = ins[0][pl.ds(a + SUBLANES * whole, shard_w - whole, stride=SUBLANES), :]
            tail = jnp.concatenate([tail, jnp.zeros((LANES - (shard_w - whole), LANES), F32)], axis=0)
            blk = jnp.concatenate([main.T, tail.T[:, :shard_w - whole]], axis=1)
            g_in[k_me, LANES * a:LANES * (a + 1), :] = blk.astype(BF16)
        for i in range(1, n):
            outs[i][k_me] = ins[i][...].astype(BF16)
        started = []
        for i in range(n):
            for j, chip in enumerate(chips):
                cp = copy(i, j, k_me, c, (chip[0], chip[1], c))
                cp.start()
                started.append(cp)

        call_ref[me] = c_ref[...]
        peers = [(_flip(x, k & 4), _flip(y, k & 2), _flip(c, k & 1)) for k in range(1, 8)]
        for k, peer in enumerate(peers):
            cp = small(call_ref, me, k, peer)
            cp.start()
            started.append(cp)
        for k, peer in enumerate(peers):
            small(call_ref, 4 * peer[0] + 2 * peer[1] + peer[2], k, peer).wait_recv()
        c_all = call_ref[:, 0, :].astype(BF16)
        ada_ref[k_me] = _dot(c_all, wa_ref[...].astype(BF16)) + ba_ref[...]
        for j, chip in enumerate(chips):
            cp = small(ada_ref, k_me, 7 + j, (chip[0], chip[1], c))
            cp.start()
            started.append(cp)

        for j, chip in enumerate(chips):
            chip_k = 2 * chip[0] + chip[1]
            for i in range(n):
                copy(i, j, chip_k, c, (chip[0], chip[1], c)).wait_recv()
                cp = copy(i, 3 + j, chip_k, c, sibling)
                cp.start()
                started.append(cp)
        for j, chip in enumerate(chips):
            chip_k = 2 * chip[0] + chip[1]
            small(ada_ref, chip_k, 7 + j, (chip[0], chip[1], c)).wait_recv()
            for i in range(n):
                copy(i, 3 + j, chip_k, 1 - c, sibling).wait_recv()
        for cp in started:
            cp.wait_send()

        def ref_cols(slots, a, b):
            runs = []
            for k in range(4):
                lo, hi = max(a, shard_w * k), min(b, shard_w * (k + 1))
                if lo < hi:
                    runs.append(slots[k][:, lo - shard_w * k:hi - shard_w * k])
            return runs

        for r0 in range(0, D_MODEL, rows):
            rs = slice(r0, r0 + rows)
            slots = [g_in[k, rs, :] for k in range(4)]
            row = jnp.concatenate(
                ref_cols(slots, 0, R_ZA) + ref_cols(slots, R_QB, R_FB) + ref_cols(slots, R_FB, R_ZB)
                + [jnp.zeros((rows, W_F - N_HEADS), BF16)] + ref_cols(slots, R_ZA, R_QB) + ref_cols(slots, R_ZB, R_END),
                axis=1)
            wint_ref[rs, :] = row
            wintt_ref[:, rs] = row.T

    vmem = pl.BlockSpec(memory_space=pltpu.VMEM)
    return pl.pallas_call(
        body, name=name,
        out_shape=[jax.ShapeDtypeStruct((4, 8, n_col), F32), jax.ShapeDtypeStruct((D_MODEL, W_INT), BF16),
                   jax.ShapeDtypeStruct((W_INT, D_MODEL), BF16)]
        + [jax.ShapeDtypeStruct((4,) + s.shape, BF16) for s in small_shards],
        in_specs=[vmem] * (3 + n),
        out_specs=[vmem] * (2 + n),
        scratch_shapes=[pltpu.VMEM((4, D_MODEL, shard_w), BF16), pltpu.VMEM((8,) + c_pad.shape, F32),
                        pltpu.SemaphoreType.DMA((6 * n + 10,)), pltpu.SemaphoreType.DMA((6 * n + 10,))],
        compiler_params=_params(vmem=VMEM_LIMIT),
    )(c_pad, w_ada, b_ada_shard, *shards)


def _reduce_scatter(pieces, pack, name):
    n = len(pieces)

    def body(*refs):
        pack_ref, ins = refs[0], refs[1:1 + n]
        packs_ref, outs = refs[1 + n], refs[2 + n:2 + 2 * n]
        rest = refs[2 + 2 * n:]
        own, got = rest[:n], rest[n:2 * n]
        sendb, recvb = rest[2 * n:3 * n], rest[3 * n:4 * n]
        send_sems, recv_sems, local_sems = rest[4 * n:4 * n + 3]
        x, y, c = lax.axis_index("x"), lax.axis_index("y"), lax.axis_index("c")
        k_me = 2 * x + y
        me = 2 * k_me + c
        sibling = (x, y, 1 - c)
        chips = [(_flip(x, fx), _flip(y, fy)) for fx, fy in _CHIP_FLIPS]
        hrs = [p.shape[1] // 2 for p in pieces]

        def remote(i, slot, src, dst, to):
            return pltpu.make_async_remote_copy(
                src_ref=src, dst_ref=dst, send_sem=send_sems.at[5 * i + slot], recv_sem=recv_sems.at[5 * i + slot],
                device_id=to, device_id_type=MESH)

        started = []
        packs_ref[me] = pack_ref[...]
        peers = [(_flip(x, k & 4), _flip(y, k & 2), _flip(c, k & 1)) for k in range(1, 8)]
        for k, peer in enumerate(peers):
            cp = pltpu.make_async_remote_copy(
                src_ref=pack_ref, dst_ref=packs_ref.at[me], send_sem=send_sems.at[5 * n + k],
                recv_sem=recv_sems.at[5 * n + k], device_id=peer, device_id_type=MESH)
            cp.start()
            started.append(cp)
        loads = []
        for i in range(n):
            ld = pltpu.make_async_copy(ins[i].at[:, pl.ds(c * hrs[i], hrs[i]), :], own[i], local_sems.at[i])
            ld.start()
            loads.append(ld)
            cp = remote(i, 0, ins[i].at[:, pl.ds((1 - c) * hrs[i], hrs[i]), :], got[i], sibling)
            cp.start()
            started.append(cp)
        for i in range(n):
            loads[i].wait()
            remote(i, 0, ins[i].at[:, pl.ds(c * hrs[i], hrs[i]), :], got[i], sibling).wait_recv()
            for j, chip in enumerate(chips):
                chip_k = 2 * chip[0] + chip[1]
                sendb[i][j] = (own[i][chip_k] + got[i][chip_k]).astype(BF16)
                cp = remote(i, 1 + j, sendb[i].at[j], recvb[i].at[j], (chip[0], chip[1], c))
                cp.start()
                started.append(cp)
        for i in range(n):
            acc = own[i][k_me] + got[i][k_me]
            for j, chip in enumerate(chips):
                remote(i, 1 + j, sendb[i].at[j], recvb[i].at[j], (chip[0], chip[1], c)).wait_recv()
                acc = acc + recvb[i][j].astype(F32)
            mine = outs[i].at[pl.ds(c * hrs[i], hrs[i]), :]
            outs[i][pl.ds(pl.multiple_of(c * hrs[i], SUBLANES), hrs[i]), :] = acc
            cp = remote(i, 4, mine, mine, sibling)
            cp.start()
            started.append(cp)
        for i in range(n):
            theirs = outs[i].at[pl.ds((1 - c) * hrs[i], hrs[i]), :]
            remote(i, 4, theirs, theirs, sibling).wait_recv()
        for k, peer in enumerate(peers):
            pltpu.make_async_remote_copy(
                src_ref=pack_ref, dst_ref=packs_ref.at[4 * peer[0] + 2 * peer[1] + peer[2]],
                send_sem=send_sems.at[5 * n + k], recv_sem=recv_sems.at[5 * n + k],
                device_id=peer, device_id_type=MESH).wait_recv()
        for cp in started:
            cp.wait_send()

    vmem = pl.BlockSpec(memory_space=pltpu.VMEM)
    scratch = []
    scratch += [pltpu.VMEM((4, p.shape[1] // 2, p.shape[2]), F32) for p in pieces]
    scratch += [pltpu.VMEM((4, p.shape[1] // 2, p.shape[2]), F32) for p in pieces]
    scratch += [pltpu.VMEM((3, p.shape[1] // 2, p.shape[2]), BF16) for p in pieces]
    scratch += [pltpu.VMEM((3, p.shape[1] // 2, p.shape[2]), BF16) for p in pieces]
    scratch += [pltpu.SemaphoreType.DMA((5 * n + 7,)), pltpu.SemaphoreType.DMA((5 * n + 7,)), pltpu.SemaphoreType.DMA((n,))]
    return pl.pallas_call(
        body, name=name,
        out_shape=[jax.ShapeDtypeStruct((8,) + pack.shape, F32)] + [jax.ShapeDtypeStruct(p.shape[1:], F32) for p in pieces],
        in_specs=[vmem] + [pl.BlockSpec(memory_space=pl.ANY)] * n,
        out_specs=[vmem] * (1 + n),
        scratch_shapes=scratch,
        compiler_params=_params(vmem=VMEM_LIMIT),
    )(pack, *pieces)


def _rope_fwd(t, cos, sin, lane):
    lo = (lane % HEAD_DIM) < (HEAD_DIM // 2)
    return t * cos + jnp.where(lo, -pltpu.roll(t, 96, 1), pltpu.roll(t, 32, 1)) * sin


def _norm_proj(x, gmod, shift, w_int, pos, freq, bf_pad, tm):
    s = x.shape[0]

    def body(x_ref, g_ref, sh_ref, w_ref, pos_ref, fr_ref, bf_ref,
             a_ref, vb_ref, f_ref, gg_ref, ht_ref, cos_ref, sin_ref, q_ref, k_ref, v_ref, st_ref, carry):
        @pl.when(pl.program_id(0) == 0)
        def _():
            carry[...] = jnp.zeros_like(carry)

        xv = x_ref[...]
        r = lax.rsqrt(jnp.mean(xv * xv, axis=-1, keepdims=True) + NORM_EPS)
        hf = (xv * r) * g_ref[...] + sh_ref[...]
        hb = hf.astype(BF16)
        ht_ref[...] = hb.T
        pa = _dot(hb, w_ref[:, OFF_A:OFF_A + W_A])
        ang = pos_ref[...] * fr_ref[...]
        cosv, sinv = jnp.cos(ang), jnp.sin(ang)
        cos_ref[...] = cosv
        sin_ref[...] = sinv
        lane = lax.broadcasted_iota(jnp.int32, (tm, LANES), 1)
        for j in range(5):
            t = pa[:, LANES * j:LANES * (j + 1)]
            a_ref[:, LANES * j:LANES * (j + 1)] = _rope_fwd(t, cosv, sinv, lane).astype(BF16)
        a_ref[:, 640:768] = pa[:, 640:768].astype(BF16)
        pf = _dot(hb, w_ref[:, OFF_F:OFF_F + W_F])
        f_ref[...] = pf
        bblk = _dot(hb, w_ref[:, OFF_B:OFF_B + W_B]).astype(BF16)
        vb_ref[...] = bblk[:, 1024:1536]
        gg_ref[...] = _dot(hb, w_ref[:, OFF_G:OFF_G + W_G]).astype(BF16)
        _augment_heads(bblk, _cumsum_tile(pf, bf_ref[...], carry), q_ref, k_ref, v_ref, st_ref)

    row = lambda w: pl.BlockSpec((tm, w), lambda i: (i, 0))
    return pl.pallas_call(
        body, name="norm_proj", grid=(s // tm,),
        out_shape=[jax.ShapeDtypeStruct((s, W_A), BF16), jax.ShapeDtypeStruct((s, 512), BF16),
                   jax.ShapeDtypeStruct((s, W_F), F32), jax.ShapeDtypeStruct((s, W_G), BF16),
                   jax.ShapeDtypeStruct((D_MODEL, s), BF16),
                   jax.ShapeDtypeStruct((s, LANES), F32), jax.ShapeDtypeStruct((s, LANES), F32)]
        + [jax.ShapeDtypeStruct((s, 1024), BF16)] * 3 + [jax.ShapeDtypeStruct((s // tm, SUBLANES, LANES), F32)],
        in_specs=[row(D_MODEL), _const_spec((1, D_MODEL)), _const_spec((1, D_MODEL)), _const_spec((D_MODEL, W_INT)),
                  row(LANES), _const_spec((1, LANES)), _const_spec((1, LANES))],
        out_specs=[row(W_A), row(512), row(W_F), row(W_G), pl.BlockSpec((D_MODEL, tm), lambda i: (0, i)),
                   row(LANES), row(LANES), row(1024), row(1024), row(1024),
                   pl.BlockSpec((1, SUBLANES, LANES), lambda i: (i, 0, 0))],
        scratch_shapes=[pltpu.VMEM((SUBLANES, LANES), F32)],
        compiler_params=_params(("arbitrary",), VMEM_LIMIT),
    )(x, gmod, shift, w_int, pos, freq, bf_pad)


def _log_sigmoid(u):
    return jnp.minimum(u, 0.0) - jnp.log(1.0 + jnp.exp(-jnp.abs(u)))


def _cumsum_tile(f, b_f, carry):
    tb = f.shape[0]
    lane = lax.broadcasted_iota(jnp.int32, (tb, LANES), 1)
    logf = jnp.where(lane < N_HEADS, _log_sigmoid(f + b_f), 0.0)
    hi, mid, lo = _split3(logf)
    rows = lax.broadcasted_iota(jnp.int32, (tb, tb), 0)
    cols = lax.broadcasted_iota(jnp.int32, (tb, tb), 1)
    tril = (cols <= rows).astype(BF16)
    cum = _dot(tril, hi) + _dot(tril, mid) + _dot(tril, lo) + carry[0:1, :]
    carry[...] = jnp.broadcast_to(cum[tb - 1:tb, :], carry.shape)
    return cum


def _aug_lane(h):
    return 64 if h % 2 == 0 else 0


def _augment_heads(bblk, cumv, q_ref, k_ref, v_ref, st_ref):
    t = bblk.shape[0]
    lane = lax.broadcasted_iota(jnp.int32, (t, LANES), 1)
    lane_b = lane.astype(BF16)
    sub8 = lax.broadcasted_iota(jnp.int32, (SUBLANES, LANES), 0)
    lane8 = lax.broadcasted_iota(jnp.int32, (SUBLANES, LANES), 1)
    one = jnp.ones((t, LANES), BF16)
    zero = jnp.zeros((t, LANES), BF16)
    stats = jnp.zeros((SUBLANES, LANES), F32)
    for p in range(4):
        qblk = bblk[:, LANES * p:LANES * (p + 1)] * SCALE
        kblk = bblk[:, 512 + LANES * p:512 + LANES * (p + 1)]
        vblk = bblk[:, 1024 + LANES * p:1024 + LANES * (p + 1)]
        qf, kf = qblk.astype(F32), kblk.astype(F32)
        q2, k2, qk = qf * qf, kf * kf, qf * kf
        for odd in range(2):
            h = 2 * p + odd
            a0 = _aug_lane(h)
            data_b = (lane_b < 64) if odd == 0 else (lane_b >= 64)
            data = (lane < 64) if odd == 0 else (lane >= 64)
            hi, mid, lo = _split3(jnp.broadcast_to(cumv[:, h:h + 1], (t, LANES)))
            ones3_q = (lane_b >= a0 + 3) & (lane_b < a0 + 6)
            ones3_k = (lane_b >= a0) & (lane_b < a0 + 3)
            aug_q = jnp.where(lane_b == a0, hi, jnp.where(lane_b == a0 + 1, mid, jnp.where(
                lane_b == a0 + 2, lo, jnp.where(ones3_q, one, zero))))
            aug_k = jnp.where(ones3_k, one, jnp.where(lane_b == a0 + 3, -hi, jnp.where(
                lane_b == a0 + 4, -mid, jnp.where(lane_b == a0 + 5, -lo, zero))))
            q_ref[:, LANES * h:LANES * (h + 1)] = jnp.where(data_b, qblk, aug_q)
            k_ref[:, LANES * h:LANES * (h + 1)] = jnp.where(data_b, kblk, aug_k)
            v_ref[:, LANES * h:LANES * (h + 1)] = jnp.where(data_b, vblk, jnp.where(lane_b == a0, one, zero))
            qn = jnp.sqrt(jnp.max(jnp.sum(jnp.where(data, q2, 0.0), axis=-1, keepdims=True)))
            kn = jnp.sqrt(jnp.max(jnp.sum(jnp.where(data, k2, 0.0), axis=-1, keepdims=True)))
            dmin = jnp.min(jnp.sum(jnp.where(data, qk, 0.0), axis=-1, keepdims=True))
            c_first, c_last = cumv[0:1, h:h + 1], cumv[t - 1:t, h:h + 1]
            row = jnp.where(lane8 == 0, qn, jnp.where(lane8 == 1, kn, jnp.where(
                lane8 == 2, c_first, jnp.where(lane8 == 3, c_last, jnp.where(lane8 == 4, dmin, 0.0)))))
            stats = jnp.where(sub8 == h, row, stats)
    st_ref[0] = stats


PRUNE_MARGIN = 88.0


def _fox_tile_ranges(stats):
    nt = stats.shape[0]
    qn, kn, c_first, c_last, d_min = (stats[:, :, n] for n in range(5))
    bound = (1.01 * qn[:, None, :] * kn[None, :, :] - jnp.minimum(d_min, 0.0)[:, None, :] + 0.05
             + c_first[:, None, :] - c_last[None, :, :])
    idx = jnp.arange(nt)
    skip = (bound <= -PRUNE_MARGIN) & (idx[None, :, None] < idx[:, None, None])
    first_key = jnp.sum(jnp.cumprod(skip, axis=1), axis=1)
    needed = (idx[None, :, None] >= first_key[:, None, :]) & (idx[None, :, None] <= idx[:, None, None])
    last_query = jnp.max(jnp.where(needed, idx[:, None, None], 0), axis=0)
    n_query = last_query - idx[:, None] + 1
    table = jnp.zeros((4, SUBLANES, LANES), F32)
    for odd in range(2):
        table = table.at[:, odd, :nt].set(first_key[:, odd::2].T.astype(F32))
        table = table.at[:, 2 + odd, :nt].set(n_query[:, odd::2].T.astype(F32))
    return table


def _lane_scalar(block, row, lane_idx):
    sub8 = lax.broadcasted_iota(jnp.int32, (SUBLANES, LANES), 0)
    lane8 = lax.broadcasted_iota(jnp.int32, (SUBLANES, LANES), 1)
    return jnp.sum(jnp.where((sub8 == row) & (lane8 == lane_idx), block, 0.0)).astype(jnp.int32)


def _fox_fwd(qa, ka, va, ranges, t):
    s = qa.shape[0]
    nt = s // t
    nc = t // LANES

    def body(rg_ref, q_ref, k_ref, v_ref, o_ref, lse_ref):
        i = pl.program_id(1)
        lane = lax.broadcasted_iota(jnp.int32, (t, LANES), 1)
        rows = lax.broadcasted_iota(jnp.int32, (t, t), 0)
        cols = lax.broadcasted_iota(jnp.int32, (t, t), 1)
        firsts = [jnp.clip(_lane_scalar(rg_ref[0], hh, i), 0, i) for hh in range(2)]
        first = jnp.maximum(firsts[0], firsts[1])

        def update(js, carry, heads=(0, 1), diagonal=False):
            offs = [pl.multiple_of(j * t, t) for j in js]
            kts = [k_ref[pl.ds(off, t), :] for off in offs]
            vts = [v_ref[pl.ds(off, t), :] for off in offs]
            scs = {hh: [_dot(q_ref[:, LANES * hh:LANES * (hh + 1)], kt[:, LANES * hh:LANES * (hh + 1)], NT) for kt in kts]
                   for hh in heads}
            if diagonal:
                scs = {hh: [jnp.where(cols <= rows, sc, NEG) for sc in scs[hh]] for hh in heads}
            m_new = {}
            for hh in heads:
                part = None
                for sc in scs[hh]:
                    for cch in range(nc):
                        chunk = sc[:, LANES * cch:LANES * (cch + 1)]
                        part = chunk if part is None else jnp.maximum(part, chunk)
                m_new[hh] = jnp.maximum(carry[2 * hh], jnp.max(part, axis=-1, keepdims=True))
            alphas = {hh: jnp.exp(carry[2 * hh] - m_new[hh]) for hh in heads}
            ps = {hh: [jnp.exp(sc - m_new[hh]).astype(BF16) for sc in scs[hh]] for hh in heads}
            out = list(carry)
            for hh in heads:
                pv = None
                for p, vt in zip(ps[hh], vts):
                    term = _dot(p, vt[:, LANES * hh:LANES * (hh + 1)])
                    pv = term if pv is None else pv + term
                out[2 * hh], out[2 * hh + 1] = m_new[hh], alphas[hh] * carry[2 * hh + 1] + pv
            return tuple(out)

        col0 = jnp.full((t, 1), NEG, F32)
        zero = jnp.zeros((t, LANES), F32)
        carry = (col0, zero, col0, zero)
        for hh in range(2):
            carry = lax.fori_loop(firsts[hh], first, lambda j, cr, hh=hh: update([j], cr, heads=(hh,)), carry)
        n_off = i - first
        carry = lax.fori_loop(0, n_off // 2, lambda u, cr: update([first + 2 * u, first + 2 * u + 1], cr), carry)
        carry = lax.fori_loop(0, n_off % 2, lambda u, cr: update([i - 1], cr), carry)
        m0, acc0, m1, acc1 = update([i], carry, diagonal=True)
        l0, l1 = acc0[:, _aug_lane(0):_aug_lane(0) + 1], acc1[:, _aug_lane(1):_aug_lane(1) + 1]
        o_ref[...] = jnp.where(lane < 64, acc0 * (1.0 / l0), acc1 * (1.0 / l1)).astype(BF16)
        sub = lax.broadcasted_iota(jnp.int32, (SUBLANES, t), 0)
        lse0 = jnp.broadcast_to(m0 + jnp.log(l0), (t, LANES)).T[0:SUBLANES, :]
        lse1 = jnp.broadcast_to(m1 + jnp.log(l1), (t, LANES)).T[0:SUBLANES, :]
        lse_ref[0] = jnp.where(sub == 0, lse0, jnp.where(sub == 1, lse1, 0.0))

    pair = pl.BlockSpec((s, 2 * LANES), lambda p, i: (0, p))
    return pl.pallas_call(
        body, name="fox_fwd", grid=(4, nt),
        out_shape=[jax.ShapeDtypeStruct((s, 512), BF16), jax.ShapeDtypeStruct((4, SUBLANES, s), F32)],
        in_specs=[pl.BlockSpec((1, SUBLANES, LANES), lambda p, i: (p, 0, 0)),
                  pl.BlockSpec((t, 2 * LANES), lambda p, i: (i, p)), pair, pair],
        out_specs=[pl.BlockSpec((t, LANES), lambda p, i: (i, p)),
                   pl.BlockSpec((1, SUBLANES, t), lambda p, i: (p, 0, i))],
        compiler_params=_params(("parallel", "arbitrary"), VMEM_LIMIT),
    )(ranges, qa, ka, va)


def _dup_halves(blk, lane):
    f = blk.astype(F32)
    r = pltpu.roll(f, 64, 1)
    return jnp.where(lane < 64, f, r).astype(BF16), jnp.where(lane >= 64, f, r).astype(BF16)


GROUP = 4
GROUP_ROWS = GROUP * WINDOW


def _stack_heads(ref, g, lane):
    parts = []
    for pb in (2 * g, 2 * g + 1):
        blk = ref[:, LANES * pb:LANES * (pb + 1)]
        zero = jnp.zeros_like(blk)
        parts += [jnp.where(lane < 64, blk, zero), jnp.where(lane >= 64, blk, zero)]
    return jnp.concatenate(parts, axis=0)


def _swa_band(a_ref, ap_ref, g, lane):
    k = jnp.concatenate([_dup_halves(ap_ref[:, 512:640], lane)[g], _dup_halves(a_ref[:, 512:640], lane)[g]], axis=0)
    v = jnp.concatenate([_dup_halves(ap_ref[:, 640:768], lane)[g], _dup_halves(a_ref[:, 640:768], lane)[g]], axis=0)
    return k, v


def _swa_logits(q, k, has_prev):
    sc = _dot(q, k, NT) * SCALE
    rr = lax.broadcasted_iota(jnp.int32, sc.shape, 0) % WINDOW
    cc = lax.broadcasted_iota(jnp.int32, sc.shape, 1)
    valid = (cc > rr) & (cc <= rr + WINDOW) & (has_prev | (cc >= WINDOW))
    return jnp.where(valid, sc, NEG)


def _per_head_column(values):
    return jnp.concatenate([jnp.broadcast_to(v, (WINDOW, 1)) for v in values], axis=0)


SWA_BLOCKS = 4
SWA_ROWS = SWA_BLOCKS * WINDOW
SWA_FWD_BLOCKS = 8


def _swa_blocks(a_ref, ap_ref, n_blocks):
    return [ap_ref] + [a_ref.at[pl.ds(WINDOW * jb, WINDOW), :] for jb in range(n_blocks)]


def _swa_fwd(a, sinks):
    s = a.shape[0]
    n_blocks = SWA_FWD_BLOCKS
    n_rows = n_blocks * WINDOW

    def body(sink_ref, a_ref, ap_ref, o_ref, l_ref):
        lane = lax.broadcasted_iota(jnp.int32, (WINDOW, LANES), 1)
        blocks = _swa_blocks(a_ref, ap_ref, n_blocks)
        units = [(jb, g) for jb in range(n_blocks) for g in range(2)]
        sinks_col = [_per_head_column([sink_ref[GROUP * g + hh] for hh in range(GROUP)]) for g in range(2)]
        bands = [_swa_band(blocks[jb + 1], blocks[jb], g, lane) for jb, g in units]
        scs = [_swa_logits(_stack_heads(blocks[jb + 1], g, lane), bands[u][0],
                           (pl.program_id(0) > 0) if jb == 0 else True) for u, (jb, g) in enumerate(units)]
        ms = [jnp.maximum(jnp.max(scs[u], axis=-1, keepdims=True), sinks_col[g]) for u, (jb, g) in enumerate(units)]
        ps = [jnp.exp(scs[u] - ms[u]) for u in range(len(units))]
        dens = [jnp.sum(ps[u], axis=-1, keepdims=True) + jnp.exp(sinks_col[g] - ms[u]) for u, (jb, g) in enumerate(units)]
        outs = [_dot((ps[u] * (1.0 / dens[u])).astype(BF16), bands[u][1]) for u in range(len(units))]
        for jb in range(n_blocks):
            rows = slice(WINDOW * jb, WINDOW * (jb + 1))
            l_all = jnp.zeros((WINDOW, LANES), F32)
            for g in range(2):
                u = 2 * jb + g
                lcol = ms[u] + jnp.log(dens[u])
                for pb in range(2):
                    r0 = 2 * pb * WINDOW
                    o_ref[rows, LANES * (2 * g + pb):LANES * (2 * g + pb + 1)] = jnp.where(
                        lane < 64, outs[u][r0:r0 + WINDOW], outs[u][r0 + WINDOW:r0 + 2 * WINDOW]).astype(BF16)
                for hh in range(GROUP):
                    l_all = jnp.where(lane == GROUP * g + hh, lcol[WINDOW * hh:WINDOW * (hh + 1)], l_all)
            l_ref[rows, :] = l_all

    return pl.pallas_call(
        body, name="swa_fwd", grid=(s // n_rows,),
        out_shape=[jax.ShapeDtypeStruct((s, 512), BF16), jax.ShapeDtypeStruct((s, LANES), F32)],
        in_specs=[pl.BlockSpec(memory_space=pltpu.SMEM),
                  pl.BlockSpec((n_rows, W_A), lambda i: (i, 0)),
                  pl.BlockSpec((WINDOW, W_A), lambda i: (jnp.maximum(n_blocks * i - 1, 0), 0))],
        out_specs=[pl.BlockSpec((n_rows, 512), lambda i: (i, 0)), pl.BlockSpec((n_rows, LANES), lambda i: (i, 0))],
        compiler_params=_params(("parallel",)),
    )(sinks, a, a)


def _mid(att_a, att_b, g, x, target, gate, g_final, wo_a, wo_b, w_out, tm=256):
    s = x.shape[0]
    nt = s // tm

    def body(aa_ref, ab_ref, g_ref, x_ref, t_ref, gate_ref, gf_ref, woa_ref, wob_ref, wout_ref,
             dx_ref, daa_ref, dab_ref, dg_ref, delta_ref, dwoa_ref, dwob_ref, dwout_ref, vec_ref,
             acc_gf, acc_gate, acc_loss):
        step = pl.program_id(0)

        @pl.when(step == 0)
        def _():
            dwoa_ref[...] = jnp.zeros_like(dwoa_ref)
            dwob_ref[...] = jnp.zeros_like(dwob_ref)
            dwout_ref[...] = jnp.zeros_like(dwout_ref)
            acc_gf[...] = jnp.zeros_like(acc_gf)
            acc_gate[...] = jnp.zeros_like(acc_gate)
            acc_loss[...] = jnp.zeros_like(acc_loss)

        def fold(v):
            return jnp.sum(v.reshape(tm // SUBLANES, SUBLANES, D_MODEL), axis=0)

        gate = gate_ref[...]
        gfin = gf_ref[...]
        branches = []
        for att_ref, z_off, wo_ref in ((aa_ref, 0, woa_ref), (ab_ref, 512, wob_ref)):
            att = att_ref[...].astype(F32)
            z = g_ref[:, z_off:z_off + 512].astype(F32)
            sz = _sigmoid(z)
            silu = z * sz
            u = (att * silu).astype(BF16)
            branches.append((att, z, sz, silu, u, _dot(u, wo_ref[...])))
        ga = g_ref[:, 1024:2048].astype(F32)
        gb = g_ref[:, 2048:3072].astype(F32)
        sga, sgb = _sigmoid(ga), _sigmoid(gb)
        y_a, y_b = branches[0][5], branches[1][5]
        mb = (sga * y_a + sgb * y_b).astype(BF16)
        o = _dot(mb, wout_ref[...])
        x2 = x_ref[...] + gate * o
        r2 = lax.rsqrt(jnp.mean(x2 * x2, axis=-1, keepdims=True) + NORM_EPS)
        xn2 = x2 * r2
        err = xn2 * gfin - t_ref[...]
        acc_loss[...] += fold(err * err)
        dy = err * (1.0 / D_MODEL)
        acc_gf[...] += fold(dy * xn2)
        dxn = dy * gfin
        dx2 = r2 * (dxn - xn2 * jnp.mean(dxn * xn2, axis=-1, keepdims=True))
        dx_ref[...] = dx2
        acc_gate[...] += fold(dx2 * o)
        d_o = (dx2 * gate).astype(BF16)
        dm = _dot(d_o, wout_ref[...], NT)
        dwout_ref[...] += _dot(mb, d_o, TN)
        dg_ref[:, 1024:2048] = (dm * y_a * sga * (1.0 - sga)).astype(BF16)
        dg_ref[:, 2048:3072] = (dm * y_b * sgb * (1.0 - sgb)).astype(BF16)
        dybs = [(dm * sg).astype(BF16) for sg in (sga, sgb)]
        dus = [_dot(dyb, wo_ref[...], NT) for dyb, wo_ref in zip(dybs, (woa_ref, wob_ref))]
        for branch, dyb, dwo_ref in zip(branches, dybs, (dwoa_ref, dwob_ref)):
            dwo_ref[...] += _dot(branch[4], dyb, TN)
        for (att, z, sz, silu, u, _), du, datt_ref, z_off in (
                (branches[0], dus[0], daa_ref, 0), (branches[1], dus[1], dab_ref, 512)):
            datt = du * silu
            datt_ref[...] = datt.astype(BF16)
            dg_ref[:, z_off:z_off + 512] = (du * att * (sz * (1.0 + z * (1.0 - sz)))).astype(BF16)
            if z_off == 512:
                prod = datt * att
                hi = prod.astype(BF16)
                lo = (prod - hi.astype(F32)).astype(BF16)
                er = lax.broadcasted_iota(jnp.int32, (512, LANES), 0)
                ec = lax.broadcasted_iota(jnp.int32, (512, LANES), 1)
                e = (er // HEAD_DIM == ec).astype(BF16)
                delta = _dot(hi, e) + _dot(lo, e)
                delta_ref[...] = delta.T[0:SUBLANES, :]

        @pl.when(step == nt - 1)
        def _():
            sub = lax.broadcasted_iota(jnp.int32, (SUBLANES, D_MODEL), 0)
            dgf = jnp.sum(acc_gf[...], axis=0, keepdims=True)
            dgate = jnp.sum(acc_gate[...], axis=0, keepdims=True)
            loss = 0.5 * jnp.sum(acc_loss[...]) * (1.0 / D_MODEL)
            vec_ref[...] = jnp.where(sub == 0, dgf, jnp.where(sub == 1, dgate, jnp.where(sub == 2, loss, 0.0)))

    row = lambda w: pl.BlockSpec((tm, w), lambda i: (i, 0))
    return pl.pallas_call(
        body, name="mid", grid=(nt,),
        out_shape=[jax.ShapeDtypeStruct((s, D_MODEL), F32), jax.ShapeDtypeStruct((s, 512), BF16),
                   jax.ShapeDtypeStruct((s, 512), BF16), jax.ShapeDtypeStruct((s, W_G), BF16),
                   jax.ShapeDtypeStruct((SUBLANES, s), F32),
                   jax.ShapeDtypeStruct((512, D_MODEL), F32), jax.ShapeDtypeStruct((512, D_MODEL), F32),
                   jax.ShapeDtypeStruct((D_MODEL, D_MODEL), F32), jax.ShapeDtypeStruct((SUBLANES, D_MODEL), F32)],
        in_specs=[row(512), row(512), row(W_G), row(D_MODEL), row(D_MODEL),
                  _const_spec((1, D_MODEL)), _const_spec((1, D_MODEL)),
                  _const_spec((512, D_MODEL)), _const_spec((512, D_MODEL)), _const_spec((D_MODEL, D_MODEL))],
        out_specs=[row(D_MODEL), row(512), row(512), row(W_G),
                   pl.BlockSpec((SUBLANES, tm), lambda i: (0, i)),
                   pl.BlockSpec((512, D_MODEL), lambda i: (0, 0)), pl.BlockSpec((512, D_MODEL), lambda i: (0, 0)),
                   pl.BlockSpec((D_MODEL, D_MODEL), lambda i: (0, 0)), pl.BlockSpec((SUBLANES, D_MODEL), lambda i: (0, 0))],
        scratch_shapes=[pltpu.VMEM((SUBLANES, D_MODEL), F32)] * 3,
        compiler_params=_params(("arbitrary",), VMEM_LIMIT),
    )(att_a, att_b, g, x, target, gate, g_final, wo_a, wo_b, w_out)


def _rope_bwd(dt, cos, sin, lane):
    u = dt * sin
    lo = (lane % HEAD_DIM) < (HEAD_DIM // 2)
    return dt * cos + jnp.where(lo, pltpu.roll(u, 96, 1), -pltpu.roll(u, 32, 1))


def _swa_bwd(a, datt, l_all, sinks, cos, sin):
    s = a.shape[0]
    nt = s // SWA_ROWS

    def body(sink_ref, a_ref, ap_ref, do_ref, l_ref, cos_ref, sin_ref, da_ref, ds_ref, halo):
        step = pl.program_id(0)
        tile = nt - 1 - step

        @pl.when(step == 0)
        def _():
            halo[...] = jnp.zeros_like(halo)
            ds_ref[...] = jnp.zeros_like(ds_ref)

        lane = lax.broadcasted_iota(jnp.int32, (WINDOW, LANES), 1)
        sub8 = lax.broadcasted_iota(jnp.int32, (SUBLANES, LANES), 0)
        lane8 = lax.broadcasted_iota(jnp.int32, (SUBLANES, LANES), 1)
        blocks = _swa_blocks(a_ref, ap_ref, SWA_BLOCKS)
        dsink = jnp.zeros((SUBLANES, LANES), F32)

        def join(pair, r0):
            x0, x1 = pair[0][r0:r0 + WINDOW], pair[1][r0:r0 + WINDOW]
            return jnp.where(lane < 64, x0 + pltpu.roll(x0, 64, 1), x1 + pltpu.roll(x1, 64, 1))

        units = [(jb, g) for jb in range(SWA_BLOCKS) for g in range(2)]
        n_u = len(units)
        sinks_col = [_per_head_column([sink_ref[GROUP * g + hh] for hh in range(GROUP)]) for g in range(2)]
        bands = [_swa_band(blocks[jb + 1], blocks[jb], g, lane) for jb, g in units]
        qs = [_stack_heads(blocks[jb + 1], g, lane) for jb, g in units]
        doms = [_stack_heads(do_ref.at[pl.ds(WINDOW * jb, WINDOW), :], g, lane) for jb, g in units]
        lcols = []
        for jb, g in units:
            lv = l_ref[WINDOW * jb:WINDOW * (jb + 1), :]
            lcols.append(_per_head_column([lv[:, GROUP * g + hh:GROUP * g + hh + 1] for hh in range(GROUP)]))
        ps = [jnp.exp(_swa_logits(qs[u], bands[u][0], (tile > 0) if jb == 0 else True) - lcols[u])
              for u, (jb, g) in enumerate(units)]
        dps = [_dot(doms[u], bands[u][1], NT) for u in range(n_u)]
        deltas = [jnp.sum(ps[u] * dps[u], axis=-1, keepdims=True) for u in range(n_u)]
        for u, (jb, g) in enumerate(units):
            sink_term = jnp.exp(sinks_col[g] - lcols[u]) * deltas[u]
            for hh in range(GROUP):
                tot = jnp.sum(sink_term[WINDOW * hh:WINDOW * (hh + 1)])
                dsink = dsink + jnp.where((sub8 == 0) & (lane8 == GROUP * g + hh), -tot, 0.0)
        dss = [(ps[u] * (dps[u] - deltas[u])).astype(BF16) for u in range(n_u)]
        dqs = [_dot(dss[u], bands[u][0]) * SCALE for u in range(n_u)]
        dks = [_dot(dss[u], qs[u], TN) * SCALE for u in range(n_u)]
        dvs = [_dot(ps[u].astype(BF16), doms[u], TN) for u in range(n_u)]

        carry_k, carry_v = halo[:, 0:LANES], halo[:, LANES:2 * LANES]
        for jb in reversed(range(SWA_BLOCKS)):
            rows = slice(WINDOW * jb, WINDOW * (jb + 1))
            cosv, sinv = cos_ref[rows, :], sin_ref[rows, :]
            for g in range(2):
                dq = dqs[2 * jb + g]
                for pb in range(2):
                    r0 = 2 * pb * WINDOW
                    dq_pair = jnp.where(lane < 64, dq[r0:r0 + WINDOW], dq[r0 + WINDOW:r0 + 2 * WINDOW])
                    da_ref[rows, LANES * (2 * g + pb):LANES * (2 * g + pb + 1)] = _rope_bwd(
                        dq_pair, cosv, sinv, lane).astype(BF16)
            dkb, dvb = dks[2 * jb:2 * jb + 2], dvs[2 * jb:2 * jb + 2]
            da_ref[rows, 512:640] = _rope_bwd(join(dkb, WINDOW) + carry_k, cosv, sinv, lane).astype(BF16)
            da_ref[rows, 640:768] = (join(dvb, WINDOW) + carry_v).astype(BF16)
            carry_k, carry_v = join(dkb, 0), join(dvb, 0)
        halo[:, 0:LANES] = carry_k
        halo[:, LANES:2 * LANES] = carry_v
        ds_ref[...] += dsink

    rev = lambda w: pl.BlockSpec((SWA_ROWS, w), lambda i: (nt - 1 - i, 0))
    return pl.pallas_call(
        body, name="swa_bwd", grid=(nt,),
        out_shape=[jax.ShapeDtypeStruct((s, W_A), BF16), jax.ShapeDtypeStruct((SUBLANES, LANES), F32)],
        in_specs=[pl.BlockSpec(memory_space=pltpu.SMEM), rev(W_A),
                  pl.BlockSpec((WINDOW, W_A), lambda i: (jnp.maximum(SWA_BLOCKS * (nt - 1 - i) - 1, 0), 0)),
                  rev(512), rev(LANES), rev(LANES), rev(LANES)],
        out_specs=[rev(W_A), pl.BlockSpec((SUBLANES, LANES), lambda i: (0, 0))],
        scratch_shapes=[pltpu.VMEM((WINDOW, 2 * LANES), F32)],
        compiler_params=_params(("arbitrary",)),
    )(sinks, a, a, datt, l_all, cos, sin)


def _fox_bwd(qa, ka, vb, do, lse, delta, ranges, t):
    s = qa.shape[0]
    nt = s // t

    def body(rg_ref, q_ref, do_ref, lse_ref, dl_ref, k_ref, v_ref, dq_ref, dk_ref, dv_ref, dc_ref, dr_ref, dq_acc):
        p = pl.program_id(0)
        j = pl.program_id(1)
        n_queries = [jnp.clip(_lane_scalar(rg_ref[0], 2 + hh, j), 1, nt - j) for hh in range(2)]

        @pl.when(j == 0)
        def _():
            dq_acc[...] = jnp.zeros_like(dq_acc)

        lane = lax.broadcasted_iota(jnp.int32, (t, LANES), 1)
        rows = lax.broadcasted_iota(jnp.int32, (t, t), 0)
        cols = lax.broadcasted_iota(jnp.int32, (t, t), 1)
        kt = k_ref[...]
        vt = v_ref[...]

        ks = [kt[:, LANES * hh:LANES * (hh + 1)] for hh in range(2)]

        def tile(qis, carry, heads=(0, 1), diagonal=False):
            dk0, dk1, dv = carry
            offs = [pl.multiple_of(i * t, t) for i in qis]
            units = [(u, hh) for u in range(len(qis)) for hh in heads]
            qts = [q_ref[pl.ds(off, t), :] for off in offs]
            dos = [do_ref[pl.ds(off, t), :] for off in offs]
            lses = [lse_ref[0, :, pl.ds(off, t)] for off in offs]
            dls = [dl_ref[0, :, pl.ds(off, t)] for off in offs]
            qs = [qts[u][:, LANES * hh:LANES * (hh + 1)] for u, hh in units]
            doms = [jnp.where((lane < 64) if hh == 0 else (lane >= 64), dos[u], jnp.zeros_like(dos[u])) for u, hh in units]
            sts = [_dot(ks[hh], qs[n], NT) for n, (u, hh) in enumerate(units)]
            dpts = [_dot(vt, doms[n], NT) for n in range(len(units))]
            if diagonal:
                sts = [jnp.where(cols >= rows, st, NEG) for st in sts]
            pts = [jnp.exp(sts[n] - lses[u][hh:hh + 1, :]) for n, (u, hh) in enumerate(units)]
            dsts = [(pts[n] * (dpts[n] - dls[u][hh:hh + 1, :])).astype(BF16) for n, (u, hh) in enumerate(units)]
            for n, (u, hh) in enumerate(units):
                dv = dv + _dot(pts[n].astype(BF16), doms[n])
                term = _dot(dsts[n], qs[n])
                dk0, dk1 = (dk0 + term, dk1) if hh == 0 else (dk0, dk1 + term)
                dq_acc[hh, pl.ds(offs[u], t), :] += _dot(dsts[n], ks[hh], TN)
            return dk0, dk1, dv

        zero = jnp.zeros((t, LANES), F32)
        carry = tile([j], (zero, zero, zero), diagonal=True)
        n_rest = jnp.minimum(n_queries[0], n_queries[1]) - 1
        carry = lax.fori_loop(0, n_rest // 2, lambda u, cr: tile([j + 1 + 2 * u, j + 2 + 2 * u], cr), carry)
        carry = lax.fori_loop(0, n_rest % 2, lambda u, cr: tile([j + n_rest], cr), carry)
        for hh in range(2):
            carry = lax.fori_loop(j + 1 + n_rest, j + n_queries[hh], lambda i, cr, hh=hh: tile([i], cr, heads=(hh,)), carry)
        dk0, dk1, dv = carry
        e0, e1 = _aug_lane(0), _aug_lane(1)
        dk_ref[...] = jnp.where(lane < 64, dk0, dk1).astype(BF16)
        dv_ref[...] = dv.astype(BF16)
        c0 = jnp.broadcast_to(dk0[:, e0 + 3:e0 + 4], (t, LANES))
        c1 = jnp.broadcast_to(dk1[:, e1 + 3:e1 + 4], (t, LANES))
        dc_ref[0] = jnp.where(lane == 2 * p, -c0, jnp.where(lane == 2 * p + 1, -c1, 0.0))

        @pl.when(j == nt - 1)
        def _():
            lane_s = lax.broadcasted_iota(jnp.int32, (s, LANES), 1)
            a0, a1 = dq_acc[0], dq_acc[1]
            dq_ref[...] = (jnp.where(lane_s < 64, a0, a1) * SCALE).astype(BF16)
            r0 = jnp.broadcast_to(a0[:, e0:e0 + 1], (s, LANES))
            r1 = jnp.broadcast_to(a1[:, e1:e1 + 1], (s, LANES))
            dr_ref[0] = jnp.where(lane_s == 2 * p, r0, jnp.where(lane_s == 2 * p + 1, r1, 0.0))

    return pl.pallas_call(
        body, name="fox_bwd", grid=(4, nt),
        out_shape=[jax.ShapeDtypeStruct((s, 512), BF16), jax.ShapeDtypeStruct((s, 512), BF16),
                   jax.ShapeDtypeStruct((s, 512), BF16), jax.ShapeDtypeStruct((4, s, LANES), F32),
                   jax.ShapeDtypeStruct((4, s, LANES), F32)],
        in_specs=[pl.BlockSpec((1, SUBLANES, LANES), lambda p, j: (p, 0, 0)),
                  pl.BlockSpec((s, 2 * LANES), lambda p, j: (0, p)),
                  pl.BlockSpec((s, LANES), lambda p, j: (0, p)),
                  pl.BlockSpec((1, SUBLANES, s), lambda p, j: (p, 0, 0)),
                  pl.BlockSpec((1, SUBLANES, s), lambda p, j: (p, 0, 0)),
                  pl.BlockSpec((t, 2 * LANES), lambda p, j: (j, p)),
                  pl.BlockSpec((t, LANES), lambda p, j: (j, p))],
        out_specs=[pl.BlockSpec((s, LANES), lambda p, j: (0, p)),
                   pl.BlockSpec((t, LANES), lambda p, j: (j, p)),
                   pl.BlockSpec((t, LANES), lambda p, j: (j, p)),
                   pl.BlockSpec((1, t, LANES), lambda p, j: (p, j, 0)),
                   pl.BlockSpec((1, s, LANES), lambda p, j: (p, 0, 0))],
        scratch_shapes=[pltpu.VMEM((2, s, LANES), F32)],
        compiler_params=_params(("parallel", "arbitrary"), VMEM_LIMIT),
    )(ranges, qa, do, lse, delta, ka, vb)


def _forget_logit_grad(dc_ref, dr_ref, f, b_f, carry):
    tb = f.shape[0]
    lane = lax.broadcasted_iota(jnp.int32, (tb, LANES), 1)
    dc = dc_ref[0] + dr_ref[0]
    for k in range(1, 4):
        dc = dc + (dc_ref[k] + dr_ref[k])
    hi, mid, lo = _split3(dc)
    rows = lax.broadcasted_iota(jnp.int32, (tb, tb), 0)
    cols = lax.broadcasted_iota(jnp.int32, (tb, tb), 1)
    triu = (cols >= rows).astype(BF16)
    dlogf = _dot(triu, hi) + _dot(triu, mid) + _dot(triu, lo) + carry[0:1, :]
    carry[...] = jnp.broadcast_to(dlogf[0:1, :], carry.shape)
    return jnp.where(lane < N_HEADS, dlogf * _sigmoid(-(f + b_f)), 0.0)


def _dh_norm_bwd(d_a, d_q, d_k, d_v, dcum_k, dcum_q, f, bf_pad, d_g, w_t, x, dx2, gnorm, scale1, tm=512):
    s = x.shape[0]
    nt = s // tm

    def body(da_ref, dq_ref, dk_ref, dv_ref, dc_ref, dr_ref, f_ref, bf_ref, dg_ref, w_ref, x_ref, dx2_ref, g_ref, sc_ref,
             gx_ref, vec_ref, df_ref, db_ref, a_sh, a_sc, a_g, carry):
        step = pl.program_id(0)

        @pl.when(step == 0)
        def _():
            a_sh[...] = jnp.zeros_like(a_sh)
            a_sc[...] = jnp.zeros_like(a_sc)
            a_g[...] = jnp.zeros_like(a_g)
            carry[...] = jnp.zeros_like(carry)
            db_ref[...] = jnp.zeros_like(db_ref)

        def fold(v):
            return jnp.sum(v.reshape(tm // SUBLANES, SUBLANES, D_MODEL), axis=0)

        dfb = _forget_logit_grad(dc_ref, dr_ref, f_ref[...], bf_ref[...], carry)
        d_f = dfb.astype(BF16)
        df_ref[...] = d_f
        sub8 = lax.broadcasted_iota(jnp.int32, (SUBLANES, LANES), 0)
        db_ref[...] += jnp.where(sub8 == 0, jnp.sum(dfb, axis=0, keepdims=True), 0.0)
        d_all = jnp.concatenate([da_ref[...], dq_ref[...], dk_ref[...], dv_ref[...], d_f, dg_ref[...]], axis=1)
        dh = _dot(d_all, w_ref[...])
        xv = x_ref[...]
        r = lax.rsqrt(jnp.mean(xv * xv, axis=-1, keepdims=True) + NORM_EPS)
        xn = xv * r
        gn = g_ref[...]
        a_sh[...] += fold(dh)
        a_sc[...] += fold(dh * (xn * gn))
        dn1 = dh * sc_ref[...]
        a_g[...] += fold(dn1 * xn)
        dxn = dn1 * gn
        gx_ref[...] = dx2_ref[...] + r * (dxn - xn * jnp.mean(dxn * xn, axis=-1, keepdims=True))

        @pl.when(step == nt - 1)
        def _():
            sub = lax.broadcasted_iota(jnp.int32, (SUBLANES, D_MODEL), 0)
            v_sh = jnp.sum(a_sh[...], axis=0, keepdims=True)
            v_sc = jnp.sum(a_sc[...], axis=0, keepdims=True)
            v_g = jnp.sum(a_g[...], axis=0, keepdims=True)
            vec_ref[...] = jnp.where(sub == 0, v_sh, jnp.where(sub == 1, v_sc, jnp.where(sub == 2, v_g, 0.0)))

    row = lambda w: pl.BlockSpec((tm, w), lambda i: (nt - 1 - i, 0))
    slabs = pl.BlockSpec((4, tm, LANES), lambda i: (0, nt - 1 - i, 0))
    return pl.pallas_call(
        body, name="dh_norm_bwd", grid=(nt,),
        out_shape=[jax.ShapeDtypeStruct((s, D_MODEL), F32), jax.ShapeDtypeStruct((SUBLANES, D_MODEL), F32),
                   jax.ShapeDtypeStruct((s, LANES), BF16), jax.ShapeDtypeStruct((SUBLANES, LANES), F32)],
        in_specs=[row(W_A), row(512), row(512), row(512), slabs, slabs, row(W_F), _const_spec((1, LANES)), row(W_G),
                  _const_spec((W_INT, D_MODEL)), row(D_MODEL), row(D_MODEL), _const_spec((1, D_MODEL)),
                  _const_spec((1, D_MODEL))],
        out_specs=[row(D_MODEL), pl.BlockSpec((SUBLANES, D_MODEL), lambda i: (0, 0)), row(LANES),
                   pl.BlockSpec((SUBLANES, LANES), lambda i: (0, 0))],
        scratch_shapes=[pltpu.VMEM((SUBLANES, D_MODEL), F32)] * 3 + [pltpu.VMEM((SUBLANES, LANES), F32)],
        compiler_params=_params(("arbitrary",), VMEM_LIMIT),
    )(d_a, d_q, d_k, d_v, dcum_k, dcum_q, f, bf_pad, d_g, w_t, x, dx2, gnorm, scale1)


def _dw_in(h_t, d_a, d_q, d_k, d_v, d_f, d_g, ts=1024, tc=512):
    s = h_t.shape[1]
    ns = s // ts
    w_fg = 512 + W_F + W_G - 512
    rows = 128
    n_slot = (R_END // 4 + LANES - 1) // LANES * LANES
    order = [(0, 0, W_A), (4, 0, 512), (1, 0, 512), (2, 0, 512), (3, 0, 512), (4, 512, N_HEADS), (4, 512 + W_F, W_G - 512)]

    def slot_pieces(k):
        lo, hi, out, col = (R_END // 4) * k, (R_END // 4) * (k + 1), [], 0
        for acc_i, c0, w in order:
            a, b = max(lo, col), min(hi, col + w)
            if a < b:
                out.append((acc_i, c0 + a - col, b - a))
            col += w
        return out

    def body(h_ref, da_ref, dq_ref, dk_ref, dv_ref, df_ref, dg_ref, o_ref, acc_a, acc_q, acc_k, acc_v, acc_fg, stage, sem):
        k = pl.program_id(0)
        accs = (acc_a, acc_q, acc_k, acc_v, acc_fg)

        @pl.when(k == 0)
        def _():
            for acc in accs:
                acc[...] = jnp.zeros_like(acc)

        hv = h_ref[...]

        def add(acc, c_acc, d_ref, c_d, width):
            for c0 in range(0, width, tc):
                w = min(tc, width - c0)
                acc[:, c_acc + c0:c_acc + c0 + w] += _dot(hv, d_ref[:, c_d + c0:c_d + c0 + w])

        add(acc_a, 0, da_ref, 0, W_A)
        add(acc_q, 0, dq_ref, 0, 512)
        add(acc_k, 0, dk_ref, 0, 512)
        add(acc_v, 0, dv_ref, 0, 512)
        add(acc_fg, 0, dg_ref, 0, 512)
        add(acc_fg, 512, df_ref, 0, W_F)
        add(acc_fg, 512 + W_F, dg_ref, 512, W_G - 512)

        @pl.when(k == ns - 1)
        def _():
            pending = [None, None]
            for n, r0 in enumerate(range(0, D_MODEL, rows)):
                buf = n % 2
                if pending[buf] is not None:
                    for cp in pending[buf]:
                        cp.wait()
                rs = slice(r0, r0 + rows)
                copies = []
                for slot in range(4):
                    parts = [accs[acc_i][rs, c0:c0 + w] for acc_i, c0, w in slot_pieces(slot)]
                    parts.append(jnp.zeros((rows, n_slot - R_END // 4), F32))
                    stage[buf, slot] = jnp.concatenate(parts, axis=1)
                    cp = pltpu.make_async_copy(stage.at[buf, slot], o_ref.at[slot, pl.ds(r0, rows), :], sem.at[4 * buf + slot])
                    cp.start()
                    copies.append(cp)
                pending[buf] = copies
            for copies in pending:
                for cp in copies:
                    cp.wait()

    spec = lambda d: pl.BlockSpec((ts, d.shape[1]), lambda k: (k, 0))
    return pl.pallas_call(
        body, name="dw_in", grid=(ns,),
        out_shape=jax.ShapeDtypeStruct((4, D_MODEL, n_slot), F32),
        in_specs=[pl.BlockSpec((D_MODEL, ts), lambda k: (0, k))] + [spec(d) for d in (d_a, d_q, d_k, d_v, d_f, d_g)],
        out_specs=pl.BlockSpec(memory_space=pl.ANY),
        scratch_shapes=[pltpu.VMEM((D_MODEL, W_A), F32)] + [pltpu.VMEM((D_MODEL, 512), F32)] * 3
        + [pltpu.VMEM((D_MODEL, w_fg), F32), pltpu.VMEM((2, 4, rows, n_slot), F32), pltpu.SemaphoreType.DMA((8,))],
        compiler_params=_params(("arbitrary",), VMEM_LIMIT),
    )(h_t, d_a, d_q, d_k, d_v, d_f, d_g)


def _small_grads(packs, c_t, dada_shard):
    def body(p_ref, ct_ref, da_ref, sum_ref, gw_ref):
        acc = p_ref[0]
        for dev in range(1, 8):
            acc = acc + p_ref[dev]
        sum_ref[...] = acc
        gw_ref[...] = jnp.dot(ct_ref[...], da_ref[...], preferred_element_type=F32, precision=lax.Precision.HIGHEST)

    return pl.pallas_call(
        body, name="small_grads",
        out_shape=[jax.ShapeDtypeStruct(packs.shape[1:], F32),
                   jax.ShapeDtypeStruct((c_t.shape[0], dada_shard.shape[1]), F32)],
    )(packs, c_t, dada_shard)


def _adamw_body(w_ref, g_ref, m_ref, v_ref, d_ref, mo_ref, vo_ref):
    c1 = 1.0 / (1.0 - ADAM_B1 ** ADAM_STEP)
    c2 = 1.0 / (1.0 - ADAM_B2 ** ADAM_STEP)
    gv = g_ref[...]
    mn = ADAM_B1 * m_ref[...] + (1.0 - ADAM_B1) * gv
    vn = ADAM_B2 * v_ref[...] + (1.0 - ADAM_B2) * (gv * gv)
    mo_ref[...] = mn
    vo_ref[...] = vn
    d_ref[...] = -ADAM_LR * ((mn * c1) / (jnp.sqrt(vn * c2) + ADAM_EPS) + ADAM_WD * w_ref[...])


def _adamw3(w, g, m, v, name, tb=128):
    spec = pl.BlockSpec((tb, SUBLANES, LANES), lambda i: (i, 0, 0))
    return pl.pallas_call(
        functools.partial(_adamw_body), name=name, grid=(pl.cdiv(w.shape[0], tb),),
        out_shape=[jax.ShapeDtypeStruct(w.shape, F32)] * 3,
        in_specs=[spec] * 4, out_specs=[spec] * 3,
        compiler_params=_params(("parallel",)),
    )(w, g, m, v)


def _adamw_many(items, name):
    n = len(items)

    def body(*refs):
        for i in range(n):
            _adamw_body(*refs[4 * i:4 * i + 4], *refs[4 * n + 3 * i:4 * n + 3 * i + 3])

    return pl.pallas_call(
        body, name=name,
        out_shape=[jax.ShapeDtypeStruct(it[0].shape, F32) for it in items for _ in range(3)],
        compiler_params=_params(vmem=VMEM_LIMIT),
    )(*[arr for it in items for arr in it])


def _rope_inputs(positions):
    inv_freq = 10000.0 ** (-jnp.arange(0, HEAD_DIM, 2, dtype=F32) / HEAD_DIM)
    pos = jnp.broadcast_to(positions.astype(F32)[:, None], (positions.shape[0], LANES))
    return pos, jnp.tile(inv_freq, 4)[None, :]


def _pad_rows(v, rows=SUBLANES):
    return jnp.pad(v, ((0, rows - v.shape[0]), (0, 0)))


def kernel(x, c, positions, w_ada, b_ada, g_norm, w_in, b_f, sinks, w_o_swa, w_o_fox, w_out, g_final, loss_target, m_w_ada, m_b_ada, m_g_norm, m_w_in, m_b_f, m_sinks, m_w_o_swa, m_w_o_fox, m_w_out, m_g_final, v_w_ada, v_b_ada, v_g_norm, v_w_in, v_b_f, v_sinks, v_w_o_swa, v_w_o_fox, v_w_out, v_g_final):
    ix, iy, ic = lax.axis_index("x"), lax.axis_index("y"), lax.axis_index("c")
    chip = 2 * ix + iy
    dev = 2 * chip + ic
    xs, tgt = x[0], loss_target[0]
    s = xs.shape[0]

    b_ada_shard = lax.dynamic_slice(b_ada, (0, chip * 768), (1, 768))
    ada_parts, w_int, w_int_t, g_oa, g_ob, g_out = _gather_inputs(
        _pad_rows(c), w_ada[0], b_ada_shard, jnp.transpose(w_in, (2, 0, 1)).reshape(-1, LANES),
        [w_o_swa[0], w_o_fox[0], w_out[0]], "gather_inputs")
    ada = lax.dynamic_index_in_dim(ada_parts, dev, axis=1, keepdims=False).reshape(1, 3 * D_MODEL)
    shift, scale, gate = ada[:, :D_MODEL], ada[:, D_MODEL:2 * D_MODEL], ada[:, 2 * D_MODEL:]
    scale1 = 1.0 + scale

    wo_a = jnp.transpose(g_oa, (1, 0, 2)).reshape(512, D_MODEL)
    wo_b = jnp.transpose(g_ob, (1, 0, 2)).reshape(512, D_MODEL)
    w_o = g_out.reshape(D_MODEL, D_MODEL)

    pos, freq = _rope_inputs(positions[0])
    bf_pad = jnp.pad(b_f, ((0, 0), (0, LANES - N_HEADS)))
    sink_vec = sinks[0]

    a, vb, f, g, h_t, cos, sin, qa, ka, va, stats = _norm_proj(
        xs, g_norm * scale1, shift, w_int, pos, freq, bf_pad, FOX_TILE)
    att_a, l_swa = _swa_fwd(a, sink_vec)
    ranges = _fox_tile_ranges(stats)
    att_b, lse = _fox_fwd(qa, ka, va, ranges, FOX_TILE)

    dx2, datt_a, datt_b, d_g, delta8, dwo_a, dwo_b, dw_out, vec_mid = _mid(
        att_a, att_b, g, xs, tgt, gate, g_final.reshape(1, D_MODEL), wo_a, wo_b, w_o)
    delta = jnp.pad(delta8.reshape(4, 2, s), ((0, 0), (0, SUBLANES - 2), (0, 0)))
    d_a, dsink = _swa_bwd(a, datt_a, l_swa, sink_vec, cos, sin)
    dq, dk, dv, dcum_k, dcum_q = _fox_bwd(qa, ka, vb, datt_b, lse, delta, ranges, FOX_TILE)
    grad_x, vec_dh, d_f, dbf = _dh_norm_bwd(
        d_a, dq, dk, dv, dcum_k, dcum_q, f, bf_pad, d_g, w_int_t, xs, dx2, g_norm, scale1)
    dw_in_slots = _dw_in(h_t, d_a, dq, dk, dv, d_f, d_g)

    tail = jnp.pad(jnp.concatenate([dbf[0:1, :N_HEADS], dsink[0:1, :N_HEADS]], axis=1), ((0, 0), (0, D_MODEL - 2 * N_HEADS)))
    pack = jnp.concatenate([c, vec_dh[0:2], vec_mid[1:2], vec_dh[2:3], vec_mid[0:1], tail, vec_mid[2:3]], axis=0)

    def slots(w, axis):
        if axis == 1:
            return jnp.transpose(w.reshape(w.shape[0], 4, w.shape[1] // 4), (1, 0, 2))
        return w.reshape(4, w.shape[0] // 4, w.shape[1])

    packs, g_wo_a, g_wo_b, g_w_out, g_w_in = _reduce_scatter(
        [slots(dwo_a, 1), slots(dwo_b, 1), slots(dw_out, 0), dw_in_slots], pack, "reduce_grads")
    g_w_in = g_w_in[:, :w_in.shape[2]]
    dada_all = packs[:, 1:4, :].reshape(8, 3 * D_MODEL)
    dada_shard = lax.dynamic_slice(dada_all, (0, chip * 768), (8, 768))
    sums, g_w_ada = _small_grads(packs, packs[:, 0, :].T, dada_shard)
    g_b_ada = sums[1:4].reshape(1, 3 * D_MODEL)
    g_g_norm = sums[4:5]
    g_g_final = sums[5]
    g_b_f = sums[6:7, :N_HEADS]
    g_sinks = sums[6:7, N_HEADS:2 * N_HEADS]
    loss = sums[7, 0]

    grads = {
        "w_ada": g_w_ada, "b_ada": g_b_ada, "g_norm": g_g_norm, "w_in": g_w_in, "b_f": g_b_f, "sinks": g_sinks,
        "w_o_swa": g_wo_a, "w_o_fox": g_wo_b, "w_out": g_w_out, "g_final": g_g_final,
    }
    params = {
        "w_ada": (w_ada, m_w_ada, v_w_ada), "b_ada": (b_ada, m_b_ada, v_b_ada), "g_norm": (g_norm, m_g_norm, v_g_norm),
        "w_in": (w_in, m_w_in, v_w_in), "b_f": (b_f, m_b_f, v_b_f), "sinks": (sinks, m_sinks, v_sinks),
        "w_o_swa": (w_o_swa, m_w_o_swa, v_w_o_swa), "w_o_fox": (w_o_fox, m_w_o_fox, v_w_o_fox),
        "w_out": (w_out, m_w_out, v_w_out), "g_final": (g_final, m_g_final, v_g_final),
    }
    n_col = w_in.shape[2]

    def as_stored(t):
        return jnp.transpose(t, (2, 0, 1)).reshape(n_col, SUBLANES, LANES)

    def from_stored(t):
        return jnp.transpose(t, (1, 2, 0)).reshape(1, D_MODEL, n_col)

    names = list(grads)
    others = [nm for nm in names if nm != "w_in"]

    def as_2d(t):
        return t.reshape((t.shape[-2], t.shape[-1]) if t.ndim >= 2 else (1, t.shape[0]))

    flat = _adamw_many([tuple(as_2d(t) for t in (params[nm][0], grads[nm], params[nm][1], params[nm][2])) for nm in others],
                       "adamw_small")
    results = {}
    for i, nm in enumerate(others):
        shape = params[nm][0].shape
        results[nm] = [t.reshape(shape) for t in (grads[nm], *flat[3 * i:3 * i + 3])]
    w, m, v = params["w_in"]
    g_st = as_stored(grads["w_in"][None])
    d_, m_, v_ = _adamw3(as_stored(w), g_st, as_stored(m), as_stored(v), "adamw_w_in")
    results["w_in"] = [from_stored(t) for t in (g_st, d_, m_, v_)]
    return (loss, grad_x[None], *[results[nm][0] for nm in names], *[results[nm][1] for nm in names],
            *[results[nm][2] for nm in names], *[results[nm][3] for nm in names])
```

```python
import functools

import numpy as np
import jax
import jax.numpy as jnp
from jax import lax
from jax.experimental import pallas as pl
from jax.experimental.pallas import tpu as pltpu

F32 = jnp.float32
BF16 = jnp.bfloat16
MESH = pl.DeviceIdType.MESH

D_MODEL = 1024
HEAD_DIM = 64
N_HEADS = 8
WINDOW = 128
NORM_EPS = 1e-6
SCALE = HEAD_DIM ** -0.5
NEG = -1e30
LANES = 128
SUBLANES = 8
VMEM_LIMIT = 60 * 1024 * 1024
FOX_TILE = 512

W_A, W_B, W_F, W_G = 768, 1536, 128, 3072
OFF_A, OFF_B, OFF_F, OFF_G = 0, 768, 2304, 2432
W_INT = W_A + W_B + W_F + W_G
R_ZA, R_QB, R_FB, R_ZB, R_END = 768, 1280, 2816, 2824, 5384

ADAM_LR, ADAM_B1, ADAM_B2, ADAM_EPS, ADAM_WD, ADAM_STEP = 0.001, 0.9, 0.999, 1e-08, 0.01, 10

NT = (((1,), (1,)), ((), ()))
TN = (((0,), (0,)), ((), ()))


def _dot(a, b, dims=None):
    if dims is None:
        return jnp.dot(a, b, preferred_element_type=F32)
    return lax.dot_general(a, b, dims, preferred_element_type=F32)


def _split3(v):
    hi = v.astype(BF16)
    r1 = v - hi.astype(F32)
    mid = r1.astype(BF16)
    lo = (r1 - mid.astype(F32)).astype(BF16)
    return hi, mid, lo


def _sigmoid(v):
    return 1.0 / (1.0 + jnp.exp(-v))


def _params(sem=None, vmem=None):
    return pltpu.CompilerParams(dimension_semantics=sem, vmem_limit_bytes=vmem)


def _const_spec(shape):
    nd = len(shape)
    return pl.BlockSpec(shape, lambda *_: (0,) * nd, pipeline_mode=pl.Buffered(1))


def _flip(v, f):
    return 1 - v if f else v


_CHIP_FLIPS = ((1, 0), (0, 1), (1, 1))


def _gather_inputs(c_pad, w_ada, b_ada_shard, w_in_shard, small_shards, name):
    shards = [w_in_shard] + list(small_shards)
    n = len(shards)
    n_col = w_ada.shape[1]
    shard_w = w_in_shard.shape[0] // SUBLANES
    rows = 128

    def body(*refs):
        c_ref, wa_ref, ba_ref = refs[:3]
        ins = refs[3:3 + n]
        ada_ref, wint_ref, wintt_ref = refs[3 + n:6 + n]
        g_in, call_ref, send_sems, recv_sems = refs[5 + 2 * n:9 + 2 * n]
        outs = (g_in,) + tuple(refs[6 + n:5 + 2 * n])
        x, y, c = lax.axis_index("x"), lax.axis_index("y"), lax.axis_index("c")
        k_me = 2 * x + y
        me = 2 * k_me + c
        sibling = (x, y, 1 - c)
        chips = [(_flip(x, fx), _flip(y, fy)) for fx, fy in _CHIP_FLIPS]

        def piece(i, chip_k, half):
            hr = outs[i].shape[1] // 2
            return outs[i].at[chip_k, pl.ds(half * hr, hr), :]

        def copy(i, slot, chip_k, half, to):
            return pltpu.make_async_remote_copy(
                src_ref=piece(i, chip_k, half), dst_ref=piece(i, chip_k, half),
                send_sem=send_sems.at[6 * i + slot], recv_sem=recv_sems.at[6 * i + slot],
                device_id=to, device_id_type=MESH)

        def small(ref, slot, sem, to):
            return pltpu.make_async_remote_copy(
                src_ref=ref.at[slot], dst_ref=ref.at[slot], send_sem=send_sems.at[6 * n + sem],
                recv_sem=recv_sems.at[6 * n + sem], device_id=to, device_id_type=MESH)

        whole = shard_w // LANES * LANES
        for a in range(SUBLANES):
            main = ins[0][pl.ds(a, whole, stride=SUBLANES), :]
            tail = ins[0][pl.ds(a + SUBLANES * whole, shard_w - whole, stride=SUBLANES), :]
            tail = jnp.concatenate([tail, jnp.zeros((LANES - (shard_w - whole), LANES), F32)], axis=0)
            blk = jnp.concatenate([main.T, tail.T[:, :shard_w - whole]], axis=1)
            g_in[k_me, LANES * a:LANES * (a + 1), :] = blk.astype(BF16)
        for i in range(1, n):
            outs[i][k_me] = ins[i][...].astype(BF16)
        started = []
        for i in range(n):
            for j, chip in enumerate(chips):
                cp = copy(i, j, k_me, c, (chip[0], chip[1], c))
                cp.start()
                started.append(cp)

        call_ref[me] = c_ref[...]
        peers = [(_flip(x, k & 4), _flip(y, k & 2), _flip(c, k & 1)) for k in range(1, 8)]
        for k, peer in enumerate(peers):
            cp = small(call_ref, me, k, peer)
            cp.start()
            started.append(cp)
        for k, peer in enumerate(peers):
            small(call_ref, 4 * peer[0] + 2 * peer[1] + peer[2], k, peer).wait_recv()
        c_all = call_ref[:, 0, :].astype(BF16)
        ada_ref[k_me] = _dot(c_all, wa_ref[...].astype(BF16)) + ba_ref[...]
        for j, chip in enumerate(chips):
            cp = small(ada_ref, k_me, 7 + j, (chip[0], chip[1], c))
            cp.start()
            started.append(cp)

        for j, chip in enumerate(chips):
            chip_k = 2 * chip[0] + chip[1]
            for i in range(n):
                copy(i, j, chip_k, c, (chip[0], chip[1], c)).wait_recv()
                cp = copy(i, 3 + j, chip_k, c, sibling)
                cp.start()
                started.append(cp)
        for j, chip in enumerate(chips):
            chip_k = 2 * chip[0] + chip[1]
            small(ada_ref, chip_k, 7 + j, (chip[0], chip[1], c)).wait_recv()
            for i in range(n):
                copy(i, 3 + j, chip_k, 1 - c, sibling).wait_recv()
        for cp in started:
            cp.wait_send()

        def ref_cols(slots, a, b):
            runs = []
            for k in range(4):
                lo, hi = max(a, shard_w * k), min(b, shard_w * (k + 1))
                if lo < hi:
                    runs.append(slots[k][:, lo - shard_w * k:hi - shard_w * k])
            return runs

        for r0 in range(0, D_MODEL, rows):
            rs = slice(r0, r0 + rows)
            slots = [g_in[k, rs, :] for k in range(4)]
            row = jnp.concatenate(
                ref_cols(slots, 0, R_ZA) + ref_cols(slots, R_QB, R_FB) + ref_cols(slots, R_FB, R_ZB)
                + [jnp.zeros((rows, W_F - N_HEADS), BF16)] + ref_cols(slots, R_ZA, R_QB) + ref_cols(slots, R_ZB, R_END),
                axis=1)
            wint_ref[rs, :] = row
            wintt_ref[:, rs] = row.T

    vmem = pl.BlockSpec(memory_space=pltpu.VMEM)
    return pl.pallas_call(
        body, name=name,
        out_shape=[jax.ShapeDtypeStruct((4, 8, n_col), F32), jax.ShapeDtypeStruct((D_MODEL, W_INT), BF16),
                   jax.ShapeDtypeStruct((W_INT, D_MODEL), BF16)]
        + [jax.ShapeDtypeStruct((4,) + s.shape, BF16) for s in small_shards],
        in_specs=[vmem] * (3 + n),
        out_specs=[vmem] * (2 + n),
        scratch_shapes=[pltpu.VMEM((4, D_MODEL, shard_w), BF16), pltpu.VMEM((8,) + c_pad.shape, F32),
                        pltpu.SemaphoreType.DMA((6 * n + 10,)), pltpu.SemaphoreType.DMA((6 * n + 10,))],
        compiler_params=_params(vmem=VMEM_LIMIT),
    )(c_pad, w_ada, b_ada_shard, *shards)


def _reduce_scatter(pieces, pack, name):
    n = len(pieces)

    def body(*refs):
        pack_ref, ins = refs[0], refs[1:1 + n]
        packs_ref, outs = refs[1 + n], refs[2 + n:2 + 2 * n]
        rest = refs[2 + 2 * n:]
        own, got = rest[:n], rest[n:2 * n]
        sendb, recvb = rest[2 * n:3 * n], rest[3 * n:4 * n]
        send_sems, recv_sems, local_sems = rest[4 * n:4 * n + 3]
        x, y, c = lax.axis_index("x"), lax.axis_index("y"), lax.axis_index("c")
        k_me = 2 * x + y
        me = 2 * k_me + c
        sibling = (x, y, 1 - c)
        chips = [(_flip(x, fx), _flip(y, fy)) for fx, fy in _CHIP_FLIPS]
        hrs = [p.shape[1] // 2 for p in pieces]

        def remote(i, slot, src, dst, to):
            return pltpu.make_async_remote_copy(
                src_ref=src, dst_ref=dst, send_sem=send_sems.at[5 * i + slot], recv_sem=recv_sems.at[5 * i + slot],
                device_id=to, device_id_type=MESH)

        started = []
        packs_ref[me] = pack_ref[...]
        peers = [(_flip(x, k & 4), _flip(y, k & 2), _flip(c, k & 1)) for k in range(1, 8)]
        for k, peer in enumerate(peers):
            cp = pltpu.make_async_remote_copy(
                src_ref=pack_ref, dst_ref=packs_ref.at[me], send_sem=send_sems.at[5 * n + k],
                recv_sem=recv_sems.at[5 * n + k], device_id=peer, device_id_type=MESH)
            cp.start()
            started.append(cp)
        loads = []
        for i in range(n):
            ld = pltpu.make_async_copy(ins[i].at[:, pl.ds(c * hrs[i], hrs[i]), :], own[i], local_sems.at[i])
            ld.start()
            loads.append(ld)
            cp = remote(i, 0, ins[i].at[:, pl.ds((1 - c) * hrs[i], hrs[i]), :], got[i], sibling)
            cp.start()
            started.append(cp)
        for i in range(n):
            loads[i].wait()
            remote(i, 0, ins[i].at[:, pl.ds(c * hrs[i], hrs[i]), :], got[i], sibling).wait_recv()
            for j, chip in enumerate(chips):
                chip_k = 2 * chip[0] + chip[1]
                sendb[i][j] = (own[i][chip_k] + got[i][chip_k]).astype(BF16)
                cp = remote(i, 1 + j, sendb[i].at[j], recvb[i].at[j], (chip[0], chip[1], c))
                cp.start()
                started.append(cp)
        for i in range(n):
            acc = own[i][k_me] + got[i][k_me]
            for j, chip in enumerate(chips):
                remote(i, 1 + j, sendb[i].at[j], recvb[i].at[j], (chip[0], chip[1], c)).wait_recv()
                acc = acc + recvb[i][j].astype(F32)
            mine = outs[i].at[pl.ds(c * hrs[i], hrs[i]), :]
            outs[i][pl.ds(pl.multiple_of(c * hrs[i], SUBLANES), hrs[i]), :] = acc
            cp = remote(i, 4, mine, mine, sibling)
            cp.start()
            started.append(cp)
        for i in range(n):
            theirs = outs[i].at[pl.ds((1 - c) * hrs[i], hrs[i]), :]
            remote(i, 4, theirs, theirs, sibling).wait_recv()
        for k, peer in enumerate(peers):
            pltpu.make_async_remote_copy(
                src_ref=pack_ref, dst_ref=packs_ref.at[4 * peer[0] + 2 * peer[1] + peer[2]],
                send_sem=send_sems.at[5 * n + k], recv_sem=recv_sems.at[5 * n + k],
                device_id=peer, device_id_type=MESH).wait_recv()
        for cp in started:
            cp.wait_send()

    vmem = pl.BlockSpec(memory_space=pltpu.VMEM)
    scratch = []
    scratch += [pltpu.VMEM((4, p.shape[1] // 2, p.shape[2]), F32) for p in pieces]
    scratch += [pltpu.VMEM((4, p.shape[1] // 2, p.shape[2]), F32) for p in pieces]
    scratch += [pltpu.VMEM((3, p.shape[1] // 2, p.shape[2]), BF16) for p in pieces]
    scratch += [pltpu.VMEM((3, p.shape[1] // 2, p.shape[2]), BF16) for p in pieces]
    scratch += [pltpu.SemaphoreType.DMA((5 * n + 7,)), pltpu.SemaphoreType.DMA((5 * n + 7,)), pltpu.SemaphoreType.DMA((n,))]
    return pl.pallas_call(
        body, name=name,
        out_shape=[jax.ShapeDtypeStruct((8,) + pack.shape, F32)] + [jax.ShapeDtypeStruct(p.shape[1:], F32) for p in pieces],
        in_specs=[vmem] + [pl.BlockSpec(memory_space=pl.ANY)] * n,
        out_specs=[vmem] * (1 + n),
        scratch_shapes=scratch,
        compiler_params=_params(vmem=VMEM_LIMIT),
    )(pack, *pieces)


def _rope_fwd(t, cos, sin, lane):
    lo = (lane % HEAD_DIM) < (HEAD_DIM // 2)
    return t * cos + jnp.where(lo, -pltpu.roll(t, 96, 1), pltpu.roll(t, 32, 1)) * sin


def _norm_proj(x, gmod, shift, w_int, pos, freq, bf_pad, tm):
    s = x.shape[0]

    def body(x_ref, g_ref, sh_ref, w_ref, pos_ref, fr_ref, bf_ref,
             a_ref, vb_ref, f_ref, gg_ref, ht_ref, cos_ref, sin_ref, q_ref, k_ref, v_ref, st_ref, carry):
        @pl.when(pl.program_id(0) == 0)
        def _():
            carry[...] = jnp.zeros_like(carry)

        xv = x_ref[...]
        r = lax.rsqrt(jnp.mean(xv * xv, axis=-1, keepdims=True) + NORM_EPS)
        hf = (xv * r) * g_ref[...] + sh_ref[...]
        hb = hf.astype(BF16)
        ht_ref[...] = hb.T
        pf = _dot(hb, w_ref[:, OFF_F:OFF_F + W_F])
        f_ref[...] = pf
        cumv = _cumsum_tile(pf, bf_ref[...], carry)
        bblk = _dot(hb, w_ref[:, OFF_B:OFF_B + W_B]).astype(BF16)
        vb_ref[...] = bblk[:, 1024:1536]
        pa = _dot(hb, w_ref[:, OFF_A:OFF_A + W_A])
        gg_ref[...] = _dot(hb, w_ref[:, OFF_G:OFF_G + W_G]).astype(BF16)
        _augment_heads(bblk, cumv, q_ref, k_ref, v_ref, st_ref)
        ang = pos_ref[...] * fr_ref[...]
        cosv, sinv = jnp.cos(ang), jnp.sin(ang)
        cos_ref[...] = cosv
        sin_ref[...] = sinv
        lane = lax.broadcasted_iota(jnp.int32, (tm, LANES), 1)
        for j in range(5):
            t = pa[:, LANES * j:LANES * (j + 1)]
            a_ref[:, LANES * j:LANES * (j + 1)] = _rope_fwd(t, cosv, sinv, lane).astype(BF16)
        a_ref[:, 640:768] = pa[:, 640:768].astype(BF16)

    row = lambda w: pl.BlockSpec((tm, w), lambda i: (i, 0))
    return pl.pallas_call(
        body, name="norm_proj", grid=(s // tm,),
        out_shape=[jax.ShapeDtypeStruct((s, W_A), BF16), jax.ShapeDtypeStruct((s, 512), BF16),
                   jax.ShapeDtypeStruct((s, W_F), F32), jax.ShapeDtypeStruct((s, W_G), BF16),
                   jax.ShapeDtypeStruct((D_MODEL, s), BF16),
                   jax.ShapeDtypeStruct((s, LANES), F32), jax.ShapeDtypeStruct((s, LANES), F32)]
        + [jax.ShapeDtypeStruct((s, 1024), BF16)] * 3 + [jax.ShapeDtypeStruct((s // tm, SUBLANES, LANES), F32)],
        in_specs=[row(D_MODEL), _const_spec((1, D_MODEL)), _const_spec((1, D_MODEL)), _const_spec((D_MODEL, W_INT)),
                  row(LANES), _const_spec((1, LANES)), _const_spec((1, LANES))],
        out_specs=[row(W_A), row(512), row(W_F), row(W_G), pl.BlockSpec((D_MODEL, tm), lambda i: (0, i)),
                   row(LANES), row(LANES), row(1024), row(1024), row(1024),
                   pl.BlockSpec((1, SUBLANES, LANES), lambda i: (i, 0, 0))],
        scratch_shapes=[pltpu.VMEM((SUBLANES, LANES), F32)],
        compiler_params=_params(("arbitrary",), VMEM_LIMIT),
    )(x, gmod, shift, w_int, pos, freq, bf_pad)


def _log_sigmoid(u):
    return jnp.minimum(u, 0.0) - jnp.log(1.0 + jnp.exp(-jnp.abs(u)))


def _cumsum_tile(f, b_f, carry):
    tb = f.shape[0]
    lane = lax.broadcasted_iota(jnp.int32, (tb, LANES), 1)
    logf = jnp.where(lane < N_HEADS, _log_sigmoid(f + b_f), 0.0)
    hi, mid, lo = _split3(logf)
    rows = lax.broadcasted_iota(jnp.int32, (tb, tb), 0)
    cols = lax.broadcasted_iota(jnp.int32, (tb, tb), 1)
    tril = (cols <= rows).astype(BF16)
    cum = _dot(tril, hi) + _dot(tril, mid) + _dot(tril, lo) + carry[0:1, :]
    carry[...] = jnp.broadcast_to(cum[tb - 1:tb, :], carry.shape)
    return cum


def _aug_lane(h):
    return 64 if h % 2 == 0 else 0


def _augment_heads(bblk, cumv, q_ref, k_ref, v_ref, st_ref):
    t = bblk.shape[0]
    lane = lax.broadcasted_iota(jnp.int32, (t, LANES), 1)
    lane_b = lane.astype(BF16)
    sub8 = lax.broadcasted_iota(jnp.int32, (SUBLANES, LANES), 0)
    lane8 = lax.broadcasted_iota(jnp.int32, (SUBLANES, LANES), 1)
    one = jnp.ones((t, LANES), BF16)
    zero = jnp.zeros((t, LANES), BF16)
    stats = jnp.zeros((SUBLANES, LANES), F32)
    for p in range(4):
        qblk = bblk[:, LANES * p:LANES * (p + 1)] * SCALE
        kblk = bblk[:, 512 + LANES * p:512 + LANES * (p + 1)]
        vblk = bblk[:, 1024 + LANES * p:1024 + LANES * (p + 1)]
        qf, kf = qblk.astype(F32), kblk.astype(F32)
        q2, k2, qk = qf * qf, kf * kf, qf * kf
        for odd in range(2):
            h = 2 * p + odd
            a0 = _aug_lane(h)
            data_b = (lane_b < 64) if odd == 0 else (lane_b >= 64)
            data = (lane < 64) if odd == 0 else (lane >= 64)
            hi, mid, lo = _split3(jnp.broadcast_to(cumv[:, h:h + 1], (t, LANES)))
            ones3_q = (lane_b >= a0 + 3) & (lane_b < a0 + 6)
            ones3_k = (lane_b >= a0) & (lane_b < a0 + 3)
            aug_q = jnp.where(lane_b == a0, hi, jnp.where(lane_b == a0 + 1, mid, jnp.where(
                lane_b == a0 + 2, lo, jnp.where(ones3_q, one, zero))))
            aug_k = jnp.where(ones3_k, one, jnp.where(lane_b == a0 + 3, -hi, jnp.where(
                lane_b == a0 + 4, -mid, jnp.where(lane_b == a0 + 5, -lo, zero))))
            q_ref[:, LANES * h:LANES * (h + 1)] = jnp.where(data_b, qblk, aug_q)
            k_ref[:, LANES * h:LANES * (h + 1)] = jnp.where(data_b, kblk, aug_k)
            v_ref[:, LANES * h:LANES * (h + 1)] = jnp.where(data_b, vblk, jnp.where(lane_b == a0, one, zero))
            qn = jnp.sqrt(jnp.max(jnp.sum(jnp.where(data, q2, 0.0), axis=-1, keepdims=True)))
            kn = jnp.sqrt(jnp.max(jnp.sum(jnp.where(data, k2, 0.0), axis=-1, keepdims=True)))
            dmin = jnp.min(jnp.sum(jnp.where(data, qk, 0.0), axis=-1, keepdims=True))
            c_first, c_last = cumv[0:1, h:h + 1], cumv[t - 1:t, h:h + 1]
            row = jnp.where(lane8 == 0, qn, jnp.where(lane8 == 1, kn, jnp.where(
                lane8 == 2, c_first, jnp.where(lane8 == 3, c_last, jnp.where(lane8 == 4, dmin, 0.0)))))
            stats = jnp.where(sub8 == h, row, stats)
    st_ref[0] = stats


PRUNE_MARGIN = 88.0


def _fox_tile_ranges(stats):
    nt = stats.shape[0]
    qn, kn, c_first, c_last, d_min = (stats[:, :, n] for n in range(5))
    bound = (1.01 * qn[:, None, :] * kn[None, :, :] - jnp.minimum(d_min, 0.0)[:, None, :] + 0.05
             + c_first[:, None, :] - c_last[None, :, :])
    idx = jnp.arange(nt)
    skip = (bound <= -PRUNE_MARGIN) & (idx[None, :, None] < idx[:, None, None])
    first_key = jnp.sum(jnp.cumprod(skip, axis=1), axis=1)
    needed = (idx[None, :, None] >= first_key[:, None, :]) & (idx[None, :, None] <= idx[:, None, None])
    last_query = jnp.max(jnp.where(needed, idx[:, None, None], 0), axis=0)
    n_query = last_query - idx[:, None] + 1
    table = jnp.zeros((4, SUBLANES, LANES), F32)
    for odd in range(2):
        table = table.at[:, odd, :nt].set(first_key[:, odd::2].T.astype(F32))
        table = table.at[:, 2 + odd, :nt].set(n_query[:, odd::2].T.astype(F32))
    return table


def _lane_scalar(block, row, lane_idx):
    sub8 = lax.broadcasted_iota(jnp.int32, (SUBLANES, LANES), 0)
    lane8 = lax.broadcasted_iota(jnp.int32, (SUBLANES, LANES), 1)
    return jnp.sum(jnp.where((sub8 == row) & (lane8 == lane_idx), block, 0.0)).astype(jnp.int32)


def _fox_fwd(qa, ka, va, ranges, t):
    s = qa.shape[0]
    nt = s // t
    nc = t // LANES

    def body(rg_ref, q_ref, k_ref, v_ref, o_ref, lse_ref):
        i = pl.program_id(1)
        lane = lax.broadcasted_iota(jnp.int32, (t, LANES), 1)
        rows = lax.broadcasted_iota(jnp.int32, (t, t), 0)
        cols = lax.broadcasted_iota(jnp.int32, (t, t), 1)
        firsts = [jnp.clip(_lane_scalar(rg_ref[0], hh, i), 0, i) for hh in range(2)]
        first = jnp.maximum(firsts[0], firsts[1])

        def update(js, carry, heads=(0, 1), diagonal=False):
            offs = [pl.multiple_of(j * t, t) for j in js]
            kts = [k_ref[pl.ds(off, t), :] for off in offs]
            vts = [v_ref[pl.ds(off, t), :] for off in offs]
            scs = {hh: [_dot(q_ref[:, LANES * hh:LANES * (hh + 1)], kt[:, LANES * hh:LANES * (hh + 1)], NT) for kt in kts]
                   for hh in heads}
            if diagonal:
                scs = {hh: [jnp.where(cols <= rows, sc, NEG) for sc in scs[hh]] for hh in heads}
            m_new = {}
            for hh in heads:
                part = None
                for sc in scs[hh]:
                    for cch in range(nc):
                        chunk = sc[:, LANES * cch:LANES * (cch + 1)]
                        part = chunk if part is None else jnp.maximum(part, chunk)
                m_new[hh] = jnp.maximum(carry[2 * hh], jnp.max(part, axis=-1, keepdims=True))
            alphas = {hh: jnp.exp(carry[2 * hh] - m_new[hh]) for hh in heads}
            ps = {hh: [jnp.exp(sc - m_new[hh]).astype(BF16) for sc in scs[hh]] for hh in heads}
            out = list(carry)
            for hh in heads:
                pv = None
                for p, vt in zip(ps[hh], vts):
                    term = _dot(p, vt[:, LANES * hh:LANES * (hh + 1)])
                    pv = term if pv is None else pv + term
                out[2 * hh], out[2 * hh + 1] = m_new[hh], alphas[hh] * carry[2 * hh + 1] + pv
            return tuple(out)

        col0 = jnp.full((t, 1), NEG, F32)
        zero = jnp.zeros((t, LANES), F32)
        carry = (col0, zero, col0, zero)
        for hh in range(2):
            carry = lax.fori_loop(firsts[hh], first, lambda j, cr, hh=hh: update([j], cr, heads=(hh,)), carry)
        n_off = i - first
        carry = lax.fori_loop(0, n_off // 2, lambda u, cr: update([first + 2 * u, first + 2 * u + 1], cr), carry)
        carry = lax.fori_loop(0, n_off % 2, lambda u, cr: update([i - 1], cr), carry)
        m0, acc0, m1, acc1 = update([i], carry, diagonal=True)
        l0, l1 = acc0[:, _aug_lane(0):_aug_lane(0) + 1], acc1[:, _aug_lane(1):_aug_lane(1) + 1]
        o_ref[...] = jnp.where(lane < 64, acc0 * (1.0 / l0), acc1 * (1.0 / l1)).astype(BF16)
        sub = lax.broadcasted_iota(jnp.int32, (SUBLANES, t), 0)
        lse0 = jnp.broadcast_to(m0 + jnp.log(l0), (t, LANES)).T[0:SUBLANES, :]
        lse1 = jnp.broadcast_to(m1 + jnp.log(l1), (t, LANES)).T[0:SUBLANES, :]
        lse_ref[0] = jnp.where(sub == 0, lse0, jnp.where(sub == 1, lse1, 0.0))

    pair = pl.BlockSpec((s, 2 * LANES), lambda p, i: (0, p))
    return pl.pallas_call(
        body, name="fox_fwd", grid=(4, nt),
        out_shape=[jax.ShapeDtypeStruct((s, 512), BF16), jax.ShapeDtypeStruct((4, SUBLANES, s), F32)],
        in_specs=[pl.BlockSpec((1, SUBLANES, LANES), lambda p, i: (p, 0, 0)),
                  pl.BlockSpec((t, 2 * LANES), lambda p, i: (i, p)), pair, pair],
        out_specs=[pl.BlockSpec((t, LANES), lambda p, i: (i, p)),
                   pl.BlockSpec((1, SUBLANES, t), lambda p, i: (p, 0, i))],
        compiler_params=_params(("parallel", "arbitrary"), VMEM_LIMIT),
    )(ranges, qa, ka, va)


def _dup_halves(blk, lane):
    f = blk.astype(F32)
    r = pltpu.roll(f, 64, 1)
    return jnp.where(lane < 64, f, r).astype(BF16), jnp.where(lane >= 64, f, r).astype(BF16)


GROUP = 4
GROUP_ROWS = GROUP * WINDOW


def _stack_heads(ref, g, lane):
    parts = []
    for pb in (2 * g, 2 * g + 1):
        blk = ref[:, LANES * pb:LANES * (pb + 1)]
        zero = jnp.zeros_like(blk)
        parts += [jnp.where(lane < 64, blk, zero), jnp.where(lane >= 64, blk, zero)]
    return jnp.concatenate(parts, axis=0)


def _swa_band(a_ref, ap_ref, g, lane):
    k = jnp.concatenate([_dup_halves(ap_ref[:, 512:640], lane)[g], _dup_halves(a_ref[:, 512:640], lane)[g]], axis=0)
    v = jnp.concatenate([_dup_halves(ap_ref[:, 640:768], lane)[g], _dup_halves(a_ref[:, 640:768], lane)[g]], axis=0)
    return k, v


def _swa_logits(q, k, has_prev):
    sc = _dot(q, k, NT) * SCALE
    rr = lax.broadcasted_iota(jnp.int32, sc.shape, 0) % WINDOW
    cc = lax.broadcasted_iota(jnp.int32, sc.shape, 1)
    valid = (cc > rr) & (cc <= rr + WINDOW) & (has_prev | (cc >= WINDOW))
    return jnp.where(valid, sc, NEG)


def _per_head_column(values):
    return jnp.concatenate([jnp.broadcast_to(v, (WINDOW, 1)) for v in values], axis=0)


SWA_BLOCKS = 4
SWA_ROWS = SWA_BLOCKS * WINDOW
SWA_FWD_BLOCKS = 8


def _swa_blocks(a_ref, ap_ref, n_blocks):
    return [ap_ref] + [a_ref.at[pl.ds(WINDOW * jb, WINDOW), :] for jb in range(n_blocks)]


def _swa_fwd(a, sinks):
    s = a.shape[0]
    n_blocks = SWA_FWD_BLOCKS
    n_rows = n_blocks * WINDOW

    def body(sink_ref, a_ref, ap_ref, o_ref, l_ref):
        lane = lax.broadcasted_iota(jnp.int32, (WINDOW, LANES), 1)
        blocks = _swa_blocks(a_ref, ap_ref, n_blocks)
        units = [(jb, g) for jb in range(n_blocks) for g in range(2)]
        sinks_col = [_per_head_column([sink_ref[GROUP * g + hh] for hh in range(GROUP)]) for g in range(2)]
        bands = [_swa_band(blocks[jb + 1], blocks[jb], g, lane) for jb, g in units]
        scs = [_swa_logits(_stack_heads(blocks[jb + 1], g, lane), bands[u][0],
                           (pl.program_id(0) > 0) if jb == 0 else True) for u, (jb, g) in enumerate(units)]
        ms = [jnp.maximum(jnp.max(scs[u], axis=-1, keepdims=True), sinks_col[g]) for u, (jb, g) in enumerate(units)]
        ps = [jnp.exp(scs[u] - ms[u]) for u in range(len(units))]
        dens = [jnp.sum(ps[u], axis=-1, keepdims=True) + jnp.exp(sinks_col[g] - ms[u]) for u, (jb, g) in enumerate(units)]
        outs = [_dot((ps[u] * (1.0 / dens[u])).astype(BF16), bands[u][1]) for u in range(len(units))]
        for jb in range(n_blocks):
            rows = slice(WINDOW * jb, WINDOW * (jb + 1))
            l_all = jnp.zeros((WINDOW, LANES), F32)
            for g in range(2):
                u = 2 * jb + g
                lcol = ms[u] + jnp.log(dens[u])
                for pb in range(2):
                    r0 = 2 * pb * WINDOW
                    o_ref[rows, LANES * (2 * g + pb):LANES * (2 * g + pb + 1)] = jnp.where(
                        lane < 64, outs[u][r0:r0 + WINDOW], outs[u][r0 + WINDOW:r0 + 2 * WINDOW]).astype(BF16)
                for hh in range(GROUP):
                    l_all = jnp.where(lane == GROUP * g + hh, lcol[WINDOW * hh:WINDOW * (hh + 1)], l_all)
            l_ref[rows, :] = l_all

    return pl.pallas_call(
        body, name="swa_fwd", grid=(s // n_rows,),
        out_shape=[jax.ShapeDtypeStruct((s, 512), BF16), jax.ShapeDtypeStruct((s, LANES), F32)],
        in_specs=[pl.BlockSpec(memory_space=pltpu.SMEM),
                  pl.BlockSpec((n_rows, W_A), lambda i: (i, 0)),
                  pl.BlockSpec((WINDOW, W_A), lambda i: (jnp.maximum(n_blocks * i - 1, 0), 0))],
        out_specs=[pl.BlockSpec((n_rows, 512), lambda i: (i, 0)), pl.BlockSpec((n_rows, LANES), lambda i: (i, 0))],
        compiler_params=_params(("parallel",)),
    )(sinks, a, a)


def _mid(att_a, att_b, g, x, target, gate, g_final, wo_a, wo_b, w_out, tm=256):
    s = x.shape[0]
    nt = s // tm

    def body(aa_ref, ab_ref, g_ref, x_ref, t_ref, gate_ref, gf_ref, woa_ref, wob_ref, wout_ref,
             dx_ref, daa_ref, dab_ref, dg_ref, delta_ref, dwoa_ref, dwob_ref, dwout_ref, vec_ref,
             acc_gf, acc_gate, acc_loss):
        step = pl.program_id(0)

        @pl.when(step == 0)
        def _():
            dwoa_ref[...] = jnp.zeros_like(dwoa_ref)
            dwob_ref[...] = jnp.zeros_like(dwob_ref)
            dwout_ref[...] = jnp.zeros_like(dwout_ref)
            acc_gf[...] = jnp.zeros_like(acc_gf)
            acc_gate[...] = jnp.zeros_like(acc_gate)
            acc_loss[...] = jnp.zeros_like(acc_loss)

        def fold(v):
            return jnp.sum(v.reshape(tm // SUBLANES, SUBLANES, D_MODEL), axis=0)

        gate = gate_ref[...]
        gfin = gf_ref[...]
        branches = []
        for att_ref, z_off, wo_ref in ((aa_ref, 0, woa_ref), (ab_ref, 512, wob_ref)):
            att = att_ref[...].astype(F32)
            z = g_ref[:, z_off:z_off + 512].astype(F32)
            sz = _sigmoid(z)
            silu = z * sz
            u = (att * silu).astype(BF16)
            branches.append((att, z, sz, silu, u, _dot(u, wo_ref[...])))
        ga = g_ref[:, 1024:2048].astype(F32)
        gb = g_ref[:, 2048:3072].astype(F32)
        sga, sgb = _sigmoid(ga), _sigmoid(gb)
        y_a, y_b = branches[0][5], branches[1][5]
        mb = (sga * y_a + sgb * y_b).astype(BF16)
        o = _dot(mb, wout_ref[...])
        x2 = x_ref[...] + gate * o
        r2 = lax.rsqrt(jnp.mean(x2 * x2, axis=-1, keepdims=True) + NORM_EPS)
        xn2 = x2 * r2
        err = xn2 * gfin - t_ref[...]
        acc_loss[...] += fold(err * err)
        dy = err * (1.0 / D_MODEL)
        acc_gf[...] += fold(dy * xn2)
        dxn = dy * gfin
        dx2 = r2 * (dxn - xn2 * jnp.mean(dxn * xn2, axis=-1, keepdims=True))
        dx_ref[...] = dx2
        acc_gate[...] += fold(dx2 * o)
        d_o = (dx2 * gate).astype(BF16)
        dm = _dot(d_o, wout_ref[...], NT)
        dwout_ref[...] += _dot(mb, d_o, TN)
        dg_ref[:, 1024:2048] = (dm * y_a * sga * (1.0 - sga)).astype(BF16)
        dg_ref[:, 2048:3072] = (dm * y_b * sgb * (1.0 - sgb)).astype(BF16)
        dybs = [(dm * sg).astype(BF16) for sg in (sga, sgb)]
        dus = [_dot(dyb, wo_ref[...], NT) for dyb, wo_ref in zip(dybs, (woa_ref, wob_ref))]
        for branch, dyb, dwo_ref in zip(branches, dybs, (dwoa_ref, dwob_ref)):
            dwo_ref[...] += _dot(branch[4], dyb, TN)
        for (att, z, sz, silu, u, _), du, datt_ref, z_off in (
                (branches[0], dus[0], daa_ref, 0), (branches[1], dus[1], dab_ref, 512)):
            datt = du * silu
            datt_ref[...] = datt.astype(BF16)
            dg_ref[:, z_off:z_off + 512] = (du * att * (sz * (1.0 + z * (1.0 - sz)))).astype(BF16)
            if z_off == 512:
                prod = datt * att
                hi = prod.astype(BF16)
                lo = (prod - hi.astype(F32)).astype(BF16)
                er = lax.broadcasted_iota(jnp.int32, (512, LANES), 0)
                ec = lax.broadcasted_iota(jnp.int32, (512, LANES), 1)
                e = (er // HEAD_DIM == ec).astype(BF16)
                delta = _dot(hi, e) + _dot(lo, e)
                delta_ref[...] = delta.T[0:SUBLANES, :]

        @pl.when(step == nt - 1)
        def _():
            sub = lax.broadcasted_iota(jnp.int32, (SUBLANES, D_MODEL), 0)
            dgf = jnp.sum(acc_gf[...], axis=0, keepdims=True)
            dgate = jnp.sum(acc_gate[...], axis=0, keepdims=True)
            loss = 0.5 * jnp.sum(acc_loss[...]) * (1.0 / D_MODEL)
            vec_ref[...] = jnp.where(sub == 0, dgf, jnp.where(sub == 1, dgate, jnp.where(sub == 2, loss, 0.0)))

    row = lambda w: pl.BlockSpec((tm, w), lambda i: (i, 0))
    return pl.pallas_call(
        body, name="mid", grid=(nt,),
        out_shape=[jax.ShapeDtypeStruct((s, D_MODEL), F32), jax.ShapeDtypeStruct((s, 512), BF16),
                   jax.ShapeDtypeStruct((s, 512), BF16), jax.ShapeDtypeStruct((s, W_G), BF16),
                   jax.ShapeDtypeStruct((SUBLANES, s), F32),
                   jax.ShapeDtypeStruct((512, D_MODEL), F32), jax.ShapeDtypeStruct((512, D_MODEL), F32),
                   jax.ShapeDtypeStruct((D_MODEL, D_MODEL), F32), jax.ShapeDtypeStruct((SUBLANES, D_MODEL), F32)],
        in_specs=[row(512), row(512), row(W_G), row(D_MODEL), row(D_MODEL),
                  _const_spec((1, D_MODEL)), _const_spec((1, D_MODEL)),
                  _const_spec((512, D_MODEL)), _const_spec((512, D_MODEL)), _const_spec((D_MODEL, D_MODEL))],
        out_specs=[row(D_MODEL), row(512), row(512), row(W_G),
                   pl.BlockSpec((SUBLANES, tm), lambda i: (0, i)),
                   pl.BlockSpec((512, D_MODEL), lambda i: (0, 0)), pl.BlockSpec((512, D_MODEL), lambda i: (0, 0)),
                   pl.BlockSpec((D_MODEL, D_MODEL), lambda i: (0, 0)), pl.BlockSpec((SUBLANES, D_MODEL), lambda i: (0, 0))],
        scratch_shapes=[pltpu.VMEM((SUBLANES, D_MODEL), F32)] * 3,
        compiler_params=_params(("arbitrary",), VMEM_LIMIT),
    )(att_a, att_b, g, x, target, gate, g_final, wo_a, wo_b, w_out)


def _rope_bwd(dt, cos, sin, lane):
    u = dt * sin
    lo = (lane % HEAD_DIM) < (HEAD_DIM // 2)
    return dt * cos + jnp.where(lo, pltpu.roll(u, 96, 1), -pltpu.roll(u, 32, 1))


def _swa_bwd(a, datt, l_all, sinks, cos, sin):
    s = a.shape[0]
    nt = s // SWA_ROWS

    def body(sink_ref, a_ref, ap_ref, do_ref, l_ref, cos_ref, sin_ref, da_ref, ds_ref, halo):
        step = pl.program_id(0)
        tile = nt - 1 - step

        @pl.when(step == 0)
        def _():
            halo[...] = jnp.zeros_like(halo)
            ds_ref[...] = jnp.zeros_like(ds_ref)

        lane = lax.broadcasted_iota(jnp.int32, (WINDOW, LANES), 1)
        sub8 = lax.broadcasted_iota(jnp.int32, (SUBLANES, LANES), 0)
        lane8 = lax.broadcasted_iota(jnp.int32, (SUBLANES, LANES), 1)
        blocks = _swa_blocks(a_ref, ap_ref, SWA_BLOCKS)
        dsink = jnp.zeros((SUBLANES, LANES), F32)

        def join(pair, r0):
            x0, x1 = pair[0][r0:r0 + WINDOW], pair[1][r0:r0 + WINDOW]
            return jnp.where(lane < 64, x0 + pltpu.roll(x0, 64, 1), x1 + pltpu.roll(x1, 64, 1))

        units = [(jb, g) for jb in range(SWA_BLOCKS) for g in range(2)]
        n_u = len(units)
        sinks_col = [_per_head_column([sink_ref[GROUP * g + hh] for hh in range(GROUP)]) for g in range(2)]
        bands = [_swa_band(blocks[jb + 1], blocks[jb], g, lane) for jb, g in units]
        qs = [_stack_heads(blocks[jb + 1], g, lane) for jb, g in units]
        doms = [_stack_heads(do_ref.at[pl.ds(WINDOW * jb, WINDOW), :], g, lane) for jb, g in units]
        lcols = []
        for jb, g in units:
            lv = l_ref[WINDOW * jb:WINDOW * (jb + 1), :]
            lcols.append(_per_head_column([lv[:, GROUP * g + hh:GROUP * g + hh + 1] for hh in range(GROUP)]))
        ps = [jnp.exp(_swa_logits(qs[u], bands[u][0], (tile > 0) if jb == 0 else True) - lcols[u])
              for u, (jb, g) in enumerate(units)]
        dps = [_dot(doms[u], bands[u][1], NT) for u in range(n_u)]
        deltas = [jnp.sum(ps[u] * dps[u], axis=-1, keepdims=True) for u in range(n_u)]
        for u, (jb, g) in enumerate(units):
            sink_term = jnp.exp(sinks_col[g] - lcols[u]) * deltas[u]
            for hh in range(GROUP):
                tot = jnp.sum(sink_term[WINDOW * hh:WINDOW * (hh + 1)])
                dsink = dsink + jnp.where((sub8 == 0) & (lane8 == GROUP * g + hh), -tot, 0.0)
        dss = [(ps[u] * (dps[u] - deltas[u])).astype(BF16) for u in range(n_u)]
        dqs = [_dot(dss[u], bands[u][0]) * SCALE for u in range(n_u)]
        dks = [_dot(dss[u], qs[u], TN) * SCALE for u in range(n_u)]
        dvs = [_dot(ps[u].astype(BF16), doms[u], TN) for u in range(n_u)]

        carry_k, carry_v = halo[:, 0:LANES], halo[:, LANES:2 * LANES]
        for jb in reversed(range(SWA_BLOCKS)):
            rows = slice(WINDOW * jb, WINDOW * (jb + 1))
            cosv, sinv = cos_ref[rows, :], sin_ref[rows, :]
            for g in range(2):
                dq = dqs[2 * jb + g]
                for pb in range(2):
                    r0 = 2 * pb * WINDOW
                    dq_pair = jnp.where(lane < 64, dq[r0:r0 + WINDOW], dq[r0 + WINDOW:r0 + 2 * WINDOW])
                    da_ref[rows, LANES * (2 * g + pb):LANES * (2 * g + pb + 1)] = _rope_bwd(
                        dq_pair, cosv, sinv, lane).astype(BF16)
            dkb, dvb = dks[2 * jb:2 * jb + 2], dvs[2 * jb:2 * jb + 2]
            da_ref[rows, 512:640] = _rope_bwd(join(dkb, WINDOW) + carry_k, cosv, sinv, lane).astype(BF16)
            da_ref[rows, 640:768] = (join(dvb, WINDOW) + carry_v).astype(BF16)
            carry_k, carry_v = join(dkb, 0), join(dvb, 0)
        halo[:, 0:LANES] = carry_k
        halo[:, LANES:2 * LANES] = carry_v
        ds_ref[...] += dsink

    rev = lambda w: pl.BlockSpec((SWA_ROWS, w), lambda i: (nt - 1 - i, 0))
    return pl.pallas_call(
        body, name="swa_bwd", grid=(nt,),
        out_shape=[jax.ShapeDtypeStruct((s, W_A), BF16), jax.ShapeDtypeStruct((SUBLANES, LANES), F32)],
        in_specs=[pl.BlockSpec(memory_space=pltpu.SMEM), rev(W_A),
                  pl.BlockSpec((WINDOW, W_A), lambda i: (jnp.maximum(SWA_BLOCKS * (nt - 1 - i) - 1, 0), 0)),
                  rev(512), rev(LANES), rev(LANES), rev(LANES)],
        out_specs=[rev(W_A), pl.BlockSpec((SUBLANES, LANES), lambda i: (0, 0))],
        scratch_shapes=[pltpu.VMEM((WINDOW, 2 * LANES), F32)],
        compiler_params=_params(("arbitrary",)),
    )(sinks, a, a, datt, l_all, cos, sin)


def _fox_bwd(qa, ka, vb, do, lse, delta, ranges, t):
    s = qa.shape[0]
    nt = s // t

    def body(rg_ref, q_ref, do_ref, lse_ref, dl_ref, k_ref, v_ref, dq_ref, dk_ref, dv_ref, dc_ref, dr_ref, dq_acc):
        p = pl.program_id(0)
        j = pl.program_id(1)
        n_queries = [jnp.clip(_lane_scalar(rg_ref[0], 2 + hh, j), 1, nt - j) for hh in range(2)]

        @pl.when(j == 0)
        def _():
            dq_acc[...] = jnp.zeros_like(dq_acc)

        lane = lax.broadcasted_iota(jnp.int32, (t, LANES), 1)
        rows = lax.broadcasted_iota(jnp.int32, (t, t), 0)
        cols = lax.broadcasted_iota(jnp.int32, (t, t), 1)
        kt = k_ref[...]
        vt = v_ref[...]

        ks = [kt[:, LANES * hh:LANES * (hh + 1)] for hh in range(2)]

        def tile(qis, carry, heads=(0, 1), diagonal=False):
            dk0, dk1, dv = carry
            offs = [pl.multiple_of(i * t, t) for i in qis]
            units = [(u, hh) for u in range(len(qis)) for hh in heads]
            qts = [q_ref[pl.ds(off, t), :] for off in offs]
            dos = [do_ref[pl.ds(off, t), :] for off in offs]
            lses = [lse_ref[0, :, pl.ds(off, t)] for off in offs]
            dls = [dl_ref[0, :, pl.ds(off, t)] for off in offs]
            qs = [qts[u][:, LANES * hh:LANES * (hh + 1)] for u, hh in units]
            doms = [jnp.where((lane < 64) if hh == 0 else (lane >= 64), dos[u], jnp.zeros_like(dos[u])) for u, hh in units]
            sts = [_dot(ks[hh], qs[n], NT) for n, (u, hh) in enumerate(units)]
            dpts = [_dot(vt, doms[n], NT) for n in range(len(units))]
            if diagonal:
                sts = [jnp.where(cols >= rows, st, NEG) for st in sts]
            pts = [jnp.exp(sts[n] - lses[u][hh:hh + 1, :]) for n, (u, hh) in enumerate(units)]
            dsts = [(pts[n] * (dpts[n] - dls[u][hh:hh + 1, :])).astype(BF16) for n, (u, hh) in enumerate(units)]
            for n, (u, hh) in enumerate(units):
                dv = dv + _dot(pts[n].astype(BF16), doms[n])
                term = _dot(dsts[n], qs[n])
                dk0, dk1 = (dk0 + term, dk1) if hh == 0 else (dk0, dk1 + term)
                dq_acc[hh, pl.ds(offs[u], t), :] += _dot(dsts[n], ks[hh], TN)
            return dk0, dk1, dv

        zero = jnp.zeros((t, LANES), F32)
        carry = tile([j], (zero, zero, zero), diagonal=True)
        n_rest = jnp.minimum(n_queries[0], n_queries[1]) - 1
        carry = lax.fori_loop(0, n_rest // 2, lambda u, cr: tile([j + 1 + 2 * u, j + 2 + 2 * u], cr), carry)
        carry = lax.fori_loop(0, n_rest % 2, lambda u, cr: tile([j + n_rest], cr), carry)
        for hh in range(2):
            carry = lax.fori_loop(j + 1 + n_rest, j + n_queries[hh], lambda i, cr, hh=hh: tile([i], cr, heads=(hh,)), carry)
        dk0, dk1, dv = carry
        e0, e1 = _aug_lane(0), _aug_lane(1)
        dk_ref[...] = jnp.where(lane < 64, dk0, dk1).astype(BF16)
        dv_ref[...] = dv.astype(BF16)
        c0 = jnp.broadcast_to(dk0[:, e0 + 3:e0 + 4], (t, LANES))
        c1 = jnp.broadcast_to(dk1[:, e1 + 3:e1 + 4], (t, LANES))
        dc_ref[0] = jnp.where(lane == 2 * p, -c0, jnp.where(lane == 2 * p + 1, -c1, 0.0))

        @pl.when(j == nt - 1)
        def _():
            lane_s = lax.broadcasted_iota(jnp.int32, (s, LANES), 1)
            a0, a1 = dq_acc[0], dq_acc[1]
            dq_ref[...] = (jnp.where(lane_s < 64, a0, a1) * SCALE).astype(BF16)
            r0 = jnp.broadcast_to(a0[:, e0:e0 + 1], (s, LANES))
            r1 = jnp.broadcast_to(a1[:, e1:e1 + 1], (s, LANES))
            dr_ref[0] = jnp.where(lane_s == 2 * p, r0, jnp.where(lane_s == 2 * p + 1, r1, 0.0))

    return pl.pallas_call(
        body, name="fox_bwd", grid=(4, nt),
        out_shape=[jax.ShapeDtypeStruct((s, 512), BF16), jax.ShapeDtypeStruct((s, 512), BF16),
                   jax.ShapeDtypeStruct((s, 512), BF16), jax.ShapeDtypeStruct((4, s, LANES), F32),
                   jax.ShapeDtypeStruct((4, s, LANES), F32)],
        in_specs=[pl.BlockSpec((1, SUBLANES, LANES), lambda p, j: (p, 0, 0)),
                  pl.BlockSpec((s, 2 * LANES), lambda p, j: (0, p)),
                  pl.BlockSpec((s, LANES), lambda p, j: (0, p)),
                  pl.BlockSpec((1, SUBLANES, s), lambda p, j: (p, 0, 0)),
                  pl.BlockSpec((1, SUBLANES, s), lambda p, j: (p, 0, 0)),
                  pl.BlockSpec((t, 2 * LANES), lambda p, j: (j, p)),
                  pl.BlockSpec((t, LANES), lambda p, j: (j, p))],
        out_specs=[pl.BlockSpec((s, LANES), lambda p, j: (0, p)),
                   pl.BlockSpec((t, LANES), lambda p, j: (j, p)),
                   pl.BlockSpec((t, LANES), lambda p, j: (j, p)),
                   pl.BlockSpec((1, t, LANES), lambda p, j: (p, j, 0)),
                   pl.BlockSpec((1, s, LANES), lambda p, j: (p, 0, 0))],
        scratch_shapes=[pltpu.VMEM((2, s, LANES), F32)],
        compiler_params=_params(("parallel", "arbitrary"), VMEM_LIMIT),
    )(ranges, qa, do, lse, delta, ka, vb)


def _forget_logit_grad(dc_ref, dr_ref, f, b_f, carry):
    tb = f.shape[0]
    lane = lax.broadcasted_iota(jnp.int32, (tb, LANES), 1)
    dc = dc_ref[0] + dr_ref[0]
    for k in range(1, 4):
        dc = dc + (dc_ref[k] + dr_ref[k])
    hi, mid, lo = _split3(dc)
    rows = lax.broadcasted_iota(jnp.int32, (tb, tb), 0)
    cols = lax.broadcasted_iota(jnp.int32, (tb, tb), 1)
    triu = (cols >= rows).astype(BF16)
    dlogf = _dot(triu, hi) + _dot(triu, mid) + _dot(triu, lo) + carry[0:1, :]
    carry[...] = jnp.broadcast_to(dlogf[0:1, :], carry.shape)
    return jnp.where(lane < N_HEADS, dlogf * _sigmoid(-(f + b_f)), 0.0)


def _dh_norm_bwd(d_a, d_q, d_k, d_v, dcum_k, dcum_q, f, bf_pad, d_g, w_t, x, dx2, gnorm, scale1, tm=512):
    s = x.shape[0]
    nt = s // tm

    def body(da_ref, dq_ref, dk_ref, dv_ref, dc_ref, dr_ref, f_ref, bf_ref, dg_ref, w_ref, x_ref, dx2_ref, g_ref, sc_ref,
             gx_ref, vec_ref, df_ref, db_ref, a_sh, a_sc, a_g, carry):
        step = pl.program_id(0)

        @pl.when(step == 0)
        def _():
            a_sh[...] = jnp.zeros_like(a_sh)
            a_sc[...] = jnp.zeros_like(a_sc)
            a_g[...] = jnp.zeros_like(a_g)
            carry[...] = jnp.zeros_like(carry)
            db_ref[...] = jnp.zeros_like(db_ref)

        def fold(v):
            return jnp.sum(v.reshape(tm // SUBLANES, SUBLANES, D_MODEL), axis=0)

        dfb = _forget_logit_grad(dc_ref, dr_ref, f_ref[...], bf_ref[...], carry)
        d_f = dfb.astype(BF16)
        df_ref[...] = d_f
        sub8 = lax.broadcasted_iota(jnp.int32, (SUBLANES, LANES), 0)
        db_ref[...] += jnp.where(sub8 == 0, jnp.sum(dfb, axis=0, keepdims=True), 0.0)
        d_all = jnp.concatenate([da_ref[...], dq_ref[...], dk_ref[...], dv_ref[...], d_f, dg_ref[...]], axis=1)
        dh = _dot(d_all, w_ref[...])
        xv = x_ref[...]
        r = lax.rsqrt(jnp.mean(xv * xv, axis=-1, keepdims=True) + NORM_EPS)
        xn = xv * r
        gn = g_ref[...]
        a_sh[...] += fold(dh)
        a_sc[...] += fold(dh * (xn * gn))
        dn1 = dh * sc_ref[...]
        a_g[...] += fold(dn1 * xn)
        dxn = dn1 * gn
        gx_ref[...] = dx2_ref[...] + r * (dxn - xn * jnp.mean(dxn * xn, axis=-1, keepdims=True))

        @pl.when(step == nt - 1)
        def _():
            sub = lax.broadcasted_iota(jnp.int32, (SUBLANES, D_MODEL), 0)
            v_sh = jnp.sum(a_sh[...], axis=0, keepdims=True)
            v_sc = jnp.sum(a_sc[...], axis=0, keepdims=True)
            v_g = jnp.sum(a_g[...], axis=0, keepdims=True)
            vec_ref[...] = jnp.where(sub == 0, v_sh, jnp.where(sub == 1, v_sc, jnp.where(sub == 2, v_g, 0.0)))

    row = lambda w: pl.BlockSpec((tm, w), lambda i: (nt - 1 - i, 0))
    slabs = pl.BlockSpec((4, tm, LANES), lambda i: (0, nt - 1 - i, 0))
    return pl.pallas_call(
        body, name="dh_norm_bwd", grid=(nt,),
        out_shape=[jax.ShapeDtypeStruct((s, D_MODEL), F32), jax.ShapeDtypeStruct((SUBLANES, D_MODEL), F32),
                   jax.ShapeDtypeStruct((s, LANES), BF16), jax.ShapeDtypeStruct((SUBLANES, LANES), F32)],
        in_specs=[row(W_A), row(512), row(512), row(512), slabs, slabs, row(W_F), _const_spec((1, LANES)), row(W_G),
                  _const_spec((W_INT, D_MODEL)), row(D_MODEL), row(D_MODEL), _const_spec((1, D_MODEL)),
                  _const_spec((1, D_MODEL))],
        out_specs=[row(D_MODEL), pl.BlockSpec((SUBLANES, D_MODEL), lambda i: (0, 0)), row(LANES),
                   pl.BlockSpec((SUBLANES, LANES), lambda i: (0, 0))],
        scratch_shapes=[pltpu.VMEM((SUBLANES, D_MODEL), F32)] * 3 + [pltpu.VMEM((SUBLANES, LANES), F32)],
        compiler_params=_params(("arbitrary",), VMEM_LIMIT),
    )(d_a, d_q, d_k, d_v, dcum_k, dcum_q, f, bf_pad, d_g, w_t, x, dx2, gnorm, scale1)


def _dw_in(h_t, d_a, d_q, d_k, d_v, d_f, d_g, ts=1024, tc=512):
    s = h_t.shape[1]
    ns = s // ts
    w_fg = 512 + W_F + W_G - 512
    rows = 128
    n_slot = (R_END // 4 + LANES - 1) // LANES * LANES
    order = [(0, 0, W_A), (4, 0, 512), (1, 0, 512), (2, 0, 512), (3, 0, 512), (4, 512, N_HEADS), (4, 512 + W_F, W_G - 512)]

    def slot_pieces(k):
        lo, hi, out, col = (R_END // 4) * k, (R_END // 4) * (k + 1), [], 0
        for acc_i, c0, w in order:
            a, b = max(lo, col), min(hi, col + w)
            if a < b:
                out.append((acc_i, c0 + a - col, b - a))
            col += w
        return out

    def body(h_ref, da_ref, dq_ref, dk_ref, dv_ref, df_ref, dg_ref, o_ref, acc_a, acc_q, acc_k, acc_v, acc_fg, stage, sem):
        k = pl.program_id(0)
        accs = (acc_a, acc_q, acc_k, acc_v, acc_fg)

        @pl.when(k == 0)
        def _():
            for acc in accs:
                acc[...] = jnp.zeros_like(acc)

        hv = h_ref[...]

        def add(acc, c_acc, d_ref, c_d, width):
            for c0 in range(0, width, tc):
                w = min(tc, width - c0)
                acc[:, c_acc + c0:c_acc + c0 + w] += _dot(hv, d_ref[:, c_d + c0:c_d + c0 + w])

        add(acc_a, 0, da_ref, 0, W_A)
        add(acc_q, 0, dq_ref, 0, 512)
        add(acc_k, 0, dk_ref, 0, 512)
        add(acc_v, 0, dv_ref, 0, 512)
        add(acc_fg, 0, dg_ref, 0, 512)
        add(acc_fg, 512, df_ref, 0, W_F)
        add(acc_fg, 512 + W_F, dg_ref, 512, W_G - 512)

        @pl.when(k == ns - 1)
        def _():
            pending = [None, None]
            for n, r0 in enumerate(range(0, D_MODEL, rows)):
                buf = n % 2
                if pending[buf] is not None:
                    for cp in pending[buf]:
                        cp.wait()
                rs = slice(r0, r0 + rows)
                copies = []
                for slot in range(4):
                    parts = [accs[acc_i][rs, c0:c0 + w] for acc_i, c0, w in slot_pieces(slot)]
                    parts.append(jnp.zeros((rows, n_slot - R_END // 4), F32))
                    stage[buf, slot] = jnp.concatenate(parts, axis=1)
                    cp = pltpu.make_async_copy(stage.at[buf, slot], o_ref.at[slot, pl.ds(r0, rows), :], sem.at[4 * buf + slot])
                    cp.start()
                    copies.append(cp)
                pending[buf] = copies
            for copies in pending:
                for cp in copies:
                    cp.wait()

    spec = lambda d: pl.BlockSpec((ts, d.shape[1]), lambda k: (k, 0))
    return pl.pallas_call(
        body, name="dw_in", grid=(ns,),
        out_shape=jax.ShapeDtypeStruct((4, D_MODEL, n_slot), F32),
        in_specs=[pl.BlockSpec((D_MODEL, ts), lambda k: (0, k))] + [spec(d) for d in (d_a, d_q, d_k, d_v, d_f, d_g)],
        out_specs=pl.BlockSpec(memory_space=pl.ANY),
        scratch_shapes=[pltpu.VMEM((D_MODEL, W_A), F32)] + [pltpu.VMEM((D_MODEL, 512), F32)] * 3
        + [pltpu.VMEM((D_MODEL, w_fg), F32), pltpu.VMEM((2, 4, rows, n_slot), F32), pltpu.SemaphoreType.DMA((8,))],
        compiler_params=_params(("arbitrary",), VMEM_LIMIT),
    )(h_t, d_a, d_q, d_k, d_v, d_f, d_g)


def _small_grads(packs, c_t, dada_shard):
    def body(p_ref, ct_ref, da_ref, sum_ref, gw_ref):
        acc = p_ref[0]
        for dev in range(1, 8):
            acc = acc + p_ref[dev]
        sum_ref[...] = acc
        gw_ref[...] = jnp.dot(ct_ref[...], da_ref[...], preferred_element_type=F32, precision=lax.Precision.HIGHEST)

    return pl.pallas_call(
        body, name="small_grads",
        out_shape=[jax.ShapeDtypeStruct(packs.shape[1:], F32),
                   jax.ShapeDtypeStruct((c_t.shape[0], dada_shard.shape[1]), F32)],
    )(packs, c_t, dada_shard)


def _adamw_body(w_ref, g_ref, m_ref, v_ref, d_ref, mo_ref, vo_ref):
    c1 = 1.0 / (1.0 - ADAM_B1 ** ADAM_STEP)
    c2 = 1.0 / (1.0 - ADAM_B2 ** ADAM_STEP)
    gv = g_ref[...]
    mn = ADAM_B1 * m_ref[...] + (1.0 - ADAM_B1) * gv
    vn = ADAM_B2 * v_ref[...] + (1.0 - ADAM_B2) * (gv * gv)
    mo_ref[...] = mn
    vo_ref[...] = vn
    d_ref[...] = -ADAM_LR * ((mn * c1) / (jnp.sqrt(vn * c2) + ADAM_EPS) + ADAM_WD * w_ref[...])


def _adamw3(w, g, m, v, name, tb=128):
    spec = pl.BlockSpec((tb, SUBLANES, LANES), lambda i: (i, 0, 0))
    return pl.pallas_call(
        functools.partial(_adamw_body), name=name, grid=(pl.cdiv(w.shape[0], tb),),
        out_shape=[jax.ShapeDtypeStruct(w.shape, F32)] * 3,
        in_specs=[spec] * 4, out_specs=[spec] * 3,
        compiler_params=_params(("parallel",)),
    )(w, g, m, v)


def _adamw_many(items, name):
    n = len(items)

    def body(*refs):
        for i in range(n):
            _adamw_body(*refs[4 * i:4 * i + 4], *refs[4 * n + 3 * i:4 * n + 3 * i + 3])

    return pl.pallas_call(
        body, name=name,
        out_shape=[jax.ShapeDtypeStruct(it[0].shape, F32) for it in items for _ in range(3)],
        compiler_params=_params(vmem=VMEM_LIMIT),
    )(*[arr for it in items for arr in it])


def _rope_inputs(positions):
    inv_freq = 10000.0 ** (-jnp.arange(0, HEAD_DIM, 2, dtype=F32) / HEAD_DIM)
    pos = jnp.broadcast_to(positions.astype(F32)[:, None], (positions.shape[0], LANES))
    return pos, jnp.tile(inv_freq, 4)[None, :]


def _pad_rows(v, rows=SUBLANES):
    return jnp.pad(v, ((0, rows - v.shape[0]), (0, 0)))


def kernel(x, c, positions, w_ada, b_ada, g_norm, w_in, b_f, sinks, w_o_swa, w_o_fox, w_out, g_final, loss_target, m_w_ada, m_b_ada, m_g_norm, m_w_in, m_b_f, m_sinks, m_w_o_swa, m_w_o_fox, m_w_out, m_g_final, v_w_ada, v_b_ada, v_g_norm, v_w_in, v_b_f, v_sinks, v_w_o_swa, v_w_o_fox, v_w_out, v_g_final):
    ix, iy, ic = lax.axis_index("x"), lax.axis_index("y"), lax.axis_index("c")
    chip = 2 * ix + iy
    dev = 2 * chip + ic
    xs, tgt = x[0], loss_target[0]
    s = xs.shape[0]

    b_ada_shard = lax.dynamic_slice(b_ada, (0, chip * 768), (1, 768))
    ada_parts, w_int, w_int_t, g_oa, g_ob, g_out = _gather_inputs(
        _pad_rows(c), w_ada[0], b_ada_shard, jnp.transpose(w_in, (2, 0, 1)).reshape(-1, LANES),
        [w_o_swa[0], w_o_fox[0], w_out[0]], "gather_inputs")
    ada = lax.dynamic_index_in_dim(ada_parts, dev, axis=1, keepdims=False).reshape(1, 3 * D_MODEL)
    shift, scale, gate = ada[:, :D_MODEL], ada[:, D_MODEL:2 * D_MODEL], ada[:, 2 * D_MODEL:]
    scale1 = 1.0 + scale

    wo_a = jnp.transpose(g_oa, (1, 0, 2)).reshape(512, D_MODEL)
    wo_b = jnp.transpose(g_ob, (1, 0, 2)).reshape(512, D_MODEL)
    w_o = g_out.reshape(D_MODEL, D_MODEL)

    pos, freq = _rope_inputs(positions[0])
    bf_pad = jnp.pad(b_f, ((0, 0), (0, LANES - N_HEADS)))
    sink_vec = sinks[0]

    a, vb, f, g, h_t, cos, sin, qa, ka, va, stats = _norm_proj(
        xs, g_norm * scale1, shift, w_int, pos, freq, bf_pad, FOX_TILE)
    att_a, l_swa = _swa_fwd(a, sink_vec)
    ranges = _fox_tile_ranges(stats)
    att_b, lse = _fox_fwd(qa, ka, va, ranges, FOX_TILE)

    dx2, datt_a, datt_b, d_g, delta8, dwo_a, dwo_b, dw_out, vec_mid = _mid(
        att_a, att_b, g, xs, tgt, gate, g_final.reshape(1, D_MODEL), wo_a, wo_b, w_o)
    delta = jnp.pad(delta8.reshape(4, 2, s), ((0, 0), (0, SUBLANES - 2), (0, 0)))
    d_a, dsink = _swa_bwd(a, datt_a, l_swa, sink_vec, cos, sin)
    dq, dk, dv, dcum_k, dcum_q = _fox_bwd(qa, ka, vb, datt_b, lse, delta, ranges, FOX_TILE)
    grad_x, vec_dh, d_f, dbf = _dh_norm_bwd(
        d_a, dq, dk, dv, dcum_k, dcum_q, f, bf_pad, d_g, w_int_t, xs, dx2, g_norm, scale1)
    dw_in_slots = _dw_in(h_t, d_a, dq, dk, dv, d_f, d_g)

    tail = jnp.pad(jnp.concatenate([dbf[0:1, :N_HEADS], dsink[0:1, :N_HEADS]], axis=1), ((0, 0), (0, D_MODEL - 2 * N_HEADS)))
    pack = jnp.concatenate([c, vec_dh[0:2], vec_mid[1:2], vec_dh[2:3], vec_mid[0:1], tail, vec_mid[2:3]], axis=0)

    def slots(w, axis):
        if axis == 1:
            return jnp.transpose(w.reshape(w.shape[0], 4, w.shape[1] // 4), (1, 0, 2))
        return w.reshape(4, w.shape[0] // 4, w.shape[1])

    packs, g_wo_a, g_wo_b, g_w_out, g_w_in = _reduce_scatter(
        [slots(dwo_a, 1), slots(dwo_b, 1), slots(dw_out, 0), dw_in_slots], pack, "reduce_grads")
    g_w_in = g_w_in[:, :w_in.shape[2]]
    dada_all = packs[:, 1:4, :].reshape(8, 3 * D_MODEL)
    dada_shard = lax.dynamic_slice(dada_all, (0, chip * 768), (8, 768))
    sums, g_w_ada = _small_grads(packs, packs[:, 0, :].T, dada_shard)
    g_b_ada = sums[1:4].reshape(1, 3 * D_MODEL)
    g_g_norm = sums[4:5]
    g_g_final = sums[5]
    g_b_f = sums[6:7, :N_HEADS]
    g_sinks = sums[6:7, N_HEADS:2 * N_HEADS]
    loss = sums[7, 0]

    grads = {
        "w_ada": g_w_ada, "b_ada": g_b_ada, "g_norm": g_g_norm, "w_in": g_w_in, "b_f": g_b_f, "sinks": g_sinks,
        "w_o_swa": g_wo_a, "w_o_fox": g_wo_b, "w_out": g_w_out, "g_final": g_g_final,
    }
    params = {
        "w_ada": (w_ada, m_w_ada, v_w_ada), "b_ada": (b_ada, m_b_ada, v_b_ada), "g_norm": (g_norm, m_g_norm, v_g_norm),
        "w_in": (w_in, m_w_in, v_w_in), "b_f": (b_f, m_b_f, v_b_f), "sinks": (sinks, m_sinks, v_sinks),
        "w_o_swa": (w_o_swa, m_w_o_swa, v_w_o_swa), "w_o_fox": (w_o_fox, m_w_o_fox, v_w_o_fox),
        "w_out": (w_out, m_w_out, v_w_out), "g_final": (g_final, m_g_final, v_g_final),
    }
    n_col = w_in.shape[2]

    def as_stored(t):
        return jnp.transpose(t, (2, 0, 1)).reshape(n_col, SUBLANES, LANES)

    def from_stored(t):
        return jnp.transpose(t, (1, 2, 0)).reshape(1, D_MODEL, n_col)

    names = list(grads)
    others = [nm for nm in names if nm != "w_in"]

    def as_2d(t):
        return t.reshape((t.shape[-2], t.shape[-1]) if t.ndim >= 2 else (1, t.shape[0]))

    flat = _adamw_many([tuple(as_2d(t) for t in (params[nm][0], grads[nm], params[nm][1], params[nm][2])) for nm in others],
                       "adamw_small")
    results = {}
    for i, nm in enumerate(others):
        shape = params[nm][0].shape
        results[nm] = [t.reshape(shape) for t in (grads[nm], *flat[3 * i:3 * i + 3])]
    w, m, v = params["w_in"]
    g_st = as_stored(grads["w_in"][None])
    d_, m_, v_ = _adamw3(as_stored(w), g_st, as_stored(m), as_stored(v), "adamw_w_in")
    results["w_in"] = [from_stored(t) for t in (g_st, d_, m_, v_)]
    return (loss, grad_x[None], *[results[nm][0] for nm in names], *[results[nm][1] for nm in names],
            *[results[nm][2] for nm in names], *[results[nm][3] for nm in names])
```

```python
import functools

import numpy as np
import jax
import jax.numpy as jnp
from jax import lax
from jax.experimental import pallas as pl
from jax.experimental.pallas import tpu as pltpu

F32 = jnp.float32
BF16 = jnp.bfloat16
MESH = pl.DeviceIdType.MESH

D_MODEL = 1024
HEAD_DIM = 64
N_HEADS = 8
WINDOW = 128
NORM_EPS = 1e-6
SCALE = HEAD_DIM ** -0.5
NEG = -1e30
LANES = 128
SUBLANES = 8
VMEM_LIMIT = 60 * 1024 * 1024
FOX_TILE = 512

W_A, W_B, W_F, W_G = 768, 1536, 128, 3072
OFF_A, OFF_B, OFF_F, OFF_G = 0, 768, 2304, 2432
W_INT = W_A + W_B + W_F + W_G
R_ZA, R_QB, R_FB, R_ZB, R_END = 768, 1280, 2816, 2824, 5384

ADAM_LR, ADAM_B1, ADAM_B2, ADAM_EPS, ADAM_WD, ADAM_STEP = 0.001, 0.9, 0.999, 1e-08, 0.01, 10

NT = (((1,), (1,)), ((), ()))
TN = (((0,), (0,)), ((), ()))


def _dot(a, b, dims=None):
    if dims is None:
        return jnp.dot(a, b, preferred_element_type=F32)
    return lax.dot_general(a, b, dims, preferred_element_type=F32)


def _split3(v):
    hi = v.astype(BF16)
    r1 = v - hi.astype(F32)
    mid = r1.astype(BF16)
    lo = (r1 - mid.astype(F32)).astype(BF16)
    return hi, mid, lo


def _sigmoid(v):
    return 1.0 / (1.0 + jnp.exp(-v))


def _params(sem=None, vmem=None):
    return pltpu.CompilerParams(dimension_semantics=sem, vmem_limit_bytes=vmem)


def _const_spec(shape):
    nd = len(shape)
    return pl.BlockSpec(shape, lambda *_: (0,) * nd, pipeline_mode=pl.Buffered(1))


def _flip(v, f):
    return 1 - v if f else v


_CHIP_FLIPS = ((1, 0), (0, 1), (1, 1))


def _gather_inputs(c_pad, w_ada, b_ada_shard, w_in_shard, small_shards, name):
    shards = [w_in_shard] + list(small_shards)
    n = len(shards)
    n_col = w_ada.shape[1]
    shard_w = w_in_shard.shape[0] // SUBLANES
    rows = 128

    def body(*refs):
        c_ref, wa_ref, ba_ref = refs[:3]
        ins = refs[3:3 + n]
        ada_ref, wint_ref, wintt_ref = refs[3 + n:6 + n]
        g_in, call_ref, send_sems, recv_sems = refs[5 + 2 * n:9 + 2 * n]
        outs = (g_in,) + tuple(refs[6 + n:5 + 2 * n])
        x, y, c = lax.axis_index("x"), lax.axis_index("y"), lax.axis_index("c")
        k_me = 2 * x + y
        me = 2 * k_me + c
        sibling = (x, y, 1 - c)
        chips = [(_flip(x, fx), _flip(y, fy)) for fx, fy in _CHIP_FLIPS]

        def piece(i, chip_k, half):
            hr = outs[i].shape[1] // 2
            return outs[i].at[chip_k, pl.ds(half * hr, hr), :]

        def copy(i, slot, chip_k, half, to):
            return pltpu.make_async_remote_copy(
                src_ref=piece(i, chip_k, half), dst_ref=piece(i, chip_k, half),
                send_sem=send_sems.at[6 * i + slot], recv_sem=recv_sems.at[6 * i + slot],
                device_id=to, device_id_type=MESH)

        def small(ref, slot, sem, to):
            return pltpu.make_async_remote_copy(
                src_ref=ref.at[slot], dst_ref=ref.at[slot], send_sem=send_sems.at[6 * n + sem],
                recv_sem=recv_sems.at[6 * n + sem], device_id=to, device_id_type=MESH)

        whole = shard_w // LANES * LANES
        for a in range(SUBLANES):
            main = ins[0][pl.ds(a, whole, stride=SUBLANES), :]
            tail = ins[0][pl.ds(a + SUBLANES * whole, shard_w - whole, stride=SUBLANES), :]
            tail = jnp.concatenate([tail, jnp.zeros((LANES - (shard_w - whole), LANES), F32)], axis=0)
            blk = jnp.concatenate([main.T, tail.T[:, :shard_w - whole]], axis=1)
            g_in[k_me, LANES * a:LANES * (a + 1), :] = blk.astype(BF16)
        for i in range(1, n):
            outs[i][k_me] = ins[i][...].astype(BF16)
        started = []
        for i in range(n):
            for j, chip in enumerate(chips):
                cp = copy(i, j, k_me, c, (chip[0], chip[1], c))
                cp.start()
                started.append(cp)

        call_ref[me] = c_ref[...]
        peers = [(_flip(x, k & 4), _flip(y, k & 2), _flip(c, k & 1)) for k in range(1, 8)]
        for k, peer in enumerate(peers):
            cp = small(call_ref, me, k, peer)
            cp.start()
            started.append(cp)
        for k, peer in enumerate(peers):
            small(call_ref, 4 * peer[0] + 2 * peer[1] + peer[2], k, peer).wait_recv()
        c_all = call_ref[:, 0, :].astype(BF16)
        ada_ref[k_me] = _dot(c_all, wa_ref[...].astype(BF16)) + ba_ref[...]
        for j, chip in enumerate(chips):
            cp = small(ada_ref, k_me, 7 + j, (chip[0], chip[1], c))
            cp.start()
            started.append(cp)

        for j, chip in enumerate(chips):
            chip_k = 2 * chip[0] + chip[1]
            for i in range(n):
                copy(i, j, chip_k, c, (chip[0], chip[1], c)).wait_recv()
                cp = copy(i, 3 + j, chip_k, c, sibling)
                cp.start()
                started.append(cp)
        for j, chip in enumerate(chips):
            chip_k = 2 * chip[0] + chip[1]
            small(ada_ref, chip_k, 7 + j, (chip[0], chip[1], c)).wait_recv()
            for i in range(n):
                copy(i, 3 + j, chip_k, 1 - c, sibling).wait_recv()
        for cp in started:
            cp.wait_send()

        def ref_cols(slots, a, b):
            runs = []
            for k in range(4):
                lo, hi = max(a, shard_w * k), min(b, shard_w * (k + 1))
                if lo < hi:
                    runs.append(slots[k][:, lo - shard_w * k:hi - shard_w * k])
            return runs

        for r0 in range(0, D_MODEL, rows):
            rs = slice(r0, r0 + rows)
            slots = [g_in[k, rs, :] for k in range(4)]
            row = jnp.concatenate(
                ref_cols(slots, 0, R_ZA) + ref_cols(slots, R_QB, R_FB) + ref_cols(slots, R_FB, R_ZB)
                + [jnp.zeros((rows, W_F - N_HEADS), BF16)] + ref_cols(slots, R_ZA, R_QB) + ref_cols(slots, R_ZB, R_END),
                axis=1)
            wint_ref[rs, :] = row
            wintt_ref[:, rs] = row.T

    vmem = pl.BlockSpec(memory_space=pltpu.VMEM)
    return pl.pallas_call(
        body, name=name,
        out_shape=[jax.ShapeDtypeStruct((4, 8, n_col), F32), jax.ShapeDtypeStruct((D_MODEL, W_INT), BF16),
                   jax.ShapeDtypeStruct((W_INT, D_MODEL), BF16)]
        + [jax.ShapeDtypeStruct((4,) + s.shape, BF16) for s in small_shards],
        in_specs=[vmem] * (3 + n),
        out_specs=[vmem] * (2 + n),
        scratch_shapes=[pltpu.VMEM((4, D_MODEL, shard_w), BF16), pltpu.VMEM((8,) + c_pad.shape, F32),
                        pltpu.SemaphoreType.DMA((6 * n + 10,)), pltpu.SemaphoreType.DMA((6 * n + 10,))],
        compiler_params=_params(vmem=VMEM_LIMIT),
    )(c_pad, w_ada, b_ada_shard, *shards)


def _reduce_scatter(pieces, pack, name):
    n = len(pieces)

    def body(*refs):
        pack_ref, ins = refs[0], refs[1:1 + n]
        packs_ref, outs = refs[1 + n], refs[2 + n:2 + 2 * n]
        rest = refs[2 + 2 * n:]
        own, got = rest[:n], rest[n:2 * n]
        sendb, recvb = rest[2 * n:3 * n], rest[3 * n:4 * n]
        send_sems, recv_sems, local_sems = rest[4 * n:4 * n + 3]
        x, y, c = lax.axis_index("x"), lax.axis_index("y"), lax.axis_index("c")
        k_me = 2 * x + y
        me = 2 * k_me + c
        sibling = (x, y, 1 - c)
        chips = [(_flip(x, fx), _flip(y, fy)) for fx, fy in _CHIP_FLIPS]
        hrs = [p.shape[1] // 2 for p in pieces]

        def remote(i, slot, src, dst, to):
            return pltpu.make_async_remote_copy(
                src_ref=src, dst_ref=dst, send_sem=send_sems.at[5 * i + slot], recv_sem=recv_sems.at[5 * i + slot],
                device_id=to, device_id_type=MESH)

        started = []
        packs_ref[me] = pack_ref[...]
        peers = [(_flip(x, k & 4), _flip(y, k & 2), _flip(c, k & 1)) for k in range(1, 8)]
        for k, peer in enumerate(peers):
            cp = pltpu.make_async_remote_copy(
                src_ref=pack_ref, dst_ref=packs_ref.at[me], send_sem=send_sems.at[5 * n + k],
                recv_sem=recv_sems.at[5 * n + k], device_id=peer, device_id_type=MESH)
            cp.start()
            started.append(cp)
        loads = []
        for i in range(n):
            ld = pltpu.make_async_copy(ins[i].at[:, pl.ds(c * hrs[i], hrs[i]), :], own[i], local_sems.at[i])
            ld.start()
            loads.append(ld)
            cp = remote(i, 0, ins[i].at[:, pl.ds((1 - c) * hrs[i], hrs[i]), :], got[i], sibling)
            cp.start()
            started.append(cp)
        for i in range(n):
            loads[i].wait()
            remote(i, 0, ins[i].at[:, pl.ds(c * hrs[i], hrs[i]), :], got[i], sibling).wait_recv()
            for j, chip in enumerate(chips):
                chip_k = 2 * chip[0] + chip[1]
                sendb[i][j] = (own[i][chip_k] + got[i][chip_k]).astype(BF16)
                cp = remote(i, 1 + j, sendb[i].at[j], recvb[i].at[j], (chip[0], chip[1], c))
                cp.start()
                started.append(cp)
        for i in range(n):
            acc = own[i][k_me] + got[i][k_me]
            for j, chip in enumerate(chips):
                remote(i, 1 + j, sendb[i].at[j], recvb[i].at[j], (chip[0], chip[1], c)).wait_recv()
                acc = acc + recvb[i][j].astype(F32)
            mine = outs[i].at[pl.ds(c * hrs[i], hrs[i]), :]
            outs[i][pl.ds(pl.multiple_of(c * hrs[i], SUBLANES), hrs[i]), :] = acc
            cp = remote(i, 4, mine, mine, sibling)
            cp.start()
            started.append(cp)
        for i in range(n):
            theirs = outs[i].at[pl.ds((1 - c) * hrs[i], hrs[i]), :]
            remote(i, 4, theirs, theirs, sibling).wait_recv()
        for k, peer in enumerate(peers):
            pltpu.make_async_remote_copy(
                src_ref=pack_ref, dst_ref=packs_ref.at[4 * peer[0] + 2 * peer[1] + peer[2]],
                send_sem=send_sems.at[5 * n + k], recv_sem=recv_sems.at[5 * n + k],
                device_id=peer, device_id_type=MESH).wait_recv()
        for cp in started:
            cp.wait_send()

    vmem = pl.BlockSpec(memory_space=pltpu.VMEM)
    scratch = []
    scratch += [pltpu.VMEM((4, p.shape[1] // 2, p.shape[2]), F32) for p in pieces]
    scratch += [pltpu.VMEM((4, p.shape[1] // 2, p.shape[2]), F32) for p in pieces]
    scratch += [pltpu.VMEM((3, p.shape[1] // 2, p.shape[2]), BF16) for p in pieces]
    scratch += [pltpu.VMEM((3, p.shape[1] // 2, p.shape[2]), BF16) for p in pieces]
    scratch += [pltpu.SemaphoreType.DMA((5 * n + 7,)), pltpu.SemaphoreType.DMA((5 * n + 7,)), pltpu.SemaphoreType.DMA((n,))]
    return pl.pallas_call(
        body, name=name,
        out_shape=[jax.ShapeDtypeStruct((8,) + pack.shape, F32)] + [jax.ShapeDtypeStruct(p.shape[1:], F32) for p in pieces],
        in_specs=[vmem] + [pl.BlockSpec(memory_space=pl.ANY)] * n,
        out_specs=[vmem] * (1 + n),
        scratch_shapes=scratch,
        compiler_params=_params(vmem=VMEM_LIMIT),
    )(pack, *pieces)


def _rope_fwd(t, cos, sin, lane):
    lo = (lane % HEAD_DIM) < (HEAD_DIM // 2)
    return t * cos + jnp.where(lo, -pltpu.roll(t, 96, 1), pltpu.roll(t, 32, 1)) * sin


def _norm_proj(x, gmod, shift, w_int, pos, freq, bf_pad, tm):
    s = x.shape[0]

    def body(x_ref, g_ref, sh_ref, w_ref, pos_ref, fr_ref, bf_ref,
             a_ref, vb_ref, f_ref, gg_ref, ht_ref, cos_ref, sin_ref, q_ref, k_ref, v_ref, st_ref, carry):
        @pl.when(pl.program_id(0) == 0)
        def _():
            carry[...] = jnp.zeros_like(carry)

        xv = x_ref[...]
        r = lax.rsqrt(jnp.mean(xv * xv, axis=-1, keepdims=True) + NORM_EPS)
        hf = (xv * r) * g_ref[...] + sh_ref[...]
        hb = hf.astype(BF16)
        ht_ref[...] = hb.T
        pf = _dot(hb, w_ref[:, OFF_F:OFF_F + W_F])
        f_ref[...] = pf
        cumv = _cumsum_tile(pf, bf_ref[...], carry)
        bblk = _dot(hb, w_ref[:, OFF_B:OFF_B + W_B]).astype(BF16)
        vb_ref[...] = bblk[:, 1024:1536]
        pa = _dot(hb, w_ref[:, OFF_A:OFF_A + W_A])
        gg_ref[...] = _dot(hb, w_ref[:, OFF_G:OFF_G + W_G]).astype(BF16)
        _augment_heads(bblk, cumv, q_ref, k_ref, v_ref, st_ref)
        ang = pos_ref[...] * fr_ref[...]
        cosv, sinv = jnp.cos(ang), jnp.sin(ang)
        cos_ref[...] = cosv
        sin_ref[...] = sinv
        lane = lax.broadcasted_iota(jnp.int32, (tm, LANES), 1)
        for j in range(5):
            t = pa[:, LANES * j:LANES * (j + 1)]
            a_ref[:, LANES * j:LANES * (j + 1)] = _rope_fwd(t, cosv, sinv, lane).astype(BF16)
        a_ref[:, 640:768] = pa[:, 640:768].astype(BF16)

    row = lambda w: pl.BlockSpec((tm, w), lambda i: (i, 0))
    return pl.pallas_call(
        body, name="norm_proj", grid=(s // tm,),
        out_shape=[jax.ShapeDtypeStruct((s, W_A), BF16), jax.ShapeDtypeStruct((s, 512), BF16),
                   jax.ShapeDtypeStruct((s, W_F), F32), jax.ShapeDtypeStruct((s, W_G), BF16),
                   jax.ShapeDtypeStruct((D_MODEL, s), BF16),
                   jax.ShapeDtypeStruct((s, LANES), F32), jax.ShapeDtypeStruct((s, LANES), F32)]
        + [jax.ShapeDtypeStruct((s, 1024), BF16)] * 3 + [jax.ShapeDtypeStruct((s // tm, SUBLANES, LANES), F32)],
        in_specs=[row(D_MODEL), _const_spec((1, D_MODEL)), _const_spec((1, D_MODEL)), _const_spec((D_MODEL, W_INT)),
                  row(LANES), _const_spec((1, LANES)), _const_spec((1, LANES))],
        out_specs=[row(W_A), row(512), row(W_F), row(W_G), pl.BlockSpec((D_MODEL, tm), lambda i: (0, i)),
                   row(LANES), row(LANES), row(1024), row(1024), row(1024),
                   pl.BlockSpec((1, SUBLANES, LANES), lambda i: (i, 0, 0))],
        scratch_shapes=[pltpu.VMEM((SUBLANES, LANES), F32)],
        compiler_params=_params(("arbitrary",), VMEM_LIMIT),
    )(x, gmod, shift, w_int, pos, freq, bf_pad)


def _log_sigmoid(u):
    return jnp.minimum(u, 0.0) - jnp.log(1.0 + jnp.exp(-jnp.abs(u)))


def _cumsum_tile(f, b_f, carry):
    tb = f.shape[0]
    lane = lax.broadcasted_iota(jnp.int32, (tb, LANES), 1)
    logf = jnp.where(lane < N_HEADS, _log_sigmoid(f + b_f), 0.0)
    hi, mid, lo = _split3(logf)
    rows = lax.broadcasted_iota(jnp.int32, (tb, tb), 0)
    cols = lax.broadcasted_iota(jnp.int32, (tb, tb), 1)
    tril = (cols <= rows).astype(BF16)
    cum = _dot(tril, hi) + _dot(tril, mid) + _dot(tril, lo) + carry[0:1, :]
    carry[...] = jnp.broadcast_to(cum[tb - 1:tb, :], carry.shape)
    return cum


def _aug_lane(h):
    return 64 if h % 2 == 0 else 0


def _augment_heads(bblk, cumv, q_ref, k_ref, v_ref, st_ref):
    t = bblk.shape[0]
    lane = lax.broadcasted_iota(jnp.int32, (t, LANES), 1)
    lane_b = lane.astype(BF16)
    sub8 = lax.broadcasted_iota(jnp.int32, (SUBLANES, LANES), 0)
    lane8 = lax.broadcasted_iota(jnp.int32, (SUBLANES, LANES), 1)
    one = jnp.ones((t, LANES), BF16)
    zero = jnp.zeros((t, LANES), BF16)
    stats = jnp.zeros((SUBLANES, LANES), F32)
    for p in range(4):
        qblk = bblk[:, LANES * p:LANES * (p + 1)] * SCALE
        kblk = bblk[:, 512 + LANES * p:512 + LANES * (p + 1)]
        vblk = bblk[:, 1024 + LANES * p:1024 + LANES * (p + 1)]
        qf, kf = qblk.astype(F32), kblk.astype(F32)
        q2, k2, qk = qf * qf, kf * kf, qf * kf
        for odd in range(2):
            h = 2 * p + odd
            a0 = _aug_lane(h)
            data_b = (lane_b < 64) if odd == 0 else (lane_b >= 64)
            data = (lane < 64) if odd == 0 else (lane >= 64)
            hi, mid, lo = _split3(jnp.broadcast_to(cumv[:, h:h + 1], (t, LANES)))
            ones3_q = (lane_b >= a0 + 3) & (lane_b < a0 + 6)
            ones3_k = (lane_b >= a0) & (lane_b < a0 + 3)
            aug_q = jnp.where(lane_b == a0, hi, jnp.where(lane_b == a0 + 1, mid, jnp.where(
                lane_b == a0 + 2, lo, jnp.where(ones3_q, one, zero))))
            aug_k = jnp.where(ones3_k, one, jnp.where(lane_b == a0 + 3, -hi, jnp.where(
                lane_b == a0 + 4, -mid, jnp.where(lane_b == a0 + 5, -lo, zero))))
            q_ref[:, LANES * h:LANES * (h + 1)] = jnp.where(data_b, qblk, aug_q)
            k_ref[:, LANES * h:LANES * (h + 1)] = jnp.where(data_b, kblk, aug_k)
            v_ref[:, LANES * h:LANES * (h + 1)] = jnp.where(data_b, vblk, jnp.where(lane_b == a0, one, zero))
            qn = jnp.sqrt(jnp.max(jnp.sum(jnp.where(data, q2, 0.0), axis=-1, keepdims=True)))
            kn = jnp.sqrt(jnp.max(jnp.sum(jnp.where(data, k2, 0.0), axis=-1, keepdims=True)))
            dmin = jnp.min(jnp.sum(jnp.where(data, qk, 0.0), axis=-1, keepdims=True))
            c_first, c_last = cumv[0:1, h:h + 1], cumv[t - 1:t, h:h + 1]
            row = jnp.where(lane8 == 0, qn, jnp.where(lane8 == 1, kn, jnp.where(
                lane8 == 2, c_first, jnp.where(lane8 == 3, c_last, jnp.where(lane8 == 4, dmin, 0.0)))))
            stats = jnp.where(sub8 == h, row, stats)
    st_ref[0] = stats


PRUNE_MARGIN = 88.0


def _fox_tile_ranges(stats):
    nt = stats.shape[0]
    qn, kn, c_first, c_last, d_min = (stats[:, :, n] for n in range(5))
    bound = (1.01 * qn[:, None, :] * kn[None, :, :] - jnp.minimum(d_min, 0.0)[:, None, :] + 0.05
             + c_first[:, None, :] - c_last[None, :, :])
    idx = jnp.arange(nt)
    skip = (bound <= -PRUNE_MARGIN) & (idx[None, :, None] < idx[:, None, None])
    first_key = jnp.sum(jnp.cumprod(skip, axis=1), axis=1)
    needed = (idx[None, :, None] >= first_key[:, None, :]) & (idx[None, :, None] <= idx[:, None, None])
    last_query = jnp.max(jnp.where(needed, idx[:, None, None], 0), axis=0)
    n_query = last_query - idx[:, None] + 1
    table = jnp.zeros((4, SUBLANES, LANES), F32)
    for odd in range(2):
        table = table.at[:, odd, :nt].set(first_key[:, odd::2].T.astype(F32))
        table = table.at[:, 2 + odd, :nt].set(n_query[:, odd::2].T.astype(F32))
    return table


def _lane_scalar(block, row, lane_idx):
    sub8 = lax.broadcasted_iota(jnp.int32, (SUBLANES, LANES), 0)
    lane8 = lax.broadcasted_iota(jnp.int32, (SUBLANES, LANES), 1)
    return jnp.sum(jnp.where((sub8 == row) & (lane8 == lane_idx), block, 0.0)).astype(jnp.int32)


def _fox_fwd(qa, ka, va, ranges, t):
    s = qa.shape[0]
    nt = s // t
    nc = t // LANES

    def body(rg_ref, q_ref, k_ref, v_ref, o_ref, lse_ref):
        i = pl.program_id(1)
        lane = lax.broadcasted_iota(jnp.int32, (t, LANES), 1)
        rows = lax.broadcasted_iota(jnp.int32, (t, t), 0)
        cols = lax.broadcasted_iota(jnp.int32, (t, t), 1)
        firsts = [jnp.clip(_lane_scalar(rg_ref[0], hh, i), 0, i) for hh in range(2)]
        first = jnp.maximum(firsts[0], firsts[1])

        def update(js, carry, heads=(0, 1), diagonal=False):
            offs = [pl.multiple_of(j * t, t) for j in js]
            kts = [k_ref[pl.ds(off, t), :] for off in offs]
            vts = [v_ref[pl.ds(off, t), :] for off in offs]
            scs = {hh: [_dot(q_ref[:, LANES * hh:LANES * (hh + 1)], kt[:, LANES * hh:LANES * (hh + 1)], NT) for kt in kts]
                   for hh in heads}
            if diagonal:
                scs = {hh: [jnp.where(cols <= rows, sc, NEG) for sc in scs[hh]] for hh in heads}
            m_new = {}
            for hh in heads:
                part = None
                for sc in scs[hh]:
                    for cch in range(nc):
                        chunk = sc[:, LANES * cch:LANES * (cch + 1)]
                        part = chunk if part is None else jnp.maximum(part, chunk)
                m_new[hh] = jnp.maximum(carry[2 * hh], jnp.max(part, axis=-1, keepdims=True))
            alphas = {hh: jnp.exp(carry[2 * hh] - m_new[hh]) for hh in heads}
            ps = {hh: [jnp.exp(sc - m_new[hh]).astype(BF16) for sc in scs[hh]] for hh in heads}
            out = list(carry)
            for hh in heads:
                pv = None
                for p, vt in zip(ps[hh], vts):
                    term = _dot(p, vt[:, LANES * hh:LANES * (hh + 1)])
                    pv = term if pv is None else pv + term
                out[2 * hh], out[2 * hh + 1] = m_new[hh], alphas[hh] * carry[2 * hh + 1] + pv
            return tuple(out)

        col0 = jnp.full((t, 1), NEG, F32)
        zero = jnp.zeros((t, LANES), F32)
        carry = (col0, zero, col0, zero)
        for hh in range(2):
            carry = lax.fori_loop(firsts[hh], first, lambda j, cr, hh=hh: update([j], cr, heads=(hh,)), carry)
        n_off = i - first
        carry = lax.fori_loop(0, n_off // 2, lambda u, cr: update([first + 2 * u, first + 2 * u + 1], cr), carry)
        carry = lax.fori_loop(0, n_off % 2, lambda u, cr: update([i - 1], cr), carry)
        m0, acc0, m1, acc1 = update([i], carry, diagonal=True)
        l0, l1 = acc0[:, _aug_lane(0):_aug_lane(0) + 1], acc1[:, _aug_lane(1):_aug_lane(1) + 1]
        o_ref[...] = jnp.where(lane < 64, acc0 * (1.0 / l0), acc1 * (1.0 / l1)).astype(BF16)
        sub = lax.broadcasted_iota(jnp.int32, (SUBLANES, t), 0)
        lse0 = jnp.broadcast_to(m0 + jnp.log(l0), (t, LANES)).T[0:SUBLANES, :]
        lse1 = jnp.broadcast_to(m1 + jnp.log(l1), (t, LANES)).T[0:SUBLANES, :]
        lse_ref[0] = jnp.where(sub == 0, lse0, jnp.where(sub == 1, lse1, 0.0))

    pair = pl.BlockSpec((s, 2 * LANES), lambda p, i: (0, p))
    return pl.pallas_call(
        body, name="fox_fwd", grid=(4, nt),
        out_shape=[jax.ShapeDtypeStruct((s, 512), BF16), jax.ShapeDtypeStruct((4, SUBLANES, s), F32)],
        in_specs=[pl.BlockSpec((1, SUBLANES, LANES), lambda p, i: (p, 0, 0)),
                  pl.BlockSpec((t, 2 * LANES), lambda p, i: (i, p)), pair, pair],
        out_specs=[pl.BlockSpec((t, LANES), lambda p, i: (i, p)),
                   pl.BlockSpec((1, SUBLANES, t), lambda p, i: (p, 0, i))],
        compiler_params=_params(("parallel", "arbitrary"), VMEM_LIMIT),
    )(ranges, qa, ka, va)


def _dup_halves(blk, lane):
    f = blk.astype(F32)
    r = pltpu.roll(f, 64, 1)
    return jnp.where(lane < 64, f, r).astype(BF16), jnp.where(lane >= 64, f, r).astype(BF16)


GROUP = 4
GROUP_ROWS = GROUP * WINDOW


def _stack_heads(ref, g, lane):
    parts = []
    for pb in (2 * g, 2 * g + 1):
        blk = ref[:, LANES * pb:LANES * (pb + 1)]
        zero = jnp.zeros_like(blk)
        parts += [jnp.where(lane < 64, blk, zero), jnp.where(lane >= 64, blk, zero)]
    return jnp.concatenate(parts, axis=0)


def _swa_band(a_ref, ap_ref, g, lane):
    k = jnp.concatenate([_dup_halves(ap_ref[:, 512:640], lane)[g], _dup_halves(a_ref[:, 512:640], lane)[g]], axis=0)
    v = jnp.concatenate([_dup_halves(ap_ref[:, 640:768], lane)[g], _dup_halves(a_ref[:, 640:768], lane)[g]], axis=0)
    return k, v


def _swa_logits(q, k, has_prev):
    sc = _dot(q, k, NT) * SCALE
    rr = lax.broadcasted_iota(jnp.int32, sc.shape, 0) % WINDOW
    cc = lax.broadcasted_iota(jnp.int32, sc.shape, 1)
    valid = (cc > rr) & (cc <= rr + WINDOW) & (has_prev | (cc >= WINDOW))
    return jnp.where(valid, sc, NEG)


def _per_head_column(values):
    return jnp.concatenate([jnp.broadcast_to(v, (WINDOW, 1)) for v in values], axis=0)


SWA_BLOCKS = 4
SWA_ROWS = SWA_BLOCKS * WINDOW
SWA_FWD_BLOCKS = 8


def _swa_blocks(a_ref, ap_ref, n_blocks):
    return [ap_ref] + [a_ref.at[pl.ds(WINDOW * jb, WINDOW), :] for jb in range(n_blocks)]


def _swa_fwd(a, sinks):
    s = a.shape[0]
    n_blocks = SWA_FWD_BLOCKS
    n_rows = n_blocks * WINDOW

    def body(sink_ref, a_ref, ap_ref, o_ref, l_ref):
        lane = lax.broadcasted_iota(jnp.int32, (WINDOW, LANES), 1)
        blocks = _swa_blocks(a_ref, ap_ref, n_blocks)
        units = [(jb, g) for jb in range(n_blocks) for g in range(2)]
        sinks_col = [_per_head_column([sink_ref[GROUP * g + hh] for hh in range(GROUP)]) for g in range(2)]
        bands = [_swa_band(blocks[jb + 1], blocks[jb], g, lane) for jb, g in units]
        scs = [_swa_logits(_stack_heads(blocks[jb + 1], g, lane), bands[u][0],
                           (pl.program_id(0) > 0) if jb == 0 else True) for u, (jb, g) in enumerate(units)]
        ms = [jnp.maximum(jnp.max(scs[u], axis=-1, keepdims=True), sinks_col[g]) for u, (jb, g) in enumerate(units)]
        ps = [jnp.exp(scs[u] - ms[u]) for u in range(len(units))]
        dens = [jnp.sum(ps[u], axis=-1, keepdims=True) + jnp.exp(sinks_col[g] - ms[u]) for u, (jb, g) in enumerate(units)]
        outs = [_dot((ps[u] * (1.0 / dens[u])).astype(BF16), bands[u][1]) for u in range(len(units))]
        for jb in range(n_blocks):
            rows = slice(WINDOW * jb, WINDOW * (jb + 1))
            l_all = jnp.zeros((WINDOW, LANES), F32)
            for g in range(2):
                u = 2 * jb + g
                lcol = ms[u] + jnp.log(dens[u])
                for pb in range(2):
                    r0 = 2 * pb * WINDOW
                    o_ref[rows, LANES * (2 * g + pb):LANES * (2 * g + pb + 1)] = jnp.where(
                        lane < 64, outs[u][r0:r0 + WINDOW], outs[u][r0 + WINDOW:r0 + 2 * WINDOW]).astype(BF16)
                for hh in range(GROUP):
                    l_all = jnp.where(lane == GROUP * g + hh, lcol[WINDOW * hh:WINDOW * (hh + 1)], l_all)
            l_ref[rows, :] = l_all

    return pl.pallas_call(
        body, name="swa_fwd", grid=(s // n_rows,),
        out_shape=[jax.ShapeDtypeStruct((s, 512), BF16), jax.ShapeDtypeStruct((s, LANES), F32)],
        in_specs=[pl.BlockSpec(memory_space=pltpu.SMEM),
                  pl.BlockSpec((n_rows, W_A), lambda i: (i, 0)),
                  pl.BlockSpec((WINDOW, W_A), lambda i: (jnp.maximum(n_blocks * i - 1, 0), 0))],
        out_specs=[pl.BlockSpec((n_rows, 512), lambda i: (i, 0)), pl.BlockSpec((n_rows, LANES), lambda i: (i, 0))],
        compiler_params=_params(("parallel",)),
    )(sinks, a, a)


def _mid(att_a, att_b, g, x, target, gate, g_final, wo_a, wo_b, w_out, tm=256):
    s = x.shape[0]
    nt = s // tm

    def body(aa_ref, ab_ref, g_ref, x_ref, t_ref, gate_ref, gf_ref, woa_ref, wob_ref, wout_ref,
             dx_ref, daa_ref, dab_ref, dg_ref, delta_ref, dwoa_ref, dwob_ref, dwout_ref, vec_ref,
             acc_gf, acc_gate, acc_loss):
        step = pl.program_id(0)

        @pl.when(step == 0)
        def _():
            dwoa_ref[...] = jnp.zeros_like(dwoa_ref)
            dwob_ref[...] = jnp.zeros_like(dwob_ref)
            dwout_ref[...] = jnp.zeros_like(dwout_ref)
            acc_gf[...] = jnp.zeros_like(acc_gf)
            acc_gate[...] = jnp.zeros_like(acc_gate)
            acc_loss[...] = jnp.zeros_like(acc_loss)

        def fold(v):
            return jnp.sum(v.reshape(v.shape[0] // SUBLANES, SUBLANES, D_MODEL), axis=0)

        gate = gate_ref[...]
        gfin = gf_ref[...]
        branches = []
        for att_ref, z_off, wo_ref in ((aa_ref, 0, woa_ref), (ab_ref, 512, wob_ref)):
            att = att_ref[...].astype(F32)
            z = g_ref[:, z_off:z_off + 512].astype(F32)
            sz = _sigmoid(z)
            silu = z * sz
            u = (att * silu).astype(BF16)
            branches.append((att, z, sz, silu, u, _dot(u, wo_ref[...])))
        ga = g_ref[:, 1024:2048].astype(F32)
        gb = g_ref[:, 2048:3072].astype(F32)
        sga, sgb = _sigmoid(ga), _sigmoid(gb)
        y_a, y_b = branches[0][5], branches[1][5]
        mb = (sga * y_a + sgb * y_b).astype(BF16)
        o = _dot(mb, wout_ref[...])
        d_o_parts = []
        for r0 in range(0, tm, tm // 2):
            rs = slice(r0, r0 + tm // 2)
            o_h = o[rs]
            x2 = x_ref[rs, :] + gate * o_h
            r2 = lax.rsqrt(jnp.mean(x2 * x2, axis=-1, keepdims=True) + NORM_EPS)
            xn2 = x2 * r2
            err = xn2 * gfin - t_ref[rs, :]
            acc_loss[...] += fold(err * err)
            dy = err * (1.0 / D_MODEL)
            acc_gf[...] += fold(dy * xn2)
            dxn = dy * gfin
            dx2 = r2 * (dxn - xn2 * jnp.mean(dxn * xn2, axis=-1, keepdims=True))
            dx_ref[rs, :] = dx2
            acc_gate[...] += fold(dx2 * o_h)
            d_o_parts.append((dx2 * gate).astype(BF16))
        d_o = jnp.concatenate(d_o_parts, axis=0)
        dm = _dot(d_o, wout_ref[...], NT)
        dwout_ref[...] += _dot(mb, d_o, TN)
        dg_ref[:, 1024:2048] = (dm * y_a * sga * (1.0 - sga)).astype(BF16)
        dg_ref[:, 2048:3072] = (dm * y_b * sgb * (1.0 - sgb)).astype(BF16)
        dybs = [(dm * sg).astype(BF16) for sg in (sga, sgb)]
        dus = [_dot(dyb, wo_ref[...], NT) for dyb, wo_ref in zip(dybs, (woa_ref, wob_ref))]
        for branch, dyb, dwo_ref in zip(branches, dybs, (dwoa_ref, dwob_ref)):
            dwo_ref[...] += _dot(branch[4], dyb, TN)
        for (att, z, sz, silu, u, _), du, datt_ref, z_off in (
                (branches[0], dus[0], daa_ref, 0), (branches[1], dus[1], dab_ref, 512)):
            datt = du * silu
            datt_ref[...] = datt.astype(BF16)
            dg_ref[:, z_off:z_off + 512] = (du * att * (sz * (1.0 + z * (1.0 - sz)))).astype(BF16)
            if z_off == 512:
                prod = datt * att
                hi = prod.astype(BF16)
                lo = (prod - hi.astype(F32)).astype(BF16)
                er = lax.broadcasted_iota(jnp.int32, (512, LANES), 0)
                ec = lax.broadcasted_iota(jnp.int32, (512, LANES), 1)
                e = (er // HEAD_DIM == ec).astype(BF16)
                delta = _dot(hi, e) + _dot(lo, e)
                delta_ref[...] = delta.T[0:SUBLANES, :]

        @pl.when(step == nt - 1)
        def _():
            sub = lax.broadcasted_iota(jnp.int32, (SUBLANES, D_MODEL), 0)
            dgf = jnp.sum(acc_gf[...], axis=0, keepdims=True)
            dgate = jnp.sum(acc_gate[...], axis=0, keepdims=True)
            loss = 0.5 * jnp.sum(acc_loss[...]) * (1.0 / D_MODEL)
            vec_ref[...] = jnp.where(sub == 0, dgf, jnp.where(sub == 1, dgate, jnp.where(sub == 2, loss, 0.0)))

    row = lambda w: pl.BlockSpec((tm, w), lambda i: (i, 0))
    return pl.pallas_call(
        body, name="mid", grid=(nt,),
        out_shape=[jax.ShapeDtypeStruct((s, D_MODEL), F32), jax.ShapeDtypeStruct((s, 512), BF16),
                   jax.ShapeDtypeStruct((s, 512), BF16), jax.ShapeDtypeStruct((s, W_G), BF16),
                   jax.ShapeDtypeStruct((SUBLANES, s), F32),
                   jax.ShapeDtypeStruct((512, D_MODEL), F32), jax.ShapeDtypeStruct((512, D_MODEL), F32),
                   jax.ShapeDtypeStruct((D_MODEL, D_MODEL), F32), jax.ShapeDtypeStruct((SUBLANES, D_MODEL), F32)],
        in_specs=[row(512), row(512), row(W_G), row(D_MODEL), row(D_MODEL),
                  _const_spec((1, D_MODEL)), _const_spec((1, D_MODEL)),
                  _const_spec((512, D_MODEL)), _const_spec((512, D_MODEL)), _const_spec((D_MODEL, D_MODEL))],
        out_specs=[row(D_MODEL), row(512), row(512), row(W_G),
                   pl.BlockSpec((SUBLANES, tm), lambda i: (0, i)),
                   pl.BlockSpec((512, D_MODEL), lambda i: (0, 0)), pl.BlockSpec((512, D_MODEL), lambda i: (0, 0)),
                   pl.BlockSpec((D_MODEL, D_MODEL), lambda i: (0, 0)), pl.BlockSpec((SUBLANES, D_MODEL), lambda i: (0, 0))],
        scratch_shapes=[pltpu.VMEM((SUBLANES, D_MODEL), F32)] * 3,
        compiler_params=_params(("arbitrary",), VMEM_LIMIT),
    )(att_a, att_b, g, x, target, gate, g_final, wo_a, wo_b, w_out)


def _rope_bwd(dt, cos, sin, lane):
    u = dt * sin
    lo = (lane % HEAD_DIM) < (HEAD_DIM // 2)
    return dt * cos + jnp.where(lo, pltpu.roll(u, 96, 1), -pltpu.roll(u, 32, 1))


def _swa_bwd(a, datt, l_all, sinks, cos, sin):
    s = a.shape[0]
    nt = s // SWA_ROWS

    def body(sink_ref, a_ref, ap_ref, do_ref, l_ref, cos_ref, sin_ref, da_ref, ds_ref, halo):
        step = pl.program_id(0)
        tile = nt - 1 - step

        @pl.when(step == 0)
        def _():
            halo[...] = jnp.zeros_like(halo)
            ds_ref[...] = jnp.zeros_like(ds_ref)

        lane = lax.broadcasted_iota(jnp.int32, (WINDOW, LANES), 1)
        sub8 = lax.broadcasted_iota(jnp.int32, (SUBLANES, LANES), 0)
        lane8 = lax.broadcasted_iota(jnp.int32, (SUBLANES, LANES), 1)
        blocks = _swa_blocks(a_ref, ap_ref, SWA_BLOCKS)
        dsink = jnp.zeros((SUBLANES, LANES), F32)

        def join(pair, r0):
            x0, x1 = pair[0][r0:r0 + WINDOW], pair[1][r0:r0 + WINDOW]
            return jnp.where(lane < 64, x0 + pltpu.roll(x0, 64, 1), x1 + pltpu.roll(x1, 64, 1))

        units = [(jb, g) for jb in range(SWA_BLOCKS) for g in range(2)]
        n_u = len(units)
        sinks_col = [_per_head_column([sink_ref[GROUP * g + hh] for hh in range(GROUP)]) for g in range(2)]
        bands = [_swa_band(blocks[jb + 1], blocks[jb], g, lane) for jb, g in units]
        qs = [_stack_heads(blocks[jb + 1], g, lane) for jb, g in units]
        doms = [_stack_heads(do_ref.at[pl.ds(WINDOW * jb, WINDOW), :], g, lane) for jb, g in units]
        lcols = []
        for jb, g in units:
            lv = l_ref[WINDOW * jb:WINDOW * (jb + 1), :]
            lcols.append(_per_head_column([lv[:, GROUP * g + hh:GROUP * g + hh + 1] for hh in range(GROUP)]))
        ps = [jnp.exp(_swa_logits(qs[u], bands[u][0], (tile > 0) if jb == 0 else True) - lcols[u])
              for u, (jb, g) in enumerate(units)]
        dps = [_dot(doms[u], bands[u][1], NT) for u in range(n_u)]
        deltas = [jnp.sum(ps[u] * dps[u], axis=-1, keepdims=True) for u in range(n_u)]
        for u, (jb, g) in enumerate(units):
            sink_term = jnp.exp(sinks_col[g] - lcols[u]) * deltas[u]
            for hh in range(GROUP):
                tot = jnp.sum(sink_term[WINDOW * hh:WINDOW * (hh + 1)])
                dsink = dsink + jnp.where((sub8 == 0) & (lane8 == GROUP * g + hh), -tot, 0.0)
        dss = [(ps[u] * (dps[u] - deltas[u])).astype(BF16) for u in range(n_u)]
        dqs = [_dot(dss[u], bands[u][0]) * SCALE for u in range(n_u)]
        dks = [_dot(dss[u], qs[u], TN) * SCALE for u in range(n_u)]
        dvs = [_dot(ps[u].astype(BF16), doms[u], TN) for u in range(n_u)]

        carry_k, carry_v = halo[:, 0:LANES], halo[:, LANES:2 * LANES]
        for jb in reversed(range(SWA_BLOCKS)):
            rows = slice(WINDOW * jb, WINDOW * (jb + 1))
            cosv, sinv = cos_ref[rows, :], sin_ref[rows, :]
            for g in range(2):
                dq = dqs[2 * jb + g]
                for pb in range(2):
                    r0 = 2 * pb * WINDOW
                    dq_pair = jnp.where(lane < 64, dq[r0:r0 + WINDOW], dq[r0 + WINDOW:r0 + 2 * WINDOW])
                    da_ref[rows, LANES * (2 * g + pb):LANES * (2 * g + pb + 1)] = _rope_bwd(
                        dq_pair, cosv, sinv, lane).astype(BF16)
            dkb, dvb = dks[2 * jb:2 * jb + 2], dvs[2 * jb:2 * jb + 2]
            da_ref[rows, 512:640] = _rope_bwd(join(dkb, WINDOW) + carry_k, cosv, sinv, lane).astype(BF16)
            da_ref[rows, 640:768] = (join(dvb, WINDOW) + carry_v).astype(BF16)
            carry_k, carry_v = join(dkb, 0), join(dvb, 0)
        halo[:, 0:LANES] = carry_k
        halo[:, LANES:2 * LANES] = carry_v
        ds_ref[...] += dsink

    rev = lambda w: pl.BlockSpec((SWA_ROWS, w), lambda i: (nt - 1 - i, 0))
    return pl.pallas_call(
        body, name="swa_bwd", grid=(nt,),
        out_shape=[jax.ShapeDtypeStruct((s, W_A), BF16), jax.ShapeDtypeStruct((SUBLANES, LANES), F32)],
        in_specs=[pl.BlockSpec(memory_space=pltpu.SMEM), rev(W_A),
                  pl.BlockSpec((WINDOW, W_A), lambda i: (jnp.maximum(SWA_BLOCKS * (nt - 1 - i) - 1, 0), 0)),
                  rev(512), rev(LANES), rev(LANES), rev(LANES)],
        out_specs=[rev(W_A), pl.BlockSpec((SUBLANES, LANES), lambda i: (0, 0))],
        scratch_shapes=[pltpu.VMEM((WINDOW, 2 * LANES), F32)],
        compiler_params=_params(("arbitrary",)),
    )(sinks, a, a, datt, l_all, cos, sin)


def _fox_bwd(qa, ka, vb, do, lse, delta, ranges, t):
    s = qa.shape[0]
    nt = s // t

    def body(rg_ref, q_ref, do_ref, lse_ref, dl_ref, k_ref, v_ref, dq_ref, dk_ref, dv_ref, dc_ref, dr_ref, dq_acc):
        p = pl.program_id(0)
        j = pl.program_id(1)
        n_queries = [jnp.clip(_lane_scalar(rg_ref[0], 2 + hh, j), 1, nt - j) for hh in range(2)]

        @pl.when(j == 0)
        def _():
            dq_acc[...] = jnp.zeros_like(dq_acc)

        lane = lax.broadcasted_iota(jnp.int32, (t, LANES), 1)
        rows = lax.broadcasted_iota(jnp.int32, (t, t), 0)
        cols = lax.broadcasted_iota(jnp.int32, (t, t), 1)
        kt = k_ref[...]
        vt = v_ref[...]

        ks = [kt[:, LANES * hh:LANES * (hh + 1)] for hh in range(2)]

        def tile(qis, carry, heads=(0, 1), diagonal=False):
            dk0, dk1, dv = carry
            offs = [pl.multiple_of(i * t, t) for i in qis]
            units = [(u, hh) for u in range(len(qis)) for hh in heads]
            qts = [q_ref[pl.ds(off, t), :] for off in offs]
            dos = [do_ref[pl.ds(off, t), :] for off in offs]
            lses = [lse_ref[0, :, pl.ds(off, t)] for off in offs]
            dls = [dl_ref[0, :, pl.ds(off, t)] for off in offs]
            qs = [qts[u][:, LANES * hh:LANES * (hh + 1)] for u, hh in units]
            doms = [jnp.where((lane < 64) if hh == 0 else (lane >= 64), dos[u], jnp.zeros_like(dos[u])) for u, hh in units]
            sts = [_dot(ks[hh], qs[n], NT) for n, (u, hh) in enumerate(units)]
            dpts = [_dot(vt, doms[n], NT) for n in range(len(units))]
            if diagonal:
                sts = [jnp.where(cols >= rows, st, NEG) for st in sts]
            pts = [jnp.exp(sts[n] - lses[u][hh:hh + 1, :]) for n, (u, hh) in enumerate(units)]
            dsts = [(pts[n] * (dpts[n] - dls[u][hh:hh + 1, :])).astype(BF16) for n, (u, hh) in enumerate(units)]
            for n, (u, hh) in enumerate(units):
                dv = dv + _dot(pts[n].astype(BF16), doms[n])
                term = _dot(dsts[n], qs[n])
                dk0, dk1 = (dk0 + term, dk1) if hh == 0 else (dk0, dk1 + term)
                dq_acc[hh, pl.ds(offs[u], t), :] += _dot(dsts[n], ks[hh], TN)
            return dk0, dk1, dv

        zero = jnp.zeros((t, LANES), F32)
        carry = tile([j], (zero, zero, zero), diagonal=True)
        n_rest = jnp.minimum(n_queries[0], n_queries[1]) - 1
        carry = lax.fori_loop(0, n_rest // 2, lambda u, cr: tile([j + 1 + 2 * u, j + 2 + 2 * u], cr), carry)
        carry = lax.fori_loop(0, n_rest % 2, lambda u, cr: tile([j + n_rest], cr), carry)
        for hh in range(2):
            carry = lax.fori_loop(j + 1 + n_rest, j + n_queries[hh], lambda i, cr, hh=hh: tile([i], cr, heads=(hh,)), carry)
        dk0, dk1, dv = carry
        e0, e1 = _aug_lane(0), _aug_lane(1)
        dk_ref[...] = jnp.where(lane < 64, dk0, dk1).astype(BF16)
        dv_ref[...] = dv.astype(BF16)
        c0 = jnp.broadcast_to(dk0[:, e0 + 3:e0 + 4], (t, LANES))
        c1 = jnp.broadcast_to(dk1[:, e1 + 3:e1 + 4], (t, LANES))
        dc_ref[0] = jnp.where(lane == 2 * p, -c0, jnp.where(lane == 2 * p + 1, -c1, 0.0))

        @pl.when(j == nt - 1)
        def _():
            lane_s = lax.broadcasted_iota(jnp.int32, (s, LANES), 1)
            a0, a1 = dq_acc[0], dq_acc[1]
            dq_ref[...] = (jnp.where(lane_s < 64, a0, a1) * SCALE).astype(BF16)
            r0 = jnp.broadcast_to(a0[:, e0:e0 + 1], (s, LANES))
            r1 = jnp.broadcast_to(a1[:, e1:e1 + 1], (s, LANES))
            dr_ref[0] = jnp.where(lane_s == 2 * p, r0, jnp.where(lane_s == 2 * p + 1, r1, 0.0))

    return pl.pallas_call(
        body, name="fox_bwd", grid=(4, nt),
        out_shape=[jax.ShapeDtypeStruct((s, 512), BF16), jax.ShapeDtypeStruct((s, 512), BF16),
                   jax.ShapeDtypeStruct((s, 512), BF16), jax.ShapeDtypeStruct((4, s, LANES), F32),
                   jax.ShapeDtypeStruct((4, s, LANES), F32)],
        in_specs=[pl.BlockSpec((1, SUBLANES, LANES), lambda p, j: (p, 0, 0)),
                  pl.BlockSpec((s, 2 * LANES), lambda p, j: (0, p)),
                  pl.BlockSpec((s, LANES), lambda p, j: (0, p)),
                  pl.BlockSpec((1, SUBLANES, s), lambda p, j: (p, 0, 0)),
                  pl.BlockSpec((1, SUBLANES, s), lambda p, j: (p, 0, 0)),
                  pl.BlockSpec((t, 2 * LANES), lambda p, j: (j, p)),
                  pl.BlockSpec((t, LANES), lambda p, j: (j, p))],
        out_specs=[pl.BlockSpec((s, LANES), lambda p, j: (0, p)),
                   pl.BlockSpec((t, LANES), lambda p, j: (j, p)),
                   pl.BlockSpec((t, LANES), lambda p, j: (j, p)),
                   pl.BlockSpec((1, t, LANES), lambda p, j: (p, j, 0)),
                   pl.BlockSpec((1, s, LANES), lambda p, j: (p, 0, 0))],
        scratch_shapes=[pltpu.VMEM((2, s, LANES), F32)],
        compiler_params=_params(("parallel", "arbitrary"), VMEM_LIMIT),
    )(ranges, qa, do, lse, delta, ka, vb)


def _forget_logit_grad(dc_ref, dr_ref, f, b_f, carry):
    tb = f.shape[0]
    lane = lax.broadcasted_iota(jnp.int32, (tb, LANES), 1)
    dc = dc_ref[0] + dr_ref[0]
    for k in range(1, 4):
        dc = dc + (dc_ref[k] + dr_ref[k])
    hi, mid, lo = _split3(dc)
    rows = lax.broadcasted_iota(jnp.int32, (tb, tb), 0)
    cols = lax.broadcasted_iota(jnp.int32, (tb, tb), 1)
    triu = (cols >= rows).astype(BF16)
    dlogf = _dot(triu, hi) + _dot(triu, mid) + _dot(triu, lo) + carry[0:1, :]
    carry[...] = jnp.broadcast_to(dlogf[0:1, :], carry.shape)
    return jnp.where(lane < N_HEADS, dlogf * _sigmoid(-(f + b_f)), 0.0)


def _dh_norm_bwd(d_a, d_q, d_k, d_v, dcum_k, dcum_q, f, bf_pad, d_g, w_t, x, dx2, gnorm, scale1, tm=512):
    s = x.shape[0]
    nt = s // tm

    def body(da_ref, dq_ref, dk_ref, dv_ref, dc_ref, dr_ref, f_ref, bf_ref, dg_ref, w_ref, x_ref, dx2_ref, g_ref, sc_ref,
             gx_ref, vec_ref, df_ref, db_ref, a_sh, a_sc, a_g, carry):
        step = pl.program_id(0)

        @pl.when(step == 0)
        def _():
            a_sh[...] = jnp.zeros_like(a_sh)
            a_sc[...] = jnp.zeros_like(a_sc)
            a_g[...] = jnp.zeros_like(a_g)
            carry[...] = jnp.zeros_like(carry)
            db_ref[...] = jnp.zeros_like(db_ref)

        def fold(v):
            return jnp.sum(v.reshape(tm // SUBLANES, SUBLANES, D_MODEL), axis=0)

        dfb = _forget_logit_grad(dc_ref, dr_ref, f_ref[...], bf_ref[...], carry)
        d_f = dfb.astype(BF16)
        df_ref[...] = d_f
        sub8 = lax.broadcasted_iota(jnp.int32, (SUBLANES, LANES), 0)
        db_ref[...] += jnp.where(sub8 == 0, jnp.sum(dfb, axis=0, keepdims=True), 0.0)
        d_all = jnp.concatenate([da_ref[...], dq_ref[...], dk_ref[...], dv_ref[...], d_f, dg_ref[...]], axis=1)
        dh = _dot(d_all, w_ref[...])
        xv = x_ref[...]
        r = lax.rsqrt(jnp.mean(xv * xv, axis=-1, keepdims=True) + NORM_EPS)
        xn = xv * r
        gn = g_ref[...]
        a_sh[...] += fold(dh)
        a_sc[...] += fold(dh * (xn * gn))
        dn1 = dh * sc_ref[...]
        a_g[...] += fold(dn1 * xn)
        dxn = dn1 * gn
        gx_ref[...] = dx2_ref[...] + r * (dxn - xn * jnp.mean(dxn * xn, axis=-1, keepdims=True))

        @pl.when(step == nt - 1)
        def _():
            sub = lax.broadcasted_iota(jnp.int32, (SUBLANES, D_MODEL), 0)
            v_sh = jnp.sum(a_sh[...], axis=0, keepdims=True)
            v_sc = jnp.sum(a_sc[...], axis=0, keepdims=True)
            v_g = jnp.sum(a_g[...], axis=0, keepdims=True)
            vec_ref[...] = jnp.where(sub == 0, v_sh, jnp.where(sub == 1, v_sc, jnp.where(sub == 2, v_g, 0.0)))

    row = lambda w: pl.BlockSpec((tm, w), lambda i: (nt - 1 - i, 0))
    slabs = pl.BlockSpec((4, tm, LANES), lambda i: (0, nt - 1 - i, 0))
    return pl.pallas_call(
        body, name="dh_norm_bwd", grid=(nt,),
        out_shape=[jax.ShapeDtypeStruct((s, D_MODEL), F32), jax.ShapeDtypeStruct((SUBLANES, D_MODEL), F32),
                   jax.ShapeDtypeStruct((s, LANES), BF16), jax.ShapeDtypeStruct((SUBLANES, LANES), F32)],
        in_specs=[row(W_A), row(512), row(512), row(512), slabs, slabs, row(W_F), _const_spec((1, LANES)), row(W_G),
                  _const_spec((W_INT, D_MODEL)), row(D_MODEL), row(D_MODEL), _const_spec((1, D_MODEL)),
                  _const_spec((1, D_MODEL))],
        out_specs=[row(D_MODEL), pl.BlockSpec((SUBLANES, D_MODEL), lambda i: (0, 0)), row(LANES),
                   pl.BlockSpec((SUBLANES, LANES), lambda i: (0, 0))],
        scratch_shapes=[pltpu.VMEM((SUBLANES, D_MODEL), F32)] * 3 + [pltpu.VMEM((SUBLANES, LANES), F32)],
        compiler_params=_params(("arbitrary",), VMEM_LIMIT),
    )(d_a, d_q, d_k, d_v, dcum_k, dcum_q, f, bf_pad, d_g, w_t, x, dx2, gnorm, scale1)


def _dw_in(h_t, d_a, d_q, d_k, d_v, d_f, d_g, ts=1024, tc=512):
    s = h_t.shape[1]
    ns = s // ts
    w_fg = 512 + W_F + W_G - 512
    rows = 128
    n_slot = (R_END // 4 + LANES - 1) // LANES * LANES
    order = [(0, 0, W_A), (4, 0, 512), (1, 0, 512), (2, 0, 512), (3, 0, 512), (4, 512, N_HEADS), (4, 512 + W_F, W_G - 512)]

    def slot_pieces(k):
        lo, hi, out, col = (R_END // 4) * k, (R_END // 4) * (k + 1), [], 0
        for acc_i, c0, w in order:
            a, b = max(lo, col), min(hi, col + w)
            if a < b:
                out.append((acc_i, c0 + a - col, b - a))
            col += w
        return out

    def body(h_ref, da_ref, dq_ref, dk_ref, dv_ref, df_ref, dg_ref, o_ref, acc_a, acc_q, acc_k, acc_v, acc_fg, stage, sem):
        k = pl.program_id(0)
        accs = (acc_a, acc_q, acc_k, acc_v, acc_fg)

        @pl.when(k == 0)
        def _():
            for acc in accs:
                acc[...] = jnp.zeros_like(acc)

        hv = h_ref[...]

        def add(acc, c_acc, d_ref, c_d, width):
            for c0 in range(0, width, tc):
                w = min(tc, width - c0)
                acc[:, c_acc + c0:c_acc + c0 + w] += _dot(hv, d_ref[:, c_d + c0:c_d + c0 + w])

        add(acc_a, 0, da_ref, 0, W_A)
        add(acc_q, 0, dq_ref, 0, 512)
        add(acc_k, 0, dk_ref, 0, 512)
        add(acc_v, 0, dv_ref, 0, 512)
        add(acc_fg, 0, dg_ref, 0, 512)
        add(acc_fg, 512, df_ref, 0, W_F)
        add(acc_fg, 512 + W_F, dg_ref, 512, W_G - 512)

        @pl.when(k == ns - 1)
        def _():
            pending = [None, None]
            for n, r0 in enumerate(range(0, D_MODEL, rows)):
                buf = n % 2
                if pending[buf] is not None:
                    for cp in pending[buf]:
                        cp.wait()
                rs = slice(r0, r0 + rows)
                copies = []
                for slot in range(4):
                    parts = [accs[acc_i][rs, c0:c0 + w] for acc_i, c0, w in slot_pieces(slot)]
                    parts.append(jnp.zeros((rows, n_slot - R_END // 4), F32))
                    stage[buf, slot] = jnp.concatenate(parts, axis=1)
                    cp = pltpu.make_async_copy(stage.at[buf, slot], o_ref.at[slot, pl.ds(r0, rows), :], sem.at[4 * buf + slot])
                    cp.start()
                    copies.append(cp)
                pending[buf] = copies
            for copies in pending:
                for cp in copies:
                    cp.wait()

    spec = lambda d: pl.BlockSpec((ts, d.shape[1]), lambda k: (k, 0))
    return pl.pallas_call(
        body, name="dw_in", grid=(ns,),
        out_shape=jax.ShapeDtypeStruct((4, D_MODEL, n_slot), F32),
        in_specs=[pl.BlockSpec((D_MODEL, ts), lambda k: (0, k))] + [spec(d) for d in (d_a, d_q, d_k, d_v, d_f, d_g)],
        out_specs=pl.BlockSpec(memory_space=pl.ANY),
        scratch_shapes=[pltpu.VMEM((D_MODEL, W_A), F32)] + [pltpu.VMEM((D_MODEL, 512), F32)] * 3
        + [pltpu.VMEM((D_MODEL, w_fg), F32), pltpu.VMEM((2, 4, rows, n_slot), F32), pltpu.SemaphoreType.DMA((8,))],
        compiler_params=_params(("arbitrary",), VMEM_LIMIT),
    )(h_t, d_a, d_q, d_k, d_v, d_f, d_g)


def _small_grads(packs, c_t, dada_shard):
    def body(p_ref, ct_ref, da_ref, sum_ref, gw_ref):
        acc = p_ref[0]
        for dev in range(1, 8):
            acc = acc + p_ref[dev]
        sum_ref[...] = acc
        gw_ref[...] = jnp.dot(ct_ref[...], da_ref[...], preferred_element_type=F32, precision=lax.Precision.HIGHEST)

    return pl.pallas_call(
        body, name="small_grads",
        out_shape=[jax.ShapeDtypeStruct(packs.shape[1:], F32),
                   jax.ShapeDtypeStruct((c_t.shape[0], dada_shard.shape[1]), F32)],
    )(packs, c_t, dada_shard)


def _adamw_body(w_ref, g_ref, m_ref, v_ref, d_ref, mo_ref, vo_ref):
    c1 = 1.0 / (1.0 - ADAM_B1 ** ADAM_STEP)
    c2 = 1.0 / (1.0 - ADAM_B2 ** ADAM_STEP)
    gv = g_ref[...]
    mn = ADAM_B1 * m_ref[...] + (1.0 - ADAM_B1) * gv
    vn = ADAM_B2 * v_ref[...] + (1.0 - ADAM_B2) * (gv * gv)
    mo_ref[...] = mn
    vo_ref[...] = vn
    d_ref[...] = -ADAM_LR * ((mn * c1) / (jnp.sqrt(vn * c2) + ADAM_EPS) + ADAM_WD * w_ref[...])


def _adamw3(w, g, m, v, name, tb=128):
    spec = pl.BlockSpec((tb, SUBLANES, LANES), lambda i: (i, 0, 0))
    return pl.pallas_call(
        functools.partial(_adamw_body), name=name, grid=(pl.cdiv(w.shape[0], tb),),
        out_shape=[jax.ShapeDtypeStruct(w.shape, F32)] * 3,
        in_specs=[spec] * 4, out_specs=[spec] * 3,
        compiler_params=_params(("parallel",)),
    )(w, g, m, v)


def _adamw_many(items, name):
    n = len(items)

    def body(*refs):
        for i in range(n):
            _adamw_body(*refs[4 * i:4 * i + 4], *refs[4 * n + 3 * i:4 * n + 3 * i + 3])

    return pl.pallas_call(
        body, name=name,
        out_shape=[jax.ShapeDtypeStruct(it[0].shape, F32) for it in items for _ in range(3)],
        compiler_params=_params(vmem=VMEM_LIMIT),
    )(*[arr for it in items for arr in it])


def _rope_inputs(positions):
    inv_freq = 10000.0 ** (-jnp.arange(0, HEAD_DIM, 2, dtype=F32) / HEAD_DIM)
    pos = jnp.broadcast_to(positions.astype(F32)[:, None], (positions.shape[0], LANES))
    return pos, jnp.tile(inv_freq, 4)[None, :]


def _pad_rows(v, rows=SUBLANES):
    return jnp.pad(v, ((0, rows - v.shape[0]), (0, 0)))


def kernel(x, c, positions, w_ada, b_ada, g_norm, w_in, b_f, sinks, w_o_swa, w_o_fox, w_out, g_final, loss_target, m_w_ada, m_b_ada, m_g_norm, m_w_in, m_b_f, m_sinks, m_w_o_swa, m_w_o_fox, m_w_out, m_g_final, v_w_ada, v_b_ada, v_g_norm, v_w_in, v_b_f, v_sinks, v_w_o_swa, v_w_o_fox, v_w_out, v_g_final):
    ix, iy, ic = lax.axis_index("x"), lax.axis_index("y"), lax.axis_index("c")
    chip = 2 * ix + iy
    dev = 2 * chip + ic
    xs, tgt = x[0], loss_target[0]
    s = xs.shape[0]

    b_ada_shard = lax.dynamic_slice(b_ada, (0, chip * 768), (1, 768))
    ada_parts, w_int, w_int_t, g_oa, g_ob, g_out = _gather_inputs(
        _pad_rows(c), w_ada[0], b_ada_shard, jnp.transpose(w_in, (2, 0, 1)).reshape(-1, LANES),
        [w_o_swa[0], w_o_fox[0], w_out[0]], "gather_inputs")
    ada = lax.dynamic_index_in_dim(ada_parts, dev, axis=1, keepdims=False).reshape(1, 3 * D_MODEL)
    shift, scale, gate = ada[:, :D_MODEL], ada[:, D_MODEL:2 * D_MODEL], ada[:, 2 * D_MODEL:]
    scale1 = 1.0 + scale

    wo_a = jnp.transpose(g_oa, (1, 0, 2)).reshape(512, D_MODEL)
    wo_b = jnp.transpose(g_ob, (1, 0, 2)).reshape(512, D_MODEL)
    w_o = g_out.reshape(D_MODEL, D_MODEL)

    pos, freq = _rope_inputs(positions[0])
    bf_pad = jnp.pad(b_f, ((0, 0), (0, LANES - N_HEADS)))
    sink_vec = sinks[0]

    a, vb, f, g, h_t, cos, sin, qa, ka, va, stats = _norm_proj(
        xs, g_norm * scale1, shift, w_int, pos, freq, bf_pad, FOX_TILE)
    att_a, l_swa = _swa_fwd(a, sink_vec)
    ranges = _fox_tile_ranges(stats)
    att_b, lse = _fox_fwd(qa, ka, va, ranges, FOX_TILE)

    dx2, datt_a, datt_b, d_g, delta8, dwo_a, dwo_b, dw_out, vec_mid = _mid(
        att_a, att_b, g, xs, tgt, gate, g_final.reshape(1, D_MODEL), wo_a, wo_b, w_o)
    delta = jnp.pad(delta8.reshape(4, 2, s), ((0, 0), (0, SUBLANES - 2), (0, 0)))
    d_a, dsink = _swa_bwd(a, datt_a, l_swa, sink_vec, cos, sin)
    dq, dk, dv, dcum_k, dcum_q = _fox_bwd(qa, ka, vb, datt_b, lse, delta, ranges, FOX_TILE)
    grad_x, vec_dh, d_f, dbf = _dh_norm_bwd(
        d_a, dq, dk, dv, dcum_k, dcum_q, f, bf_pad, d_g, w_int_t, xs, dx2, g_norm, scale1)
    dw_in_slots = _dw_in(h_t, d_a, dq, dk, dv, d_f, d_g)

    tail = jnp.pad(jnp.concatenate([dbf[0:1, :N_HEADS], dsink[0:1, :N_HEADS]], axis=1), ((0, 0), (0, D_MODEL - 2 * N_HEADS)))
    pack = jnp.concatenate([c, vec_dh[0:2], vec_mid[1:2], vec_dh[2:3], vec_mid[0:1], tail, vec_mid[2:3]], axis=0)

    def slots(w, axis):
        if axis == 1:
            return jnp.transpose(w.reshape(w.shape[0], 4, w.shape[1] // 4), (1, 0, 2))
        return w.reshape(4, w.shape[0] // 4, w.shape[1])

    packs, g_wo_a, g_wo_b, g_w_out, g_w_in = _reduce_scatter(
        [slots(dwo_a, 1), slots(dwo_b, 1), slots(dw_out, 0), dw_in_slots], pack, "reduce_grads")
    g_w_in = g_w_in[:, :w_in.shape[2]]
    dada_all = packs[:, 1:4, :].reshape(8, 3 * D_MODEL)
    dada_shard = lax.dynamic_slice(dada_all, (0, chip * 768), (8, 768))
    sums, g_w_ada = _small_grads(packs, packs[:, 0, :].T, dada_shard)
    g_b_ada = sums[1:4].reshape(1, 3 * D_MODEL)
    g_g_norm = sums[4:5]
    g_g_final = sums[5]
    g_b_f = sums[6:7, :N_HEADS]
    g_sinks = sums[6:7, N_HEADS:2 * N_HEADS]
    loss = sums[7, 0]

    grads = {
        "w_ada": g_w_ada, "b_ada": g_b_ada, "g_norm": g_g_norm, "w_in": g_w_in, "b_f": g_b_f, "sinks": g_sinks,
        "w_o_swa": g_wo_a, "w_o_fox": g_wo_b, "w_out": g_w_out, "g_final": g_g_final,
    }
    params = {
        "w_ada": (w_ada, m_w_ada, v_w_ada), "b_ada": (b_ada, m_b_ada, v_b_ada), "g_norm": (g_norm, m_g_norm, v_g_norm),
        "w_in": (w_in, m_w_in, v_w_in), "b_f": (b_f, m_b_f, v_b_f), "sinks": (sinks, m_sinks, v_sinks),
        "w_o_swa": (w_o_swa, m_w_o_swa, v_w_o_swa), "w_o_fox": (w_o_fox, m_w_o_fox, v_w_o_fox),
        "w_out": (w_out, m_w_out, v_w_out), "g_final": (g_final, m_g_final, v_g_final),
    }
    n_col = w_in.shape[2]

    def as_stored(t):
        return jnp.transpose(t, (2, 0, 1)).reshape(n_col, SUBLANES, LANES)

    def from_stored(t):
        return jnp.transpose(t, (1, 2, 0)).reshape(1, D_MODEL, n_col)

    names = list(grads)
    others = [nm for nm in names if nm != "w_in"]

    def as_2d(t):
        return t.reshape((t.shape[-2], t.shape[-1]) if t.ndim >= 2 else (1, t.shape[0]))

    flat = _adamw_many([tuple(as_2d(t) for t in (params[nm][0], grads[nm], params[nm][1], params[nm][2])) for nm in others],
                       "adamw_small")
    results = {}
    for i, nm in enumerate(others):
        shape = params[nm][0].shape
        results[nm] = [t.reshape(shape) for t in (grads[nm], *flat[3 * i:3 * i + 3])]
    w, m, v = params["w_in"]
    g_st = as_stored(grads["w_in"][None])
    d_, m_, v_ = _adamw3(as_stored(w), g_st, as_stored(m), as_stored(v), "adamw_w_in")
    results["w_in"] = [from_stored(t) for t in (g_st, d_, m_, v_)]
    return (loss, grad_x[None], *[results[nm][0] for nm in names], *[results[nm][1] for nm in names],
            *[results[nm][2] for nm in names], *[results[nm][3] for nm in names])
```
